```python
import math
import jax, jax.numpy as jnp
from jax import lax
import numpy as np

D_MODEL = 1024
BATCH = 16
SEQ = 2048
DEPTH = 1

D_RNN = 1024
N_RNN_BLOCKS = 8
RNN_BLOCK = D_RNN // N_RNN_BLOCKS
CONV_WIDTH = 4
LRU_C = 8.0
HEAD_DIM = 64
N_Q_HEADS = 16
N_KV_HEADS = 4
Q_PER_KV = N_Q_HEADS // N_KV_HEADS
D_ATTN = N_Q_HEADS * HEAD_DIM
D_KV = N_KV_HEADS * HEAD_DIM
WINDOW = 128
BLOCK = 128
ROPE_DIM = HEAD_DIM // 4
ROPE_THETA = 500000.0
NORM_EPS = 1e-6

OFF_RNN_X = 0
OFF_RNN_G = OFF_RNN_X + D_RNN
OFF_Q = OFF_RNN_G + D_RNN
OFF_K = OFF_Q + D_ATTN
OFF_V = OFF_K + D_KV
OFF_ATTN_G = OFF_V + D_KV
OFF_MERGE_R = OFF_ATTN_G + D_ATTN
OFF_MERGE_A = OFF_MERGE_R + D_MODEL
D_IN = OFF_MERGE_A + D_MODEL

kernel_name = "hybrid_rglru_swa_sink_gated_merge"


def rms_norm(x, g):
    xf = x.astype(jnp.float32)
    y = xf * lax.rsqrt(jnp.mean(xf * xf, axis=-1, keepdims=True) + NORM_EPS)
    return (y * g.astype(jnp.float32)).astype(x.dtype)


def causal_depthwise_conv(u, w, b):
    C = u.shape[-1]
    y = lax.conv_general_dilated(
        u, w.astype(u.dtype)[:, None, :], window_strides=(1,),
        padding=[(CONV_WIDTH - 1, 0)], dimension_numbers=("NWC", "WIO", "NWC"),
        feature_group_count=C)
    return y + b.astype(u.dtype)


def block_diag_linear(u, w, b):
    B, S, _ = u.shape
    ub = u.reshape(B, S, N_RNN_BLOCKS, RNN_BLOCK)
    y = jnp.einsum("bsnc,ncd->bsnd", ub, w.astype(u.dtype))
    return y.reshape(B, S, D_RNN) + b.astype(u.dtype)


def rg_lru(u, w_a, b_a, w_x, b_x, lam):
    r = jax.nn.sigmoid(block_diag_linear(u, w_a, b_a).astype(jnp.float32))
    i = jax.nn.sigmoid(block_diag_linear(u, w_x, b_x).astype(jnp.float32))
    log_a = -LRU_C * r * jax.nn.softplus(-lam.astype(jnp.float32))
    a = jnp.exp(log_a)
    b = jnp.sqrt(-jnp.expm1(2.0 * log_a)) * (i * u.astype(jnp.float32))

    def combine(c1, c2):
        a1, b1 = c1
        a2, b2 = c2
        return a1 * a2, a2 * b1 + b2

    _, h = lax.associative_scan(combine, (a, b), axis=1)
    return h.astype(u.dtype)


def rope_tables(seq_len):
    pos = jnp.arange(seq_len, dtype=jnp.float32)
    inv_freq = ROPE_THETA ** (-jnp.arange(0, ROPE_DIM, 2, dtype=jnp.float32) / ROPE_DIM)
    ang = pos[:, None] * inv_freq[None, :]
    return jnp.cos(ang)[:, None, :], jnp.sin(ang)[:, None, :]


def apply_partial_rope(t, cos, sin):
    half = ROPE_DIM // 2
    c = cos.astype(t.dtype)
    s = sin.astype(t.dtype)
    t1 = t[..., :half]
    t2 = t[..., half:ROPE_DIM]
    rot = jnp.concatenate([t1 * c - t2 * s, t2 * c + t1 * s], axis=-1)
    return jnp.concatenate([rot, t[..., ROPE_DIM:]], axis=-1)


def sliding_window_attention_with_sinks(q, k, v, sinks):
    B, S, _, _ = q.shape
    nb = S // BLOCK
    qb = q.reshape(B, nb, BLOCK, N_KV_HEADS, Q_PER_KV, HEAD_DIM)
    kb = k.reshape(B, nb, BLOCK, N_KV_HEADS, HEAD_DIM)
    vb = v.reshape(B, nb, BLOCK, N_KV_HEADS, HEAD_DIM)
    zeros = jnp.zeros_like(kb[:, :1])
    kw = jnp.concatenate([jnp.concatenate([zeros, kb[:, :-1]], axis=1), kb], axis=2)
    vw = jnp.concatenate([jnp.concatenate([zeros, vb[:, :-1]], axis=1), vb], axis=2)

    scale = 1.0 / math.sqrt(HEAD_DIM)
    scores = jnp.einsum("bnqkgd,bnjkd->bnkgqj", qb, kw).astype(jnp.float32) * scale

    blk = jnp.arange(nb)[:, None, None]
    q_abs = blk * BLOCK + jnp.arange(BLOCK)[None, :, None]
    k_abs = (blk - 1) * BLOCK + jnp.arange(2 * BLOCK)[None, None, :]
    valid = (k_abs <= q_abs) & (q_abs - k_abs < WINDOW) & (k_abs >= 0)
    scores = jnp.where(valid[None, :, None, None, :, :], scores, -jnp.inf)

    sink = sinks.astype(jnp.float32).reshape(N_KV_HEADS, Q_PER_KV)[None, None, :, :, None]
    m = jnp.maximum(jnp.max(scores, axis=-1), sink)
    p = jnp.exp(scores - m[..., None])
    denom = jnp.sum(p, axis=-1) + jnp.exp(sink - m)
    probs = (p / denom[..., None]).astype(v.dtype)
    out = jnp.einsum("bnkgqj,bnjkd->bnqkgd", probs, vw)
    return out.reshape(B, S, D_ATTN)


def _fwd_setup_inputs(seed: int = 0) -> dict:
    key = jax.random.key(seed)
    ks = jax.random.split(key, 17)
    f32 = jnp.float32
    x = jax.random.normal(ks[0], (BATCH, SEQ, D_MODEL), f32)
    norm_g = 1.0 + 0.02 * jax.random.normal(ks[1], (DEPTH, D_MODEL), f32)
    w_in = jax.random.normal(ks[2], (DEPTH, D_MODEL, D_IN), f32) * D_MODEL ** -0.5
    conv_w = jax.random.normal(ks[3], (DEPTH, CONV_WIDTH, D_RNN), f32) * CONV_WIDTH ** -0.5
    conv_b = 0.02 * jax.random.normal(ks[4], (DEPTH, D_RNN), f32)
    lru_w_a = jax.random.normal(ks[5], (DEPTH, N_RNN_BLOCKS, RNN_BLOCK, RNN_BLOCK), f32) * RNN_BLOCK ** -0.5
    lru_b_a = 0.02 * jax.random.normal(ks[6], (DEPTH, D_RNN), f32)
    lru_w_x = jax.random.normal(ks[7], (DEPTH, N_RNN_BLOCKS, RNN_BLOCK, RNN_BLOCK), f32) * RNN_BLOCK ** -0.5
    lru_b_x = 0.02 * jax.random.normal(ks[8], (DEPTH, D_RNN), f32)
    a0 = jax.random.uniform(ks[9], (DEPTH, D_RNN), f32, 0.9, 0.999)
    lru_lambda = jnp.log(a0) - jnp.log1p(-a0)
    attn_sinks = 0.5 * jax.random.normal(ks[10], (DEPTH, N_Q_HEADS), f32)
    w_rnn_out = jax.random.normal(ks[11], (DEPTH, D_RNN, D_MODEL), f32) * D_RNN ** -0.5
    w_attn_out = jax.random.normal(ks[12], (DEPTH, D_ATTN, D_MODEL), f32) * D_ATTN ** -0.5
    w_o = jax.random.normal(ks[13], (DEPTH, D_MODEL, D_MODEL), f32) * D_MODEL ** -0.5
    final_norm_g = 1.0 + 0.02 * jax.random.normal(ks[14], (D_MODEL,), f32)
    return {"x": x, "norm_g": norm_g, "w_in": w_in, "conv_w": conv_w, "conv_b": conv_b,
            "lru_w_a": lru_w_a, "lru_b_a": lru_b_a, "lru_w_x": lru_w_x, "lru_b_x": lru_b_x,
            "lru_lambda": lru_lambda, "attn_sinks": attn_sinks, "w_rnn_out": w_rnn_out,
            "w_attn_out": w_attn_out, "w_o": w_o, "final_norm_g": final_norm_g}


def _fwd_reference(x, norm_g, w_in, conv_w, conv_b, lru_w_a, lru_b_a, lru_w_x, lru_b_x,
              lru_lambda, attn_sinks, w_rnn_out, w_attn_out, w_o, final_norm_g):
    B, S, _ = x.shape
    cos, sin = rope_tables(S)
    for l in range(DEPTH):
        h = rms_norm(x, norm_g[l])
        proj = jnp.einsum("bsd,de->bse", h, w_in[l])
        u = proj[..., OFF_RNN_X:OFF_RNN_G]
        g_rnn = proj[..., OFF_RNN_G:OFF_Q]
        q = proj[..., OFF_Q:OFF_K].reshape(B, S, N_Q_HEADS, HEAD_DIM)
        k = proj[..., OFF_K:OFF_V].reshape(B, S, N_KV_HEADS, HEAD_DIM)
        v = proj[..., OFF_V:OFF_ATTN_G].reshape(B, S, N_KV_HEADS, HEAD_DIM)
        g_attn = proj[..., OFF_ATTN_G:OFF_MERGE_R]
        m_rnn = proj[..., OFF_MERGE_R:OFF_MERGE_A]
        m_attn = proj[..., OFF_MERGE_A:D_IN]

        u = causal_depthwise_conv(u, conv_w[l], conv_b[l])
        y_rnn = rg_lru(u, lru_w_a[l], lru_b_a[l], lru_w_x[l], lru_b_x[l], lru_lambda[l]) * jax.nn.silu(g_rnn)

        q = apply_partial_rope(q, cos, sin)
        k = apply_partial_rope(k, cos, sin)
        y_attn = sliding_window_attention_with_sinks(q, k, v, attn_sinks[l]) * jax.nn.silu(g_attn)

        merged = (jax.nn.sigmoid(m_rnn) * jnp.einsum("bsr,rd->bsd", y_rnn, w_rnn_out[l])
                  + jax.nn.sigmoid(m_attn) * jnp.einsum("bsa,ad->bsd", y_attn, w_attn_out[l]))
        x = x + jnp.einsum("bsd,de->bse", merged, w_o[l])
    return rms_norm(x, final_norm_g)


import jax as _jax
import jax.numpy as _jnp

TWIN_FORMAT = 'train_step'
FWD_PARAMS = ['x', 'norm_g', 'w_in', 'conv_w', 'conv_b', 'lru_w_a', 'lru_b_a', 'lru_w_x', 'lru_b_x', 'lru_lambda', 'attn_sinks', 'w_rnn_out', 'w_attn_out', 'w_o', 'final_norm_g']
TWIN_WEIGHTS = ['norm_g', 'w_in', 'conv_w', 'conv_b', 'lru_w_a', 'lru_b_a', 'lru_w_x', 'lru_b_x', 'lru_lambda', 'attn_sinks', 'w_rnn_out', 'w_attn_out', 'w_o', 'final_norm_g']
TWIN_DIFF_INPUT = 'x'
TWIN_INPUTS = ['x', 'norm_g', 'w_in', 'conv_w', 'conv_b', 'lru_w_a', 'lru_b_a', 'lru_w_x', 'lru_b_x', 'lru_lambda', 'attn_sinks', 'w_rnn_out', 'w_attn_out', 'w_o', 'final_norm_g', 'loss_target', 'm_norm_g', 'm_w_in', 'm_conv_w', 'm_conv_b', 'm_lru_w_a', 'm_lru_b_a', 'm_lru_w_x', 'm_lru_b_x', 'm_lru_lambda', 'm_attn_sinks', 'm_w_rnn_out', 'm_w_attn_out', 'm_w_o', 'm_final_norm_g', 'v_norm_g', 'v_w_in', 'v_conv_w', 'v_conv_b', 'v_lru_w_a', 'v_lru_b_a', 'v_lru_w_x', 'v_lru_b_x', 'v_lru_lambda', 'v_attn_sinks', 'v_w_rnn_out', 'v_w_attn_out', 'v_w_o', 'v_final_norm_g']
TWIN_OUTPUTS = ['loss', 'grad_x', 'grad_norm_g', 'grad_w_in', 'grad_conv_w', 'grad_conv_b', 'grad_lru_w_a', 'grad_lru_b_a', 'grad_lru_w_x', 'grad_lru_b_x', 'grad_lru_lambda', 'grad_attn_sinks', 'grad_w_rnn_out', 'grad_w_attn_out', 'grad_w_o', 'grad_final_norm_g', 'delta_norm_g', 'delta_w_in', 'delta_conv_w', 'delta_conv_b', 'delta_lru_w_a', 'delta_lru_b_a', 'delta_lru_w_x', 'delta_lru_b_x', 'delta_lru_lambda', 'delta_attn_sinks', 'delta_w_rnn_out', 'delta_w_attn_out', 'delta_w_o', 'delta_final_norm_g', 'new_m_norm_g', 'new_m_w_in', 'new_m_conv_w', 'new_m_conv_b', 'new_m_lru_w_a', 'new_m_lru_b_a', 'new_m_lru_w_x', 'new_m_lru_b_x', 'new_m_lru_lambda', 'new_m_attn_sinks', 'new_m_w_rnn_out', 'new_m_w_attn_out', 'new_m_w_o', 'new_m_final_norm_g', 'new_v_norm_g', 'new_v_w_in', 'new_v_conv_w', 'new_v_conv_b', 'new_v_lru_w_a', 'new_v_lru_b_a', 'new_v_lru_w_x', 'new_v_lru_b_x', 'new_v_lru_lambda', 'new_v_attn_sinks', 'new_v_w_rnn_out', 'new_v_w_attn_out', 'new_v_w_o', 'new_v_final_norm_g']
TWIN_LEAF_KINDS = {'loss': 'loss', 'grad_x': 'grad_x', 'grad_norm_g': 'grad_w', 'grad_w_in': 'grad_w', 'grad_conv_w': 'grad_w', 'grad_conv_b': 'grad_w', 'grad_lru_w_a': 'grad_w', 'grad_lru_b_a': 'grad_w', 'grad_lru_w_x': 'grad_w', 'grad_lru_b_x': 'grad_w', 'grad_lru_lambda': 'grad_w', 'grad_attn_sinks': 'grad_w', 'grad_w_rnn_out': 'grad_w', 'grad_w_attn_out': 'grad_w', 'grad_w_o': 'grad_w', 'grad_final_norm_g': 'grad_w', 'delta_norm_g': 'delta_w', 'delta_w_in': 'delta_w', 'delta_conv_w': 'delta_w', 'delta_conv_b': 'delta_w', 'delta_lru_w_a': 'delta_w', 'delta_lru_b_a': 'delta_w', 'delta_lru_w_x': 'delta_w', 'delta_lru_b_x': 'delta_w', 'delta_lru_lambda': 'delta_w', 'delta_attn_sinks': 'delta_w', 'delta_w_rnn_out': 'delta_w', 'delta_w_attn_out': 'delta_w', 'delta_w_o': 'delta_w', 'delta_final_norm_g': 'delta_w', 'new_m_norm_g': 'new_m', 'new_m_w_in': 'new_m', 'new_m_conv_w': 'new_m', 'new_m_conv_b': 'new_m', 'new_m_lru_w_a': 'new_m', 'new_m_lru_b_a': 'new_m', 'new_m_lru_w_x': 'new_m', 'new_m_lru_b_x': 'new_m', 'new_m_lru_lambda': 'new_m', 'new_m_attn_sinks': 'new_m', 'new_m_w_rnn_out': 'new_m', 'new_m_w_attn_out': 'new_m', 'new_m_w_o': 'new_m', 'new_m_final_norm_g': 'new_m', 'new_v_norm_g': 'new_v', 'new_v_w_in': 'new_v', 'new_v_conv_w': 'new_v', 'new_v_conv_b': 'new_v', 'new_v_lru_w_a': 'new_v', 'new_v_lru_b_a': 'new_v', 'new_v_lru_w_x': 'new_v', 'new_v_lru_b_x': 'new_v', 'new_v_lru_lambda': 'new_v', 'new_v_attn_sinks': 'new_v', 'new_v_w_rnn_out': 'new_v', 'new_v_w_attn_out': 'new_v', 'new_v_w_o': 'new_v', 'new_v_final_norm_g': 'new_v'}


def _forward(args):
    return _fwd_reference(*[args[k] for k in FWD_PARAMS])


def _output_shape():
    out = _jax.eval_shape(lambda: _forward(_fwd_setup_inputs(0)))
    return out.shape, out.dtype

N_MICROBATCH = 1
ADAM_LR = 0.001
ADAM_B1 = 0.9
ADAM_B2 = 0.999
ADAM_EPS = 1e-08
ADAM_WD = 0.01
ADAM_STEP = 10
PER_EXAMPLE_BATCH_AXIS = {'x': 0, 'loss_target': 0}
SHARED_INPUTS = []
_WEIGHT_DTYPES = {'norm_g': _jnp.float32, 'w_in': _jnp.float32, 'conv_w': _jnp.float32, 'conv_b': _jnp.float32, 'lru_w_a': _jnp.float32, 'lru_b_a': _jnp.float32, 'lru_w_x': _jnp.float32, 'lru_b_x': _jnp.float32, 'lru_lambda': _jnp.float32, 'attn_sinks': _jnp.float32, 'w_rnn_out': _jnp.float32, 'w_attn_out': _jnp.float32, 'w_o': _jnp.float32, 'final_norm_g': _jnp.float32}
MOMENT_SCALE = {'norm_g': 5.033427e-02, 'w_in': 2.017199e-02, 'conv_w': 3.396954e-02, 'conv_b': 1.503193e-01, 'lru_w_a': 8.187714e-03, 'lru_b_a': 8.517714e-03, 'lru_w_x': 1.395266e-02, 'lru_b_x': 1.288594e-02, 'lru_lambda': 1.744180e-02, 'attn_sinks': 7.870628e-03, 'w_rnn_out': 3.124209e-02, 'w_attn_out': 1.160199e-02, 'w_o': 3.203336e-02, 'final_norm_g': 3.198160e+01}


def _to_microbatches(a, axis):
    t = _jnp.moveaxis(a, axis, 0)
    t = t.reshape((N_MICROBATCH, t.shape[0] // N_MICROBATCH) + t.shape[1:])
    return _jnp.moveaxis(t, 1, axis + 1)


def setup_inputs(seed: int = 0) -> dict:
    inp = _fwd_setup_inputs(seed)
    key = _jax.random.fold_in(_jax.random.key(seed), 7919)
    shape, _ = _output_shape()
    out = dict(inp)
    out["loss_target"] = _jax.random.normal(_jax.random.fold_in(key, 0), shape, _jnp.float32)
    for i, name in enumerate(TWIN_WEIGHTS):
        w = inp[name].astype(_jnp.float32)
        if MOMENT_SCALE is None:
            s = _jnp.sqrt(_jnp.mean(_jnp.square(w)) + 1e-30)
        else:
            s = MOMENT_SCALE[name]
        km, kv = _jax.random.split(_jax.random.fold_in(key, i + 1))
        out[name] = w
        out["m_" + name] = s * _jax.random.normal(km, w.shape, _jnp.float32)
        out["v_" + name] = (s * s) * _jax.random.uniform(kv, w.shape, _jnp.float32, 0.5, 1.5)
    if N_MICROBATCH > 1:
        for name, axis in PER_EXAMPLE_BATCH_AXIS.items():
            out[name] = _to_microbatches(out[name], axis)
    return {'x': out['x'], 'norm_g': out['norm_g'], 'w_in': out['w_in'], 'conv_w': out['conv_w'], 'conv_b': out['conv_b'], 'lru_w_a': out['lru_w_a'], 'lru_b_a': out['lru_b_a'], 'lru_w_x': out['lru_w_x'], 'lru_b_x': out['lru_b_x'], 'lru_lambda': out['lru_lambda'], 'attn_sinks': out['attn_sinks'], 'w_rnn_out': out['w_rnn_out'], 'w_attn_out': out['w_attn_out'], 'w_o': out['w_o'], 'final_norm_g': out['final_norm_g'], 'loss_target': out['loss_target'], 'm_norm_g': out['m_norm_g'], 'm_w_in': out['m_w_in'], 'm_conv_w': out['m_conv_w'], 'm_conv_b': out['m_conv_b'], 'm_lru_w_a': out['m_lru_w_a'], 'm_lru_b_a': out['m_lru_b_a'], 'm_lru_w_x': out['m_lru_w_x'], 'm_lru_b_x': out['m_lru_b_x'], 'm_lru_lambda': out['m_lru_lambda'], 'm_attn_sinks': out['m_attn_sinks'], 'm_w_rnn_out': out['m_w_rnn_out'], 'm_w_attn_out': out['m_w_attn_out'], 'm_w_o': out['m_w_o'], 'm_final_norm_g': out['m_final_norm_g'], 'v_norm_g': out['v_norm_g'], 'v_w_in': out['v_w_in'], 'v_conv_w': out['v_conv_w'], 'v_conv_b': out['v_conv_b'], 'v_lru_w_a': out['v_lru_w_a'], 'v_lru_b_a': out['v_lru_b_a'], 'v_lru_w_x': out['v_lru_w_x'], 'v_lru_b_x': out['v_lru_b_x'], 'v_lru_lambda': out['v_lru_lambda'], 'v_attn_sinks': out['v_attn_sinks'], 'v_w_rnn_out': out['v_w_rnn_out'], 'v_w_attn_out': out['v_w_attn_out'], 'v_w_o': out['v_w_o'], 'v_final_norm_g': out['v_final_norm_g']}


def _loss(weights, diff, rest, loss_target):
    with _jax.named_scope("forward"):
        args = {**rest, TWIN_DIFF_INPUT: diff, **{k: w.astype(_WEIGHT_DTYPES[k]) for k, w in weights.items()}}
        y = _forward(args)
    with _jax.named_scope("loss_head"):
        err = _jnp.square(y.astype(_jnp.float32) - loss_target)
        return 0.5 * _jnp.sum(_jnp.mean(err, axis=-1)) if err.ndim else 0.5 * err


def _adamw(w, g, m, v):
    m = ADAM_B1 * m + (1.0 - ADAM_B1) * g
    v = ADAM_B2 * v + (1.0 - ADAM_B2) * _jnp.square(g)
    m_hat = m / (1.0 - ADAM_B1 ** ADAM_STEP)
    v_hat = v / (1.0 - ADAM_B2 ** ADAM_STEP)
    delta = -ADAM_LR * (m_hat / (_jnp.sqrt(v_hat) + ADAM_EPS) + ADAM_WD * w)
    return delta, m, v


def reference(x, norm_g, w_in, conv_w, conv_b, lru_w_a, lru_b_a, lru_w_x, lru_b_x, lru_lambda, attn_sinks, w_rnn_out, w_attn_out, w_o, final_norm_g, loss_target, m_norm_g, m_w_in, m_conv_w, m_conv_b, m_lru_w_a, m_lru_b_a, m_lru_w_x, m_lru_b_x, m_lru_lambda, m_attn_sinks, m_w_rnn_out, m_w_attn_out, m_w_o, m_final_norm_g, v_norm_g, v_w_in, v_conv_w, v_conv_b, v_lru_w_a, v_lru_b_a, v_lru_w_x, v_lru_b_x, v_lru_lambda, v_attn_sinks, v_w_rnn_out, v_w_attn_out, v_w_o, v_final_norm_g):
    given = dict(x=x, norm_g=norm_g, w_in=w_in, conv_w=conv_w, conv_b=conv_b, lru_w_a=lru_w_a, lru_b_a=lru_b_a, lru_w_x=lru_w_x, lru_b_x=lru_b_x, lru_lambda=lru_lambda, attn_sinks=attn_sinks, w_rnn_out=w_rnn_out, w_attn_out=w_attn_out, w_o=w_o, final_norm_g=final_norm_g, loss_target=loss_target, m_norm_g=m_norm_g, m_w_in=m_w_in, m_conv_w=m_conv_w, m_conv_b=m_conv_b, m_lru_w_a=m_lru_w_a, m_lru_b_a=m_lru_b_a, m_lru_w_x=m_lru_w_x, m_lru_b_x=m_lru_b_x, m_lru_lambda=m_lru_lambda, m_attn_sinks=m_attn_sinks, m_w_rnn_out=m_w_rnn_out, m_w_attn_out=m_w_attn_out, m_w_o=m_w_o, m_final_norm_g=m_final_norm_g, v_norm_g=v_norm_g, v_w_in=v_w_in, v_conv_w=v_conv_w, v_conv_b=v_conv_b, v_lru_w_a=v_lru_w_a, v_lru_b_a=v_lru_b_a, v_lru_w_x=v_lru_w_x, v_lru_b_x=v_lru_b_x, v_lru_lambda=v_lru_lambda, v_attn_sinks=v_attn_sinks, v_w_rnn_out=v_w_rnn_out, v_w_attn_out=v_w_attn_out, v_w_o=v_w_o, v_final_norm_g=v_final_norm_g)
    weights = {n: given[n] for n in TWIN_WEIGHTS}
    shared = {n: given[n] for n in SHARED_INPUTS}
    per_example = {n: given[n] for n in ['x']}
    grad_fn = _jax.value_and_grad(_loss, argnums=(0, 1))

    def one_microbatch(ex, loss_target):
        ex = dict(ex)
        diff = ex.pop(TWIN_DIFF_INPUT)
        return grad_fn(weights, diff, {**shared, **ex}, loss_target)

    if N_MICROBATCH == 1:
        loss, (grad_w, grad_x) = one_microbatch(per_example, given["loss_target"])
    else:
        def body(carry, xs):
            loss_sum, grad_sum = carry
            l_k, (gw_k, gx_k) = one_microbatch(xs[0], xs[1])
            with _jax.named_scope("update"):
                return (loss_sum + l_k, _jax.tree.map(_jnp.add, grad_sum, gw_k)), gx_k

        init = (_jnp.zeros((), _jnp.float32), _jax.tree.map(_jnp.zeros_like, weights))
        (loss, grad_w), grad_x = _jax.lax.scan(body, init, (per_example, given["loss_target"]))
    with _jax.named_scope("update"):
        delta_w, new_m, new_v = {}, {}, {}
        for n in TWIN_WEIGHTS:
            delta_w[n], new_m[n], new_v[n] = _adamw(weights[n], grad_w[n], given["m_" + n], given["v_" + n])
    return (loss, grad_x, *[grad_w[n] for n in TWIN_WEIGHTS], *[delta_w[n] for n in TWIN_WEIGHTS],
            *[new_m[n] for n in TWIN_WEIGHTS], *[new_v[n] for n in TWIN_WEIGHTS])
```

```python
import functools
import math

import jax
import jax.numpy as jnp
from jax import lax
from jax.experimental import pallas as pl
from jax.experimental.pallas import tpu as pltpu

F32 = jnp.float32
BF16 = jnp.bfloat16
MESH = pl.DeviceIdType.MESH

D_MODEL = 1024
D_RNN = 1024
N_RNN_BLOCKS = 8
RNN_BLOCK = D_RNN // N_RNN_BLOCKS
CONV_WIDTH = 4
LRU_C = 8.0
HEAD_DIM = 64
N_Q_HEADS = 16
N_KV_HEADS = 4
D_ATTN = N_Q_HEADS * HEAD_DIM
D_KV = N_KV_HEADS * HEAD_DIM
WINDOW = 128
ROPE_DIM = HEAD_DIM // 4
ROPE_THETA = 500000.0
NORM_EPS = 1e-6
OFF_RNN_X = 0
OFF_RNN_G = OFF_RNN_X + D_RNN
OFF_Q = OFF_RNN_G + D_RNN
OFF_K = OFF_Q + D_ATTN
OFF_V = OFF_K + D_KV
OFF_ATTN_G = OFF_V + D_KV
OFF_MERGE_R = OFF_ATTN_G + D_ATTN
OFF_MERGE_A = OFF_MERGE_R + D_MODEL
D_IN = OFF_MERGE_A + D_MODEL

ADAM_LR = 0.001
ADAM_B1 = 0.9
ADAM_B2 = 0.999
ADAM_EPS = 1e-08
ADAM_WD = 0.01
ADAM_STEP = 10

N_CHIPS = 4
W_BLK = D_IN // N_CHIPS
ROW_BLK = D_MODEL // N_CHIPS
LANES = 128
ATT_BLK = 128
VMEM_LIMIT = 56 * 1024 * 1024
NEG_BIG = -1e30
ATTN_SCALE = 1.0 / math.sqrt(HEAD_DIM)

PK_WA = 0
PK_WX = PK_WA + N_RNN_BLOCKS * RNN_BLOCK
PK_VEC = PK_WX + N_RNN_BLOCKS * RNN_BLOCK
PK_SINK = PK_VEC + 6 * 8
PK_CONV = PK_SINK + 8
PK_ROWS = PK_CONV + 32 + 8
PK_HALF = PK_ROWS // 2


def _params(**kw):
    return pltpu.CompilerParams(vmem_limit_bytes=VMEM_LIMIT, **kw)


def _sigmoid(z):
    return 1.0 / (1.0 + jnp.exp(-z))


def _dot(a, b):
    return jnp.dot(a, b, preferred_element_type=F32)


def _dot_nt(a, b):
    return lax.dot_general(a, b, (((1,), (1,)), ((), ())), preferred_element_type=F32)


def _dot_tn(a, b):
    return lax.dot_general(a, b, (((0,), (0,)), ((), ())), preferred_element_type=F32)


def _cast_bf16(w, name):
    r, c = w.shape
    tr = min(r, 256)

    def body(w_ref, o_ref):
        o_ref[...] = w_ref[...].astype(BF16)

    return pl.pallas_call(
        body, name=name, grid=(r // tr,), out_shape=jax.ShapeDtypeStruct((r, c), BF16),
        in_specs=[pl.BlockSpec((tr, c), lambda i: (i, 0))], out_specs=pl.BlockSpec((tr, c), lambda i: (i, 0)),
        compiler_params=_params())(w)


def _rmsnorm_fwd(x, g):
    t, d = x.shape
    tm = min(t, 512)

    def body(x_ref, g_ref, o_ref):
        xv = x_ref[...]
        r = lax.rsqrt(jnp.mean(xv * xv, axis=-1, keepdims=True) + NORM_EPS)
        o_ref[...] = (xv * r * g_ref[...]).astype(BF16)

    return pl.pallas_call(
        body, name="rmsnorm_fwd", grid=(t // tm,), out_shape=jax.ShapeDtypeStruct((t, d), BF16),
        in_specs=[pl.BlockSpec((tm, d), lambda i: (i, 0)), pl.BlockSpec((1, d), lambda i: (0, 0))],
        out_specs=pl.BlockSpec((tm, d), lambda i: (i, 0)), compiler_params=_params())(x, g)


def _rmsnorm_bwd(x, dh, dx2, g):
    t, d = x.shape
    tm = min(t, 512)

    def body(x_ref, dh_ref, dx2_ref, g_ref, gx_ref, dg_ref):
        i = pl.program_id(0)
        xv = x_ref[...]
        dhv = dh_ref[...]
        r = lax.rsqrt(jnp.mean(xv * xv, axis=-1, keepdims=True) + NORM_EPS)
        nrm = xv * r
        dn = dhv * g_ref[...]
        gx_ref[...] = dx2_ref[...] + r * (dn - nrm * jnp.mean(dn * nrm, axis=-1, keepdims=True))

        @pl.when(i == 0)
        def _():
            dg_ref[...] = jnp.zeros_like(dg_ref)

        dg_ref[...] += jnp.sum(dhv * nrm, axis=0, keepdims=True)

    return pl.pallas_call(
        body, name="rmsnorm_bwd", grid=(t // tm,),
        out_shape=(jax.ShapeDtypeStruct((t, d), F32), jax.ShapeDtypeStruct((1, d), F32)),
        in_specs=[pl.BlockSpec((tm, d), lambda i: (i, 0)), pl.BlockSpec((tm, d), lambda i: (i, 0)),
                  pl.BlockSpec((tm, d), lambda i: (i, 0)), pl.BlockSpec((1, d), lambda i: (0, 0))],
        out_specs=(pl.BlockSpec((tm, d), lambda i: (i, 0)), pl.BlockSpec((1, d), lambda i: (0, 0))),
        compiler_params=_params())(x, dh, dx2, g)


def _adamw(w, g, m, v, name):
    r, c = w.shape
    tr = _row_tile(r, c * 4, 1024 * 1024)
    c1 = 1.0 - ADAM_B1 ** ADAM_STEP
    c2 = 1.0 - ADAM_B2 ** ADAM_STEP

    def body(w_ref, g_ref, m_ref, v_ref, d_ref, nm_ref, nv_ref):
        gv = g_ref[...]
        nm = ADAM_B1 * m_ref[...] + (1.0 - ADAM_B1) * gv
        nv = ADAM_B2 * v_ref[...] + (1.0 - ADAM_B2) * (gv * gv)
        m_hat = nm / c1
        v_hat = nv / c2
        d_ref[...] = -ADAM_LR * (m_hat / (jnp.sqrt(v_hat) + ADAM_EPS) + ADAM_WD * w_ref[...])
        nm_ref[...] = nm
        nv_ref[...] = nv

    spec = pl.BlockSpec((tr, c), lambda i: (i, 0))
    sds = jax.ShapeDtypeStruct((r, c), F32)
    return pl.pallas_call(
        body, name=name, grid=(r // tr,), out_shape=(sds, sds, sds),
        in_specs=[spec, spec, spec, spec], out_specs=(spec, spec, spec), compiler_params=_params())(w, g, m, v)


def _in_proj(h, w_bm):
    t, d = h.shape
    nb, _, wb = w_bm.shape
    tm = min(t, 512)

    def body(h_ref, w_ref, o_ref):
        o_ref[...] = _dot(h_ref[...], w_ref[...])

    return pl.pallas_call(
        body, name="in_proj", grid=(nb, t // tm), out_shape=jax.ShapeDtypeStruct((t, nb * wb), F32),
        in_specs=[pl.BlockSpec((tm, d), lambda j, i: (i, 0)), pl.BlockSpec((None, d, wb), lambda j, i: (j, 0, 0))],
        out_specs=pl.BlockSpec((tm, wb), lambda j, i: (i, j)), compiler_params=_params())(h, w_bm)


def _grad_w_in(h, dproj):
    t, d = h.shape
    nb = N_CHIPS
    tk = min(t, 512)
    nk = t // tk

    def body(h_ref, dp_ref, o_ref, acc_ref):
        k = pl.program_id(1)

        @pl.when(k == 0)
        def _():
            acc_ref[...] = jnp.zeros_like(acc_ref)

        acc_ref[...] += _dot_tn(h_ref[...], dp_ref[...])

        @pl.when(k == nk - 1)
        def _():
            o_ref[...] = acc_ref[...]

    return pl.pallas_call(
        body, name="grad_w_in", grid=(nb, nk), out_shape=jax.ShapeDtypeStruct((nb, d, W_BLK), F32),
        in_specs=[pl.BlockSpec((tk, d), lambda j, k: (k, 0)), pl.BlockSpec((tk, W_BLK), lambda j, k: (k, j))],
        out_specs=pl.BlockSpec((None, d, W_BLK), lambda j, k: (j, 0, 0)),
        scratch_shapes=[pltpu.VMEM((d, W_BLK), F32)], compiler_params=_params())(h, dproj)


def _grad_h(dproj, w_bm):
    t = dproj.shape[0]
    nb, d, wb = w_bm.shape
    tm = min(t, 1024)

    def body(dp_ref, w_ref, o_ref, acc_ref):
        k = pl.program_id(1)

        @pl.when(k == 0)
        def _():
            acc_ref[...] = jnp.zeros_like(acc_ref)

        acc_ref[...] += _dot_nt(dp_ref[...], w_ref[...])

        @pl.when(k == nb - 1)
        def _():
            o_ref[...] = acc_ref[...]

    return pl.pallas_call(
        body, name="grad_h", grid=(t // tm, nb), out_shape=jax.ShapeDtypeStruct((t, d), F32),
        in_specs=[pl.BlockSpec((tm, wb), lambda i, k: (i, k)), pl.BlockSpec((None, d, wb), lambda i, k: (k, 0, 0))],
        out_specs=pl.BlockSpec((tm, d), lambda i, k: (i, 0)),
        scratch_shapes=[pltpu.VMEM((tm, d), F32)], compiler_params=_params())(dproj, w_bm)


def _grad_w_sq(a, b, name):
    t, d = a.shape
    n = b.shape[1]
    tk = min(t, 512)
    nk = t // tk

    def body(a_ref, b_ref, o_ref):
        k = pl.program_id(0)

        @pl.when(k == 0)
        def _():
            o_ref[...] = jnp.zeros_like(o_ref)

        o_ref[...] += _dot_tn(a_ref[...], b_ref[...])

    return pl.pallas_call(
        body, name=name, grid=(nk,), out_shape=jax.ShapeDtypeStruct((d, n), F32),
        in_specs=[pl.BlockSpec((tk, d), lambda k: (k, 0)), pl.BlockSpec((tk, n), lambda k: (k, 0))],
        out_specs=pl.BlockSpec((d, n), lambda k: (0, 0)), compiler_params=_params())(a, b)


def _shift_down(v, d, fill):
    n = v.shape[0]
    if d % 8 == 0:
        return jnp.concatenate([jnp.full((d,) + v.shape[1:], fill, v.dtype), v[: n - d]], axis=0)
    row = lax.broadcasted_iota(jnp.int32, v.shape, 0)
    return jnp.where(row >= d, pltpu.roll(v, d, axis=0), fill)


def _shift_up(v, d, fill):
    n = v.shape[0]
    if d % 8 == 0:
        return jnp.concatenate([v[d:], jnp.full((d,) + v.shape[1:], fill, v.dtype)], axis=0)
    row = lax.broadcasted_iota(jnp.int32, v.shape, 0)
    return jnp.where(row < n - d, pltpu.roll(v, n - d, axis=0), fill)


def _scan(a, b, shift):
    n = a.shape[0]
    d = 1
    while d < n:
        b = a * shift(b, d, 0.0) + b
        if 2 * d < n:
            a = a * shift(a, d, 1.0)
        d *= 2
    return b


def _neg_expm1(y):
    series = -y * (1.0 + y * (1.0 / 2.0) * (1.0 + y * (1.0 / 3.0) * (1.0 + y * (1.0 / 4.0) * (
        1.0 + y * (1.0 / 5.0) * (1.0 + y * (1.0 / 6.0) * (1.0 + y * (1.0 / 7.0)))))))
    return jnp.where(y > -0.25, series, 1.0 - jnp.exp(y))


def _softplus(z):
    e = jnp.exp(-jnp.abs(z))
    w = 1.0 + e
    log1p = jnp.where(w == 1.0, e, jnp.log(w) * (e / jnp.where(w == 1.0, 1.0, w - 1.0)))
    return jnp.maximum(z, 0.0) + log1p


def _conv(up, cw, cb):
    out = cb + cw[CONV_WIDTH - 1:CONV_WIDTH, :] * up
    for j in range(CONV_WIDTH - 1):
        out = out + cw[j:j + 1, :] * _shift_down(up, CONV_WIDTH - 1 - j, 0.0)
    return out


def _lru_gates(u, wa_ref, ba_ref, wx_ref, bx_ref, lam_ref):
    ub = u.astype(BF16)
    r = _sigmoid(_dot(ub, wa_ref[...].astype(BF16)) + ba_ref[...])
    i = _sigmoid(_dot(ub, wx_ref[...].astype(BF16)) + bx_ref[...])
    sp = _softplus(-lam_ref[...])
    log_a = (-LRU_C) * r * sp
    a = jnp.exp(log_a)
    mult = jnp.sqrt(_neg_expm1(2.0 * log_a))
    return r, i, sp, a, mult


def _lru_specs(s):
    cb = RNN_BLOCK
    vec = pl.BlockSpec((1, cb), lambda n, b: (0, n))
    return dict(
        up=pl.BlockSpec((None, s, cb), lambda n, b: (b, 0, OFF_RNN_X // cb + n)),
        gr=pl.BlockSpec((None, s, cb), lambda n, b: (b, 0, OFF_RNN_G // cb + n)),
        act=pl.BlockSpec((None, s, cb), lambda n, b: (b, 0, n)),
        cw=pl.BlockSpec((CONV_WIDTH, cb), lambda n, b: (0, n)),
        vec=vec,
        wblk=pl.BlockSpec((None, cb, cb), lambda n, b: (n, 0, 0)),
    )


def _lru_fwd(proj3, cw, cb, wa, ba, wx, bx, lam):
    bsz, s, _ = proj3.shape
    sp = _lru_specs(s)

    def body(up_ref, gr_ref, cw_ref, cb_ref, wa_ref, ba_ref, wx_ref, bx_ref, lam_ref, h_ref, y_ref):
        u = _conv(up_ref[...], cw_ref[...], cb_ref[...])
        _, i, _, a, mult = _lru_gates(u, wa_ref, ba_ref, wx_ref, bx_ref, lam_ref)
        h = _scan(a, mult * (i * u), _shift_down)
        h_ref[...] = h
        g = gr_ref[...]
        y_ref[...] = (h * (g * _sigmoid(g))).astype(BF16)

    return pl.pallas_call(
        body, name="lru_fwd", grid=(N_RNN_BLOCKS, bsz),
        out_shape=(jax.ShapeDtypeStruct((bsz, s, D_RNN), F32), jax.ShapeDtypeStruct((bsz, s, D_RNN), BF16)),
        in_specs=[sp["up"], sp["gr"], sp["cw"], sp["vec"], sp["wblk"], sp["vec"], sp["wblk"], sp["vec"], sp["vec"]],
        out_specs=(sp["act"], sp["act"]), compiler_params=_params())(proj3, proj3, cw, cb, wa, ba, wx, bx, lam)


def _lru_bwd(proj3, h3, dy3, cw, cb, wa, ba, wx, bx, lam):
    bsz, s, _ = proj3.shape
    sp = _lru_specs(s)

    def body(up_ref, gr_ref, h_ref, dy_ref, cw_ref, cb_ref, wa_ref, ba_ref, wx_ref, bx_ref, lam_ref,
             dup_ref, dgr_ref, dcw_ref, dcb_ref, dwa_ref, dba_ref, dwx_ref, dbx_ref, dlam_ref):
        b = pl.program_id(1)
        up = up_ref[...]
        cwv = cw_ref[...]
        u = _conv(up, cwv, cb_ref[...])
        r, i, spv, a, mult = _lru_gates(u, wa_ref, ba_ref, wx_ref, bx_ref, lam_ref)
        h = h_ref[...]
        g = gr_ref[...]
        dy = dy_ref[...]
        sg = _sigmoid(g)
        dgr_ref[...] = (dy * h * (sg * (1.0 + g * (1.0 - sg)))).astype(BF16)
        dh = dy * (g * sg)
        adj = _scan(_shift_up(a, 1, 0.0), dh, _shift_up)
        da = adj * _shift_down(h, 1, 0.0)
        dmult = adj * (i * u)
        di = adj * mult * u
        du = adj * mult * i
        dla = da * a - dmult * (a * a) / mult
        dr = dla * ((-LRU_C) * spv)
        dsp = jnp.sum(dla * ((-LRU_C) * r), axis=0, keepdims=True)
        dza = dr * r * (1.0 - r)
        dzx = di * i * (1.0 - i)
        ub = u.astype(BF16)
        dzab = dza.astype(BF16)
        dzxb = dzx.astype(BF16)
        du = du + _dot_nt(dzab, wa_ref[...].astype(BF16)) + _dot_nt(dzxb, wx_ref[...].astype(BF16))
        dup = cwv[CONV_WIDTH - 1:CONV_WIDTH, :] * du
        for j in range(CONV_WIDTH - 1):
            dup = dup + cwv[j:j + 1, :] * _shift_up(du, CONV_WIDTH - 1 - j, 0.0)
        dup_ref[...] = dup.astype(BF16)

        @pl.when(b == 0)
        def _():
            for ref in (dcw_ref, dcb_ref, dwa_ref, dba_ref, dwx_ref, dbx_ref, dlam_ref):
                ref[...] = jnp.zeros_like(ref)

        rows = [jnp.sum(du * _shift_down(up, CONV_WIDTH - 1 - j, 0.0), axis=0, keepdims=True)
                for j in range(CONV_WIDTH - 1)]
        rows.append(jnp.sum(du * up, axis=0, keepdims=True))
        dcw_ref[...] += jnp.concatenate(rows, axis=0)
        dcb_ref[...] += jnp.sum(du, axis=0, keepdims=True)
        dwa_ref[...] += _dot_tn(ub, dzab)
        dba_ref[...] += jnp.sum(dza, axis=0, keepdims=True)
        dwx_ref[...] += _dot_tn(ub, dzxb)
        dbx_ref[...] += jnp.sum(dzx, axis=0, keepdims=True)
        dlam_ref[...] += dsp * (-_sigmoid(-lam_ref[...]))

    act_b = jax.ShapeDtypeStruct((bsz, s, D_RNN), BF16)
    vec = jax.ShapeDtypeStruct((1, D_RNN), F32)
    wsd = jax.ShapeDtypeStruct((N_RNN_BLOCKS, RNN_BLOCK, RNN_BLOCK), F32)
    return pl.pallas_call(
        body, name="lru_bwd", grid=(N_RNN_BLOCKS, bsz),
        out_shape=(act_b, act_b, jax.ShapeDtypeStruct((CONV_WIDTH, D_RNN), F32), vec, wsd, vec, wsd, vec, vec),
        in_specs=[sp["up"], sp["gr"], sp["act"], sp["act"], sp["cw"], sp["vec"], sp["wblk"], sp["vec"],
                  sp["wblk"], sp["vec"], sp["vec"]],
        out_specs=(sp["act"], sp["act"], sp["cw"], sp["vec"], sp["wblk"], sp["vec"], sp["wblk"], sp["vec"], sp["vec"]),
        compiler_params=_params())(proj3, proj3, h3, dy3, cw, cb, wa, ba, wx, bx, lam)


def _rope_tables(s):
    half = ROPE_DIM // 2
    pos = jnp.arange(s, dtype=F32)
    inv_freq = ROPE_THETA ** (-jnp.arange(0, ROPE_DIM, 2, dtype=F32) / ROPE_DIM)
    ang = pos[:, None] * inv_freq[None, :]
    cos, sin = jnp.cos(ang), jnp.sin(ang)
    rest = HEAD_DIM - ROPE_DIM
    cos64 = jnp.concatenate([cos, cos, jnp.ones((s, rest), F32)], axis=1)
    sin64 = jnp.concatenate([-sin, sin, jnp.zeros((s, rest), F32)], axis=1)
    assert half * 2 == ROPE_DIM
    return jnp.tile(cos64, (1, LANES // HEAD_DIM)), jnp.tile(sin64, (1, LANES // HEAD_DIM))


def _swap_rot_halves(v):
    half = ROPE_DIM // 2
    lane = lax.broadcasted_iota(jnp.int32, v.shape, 1) % HEAD_DIM
    second = jnp.where(lane < ROPE_DIM, pltpu.roll(v, half, axis=1), 0.0)
    return jnp.where(lane < half, pltpu.roll(v, LANES - half, axis=1), second)


def _rope(v, cos, sin):
    tiles = []
    for t in range(v.shape[1] // LANES):
        vt = v[:, t * LANES:(t + 1) * LANES]
        tiles.append(vt * cos + _swap_rot_halves(vt) * sin)
    return tiles[0] if len(tiles) == 1 else jnp.concatenate(tiles, axis=1)


def _unrope(v, cos, sin):
    tiles = []
    for t in range(v.shape[1] // LANES):
        vt = v[:, t * LANES:(t + 1) * LANES]
        tiles.append(vt * cos + _swap_rot_halves(vt * sin))
    return tiles[0] if len(tiles) == 1 else jnp.concatenate(tiles, axis=1)


HEADS_PER_STEP = 8
QW = HEADS_PER_STEP * HEAD_DIM
N_PAIRS = N_Q_HEADS // HEADS_PER_STEP
Q_PER_KV = N_Q_HEADS // N_KV_HEADS


def _attn_specs(s, order):
    def mk(width, base, **kw):
        if order == "bp":
            return pl.BlockSpec((None, s, width), lambda b, p: (b, 0, base + p), **kw)
        return pl.BlockSpec((None, s, width), lambda p, b: (b, 0, base + p), **kw)
    one = dict(pipeline_mode=pl.Buffered(1))
    tbl = pl.BlockSpec((s, LANES), lambda *_: (0, 0))
    return dict(q=mk(QW, OFF_Q // QW), k=mk(LANES, OFF_K // LANES), v=mk(LANES, OFF_V // LANES),
                g=mk(QW, OFF_ATTN_G // QW), act=mk(QW, 0), kv=mk(LANES, 0), tbl=tbl,
                q1=mk(QW, OFF_Q // QW, **one), g1=mk(QW, OFF_ATTN_G // QW, **one), act1=mk(QW, 0, **one),
                smem=pl.BlockSpec(memory_space=pltpu.SMEM))


def _attn_prep(q_ref, k_ref, v_ref, cos_ref, sin_ref, q_sc, kp_sc, vp_sc, nb):
    kp_sc[0:ATT_BLK, :] = jnp.zeros((ATT_BLK, LANES), BF16)
    vp_sc[0:ATT_BLK, :] = jnp.zeros((ATT_BLK, LANES), BF16)

    def prep(n, carry):
        r0 = pl.multiple_of(n * ATT_BLK, ATT_BLK)
        cs = cos_ref[pl.ds(r0, ATT_BLK), :]
        sn = sin_ref[pl.ds(r0, ATT_BLK), :]
        q_sc[pl.ds(r0, ATT_BLK), :] = (_rope(q_ref[pl.ds(r0, ATT_BLK), :], cs, sn) * ATTN_SCALE).astype(BF16)
        kp_sc[pl.ds(r0 + ATT_BLK, ATT_BLK), :] = _rope(k_ref[pl.ds(r0, ATT_BLK), :], cs, sn).astype(BF16)
        vp_sc[pl.ds(r0 + ATT_BLK, ATT_BLK), :] = v_ref[pl.ds(r0, ATT_BLK), :].astype(BF16)
        return carry

    lax.fori_loop(0, nb, prep, 0)


def _band_mask(n):
    row = lax.broadcasted_iota(jnp.int32, (ATT_BLK, 2 * ATT_BLK), 0)
    col = lax.broadcasted_iota(jnp.int32, (ATT_BLK, 2 * ATT_BLK), 1)
    lo = jnp.where(n == 0, ATT_BLK, 0)
    return (col > row) & (col <= row + WINDOW) & (col >= lo)


def _softmax_with_sink(scores, valid, sink):
    sc = jnp.where(valid, scores, NEG_BIG)
    m = jnp.maximum(jnp.max(sc, axis=1, keepdims=True), sink)
    e = jnp.exp(sc - m)
    es = jnp.exp(sink - m)
    inv = 1.0 / (jnp.sum(e, axis=1, keepdims=True) + es)
    return e * inv, es * inv


def _attn_fwd(proj3, sinks, cosf, sinf):
    bsz, s, _ = proj3.shape
    nb = s // ATT_BLK
    sp = _attn_specs(s, "bp")

    def body(sink_ref, q_ref, k_ref, v_ref, g_ref, cos_ref, sin_ref, o_ref, y_ref, q_sc, kp_sc, vp_sc):
        p = pl.program_id(1)
        _attn_prep(q_ref, k_ref, v_ref, cos_ref, sin_ref, q_sc, kp_sc, vp_sc, nb)

        def blk(n, carry):
            r0 = pl.multiple_of(n * ATT_BLK, ATT_BLK)
            valid = _band_mask(n)
            qn = q_sc[pl.ds(r0, ATT_BLK), :]
            kw = kp_sc[pl.ds(r0, 2 * ATT_BLK), :]
            vw = vp_sc[pl.ds(r0, 2 * ATT_BLK), :]
            outs = []
            for hq in range(HEADS_PER_STEP):
                hk = hq // Q_PER_KV
                kh = kw[:, hk * HEAD_DIM:(hk + 1) * HEAD_DIM]
                vh = vw[:, hk * HEAD_DIM:(hk + 1) * HEAD_DIM]
                qg = qn[:, hq * HEAD_DIM:(hq + 1) * HEAD_DIM]
                pn, _ = _softmax_with_sink(_dot_nt(qg, kh), valid, sink_ref[p * HEADS_PER_STEP + hq])
                outs.append(_dot(pn.astype(BF16), vh))
            o = jnp.concatenate(outs, axis=1)
            o_ref[pl.ds(r0, ATT_BLK), :] = o
            g = g_ref[pl.ds(r0, ATT_BLK), :]
            y_ref[pl.ds(r0, ATT_BLK), :] = (o * (g * _sigmoid(g))).astype(BF16)
            return carry

        lax.fori_loop(0, nb, blk, 0)

    return pl.pallas_call(
        body, name="attn_fwd", grid=(bsz, N_PAIRS),
        out_shape=(jax.ShapeDtypeStruct((bsz, s, D_ATTN), F32), jax.ShapeDtypeStruct((bsz, s, D_ATTN), BF16)),
        in_specs=[sp["smem"], sp["q"], sp["k"], sp["v"], sp["g"], sp["tbl"], sp["tbl"]],
        out_specs=(sp["act"], sp["act"]),
        scratch_shapes=[pltpu.VMEM((s, QW), BF16), pltpu.VMEM((s + ATT_BLK, LANES), BF16),
                        pltpu.VMEM((s + ATT_BLK, LANES), BF16)],
        compiler_params=_params())(sinks, proj3, proj3, proj3, proj3, cosf, sinf)


def _attn_bwd(proj3, o3, dy3, sinks, cosf, sinf):
    bsz, s, _ = proj3.shape
    nb = s // ATT_BLK
    sp = _attn_specs(s, "pb")
    kv_per = HEADS_PER_STEP // Q_PER_KV

    def body(sink_ref, q_ref, k_ref, v_ref, g_ref, o_ref, dy_ref, cos_ref, sin_ref,
             dq_ref, dk_ref, dv_ref, dg_ref, ds_ref, q_sc, kp_sc, vp_sc, dq_sc, dk_sc, dv_sc):
        p = pl.program_id(0)
        b = pl.program_id(1)
        _attn_prep(q_ref, k_ref, v_ref, cos_ref, sin_ref, q_sc, kp_sc, vp_sc, nb)
        dk_sc[...] = jnp.zeros_like(dk_sc)
        dv_sc[...] = jnp.zeros_like(dv_sc)
        lane1 = lax.broadcasted_iota(jnp.int32, (1, LANES), 1)

        def blk(n, dsink):
            r0 = pl.multiple_of(n * ATT_BLK, ATT_BLK)
            valid = _band_mask(n)
            qn = q_sc[pl.ds(r0, ATT_BLK), :]
            kw = kp_sc[pl.ds(r0, 2 * ATT_BLK), :]
            vw = vp_sc[pl.ds(r0, 2 * ATT_BLK), :]
            g = g_ref[pl.ds(r0, ATT_BLK), :]
            dy = dy_ref[pl.ds(r0, ATT_BLK), :]
            sg = _sigmoid(g)
            dg_ref[pl.ds(r0, ATT_BLK), :] = (dy * o_ref[pl.ds(r0, ATT_BLK), :] * (sg * (1.0 + g * (1.0 - sg)))).astype(BF16)
            do = (dy * (g * sg)).astype(BF16)
            dqs = []
            for hq in range(HEADS_PER_STEP):
                hk = hq // Q_PER_KV
                kh = kw[:, hk * HEAD_DIM:(hk + 1) * HEAD_DIM]
                vh = vw[:, hk * HEAD_DIM:(hk + 1) * HEAD_DIM]
                qg = qn[:, hq * HEAD_DIM:(hq + 1) * HEAD_DIM]
                dog = do[:, hq * HEAD_DIM:(hq + 1) * HEAD_DIM]
                pn, ps = _softmax_with_sink(_dot_nt(qg, kh), valid, sink_ref[p * HEADS_PER_STEP + hq])
                dp = _dot_nt(dog, vh)
                delta = jnp.sum(pn * dp, axis=1, keepdims=True)
                dsc = (pn * (dp - delta)).astype(BF16)
                dsink = dsink + jnp.where(lane1 == hq, -jnp.sum(ps * delta), 0.0)
                dqs.append(_dot(dsc, kh) * ATTN_SCALE)
                dk_sc[hk, pl.ds(r0, 2 * ATT_BLK), :] += _dot_tn(dsc, qg)
                dv_sc[hk, pl.ds(r0, 2 * ATT_BLK), :] += _dot_tn(pn.astype(BF16), dog)
            dq_sc[pl.ds(r0, ATT_BLK), :] = jnp.concatenate(dqs, axis=1)
            return dsink

        dsink = lax.fori_loop(0, nb, blk, jnp.zeros((1, LANES), F32))

        @pl.when(b == 0)
        def _():
            ds_ref[...] = jnp.zeros_like(ds_ref)

        ds_ref[...] += dsink

        def post(n, carry):
            r0 = pl.multiple_of(n * ATT_BLK, ATT_BLK)
            cs = cos_ref[pl.ds(r0, ATT_BLK), :]
            sn = sin_ref[pl.ds(r0, ATT_BLK), :]
            dq_ref[pl.ds(r0, ATT_BLK), :] = _unrope(dq_sc[pl.ds(r0, ATT_BLK), :], cs, sn).astype(BF16)
            dk = jnp.concatenate([dk_sc[j, pl.ds(r0 + ATT_BLK, ATT_BLK), :] for j in range(kv_per)], axis=1)
            dk_ref[pl.ds(r0, ATT_BLK), :] = _unrope(dk, cs, sn).astype(BF16)
            dv = jnp.concatenate([dv_sc[j, pl.ds(r0 + ATT_BLK, ATT_BLK), :] for j in range(kv_per)], axis=1)
            dv_ref[pl.ds(r0, ATT_BLK), :] = dv.astype(BF16)
            return carry

        lax.fori_loop(0, nb, post, 0)

    act = jax.ShapeDtypeStruct((bsz, s, D_ATTN), BF16)
    kvs = jax.ShapeDtypeStruct((bsz, s, D_KV), BF16)
    return pl.pallas_call(
        body, name="attn_bwd", grid=(N_PAIRS, bsz),
        out_shape=(act, kvs, kvs, act, jax.ShapeDtypeStruct((N_PAIRS, 1, LANES), F32)),
        in_specs=[sp["smem"], sp["q1"], sp["k"], sp["v"], sp["g1"], sp["act1"], sp["act1"], sp["tbl"], sp["tbl"]],
        out_specs=(sp["act"], sp["kv"], sp["kv"], sp["act"], pl.BlockSpec((None, 1, LANES), lambda p, b: (p, 0, 0))),
        scratch_shapes=[pltpu.VMEM((s, QW), BF16), pltpu.VMEM((s + ATT_BLK, LANES), BF16),
                        pltpu.VMEM((s + ATT_BLK, LANES), BF16), pltpu.VMEM((s, QW), F32),
                        pltpu.VMEM((kv_per, s + ATT_BLK, HEAD_DIM), F32),
                        pltpu.VMEM((kv_per, s + ATT_BLK, HEAD_DIM), F32)],
        compiler_params=_params())(sinks, proj3, proj3, proj3, proj3, o3, dy3, cosf, sinf)


def _merge_fwd_bwd(x, tgt, y_rnn, y_attn, proj, w_r, w_a, w_o, gf):
    t, d = x.shape
    tm = min(t, 256)

    hw = d // 2

    def body(x_ref, t_ref, yr_ref, ya_ref, mr0_ref, mr1_ref, ma0_ref, ma1_ref, wr_ref, wa_ref, wo_ref, gf_ref,
             dmg_ref, dyr_ref, dya_ref, mg_ref, dx2_ref, dx2b_ref, dpr_ref, dpa_ref, loss_ref, dgf_ref):
        i = pl.program_id(0)
        wr = wr_ref[...]
        wa = wa_ref[...]
        wo = wo_ref[...]
        gfv = gf_ref[...]
        pr = _dot(yr_ref[...], wr)
        pa = _dot(ya_ref[...], wa)
        sr = _sigmoid(jnp.concatenate([mr0_ref[...], mr1_ref[...]], axis=1))
        sa = _sigmoid(jnp.concatenate([ma0_ref[...], ma1_ref[...]], axis=1))
        mb = (sr * pr + sa * pa).astype(BF16)
        mg_ref[...] = mb
        x2 = x_ref[...] + _dot(mb, wo)
        r2 = lax.rsqrt(jnp.mean(x2 * x2, axis=-1, keepdims=True) + NORM_EPS)
        nrm = x2 * r2
        err = nrm * gfv - t_ref[...]
        dy = err * (1.0 / d)
        dn = dy * gfv
        dx2 = r2 * (dn - nrm * jnp.mean(dn * nrm, axis=-1, keepdims=True))
        dx2_ref[...] = dx2
        dx2b = dx2.astype(BF16)
        dx2b_ref[...] = dx2b
        dmerged = _dot_nt(dx2b, wo)
        dpr = (dmerged * sr).astype(BF16)
        dpa = (dmerged * sa).astype(BF16)
        dpr_ref[...] = dpr
        dpa_ref[...] = dpa
        dmg_ref[:, 0:d] = (dmerged * pr * (sr * (1.0 - sr))).astype(BF16)
        dmg_ref[:, d:2 * d] = (dmerged * pa * (sa * (1.0 - sa))).astype(BF16)
        dyr_ref[...] = _dot_nt(dpr, wr)
        dya_ref[...] = _dot_nt(dpa, wa)

        @pl.when(i == 0)
        def _():
            loss_ref[...] = jnp.zeros_like(loss_ref)
            dgf_ref[...] = jnp.zeros_like(dgf_ref)

        loss_ref[...] += jnp.full((1, LANES), 0.5 / d, F32) * jnp.sum(err * err)
        dgf_ref[...] += jnp.sum(dy * nrm, axis=0, keepdims=True)

    tile = pl.BlockSpec((tm, d), lambda i: (i, 0))
    wsp = pl.BlockSpec((d, d), lambda i: (0, 0))

    def gate(col_blk):
        return pl.BlockSpec((tm, hw), lambda i: (i, col_blk))

    fb = jax.ShapeDtypeStruct((t, d), BF16)
    ff = jax.ShapeDtypeStruct((t, d), F32)
    return pl.pallas_call(
        body, name="merge_fwd_bwd", grid=(t // tm,),
        out_shape=(jax.ShapeDtypeStruct((t, 2 * d), BF16), ff, ff, fb, ff, fb, fb, fb,
                   jax.ShapeDtypeStruct((1, LANES), F32), jax.ShapeDtypeStruct((1, d), F32)),
        in_specs=[tile, tile, tile, tile] + [gate(OFF_MERGE_R // hw + j) for j in range(4)] + [
            wsp, wsp, wsp, pl.BlockSpec((1, d), lambda i: (0, 0))],
        out_specs=(pl.BlockSpec((tm, 2 * d), lambda i: (i, 0)), tile, tile, tile, tile, tile, tile, tile,
                   pl.BlockSpec((1, LANES), lambda i: (0, 0)), pl.BlockSpec((1, d), lambda i: (0, 0))),
        compiler_params=_params())(x, tgt, y_rnn, y_attn, proj, proj, proj, proj, w_r, w_a, w_o, gf)


def _local_grads(x, tgt, norm_g, w_in_bm, conv_w, conv_b, lru_w_a, lru_b_a, lru_w_x, lru_b_x, lam, sinks,
                 w_r, w_a, w_o, gf):
    bsz, s, d = x.shape
    t = bsz * s
    x2 = x.reshape(t, d)
    h = _rmsnorm_fwd(x2, norm_g)
    proj = _in_proj(h, w_in_bm)
    proj3 = proj.reshape(bsz, s, D_IN)
    h_lru, y_rnn = _lru_fwd(proj3, conv_w, conv_b, lru_w_a, lru_b_a, lru_w_x, lru_b_x, lam)
    cosf, sinf = _rope_tables(s)
    o_attn, y_attn = _attn_fwd(proj3, sinks, cosf, sinf)
    y_rnn2 = y_rnn.reshape(t, d)
    y_attn2 = y_attn.reshape(t, d)
    dmg, dyr, dya, merged, dx2, dx2b, dpr, dpa, loss, dgf = _merge_fwd_bwd(
        x2, tgt.reshape(t, d), y_rnn2, y_attn2, proj, w_r, w_a, w_o, gf)
    gw_o = _grad_w_sq(merged, dx2b, "grad_w_o")
    gw_r = _grad_w_sq(y_rnn2, dpr, "grad_w_rnn_out")
    gw_a = _grad_w_sq(y_attn2, dpa, "grad_w_attn_out")
    dup, dgr, dcw, dcb, dwa, dba, dwx, dbx, dlam = _lru_bwd(
        proj3, h_lru, dyr.reshape(bsz, s, d), conv_w, conv_b, lru_w_a, lru_b_a, lru_w_x, lru_b_x, lam)
    dq, dk, dv, dga, dsink = _attn_bwd(proj3, o_attn, dya.reshape(bsz, s, d), sinks, cosf, sinf)
    dproj = jnp.concatenate([dup, dgr, dq, dk, dv, dga, dmg.reshape(bsz, s, 2 * d)], axis=-1).reshape(t, D_IN)
    gw_in = _grad_w_in(h, dproj)
    dh = _grad_h(dproj, w_in_bm)
    grad_x, dng = _rmsnorm_bwd(x2, dh, dx2, norm_g)
    small = dict(norm_g=dng, conv_w=dcw, conv_b=dcb, lru_w_a=dwa, lru_b_a=dba, lru_w_x=dwx, lru_b_x=dbx,
                 lru_lambda=dlam, attn_sinks=dsink[:, 0, :HEADS_PER_STEP].reshape(1, N_Q_HEADS), final_norm_g=dgf)
    return loss[0, 0], grad_x.reshape(bsz, s, d), gw_in, gw_r, gw_a, gw_o, small


ANY = pl.BlockSpec(memory_space=pl.ANY)


def _mesh_pos():
    return lax.axis_index("x"), lax.axis_index("y"), lax.axis_index("c")


def _chip_peers(x, y, c):
    return [((1 - x, y, c), 2 * (1 - x) + y), ((x, 1 - y, c), 2 * x + (1 - y)),
            ((1 - x, 1 - y, c), 2 * (1 - x) + (1 - y))]


def _remote(src, dst, send_sems, recv_sems, idx, peer):
    return pltpu.make_async_remote_copy(src_ref=src, dst_ref=dst, send_sem=send_sems.at[idx],
                                        recv_sem=recv_sems.at[idx], device_id=peer, device_id_type=MESH)


def _all_gather(shards):
    n = len(shards)

    def body(*refs):
        ins, outs = refs[:n], refs[n:2 * n]
        send_sems, recv_sems, loc_sems = refs[2 * n:]
        x, y, c = _mesh_pos()
        me = 2 * x + y
        local = [pltpu.make_async_copy(ins[t], outs[t].at[me], loc_sems.at[t]) for t in range(n)]
        for cp in local:
            cp.start()
        sends, recvs = [], []
        for t in range(n):
            for k, (peer, pj) in enumerate(_chip_peers(x, y, c)):
                sends.append(_remote(ins[t], outs[t].at[me], send_sems, recv_sems, 3 * t + k, peer))
                recvs.append(_remote(ins[t], outs[t].at[pj], send_sems, recv_sems, 3 * t + k, peer))
        for cp in sends:
            cp.start()
        for snd, rcv in zip(sends, recvs):
            snd.wait_send()
            rcv.wait_recv()
        for cp in local:
            cp.wait()

    return pl.pallas_call(
        body, name="all_gather_weights",
        out_shape=[jax.ShapeDtypeStruct((N_CHIPS,) + a.shape, a.dtype) for a in shards],
        in_specs=[ANY] * n, out_specs=[ANY] * n,
        scratch_shapes=[pltpu.SemaphoreType.DMA((3 * n,)), pltpu.SemaphoreType.DMA((3 * n,)),
                        pltpu.SemaphoreType.DMA((n,))],
        compiler_params=_params())(*shards)


def _pair_exchange(bigs, small):
    n = len(bigs)

    def body(*refs):
        ins, outs = refs[:n + 1], refs[n + 1:2 * n + 2]
        send_sems, recv_sems = refs[2 * n + 2:]
        x, y, c = _mesh_pos()
        sib = (x, y, 1 - c)
        cps = []
        for t in range(n):
            hr = bigs[t].shape[1] // 2
            src = ins[t].at[:, pl.ds(pl.multiple_of((1 - c) * hr, 8), hr), :]
            cps.append(_remote(src, outs[t], send_sems, recv_sems, t, sib))
        cps.append(_remote(ins[n], outs[n], send_sems, recv_sems, n, sib))
        for cp in cps:
            cp.start()
        for cp in cps:
            cp.wait()

    out_shape = [jax.ShapeDtypeStruct((a.shape[0], a.shape[1] // 2, a.shape[2]), a.dtype) for a in bigs]
    out_shape.append(jax.ShapeDtypeStruct(small.shape, small.dtype))
    return pl.pallas_call(
        body, name="pair_exchange", out_shape=out_shape, in_specs=[ANY] * (n + 1), out_specs=[ANY] * (n + 1),
        scratch_shapes=[pltpu.SemaphoreType.DMA((n + 1,)), pltpu.SemaphoreType.DMA((n + 1,))],
        compiler_params=_params())(*bigs, small)


def _row_tile(rows, row_bytes, cap_bytes=2 * 1024 * 1024):
    best = None
    for tr in range(8, rows + 1, 8):
        if rows % tr == 0 and tr * row_bytes <= cap_bytes:
            best = tr
    return best if best is not None else rows


def _pair_sum(own, got, c_idx, name):
    nblk, r, w = own.shape
    hr = r // 2
    tr = _row_tile(hr, w * 4)
    steps = hr // tr

    def body(c_ref, own_ref, got_ref, f_ref, b_ref):
        sm = own_ref[...] + got_ref[...]
        f_ref[...] = sm
        b_ref[...] = sm.astype(BF16)

    grid_spec = pltpu.PrefetchScalarGridSpec(
        num_scalar_prefetch=1, grid=(nblk, steps),
        in_specs=[pl.BlockSpec((None, tr, w), lambda j, i, c_ref: (j, c_ref[0] * steps + i, 0)),
                  pl.BlockSpec((None, tr, w), lambda j, i, c_ref: (j, i, 0))],
        out_specs=[pl.BlockSpec((None, tr, w), lambda j, i, c_ref: (j, i, 0)),
                   pl.BlockSpec((None, tr, w), lambda j, i, c_ref: (j, i, 0))])
    return pl.pallas_call(
        body, name=name, grid_spec=grid_spec,
        out_shape=[jax.ShapeDtypeStruct((nblk, hr, w), F32), jax.ShapeDtypeStruct((nblk, hr, w), BF16)],
        compiler_params=_params())(c_idx, own, got)


def _add2(a, b, name):
    r, w = a.shape
    tr = _row_tile(r, w * 4)

    def body(a_ref, b_ref, o_ref):
        o_ref[...] = a_ref[...] + b_ref[...]

    spec = pl.BlockSpec((tr, w), lambda i: (i, 0))
    return pl.pallas_call(body, name=name, grid=(r // tr,), out_shape=jax.ShapeDtypeStruct((r, w), F32),
                          in_specs=[spec, spec], out_specs=spec, compiler_params=_params())(a, b)


def _chip_exchange(bigs_b, small):
    n = len(bigs_b)

    def body(*refs):
        ins, outs = refs[:n + 1], refs[n + 1:2 * n + 2]
        send_sems, recv_sems, loc_sem = refs[2 * n + 2:]
        x, y, c = _mesh_pos()
        me = 2 * x + y
        my_rows = ins[n].at[pl.ds(pl.multiple_of(c * PK_HALF, 8), PK_HALF), :]
        local = pltpu.make_async_copy(my_rows, outs[n].at[me], loc_sem)
        local.start()
        sends, recvs = [], []
        for t in range(n + 1):
            for k, (peer, pj) in enumerate(_chip_peers(x, y, c)):
                src = ins[t].at[pj] if t < n else my_rows
                sends.append(_remote(src, outs[t].at[me], send_sems, recv_sems, 3 * t + k, peer))
                recvs.append(_remote(src, outs[t].at[pj], send_sems, recv_sems, 3 * t + k, peer))
        for cp in sends:
            cp.start()
        for snd, rcv in zip(sends, recvs):
            snd.wait_send()
            rcv.wait_recv()
        local.wait()

    out_shape = [jax.ShapeDtypeStruct(a.shape, a.dtype) for a in bigs_b]
    out_shape.append(jax.ShapeDtypeStruct((N_CHIPS, PK_HALF, LANES), F32))
    return pl.pallas_call(
        body, name="chip_exchange", out_shape=out_shape, in_specs=[ANY] * (n + 1), out_specs=[ANY] * (n + 1),
        scratch_shapes=[pltpu.SemaphoreType.DMA((3 * n + 3,)), pltpu.SemaphoreType.DMA((3 * n + 3,)),
                        pltpu.SemaphoreType.DMA],
        compiler_params=_params())(*bigs_b, small)


def _chip_sum(own, got, idx, name):
    _, r, w = own.shape
    tr = _row_tile(r, w * 4)

    def body(idx_ref, own_ref, g1_ref, g2_ref, g3_ref, o_ref):
        o_ref[...] = ((own_ref[...] + g1_ref[...].astype(F32)) + g2_ref[...].astype(F32)) + g3_ref[...].astype(F32)

    def pick(k):
        return pl.BlockSpec((None, tr, w), lambda i, idx_ref: (idx_ref[k], i, 0))

    grid_spec = pltpu.PrefetchScalarGridSpec(
        num_scalar_prefetch=1, grid=(r // tr,), in_specs=[pick(0), pick(1), pick(2), pick(3)],
        out_specs=pl.BlockSpec((tr, w), lambda i, idx_ref: (i, 0)))
    return pl.pallas_call(body, name=name, grid_spec=grid_spec, out_shape=jax.ShapeDtypeStruct((r, w), F32),
                          compiler_params=_params())(idx, own, got, got, got)


def _tree_sum4(a):
    _, r, w = a.shape
    tr = _row_tile(r, w * 4)

    def body(a_ref, o_ref):
        o_ref[...] = (a_ref[0] + a_ref[1]) + (a_ref[2] + a_ref[3])

    return pl.pallas_call(
        body, name="small_sum", grid=(r // tr,), out_shape=jax.ShapeDtypeStruct((r, w), F32),
        in_specs=[pl.BlockSpec((N_CHIPS, tr, w), lambda i: (0, i, 0))], out_specs=pl.BlockSpec((tr, w), lambda i: (i, 0)),
        compiler_params=_params())(a)


def _pair_gather(halves):
    n = len(halves)

    def body(*refs):
        ins, outs = refs[:n], refs[n:2 * n]
        send_sems, recv_sems, loc_sems = refs[2 * n:]
        x, y, c = _mesh_pos()
        sib = (x, y, 1 - c)
        local, sends, recvs = [], [], []
        for t in range(n):
            hr = halves[t].shape[0]
            mine = outs[t].at[pl.ds(pl.multiple_of(c * hr, 8), hr), :]
            theirs = outs[t].at[pl.ds(pl.multiple_of((1 - c) * hr, 8), hr), :]
            local.append(pltpu.make_async_copy(ins[t], mine, loc_sems.at[t]))
            sends.append(_remote(ins[t], mine, send_sems, recv_sems, t, sib))
            recvs.append(_remote(ins[t], theirs, send_sems, recv_sems, t, sib))
        for cp in local + sends:
            cp.start()
        for snd, rcv in zip(sends, recvs):
            snd.wait_send()
            rcv.wait_recv()
        for cp in local:
            cp.wait()

    return pl.pallas_call(
        body, name="pair_gather", out_shape=[jax.ShapeDtypeStruct((2 * a.shape[0], a.shape[1]), a.dtype) for a in halves],
        in_specs=[ANY] * n, out_specs=[ANY] * n,
        scratch_shapes=[pltpu.SemaphoreType.DMA((n,)), pltpu.SemaphoreType.DMA((n,)), pltpu.SemaphoreType.DMA((n,))],
        compiler_params=_params())(*halves)


def _reduce_scatter(bigs, small, x, y, c):
    n = len(bigs)
    me = 2 * x + y
    got1 = _pair_exchange(bigs, small)
    c_idx = jnp.reshape(c, (1,)).astype(jnp.int32)
    part = [_pair_sum(bigs[t], got1[t], c_idx, f"pair_sum_{t}") for t in range(n)]
    small_pair = _add2(small, got1[n], "pair_sum_small")
    got2 = _chip_exchange([p[1] for p in part], small_pair)
    others = [jnp.where(me <= k, k + 1, k) for k in range(N_CHIPS - 1)]
    idx = jnp.stack([me] + others).astype(jnp.int32)
    halves = [_chip_sum(part[t][0], got2[t], idx, f"chip_sum_{t}") for t in range(n)]
    halves.append(_tree_sum4(got2[n]))
    return _pair_gather(halves)


_VEC_NAMES = ("norm_g", "conv_b", "lru_b_a", "lru_b_x", "lru_lambda", "final_norm_g")


def _pack_small(p, conv_full=None):
    rows = [p["lru_w_a"].reshape(PK_WX - PK_WA, LANES), p["lru_w_x"].reshape(PK_VEC - PK_WX, LANES)]
    rows += [p[k].reshape(8, LANES) for k in _VEC_NAMES]
    rows.append(jnp.pad(p["attn_sinks"].reshape(1, N_Q_HEADS), ((0, 7), (0, LANES - N_Q_HEADS))))
    tail = PK_ROWS - PK_CONV
    if conv_full is None:
        rows.append(jnp.zeros((tail, LANES), F32))
    else:
        rows.append(conv_full.reshape(32, LANES))
        rows.append(jnp.zeros((tail - 32, LANES), F32))
    return jnp.concatenate(rows, axis=0)


def _unpack_small(pk, like):
    out = {"lru_w_a": pk[PK_WA:PK_WX].reshape(like["lru_w_a"].shape),
           "lru_w_x": pk[PK_WX:PK_VEC].reshape(like["lru_w_x"].shape)}
    for j, k in enumerate(_VEC_NAMES):
        out[k] = pk[PK_VEC + 8 * j:PK_VEC + 8 * j + 8].reshape(like[k].shape)
    out["attn_sinks"] = pk[PK_SINK:PK_SINK + 1, :N_Q_HEADS].reshape(like["attn_sinks"].shape)
    return out


_WEIGHTS = ("norm_g", "w_in", "conv_w", "conv_b", "lru_w_a", "lru_b_a", "lru_w_x", "lru_b_x", "lru_lambda",
            "attn_sinks", "w_rnn_out", "w_attn_out", "w_o", "final_norm_g")
_SMALL = ("norm_g", "conv_b", "lru_w_a", "lru_b_a", "lru_w_x", "lru_b_x", "lru_lambda", "attn_sinks", "final_norm_g")
_ROW_SHARDED = ("w_rnn_out", "w_attn_out", "w_o")


def kernel(x, norm_g, w_in, conv_w, conv_b, lru_w_a, lru_b_a, lru_w_x, lru_b_x, lru_lambda, attn_sinks, w_rnn_out, w_attn_out, w_o, final_norm_g, loss_target, m_norm_g, m_w_in, m_conv_w, m_conv_b, m_lru_w_a, m_lru_b_a, m_lru_w_x, m_lru_b_x, m_lru_lambda, m_attn_sinks, m_w_rnn_out, m_w_attn_out, m_w_o, m_final_norm_g, v_norm_g, v_w_in, v_conv_w, v_conv_b, v_lru_w_a, v_lru_b_a, v_lru_w_x, v_lru_b_x, v_lru_lambda, v_attn_sinks, v_w_rnn_out, v_w_attn_out, v_w_o, v_final_norm_g):
    w = dict(norm_g=norm_g, w_in=w_in, conv_w=conv_w, conv_b=conv_b, lru_w_a=lru_w_a, lru_b_a=lru_b_a, lru_w_x=lru_w_x,
             lru_b_x=lru_b_x, lru_lambda=lru_lambda, attn_sinks=attn_sinks, w_rnn_out=w_rnn_out, w_attn_out=w_attn_out,
             w_o=w_o, final_norm_g=final_norm_g)
    m = dict(norm_g=m_norm_g, w_in=m_w_in, conv_w=m_conv_w, conv_b=m_conv_b, lru_w_a=m_lru_w_a, lru_b_a=m_lru_b_a,
             lru_w_x=m_lru_w_x, lru_b_x=m_lru_b_x, lru_lambda=m_lru_lambda, attn_sinks=m_attn_sinks,
             w_rnn_out=m_w_rnn_out, w_attn_out=m_w_attn_out, w_o=m_w_o, final_norm_g=m_final_norm_g)
    v = dict(norm_g=v_norm_g, w_in=v_w_in, conv_w=v_conv_w, conv_b=v_conv_b, lru_w_a=v_lru_w_a, lru_b_a=v_lru_b_a,
             lru_w_x=v_lru_w_x, lru_b_x=v_lru_b_x, lru_lambda=v_lru_lambda, attn_sinks=v_attn_sinks,
             w_rnn_out=v_w_rnn_out, w_attn_out=v_w_attn_out, w_o=v_w_o, final_norm_g=v_final_norm_g)
    mx, my, mc = _mesh_pos()
    me = 2 * mx + my
    d = D_MODEL

    shards = [_cast_bf16(w["w_in"][0], "cast_w_in")] + [_cast_bf16(w[k][0], "cast_" + k) for k in _ROW_SHARDED]
    g_in, g_r, g_a, g_o, g_cw = _all_gather(shards + [w["conv_w"][0]])
    conv_full = g_cw.transpose(1, 0, 2).reshape(CONV_WIDTH, D_RNN)

    loss_local, grad_x, gw_in, gw_r, gw_a, gw_o, gsmall = _local_grads(
        x, loss_target, w["norm_g"], g_in, conv_full, w["conv_b"], w["lru_w_a"][0], w["lru_b_a"], w["lru_w_x"][0],
        w["lru_b_x"], w["lru_lambda"], w["attn_sinks"][0], g_r.reshape(d, d), g_a.reshape(d, d), g_o.reshape(d, d),
        w["final_norm_g"].reshape(1, d))
    loss = lax.psum(loss_local, ("x", "y", "c"))

    gpack = _pack_small(gsmall, gsmall["conv_w"])
    bigs = [gw_in] + [g.reshape(N_CHIPS, ROW_BLK, d) for g in (gw_r, gw_a, gw_o)]
    f_in, f_r, f_a, f_o, spack = _reduce_scatter(bigs, gpack, mx, my, mc)

    grads = _unpack_small(spack, w)
    conv_all = spack[PK_CONV:PK_CONV + 32].reshape(CONV_WIDTH, D_RNN)
    grads["conv_w"] = lax.dynamic_slice_in_dim(conv_all, me * (D_RNN // N_CHIPS), D_RNN // N_CHIPS, axis=1)[None]
    grads["w_in"] = f_in[None]
    grads["w_rnn_out"], grads["w_attn_out"], grads["w_o"] = f_r[None], f_a[None], f_o[None]

    delta, new_m, new_v = {}, {}, {}
    for k in ("w_in",) + _ROW_SHARDED:
        dk, mk, vk = _adamw(w[k][0], grads[k][0], m[k][0], v[k][0], "adamw_" + k)
        delta[k], new_m[k], new_v[k] = dk[None], mk[None], vk[None]
    shp = (2 * CONV_WIDTH, LANES)
    dk, mk, vk = _adamw(w["conv_w"].reshape(shp), grads["conv_w"].reshape(shp), m["conv_w"].reshape(shp),
                        v["conv_w"].reshape(shp), "adamw_conv_w")
    delta["conv_w"], new_m["conv_w"], new_v["conv_w"] = (a.reshape(w["conv_w"].shape) for a in (dk, mk, vk))
    dk, mk, vk = _adamw(_pack_small(w), spack, _pack_small(m), _pack_small(v), "adamw_small")
    for src, dst in ((dk, delta), (mk, new_m), (vk, new_v)):
        dst.update(_unpack_small(src, w))

    return (loss, grad_x, *[grads[k] for k in _WEIGHTS], *[delta[k] for k in _WEIGHTS],
            *[new_m[k] for k in _WEIGHTS], *[new_v[k] for k in _WEIGHTS])
```

```python
import functools
import math

import jax
import jax.numpy as jnp
from jax import lax
from jax.experimental import pallas as pl
from jax.experimental.pallas import tpu as pltpu

F32 = jnp.float32
BF16 = jnp.bfloat16
MESH = pl.DeviceIdType.MESH

D_MODEL = 1024
D_RNN = 1024
N_RNN_BLOCKS = 8
RNN_BLOCK = D_RNN // N_RNN_BLOCKS
CONV_WIDTH = 4
LRU_C = 8.0
HEAD_DIM = 64
N_Q_HEADS = 16
N_KV_HEADS = 4
D_ATTN = N_Q_HEADS * HEAD_DIM
D_KV = N_KV_HEADS * HEAD_DIM
WINDOW = 128
ROPE_DIM = HEAD_DIM // 4
ROPE_THETA = 500000.0
NORM_EPS = 1e-6
OFF_RNN_X = 0
OFF_RNN_G = OFF_RNN_X + D_RNN
OFF_Q = OFF_RNN_G + D_RNN
OFF_K = OFF_Q + D_ATTN
OFF_V = OFF_K + D_KV
OFF_ATTN_G = OFF_V + D_KV
OFF_MERGE_R = OFF_ATTN_G + D_ATTN
OFF_MERGE_A = OFF_MERGE_R + D_MODEL
D_IN = OFF_MERGE_A + D_MODEL

ADAM_LR = 0.001
ADAM_B1 = 0.9
ADAM_B2 = 0.999
ADAM_EPS = 1e-08
ADAM_WD = 0.01
ADAM_STEP = 10

N_CHIPS = 4
W_BLK = D_IN // N_CHIPS
ROW_BLK = D_MODEL // N_CHIPS
LANES = 128
ATT_BLK = 128
VMEM_LIMIT = 56 * 1024 * 1024
NEG_BIG = -1e30
ATTN_SCALE = 1.0 / math.sqrt(HEAD_DIM)

PK_WA = 0
PK_WX = PK_WA + N_RNN_BLOCKS * RNN_BLOCK
PK_VEC = PK_WX + N_RNN_BLOCKS * RNN_BLOCK
PK_SINK = PK_VEC + 6 * 8
PK_CONV = PK_SINK + 8
PK_ROWS = PK_CONV + 32 + 8
PK_HALF = PK_ROWS // 2


def _params(**kw):
    return pltpu.CompilerParams(vmem_limit_bytes=VMEM_LIMIT, **kw)


def _sigmoid(z):
    return 1.0 / (1.0 + jnp.exp(-z))


def _dot(a, b):
    return jnp.dot(a, b, preferred_element_type=F32)


def _dot_nt(a, b):
    return lax.dot_general(a, b, (((1,), (1,)), ((), ())), preferred_element_type=F32)


def _dot_tn(a, b):
    return lax.dot_general(a, b, (((0,), (0,)), ((), ())), preferred_element_type=F32)


def _put_slot(src, n_slots, slot_and_blk, rows, dtype, name):
    _, c = src.shape
    tr = _row_tile(rows, c * 4)
    steps = rows // tr

    def body(idx_ref, s_ref, o_ref):
        o_ref[...] = s_ref[...].astype(dtype)

    grid_spec = pltpu.PrefetchScalarGridSpec(
        num_scalar_prefetch=1, grid=(steps,),
        in_specs=[pl.BlockSpec((tr, c), lambda i, idx_ref: (idx_ref[1] * steps + i, 0))],
        out_specs=pl.BlockSpec((None, tr, c), lambda i, idx_ref: (idx_ref[0], i, 0)))
    return pl.pallas_call(body, name=name, grid_spec=grid_spec,
                          out_shape=jax.ShapeDtypeStruct((n_slots, rows, c), dtype),
                          compiler_params=_params())(slot_and_blk, src)


def _rmsnorm_fwd(x, g):
    t, d = x.shape
    tm = min(t, 512)

    def body(x_ref, g_ref, o_ref):
        xv = x_ref[...]
        r = lax.rsqrt(jnp.mean(xv * xv, axis=-1, keepdims=True) + NORM_EPS)
        o_ref[...] = (xv * r * g_ref[...]).astype(BF16)

    return pl.pallas_call(
        body, name="rmsnorm_fwd", grid=(t // tm,), out_shape=jax.ShapeDtypeStruct((t, d), BF16),
        in_specs=[pl.BlockSpec((tm, d), lambda i: (i, 0)), pl.BlockSpec((1, d), lambda i: (0, 0))],
        out_specs=pl.BlockSpec((tm, d), lambda i: (i, 0)), compiler_params=_params())(x, g)


def _rmsnorm_bwd(x, dh, dx2, g):
    t, d = x.shape
    tm = min(t, 512)

    def body(x_ref, dh_ref, dx2_ref, g_ref, gx_ref, dg_ref):
        i = pl.program_id(0)
        xv = x_ref[...]
        dhv = dh_ref[...]
        r = lax.rsqrt(jnp.mean(xv * xv, axis=-1, keepdims=True) + NORM_EPS)
        nrm = xv * r
        dn = dhv * g_ref[...]
        gx_ref[...] = dx2_ref[...] + r * (dn - nrm * jnp.mean(dn * nrm, axis=-1, keepdims=True))

        @pl.when(i == 0)
        def _():
            dg_ref[...] = jnp.zeros_like(dg_ref)

        dg_ref[...] += jnp.sum(dhv * nrm, axis=0, keepdims=True)

    return pl.pallas_call(
        body, name="rmsnorm_bwd", grid=(t // tm,),
        out_shape=(jax.ShapeDtypeStruct((t, d), F32), jax.ShapeDtypeStruct((1, d), F32)),
        in_specs=[pl.BlockSpec((tm, d), lambda i: (i, 0)), pl.BlockSpec((tm, d), lambda i: (i, 0)),
                  pl.BlockSpec((tm, d), lambda i: (i, 0)), pl.BlockSpec((1, d), lambda i: (0, 0))],
        out_specs=(pl.BlockSpec((tm, d), lambda i: (i, 0)), pl.BlockSpec((1, d), lambda i: (0, 0))),
        compiler_params=_params())(x, dh, dx2, g)


def _adamw(w, g, m, v, name):
    r, c = w.shape
    tr = _row_tile(r, c * 4, 1024 * 1024)
    c1 = 1.0 - ADAM_B1 ** ADAM_STEP
    c2 = 1.0 - ADAM_B2 ** ADAM_STEP

    def body(w_ref, g_ref, m_ref, v_ref, d_ref, nm_ref, nv_ref):
        gv = g_ref[...]
        nm = ADAM_B1 * m_ref[...] + (1.0 - ADAM_B1) * gv
        nv = ADAM_B2 * v_ref[...] + (1.0 - ADAM_B2) * (gv * gv)
        m_hat = nm / c1
        v_hat = nv / c2
        d_ref[...] = -ADAM_LR * (m_hat / (jnp.sqrt(v_hat) + ADAM_EPS) + ADAM_WD * w_ref[...])
        nm_ref[...] = nm
        nv_ref[...] = nv

    spec = pl.BlockSpec((tr, c), lambda i: (i, 0))
    sds = jax.ShapeDtypeStruct((r, c), F32)
    return pl.pallas_call(
        body, name=name, grid=(r // tr,), out_shape=(sds, sds, sds),
        in_specs=[spec, spec, spec, spec], out_specs=(spec, spec, spec), compiler_params=_params())(w, g, m, v)


def _in_proj(h, w_bm):
    t, d = h.shape
    nb, _, wb = w_bm.shape
    tm = min(t, 512)

    def body(h_ref, w_ref, o_ref):
        o_ref[...] = _dot(h_ref[...], w_ref[...])

    return pl.pallas_call(
        body, name="in_proj", grid=(nb, t // tm), out_shape=jax.ShapeDtypeStruct((t, nb * wb), F32),
        in_specs=[pl.BlockSpec((tm, d), lambda j, i: (i, 0)), pl.BlockSpec((None, d, wb), lambda j, i: (j, 0, 0))],
        out_specs=pl.BlockSpec((tm, wb), lambda j, i: (i, j)), compiler_params=_params())(h, w_bm)


def _grad_w_in(h, dproj):
    t, d = h.shape
    nb = N_CHIPS
    tk = min(t, 512)
    nk = t // tk

    def body(h_ref, dp_ref, o_ref, acc_ref):
        k = pl.program_id(1)

        @pl.when(k == 0)
        def _():
            acc_ref[...] = jnp.zeros_like(acc_ref)

        acc_ref[...] += _dot_tn(h_ref[...], dp_ref[...])

        @pl.when(k == nk - 1)
        def _():
            o_ref[...] = acc_ref[...]

    return pl.pallas_call(
        body, name="grad_w_in", grid=(nb, nk), out_shape=jax.ShapeDtypeStruct((nb, d, W_BLK), F32),
        in_specs=[pl.BlockSpec((tk, d), lambda j, k: (k, 0)), pl.BlockSpec((tk, W_BLK), lambda j, k: (k, j))],
        out_specs=pl.BlockSpec((None, d, W_BLK), lambda j, k: (j, 0, 0)),
        scratch_shapes=[pltpu.VMEM((d, W_BLK), F32)], compiler_params=_params())(h, dproj)


def _grad_h(dproj, w_bm):
    t = dproj.shape[0]
    nb, d, wb = w_bm.shape
    tm = min(t, 1024)

    def body(dp_ref, w_ref, o_ref, acc_ref):
        k = pl.program_id(1)

        @pl.when(k == 0)
        def _():
            acc_ref[...] = jnp.zeros_like(acc_ref)

        acc_ref[...] += _dot_nt(dp_ref[...], w_ref[...])

        @pl.when(k == nb - 1)
        def _():
            o_ref[...] = acc_ref[...]

    return pl.pallas_call(
        body, name="grad_h", grid=(t // tm, nb), out_shape=jax.ShapeDtypeStruct((t, d), F32),
        in_specs=[pl.BlockSpec((tm, wb), lambda i, k: (i, k)), pl.BlockSpec((None, d, wb), lambda i, k: (k, 0, 0))],
        out_specs=pl.BlockSpec((tm, d), lambda i, k: (i, 0)),
        scratch_shapes=[pltpu.VMEM((tm, d), F32)], compiler_params=_params())(dproj, w_bm)


def _grad_w_sq(a, b, name):
    t, d = a.shape
    n = b.shape[1]
    tk = min(t, 512)
    nk = t // tk

    def body(a_ref, b_ref, o_ref):
        k = pl.program_id(0)

        @pl.when(k == 0)
        def _():
            o_ref[...] = jnp.zeros_like(o_ref)

        o_ref[...] += _dot_tn(a_ref[...], b_ref[...])

    return pl.pallas_call(
        body, name=name, grid=(nk,), out_shape=jax.ShapeDtypeStruct((d, n), F32),
        in_specs=[pl.BlockSpec((tk, d), lambda k: (k, 0)), pl.BlockSpec((tk, n), lambda k: (k, 0))],
        out_specs=pl.BlockSpec((d, n), lambda k: (0, 0)), compiler_params=_params())(a, b)


def _shift_down(v, d, fill):
    n = v.shape[0]
    if d % 8 == 0:
        return jnp.concatenate([jnp.full((d,) + v.shape[1:], fill, v.dtype), v[: n - d]], axis=0)
    row = lax.broadcasted_iota(jnp.int32, v.shape, 0)
    return jnp.where(row >= d, pltpu.roll(v, d, axis=0), fill)


def _shift_up(v, d, fill):
    n = v.shape[0]
    if d % 8 == 0:
        return jnp.concatenate([v[d:], jnp.full((d,) + v.shape[1:], fill, v.dtype)], axis=0)
    row = lax.broadcasted_iota(jnp.int32, v.shape, 0)
    return jnp.where(row < n - d, pltpu.roll(v, n - d, axis=0), fill)


def _scan(a, b, shift):
    n = a.shape[0]
    d = 1
    while d < n:
        b = a * shift(b, d, 0.0) + b
        if 2 * d < n:
            a = a * shift(a, d, 1.0)
        d *= 2
    return b


def _neg_expm1(y):
    series = -y * (1.0 + y * (1.0 / 2.0) * (1.0 + y * (1.0 / 3.0) * (1.0 + y * (1.0 / 4.0) * (
        1.0 + y * (1.0 / 5.0) * (1.0 + y * (1.0 / 6.0) * (1.0 + y * (1.0 / 7.0)))))))
    return jnp.where(y > -0.25, series, 1.0 - jnp.exp(y))


def _softplus(z):
    e = jnp.exp(-jnp.abs(z))
    w = 1.0 + e
    log1p = jnp.where(w == 1.0, e, jnp.log(w) * (e / jnp.where(w == 1.0, 1.0, w - 1.0)))
    return jnp.maximum(z, 0.0) + log1p


def _conv(up, cw, cb):
    out = cb + cw[CONV_WIDTH - 1:CONV_WIDTH, :] * up
    for j in range(CONV_WIDTH - 1):
        out = out + cw[j:j + 1, :] * _shift_down(up, CONV_WIDTH - 1 - j, 0.0)
    return out


def _lru_gates(u, wa_ref, ba_ref, wx_ref, bx_ref, lam_ref):
    ub = u.astype(BF16)
    r = _sigmoid(_dot(ub, wa_ref[...].astype(BF16)) + ba_ref[...])
    i = _sigmoid(_dot(ub, wx_ref[...].astype(BF16)) + bx_ref[...])
    sp = _softplus(-lam_ref[...])
    log_a = (-LRU_C) * r * sp
    a = jnp.exp(log_a)
    mult = jnp.sqrt(_neg_expm1(2.0 * log_a))
    return r, i, sp, a, mult


def _lru_specs(s):
    cb = RNN_BLOCK
    vec = pl.BlockSpec((1, cb), lambda n, b: (0, n))
    return dict(
        up=pl.BlockSpec((None, s, cb), lambda n, b: (b, 0, OFF_RNN_X // cb + n)),
        gr=pl.BlockSpec((None, s, cb), lambda n, b: (b, 0, OFF_RNN_G // cb + n)),
        act=pl.BlockSpec((None, s, cb), lambda n, b: (b, 0, n)),
        cw=pl.BlockSpec((CONV_WIDTH, cb), lambda n, b: (0, n)),
        vec=vec,
        wblk=pl.BlockSpec((None, cb, cb), lambda n, b: (n, 0, 0)),
    )


def _lru_fwd(proj3, cw, cb, wa, ba, wx, bx, lam):
    bsz, s, _ = proj3.shape
    sp = _lru_specs(s)

    def body(up_ref, gr_ref, cw_ref, cb_ref, wa_ref, ba_ref, wx_ref, bx_ref, lam_ref, h_ref, y_ref):
        u = _conv(up_ref[...], cw_ref[...], cb_ref[...])
        _, i, _, a, mult = _lru_gates(u, wa_ref, ba_ref, wx_ref, bx_ref, lam_ref)
        h = _scan(a, mult * (i * u), _shift_down)
        h_ref[...] = h
        g = gr_ref[...]
        y_ref[...] = (h * (g * _sigmoid(g))).astype(BF16)

    return pl.pallas_call(
        body, name="lru_fwd", grid=(N_RNN_BLOCKS, bsz),
        out_shape=(jax.ShapeDtypeStruct((bsz, s, D_RNN), F32), jax.ShapeDtypeStruct((bsz, s, D_RNN), BF16)),
        in_specs=[sp["up"], sp["gr"], sp["cw"], sp["vec"], sp["wblk"], sp["vec"], sp["wblk"], sp["vec"], sp["vec"]],
        out_specs=(sp["act"], sp["act"]), compiler_params=_params())(proj3, proj3, cw, cb, wa, ba, wx, bx, lam)


def _lru_bwd(proj3, h3, dy3, cw, cb, wa, ba, wx, bx, lam):
    bsz, s, _ = proj3.shape
    sp = _lru_specs(s)

    def body(up_ref, gr_ref, h_ref, dy_ref, cw_ref, cb_ref, wa_ref, ba_ref, wx_ref, bx_ref, lam_ref,
             dup_ref, dgr_ref, dcw_ref, dcb_ref, dwa_ref, dba_ref, dwx_ref, dbx_ref, dlam_ref):
        b = pl.program_id(1)
        up = up_ref[...]
        cwv = cw_ref[...]
        u = _conv(up, cwv, cb_ref[...])
        r, i, spv, a, mult = _lru_gates(u, wa_ref, ba_ref, wx_ref, bx_ref, lam_ref)
        h = h_ref[...]
        g = gr_ref[...]
        dy = dy_ref[...]
        sg = _sigmoid(g)
        dgr_ref[...] = (dy * h * (sg * (1.0 + g * (1.0 - sg)))).astype(BF16)
        dh = dy * (g * sg)
        adj = _scan(_shift_up(a, 1, 0.0), dh, _shift_up)
        da = adj * _shift_down(h, 1, 0.0)
        dmult = adj * (i * u)
        di = adj * mult * u
        du = adj * mult * i
        dla = da * a - dmult * (a * a) / mult
        dr = dla * ((-LRU_C) * spv)
        dsp = jnp.sum(dla * ((-LRU_C) * r), axis=0, keepdims=True)
        dza = dr * r * (1.0 - r)
        dzx = di * i * (1.0 - i)
        ub = u.astype(BF16)
        dzab = dza.astype(BF16)
        dzxb = dzx.astype(BF16)
        du = du + _dot_nt(dzab, wa_ref[...].astype(BF16)) + _dot_nt(dzxb, wx_ref[...].astype(BF16))
        dup = cwv[CONV_WIDTH - 1:CONV_WIDTH, :] * du
        for j in range(CONV_WIDTH - 1):
            dup = dup + cwv[j:j + 1, :] * _shift_up(du, CONV_WIDTH - 1 - j, 0.0)
        dup_ref[...] = dup.astype(BF16)

        @pl.when(b == 0)
        def _():
            for ref in (dcw_ref, dcb_ref, dwa_ref, dba_ref, dwx_ref, dbx_ref, dlam_ref):
                ref[...] = jnp.zeros_like(ref)

        rows = [jnp.sum(du * _shift_down(up, CONV_WIDTH - 1 - j, 0.0), axis=0, keepdims=True)
                for j in range(CONV_WIDTH - 1)]
        rows.append(jnp.sum(du * up, axis=0, keepdims=True))
        dcw_ref[...] += jnp.concatenate(rows, axis=0)
        dcb_ref[...] += jnp.sum(du, axis=0, keepdims=True)
        dwa_ref[...] += _dot_tn(ub, dzab)
        dba_ref[...] += jnp.sum(dza, axis=0, keepdims=True)
        dwx_ref[...] += _dot_tn(ub, dzxb)
        dbx_ref[...] += jnp.sum(dzx, axis=0, keepdims=True)
        dlam_ref[...] += dsp * (-_sigmoid(-lam_ref[...]))

    act_b = jax.ShapeDtypeStruct((bsz, s, D_RNN), BF16)
    vec = jax.ShapeDtypeStruct((1, D_RNN), F32)
    wsd = jax.ShapeDtypeStruct((N_RNN_BLOCKS, RNN_BLOCK, RNN_BLOCK), F32)
    return pl.pallas_call(
        body, name="lru_bwd", grid=(N_RNN_BLOCKS, bsz),
        out_shape=(act_b, act_b, jax.ShapeDtypeStruct((CONV_WIDTH, D_RNN), F32), vec, wsd, vec, wsd, vec, vec),
        in_specs=[sp["up"], sp["gr"], sp["act"], sp["act"], sp["cw"], sp["vec"], sp["wblk"], sp["vec"],
                  sp["wblk"], sp["vec"], sp["vec"]],
        out_specs=(sp["act"], sp["act"], sp["cw"], sp["vec"], sp["wblk"], sp["vec"], sp["wblk"], sp["vec"], sp["vec"]),
        compiler_params=_params())(proj3, proj3, h3, dy3, cw, cb, wa, ba, wx, bx, lam)


def _rope_tables(s):
    half = ROPE_DIM // 2
    pos = jnp.arange(s, dtype=F32)
    inv_freq = ROPE_THETA ** (-jnp.arange(0, ROPE_DIM, 2, dtype=F32) / ROPE_DIM)
    ang = pos[:, None] * inv_freq[None, :]
    cos, sin = jnp.cos(ang), jnp.sin(ang)
    rest = HEAD_DIM - ROPE_DIM
    cos64 = jnp.concatenate([cos, cos, jnp.ones((s, rest), F32)], axis=1)
    sin64 = jnp.concatenate([-sin, sin, jnp.zeros((s, rest), F32)], axis=1)
    assert half * 2 == ROPE_DIM
    return jnp.tile(cos64, (1, LANES // HEAD_DIM)), jnp.tile(sin64, (1, LANES // HEAD_DIM))


def _swap_rot_halves(v):
    half = ROPE_DIM // 2
    lane = lax.broadcasted_iota(jnp.int32, v.shape, 1) % HEAD_DIM
    second = jnp.where(lane < ROPE_DIM, pltpu.roll(v, half, axis=1), 0.0)
    return jnp.where(lane < half, pltpu.roll(v, LANES - half, axis=1), second)


def _rope(v, cos, sin):
    tiles = []
    for t in range(v.shape[1] // LANES):
        vt = v[:, t * LANES:(t + 1) * LANES]
        tiles.append(vt * cos + _swap_rot_halves(vt) * sin)
    return tiles[0] if len(tiles) == 1 else jnp.concatenate(tiles, axis=1)


def _unrope(v, cos, sin):
    tiles = []
    for t in range(v.shape[1] // LANES):
        vt = v[:, t * LANES:(t + 1) * LANES]
        tiles.append(vt * cos + _swap_rot_halves(vt * sin))
    return tiles[0] if len(tiles) == 1 else jnp.concatenate(tiles, axis=1)


HEADS_PER_STEP = 8
QW = HEADS_PER_STEP * HEAD_DIM
N_PAIRS = N_Q_HEADS // HEADS_PER_STEP
Q_PER_KV = N_Q_HEADS // N_KV_HEADS


def _attn_specs(s, order):
    def mk(width, base, **kw):
        if order == "bp":
            return pl.BlockSpec((None, s, width), lambda b, p: (b, 0, base + p), **kw)
        return pl.BlockSpec((None, s, width), lambda p, b: (b, 0, base + p), **kw)
    one = dict(pipeline_mode=pl.Buffered(1))
    tbl = pl.BlockSpec((s, LANES), lambda *_: (0, 0))
    return dict(q=mk(QW, OFF_Q // QW), k=mk(LANES, OFF_K // LANES), v=mk(LANES, OFF_V // LANES),
                g=mk(QW, OFF_ATTN_G // QW), act=mk(QW, 0), kv=mk(LANES, 0), tbl=tbl,
                q1=mk(QW, OFF_Q // QW, **one), g1=mk(QW, OFF_ATTN_G // QW, **one), act1=mk(QW, 0, **one),
                smem=pl.BlockSpec(memory_space=pltpu.SMEM))


def _attn_prep(q_ref, k_ref, v_ref, cos_ref, sin_ref, q_sc, kp_sc, vp_sc, nb):
    kp_sc[0:ATT_BLK, :] = jnp.zeros((ATT_BLK, LANES), BF16)
    vp_sc[0:ATT_BLK, :] = jnp.zeros((ATT_BLK, LANES), BF16)

    def prep(n, carry):
        r0 = pl.multiple_of(n * ATT_BLK, ATT_BLK)
        cs = cos_ref[pl.ds(r0, ATT_BLK), :]
        sn = sin_ref[pl.ds(r0, ATT_BLK), :]
        q_sc[pl.ds(r0, ATT_BLK), :] = (_rope(q_ref[pl.ds(r0, ATT_BLK), :], cs, sn) * ATTN_SCALE).astype(BF16)
        kp_sc[pl.ds(r0 + ATT_BLK, ATT_BLK), :] = _rope(k_ref[pl.ds(r0, ATT_BLK), :], cs, sn).astype(BF16)
        vp_sc[pl.ds(r0 + ATT_BLK, ATT_BLK), :] = v_ref[pl.ds(r0, ATT_BLK), :].astype(BF16)
        return carry

    lax.fori_loop(0, nb, prep, 0)


def _band_mask(n):
    row = lax.broadcasted_iota(jnp.int32, (ATT_BLK, 2 * ATT_BLK), 0)
    col = lax.broadcasted_iota(jnp.int32, (ATT_BLK, 2 * ATT_BLK), 1)
    lo = jnp.where(n == 0, ATT_BLK, 0)
    return (col > row) & (col <= row + WINDOW) & (col >= lo)


def _softmax_with_sink(scores, valid, sink):
    sc = jnp.where(valid, scores, NEG_BIG)
    m = jnp.maximum(jnp.max(sc, axis=1, keepdims=True), sink)
    e = jnp.exp(sc - m)
    es = jnp.exp(sink - m)
    inv = 1.0 / (jnp.sum(e, axis=1, keepdims=True) + es)
    return e * inv, es * inv


def _attn_fwd(proj3, sinks, cosf, sinf):
    bsz, s, _ = proj3.shape
    nb = s // ATT_BLK
    sp = _attn_specs(s, "bp")

    def body(sink_ref, q_ref, k_ref, v_ref, g_ref, cos_ref, sin_ref, o_ref, y_ref, q_sc, kp_sc, vp_sc):
        p = pl.program_id(1)
        _attn_prep(q_ref, k_ref, v_ref, cos_ref, sin_ref, q_sc, kp_sc, vp_sc, nb)

        def blk(n, carry):
            r0 = pl.multiple_of(n * ATT_BLK, ATT_BLK)
            valid = _band_mask(n)
            qn = q_sc[pl.ds(r0, ATT_BLK), :]
            kw = kp_sc[pl.ds(r0, 2 * ATT_BLK), :]
            vw = vp_sc[pl.ds(r0, 2 * ATT_BLK), :]
            outs = []
            for hq in range(HEADS_PER_STEP):
                hk = hq // Q_PER_KV
                kh = kw[:, hk * HEAD_DIM:(hk + 1) * HEAD_DIM]
                vh = vw[:, hk * HEAD_DIM:(hk + 1) * HEAD_DIM]
                qg = qn[:, hq * HEAD_DIM:(hq + 1) * HEAD_DIM]
                pn, _ = _softmax_with_sink(_dot_nt(qg, kh), valid, sink_ref[p * HEADS_PER_STEP + hq])
                outs.append(_dot(pn.astype(BF16), vh))
            o = jnp.concatenate(outs, axis=1)
            o_ref[pl.ds(r0, ATT_BLK), :] = o
            g = g_ref[pl.ds(r0, ATT_BLK), :]
            y_ref[pl.ds(r0, ATT_BLK), :] = (o * (g * _sigmoid(g))).astype(BF16)
            return carry

        lax.fori_loop(0, nb, blk, 0)

    return pl.pallas_call(
        body, name="attn_fwd", grid=(bsz, N_PAIRS),
        out_shape=(jax.ShapeDtypeStruct((bsz, s, D_ATTN), F32), jax.ShapeDtypeStruct((bsz, s, D_ATTN), BF16)),
        in_specs=[sp["smem"], sp["q"], sp["k"], sp["v"], sp["g"], sp["tbl"], sp["tbl"]],
        out_specs=(sp["act"], sp["act"]),
        scratch_shapes=[pltpu.VMEM((s, QW), BF16), pltpu.VMEM((s + ATT_BLK, LANES), BF16),
                        pltpu.VMEM((s + ATT_BLK, LANES), BF16)],
        compiler_params=_params())(sinks, proj3, proj3, proj3, proj3, cosf, sinf)


def _attn_bwd(proj3, o3, dy3, sinks, cosf, sinf):
    bsz, s, _ = proj3.shape
    nb = s // ATT_BLK
    sp = _attn_specs(s, "pb")
    kv_per = HEADS_PER_STEP // Q_PER_KV

    def body(sink_ref, q_ref, k_ref, v_ref, g_ref, o_ref, dy_ref, cos_ref, sin_ref,
             dq_ref, dk_ref, dv_ref, dg_ref, ds_ref, q_sc, kp_sc, vp_sc, dq_sc, dk_sc, dv_sc):
        p = pl.program_id(0)
        b = pl.program_id(1)
        _attn_prep(q_ref, k_ref, v_ref, cos_ref, sin_ref, q_sc, kp_sc, vp_sc, nb)
        dk_sc[...] = jnp.zeros_like(dk_sc)
        dv_sc[...] = jnp.zeros_like(dv_sc)
        lane1 = lax.broadcasted_iota(jnp.int32, (1, LANES), 1)

        def blk(n, dsink):
            r0 = pl.multiple_of(n * ATT_BLK, ATT_BLK)
            valid = _band_mask(n)
            qn = q_sc[pl.ds(r0, ATT_BLK), :]
            kw = kp_sc[pl.ds(r0, 2 * ATT_BLK), :]
            vw = vp_sc[pl.ds(r0, 2 * ATT_BLK), :]
            g = g_ref[pl.ds(r0, ATT_BLK), :]
            dy = dy_ref[pl.ds(r0, ATT_BLK), :]
            sg = _sigmoid(g)
            dg_ref[pl.ds(r0, ATT_BLK), :] = (dy * o_ref[pl.ds(r0, ATT_BLK), :] * (sg * (1.0 + g * (1.0 - sg)))).astype(BF16)
            do = (dy * (g * sg)).astype(BF16)
            dqs = []
            for hq in range(HEADS_PER_STEP):
                hk = hq // Q_PER_KV
                kh = kw[:, hk * HEAD_DIM:(hk + 1) * HEAD_DIM]
                vh = vw[:, hk * HEAD_DIM:(hk + 1) * HEAD_DIM]
                qg = qn[:, hq * HEAD_DIM:(hq + 1) * HEAD_DIM]
                dog = do[:, hq * HEAD_DIM:(hq + 1) * HEAD_DIM]
                pn, ps = _softmax_with_sink(_dot_nt(qg, kh), valid, sink_ref[p * HEADS_PER_STEP + hq])
                dp = _dot_nt(dog, vh)
                delta = jnp.sum(pn * dp, axis=1, keepdims=True)
                dsc = (pn * (dp - delta)).astype(BF16)
                dsink = dsink + jnp.where(lane1 == hq, -jnp.sum(ps * delta), 0.0)
                dqs.append(_dot(dsc, kh) * ATTN_SCALE)
                dk_sc[hk, pl.ds(r0, 2 * ATT_BLK), :] += _dot_tn(dsc, qg)
                dv_sc[hk, pl.ds(r0, 2 * ATT_BLK), :] += _dot_tn(pn.astype(BF16), dog)
            dq_sc[pl.ds(r0, ATT_BLK), :] = jnp.concatenate(dqs, axis=1)
            return dsink

        dsink = lax.fori_loop(0, nb, blk, jnp.zeros((1, LANES), F32))

        @pl.when(b == 0)
        def _():
            ds_ref[...] = jnp.zeros_like(ds_ref)

        ds_ref[...] += dsink

        def post(n, carry):
            r0 = pl.multiple_of(n * ATT_BLK, ATT_BLK)
            cs = cos_ref[pl.ds(r0, ATT_BLK), :]
            sn = sin_ref[pl.ds(r0, ATT_BLK), :]
            dq_ref[pl.ds(r0, ATT_BLK), :] = _unrope(dq_sc[pl.ds(r0, ATT_BLK), :], cs, sn).astype(BF16)
            dk = jnp.concatenate([dk_sc[j, pl.ds(r0 + ATT_BLK, ATT_BLK), :] for j in range(kv_per)], axis=1)
            dk_ref[pl.ds(r0, ATT_BLK), :] = _unrope(dk, cs, sn).astype(BF16)
            dv = jnp.concatenate([dv_sc[j, pl.ds(r0 + ATT_BLK, ATT_BLK), :] for j in range(kv_per)], axis=1)
            dv_ref[pl.ds(r0, ATT_BLK), :] = dv.astype(BF16)
            return carry

        lax.fori_loop(0, nb, post, 0)

    act = jax.ShapeDtypeStruct((bsz, s, D_ATTN), BF16)
    kvs = jax.ShapeDtypeStruct((bsz, s, D_KV), BF16)
    return pl.pallas_call(
        body, name="attn_bwd", grid=(N_PAIRS, bsz),
        out_shape=(act, kvs, kvs, act, jax.ShapeDtypeStruct((N_PAIRS, 1, LANES), F32)),
        in_specs=[sp["smem"], sp["q1"], sp["k"], sp["v"], sp["g1"], sp["act1"], sp["act1"], sp["tbl"], sp["tbl"]],
        out_specs=(sp["act"], sp["kv"], sp["kv"], sp["act"], pl.BlockSpec((None, 1, LANES), lambda p, b: (p, 0, 0))),
        scratch_shapes=[pltpu.VMEM((s, QW), BF16), pltpu.VMEM((s + ATT_BLK, LANES), BF16),
                        pltpu.VMEM((s + ATT_BLK, LANES), BF16), pltpu.VMEM((s, QW), F32),
                        pltpu.VMEM((kv_per, s + ATT_BLK, HEAD_DIM), F32),
                        pltpu.VMEM((kv_per, s + ATT_BLK, HEAD_DIM), F32)],
        compiler_params=_params())(sinks, proj3, proj3, proj3, proj3, o3, dy3, cosf, sinf)


def _merge_fwd_bwd(x, tgt, y_rnn, y_attn, proj, w_r, w_a, w_o, gf):
    t, d = x.shape
    tm = min(t, 256)

    hw = d // 2

    def body(x_ref, t_ref, yr_ref, ya_ref, mr0_ref, mr1_ref, ma0_ref, ma1_ref, wr_ref, wa_ref, wo_ref, gf_ref,
             dmg_ref, dyr_ref, dya_ref, mg_ref, dx2_ref, dx2b_ref, dpr_ref, dpa_ref, loss_ref, dgf_ref):
        i = pl.program_id(0)
        wr = wr_ref[...]
        wa = wa_ref[...]
        wo = wo_ref[...]
        gfv = gf_ref[...]
        pr = _dot(yr_ref[...], wr)
        pa = _dot(ya_ref[...], wa)
        sr = _sigmoid(jnp.concatenate([mr0_ref[...], mr1_ref[...]], axis=1))
        sa = _sigmoid(jnp.concatenate([ma0_ref[...], ma1_ref[...]], axis=1))
        mb = (sr * pr + sa * pa).astype(BF16)
        mg_ref[...] = mb
        x2 = x_ref[...] + _dot(mb, wo)
        r2 = lax.rsqrt(jnp.mean(x2 * x2, axis=-1, keepdims=True) + NORM_EPS)
        nrm = x2 * r2
        err = nrm * gfv - t_ref[...]
        dy = err * (1.0 / d)
        dn = dy * gfv
        dx2 = r2 * (dn - nrm * jnp.mean(dn * nrm, axis=-1, keepdims=True))
        dx2_ref[...] = dx2
        dx2b = dx2.astype(BF16)
        dx2b_ref[...] = dx2b
        dmerged = _dot_nt(dx2b, wo)
        dpr = (dmerged * sr).astype(BF16)
        dpa = (dmerged * sa).astype(BF16)
        dpr_ref[...] = dpr
        dpa_ref[...] = dpa
        dmg_ref[:, 0:d] = (dmerged * pr * (sr * (1.0 - sr))).astype(BF16)
        dmg_ref[:, d:2 * d] = (dmerged * pa * (sa * (1.0 - sa))).astype(BF16)
        dyr_ref[...] = _dot_nt(dpr, wr)
        dya_ref[...] = _dot_nt(dpa, wa)

        @pl.when(i == 0)
        def _():
            loss_ref[...] = jnp.zeros_like(loss_ref)
            dgf_ref[...] = jnp.zeros_like(dgf_ref)

        loss_ref[...] += jnp.full((1, LANES), 0.5 / d, F32) * jnp.sum(err * err)
        dgf_ref[...] += jnp.sum(dy * nrm, axis=0, keepdims=True)

    tile = pl.BlockSpec((tm, d), lambda i: (i, 0))
    wsp = pl.BlockSpec((d, d), lambda i: (0, 0))

    def gate(col_blk):
        return pl.BlockSpec((tm, hw), lambda i: (i, col_blk))

    fb = jax.ShapeDtypeStruct((t, d), BF16)
    ff = jax.ShapeDtypeStruct((t, d), F32)
    return pl.pallas_call(
        body, name="merge_fwd_bwd", grid=(t // tm,),
        out_shape=(jax.ShapeDtypeStruct((t, 2 * d), BF16), ff, ff, fb, ff, fb, fb, fb,
                   jax.ShapeDtypeStruct((1, LANES), F32), jax.ShapeDtypeStruct((1, d), F32)),
        in_specs=[tile, tile, tile, tile] + [gate(OFF_MERGE_R // hw + j) for j in range(4)] + [
            wsp, wsp, wsp, pl.BlockSpec((1, d), lambda i: (0, 0))],
        out_specs=(pl.BlockSpec((tm, 2 * d), lambda i: (i, 0)), tile, tile, tile, tile, tile, tile, tile,
                   pl.BlockSpec((1, LANES), lambda i: (0, 0)), pl.BlockSpec((1, d), lambda i: (0, 0))),
        compiler_params=_params())(x, tgt, y_rnn, y_attn, proj, proj, proj, proj, w_r, w_a, w_o, gf)


def _local_grads(x, tgt, norm_g, w_in_bm, conv_w, conv_b, lru_w_a, lru_b_a, lru_w_x, lru_b_x, lam, sinks,
                 w_r, w_a, w_o, gf):
    bsz, s, d = x.shape
    t = bsz * s
    x2 = x.reshape(t, d)
    h = _rmsnorm_fwd(x2, norm_g)
    proj = _in_proj(h, w_in_bm)
    proj3 = proj.reshape(bsz, s, D_IN)
    h_lru, y_rnn = _lru_fwd(proj3, conv_w, conv_b, lru_w_a, lru_b_a, lru_w_x, lru_b_x, lam)
    cosf, sinf = _rope_tables(s)
    o_attn, y_attn = _attn_fwd(proj3, sinks, cosf, sinf)
    y_rnn2 = y_rnn.reshape(t, d)
    y_attn2 = y_attn.reshape(t, d)
    dmg, dyr, dya, merged, dx2, dx2b, dpr, dpa, loss, dgf = _merge_fwd_bwd(
        x2, tgt.reshape(t, d), y_rnn2, y_attn2, proj, w_r, w_a, w_o, gf)
    gw_o = _grad_w_sq(merged, dx2b, "grad_w_o")
    gw_r = _grad_w_sq(y_rnn2, dpr, "grad_w_rnn_out")
    gw_a = _grad_w_sq(y_attn2, dpa, "grad_w_attn_out")
    dup, dgr, dcw, dcb, dwa, dba, dwx, dbx, dlam = _lru_bwd(
        proj3, h_lru, dyr.reshape(bsz, s, d), conv_w, conv_b, lru_w_a, lru_b_a, lru_w_x, lru_b_x, lam)
    dq, dk, dv, dga, dsink = _attn_bwd(proj3, o_attn, dya.reshape(bsz, s, d), sinks, cosf, sinf)
    dproj = jnp.concatenate([dup, dgr, dq, dk, dv, dga, dmg.reshape(bsz, s, 2 * d)], axis=-1).reshape(t, D_IN)
    gw_in = _grad_w_in(h, dproj)
    dh = _grad_h(dproj, w_in_bm)
    grad_x, dng = _rmsnorm_bwd(x2, dh, dx2, norm_g)
    small = dict(norm_g=dng, conv_w=dcw, conv_b=dcb, lru_w_a=dwa, lru_b_a=dba, lru_w_x=dwx, lru_b_x=dbx,
                 lru_lambda=dlam, attn_sinks=dsink[:, 0, :HEADS_PER_STEP].reshape(1, N_Q_HEADS), final_norm_g=dgf)
    return loss[0, 0], grad_x.reshape(bsz, s, d), gw_in, gw_r, gw_a, gw_o, small


ANY = pl.BlockSpec(memory_space=pl.ANY)


def _mesh_pos():
    return lax.axis_index("x"), lax.axis_index("y"), lax.axis_index("c")


def _chip_peers(x, y, c):
    return [((1 - x, y, c), 2 * (1 - x) + y), ((x, 1 - y, c), 2 * x + (1 - y)),
            ((1 - x, 1 - y, c), 2 * (1 - x) + (1 - y))]


def _remote(src, dst, send_sems, recv_sems, idx, peer):
    return pltpu.make_async_remote_copy(src_ref=src, dst_ref=dst, send_sem=send_sems.at[idx],
                                        recv_sem=recv_sems.at[idx], device_id=peer, device_id_type=MESH)


def _all_gather(bufs, split):
    n = len(bufs)
    n_fwd = 3 * sum(split)

    def body(*refs):
        ins, outs = refs[:n], refs[n:2 * n]
        send_sems, recv_sems, fsend_sems, frecv_sems = refs[2 * n:]
        x, y, c = _mesh_pos()
        me = 2 * x + y
        sib = (x, y, 1 - c)
        peers = _chip_peers(x, y, c)

        def part(ref, slot, t, half):
            if not split[t]:
                return ref.at[slot]
            hr = bufs[t].shape[1] // 2
            return ref.at[slot, pl.ds(pl.multiple_of(half * hr, 8), hr), :]

        sends, recvs = [], []
        for t in range(n):
            for k, (peer, pj) in enumerate(peers):
                src = part(ins[t], me, t, c)
                sends.append(_remote(src, part(outs[t], me, t, c), send_sems, recv_sems, 3 * t + k, peer))
                recvs.append(_remote(src, part(outs[t], pj, t, c), send_sems, recv_sems, 3 * t + k, peer))
        for cp in sends:
            cp.start()
        fwd, fwd_recv = [], []
        for t in range(n):
            for k, (peer, pj) in enumerate(peers):
                recvs[3 * t + k].wait_recv()
                if split[t]:
                    got = part(outs[t], pj, t, c)
                    f = len(fwd)
                    fwd.append(_remote(got, got, fsend_sems, frecv_sems, f, sib))
                    fwd_recv.append(_remote(got, part(outs[t], pj, t, 1 - c), fsend_sems, frecv_sems, f, sib))
                    fwd[-1].start()
        for cp in sends:
            cp.wait_send()
        for snd, rcv in zip(fwd, fwd_recv):
            snd.wait_send()
            rcv.wait_recv()

    return pl.pallas_call(
        body, name="all_gather_weights", out_shape=[jax.ShapeDtypeStruct(a.shape, a.dtype) for a in bufs],
        in_specs=[ANY] * n, out_specs=[ANY] * n, input_output_aliases={t: t for t in range(n)},
        scratch_shapes=[pltpu.SemaphoreType.DMA((3 * n,)), pltpu.SemaphoreType.DMA((3 * n,)),
                        pltpu.SemaphoreType.DMA((n_fwd,)), pltpu.SemaphoreType.DMA((n_fwd,))],
        compiler_params=_params())(*bufs)


def _pair_exchange(bigs, small):
    n = len(bigs)

    def body(*refs):
        ins, outs = refs[:n + 1], refs[n + 1:2 * n + 2]
        send_sems, recv_sems = refs[2 * n + 2:]
        x, y, c = _mesh_pos()
        sib = (x, y, 1 - c)
        cps = []
        for t in range(n):
            hr = bigs[t].shape[1] // 2
            src = ins[t].at[:, pl.ds(pl.multiple_of((1 - c) * hr, 8), hr), :]
            cps.append(_remote(src, outs[t], send_sems, recv_sems, t, sib))
        cps.append(_remote(ins[n], outs[n], send_sems, recv_sems, n, sib))
        for cp in cps:
            cp.start()
        for cp in cps:
            cp.wait()

    out_shape = [jax.ShapeDtypeStruct((a.shape[0], a.shape[1] // 2, a.shape[2]), a.dtype) for a in bigs]
    out_shape.append(jax.ShapeDtypeStruct(small.shape, small.dtype))
    return pl.pallas_call(
        body, name="pair_exchange", out_shape=out_shape, in_specs=[ANY] * (n + 1), out_specs=[ANY] * (n + 1),
        scratch_shapes=[pltpu.SemaphoreType.DMA((n + 1,)), pltpu.SemaphoreType.DMA((n + 1,))],
        compiler_params=_params())(*bigs, small)


def _row_tile(rows, row_bytes, cap_bytes=2 * 1024 * 1024):
    best = None
    for tr in range(8, rows + 1, 8):
        if rows % tr == 0 and tr * row_bytes <= cap_bytes:
            best = tr
    return best if best is not None else rows


def _pair_sum(own, got, c_idx, name):
    nblk, r, w = own.shape
    hr = r // 2
    tr = _row_tile(hr, w * 4)
    steps = hr // tr

    def body(c_ref, own_ref, got_ref, f_ref, b_ref):
        sm = own_ref[...] + got_ref[...]
        f_ref[...] = sm
        b_ref[...] = sm.astype(BF16)

    grid_spec = pltpu.PrefetchScalarGridSpec(
        num_scalar_prefetch=1, grid=(nblk, steps),
        in_specs=[pl.BlockSpec((None, tr, w), lambda j, i, c_ref: (j, c_ref[0] * steps + i, 0)),
                  pl.BlockSpec((None, tr, w), lambda j, i, c_ref: (j, i, 0))],
        out_specs=[pl.BlockSpec((None, tr, w), lambda j, i, c_ref: (j, i, 0)),
                   pl.BlockSpec((None, tr, w), lambda j, i, c_ref: (j, i, 0))])
    return pl.pallas_call(
        body, name=name, grid_spec=grid_spec,
        out_shape=[jax.ShapeDtypeStruct((nblk, hr, w), F32), jax.ShapeDtypeStruct((nblk, hr, w), BF16)],
        compiler_params=_params())(c_idx, own, got)


def _add2(a, b, name):
    r, w = a.shape
    tr = _row_tile(r, w * 4)

    def body(a_ref, b_ref, o_ref):
        o_ref[...] = a_ref[...] + b_ref[...]

    spec = pl.BlockSpec((tr, w), lambda i: (i, 0))
    return pl.pallas_call(body, name=name, grid=(r // tr,), out_shape=jax.ShapeDtypeStruct((r, w), F32),
                          in_specs=[spec, spec], out_specs=spec, compiler_params=_params())(a, b)


def _chip_exchange(bigs_b, small_slots):
    n = len(bigs_b)

    def body(*refs):
        ins, outs = refs[:n + 1], refs[n + 1:2 * n + 2]
        send_sems, recv_sems = refs[2 * n + 2:]
        x, y, c = _mesh_pos()
        me = 2 * x + y
        sends, recvs = [], []
        for t in range(n + 1):
            for k, (peer, pj) in enumerate(_chip_peers(x, y, c)):
                src = ins[t].at[pj] if t < n else ins[t].at[me]
                sends.append(_remote(src, outs[t].at[me], send_sems, recv_sems, 3 * t + k, peer))
                recvs.append(_remote(src, outs[t].at[pj], send_sems, recv_sems, 3 * t + k, peer))
        for cp in sends:
            cp.start()
        for snd, rcv in zip(sends, recvs):
            snd.wait_send()
            rcv.wait_recv()

    out_shape = [jax.ShapeDtypeStruct(a.shape, a.dtype) for a in bigs_b + [small_slots]]
    return pl.pallas_call(
        body, name="chip_exchange", out_shape=out_shape, in_specs=[ANY] * (n + 1), out_specs=[ANY] * (n + 1),
        input_output_aliases={n: n},
        scratch_shapes=[pltpu.SemaphoreType.DMA((3 * n + 3,)), pltpu.SemaphoreType.DMA((3 * n + 3,))],
        compiler_params=_params())(*bigs_b, small_slots)


def _chip_sum(own, got, idx, name):
    _, r, w = own.shape
    tr = _row_tile(r, w * 4)
    steps = r // tr

    def body(idx_ref, own_ref, g1_ref, g2_ref, g3_ref, o_ref):
        o_ref[...] = ((own_ref[...] + g1_ref[...].astype(F32)) + g2_ref[...].astype(F32)) + g3_ref[...].astype(F32)

    def pick(k):
        return pl.BlockSpec((None, tr, w), lambda i, idx_ref: (idx_ref[k], i, 0))

    grid_spec = pltpu.PrefetchScalarGridSpec(
        num_scalar_prefetch=1, grid=(steps,), in_specs=[pick(0), pick(1), pick(2), pick(3)],
        out_specs=pl.BlockSpec((tr, w), lambda i, idx_ref: (idx_ref[4] * steps + i, 0)))
    return pl.pallas_call(body, name=name, grid_spec=grid_spec, out_shape=jax.ShapeDtypeStruct((2 * r, w), F32),
                          compiler_params=_params())(idx, own, got, got, got)


def _tree_sum4(a, idx):
    _, r, w = a.shape
    tr = _row_tile(r, w * 4)
    steps = r // tr

    def body(idx_ref, a_ref, o_ref):
        o_ref[...] = (a_ref[0] + a_ref[1]) + (a_ref[2] + a_ref[3])

    grid_spec = pltpu.PrefetchScalarGridSpec(
        num_scalar_prefetch=1, grid=(steps,),
        in_specs=[pl.BlockSpec((N_CHIPS, tr, w), lambda i, idx_ref: (0, i, 0))],
        out_specs=pl.BlockSpec((tr, w), lambda i, idx_ref: (idx_ref[4] * steps + i, 0)))
    return pl.pallas_call(body, name="small_sum", grid_spec=grid_spec, out_shape=jax.ShapeDtypeStruct((2 * r, w), F32),
                          compiler_params=_params())(idx, a)


def _pair_gather(bufs):
    n = len(bufs)

    def body(*refs):
        ins, outs = refs[:n], refs[n:2 * n]
        send_sems, recv_sems = refs[2 * n:]
        x, y, c = _mesh_pos()
        sib = (x, y, 1 - c)
        sends, recvs = [], []
        for t in range(n):
            hr = bufs[t].shape[0] // 2
            mine = pl.ds(pl.multiple_of(c * hr, 8), hr)
            theirs = pl.ds(pl.multiple_of((1 - c) * hr, 8), hr)
            sends.append(_remote(ins[t].at[mine, :], outs[t].at[mine, :], send_sems, recv_sems, t, sib))
            recvs.append(_remote(ins[t].at[mine, :], outs[t].at[theirs, :], send_sems, recv_sems, t, sib))
        for cp in sends:
            cp.start()
        for snd, rcv in zip(sends, recvs):
            snd.wait_send()
            rcv.wait_recv()

    return pl.pallas_call(
        body, name="pair_gather", out_shape=[jax.ShapeDtypeStruct(a.shape, a.dtype) for a in bufs],
        in_specs=[ANY] * n, out_specs=[ANY] * n, input_output_aliases={t: t for t in range(n)},
        scratch_shapes=[pltpu.SemaphoreType.DMA((n,)), pltpu.SemaphoreType.DMA((n,))],
        compiler_params=_params())(*bufs)


def _reduce_scatter(bigs, small, x, y, c):
    n = len(bigs)
    me = 2 * x + y
    got1 = _pair_exchange(bigs, small)
    c_idx = jnp.reshape(c, (1,)).astype(jnp.int32)
    part = [_pair_sum(bigs[t], got1[t], c_idx, f"pair_sum_{t}") for t in range(n)]
    small_pair = _add2(small, got1[n], "pair_sum_small")
    small_slots = _put_slot(small_pair, N_CHIPS, jnp.stack([me, c]).astype(jnp.int32), PK_HALF, F32, "small_slot")
    got2 = _chip_exchange([p[1] for p in part], small_slots)
    others = [jnp.where(me <= k, k + 1, k) for k in range(N_CHIPS - 1)]
    idx = jnp.stack([me] + others + [c]).astype(jnp.int32)
    bufs = [_chip_sum(part[t][0], got2[t], idx, f"chip_sum_{t}") for t in range(n)]
    bufs.append(_tree_sum4(got2[n], idx))
    return _pair_gather(bufs)


_VEC_NAMES = ("norm_g", "conv_b", "lru_b_a", "lru_b_x", "lru_lambda", "final_norm_g")


def _pack_small(p, conv_full=None):
    rows = [p["lru_w_a"].reshape(PK_WX - PK_WA, LANES), p["lru_w_x"].reshape(PK_VEC - PK_WX, LANES)]
    rows += [p[k].reshape(8, LANES) for k in _VEC_NAMES]
    rows.append(jnp.pad(p["attn_sinks"].reshape(1, N_Q_HEADS), ((0, 7), (0, LANES - N_Q_HEADS))))
    tail = PK_ROWS - PK_CONV
    if conv_full is None:
        rows.append(jnp.zeros((tail, LANES), F32))
    else:
        rows.append(conv_full.reshape(32, LANES))
        rows.append(jnp.zeros((tail - 32, LANES), F32))
    return jnp.concatenate(rows, axis=0)


def _unpack_small(pk, like):
    out = {"lru_w_a": pk[PK_WA:PK_WX].reshape(like["lru_w_a"].shape),
           "lru_w_x": pk[PK_WX:PK_VEC].reshape(like["lru_w_x"].shape)}
    for j, k in enumerate(_VEC_NAMES):
        out[k] = pk[PK_VEC + 8 * j:PK_VEC + 8 * j + 8].reshape(like[k].shape)
    out["attn_sinks"] = pk[PK_SINK:PK_SINK + 1, :N_Q_HEADS].reshape(like["attn_sinks"].shape)
    return out


_WEIGHTS = ("norm_g", "w_in", "conv_w", "conv_b", "lru_w_a", "lru_b_a", "lru_w_x", "lru_b_x", "lru_lambda",
            "attn_sinks", "w_rnn_out", "w_attn_out", "w_o", "final_norm_g")
_SMALL = ("norm_g", "conv_b", "lru_w_a", "lru_b_a", "lru_w_x", "lru_b_x", "lru_lambda", "attn_sinks", "final_norm_g")
_ROW_SHARDED = ("w_rnn_out", "w_attn_out", "w_o")


def kernel(x, norm_g, w_in, conv_w, conv_b, lru_w_a, lru_b_a, lru_w_x, lru_b_x, lru_lambda, attn_sinks, w_rnn_out, w_attn_out, w_o, final_norm_g, loss_target, m_norm_g, m_w_in, m_conv_w, m_conv_b, m_lru_w_a, m_lru_b_a, m_lru_w_x, m_lru_b_x, m_lru_lambda, m_attn_sinks, m_w_rnn_out, m_w_attn_out, m_w_o, m_final_norm_g, v_norm_g, v_w_in, v_conv_w, v_conv_b, v_lru_w_a, v_lru_b_a, v_lru_w_x, v_lru_b_x, v_lru_lambda, v_attn_sinks, v_w_rnn_out, v_w_attn_out, v_w_o, v_final_norm_g):
    w = dict(norm_g=norm_g, w_in=w_in, conv_w=conv_w, conv_b=conv_b, lru_w_a=lru_w_a, lru_b_a=lru_b_a, lru_w_x=lru_w_x,
             lru_b_x=lru_b_x, lru_lambda=lru_lambda, attn_sinks=attn_sinks, w_rnn_out=w_rnn_out, w_attn_out=w_attn_out,
             w_o=w_o, final_norm_g=final_norm_g)
    m = dict(norm_g=m_norm_g, w_in=m_w_in, conv_w=m_conv_w, conv_b=m_conv_b, lru_w_a=m_lru_w_a, lru_b_a=m_lru_b_a,
             lru_w_x=m_lru_w_x, lru_b_x=m_lru_b_x, lru_lambda=m_lru_lambda, attn_sinks=m_attn_sinks,
             w_rnn_out=m_w_rnn_out, w_attn_out=m_w_attn_out, w_o=m_w_o, final_norm_g=m_final_norm_g)
    v = dict(norm_g=v_norm_g, w_in=v_w_in, conv_w=v_conv_w, conv_b=v_conv_b, lru_w_a=v_lru_w_a, lru_b_a=v_lru_b_a,
             lru_w_x=v_lru_w_x, lru_b_x=v_lru_b_x, lru_lambda=v_lru_lambda, attn_sinks=v_attn_sinks,
             w_rnn_out=v_w_rnn_out, w_attn_out=v_w_attn_out, w_o=v_w_o, final_norm_g=v_final_norm_g)
    mx, my, mc = _mesh_pos()
    me = 2 * mx + my
    d = D_MODEL

    slot0 = jnp.stack([me, jnp.zeros_like(me)]).astype(jnp.int32)
    bufs = [_put_slot(w[k][0], N_CHIPS, slot0, w[k].shape[1], BF16, "cast_" + k) for k in ("w_in",) + _ROW_SHARDED]
    bufs.append(_put_slot(w["conv_w"][0], N_CHIPS, slot0, CONV_WIDTH, F32, "slot_conv_w"))
    g_in, g_r, g_a, g_o, g_cw = _all_gather(bufs, [True, True, True, True, False])
    conv_full = g_cw.transpose(1, 0, 2).reshape(CONV_WIDTH, D_RNN)

    loss_local, grad_x, gw_in, gw_r, gw_a, gw_o, gsmall = _local_grads(
        x, loss_target, w["norm_g"], g_in, conv_full, w["conv_b"], w["lru_w_a"][0], w["lru_b_a"], w["lru_w_x"][0],
        w["lru_b_x"], w["lru_lambda"], w["attn_sinks"][0], g_r.reshape(d, d), g_a.reshape(d, d), g_o.reshape(d, d),
        w["final_norm_g"].reshape(1, d))
    loss = lax.psum(loss_local, ("x", "y", "c"))

    gpack = _pack_small(gsmall, gsmall["conv_w"])
    bigs = [gw_in] + [g.reshape(N_CHIPS, ROW_BLK, d) for g in (gw_r, gw_a, gw_o)]
    f_in, f_r, f_a, f_o, spack = _reduce_scatter(bigs, gpack, mx, my, mc)

    grads = _unpack_small(spack, w)
    conv_all = spack[PK_CONV:PK_CONV + 32].reshape(CONV_WIDTH, D_RNN)
    grads["conv_w"] = lax.dynamic_slice_in_dim(conv_all, me * (D_RNN // N_CHIPS), D_RNN // N_CHIPS, axis=1)[None]
    grads["w_in"] = f_in[None]
    grads["w_rnn_out"], grads["w_attn_out"], grads["w_o"] = f_r[None], f_a[None], f_o[None]

    delta, new_m, new_v = {}, {}, {}
    for k in ("w_in",) + _ROW_SHARDED:
        dk, mk, vk = _adamw(w[k][0], grads[k][0], m[k][0], v[k][0], "adamw_" + k)
        delta[k], new_m[k], new_v[k] = dk[None], mk[None], vk[None]
    shp = (2 * CONV_WIDTH, LANES)
    dk, mk, vk = _adamw(w["conv_w"].reshape(shp), grads["conv_w"].reshape(shp), m["conv_w"].reshape(shp),
                        v["conv_w"].reshape(shp), "adamw_conv_w")
    delta["conv_w"], new_m["conv_w"], new_v["conv_w"] = (a.reshape(w["conv_w"].shape) for a in (dk, mk, vk))
    dk, mk, vk = _adamw(_pack_small(w), spack, _pack_small(m), _pack_small(v), "adamw_small")
    for src, dst in ((dk, delta), (mk, new_m), (vk, new_v)):
        dst.update(_unpack_small(src, w))

    return (loss, grad_x, *[grads[k] for k in _WEIGHTS], *[delta[k] for k in _WEIGHTS],
            *[new_m[k] for k in _WEIGHTS], *[new_v[k] for k in _WEIGHTS])
```

```python
import functools
import math

import jax
import jax.numpy as jnp
from jax import lax
from jax.experimental import pallas as pl
from jax.experimental.pallas import tpu as pltpu

F32 = jnp.float32
BF16 = jnp.bfloat16
MESH = pl.DeviceIdType.MESH

D_MODEL = 1024
D_RNN = 1024
N_RNN_BLOCKS = 8
RNN_BLOCK = D_RNN // N_RNN_BLOCKS
CONV_WIDTH = 4
LRU_C = 8.0
HEAD_DIM = 64
N_Q_HEADS = 16
N_KV_HEADS = 4
D_ATTN = N_Q_HEADS * HEAD_DIM
D_KV = N_KV_HEADS * HEAD_DIM
WINDOW = 128
ROPE_DIM = HEAD_DIM // 4
ROPE_THETA = 500000.0
NORM_EPS = 1e-6
OFF_RNN_X = 0
OFF_RNN_G = OFF_RNN_X + D_RNN
OFF_Q = OFF_RNN_G + D_RNN
OFF_K = OFF_Q + D_ATTN
OFF_V = OFF_K + D_KV
OFF_ATTN_G = OFF_V + D_KV
OFF_MERGE_R = OFF_ATTN_G + D_ATTN
OFF_MERGE_A = OFF_MERGE_R + D_MODEL
D_IN = OFF_MERGE_A + D_MODEL

ADAM_LR = 0.001
ADAM_B1 = 0.9
ADAM_B2 = 0.999
ADAM_EPS = 1e-08
ADAM_WD = 0.01
ADAM_STEP = 10

N_CHIPS = 4
W_BLK = D_IN // N_CHIPS
ROW_BLK = D_MODEL // N_CHIPS
LANES = 128
ATT_BLK = 128
VMEM_LIMIT = 56 * 1024 * 1024
NEG_BIG = -1e30
ATTN_SCALE = 1.0 / math.sqrt(HEAD_DIM)

PK_WA = 0
PK_WX = PK_WA + N_RNN_BLOCKS * RNN_BLOCK
PK_VEC = PK_WX + N_RNN_BLOCKS * RNN_BLOCK
PK_SINK = PK_VEC + 6 * 8
PK_CONV = PK_SINK + 8
PK_ROWS = PK_CONV + 32 + 8
PK_HALF = PK_ROWS // 2


def _params(**kw):
    return pltpu.CompilerParams(vmem_limit_bytes=VMEM_LIMIT, **kw)


def _sigmoid(z):
    return 1.0 / (1.0 + jnp.exp(-z))


def _dot(a, b):
    return jnp.dot(a, b, preferred_element_type=F32)


def _dot_nt(a, b):
    return lax.dot_general(a, b, (((1,), (1,)), ((), ())), preferred_element_type=F32)


def _dot_tn(a, b):
    return lax.dot_general(a, b, (((0,), (0,)), ((), ())), preferred_element_type=F32)


def _put_slot(src, n_slots, slot_and_blk, rows, dtype, name):
    _, c = src.shape
    tr = _row_tile(rows, c * 4)
    steps = rows // tr

    def body(idx_ref, s_ref, o_ref):
        o_ref[...] = s_ref[...].astype(dtype)

    grid_spec = pltpu.PrefetchScalarGridSpec(
        num_scalar_prefetch=1, grid=(steps,),
        in_specs=[pl.BlockSpec((tr, c), lambda i, idx_ref: (idx_ref[1] * steps + i, 0))],
        out_specs=pl.BlockSpec((None, tr, c), lambda i, idx_ref: (idx_ref[0], i, 0)))
    return pl.pallas_call(body, name=name, grid_spec=grid_spec,
                          out_shape=jax.ShapeDtypeStruct((n_slots, rows, c), dtype),
                          compiler_params=_params())(slot_and_blk, src)


def _rmsnorm_fwd(x, g):
    t, d = x.shape
    tm = min(t, 512)

    def body(x_ref, g_ref, o_ref):
        xv = x_ref[...]
        r = lax.rsqrt(jnp.mean(xv * xv, axis=-1, keepdims=True) + NORM_EPS)
        o_ref[...] = (xv * r * g_ref[...]).astype(BF16)

    return pl.pallas_call(
        body, name="rmsnorm_fwd", grid=(t // tm,), out_shape=jax.ShapeDtypeStruct((t, d), BF16),
        in_specs=[pl.BlockSpec((tm, d), lambda i: (i, 0)), pl.BlockSpec((1, d), lambda i: (0, 0))],
        out_specs=pl.BlockSpec((tm, d), lambda i: (i, 0)), compiler_params=_params())(x, g)


def _rmsnorm_bwd(x, dh, dx2, g):
    t, d = x.shape
    tm = min(t, 512)

    def body(x_ref, dh_ref, dx2_ref, g_ref, gx_ref, dg_ref):
        i = pl.program_id(0)
        xv = x_ref[...]
        dhv = dh_ref[...]
        r = lax.rsqrt(jnp.mean(xv * xv, axis=-1, keepdims=True) + NORM_EPS)
        nrm = xv * r
        dn = dhv * g_ref[...]
        gx_ref[...] = dx2_ref[...] + r * (dn - nrm * jnp.mean(dn * nrm, axis=-1, keepdims=True))

        @pl.when(i == 0)
        def _():
            dg_ref[...] = jnp.zeros_like(dg_ref)

        dg_ref[...] += jnp.sum(dhv * nrm, axis=0, keepdims=True)

    return pl.pallas_call(
        body, name="rmsnorm_bwd", grid=(t // tm,),
        out_shape=(jax.ShapeDtypeStruct((t, d), F32), jax.ShapeDtypeStruct((1, d), F32)),
        in_specs=[pl.BlockSpec((tm, d), lambda i: (i, 0)), pl.BlockSpec((tm, d), lambda i: (i, 0)),
                  pl.BlockSpec((tm, d), lambda i: (i, 0)), pl.BlockSpec((1, d), lambda i: (0, 0))],
        out_specs=(pl.BlockSpec((tm, d), lambda i: (i, 0)), pl.BlockSpec((1, d), lambda i: (0, 0))),
        compiler_params=_params())(x, dh, dx2, g)


def _adamw(w, g, m, v, name):
    r, c = w.shape
    tr = _row_tile(r, c * 4, 1024 * 1024)
    c1 = 1.0 - ADAM_B1 ** ADAM_STEP
    c2 = 1.0 - ADAM_B2 ** ADAM_STEP

    def body(w_ref, g_ref, m_ref, v_ref, d_ref, nm_ref, nv_ref):
        gv = g_ref[...]
        nm = ADAM_B1 * m_ref[...] + (1.0 - ADAM_B1) * gv
        nv = ADAM_B2 * v_ref[...] + (1.0 - ADAM_B2) * (gv * gv)
        m_hat = nm / c1
        v_hat = nv / c2
        d_ref[...] = -ADAM_LR * (m_hat / (jnp.sqrt(v_hat) + ADAM_EPS) + ADAM_WD * w_ref[...])
        nm_ref[...] = nm
        nv_ref[...] = nv

    spec = pl.BlockSpec((tr, c), lambda i: (i, 0))
    sds = jax.ShapeDtypeStruct((r, c), F32)
    return pl.pallas_call(
        body, name=name, grid=(r // tr,), out_shape=(sds, sds, sds),
        in_specs=[spec, spec, spec, spec], out_specs=(spec, spec, spec), compiler_params=_params())(w, g, m, v)


def _in_proj(h, w_bm):
    t, d = h.shape
    nb, _, wb = w_bm.shape
    tm = min(t, 512)

    def body(h_ref, w_ref, o_ref):
        o_ref[...] = _dot(h_ref[...], w_ref[...])

    return pl.pallas_call(
        body, name="in_proj", grid=(nb, t // tm), out_shape=jax.ShapeDtypeStruct((t, nb * wb), F32),
        in_specs=[pl.BlockSpec((tm, d), lambda j, i: (i, 0)), pl.BlockSpec((None, d, wb), lambda j, i: (j, 0, 0))],
        out_specs=pl.BlockSpec((tm, wb), lambda j, i: (i, j)), compiler_params=_params())(h, w_bm)


def _grad_w_in(h, dproj):
    t, d = h.shape
    nb = N_CHIPS
    tk = min(t, 512)
    nk = t // tk

    def body(h_ref, dp_ref, o_ref, acc_ref):
        k = pl.program_id(1)

        @pl.when(k == 0)
        def _():
            acc_ref[...] = jnp.zeros_like(acc_ref)

        acc_ref[...] += _dot_tn(h_ref[...], dp_ref[...])

        @pl.when(k == nk - 1)
        def _():
            o_ref[...] = acc_ref[...]

    return pl.pallas_call(
        body, name="grad_w_in", grid=(nb, nk), out_shape=jax.ShapeDtypeStruct((nb, d, W_BLK), F32),
        in_specs=[pl.BlockSpec((tk, d), lambda j, k: (k, 0)), pl.BlockSpec((tk, W_BLK), lambda j, k: (k, j))],
        out_specs=pl.BlockSpec((None, d, W_BLK), lambda j, k: (j, 0, 0)),
        scratch_shapes=[pltpu.VMEM((d, W_BLK), F32)], compiler_params=_params())(h, dproj)


def _grad_h(dproj, w_bm):
    t = dproj.shape[0]
    nb, d, wb = w_bm.shape
    tm = min(t, 1024)

    def body(dp_ref, w_ref, o_ref, acc_ref):
        k = pl.program_id(1)

        @pl.when(k == 0)
        def _():
            acc_ref[...] = jnp.zeros_like(acc_ref)

        acc_ref[...] += _dot_nt(dp_ref[...], w_ref[...])

        @pl.when(k == nb - 1)
        def _():
            o_ref[...] = acc_ref[...]

    return pl.pallas_call(
        body, name="grad_h", grid=(t // tm, nb), out_shape=jax.ShapeDtypeStruct((t, d), F32),
        in_specs=[pl.BlockSpec((tm, wb), lambda i, k: (i, k)), pl.BlockSpec((None, d, wb), lambda i, k: (k, 0, 0))],
        out_specs=pl.BlockSpec((tm, d), lambda i, k: (i, 0)),
        scratch_shapes=[pltpu.VMEM((tm, d), F32)], compiler_params=_params())(dproj, w_bm)


def _grad_w_sq(a, b, name):
    t, d = a.shape
    n = b.shape[1]
    tk = min(t, 512)
    nk = t // tk

    def body(a_ref, b_ref, o_ref):
        k = pl.program_id(0)

        @pl.when(k == 0)
        def _():
            o_ref[...] = jnp.zeros_like(o_ref)

        o_ref[...] += _dot_tn(a_ref[...], b_ref[...])

    return pl.pallas_call(
        body, name=name, grid=(nk,), out_shape=jax.ShapeDtypeStruct((d, n), F32),
        in_specs=[pl.BlockSpec((tk, d), lambda k: (k, 0)), pl.BlockSpec((tk, n), lambda k: (k, 0))],
        out_specs=pl.BlockSpec((d, n), lambda k: (0, 0)), compiler_params=_params())(a, b)


def _shift_down(v, d, fill):
    n = v.shape[0]
    if d % 8 == 0:
        return jnp.concatenate([jnp.full((d,) + v.shape[1:], fill, v.dtype), v[: n - d]], axis=0)
    row = lax.broadcasted_iota(jnp.int32, v.shape, 0)
    return jnp.where(row >= d, pltpu.roll(v, d, axis=0), fill)


def _shift_up(v, d, fill):
    n = v.shape[0]
    if d % 8 == 0:
        return jnp.concatenate([v[d:], jnp.full((d,) + v.shape[1:], fill, v.dtype)], axis=0)
    row = lax.broadcasted_iota(jnp.int32, v.shape, 0)
    return jnp.where(row < n - d, pltpu.roll(v, n - d, axis=0), fill)


def _scan(a, b, shift):
    n = a.shape[0]
    d = 1
    while d < n:
        b = a * shift(b, d, 0.0) + b
        if 2 * d < n:
            a = a * shift(a, d, 1.0)
        d *= 2
    return b


def _neg_expm1(y):
    series = -y * (1.0 + y * (1.0 / 2.0) * (1.0 + y * (1.0 / 3.0) * (1.0 + y * (1.0 / 4.0) * (
        1.0 + y * (1.0 / 5.0) * (1.0 + y * (1.0 / 6.0) * (1.0 + y * (1.0 / 7.0)))))))
    return jnp.where(y > -0.25, series, 1.0 - jnp.exp(y))


def _softplus(z):
    e = jnp.exp(-jnp.abs(z))
    w = 1.0 + e
    log1p = jnp.where(w == 1.0, e, jnp.log(w) * (e / jnp.where(w == 1.0, 1.0, w - 1.0)))
    return jnp.maximum(z, 0.0) + log1p


def _conv(up, cw, cb):
    out = cb + cw[CONV_WIDTH - 1:CONV_WIDTH, :] * up
    for j in range(CONV_WIDTH - 1):
        out = out + cw[j:j + 1, :] * _shift_down(up, CONV_WIDTH - 1 - j, 0.0)
    return out


def _lru_gates(u, wa_ref, ba_ref, wx_ref, bx_ref, lam_ref):
    ub = u.astype(BF16)
    r = _sigmoid(_dot(ub, wa_ref[...].astype(BF16)) + ba_ref[...])
    i = _sigmoid(_dot(ub, wx_ref[...].astype(BF16)) + bx_ref[...])
    sp = _softplus(-lam_ref[...])
    log_a = (-LRU_C) * r * sp
    a = jnp.exp(log_a)
    mult = jnp.sqrt(_neg_expm1(2.0 * log_a))
    return r, i, sp, a, mult


def _lru_specs(s):
    cb = RNN_BLOCK
    vec = pl.BlockSpec((1, cb), lambda n, b: (0, n))
    return dict(
        up=pl.BlockSpec((None, s, cb), lambda n, b: (b, 0, OFF_RNN_X // cb + n)),
        gr=pl.BlockSpec((None, s, cb), lambda n, b: (b, 0, OFF_RNN_G // cb + n)),
        act=pl.BlockSpec((None, s, cb), lambda n, b: (b, 0, n)),
        cw=pl.BlockSpec((CONV_WIDTH, cb), lambda n, b: (0, n)),
        vec=vec,
        wblk=pl.BlockSpec((None, cb, cb), lambda n, b: (n, 0, 0)),
    )


def _lru_fwd(proj3, cw, cb, wa, ba, wx, bx, lam):
    bsz, s, _ = proj3.shape
    sp = _lru_specs(s)

    def body(up_ref, gr_ref, cw_ref, cb_ref, wa_ref, ba_ref, wx_ref, bx_ref, lam_ref, h_ref, y_ref):
        u = _conv(up_ref[...], cw_ref[...], cb_ref[...])
        _, i, _, a, mult = _lru_gates(u, wa_ref, ba_ref, wx_ref, bx_ref, lam_ref)
        h = _scan(a, mult * (i * u), _shift_down)
        h_ref[...] = h
        g = gr_ref[...]
        y_ref[...] = (h * (g * _sigmoid(g))).astype(BF16)

    return pl.pallas_call(
        body, name="lru_fwd", grid=(N_RNN_BLOCKS, bsz),
        out_shape=(jax.ShapeDtypeStruct((bsz, s, D_RNN), F32), jax.ShapeDtypeStruct((bsz, s, D_RNN), BF16)),
        in_specs=[sp["up"], sp["gr"], sp["cw"], sp["vec"], sp["wblk"], sp["vec"], sp["wblk"], sp["vec"], sp["vec"]],
        out_specs=(sp["act"], sp["act"]), compiler_params=_params())(proj3, proj3, cw, cb, wa, ba, wx, bx, lam)


def _lru_bwd(proj3, h3, dy3, cw, cb, wa, ba, wx, bx, lam):
    bsz, s, _ = proj3.shape
    sp = _lru_specs(s)

    def body(up_ref, gr_ref, h_ref, dy_ref, cw_ref, cb_ref, wa_ref, ba_ref, wx_ref, bx_ref, lam_ref,
             dup_ref, dgr_ref, dcw_ref, dcb_ref, dwa_ref, dba_ref, dwx_ref, dbx_ref, dlam_ref):
        b = pl.program_id(1)
        up = up_ref[...]
        cwv = cw_ref[...]
        u = _conv(up, cwv, cb_ref[...])
        r, i, spv, a, mult = _lru_gates(u, wa_ref, ba_ref, wx_ref, bx_ref, lam_ref)
        h = h_ref[...]
        g = gr_ref[...]
        dy = dy_ref[...]
        sg = _sigmoid(g)
        dgr_ref[...] = (dy * h * (sg * (1.0 + g * (1.0 - sg)))).astype(BF16)
        dh = dy * (g * sg)
        adj = _scan(_shift_up(a, 1, 0.0), dh, _shift_up)
        da = adj * _shift_down(h, 1, 0.0)
        dmult = adj * (i * u)
        di = adj * mult * u
        du = adj * mult * i
        dla = da * a - dmult * (a * a) / mult
        dr = dla * ((-LRU_C) * spv)
        dsp = jnp.sum(dla * ((-LRU_C) * r), axis=0, keepdims=True)
        dza = dr * r * (1.0 - r)
        dzx = di * i * (1.0 - i)
        ub = u.astype(BF16)
        dzab = dza.astype(BF16)
        dzxb = dzx.astype(BF16)
        du = du + _dot_nt(dzab, wa_ref[...].astype(BF16)) + _dot_nt(dzxb, wx_ref[...].astype(BF16))
        dup = cwv[CONV_WIDTH - 1:CONV_WIDTH, :] * du
        for j in range(CONV_WIDTH - 1):
            dup = dup + cwv[j:j + 1, :] * _shift_up(du, CONV_WIDTH - 1 - j, 0.0)
        dup_ref[...] = dup.astype(BF16)

        @pl.when(b == 0)
        def _():
            for ref in (dcw_ref, dcb_ref, dwa_ref, dba_ref, dwx_ref, dbx_ref, dlam_ref):
                ref[...] = jnp.zeros_like(ref)

        rows = [jnp.sum(du * _shift_down(up, CONV_WIDTH - 1 - j, 0.0), axis=0, keepdims=True)
                for j in range(CONV_WIDTH - 1)]
        rows.append(jnp.sum(du * up, axis=0, keepdims=True))
        dcw_ref[...] += jnp.concatenate(rows, axis=0)
        dcb_ref[...] += jnp.sum(du, axis=0, keepdims=True)
        dwa_ref[...] += _dot_tn(ub, dzab)
        dba_ref[...] += jnp.sum(dza, axis=0, keepdims=True)
        dwx_ref[...] += _dot_tn(ub, dzxb)
        dbx_ref[...] += jnp.sum(dzx, axis=0, keepdims=True)
        dlam_ref[...] += dsp * (-_sigmoid(-lam_ref[...]))

    act_b = jax.ShapeDtypeStruct((bsz, s, D_RNN), BF16)
    vec = jax.ShapeDtypeStruct((1, D_RNN), F32)
    wsd = jax.ShapeDtypeStruct((N_RNN_BLOCKS, RNN_BLOCK, RNN_BLOCK), F32)
    return pl.pallas_call(
        body, name="lru_bwd", grid=(N_RNN_BLOCKS, bsz),
        out_shape=(act_b, act_b, jax.ShapeDtypeStruct((CONV_WIDTH, D_RNN), F32), vec, wsd, vec, wsd, vec, vec),
        in_specs=[sp["up"], sp["gr"], sp["act"], sp["act"], sp["cw"], sp["vec"], sp["wblk"], sp["vec"],
                  sp["wblk"], sp["vec"], sp["vec"]],
        out_specs=(sp["act"], sp["act"], sp["cw"], sp["vec"], sp["wblk"], sp["vec"], sp["wblk"], sp["vec"], sp["vec"]),
        compiler_params=_params())(proj3, proj3, h3, dy3, cw, cb, wa, ba, wx, bx, lam)


def _rope_tables(s):
    half = ROPE_DIM // 2
    pos = jnp.arange(s, dtype=F32)
    inv_freq = ROPE_THETA ** (-jnp.arange(0, ROPE_DIM, 2, dtype=F32) / ROPE_DIM)
    ang = pos[:, None] * inv_freq[None, :]
    cos, sin = jnp.cos(ang), jnp.sin(ang)
    rest = HEAD_DIM - ROPE_DIM
    cos64 = jnp.concatenate([cos, cos, jnp.ones((s, rest), F32)], axis=1)
    sin64 = jnp.concatenate([-sin, sin, jnp.zeros((s, rest), F32)], axis=1)
    assert half * 2 == ROPE_DIM
    return jnp.tile(cos64, (1, LANES // HEAD_DIM)), jnp.tile(sin64, (1, LANES // HEAD_DIM))


def _swap_rot_halves(v):
    half = ROPE_DIM // 2
    lane = lax.broadcasted_iota(jnp.int32, v.shape, 1) % HEAD_DIM
    second = jnp.where(lane < ROPE_DIM, pltpu.roll(v, half, axis=1), 0.0)
    return jnp.where(lane < half, pltpu.roll(v, LANES - half, axis=1), second)


def _rope(v, cos, sin):
    tiles = []
    for t in range(v.shape[1] // LANES):
        vt = v[:, t * LANES:(t + 1) * LANES]
        tiles.append(vt * cos + _swap_rot_halves(vt) * sin)
    return tiles[0] if len(tiles) == 1 else jnp.concatenate(tiles, axis=1)


def _unrope(v, cos, sin):
    tiles = []
    for t in range(v.shape[1] // LANES):
        vt = v[:, t * LANES:(t + 1) * LANES]
        tiles.append(vt * cos + _swap_rot_halves(vt * sin))
    return tiles[0] if len(tiles) == 1 else jnp.concatenate(tiles, axis=1)


HEADS_PER_STEP = 8
QW = HEADS_PER_STEP * HEAD_DIM
N_PAIRS = N_Q_HEADS // HEADS_PER_STEP
Q_PER_KV = N_Q_HEADS // N_KV_HEADS
KV_PER_STEP = HEADS_PER_STEP // Q_PER_KV


QT_COLS = Q_PER_KV * ATT_BLK


def _attn_scratch(s, with_vt):
    nb = s // ATT_BLK
    pad = s + ATT_BLK
    shapes = [pltpu.VMEM((nb, LANES, QT_COLS), BF16),
              pltpu.VMEM((KV_PER_STEP, pad, LANES), BF16),
              pltpu.VMEM((KV_PER_STEP, pad, LANES), BF16)]
    if with_vt:
        shapes.append(pltpu.VMEM((LANES, pad), BF16))
    return shapes


def _attn_specs(s, order):
    def mk(width, base, **kw):
        if order == "bp":
            return pl.BlockSpec((None, s, width), lambda b, p: (b, 0, base + p), **kw)
        return pl.BlockSpec((None, s, width), lambda p, b: (b, 0, base + p), **kw)
    one = dict(pipeline_mode=pl.Buffered(1))
    tbl = pl.BlockSpec((s, LANES), lambda *_: (0, 0))
    return dict(q=mk(QW, OFF_Q // QW), k=mk(LANES, OFF_K // LANES), v=mk(LANES, OFF_V // LANES),
                g=mk(QW, OFF_ATTN_G // QW), act=mk(QW, 0), kv=mk(LANES, 0), tbl=tbl,
                q1=mk(QW, OFF_Q // QW, **one), g1=mk(QW, OFF_ATTN_G // QW, **one), act1=mk(QW, 0, **one),
                smem=pl.BlockSpec(memory_space=pltpu.SMEM))


def _to_qt(blk):
    rows = []
    for j in range(KV_PER_STEP):
        cols = []
        for tt in range(2):
            t = 2 * j + tt
            tr = blk[:, t * LANES:(t + 1) * LANES].T
            cols += [tr[0:HEAD_DIM, :], tr[HEAD_DIM:, :]]
        rows.append(jnp.concatenate(cols, axis=1))
    return jnp.concatenate(rows, axis=0)


def _from_qt(xt):
    tiles = []
    for j in range(KV_PER_STEP):
        for tt in range(2):
            g0 = 2 * tt
            pair = jnp.concatenate([xt[j * HEAD_DIM:(j + 1) * HEAD_DIM, (g0 + i) * ATT_BLK:(g0 + i + 1) * ATT_BLK]
                                    for i in range(2)], axis=0)
            tiles.append(pair.T)
    return jnp.concatenate(tiles, axis=1)


def _attn_prep(q_ref, k_ref, v_ref, cos_ref, sin_ref, qt_sc, km_sc, vm_sc, t_sc, transposed, nb):
    zeros = jnp.zeros((ATT_BLK, LANES), BF16)
    for j in range(KV_PER_STEP):
        km_sc[j, 0:ATT_BLK, :] = zeros
        vm_sc[j, 0:ATT_BLK, :] = zeros
    t_sc[:, 0:ATT_BLK] = zeros
    head_of_lane = lax.broadcasted_iota(jnp.int32, (ATT_BLK, LANES), 1) // HEAD_DIM

    def prep(n, carry):
        r0 = pl.multiple_of(n * ATT_BLK, ATT_BLK)
        cs = cos_ref[pl.ds(r0, ATT_BLK), :]
        sn = sin_ref[pl.ds(r0, ATT_BLK), :]
        qt_sc[n] = _to_qt(_rope(q_ref[pl.ds(r0, ATT_BLK), :], cs, sn) * ATTN_SCALE).astype(BF16)
        k = _rope(k_ref[pl.ds(r0, ATT_BLK), :], cs, sn)
        v = v_ref[pl.ds(r0, ATT_BLK), :]
        for j in range(KV_PER_STEP):
            km_sc[j, pl.ds(r0 + ATT_BLK, ATT_BLK), :] = jnp.where(head_of_lane == j, k, 0.0).astype(BF16)
            vm_sc[j, pl.ds(r0 + ATT_BLK, ATT_BLK), :] = jnp.where(head_of_lane == j, v, 0.0).astype(BF16)
        t_sc[:, pl.ds(r0 + ATT_BLK, ATT_BLK)] = (k if transposed == "k" else v).T.astype(BF16)
        return carry

    lax.fori_loop(0, nb, prep, 0)


def _band_mask_t(n):
    shape = (2 * ATT_BLK, QT_COLS)
    key = lax.broadcasted_iota(jnp.int32, shape, 0)
    qry = lax.broadcasted_iota(jnp.int32, shape, 1) % ATT_BLK
    lo = jnp.where(n == 0, ATT_BLK, 0)
    return (key > qry) & (key <= qry + WINDOW) & (key >= lo)


def _sink_row(sink_ref, first):
    return jnp.concatenate([jnp.full((1, ATT_BLK), sink_ref[first + g], F32) for g in range(Q_PER_KV)], axis=1)


def _softmax_cols(scores_t, valid, sink):
    sc = jnp.where(valid, scores_t, NEG_BIG)
    m = jnp.maximum(jnp.max(sc, axis=0, keepdims=True), sink)
    e = jnp.exp(sc - m)
    es = jnp.exp(sink - m)
    inv = 1.0 / (jnp.sum(e, axis=0, keepdims=True) + es)
    return e * inv, es * inv


def _attn_fwd(proj3, sinks, cosf, sinf):
    bsz, s, _ = proj3.shape
    nb = s // ATT_BLK
    sp = _attn_specs(s, "bp")

    def body(sink_ref, q_ref, k_ref, v_ref, g_ref, cos_ref, sin_ref, o_ref, y_ref, qt_sc, km_sc, vm_sc, vt_sc):
        p = pl.program_id(1)
        _attn_prep(q_ref, k_ref, v_ref, cos_ref, sin_ref, qt_sc, km_sc, vm_sc, vt_sc, "v", nb)
        kv_row = lax.broadcasted_iota(jnp.int32, (LANES, QT_COLS), 0) // HEAD_DIM

        def blk(n, carry):
            r0 = pl.multiple_of(n * ATT_BLK, ATT_BLK)
            valid = _band_mask_t(n)
            rq = qt_sc[n]
            vt = vt_sc[:, pl.ds(r0, 2 * ATT_BLK)]
            ots = []
            for j in range(KV_PER_STEP):
                st = _dot(km_sc[j, pl.ds(r0, 2 * ATT_BLK), :], rq)
                pt, _ = _softmax_cols(st, valid, _sink_row(sink_ref, p * HEADS_PER_STEP + j * Q_PER_KV))
                ots.append(_dot(vt, pt.astype(BF16)))
            o = _from_qt(jnp.where(kv_row == 0, ots[0], ots[1]))
            o_ref[pl.ds(r0, ATT_BLK), :] = o
            g = g_ref[pl.ds(r0, ATT_BLK), :]
            y_ref[pl.ds(r0, ATT_BLK), :] = (o * (g * _sigmoid(g))).astype(BF16)
            return carry

        lax.fori_loop(0, nb, blk, 0)

    return pl.pallas_call(
        body, name="attn_fwd", grid=(bsz, N_PAIRS),
        out_shape=(jax.ShapeDtypeStruct((bsz, s, D_ATTN), F32), jax.ShapeDtypeStruct((bsz, s, D_ATTN), BF16)),
        in_specs=[sp["smem"], sp["q"], sp["k"], sp["v"], sp["g"], sp["tbl"], sp["tbl"]],
        out_specs=(sp["act"], sp["act"]),
        scratch_shapes=_attn_scratch(s, True),
        compiler_params=_params())(sinks, proj3, proj3, proj3, proj3, cosf, sinf)


def _attn_bwd(proj3, o3, dy3, sinks, cosf, sinf):
    bsz, s, _ = proj3.shape
    nb = s // ATT_BLK
    sp = _attn_specs(s, "pb")

    def body(sink_ref, q_ref, k_ref, v_ref, g_ref, o_ref, dy_ref, cos_ref, sin_ref,
             dq_ref, dk_ref, dv_ref, dg_ref, ds_ref, qt_sc, km_sc, vm_sc, kt_sc, dot_sc, dqt_sc, dk_sc, dv_sc):
        p = pl.program_id(0)
        b = pl.program_id(1)
        _attn_prep(q_ref, k_ref, v_ref, cos_ref, sin_ref, qt_sc, km_sc, vm_sc, kt_sc, "k", nb)
        dk_sc[...] = jnp.zeros_like(dk_sc)
        dv_sc[...] = jnp.zeros_like(dv_sc)

        def gate(n, carry):
            r0 = pl.multiple_of(n * ATT_BLK, ATT_BLK)
            g = g_ref[pl.ds(r0, ATT_BLK), :]
            dy = dy_ref[pl.ds(r0, ATT_BLK), :]
            sg = _sigmoid(g)
            dg_ref[pl.ds(r0, ATT_BLK), :] = (dy * o_ref[pl.ds(r0, ATT_BLK), :] * (sg * (1.0 + g * (1.0 - sg)))).astype(BF16)
            dot_sc[n] = _to_qt(dy * (g * sg)).astype(BF16)
            return carry

        lax.fori_loop(0, nb, gate, 0)
        kv_lane = lax.broadcasted_iota(jnp.int32, (2 * ATT_BLK, LANES), 1) // HEAD_DIM
        kv_row = lax.broadcasted_iota(jnp.int32, (LANES, QT_COLS), 0) // HEAD_DIM

        def blk(n, acc):
            r0 = pl.multiple_of(n * ATT_BLK, ATT_BLK)
            valid = _band_mask_t(n)
            rq = qt_sc[n]
            rd = dot_sc[n]
            kt = kt_sc[:, pl.ds(r0, 2 * ATT_BLK)]
            dvs, dks, dqs, new_acc = [], [], [], []
            for j in range(KV_PER_STEP):
                st = _dot(km_sc[j, pl.ds(r0, 2 * ATT_BLK), :], rq)
                pt, ps = _softmax_cols(st, valid, _sink_row(sink_ref, p * HEADS_PER_STEP + j * Q_PER_KV))
                dpt = _dot(vm_sc[j, pl.ds(r0, 2 * ATT_BLK), :], rd)
                delta = jnp.sum(pt * dpt, axis=0, keepdims=True)
                dst = (pt * (dpt - delta)).astype(BF16)
                new_acc.append(acc[j] + ps * delta)
                dvs.append(_dot_nt(pt.astype(BF16), rd))
                dks.append(_dot_nt(dst, rq))
                dqs.append(_dot(kt, dst))
            dv_sc[pl.ds(r0, 2 * ATT_BLK), :] += jnp.where(kv_lane == 0, dvs[0], dvs[1])
            dk_sc[pl.ds(r0, 2 * ATT_BLK), :] += jnp.where(kv_lane == 0, dks[0], dks[1])
            dqt_sc[n] = jnp.where(kv_row == 0, dqs[0], dqs[1]) * ATTN_SCALE
            return tuple(new_acc)

        acc = lax.fori_loop(0, nb, blk, tuple(jnp.zeros((1, QT_COLS), F32) for _ in range(KV_PER_STEP)))
        lane1 = lax.broadcasted_iota(jnp.int32, (1, LANES), 1)
        dsink = jnp.zeros((1, LANES), F32)
        for j in range(KV_PER_STEP):
            for i in range(Q_PER_KV):
                part = jnp.sum(acc[j][:, i * ATT_BLK:(i + 1) * ATT_BLK], axis=1, keepdims=True)
                dsink = dsink - jnp.where(lane1 == j * Q_PER_KV + i, part, 0.0)

        @pl.when(b == 0)
        def _():
            ds_ref[...] = jnp.zeros_like(ds_ref)

        ds_ref[...] += dsink

        def post(n, carry):
            r0 = pl.multiple_of(n * ATT_BLK, ATT_BLK)
            cs = cos_ref[pl.ds(r0, ATT_BLK), :]
            sn = sin_ref[pl.ds(r0, ATT_BLK), :]
            dq_ref[pl.ds(r0, ATT_BLK), :] = _unrope(_from_qt(dqt_sc[n]), cs, sn).astype(BF16)
            dk_ref[pl.ds(r0, ATT_BLK), :] = _unrope(dk_sc[pl.ds(r0 + ATT_BLK, ATT_BLK), :], cs, sn).astype(BF16)
            dv_ref[pl.ds(r0, ATT_BLK), :] = dv_sc[pl.ds(r0 + ATT_BLK, ATT_BLK), :].astype(BF16)
            return carry

        lax.fori_loop(0, nb, post, 0)

    act = jax.ShapeDtypeStruct((bsz, s, D_ATTN), BF16)
    kvs = jax.ShapeDtypeStruct((bsz, s, D_KV), BF16)
    return pl.pallas_call(
        body, name="attn_bwd", grid=(N_PAIRS, bsz),
        out_shape=(act, kvs, kvs, act, jax.ShapeDtypeStruct((N_PAIRS, 1, LANES), F32)),
        in_specs=[sp["smem"], sp["q1"], sp["k"], sp["v"], sp["g1"], sp["act1"], sp["act1"], sp["tbl"], sp["tbl"]],
        out_specs=(sp["act"], sp["kv"], sp["kv"], sp["act"], pl.BlockSpec((None, 1, LANES), lambda p, b: (p, 0, 0))),
        scratch_shapes=_attn_scratch(s, True) + [pltpu.VMEM((nb, LANES, QT_COLS), BF16),
                                                 pltpu.VMEM((nb, LANES, QT_COLS), F32),
                                                 pltpu.VMEM((s + ATT_BLK, LANES), F32),
                                                 pltpu.VMEM((s + ATT_BLK, LANES), F32)],
        compiler_params=_params())(sinks, proj3, proj3, proj3, proj3, o3, dy3, cosf, sinf)


def _merge_fwd_bwd(x, tgt, y_rnn, y_attn, proj, w_r, w_a, w_o, gf):
    t, d = x.shape
    tm = min(t, 256)

    hw = d // 2

    def body(x_ref, t_ref, yr_ref, ya_ref, mr0_ref, mr1_ref, ma0_ref, ma1_ref, wr_ref, wa_ref, wo_ref, gf_ref,
             dmg_ref, dyr_ref, dya_ref, mg_ref, dx2_ref, dx2b_ref, dpr_ref, dpa_ref, loss_ref, dgf_ref):
        i = pl.program_id(0)
        wr = wr_ref[...]
        wa = wa_ref[...]
        wo = wo_ref[...]
        gfv = gf_ref[...]
        pr = _dot(yr_ref[...], wr)
        pa = _dot(ya_ref[...], wa)
        sr = _sigmoid(jnp.concatenate([mr0_ref[...], mr1_ref[...]], axis=1))
        sa = _sigmoid(jnp.concatenate([ma0_ref[...], ma1_ref[...]], axis=1))
        mb = (sr * pr + sa * pa).astype(BF16)
        mg_ref[...] = mb
        x2 = x_ref[...] + _dot(mb, wo)
        r2 = lax.rsqrt(jnp.mean(x2 * x2, axis=-1, keepdims=True) + NORM_EPS)
        nrm = x2 * r2
        err = nrm * gfv - t_ref[...]
        dy = err * (1.0 / d)
        dn = dy * gfv
        dx2 = r2 * (dn - nrm * jnp.mean(dn * nrm, axis=-1, keepdims=True))
        dx2_ref[...] = dx2
        dx2b = dx2.astype(BF16)
        dx2b_ref[...] = dx2b
        dmerged = _dot_nt(dx2b, wo)
        dpr = (dmerged * sr).astype(BF16)
        dpa = (dmerged * sa).astype(BF16)
        dpr_ref[...] = dpr
        dpa_ref[...] = dpa
        dmg_ref[:, 0:d] = (dmerged * pr * (sr * (1.0 - sr))).astype(BF16)
        dmg_ref[:, d:2 * d] = (dmerged * pa * (sa * (1.0 - sa))).astype(BF16)
        dyr_ref[...] = _dot_nt(dpr, wr)
        dya_ref[...] = _dot_nt(dpa, wa)

        @pl.when(i == 0)
        def _():
            loss_ref[...] = jnp.zeros_like(loss_ref)
            dgf_ref[...] = jnp.zeros_like(dgf_ref)

        loss_ref[...] += jnp.full((1, LANES), 0.5 / d, F32) * jnp.sum(err * err)
        dgf_ref[...] += jnp.sum(dy * nrm, axis=0, keepdims=True)

    tile = pl.BlockSpec((tm, d), lambda i: (i, 0))
    wsp = pl.BlockSpec((d, d), lambda i: (0, 0))

    def gate(col_blk):
        return pl.BlockSpec((tm, hw), lambda i: (i, col_blk))

    fb = jax.ShapeDtypeStruct((t, d), BF16)
    ff = jax.ShapeDtypeStruct((t, d), F32)
    return pl.pallas_call(
        body, name="merge_fwd_bwd", grid=(t // tm,),
        out_shape=(jax.ShapeDtypeStruct((t, 2 * d), BF16), ff, ff, fb, ff, fb, fb, fb,
                   jax.ShapeDtypeStruct((1, LANES), F32), jax.ShapeDtypeStruct((1, d), F32)),
        in_specs=[tile, tile, tile, tile] + [gate(OFF_MERGE_R // hw + j) for j in range(4)] + [
            wsp, wsp, wsp, pl.BlockSpec((1, d), lambda i: (0, 0))],
        out_specs=(pl.BlockSpec((tm, 2 * d), lambda i: (i, 0)), tile, tile, tile, tile, tile, tile, tile,
                   pl.BlockSpec((1, LANES), lambda i: (0, 0)), pl.BlockSpec((1, d), lambda i: (0, 0))),
        compiler_params=_params())(x, tgt, y_rnn, y_attn, proj, proj, proj, proj, w_r, w_a, w_o, gf)


def _local_grads(x, tgt, norm_g, w_in_bm, conv_w, conv_b, lru_w_a, lru_b_a, lru_w_x, lru_b_x, lam, sinks,
                 w_r, w_a, w_o, gf):
    bsz, s, d = x.shape
    t = bsz * s
    x2 = x.reshape(t, d)
    h = _rmsnorm_fwd(x2, norm_g)
    proj = _in_proj(h, w_in_bm)
    proj3 = proj.reshape(bsz, s, D_IN)
    h_lru, y_rnn = _lru_fwd(proj3, conv_w, conv_b, lru_w_a, lru_b_a, lru_w_x, lru_b_x, lam)
    cosf, sinf = _rope_tables(s)
    o_attn, y_attn = _attn_fwd(proj3, sinks, cosf, sinf)
    y_rnn2 = y_rnn.reshape(t, d)
    y_attn2 = y_attn.reshape(t, d)
    dmg, dyr, dya, merged, dx2, dx2b, dpr, dpa, loss, dgf = _merge_fwd_bwd(
        x2, tgt.reshape(t, d), y_rnn2, y_attn2, proj, w_r, w_a, w_o, gf)
    gw_o = _grad_w_sq(merged, dx2b, "grad_w_o")
    gw_r = _grad_w_sq(y_rnn2, dpr, "grad_w_rnn_out")
    gw_a = _grad_w_sq(y_attn2, dpa, "grad_w_attn_out")
    dup, dgr, dcw, dcb, dwa, dba, dwx, dbx, dlam = _lru_bwd(
        proj3, h_lru, dyr.reshape(bsz, s, d), conv_w, conv_b, lru_w_a, lru_b_a, lru_w_x, lru_b_x, lam)
    dq, dk, dv, dga, dsink = _attn_bwd(proj3, o_attn, dya.reshape(bsz, s, d), sinks, cosf, sinf)
    dproj = jnp.concatenate([dup, dgr, dq, dk, dv, dga, dmg.reshape(bsz, s, 2 * d)], axis=-1).reshape(t, D_IN)
    gw_in = _grad_w_in(h, dproj)
    dh = _grad_h(dproj, w_in_bm)
    grad_x, dng = _rmsnorm_bwd(x2, dh, dx2, norm_g)
    small = dict(norm_g=dng, conv_w=dcw, conv_b=dcb, lru_w_a=dwa, lru_b_a=dba, lru_w_x=dwx, lru_b_x=dbx,
                 lru_lambda=dlam, attn_sinks=dsink[:, 0, :HEADS_PER_STEP].reshape(1, N_Q_HEADS), final_norm_g=dgf)
    return loss[0, 0], grad_x.reshape(bsz, s, d), gw_in, gw_r, gw_a, gw_o, small


ANY = pl.BlockSpec(memory_space=pl.ANY)


def _mesh_pos():
    return lax.axis_index("x"), lax.axis_index("y"), lax.axis_index("c")


def _chip_peers(x, y, c):
    return [((1 - x, y, c), 2 * (1 - x) + y), ((x, 1 - y, c), 2 * x + (1 - y)),
            ((1 - x, 1 - y, c), 2 * (1 - x) + (1 - y))]


def _remote(src, dst, send_sems, recv_sems, idx, peer):
    return pltpu.make_async_remote_copy(src_ref=src, dst_ref=dst, send_sem=send_sems.at[idx],
                                        recv_sem=recv_sems.at[idx], device_id=peer, device_id_type=MESH)


def _all_gather(bufs, split):
    n = len(bufs)
    n_fwd = 3 * sum(split)

    def body(*refs):
        ins, outs = refs[:n], refs[n:2 * n]
        send_sems, recv_sems, fsend_sems, frecv_sems = refs[2 * n:]
        x, y, c = _mesh_pos()
        me = 2 * x + y
        sib = (x, y, 1 - c)
        peers = _chip_peers(x, y, c)

        def part(ref, slot, t, half):
            if not split[t]:
                return ref.at[slot]
            hr = bufs[t].shape[1] // 2
            return ref.at[slot, pl.ds(pl.multiple_of(half * hr, 8), hr), :]

        sends, recvs = [], []
        for t in range(n):
            for k, (peer, pj) in enumerate(peers):
                src = part(ins[t], me, t, c)
                sends.append(_remote(src, part(outs[t], me, t, c), send_sems, recv_sems, 3 * t + k, peer))
                recvs.append(_remote(src, part(outs[t], pj, t, c), send_sems, recv_sems, 3 * t + k, peer))
        for cp in sends:
            cp.start()
        fwd, fwd_recv = [], []
        for t in range(n):
            for k, (peer, pj) in enumerate(peers):
                recvs[3 * t + k].wait_recv()
                if split[t]:
                    got = part(outs[t], pj, t, c)
                    f = len(fwd)
                    fwd.append(_remote(got, got, fsend_sems, frecv_sems, f, sib))
                    fwd_recv.append(_remote(got, part(outs[t], pj, t, 1 - c), fsend_sems, frecv_sems, f, sib))
                    fwd[-1].start()
        for cp in sends:
            cp.wait_send()
        for snd, rcv in zip(fwd, fwd_recv):
            snd.wait_send()
            rcv.wait_recv()

    return pl.pallas_call(
        body, name="all_gather_weights", out_shape=[jax.ShapeDtypeStruct(a.shape, a.dtype) for a in bufs],
        in_specs=[ANY] * n, out_specs=[ANY] * n, input_output_aliases={t: t for t in range(n)},
        scratch_shapes=[pltpu.SemaphoreType.DMA((3 * n,)), pltpu.SemaphoreType.DMA((3 * n,)),
                        pltpu.SemaphoreType.DMA((n_fwd,)), pltpu.SemaphoreType.DMA((n_fwd,))],
        compiler_params=_params())(*bufs)


def _pair_exchange(bigs, small):
    n = len(bigs)

    def body(*refs):
        ins, outs = refs[:n + 1], refs[n + 1:2 * n + 2]
        send_sems, recv_sems = refs[2 * n + 2:]
        x, y, c = _mesh_pos()
        sib = (x, y, 1 - c)
        cps = []
        for t in range(n):
            hr = bigs[t].shape[1] // 2
            src = ins[t].at[:, pl.ds(pl.multiple_of((1 - c) * hr, 8), hr), :]
            cps.append(_remote(src, outs[t], send_sems, recv_sems, t, sib))
        cps.append(_remote(ins[n], outs[n], send_sems, recv_sems, n, sib))
        for cp in cps:
            cp.start()
        for cp in cps:
            cp.wait()

    out_shape = [jax.ShapeDtypeStruct((a.shape[0], a.shape[1] // 2, a.shape[2]), a.dtype) for a in bigs]
    out_shape.append(jax.ShapeDtypeStruct(small.shape, small.dtype))
    return pl.pallas_call(
        body, name="pair_exchange", out_shape=out_shape, in_specs=[ANY] * (n + 1), out_specs=[ANY] * (n + 1),
        scratch_shapes=[pltpu.SemaphoreType.DMA((n + 1,)), pltpu.SemaphoreType.DMA((n + 1,))],
        compiler_params=_params())(*bigs, small)


def _row_tile(rows, row_bytes, cap_bytes=2 * 1024 * 1024):
    best = None
    for tr in range(8, rows + 1, 8):
        if rows % tr == 0 and tr * row_bytes <= cap_bytes:
            best = tr
    return best if best is not None else rows


def _pair_sum(own, got, c_idx, name):
    nblk, r, w = own.shape
    hr = r // 2
    tr = _row_tile(hr, w * 4)
    steps = hr // tr

    def body(c_ref, own_ref, got_ref, f_ref, b_ref):
        sm = own_ref[...] + got_ref[...]
        f_ref[...] = sm
        b_ref[...] = sm.astype(BF16)

    grid_spec = pltpu.PrefetchScalarGridSpec(
        num_scalar_prefetch=1, grid=(nblk, steps),
        in_specs=[pl.BlockSpec((None, tr, w), lambda j, i, c_ref: (j, c_ref[0] * steps + i, 0)),
                  pl.BlockSpec((None, tr, w), lambda j, i, c_ref: (j, i, 0))],
        out_specs=[pl.BlockSpec((None, tr, w), lambda j, i, c_ref: (j, i, 0)),
                   pl.BlockSpec((None, tr, w), lambda j, i, c_ref: (j, i, 0))])
    return pl.pallas_call(
        body, name=name, grid_spec=grid_spec,
        out_shape=[jax.ShapeDtypeStruct((nblk, hr, w), F32), jax.ShapeDtypeStruct((nblk, hr, w), BF16)],
        compiler_params=_params())(c_idx, own, got)


def _add2(a, b, name):
    r, w = a.shape
    tr = _row_tile(r, w * 4)

    def body(a_ref, b_ref, o_ref):
        o_ref[...] = a_ref[...] + b_ref[...]

    spec = pl.BlockSpec((tr, w), lambda i: (i, 0))
    return pl.pallas_call(body, name=name, grid=(r // tr,), out_shape=jax.ShapeDtypeStruct((r, w), F32),
                          in_specs=[spec, spec], out_specs=spec, compiler_params=_params())(a, b)


def _chip_exchange(bigs_b, small_slots):
    n = len(bigs_b)

    def body(*refs):
        ins, outs = refs[:n + 1], refs[n + 1:2 * n + 2]
        send_sems, recv_sems = refs[2 * n + 2:]
        x, y, c = _mesh_pos()
        me = 2 * x + y
        sends, recvs = [], []
        for t in range(n + 1):
            for k, (peer, pj) in enumerate(_chip_peers(x, y, c)):
                src = ins[t].at[pj] if t < n else ins[t].at[me]
                sends.append(_remote(src, outs[t].at[me], send_sems, recv_sems, 3 * t + k, peer))
                recvs.append(_remote(src, outs[t].at[pj], send_sems, recv_sems, 3 * t + k, peer))
        for cp in sends:
            cp.start()
        for snd, rcv in zip(sends, recvs):
            snd.wait_send()
            rcv.wait_recv()

    out_shape = [jax.ShapeDtypeStruct(a.shape, a.dtype) for a in bigs_b + [small_slots]]
    return pl.pallas_call(
        body, name="chip_exchange", out_shape=out_shape, in_specs=[ANY] * (n + 1), out_specs=[ANY] * (n + 1),
        input_output_aliases={n: n},
        scratch_shapes=[pltpu.SemaphoreType.DMA((3 * n + 3,)), pltpu.SemaphoreType.DMA((3 * n + 3,))],
        compiler_params=_params())(*bigs_b, small_slots)


def _chip_sum(own, got, idx, name):
    _, r, w = own.shape
    tr = _row_tile(r, w * 4)
    steps = r // tr

    def body(idx_ref, own_ref, g1_ref, g2_ref, g3_ref, o_ref):
        o_ref[...] = ((own_ref[...] + g1_ref[...].astype(F32)) + g2_ref[...].astype(F32)) + g3_ref[...].astype(F32)

    def pick(k):
        return pl.BlockSpec((None, tr, w), lambda i, idx_ref: (idx_ref[k], i, 0))

    grid_spec = pltpu.PrefetchScalarGridSpec(
        num_scalar_prefetch=1, grid=(steps,), in_specs=[pick(0), pick(1), pick(2), pick(3)],
        out_specs=pl.BlockSpec((tr, w), lambda i, idx_ref: (idx_ref[4] * steps + i, 0)))
    return pl.pallas_call(body, name=name, grid_spec=grid_spec, out_shape=jax.ShapeDtypeStruct((2 * r, w), F32),
                          compiler_params=_params())(idx, own, got, got, got)


def _tree_sum4(a, idx):
    _, r, w = a.shape
    tr = _row_tile(r, w * 4)
    steps = r // tr

    def body(idx_ref, a_ref, o_ref):
        o_ref[...] = (a_ref[0] + a_ref[1]) + (a_ref[2] + a_ref[3])

    grid_spec = pltpu.PrefetchScalarGridSpec(
        num_scalar_prefetch=1, grid=(steps,),
        in_specs=[pl.BlockSpec((N_CHIPS, tr, w), lambda i, idx_ref: (0, i, 0))],
        out_specs=pl.BlockSpec((tr, w), lambda i, idx_ref: (idx_ref[4] * steps + i, 0)))
    return pl.pallas_call(body, name="small_sum", grid_spec=grid_spec, out_shape=jax.ShapeDtypeStruct((2 * r, w), F32),
                          compiler_params=_params())(idx, a)


def _pair_gather(bufs):
    n = len(bufs)

    def body(*refs):
        ins, outs = refs[:n], refs[n:2 * n]
        send_sems, recv_sems = refs[2 * n:]
        x, y, c = _mesh_pos()
        sib = (x, y, 1 - c)
        sends, recvs = [], []
        for t in range(n):
            hr = bufs[t].shape[0] // 2
            mine = pl.ds(pl.multiple_of(c * hr, 8), hr)
            theirs = pl.ds(pl.multiple_of((1 - c) * hr, 8), hr)
            sends.append(_remote(ins[t].at[mine, :], outs[t].at[mine, :], send_sems, recv_sems, t, sib))
            recvs.append(_remote(ins[t].at[mine, :], outs[t].at[theirs, :], send_sems, recv_sems, t, sib))
        for cp in sends:
            cp.start()
        for snd, rcv in zip(sends, recvs):
            snd.wait_send()
            rcv.wait_recv()

    return pl.pallas_call(
        body, name="pair_gather", out_shape=[jax.ShapeDtypeStruct(a.shape, a.dtype) for a in bufs],
        in_specs=[ANY] * n, out_specs=[ANY] * n, input_output_aliases={t: t for t in range(n)},
        scratch_shapes=[pltpu.SemaphoreType.DMA((n,)), pltpu.SemaphoreType.DMA((n,))],
        compiler_params=_params())(*bufs)


def _reduce_scatter(bigs, small, x, y, c):
    n = len(bigs)
    me = 2 * x + y
    got1 = _pair_exchange(bigs, small)
    c_idx = jnp.reshape(c, (1,)).astype(jnp.int32)
    part = [_pair_sum(bigs[t], got1[t], c_idx, f"pair_sum_{t}") for t in range(n)]
    small_pair = _add2(small, got1[n], "pair_sum_small")
    small_slots = _put_slot(small_pair, N_CHIPS, jnp.stack([me, c]).astype(jnp.int32), PK_HALF, F32, "small_slot")
    got2 = _chip_exchange([p[1] for p in part], small_slots)
    others = [jnp.where(me <= k, k + 1, k) for k in range(N_CHIPS - 1)]
    idx = jnp.stack([me] + others + [c]).astype(jnp.int32)
    bufs = [_chip_sum(part[t][0], got2[t], idx, f"chip_sum_{t}") for t in range(n)]
    bufs.append(_tree_sum4(got2[n], idx))
    return _pair_gather(bufs)


_VEC_NAMES = ("norm_g", "conv_b", "lru_b_a", "lru_b_x", "lru_lambda", "final_norm_g")


def _pack_small(p, conv_full=None):
    rows = [p["lru_w_a"].reshape(PK_WX - PK_WA, LANES), p["lru_w_x"].reshape(PK_VEC - PK_WX, LANES)]
    rows += [p[k].reshape(8, LANES) for k in _VEC_NAMES]
    rows.append(jnp.pad(p["attn_sinks"].reshape(1, N_Q_HEADS), ((0, 7), (0, LANES - N_Q_HEADS))))
    tail = PK_ROWS - PK_CONV
    if conv_full is None:
        rows.append(jnp.zeros((tail, LANES), F32))
    else:
        rows.append(conv_full.reshape(32, LANES))
        rows.append(jnp.zeros((tail - 32, LANES), F32))
    return jnp.concatenate(rows, axis=0)


def _unpack_small(pk, like):
    out = {"lru_w_a": pk[PK_WA:PK_WX].reshape(like["lru_w_a"].shape),
           "lru_w_x": pk[PK_WX:PK_VEC].reshape(like["lru_w_x"].shape)}
    for j, k in enumerate(_VEC_NAMES):
        out[k] = pk[PK_VEC + 8 * j:PK_VEC + 8 * j + 8].reshape(like[k].shape)
    out["attn_sinks"] = pk[PK_SINK:PK_SINK + 1, :N_Q_HEADS].reshape(like["attn_sinks"].shape)
    return out


_WEIGHTS = ("norm_g", "w_in", "conv_w", "conv_b", "lru_w_a", "lru_b_a", "lru_w_x", "lru_b_x", "lru_lambda",
            "attn_sinks", "w_rnn_out", "w_attn_out", "w_o", "final_norm_g")
_SMALL = ("norm_g", "conv_b", "lru_w_a", "lru_b_a", "lru_w_x", "lru_b_x", "lru_lambda", "attn_sinks", "final_norm_g")
_ROW_SHARDED = ("w_rnn_out", "w_attn_out", "w_o")


def kernel(x, norm_g, w_in, conv_w, conv_b, lru_w_a, lru_b_a, lru_w_x, lru_b_x, lru_lambda, attn_sinks, w_rnn_out, w_attn_out, w_o, final_norm_g, loss_target, m_norm_g, m_w_in, m_conv_w, m_conv_b, m_lru_w_a, m_lru_b_a, m_lru_w_x, m_lru_b_x, m_lru_lambda, m_attn_sinks, m_w_rnn_out, m_w_attn_out, m_w_o, m_final_norm_g, v_norm_g, v_w_in, v_conv_w, v_conv_b, v_lru_w_a, v_lru_b_a, v_lru_w_x, v_lru_b_x, v_lru_lambda, v_attn_sinks, v_w_rnn_out, v_w_attn_out, v_w_o, v_final_norm_g):
    w = dict(norm_g=norm_g, w_in=w_in, conv_w=conv_w, conv_b=conv_b, lru_w_a=lru_w_a, lru_b_a=lru_b_a, lru_w_x=lru_w_x,
             lru_b_x=lru_b_x, lru_lambda=lru_lambda, attn_sinks=attn_sinks, w_rnn_out=w_rnn_out, w_attn_out=w_attn_out,
             w_o=w_o, final_norm_g=final_norm_g)
    m = dict(norm_g=m_norm_g, w_in=m_w_in, conv_w=m_conv_w, conv_b=m_conv_b, lru_w_a=m_lru_w_a, lru_b_a=m_lru_b_a,
             lru_w_x=m_lru_w_x, lru_b_x=m_lru_b_x, lru_lambda=m_lru_lambda, attn_sinks=m_attn_sinks,
             w_rnn_out=m_w_rnn_out, w_attn_out=m_w_attn_out, w_o=m_w_o, final_norm_g=m_final_norm_g)
    v = dict(norm_g=v_norm_g, w_in=v_w_in, conv_w=v_conv_w, conv_b=v_conv_b, lru_w_a=v_lru_w_a, lru_b_a=v_lru_b_a,
             lru_w_x=v_lru_w_x, lru_b_x=v_lru_b_x, lru_lambda=v_lru_lambda, attn_sinks=v_attn_sinks,
             w_rnn_out=v_w_rnn_out, w_attn_out=v_w_attn_out, w_o=v_w_o, final_norm_g=v_final_norm_g)
    mx, my, mc = _mesh_pos()
    me = 2 * mx + my
    d = D_MODEL

    slot0 = jnp.stack([me, jnp.zeros_like(me)]).astype(jnp.int32)
    bufs = [_put_slot(w[k][0], N_CHIPS, slot0, w[k].shape[1], BF16, "cast_" + k) for k in ("w_in",) + _ROW_SHARDED]
    bufs.append(_put_slot(w["conv_w"][0], N_CHIPS, slot0, CONV_WIDTH, F32, "slot_conv_w"))
    g_in, g_r, g_a, g_o, g_cw = _all_gather(bufs, [True, True, True, True, False])
    conv_full = g_cw.transpose(1, 0, 2).reshape(CONV_WIDTH, D_RNN)

    loss_local, grad_x, gw_in, gw_r, gw_a, gw_o, gsmall = _local_grads(
        x, loss_target, w["norm_g"], g_in, conv_full, w["conv_b"], w["lru_w_a"][0], w["lru_b_a"], w["lru_w_x"][0],
        w["lru_b_x"], w["lru_lambda"], w["attn_sinks"][0], g_r.reshape(d, d), g_a.reshape(d, d), g_o.reshape(d, d),
        w["final_norm_g"].reshape(1, d))
    loss = lax.psum(loss_local, ("x", "y", "c"))

    gpack = _pack_small(gsmall, gsmall["conv_w"])
    bigs = [gw_in] + [g.reshape(N_CHIPS, ROW_BLK, d) for g in (gw_r, gw_a, gw_o)]
    f_in, f_r, f_a, f_o, spack = _reduce_scatter(bigs, gpack, mx, my, mc)

    grads = _unpack_small(spack, w)
    conv_all = spack[PK_CONV:PK_CONV + 32].reshape(CONV_WIDTH, D_RNN)
    grads["conv_w"] = lax.dynamic_slice_in_dim(conv_all, me * (D_RNN // N_CHIPS), D_RNN // N_CHIPS, axis=1)[None]
    grads["w_in"] = f_in[None]
    grads["w_rnn_out"], grads["w_attn_out"], grads["w_o"] = f_r[None], f_a[None], f_o[None]

    delta, new_m, new_v = {}, {}, {}
    for k in ("w_in",) + _ROW_SHARDED:
        dk, mk, vk = _adamw(w[k][0], grads[k][0], m[k][0], v[k][0], "adamw_" + k)
        delta[k], new_m[k], new_v[k] = dk[None], mk[None], vk[None]
    shp = (2 * CONV_WIDTH, LANES)
    dk, mk, vk = _adamw(w["conv_w"].reshape(shp), grads["conv_w"].reshape(shp), m["conv_w"].reshape(shp),
                        v["conv_w"].reshape(shp), "adamw_conv_w")
    delta["conv_w"], new_m["conv_w"], new_v["conv_w"] = (a.reshape(w["conv_w"].shape) for a in (dk, mk, vk))
    dk, mk, vk = _adamw(_pack_small(w), spack, _pack_small(m), _pack_small(v), "adamw_small")
    for src, dst in ((dk, delta), (mk, new_m), (vk, new_v)):
        dst.update(_unpack_small(src, w))

    return (loss, grad_x, *[grads[k] for k in _WEIGHTS], *[delta[k] for k in _WEIGHTS],
            *[new_m[k] for k in _WEIGHTS], *[new_v[k] for k in _WEIGHTS])
```

```python
import functools
import math

import jax
import jax.numpy as jnp
from jax import lax
from jax.experimental import pallas as pl
from jax.experimental.pallas import tpu as pltpu

F32 = jnp.float32
BF16 = jnp.bfloat16
MESH = pl.DeviceIdType.MESH

D_MODEL = 1024
D_RNN = 1024
N_RNN_BLOCKS = 8
RNN_BLOCK = D_RNN // N_RNN_BLOCKS
CONV_WIDTH = 4
LRU_C = 8.0
HEAD_DIM = 64
N_Q_HEADS = 16
N_KV_HEADS = 4
D_ATTN = N_Q_HEADS * HEAD_DIM
D_KV = N_KV_HEADS * HEAD_DIM
WINDOW = 128
ROPE_DIM = HEAD_DIM // 4
ROPE_THETA = 500000.0
NORM_EPS = 1e-6
OFF_RNN_X = 0
OFF_RNN_G = OFF_RNN_X + D_RNN
OFF_Q = OFF_RNN_G + D_RNN
OFF_K = OFF_Q + D_ATTN
OFF_V = OFF_K + D_KV
OFF_ATTN_G = OFF_V + D_KV
OFF_MERGE_R = OFF_ATTN_G + D_ATTN
OFF_MERGE_A = OFF_MERGE_R + D_MODEL
D_IN = OFF_MERGE_A + D_MODEL

ADAM_LR = 0.001
ADAM_B1 = 0.9
ADAM_B2 = 0.999
ADAM_EPS = 1e-08
ADAM_WD = 0.01
ADAM_STEP = 10

N_CHIPS = 4
W_BLK = D_IN // N_CHIPS
ROW_BLK = D_MODEL // N_CHIPS
LANES = 128
ATT_BLK = 128
VMEM_LIMIT = 56 * 1024 * 1024
NEG_BIG = -1e30
ATTN_SCALE = 1.0 / math.sqrt(HEAD_DIM)

PK_WA = 0
PK_WX = PK_WA + N_RNN_BLOCKS * RNN_BLOCK
PK_VEC = PK_WX + N_RNN_BLOCKS * RNN_BLOCK
PK_SINK = PK_VEC + 6 * 8
PK_CONV = PK_SINK + 8
PK_ROWS = PK_CONV + 32 + 8
PK_HALF = PK_ROWS // 2


def _params(**kw):
    return pltpu.CompilerParams(vmem_limit_bytes=VMEM_LIMIT, **kw)


def _sigmoid(z):
    return 1.0 / (1.0 + jnp.exp(-z))


def _dot(a, b):
    return jnp.dot(a, b, preferred_element_type=F32)


def _dot_nt(a, b):
    return lax.dot_general(a, b, (((1,), (1,)), ((), ())), preferred_element_type=F32)


def _dot_tn(a, b):
    return lax.dot_general(a, b, (((0,), (0,)), ((), ())), preferred_element_type=F32)


def _put_slot(src, n_slots, slot_and_blk, rows, dtype, name):
    _, c = src.shape
    tr = _row_tile(rows, c * 4)
    steps = rows // tr

    def body(idx_ref, s_ref, o_ref):
        o_ref[...] = s_ref[...].astype(dtype)

    grid_spec = pltpu.PrefetchScalarGridSpec(
        num_scalar_prefetch=1, grid=(steps,),
        in_specs=[pl.BlockSpec((tr, c), lambda i, idx_ref: (idx_ref[1] * steps + i, 0))],
        out_specs=pl.BlockSpec((None, tr, c), lambda i, idx_ref: (idx_ref[0], i, 0)))
    return pl.pallas_call(body, name=name, grid_spec=grid_spec,
                          out_shape=jax.ShapeDtypeStruct((n_slots, rows, c), dtype),
                          compiler_params=_params())(slot_and_blk, src)


def _rmsnorm_fwd(x, g):
    t, d = x.shape
    tm = min(t, 512)

    def body(x_ref, g_ref, o_ref):
        xv = x_ref[...]
        r = lax.rsqrt(jnp.mean(xv * xv, axis=-1, keepdims=True) + NORM_EPS)
        o_ref[...] = (xv * r * g_ref[...]).astype(BF16)

    return pl.pallas_call(
        body, name="rmsnorm_fwd", grid=(t // tm,), out_shape=jax.ShapeDtypeStruct((t, d), BF16),
        in_specs=[pl.BlockSpec((tm, d), lambda i: (i, 0)), pl.BlockSpec((1, d), lambda i: (0, 0))],
        out_specs=pl.BlockSpec((tm, d), lambda i: (i, 0)), compiler_params=_params())(x, g)


def _rmsnorm_bwd(x, dh, dx2, g):
    t, d = x.shape
    tm = min(t, 512)

    def body(x_ref, dh_ref, dx2_ref, g_ref, gx_ref, dg_ref):
        i = pl.program_id(0)
        xv = x_ref[...]
        dhv = dh_ref[...]
        r = lax.rsqrt(jnp.mean(xv * xv, axis=-1, keepdims=True) + NORM_EPS)
        nrm = xv * r
        dn = dhv * g_ref[...]
        gx_ref[...] = dx2_ref[...] + r * (dn - nrm * jnp.mean(dn * nrm, axis=-1, keepdims=True))

        @pl.when(i == 0)
        def _():
            dg_ref[...] = jnp.zeros_like(dg_ref)

        dg_ref[...] += jnp.sum(dhv * nrm, axis=0, keepdims=True)

    return pl.pallas_call(
        body, name="rmsnorm_bwd", grid=(t // tm,),
        out_shape=(jax.ShapeDtypeStruct((t, d), F32), jax.ShapeDtypeStruct((1, d), F32)),
        in_specs=[pl.BlockSpec((tm, d), lambda i: (i, 0)), pl.BlockSpec((tm, d), lambda i: (i, 0)),
                  pl.BlockSpec((tm, d), lambda i: (i, 0)), pl.BlockSpec((1, d), lambda i: (0, 0))],
        out_specs=(pl.BlockSpec((tm, d), lambda i: (i, 0)), pl.BlockSpec((1, d), lambda i: (0, 0))),
        compiler_params=_params())(x, dh, dx2, g)


def _adamw(w, g, m, v, name):
    r, c = w.shape
    tr = _row_tile(r, c * 4, 1024 * 1024)
    c1 = 1.0 - ADAM_B1 ** ADAM_STEP
    c2 = 1.0 - ADAM_B2 ** ADAM_STEP

    def body(w_ref, g_ref, m_ref, v_ref, d_ref, nm_ref, nv_ref):
        gv = g_ref[...]
        nm = ADAM_B1 * m_ref[...] + (1.0 - ADAM_B1) * gv
        nv = ADAM_B2 * v_ref[...] + (1.0 - ADAM_B2) * (gv * gv)
        m_hat = nm / c1
        v_hat = nv / c2
        d_ref[...] = -ADAM_LR * (m_hat / (jnp.sqrt(v_hat) + ADAM_EPS) + ADAM_WD * w_ref[...])
        nm_ref[...] = nm
        nv_ref[...] = nv

    spec = pl.BlockSpec((tr, c), lambda i: (i, 0))
    sds = jax.ShapeDtypeStruct((r, c), F32)
    return pl.pallas_call(
        body, name=name, grid=(r // tr,), out_shape=(sds, sds, sds),
        in_specs=[spec, spec, spec, spec], out_specs=(spec, spec, spec), compiler_params=_params())(w, g, m, v)


def _in_proj(h, w_bm):
    t, d = h.shape
    nb, _, wb = w_bm.shape
    tm = min(t, 512)

    def body(h_ref, w_ref, o_ref):
        o_ref[...] = _dot(h_ref[...], w_ref[...])

    return pl.pallas_call(
        body, name="in_proj", grid=(nb, t // tm), out_shape=jax.ShapeDtypeStruct((t, nb * wb), F32),
        in_specs=[pl.BlockSpec((tm, d), lambda j, i: (i, 0)), pl.BlockSpec((None, d, wb), lambda j, i: (j, 0, 0))],
        out_specs=pl.BlockSpec((tm, wb), lambda j, i: (i, j)), compiler_params=_params())(h, w_bm)


def _grad_h(dproj, w_bm):
    t = dproj.shape[0]
    nb, d, wb = w_bm.shape
    tm = min(t, 1024)

    def body(dp_ref, w_ref, o_ref, acc_ref):
        k = pl.program_id(1)

        @pl.when(k == 0)
        def _():
            acc_ref[...] = jnp.zeros_like(acc_ref)

        acc_ref[...] += _dot_nt(dp_ref[...], w_ref[...])

        @pl.when(k == nb - 1)
        def _():
            o_ref[...] = acc_ref[...]

    return pl.pallas_call(
        body, name="grad_h", grid=(t // tm, nb), out_shape=jax.ShapeDtypeStruct((t, d), F32),
        in_specs=[pl.BlockSpec((tm, wb), lambda i, k: (i, k)), pl.BlockSpec((None, d, wb), lambda i, k: (k, 0, 0))],
        out_specs=pl.BlockSpec((tm, d), lambda i, k: (i, 0)),
        scratch_shapes=[pltpu.VMEM((tm, d), F32)], compiler_params=_params())(dproj, w_bm)


def _grad_w_sq(a, b, name):
    t, d = a.shape
    n = b.shape[1]
    tk = min(t, 512)
    nk = t // tk

    def body(a_ref, b_ref, o_ref):
        k = pl.program_id(0)

        @pl.when(k == 0)
        def _():
            o_ref[...] = jnp.zeros_like(o_ref)

        o_ref[...] += _dot_tn(a_ref[...], b_ref[...])

    return pl.pallas_call(
        body, name=name, grid=(nk,), out_shape=jax.ShapeDtypeStruct((d, n), F32),
        in_specs=[pl.BlockSpec((tk, d), lambda k: (k, 0)), pl.BlockSpec((tk, n), lambda k: (k, 0))],
        out_specs=pl.BlockSpec((d, n), lambda k: (0, 0)), compiler_params=_params())(a, b)


def _shift_down(v, d, fill):
    n = v.shape[0]
    if d % 8 == 0:
        return jnp.concatenate([jnp.full((d,) + v.shape[1:], fill, v.dtype), v[: n - d]], axis=0)
    row = lax.broadcasted_iota(jnp.int32, v.shape, 0)
    return jnp.where(row >= d, pltpu.roll(v, d, axis=0), fill)


def _shift_up(v, d, fill):
    n = v.shape[0]
    if d % 8 == 0:
        return jnp.concatenate([v[d:], jnp.full((d,) + v.shape[1:], fill, v.dtype)], axis=0)
    row = lax.broadcasted_iota(jnp.int32, v.shape, 0)
    return jnp.where(row < n - d, pltpu.roll(v, n - d, axis=0), fill)


def _scan(a, b, shift):
    n = a.shape[0]
    d = 1
    while d < n:
        b = a * shift(b, d, 0.0) + b
        if 2 * d < n:
            a = a * shift(a, d, 1.0)
        d *= 2
    return b


def _neg_expm1(y):
    series = -y * (1.0 + y * (1.0 / 2.0) * (1.0 + y * (1.0 / 3.0) * (1.0 + y * (1.0 / 4.0) * (
        1.0 + y * (1.0 / 5.0) * (1.0 + y * (1.0 / 6.0) * (1.0 + y * (1.0 / 7.0)))))))
    return jnp.where(y > -0.25, series, 1.0 - jnp.exp(y))


def _softplus(z):
    e = jnp.exp(-jnp.abs(z))
    w = 1.0 + e
    log1p = jnp.where(w == 1.0, e, jnp.log(w) * (e / jnp.where(w == 1.0, 1.0, w - 1.0)))
    return jnp.maximum(z, 0.0) + log1p


def _conv(up, cw, cb):
    out = cb + cw[CONV_WIDTH - 1:CONV_WIDTH, :] * up
    for j in range(CONV_WIDTH - 1):
        out = out + cw[j:j + 1, :] * _shift_down(up, CONV_WIDTH - 1 - j, 0.0)
    return out


def _lru_gates(u, wa_ref, ba_ref, wx_ref, bx_ref, lam_ref):
    ub = u.astype(BF16)
    r = _sigmoid(_dot(ub, wa_ref[...].astype(BF16)) + ba_ref[...])
    i = _sigmoid(_dot(ub, wx_ref[...].astype(BF16)) + bx_ref[...])
    sp = _softplus(-lam_ref[...])
    log_a = (-LRU_C) * r * sp
    a = jnp.exp(log_a)
    mult = jnp.sqrt(_neg_expm1(2.0 * log_a))
    return r, i, sp, a, mult


def _lru_specs(s):
    cb = RNN_BLOCK
    vec = pl.BlockSpec((1, cb), lambda n, b: (0, n))
    return dict(
        up=pl.BlockSpec((None, s, cb), lambda n, b: (b, 0, OFF_RNN_X // cb + n)),
        gr=pl.BlockSpec((None, s, cb), lambda n, b: (b, 0, OFF_RNN_G // cb + n)),
        act=pl.BlockSpec((None, s, cb), lambda n, b: (b, 0, n)),
        cw=pl.BlockSpec((CONV_WIDTH, cb), lambda n, b: (0, n)),
        vec=vec,
        wblk=pl.BlockSpec((None, cb, cb), lambda n, b: (n, 0, 0)),
    )


def _lru_fwd(proj3, cw, cb, wa, ba, wx, bx, lam):
    bsz, s, _ = proj3.shape
    sp = _lru_specs(s)

    def body(up_ref, gr_ref, cw_ref, cb_ref, wa_ref, ba_ref, wx_ref, bx_ref, lam_ref, h_ref, y_ref):
        u = _conv(up_ref[...], cw_ref[...], cb_ref[...])
        _, i, _, a, mult = _lru_gates(u, wa_ref, ba_ref, wx_ref, bx_ref, lam_ref)
        h = _scan(a, mult * (i * u), _shift_down)
        h_ref[...] = h
        g = gr_ref[...]
        y_ref[...] = (h * (g * _sigmoid(g))).astype(BF16)

    return pl.pallas_call(
        body, name="lru_fwd", grid=(N_RNN_BLOCKS, bsz),
        out_shape=(jax.ShapeDtypeStruct((bsz, s, D_RNN), F32), jax.ShapeDtypeStruct((bsz, s, D_RNN), BF16)),
        in_specs=[sp["up"], sp["gr"], sp["cw"], sp["vec"], sp["wblk"], sp["vec"], sp["wblk"], sp["vec"], sp["vec"]],
        out_specs=(sp["act"], sp["act"]), compiler_params=_params())(proj3, proj3, cw, cb, wa, ba, wx, bx, lam)


def _lru_bwd(proj3, h3, dy3, cw, cb, wa, ba, wx, bx, lam):
    bsz, s, _ = proj3.shape
    sp = _lru_specs(s)

    def body(up_ref, gr_ref, h_ref, dy_ref, cw_ref, cb_ref, wa_ref, ba_ref, wx_ref, bx_ref, lam_ref,
             dup_ref, dgr_ref, dcw_ref, dcb_ref, dwa_ref, dba_ref, dwx_ref, dbx_ref, dlam_ref):
        b = pl.program_id(1)
        up = up_ref[...]
        cwv = cw_ref[...]
        u = _conv(up, cwv, cb_ref[...])
        r, i, spv, a, mult = _lru_gates(u, wa_ref, ba_ref, wx_ref, bx_ref, lam_ref)
        h = h_ref[...]
        g = gr_ref[...]
        dy = dy_ref[...]
        sg = _sigmoid(g)
        dgr_ref[...] = (dy * h * (sg * (1.0 + g * (1.0 - sg)))).astype(BF16)
        dh = dy * (g * sg)
        adj = _scan(_shift_up(a, 1, 0.0), dh, _shift_up)
        da = adj * _shift_down(h, 1, 0.0)
        dmult = adj * (i * u)
        di = adj * mult * u
        du = adj * mult * i
        dla = da * a - dmult * (a * a) / mult
        dr = dla * ((-LRU_C) * spv)
        dsp = jnp.sum(dla * ((-LRU_C) * r), axis=0, keepdims=True)
        dza = dr * r * (1.0 - r)
        dzx = di * i * (1.0 - i)
        ub = u.astype(BF16)
        dzab = dza.astype(BF16)
        dzxb = dzx.astype(BF16)
        du = du + _dot_nt(dzab, wa_ref[...].astype(BF16)) + _dot_nt(dzxb, wx_ref[...].astype(BF16))
        dup = cwv[CONV_WIDTH - 1:CONV_WIDTH, :] * du
        for j in range(CONV_WIDTH - 1):
            dup = dup + cwv[j:j + 1, :] * _shift_up(du, CONV_WIDTH - 1 - j, 0.0)
        dup_ref[...] = dup.astype(BF16)

        @pl.when(b == 0)
        def _():
            for ref in (dcw_ref, dcb_ref, dwa_ref, dba_ref, dwx_ref, dbx_ref, dlam_ref):
                ref[...] = jnp.zeros_like(ref)

        rows = [jnp.sum(du * _shift_down(up, CONV_WIDTH - 1 - j, 0.0), axis=0, keepdims=True)
                for j in range(CONV_WIDTH - 1)]
        rows.append(jnp.sum(du * up, axis=0, keepdims=True))
        dcw_ref[...] += jnp.concatenate(rows, axis=0)
        dcb_ref[...] += jnp.sum(du, axis=0, keepdims=True)
        dwa_ref[...] += _dot_tn(ub, dzab)
        dba_ref[...] += jnp.sum(dza, axis=0, keepdims=True)
        dwx_ref[...] += _dot_tn(ub, dzxb)
        dbx_ref[...] += jnp.sum(dzx, axis=0, keepdims=True)
        dlam_ref[...] += dsp * (-_sigmoid(-lam_ref[...]))

    act_b = jax.ShapeDtypeStruct((bsz, s, D_RNN), BF16)
    vec = jax.ShapeDtypeStruct((1, D_RNN), F32)
    wsd = jax.ShapeDtypeStruct((N_RNN_BLOCKS, RNN_BLOCK, RNN_BLOCK), F32)
    return pl.pallas_call(
        body, name="lru_bwd", grid=(N_RNN_BLOCKS, bsz),
        out_shape=(act_b, act_b, jax.ShapeDtypeStruct((CONV_WIDTH, D_RNN), F32), vec, wsd, vec, wsd, vec, vec),
        in_specs=[sp["up"], sp["gr"], sp["act"], sp["act"], sp["cw"], sp["vec"], sp["wblk"], sp["vec"],
                  sp["wblk"], sp["vec"], sp["vec"]],
        out_specs=(sp["act"], sp["act"], sp["cw"], sp["vec"], sp["wblk"], sp["vec"], sp["wblk"], sp["vec"], sp["vec"]),
        compiler_params=_params())(proj3, proj3, h3, dy3, cw, cb, wa, ba, wx, bx, lam)


def _rope_tables(s):
    half = ROPE_DIM // 2
    pos = jnp.arange(s, dtype=F32)
    inv_freq = ROPE_THETA ** (-jnp.arange(0, ROPE_DIM, 2, dtype=F32) / ROPE_DIM)
    ang = pos[:, None] * inv_freq[None, :]
    cos, sin = jnp.cos(ang), jnp.sin(ang)
    rest = HEAD_DIM - ROPE_DIM
    cos64 = jnp.concatenate([cos, cos, jnp.ones((s, rest), F32)], axis=1)
    sin64 = jnp.concatenate([-sin, sin, jnp.zeros((s, rest), F32)], axis=1)
    assert half * 2 == ROPE_DIM
    return jnp.tile(cos64, (1, LANES // HEAD_DIM)), jnp.tile(sin64, (1, LANES // HEAD_DIM))


def _swap_rot_halves(v):
    half = ROPE_DIM // 2
    lane = lax.broadcasted_iota(jnp.int32, v.shape, 1) % HEAD_DIM
    second = jnp.where(lane < ROPE_DIM, pltpu.roll(v, half, axis=1), 0.0)
    return jnp.where(lane < half, pltpu.roll(v, LANES - half, axis=1), second)


def _rope(v, cos, sin):
    tiles = []
    for t in range(v.shape[1] // LANES):
        vt = v[:, t * LANES:(t + 1) * LANES]
        tiles.append(vt * cos + _swap_rot_halves(vt) * sin)
    return tiles[0] if len(tiles) == 1 else jnp.concatenate(tiles, axis=1)


def _unrope(v, cos, sin):
    tiles = []
    for t in range(v.shape[1] // LANES):
        vt = v[:, t * LANES:(t + 1) * LANES]
        tiles.append(vt * cos + _swap_rot_halves(vt * sin))
    return tiles[0] if len(tiles) == 1 else jnp.concatenate(tiles, axis=1)


HEADS_PER_STEP = 8
QW = HEADS_PER_STEP * HEAD_DIM
N_PAIRS = N_Q_HEADS // HEADS_PER_STEP
Q_PER_KV = N_Q_HEADS // N_KV_HEADS
KV_PER_STEP = HEADS_PER_STEP // Q_PER_KV


QT_COLS = Q_PER_KV * ATT_BLK


def _attn_scratch(s, with_vt):
    nb = s // ATT_BLK
    pad = s + ATT_BLK
    shapes = [pltpu.VMEM((nb, LANES, QT_COLS), BF16),
              pltpu.VMEM((KV_PER_STEP, pad, LANES), BF16),
              pltpu.VMEM((KV_PER_STEP, pad, LANES), BF16)]
    if with_vt:
        shapes.append(pltpu.VMEM((LANES, pad), BF16))
    return shapes


def _attn_specs(s, order):
    def mk(width, base, **kw):
        if order == "bp":
            return pl.BlockSpec((None, s, width), lambda b, p: (b, 0, base + p), **kw)
        return pl.BlockSpec((None, s, width), lambda p, b: (b, 0, base + p), **kw)
    one = dict(pipeline_mode=pl.Buffered(1))
    tbl = pl.BlockSpec((s, LANES), lambda *_: (0, 0))
    return dict(q=mk(QW, OFF_Q // QW), k=mk(LANES, OFF_K // LANES), v=mk(LANES, OFF_V // LANES),
                g=mk(QW, OFF_ATTN_G // QW), act=mk(QW, 0), kv=mk(LANES, 0), tbl=tbl,
                q1=mk(QW, OFF_Q // QW, **one), g1=mk(QW, OFF_ATTN_G // QW, **one), act1=mk(QW, 0, **one),
                smem=pl.BlockSpec(memory_space=pltpu.SMEM))


def _to_qt(blk):
    rows = []
    for j in range(KV_PER_STEP):
        cols = []
        for tt in range(2):
            t = 2 * j + tt
            tr = blk[:, t * LANES:(t + 1) * LANES].T
            cols += [tr[0:HEAD_DIM, :], tr[HEAD_DIM:, :]]
        rows.append(jnp.concatenate(cols, axis=1))
    return jnp.concatenate(rows, axis=0)


def _from_qt(xt):
    tiles = []
    for j in range(KV_PER_STEP):
        for tt in range(2):
            g0 = 2 * tt
            pair = jnp.concatenate([xt[j * HEAD_DIM:(j + 1) * HEAD_DIM, (g0 + i) * ATT_BLK:(g0 + i + 1) * ATT_BLK]
                                    for i in range(2)], axis=0)
            tiles.append(pair.T)
    return jnp.concatenate(tiles, axis=1)


def _attn_prep(q_ref, k_ref, v_ref, cos_ref, sin_ref, qt_sc, km_sc, vm_sc, t_sc, transposed, nb):
    zeros = jnp.zeros((ATT_BLK, LANES), BF16)
    for j in range(KV_PER_STEP):
        km_sc[j, 0:ATT_BLK, :] = zeros
        vm_sc[j, 0:ATT_BLK, :] = zeros
    t_sc[:, 0:ATT_BLK] = zeros
    head_of_lane = lax.broadcasted_iota(jnp.int32, (ATT_BLK, LANES), 1) // HEAD_DIM

    def prep(n, carry):
        r0 = pl.multiple_of(n * ATT_BLK, ATT_BLK)
        cs = cos_ref[pl.ds(r0, ATT_BLK), :]
        sn = sin_ref[pl.ds(r0, ATT_BLK), :]
        qt_sc[n] = _to_qt(_rope(q_ref[pl.ds(r0, ATT_BLK), :], cs, sn) * ATTN_SCALE).astype(BF16)
        k = _rope(k_ref[pl.ds(r0, ATT_BLK), :], cs, sn)
        v = v_ref[pl.ds(r0, ATT_BLK), :]
        for j in range(KV_PER_STEP):
            km_sc[j, pl.ds(r0 + ATT_BLK, ATT_BLK), :] = jnp.where(head_of_lane == j, k, 0.0).astype(BF16)
            vm_sc[j, pl.ds(r0 + ATT_BLK, ATT_BLK), :] = jnp.where(head_of_lane == j, v, 0.0).astype(BF16)
        t_sc[:, pl.ds(r0 + ATT_BLK, ATT_BLK)] = (k if transposed == "k" else v).T.astype(BF16)
        return carry

    lax.fori_loop(0, nb, prep, 0)


def _band_mask_t(n):
    shape = (2 * ATT_BLK, QT_COLS)
    key = lax.broadcasted_iota(jnp.int32, shape, 0)
    qry = lax.broadcasted_iota(jnp.int32, shape, 1) % ATT_BLK
    lo = jnp.where(n == 0, ATT_BLK, 0)
    return (key > qry) & (key <= qry + WINDOW) & (key >= lo)


def _sink_row(sink_ref, first):
    return jnp.concatenate([jnp.full((1, ATT_BLK), sink_ref[first + g], F32) for g in range(Q_PER_KV)], axis=1)


def _softmax_cols(scores_t, valid, sink):
    sc = jnp.where(valid, scores_t, NEG_BIG)
    m = jnp.maximum(jnp.max(sc, axis=0, keepdims=True), sink)
    e = jnp.exp(sc - m)
    es = jnp.exp(sink - m)
    inv = 1.0 / (jnp.sum(e, axis=0, keepdims=True) + es)
    return e * inv, es * inv


def _attn_fwd(proj3, sinks, cosf, sinf):
    bsz, s, _ = proj3.shape
    nb = s // ATT_BLK
    sp = _attn_specs(s, "bp")

    def body(sink_ref, q_ref, k_ref, v_ref, g_ref, cos_ref, sin_ref, o_ref, y_ref, qt_sc, km_sc, vm_sc, vt_sc):
        p = pl.program_id(1)
        _attn_prep(q_ref, k_ref, v_ref, cos_ref, sin_ref, qt_sc, km_sc, vm_sc, vt_sc, "v", nb)
        kv_row = lax.broadcasted_iota(jnp.int32, (LANES, QT_COLS), 0) // HEAD_DIM

        def blk(n, carry):
            r0 = pl.multiple_of(n * ATT_BLK, ATT_BLK)
            valid = _band_mask_t(n)
            rq = qt_sc[n]
            vt = vt_sc[:, pl.ds(r0, 2 * ATT_BLK)]
            ots = []
            for j in range(KV_PER_STEP):
                st = _dot(km_sc[j, pl.ds(r0, 2 * ATT_BLK), :], rq)
                pt, _ = _softmax_cols(st, valid, _sink_row(sink_ref, p * HEADS_PER_STEP + j * Q_PER_KV))
                ots.append(_dot(vt, pt.astype(BF16)))
            o = _from_qt(jnp.where(kv_row == 0, ots[0], ots[1]))
            o_ref[pl.ds(r0, ATT_BLK), :] = o
            g = g_ref[pl.ds(r0, ATT_BLK), :]
            y_ref[pl.ds(r0, ATT_BLK), :] = (o * (g * _sigmoid(g))).astype(BF16)
            return carry

        lax.fori_loop(0, nb, blk, 0)

    return pl.pallas_call(
        body, name="attn_fwd", grid=(bsz, N_PAIRS),
        out_shape=(jax.ShapeDtypeStruct((bsz, s, D_ATTN), F32), jax.ShapeDtypeStruct((bsz, s, D_ATTN), BF16)),
        in_specs=[sp["smem"], sp["q"], sp["k"], sp["v"], sp["g"], sp["tbl"], sp["tbl"]],
        out_specs=(sp["act"], sp["act"]),
        scratch_shapes=_attn_scratch(s, True),
        compiler_params=_params())(sinks, proj3, proj3, proj3, proj3, cosf, sinf)


def _attn_bwd(proj3, o3, dy3, sinks, cosf, sinf):
    bsz, s, _ = proj3.shape
    nb = s // ATT_BLK
    sp = _attn_specs(s, "pb")

    def body(sink_ref, q_ref, k_ref, v_ref, g_ref, o_ref, dy_ref, cos_ref, sin_ref,
             dq_ref, dk_ref, dv_ref, dg_ref, ds_ref, qt_sc, km_sc, vm_sc, kt_sc, dot_sc, dqt_sc, dk_sc, dv_sc):
        p = pl.program_id(0)
        b = pl.program_id(1)
        _attn_prep(q_ref, k_ref, v_ref, cos_ref, sin_ref, qt_sc, km_sc, vm_sc, kt_sc, "k", nb)
        dk_sc[...] = jnp.zeros_like(dk_sc)
        dv_sc[...] = jnp.zeros_like(dv_sc)

        def gate(n, carry):
            r0 = pl.multiple_of(n * ATT_BLK, ATT_BLK)
            g = g_ref[pl.ds(r0, ATT_BLK), :]
            dy = dy_ref[pl.ds(r0, ATT_BLK), :]
            sg = _sigmoid(g)
            dg_ref[pl.ds(r0, ATT_BLK), :] = (dy * o_ref[pl.ds(r0, ATT_BLK), :] * (sg * (1.0 + g * (1.0 - sg)))).astype(BF16)
            dot_sc[n] = _to_qt(dy * (g * sg)).astype(BF16)
            return carry

        lax.fori_loop(0, nb, gate, 0)
        kv_lane = lax.broadcasted_iota(jnp.int32, (2 * ATT_BLK, LANES), 1) // HEAD_DIM
        kv_row = lax.broadcasted_iota(jnp.int32, (LANES, QT_COLS), 0) // HEAD_DIM

        def blk(n, acc):
            r0 = pl.multiple_of(n * ATT_BLK, ATT_BLK)
            valid = _band_mask_t(n)
            rq = qt_sc[n]
            rd = dot_sc[n]
            kt = kt_sc[:, pl.ds(r0, 2 * ATT_BLK)]
            dvs, dks, dqs, new_acc = [], [], [], []
            for j in range(KV_PER_STEP):
                st = _dot(km_sc[j, pl.ds(r0, 2 * ATT_BLK), :], rq)
                pt, ps = _softmax_cols(st, valid, _sink_row(sink_ref, p * HEADS_PER_STEP + j * Q_PER_KV))
                dpt = _dot(vm_sc[j, pl.ds(r0, 2 * ATT_BLK), :], rd)
                delta = jnp.sum(pt * dpt, axis=0, keepdims=True)
                dst = (pt * (dpt - delta)).astype(BF16)
                new_acc.append(acc[j] + ps * delta)
                dvs.append(_dot_nt(pt.astype(BF16), rd))
                dks.append(_dot_nt(dst, rq))
                dqs.append(_dot(kt, dst))
            dv_sc[pl.ds(r0, 2 * ATT_BLK), :] += jnp.where(kv_lane == 0, dvs[0], dvs[1])
            dk_sc[pl.ds(r0, 2 * ATT_BLK), :] += jnp.where(kv_lane == 0, dks[0], dks[1])
            dqt_sc[n] = jnp.where(kv_row == 0, dqs[0], dqs[1]) * ATTN_SCALE
            return tuple(new_acc)

        acc = lax.fori_loop(0, nb, blk, tuple(jnp.zeros((1, QT_COLS), F32) for _ in range(KV_PER_STEP)))
        lane1 = lax.broadcasted_iota(jnp.int32, (1, LANES), 1)
        dsink = jnp.zeros((1, LANES), F32)
        for j in range(KV_PER_STEP):
            for i in range(Q_PER_KV):
                part = jnp.sum(acc[j][:, i * ATT_BLK:(i + 1) * ATT_BLK], axis=1, keepdims=True)
                dsink = dsink - jnp.where(lane1 == j * Q_PER_KV + i, part, 0.0)

        @pl.when(b == 0)
        def _():
            ds_ref[...] = jnp.zeros_like(ds_ref)

        ds_ref[...] += dsink

        def post(n, carry):
            r0 = pl.multiple_of(n * ATT_BLK, ATT_BLK)
            cs = cos_ref[pl.ds(r0, ATT_BLK), :]
            sn = sin_ref[pl.ds(r0, ATT_BLK), :]
            dq_ref[pl.ds(r0, ATT_BLK), :] = _unrope(_from_qt(dqt_sc[n]), cs, sn).astype(BF16)
            dk_ref[pl.ds(r0, ATT_BLK), :] = _unrope(dk_sc[pl.ds(r0 + ATT_BLK, ATT_BLK), :], cs, sn).astype(BF16)
            dv_ref[pl.ds(r0, ATT_BLK), :] = dv_sc[pl.ds(r0 + ATT_BLK, ATT_BLK), :].astype(BF16)
            return carry

        lax.fori_loop(0, nb, post, 0)

    act = jax.ShapeDtypeStruct((bsz, s, D_ATTN), BF16)
    kvs = jax.ShapeDtypeStruct((bsz, s, D_KV), BF16)
    return pl.pallas_call(
        body, name="attn_bwd", grid=(N_PAIRS, bsz),
        out_shape=(act, kvs, kvs, act, jax.ShapeDtypeStruct((N_PAIRS, 1, LANES), F32)),
        in_specs=[sp["smem"], sp["q1"], sp["k"], sp["v"], sp["g1"], sp["act1"], sp["act1"], sp["tbl"], sp["tbl"]],
        out_specs=(sp["act"], sp["kv"], sp["kv"], sp["act"], pl.BlockSpec((None, 1, LANES), lambda p, b: (p, 0, 0))),
        scratch_shapes=_attn_scratch(s, True) + [pltpu.VMEM((nb, LANES, QT_COLS), BF16),
                                                 pltpu.VMEM((nb, LANES, QT_COLS), F32),
                                                 pltpu.VMEM((s + ATT_BLK, LANES), F32),
                                                 pltpu.VMEM((s + ATT_BLK, LANES), F32)],
        compiler_params=_params())(sinks, proj3, proj3, proj3, proj3, o3, dy3, cosf, sinf)


def _merge_fwd_bwd(x, tgt, y_rnn, y_attn, proj, w_r, w_a, w_o, gf):
    t, d = x.shape
    tm = min(t, 256)

    hw = d // 2

    def body(x_ref, t_ref, yr_ref, ya_ref, mr0_ref, mr1_ref, ma0_ref, ma1_ref, wr_ref, wa_ref, wo_ref, gf_ref,
             dmg_ref, dyr_ref, dya_ref, mg_ref, dx2_ref, dx2b_ref, dpr_ref, dpa_ref, loss_ref, dgf_ref):
        i = pl.program_id(0)
        wr = wr_ref[...]
        wa = wa_ref[...]
        wo = wo_ref[...]
        gfv = gf_ref[...]
        pr = _dot(yr_ref[...], wr)
        pa = _dot(ya_ref[...], wa)
        sr = _sigmoid(jnp.concatenate([mr0_ref[...], mr1_ref[...]], axis=1))
        sa = _sigmoid(jnp.concatenate([ma0_ref[...], ma1_ref[...]], axis=1))
        mb = (sr * pr + sa * pa).astype(BF16)
        mg_ref[...] = mb
        x2 = x_ref[...] + _dot(mb, wo)
        r2 = lax.rsqrt(jnp.mean(x2 * x2, axis=-1, keepdims=True) + NORM_EPS)
        nrm = x2 * r2
        err = nrm * gfv - t_ref[...]
        dy = err * (1.0 / d)
        dn = dy * gfv
        dx2 = r2 * (dn - nrm * jnp.mean(dn * nrm, axis=-1, keepdims=True))
        dx2_ref[...] = dx2
        dx2b = dx2.astype(BF16)
        dx2b_ref[...] = dx2b
        dmerged = _dot_nt(dx2b, wo)
        dpr = (dmerged * sr).astype(BF16)
        dpa = (dmerged * sa).astype(BF16)
        dpr_ref[...] = dpr
        dpa_ref[...] = dpa
        dmg_ref[:, 0:d] = (dmerged * pr * (sr * (1.0 - sr))).astype(BF16)
        dmg_ref[:, d:2 * d] = (dmerged * pa * (sa * (1.0 - sa))).astype(BF16)
        dyr_ref[...] = _dot_nt(dpr, wr)
        dya_ref[...] = _dot_nt(dpa, wa)

        @pl.when(i == 0)
        def _():
            loss_ref[...] = jnp.zeros_like(loss_ref)
            dgf_ref[...] = jnp.zeros_like(dgf_ref)

        loss_ref[...] += jnp.full((1, LANES), 0.5 / d, F32) * jnp.sum(err * err)
        dgf_ref[...] += jnp.sum(dy * nrm, axis=0, keepdims=True)

    tile = pl.BlockSpec((tm, d), lambda i: (i, 0))
    wsp = pl.BlockSpec((d, d), lambda i: (0, 0))

    def gate(col_blk):
        return pl.BlockSpec((tm, hw), lambda i: (i, col_blk))

    fb = jax.ShapeDtypeStruct((t, d), BF16)
    ff = jax.ShapeDtypeStruct((t, d), F32)
    return pl.pallas_call(
        body, name="merge_fwd_bwd", grid=(t // tm,),
        out_shape=(jax.ShapeDtypeStruct((t, 2 * d), BF16), ff, ff, fb, ff, fb, fb, fb,
                   jax.ShapeDtypeStruct((1, LANES), F32), jax.ShapeDtypeStruct((1, d), F32)),
        in_specs=[tile, tile, tile, tile] + [gate(OFF_MERGE_R // hw + j) for j in range(4)] + [
            wsp, wsp, wsp, pl.BlockSpec((1, d), lambda i: (0, 0))],
        out_specs=(pl.BlockSpec((tm, 2 * d), lambda i: (i, 0)), tile, tile, tile, tile, tile, tile, tile,
                   pl.BlockSpec((1, LANES), lambda i: (0, 0)), pl.BlockSpec((1, d), lambda i: (0, 0))),
        compiler_params=_params())(x, tgt, y_rnn, y_attn, proj, proj, proj, proj, w_r, w_a, w_o, gf)


def _local_grads(x, tgt, norm_g, w_in_bm, conv_w, conv_b, lru_w_a, lru_b_a, lru_w_x, lru_b_x, lam, sinks,
                 w_r, w_a, w_o, gf):
    bsz, s, d = x.shape
    t = bsz * s
    x2 = x.reshape(t, d)
    h = _rmsnorm_fwd(x2, norm_g)
    proj = _in_proj(h, w_in_bm)
    proj3 = proj.reshape(bsz, s, D_IN)
    h_lru, y_rnn = _lru_fwd(proj3, conv_w, conv_b, lru_w_a, lru_b_a, lru_w_x, lru_b_x, lam)
    cosf, sinf = _rope_tables(s)
    o_attn, y_attn = _attn_fwd(proj3, sinks, cosf, sinf)
    y_rnn2 = y_rnn.reshape(t, d)
    y_attn2 = y_attn.reshape(t, d)
    dmg, dyr, dya, merged, dx2, dx2b, dpr, dpa, loss, dgf = _merge_fwd_bwd(
        x2, tgt.reshape(t, d), y_rnn2, y_attn2, proj, w_r, w_a, w_o, gf)
    gw_o = _grad_w_sq(merged, dx2b, "grad_w_o")
    gw_r = _grad_w_sq(y_rnn2, dpr, "grad_w_rnn_out")
    gw_a = _grad_w_sq(y_attn2, dpa, "grad_w_attn_out")
    dup, dgr, dcw, dcb, dwa, dba, dwx, dbx, dlam = _lru_bwd(
        proj3, h_lru, dyr.reshape(bsz, s, d), conv_w, conv_b, lru_w_a, lru_b_a, lru_w_x, lru_b_x, lam)
    dq, dk, dv, dga, dsink = _attn_bwd(proj3, o_attn, dya.reshape(bsz, s, d), sinks, cosf, sinf)
    dproj = jnp.concatenate([dup, dgr, dq, dk, dv, dga, dmg.reshape(bsz, s, 2 * d)], axis=-1).reshape(t, D_IN)
    dh = _grad_h(dproj, w_in_bm)
    grad_x, dng = _rmsnorm_bwd(x2, dh, dx2, norm_g)
    small = dict(norm_g=dng, conv_w=dcw, conv_b=dcb, lru_w_a=dwa, lru_b_a=dba, lru_w_x=dwx, lru_b_x=dbx,
                 lru_lambda=dlam, attn_sinks=dsink[:, 0, :HEADS_PER_STEP].reshape(1, N_Q_HEADS), final_norm_g=dgf)
    return loss[0, 0], grad_x.reshape(bsz, s, d), h, dproj, gw_r, gw_a, gw_o, small


ANY = pl.BlockSpec(memory_space=pl.ANY)


def _mesh_pos():
    return lax.axis_index("x"), lax.axis_index("y"), lax.axis_index("c")


def _chip_peers(x, y, c):
    return [((1 - x, y, c), 2 * (1 - x) + y), ((x, 1 - y, c), 2 * x + (1 - y)),
            ((1 - x, 1 - y, c), 2 * (1 - x) + (1 - y))]


def _remote(src, dst, send_sems, recv_sems, idx, peer):
    return pltpu.make_async_remote_copy(src_ref=src, dst_ref=dst, send_sem=send_sems.at[idx],
                                        recv_sem=recv_sems.at[idx], device_id=peer, device_id_type=MESH)


def _all_gather(bufs, split):
    n = len(bufs)
    n_fwd = 3 * sum(split)

    def body(*refs):
        ins, outs = refs[:n], refs[n:2 * n]
        send_sems, recv_sems, fsend_sems, frecv_sems = refs[2 * n:]
        x, y, c = _mesh_pos()
        me = 2 * x + y
        sib = (x, y, 1 - c)
        peers = _chip_peers(x, y, c)

        def part(ref, slot, t, half):
            if not split[t]:
                return ref.at[slot]
            hr = bufs[t].shape[1] // 2
            return ref.at[slot, pl.ds(pl.multiple_of(half * hr, 8), hr), :]

        sends, recvs = [], []
        for t in range(n):
            for k, (peer, pj) in enumerate(peers):
                src = part(ins[t], me, t, c)
                sends.append(_remote(src, part(outs[t], me, t, c), send_sems, recv_sems, 3 * t + k, peer))
                recvs.append(_remote(src, part(outs[t], pj, t, c), send_sems, recv_sems, 3 * t + k, peer))
        for cp in sends:
            cp.start()
        fwd, fwd_recv = [], []
        for t in range(n):
            for k, (peer, pj) in enumerate(peers):
                recvs[3 * t + k].wait_recv()
                if split[t]:
                    got = part(outs[t], pj, t, c)
                    f = len(fwd)
                    fwd.append(_remote(got, got, fsend_sems, frecv_sems, f, sib))
                    fwd_recv.append(_remote(got, part(outs[t], pj, t, 1 - c), fsend_sems, frecv_sems, f, sib))
                    fwd[-1].start()
        for cp in sends:
            cp.wait_send()
        for snd, rcv in zip(fwd, fwd_recv):
            snd.wait_send()
            rcv.wait_recv()

    return pl.pallas_call(
        body, name="all_gather_weights", out_shape=[jax.ShapeDtypeStruct(a.shape, a.dtype) for a in bufs],
        in_specs=[ANY] * n, out_specs=[ANY] * n, input_output_aliases={t: t for t in range(n)},
        scratch_shapes=[pltpu.SemaphoreType.DMA((3 * n,)), pltpu.SemaphoreType.DMA((3 * n,)),
                        pltpu.SemaphoreType.DMA((n_fwd,)), pltpu.SemaphoreType.DMA((n_fwd,))],
        compiler_params=_params())(*bufs)


def _pair_exchange(bigs, small):
    n = len(bigs)

    def body(*refs):
        ins, outs = refs[:n + 1], refs[n + 1:2 * n + 2]
        send_sems, recv_sems = refs[2 * n + 2:]
        x, y, c = _mesh_pos()
        sib = (x, y, 1 - c)
        cps = []
        for t in range(n):
            hr = bigs[t].shape[1] // 2
            src = ins[t].at[:, pl.ds(pl.multiple_of((1 - c) * hr, 8), hr), :]
            cps.append(_remote(src, outs[t], send_sems, recv_sems, t, sib))
        cps.append(_remote(ins[n], outs[n], send_sems, recv_sems, n, sib))
        for cp in cps:
            cp.start()
        for cp in cps:
            cp.wait()

    out_shape = [jax.ShapeDtypeStruct((a.shape[0], a.shape[1] // 2, a.shape[2]), a.dtype) for a in bigs]
    out_shape.append(jax.ShapeDtypeStruct(small.shape, small.dtype))
    return pl.pallas_call(
        body, name="pair_exchange", out_shape=out_shape, in_specs=[ANY] * (n + 1), out_specs=[ANY] * (n + 1),
        scratch_shapes=[pltpu.SemaphoreType.DMA((n + 1,)), pltpu.SemaphoreType.DMA((n + 1,))],
        compiler_params=_params())(*bigs, small)


def _row_tile(rows, row_bytes, cap_bytes=2 * 1024 * 1024):
    best = None
    for tr in range(8, rows + 1, 8):
        if rows % tr == 0 and tr * row_bytes <= cap_bytes:
            best = tr
    return best if best is not None else rows


def _pair_sum(own, got, c_idx, name):
    nblk, r, w = own.shape
    hr = r // 2
    tr = _row_tile(hr, w * 4)
    steps = hr // tr

    def body(c_ref, own_ref, got_ref, f_ref, b_ref):
        sm = own_ref[...] + got_ref[...]
        f_ref[...] = sm
        b_ref[...] = sm.astype(BF16)

    grid_spec = pltpu.PrefetchScalarGridSpec(
        num_scalar_prefetch=1, grid=(nblk, steps),
        in_specs=[pl.BlockSpec((None, tr, w), lambda j, i, c_ref: (j, c_ref[0] * steps + i, 0)),
                  pl.BlockSpec((None, tr, w), lambda j, i, c_ref: (j, i, 0))],
        out_specs=[pl.BlockSpec((None, tr, w), lambda j, i, c_ref: (j, i, 0)),
                   pl.BlockSpec((None, tr, w), lambda j, i, c_ref: (j, i, 0))])
    return pl.pallas_call(
        body, name=name, grid_spec=grid_spec,
        out_shape=[jax.ShapeDtypeStruct((nblk, hr, w), F32), jax.ShapeDtypeStruct((nblk, hr, w), BF16)],
        compiler_params=_params())(c_idx, own, got)


def _add2(a, b, name):
    r, w = a.shape
    tr = _row_tile(r, w * 4)

    def body(a_ref, b_ref, o_ref):
        o_ref[...] = a_ref[...] + b_ref[...]

    spec = pl.BlockSpec((tr, w), lambda i: (i, 0))
    return pl.pallas_call(body, name=name, grid=(r // tr,), out_shape=jax.ShapeDtypeStruct((r, w), F32),
                          in_specs=[spec, spec], out_specs=spec, compiler_params=_params())(a, b)


def _chip_exchange(bigs_b, small_slots):
    n = len(bigs_b)

    def body(*refs):
        ins, outs = refs[:n + 1], refs[n + 1:2 * n + 2]
        send_sems, recv_sems = refs[2 * n + 2:]
        x, y, c = _mesh_pos()
        me = 2 * x + y
        sends, recvs = [], []
        for t in range(n + 1):
            for k, (peer, pj) in enumerate(_chip_peers(x, y, c)):
                src = ins[t].at[pj] if t < n else ins[t].at[me]
                sends.append(_remote(src, outs[t].at[me], send_sems, recv_sems, 3 * t + k, peer))
                recvs.append(_remote(src, outs[t].at[pj], send_sems, recv_sems, 3 * t + k, peer))
        for cp in sends:
            cp.start()
        for snd, rcv in zip(sends, recvs):
            snd.wait_send()
            rcv.wait_recv()

    out_shape = [jax.ShapeDtypeStruct(a.shape, a.dtype) for a in bigs_b + [small_slots]]
    return pl.pallas_call(
        body, name="chip_exchange", out_shape=out_shape, in_specs=[ANY] * (n + 1), out_specs=[ANY] * (n + 1),
        input_output_aliases={n: n},
        scratch_shapes=[pltpu.SemaphoreType.DMA((3 * n + 3,)), pltpu.SemaphoreType.DMA((3 * n + 3,))],
        compiler_params=_params())(*bigs_b, small_slots)


def _chip_sum(own, got, idx, name):
    _, r, w = own.shape
    tr = _row_tile(r, w * 4)
    steps = r // tr

    def body(idx_ref, own_ref, g1_ref, g2_ref, g3_ref, o_ref):
        o_ref[...] = ((own_ref[...] + g1_ref[...].astype(F32)) + g2_ref[...].astype(F32)) + g3_ref[...].astype(F32)

    def pick(k):
        return pl.BlockSpec((None, tr, w), lambda i, idx_ref: (idx_ref[k], i, 0))

    grid_spec = pltpu.PrefetchScalarGridSpec(
        num_scalar_prefetch=1, grid=(steps,), in_specs=[pick(0), pick(1), pick(2), pick(3)],
        out_specs=pl.BlockSpec((tr, w), lambda i, idx_ref: (idx_ref[4] * steps + i, 0)))
    return pl.pallas_call(body, name=name, grid_spec=grid_spec, out_shape=jax.ShapeDtypeStruct((2 * r, w), F32),
                          compiler_params=_params())(idx, own, got, got, got)


def _tree_sum4(a, idx):
    _, r, w = a.shape
    tr = _row_tile(r, w * 4)
    steps = r // tr

    def body(idx_ref, a_ref, o_ref):
        o_ref[...] = (a_ref[0] + a_ref[1]) + (a_ref[2] + a_ref[3])

    grid_spec = pltpu.PrefetchScalarGridSpec(
        num_scalar_prefetch=1, grid=(steps,),
        in_specs=[pl.BlockSpec((N_CHIPS, tr, w), lambda i, idx_ref: (0, i, 0))],
        out_specs=pl.BlockSpec((tr, w), lambda i, idx_ref: (idx_ref[4] * steps + i, 0)))
    return pl.pallas_call(body, name="small_sum", grid_spec=grid_spec, out_shape=jax.ShapeDtypeStruct((2 * r, w), F32),
                          compiler_params=_params())(idx, a)


def _pair_gather(bufs):
    n = len(bufs)

    def body(*refs):
        ins, outs = refs[:n], refs[n:2 * n]
        send_sems, recv_sems = refs[2 * n:]
        x, y, c = _mesh_pos()
        sib = (x, y, 1 - c)
        sends, recvs = [], []
        for t in range(n):
            hr = bufs[t].shape[0] // 2
            mine = pl.ds(pl.multiple_of(c * hr, 8), hr)
            theirs = pl.ds(pl.multiple_of((1 - c) * hr, 8), hr)
            sends.append(_remote(ins[t].at[mine, :], outs[t].at[mine, :], send_sems, recv_sems, t, sib))
            recvs.append(_remote(ins[t].at[mine, :], outs[t].at[theirs, :], send_sems, recv_sems, t, sib))
        for cp in sends:
            cp.start()
        for snd, rcv in zip(sends, recvs):
            snd.wait_send()
            rcv.wait_recv()

    return pl.pallas_call(
        body, name="pair_gather", out_shape=[jax.ShapeDtypeStruct(a.shape, a.dtype) for a in bufs],
        in_specs=[ANY] * n, out_specs=[ANY] * n, input_output_aliases={t: t for t in range(n)},
        scratch_shapes=[pltpu.SemaphoreType.DMA((n,)), pltpu.SemaphoreType.DMA((n,))],
        compiler_params=_params())(*bufs)


def _reduce_scatter(bigs, small, x, y, c):
    n = len(bigs)
    me = 2 * x + y
    got1 = _pair_exchange(bigs, small)
    c_idx = jnp.reshape(c, (1,)).astype(jnp.int32)
    part = [_pair_sum(bigs[t], got1[t], c_idx, f"pair_sum_{t}") for t in range(n)]
    small_pair = _add2(small, got1[n], "pair_sum_small")
    small_slots = _put_slot(small_pair, N_CHIPS, jnp.stack([me, c]).astype(jnp.int32), PK_HALF, F32, "small_slot")
    got2 = _chip_exchange([p[1] for p in part], small_slots)
    others = [jnp.where(me <= k, k + 1, k) for k in range(N_CHIPS - 1)]
    idx = jnp.stack([me] + others + [c]).astype(jnp.int32)
    bufs = [_chip_sum(part[t][0], got2[t], idx, f"chip_sum_{t}") for t in range(n)]
    bufs.append(_tree_sum4(got2[n], idx))
    return _pair_gather(bufs)


def _grad_w_in_reduce_scatter(h, dproj, idx):
    t, d = h.shape
    hr = d // 2
    tk = min(t, 1024)
    nk = t // tk
    last = N_CHIPS - 1

    def body(idx_ref, h_ref, dp_ref, out_ref, acc, xrecv, sbuf, irecv,
             x_send, x_recv, i_send, i_recv, f_send, f_recv, o_sem):
        s, hf, k = pl.program_id(0), pl.program_id(1), pl.program_id(2)
        x, y, c = _mesh_pos()
        sib = (x, y, 1 - c)
        peers = [(x, 1 - y, c), (1 - x, y, c), (1 - x, 1 - y, c)]
        slot = s % 2

        @pl.when(k == 0)
        def _():
            acc[hf] = jnp.zeros((hr, W_BLK), F32)

        acc[hf] += _dot_tn(h_ref[...], dp_ref[...])

        def to_sibling():
            return _remote(acc.at[0], xrecv.at[slot], x_send, x_recv, slot, sib)

        @pl.when((hf == 0) & (k == nk - 1))
        def _():
            to_sibling().start()

        @pl.when((hf == 1) & (k == nk - 1))
        def _():
            to_sibling().wait_recv()
            to_sibling().wait_send()
            pair = acc[1] + xrecv[slot]
            for ss in range(last):
                @pl.when(s == ss)
                def _():
                    sbuf[ss] = pair.astype(BF16)
                    _remote(sbuf.at[ss], irecv.at[ss], i_send, i_recv, ss, peers[ss]).start()

            @pl.when(s == last)
            def _():
                total = pair
                for ss in range(last):
                    _remote(sbuf.at[ss], irecv.at[ss], i_send, i_recv, ss, peers[ss]).wait_recv()
                    total = total + irecv[ss].astype(F32)
                acc[1] = total
                mine = pl.ds(pl.multiple_of(c * hr, 8), hr)
                theirs = pl.ds(pl.multiple_of((1 - c) * hr, 8), hr)
                keep = pltpu.make_async_copy(acc.at[1], out_ref.at[mine, :], o_sem)
                give = pltpu.make_async_remote_copy(src_ref=acc.at[1], dst_ref=out_ref.at[mine, :], send_sem=f_send,
                                                    recv_sem=f_recv, device_id=sib, device_id_type=MESH)
                take = pltpu.make_async_remote_copy(src_ref=acc.at[1], dst_ref=out_ref.at[theirs, :], send_sem=f_send,
                                                    recv_sem=f_recv, device_id=sib, device_id_type=MESH)
                keep.start()
                give.start()
                for ss in range(last):
                    _remote(sbuf.at[ss], irecv.at[ss], i_send, i_recv, ss, peers[ss]).wait_send()
                keep.wait()
                give.wait_send()
                take.wait_recv()

    grid_spec = pltpu.PrefetchScalarGridSpec(
        num_scalar_prefetch=1, grid=(N_CHIPS, 2, nk),
        in_specs=[pl.BlockSpec((tk, hr), lambda s, hf, k, idx_ref: (k, (1 - idx_ref[1] + hf) % 2)),
                  pl.BlockSpec((tk, W_BLK), lambda s, hf, k, idx_ref: (k, idx_ref[0] ^ ((s + 1) % N_CHIPS)))],
        out_specs=ANY,
        scratch_shapes=[pltpu.VMEM((2, hr, W_BLK), F32), pltpu.VMEM((2, hr, W_BLK), F32),
                        pltpu.VMEM((last, hr, W_BLK), BF16), pltpu.VMEM((last, hr, W_BLK), BF16),
                        pltpu.SemaphoreType.DMA((2,)), pltpu.SemaphoreType.DMA((2,)),
                        pltpu.SemaphoreType.DMA((last,)), pltpu.SemaphoreType.DMA((last,)),
                        pltpu.SemaphoreType.DMA, pltpu.SemaphoreType.DMA, pltpu.SemaphoreType.DMA])
    return pl.pallas_call(body, name="grad_w_in_reduce_scatter", grid_spec=grid_spec,
                          out_shape=jax.ShapeDtypeStruct((d, W_BLK), F32), compiler_params=_params())(idx, h, dproj)


_VEC_NAMES = ("norm_g", "conv_b", "lru_b_a", "lru_b_x", "lru_lambda", "final_norm_g")


def _pack_small(p, conv_full=None):
    rows = [p["lru_w_a"].reshape(PK_WX - PK_WA, LANES), p["lru_w_x"].reshape(PK_VEC - PK_WX, LANES)]
    rows += [p[k].reshape(8, LANES) for k in _VEC_NAMES]
    rows.append(jnp.pad(p["attn_sinks"].reshape(1, N_Q_HEADS), ((0, 7), (0, LANES - N_Q_HEADS))))
    tail = PK_ROWS - PK_CONV
    if conv_full is None:
        rows.append(jnp.zeros((tail, LANES), F32))
    else:
        rows.append(conv_full.reshape(32, LANES))
        rows.append(jnp.zeros((tail - 32, LANES), F32))
    return jnp.concatenate(rows, axis=0)


def _unpack_small(pk, like):
    out = {"lru_w_a": pk[PK_WA:PK_WX].reshape(like["lru_w_a"].shape),
           "lru_w_x": pk[PK_WX:PK_VEC].reshape(like["lru_w_x"].shape)}
    for j, k in enumerate(_VEC_NAMES):
        out[k] = pk[PK_VEC + 8 * j:PK_VEC + 8 * j + 8].reshape(like[k].shape)
    out["attn_sinks"] = pk[PK_SINK:PK_SINK + 1, :N_Q_HEADS].reshape(like["attn_sinks"].shape)
    return out


_WEIGHTS = ("norm_g", "w_in", "conv_w", "conv_b", "lru_w_a", "lru_b_a", "lru_w_x", "lru_b_x", "lru_lambda",
            "attn_sinks", "w_rnn_out", "w_attn_out", "w_o", "final_norm_g")
_SMALL = ("norm_g", "conv_b", "lru_w_a", "lru_b_a", "lru_w_x", "lru_b_x", "lru_lambda", "attn_sinks", "final_norm_g")
_ROW_SHARDED = ("w_rnn_out", "w_attn_out", "w_o")


def kernel(x, norm_g, w_in, conv_w, conv_b, lru_w_a, lru_b_a, lru_w_x, lru_b_x, lru_lambda, attn_sinks, w_rnn_out, w_attn_out, w_o, final_norm_g, loss_target, m_norm_g, m_w_in, m_conv_w, m_conv_b, m_lru_w_a, m_lru_b_a, m_lru_w_x, m_lru_b_x, m_lru_lambda, m_attn_sinks, m_w_rnn_out, m_w_attn_out, m_w_o, m_final_norm_g, v_norm_g, v_w_in, v_conv_w, v_conv_b, v_lru_w_a, v_lru_b_a, v_lru_w_x, v_lru_b_x, v_lru_lambda, v_attn_sinks, v_w_rnn_out, v_w_attn_out, v_w_o, v_final_norm_g):
    w = dict(norm_g=norm_g, w_in=w_in, conv_w=conv_w, conv_b=conv_b, lru_w_a=lru_w_a, lru_b_a=lru_b_a, lru_w_x=lru_w_x,
             lru_b_x=lru_b_x, lru_lambda=lru_lambda, attn_sinks=attn_sinks, w_rnn_out=w_rnn_out, w_attn_out=w_attn_out,
             w_o=w_o, final_norm_g=final_norm_g)
    m = dict(norm_g=m_norm_g, w_in=m_w_in, conv_w=m_conv_w, conv_b=m_conv_b, lru_w_a=m_lru_w_a, lru_b_a=m_lru_b_a,
             lru_w_x=m_lru_w_x, lru_b_x=m_lru_b_x, lru_lambda=m_lru_lambda, attn_sinks=m_attn_sinks,
             w_rnn_out=m_w_rnn_out, w_attn_out=m_w_attn_out, w_o=m_w_o, final_norm_g=m_final_norm_g)
    v = dict(norm_g=v_norm_g, w_in=v_w_in, conv_w=v_conv_w, conv_b=v_conv_b, lru_w_a=v_lru_w_a, lru_b_a=v_lru_b_a,
             lru_w_x=v_lru_w_x, lru_b_x=v_lru_b_x, lru_lambda=v_lru_lambda, attn_sinks=v_attn_sinks,
             w_rnn_out=v_w_rnn_out, w_attn_out=v_w_attn_out, w_o=v_w_o, final_norm_g=v_final_norm_g)
    mx, my, mc = _mesh_pos()
    me = 2 * mx + my
    d = D_MODEL

    slot0 = jnp.stack([me, jnp.zeros_like(me)]).astype(jnp.int32)
    bufs = [_put_slot(w[k][0], N_CHIPS, slot0, w[k].shape[1], BF16, "cast_" + k) for k in ("w_in",) + _ROW_SHARDED]
    bufs.append(_put_slot(w["conv_w"][0], N_CHIPS, slot0, CONV_WIDTH, F32, "slot_conv_w"))
    g_in, g_r, g_a, g_o, g_cw = _all_gather(bufs, [True, True, True, True, False])
    conv_full = g_cw.transpose(1, 0, 2).reshape(CONV_WIDTH, D_RNN)

    loss_local, grad_x, h, dproj, gw_r, gw_a, gw_o, gsmall = _local_grads(
        x, loss_target, w["norm_g"], g_in, conv_full, w["conv_b"], w["lru_w_a"][0], w["lru_b_a"], w["lru_w_x"][0],
        w["lru_b_x"], w["lru_lambda"], w["attn_sinks"][0], g_r.reshape(d, d), g_a.reshape(d, d), g_o.reshape(d, d),
        w["final_norm_g"].reshape(1, d))
    loss = lax.psum(loss_local, ("x", "y", "c"))

    gpack = _pack_small(gsmall, gsmall["conv_w"])
    bigs = [g.reshape(N_CHIPS, ROW_BLK, d) for g in (gw_r, gw_a, gw_o)]
    f_r, f_a, f_o, spack = _reduce_scatter(bigs, gpack, mx, my, mc)
    f_in = _grad_w_in_reduce_scatter(h, dproj, jnp.stack([me, mc]).astype(jnp.int32))

    grads = _unpack_small(spack, w)
    conv_all = spack[PK_CONV:PK_CONV + 32].reshape(CONV_WIDTH, D_RNN)
    grads["conv_w"] = lax.dynamic_slice_in_dim(conv_all, me * (D_RNN // N_CHIPS), D_RNN // N_CHIPS, axis=1)[None]
    grads["w_in"] = f_in[None]
    grads["w_rnn_out"], grads["w_attn_out"], grads["w_o"] = f_r[None], f_a[None], f_o[None]

    delta, new_m, new_v = {}, {}, {}
    for k in ("w_in",) + _ROW_SHARDED:
        dk, mk, vk = _adamw(w[k][0], grads[k][0], m[k][0], v[k][0], "adamw_" + k)
        delta[k], new_m[k], new_v[k] = dk[None], mk[None], vk[None]
    shp = (2 * CONV_WIDTH, LANES)
    dk, mk, vk = _adamw(w["conv_w"].reshape(shp), grads["conv_w"].reshape(shp), m["conv_w"].reshape(shp),
                        v["conv_w"].reshape(shp), "adamw_conv_w")
    delta["conv_w"], new_m["conv_w"], new_v["conv_w"] = (a.reshape(w["conv_w"].shape) for a in (dk, mk, vk))
    dk, mk, vk = _adamw(_pack_small(w), spack, _pack_small(m), _pack_small(v), "adamw_small")
    for src, dst in ((dk, delta), (mk, new_m), (vk, new_v)):
        dst.update(_unpack_small(src, w))

    return (loss, grad_x, *[grads[k] for k in _WEIGHTS], *[delta[k] for k in _WEIGHTS],
            *[new_m[k] for k in _WEIGHTS], *[new_v[k] for k in _WEIGHTS])
```

```python
import functools
import math

import jax
import jax.numpy as jnp
from jax import lax
from jax.experimental import pallas as pl
from jax.experimental.pallas import tpu as pltpu

F32 = jnp.float32
BF16 = jnp.bfloat16
MESH = pl.DeviceIdType.MESH

D_MODEL = 1024
D_RNN = 1024
N_RNN_BLOCKS = 8
RNN_BLOCK = D_RNN // N_RNN_BLOCKS
CONV_WIDTH = 4
LRU_C = 8.0
HEAD_DIM = 64
N_Q_HEADS = 16
N_KV_HEADS = 4
D_ATTN = N_Q_HEADS * HEAD_DIM
D_KV = N_KV_HEADS * HEAD_DIM
WINDOW = 128
ROPE_DIM = HEAD_DIM // 4
ROPE_THETA = 500000.0
NORM_EPS = 1e-6
OFF_RNN_X = 0
OFF_RNN_G = OFF_RNN_X + D_RNN
OFF_Q = OFF_RNN_G + D_RNN
OFF_K = OFF_Q + D_ATTN
OFF_V = OFF_K + D_KV
OFF_ATTN_G = OFF_V + D_KV
OFF_MERGE_R = OFF_ATTN_G + D_ATTN
OFF_MERGE_A = OFF_MERGE_R + D_MODEL
D_IN = OFF_MERGE_A + D_MODEL

ADAM_LR = 0.001
ADAM_B1 = 0.9
ADAM_B2 = 0.999
ADAM_EPS = 1e-08
ADAM_WD = 0.01
ADAM_STEP = 10

N_CHIPS = 4
W_BLK = D_IN // N_CHIPS
ROW_BLK = D_MODEL // N_CHIPS
LANES = 128
ATT_BLK = 128
VMEM_LIMIT = 56 * 1024 * 1024
NEG_BIG = -1e30
ATTN_SCALE = 1.0 / math.sqrt(HEAD_DIM)

PK_WA = 0
PK_WX = PK_WA + N_RNN_BLOCKS * RNN_BLOCK
PK_VEC = PK_WX + N_RNN_BLOCKS * RNN_BLOCK
PK_SINK = PK_VEC + 6 * 8
PK_CONV = PK_SINK + 8
PK_ROWS = PK_CONV + 32 + 8
PK_HALF = PK_ROWS // 2


def _params(**kw):
    return pltpu.CompilerParams(vmem_limit_bytes=VMEM_LIMIT, **kw)


def _sigmoid(z):
    return 1.0 / (1.0 + jnp.exp(-z))


def _dot(a, b):
    return jnp.dot(a, b, preferred_element_type=F32)


def _dot_nt(a, b):
    return lax.dot_general(a, b, (((1,), (1,)), ((), ())), preferred_element_type=F32)


def _dot_tn(a, b):
    return lax.dot_general(a, b, (((0,), (0,)), ((), ())), preferred_element_type=F32)


def _put_slot(src, n_slots, slot_and_blk, rows, dtype, name):
    _, c = src.shape
    tr = _row_tile(rows, c * 4)
    steps = rows // tr

    def body(idx_ref, s_ref, o_ref):
        o_ref[...] = s_ref[...].astype(dtype)

    grid_spec = pltpu.PrefetchScalarGridSpec(
        num_scalar_prefetch=1, grid=(steps,),
        in_specs=[pl.BlockSpec((tr, c), lambda i, idx_ref: (idx_ref[1] * steps + i, 0))],
        out_specs=pl.BlockSpec((None, tr, c), lambda i, idx_ref: (idx_ref[0], i, 0)))
    return pl.pallas_call(body, name=name, grid_spec=grid_spec,
                          out_shape=jax.ShapeDtypeStruct((n_slots, rows, c), dtype),
                          compiler_params=_params())(slot_and_blk, src)


def _rmsnorm_fwd(x, g):
    t, d = x.shape
    tm = min(t, 512)

    def body(x_ref, g_ref, o_ref):
        xv = x_ref[...]
        r = lax.rsqrt(jnp.mean(xv * xv, axis=-1, keepdims=True) + NORM_EPS)
        o_ref[...] = (xv * r * g_ref[...]).astype(BF16)

    return pl.pallas_call(
        body, name="rmsnorm_fwd", grid=(t // tm,), out_shape=jax.ShapeDtypeStruct((t, d), BF16),
        in_specs=[pl.BlockSpec((tm, d), lambda i: (i, 0)), pl.BlockSpec((1, d), lambda i: (0, 0))],
        out_specs=pl.BlockSpec((tm, d), lambda i: (i, 0)), compiler_params=_params())(x, g)


def _rmsnorm_bwd(x, dh, dx2, g):
    t, d = x.shape
    tm = min(t, 512)

    def body(x_ref, dh_ref, dx2_ref, g_ref, gx_ref, dg_ref):
        i = pl.program_id(0)
        xv = x_ref[...]
        dhv = dh_ref[...]
        r = lax.rsqrt(jnp.mean(xv * xv, axis=-1, keepdims=True) + NORM_EPS)
        nrm = xv * r
        dn = dhv * g_ref[...]
        gx_ref[...] = dx2_ref[...] + r * (dn - nrm * jnp.mean(dn * nrm, axis=-1, keepdims=True))

        @pl.when(i == 0)
        def _():
            dg_ref[...] = jnp.zeros_like(dg_ref)

        dg_ref[...] += jnp.sum(dhv * nrm, axis=0, keepdims=True)

    return pl.pallas_call(
        body, name="rmsnorm_bwd", grid=(t // tm,),
        out_shape=(jax.ShapeDtypeStruct((t, d), F32), jax.ShapeDtypeStruct((1, d), F32)),
        in_specs=[pl.BlockSpec((tm, d), lambda i: (i, 0)), pl.BlockSpec((tm, d), lambda i: (i, 0)),
                  pl.BlockSpec((tm, d), lambda i: (i, 0)), pl.BlockSpec((1, d), lambda i: (0, 0))],
        out_specs=(pl.BlockSpec((tm, d), lambda i: (i, 0)), pl.BlockSpec((1, d), lambda i: (0, 0))),
        compiler_params=_params())(x, dh, dx2, g)


def _adamw(w, g, m, v, name):
    r, c = w.shape
    tr = _row_tile(r, c * 4, 1024 * 1024)
    c1 = 1.0 - ADAM_B1 ** ADAM_STEP
    c2 = 1.0 - ADAM_B2 ** ADAM_STEP

    def body(w_ref, g_ref, m_ref, v_ref, d_ref, nm_ref, nv_ref):
        gv = g_ref[...]
        nm = ADAM_B1 * m_ref[...] + (1.0 - ADAM_B1) * gv
        nv = ADAM_B2 * v_ref[...] + (1.0 - ADAM_B2) * (gv * gv)
        m_hat = nm / c1
        v_hat = nv / c2
        d_ref[...] = -ADAM_LR * (m_hat / (jnp.sqrt(v_hat) + ADAM_EPS) + ADAM_WD * w_ref[...])
        nm_ref[...] = nm
        nv_ref[...] = nv

    spec = pl.BlockSpec((tr, c), lambda i: (i, 0))
    sds = jax.ShapeDtypeStruct((r, c), F32)
    return pl.pallas_call(
        body, name=name, grid=(r // tr,), out_shape=(sds, sds, sds),
        in_specs=[spec, spec, spec, spec], out_specs=(spec, spec, spec), compiler_params=_params())(w, g, m, v)


def _in_proj(h, w_bm):
    t, d = h.shape
    nb, _, wb = w_bm.shape
    tm = min(t, 512)

    def body(h_ref, w_ref, o_ref):
        o_ref[...] = _dot(h_ref[...], w_ref[...])

    return pl.pallas_call(
        body, name="in_proj", grid=(nb, t // tm), out_shape=jax.ShapeDtypeStruct((t, nb * wb), F32),
        in_specs=[pl.BlockSpec((tm, d), lambda j, i: (i, 0)), pl.BlockSpec((None, d, wb), lambda j, i: (j, 0, 0))],
        out_specs=pl.BlockSpec((tm, wb), lambda j, i: (i, j)), compiler_params=_params())(h, w_bm)


def _grad_h(dproj, w_bm):
    t = dproj.shape[0]
    nb, d, wb = w_bm.shape
    tm = min(t, 1024)

    def body(dp_ref, w_ref, o_ref, acc_ref):
        k = pl.program_id(1)

        @pl.when(k == 0)
        def _():
            acc_ref[...] = jnp.zeros_like(acc_ref)

        acc_ref[...] += _dot_nt(dp_ref[...], w_ref[...])

        @pl.when(k == nb - 1)
        def _():
            o_ref[...] = acc_ref[...]

    return pl.pallas_call(
        body, name="grad_h", grid=(t // tm, nb), out_shape=jax.ShapeDtypeStruct((t, d), F32),
        in_specs=[pl.BlockSpec((tm, wb), lambda i, k: (i, k)), pl.BlockSpec((None, d, wb), lambda i, k: (k, 0, 0))],
        out_specs=pl.BlockSpec((tm, d), lambda i, k: (i, 0)),
        scratch_shapes=[pltpu.VMEM((tm, d), F32)], compiler_params=_params())(dproj, w_bm)


def _shift_down(v, d, fill):
    n = v.shape[0]
    if d % 8 == 0:
        return jnp.concatenate([jnp.full((d,) + v.shape[1:], fill, v.dtype), v[: n - d]], axis=0)
    row = lax.broadcasted_iota(jnp.int32, v.shape, 0)
    return jnp.where(row >= d, pltpu.roll(v, d, axis=0), fill)


def _shift_up(v, d, fill):
    n = v.shape[0]
    if d % 8 == 0:
        return jnp.concatenate([v[d:], jnp.full((d,) + v.shape[1:], fill, v.dtype)], axis=0)
    row = lax.broadcasted_iota(jnp.int32, v.shape, 0)
    return jnp.where(row < n - d, pltpu.roll(v, n - d, axis=0), fill)


def _scan(a, b, shift):
    n = a.shape[0]
    d = 1
    while d < n:
        b = a * shift(b, d, 0.0) + b
        if 2 * d < n:
            a = a * shift(a, d, 1.0)
        d *= 2
    return b


def _neg_expm1(y):
    series = -y * (1.0 + y * (1.0 / 2.0) * (1.0 + y * (1.0 / 3.0) * (1.0 + y * (1.0 / 4.0) * (
        1.0 + y * (1.0 / 5.0) * (1.0 + y * (1.0 / 6.0) * (1.0 + y * (1.0 / 7.0)))))))
    return jnp.where(y > -0.25, series, 1.0 - jnp.exp(y))


def _softplus(z):
    e = jnp.exp(-jnp.abs(z))
    w = 1.0 + e
    log1p = jnp.where(w == 1.0, e, jnp.log(w) * (e / jnp.where(w == 1.0, 1.0, w - 1.0)))
    return jnp.maximum(z, 0.0) + log1p


def _conv(up, cw, cb):
    out = cb + cw[CONV_WIDTH - 1:CONV_WIDTH, :] * up
    for j in range(CONV_WIDTH - 1):
        out = out + cw[j:j + 1, :] * _shift_down(up, CONV_WIDTH - 1 - j, 0.0)
    return out


def _lru_gates(u, wa_ref, ba_ref, wx_ref, bx_ref, lam_ref):
    ub = u.astype(BF16)
    r = _sigmoid(_dot(ub, wa_ref[...].astype(BF16)) + ba_ref[...])
    i = _sigmoid(_dot(ub, wx_ref[...].astype(BF16)) + bx_ref[...])
    sp = _softplus(-lam_ref[...])
    log_a = (-LRU_C) * r * sp
    a = jnp.exp(log_a)
    mult = jnp.sqrt(_neg_expm1(2.0 * log_a))
    return r, i, sp, a, mult


def _lru_specs(s):
    cb = RNN_BLOCK
    vec = pl.BlockSpec((1, cb), lambda n, b: (0, n))
    return dict(
        up=pl.BlockSpec((None, s, cb), lambda n, b: (b, 0, OFF_RNN_X // cb + n)),
        gr=pl.BlockSpec((None, s, cb), lambda n, b: (b, 0, OFF_RNN_G // cb + n)),
        act=pl.BlockSpec((None, s, cb), lambda n, b: (b, 0, n)),
        cw=pl.BlockSpec((CONV_WIDTH, cb), lambda n, b: (0, n)),
        vec=vec,
        wblk=pl.BlockSpec((None, cb, cb), lambda n, b: (n, 0, 0)),
    )


def _lru_fwd(proj3, cw, cb, wa, ba, wx, bx, lam):
    bsz, s, _ = proj3.shape
    sp = _lru_specs(s)

    def body(up_ref, gr_ref, cw_ref, cb_ref, wa_ref, ba_ref, wx_ref, bx_ref, lam_ref, h_ref, y_ref):
        u = _conv(up_ref[...], cw_ref[...], cb_ref[...])
        _, i, _, a, mult = _lru_gates(u, wa_ref, ba_ref, wx_ref, bx_ref, lam_ref)
        h = _scan(a, mult * (i * u), _shift_down)
        h_ref[...] = h
        g = gr_ref[...]
        y_ref[...] = (h * (g * _sigmoid(g))).astype(BF16)

    return pl.pallas_call(
        body, name="lru_fwd", grid=(N_RNN_BLOCKS, bsz),
        out_shape=(jax.ShapeDtypeStruct((bsz, s, D_RNN), F32), jax.ShapeDtypeStruct((bsz, s, D_RNN), BF16)),
        in_specs=[sp["up"], sp["gr"], sp["cw"], sp["vec"], sp["wblk"], sp["vec"], sp["wblk"], sp["vec"], sp["vec"]],
        out_specs=(sp["act"], sp["act"]), compiler_params=_params())(proj3, proj3, cw, cb, wa, ba, wx, bx, lam)


def _lru_bwd(proj3, h3, dy3, cw, cb, wa, ba, wx, bx, lam):
    bsz, s, _ = proj3.shape
    sp = _lru_specs(s)

    def body(up_ref, gr_ref, h_ref, dy_ref, cw_ref, cb_ref, wa_ref, ba_ref, wx_ref, bx_ref, lam_ref,
             dup_ref, dgr_ref, dcw_ref, dcb_ref, dwa_ref, dba_ref, dwx_ref, dbx_ref, dlam_ref):
        b = pl.program_id(1)
        up = up_ref[...]
        cwv = cw_ref[...]
        u = _conv(up, cwv, cb_ref[...])
        r, i, spv, a, mult = _lru_gates(u, wa_ref, ba_ref, wx_ref, bx_ref, lam_ref)
        h = h_ref[...]
        g = gr_ref[...]
        dy = dy_ref[...]
        sg = _sigmoid(g)
        dgr_ref[...] = (dy * h * (sg * (1.0 + g * (1.0 - sg)))).astype(BF16)
        dh = dy * (g * sg)
        adj = _scan(_shift_up(a, 1, 0.0), dh, _shift_up)
        da = adj * _shift_down(h, 1, 0.0)
        dmult = adj * (i * u)
        di = adj * mult * u
        du = adj * mult * i
        dla = da * a - dmult * (a * a) / mult
        dr = dla * ((-LRU_C) * spv)
        dsp = jnp.sum(dla * ((-LRU_C) * r), axis=0, keepdims=True)
        dza = dr * r * (1.0 - r)
        dzx = di * i * (1.0 - i)
        ub = u.astype(BF16)
        dzab = dza.astype(BF16)
        dzxb = dzx.astype(BF16)
        du = du + _dot_nt(dzab, wa_ref[...].astype(BF16)) + _dot_nt(dzxb, wx_ref[...].astype(BF16))
        dup = cwv[CONV_WIDTH - 1:CONV_WIDTH, :] * du
        for j in range(CONV_WIDTH - 1):
            dup = dup + cwv[j:j + 1, :] * _shift_up(du, CONV_WIDTH - 1 - j, 0.0)
        dup_ref[...] = dup.astype(BF16)

        @pl.when(b == 0)
        def _():
            for ref in (dcw_ref, dcb_ref, dwa_ref, dba_ref, dwx_ref, dbx_ref, dlam_ref):
                ref[...] = jnp.zeros_like(ref)

        rows = [jnp.sum(du * _shift_down(up, CONV_WIDTH - 1 - j, 0.0), axis=0, keepdims=True)
                for j in range(CONV_WIDTH - 1)]
        rows.append(jnp.sum(du * up, axis=0, keepdims=True))
        dcw_ref[...] += jnp.concatenate(rows, axis=0)
        dcb_ref[...] += jnp.sum(du, axis=0, keepdims=True)
        dwa_ref[...] += _dot_tn(ub, dzab)
        dba_ref[...] += jnp.sum(dza, axis=0, keepdims=True)
        dwx_ref[...] += _dot_tn(ub, dzxb)
        dbx_ref[...] += jnp.sum(dzx, axis=0, keepdims=True)
        dlam_ref[...] += dsp * (-_sigmoid(-lam_ref[...]))

    act_b = jax.ShapeDtypeStruct((bsz, s, D_RNN), BF16)
    vec = jax.ShapeDtypeStruct((1, D_RNN), F32)
    wsd = jax.ShapeDtypeStruct((N_RNN_BLOCKS, RNN_BLOCK, RNN_BLOCK), F32)
    return pl.pallas_call(
        body, name="lru_bwd", grid=(N_RNN_BLOCKS, bsz),
        out_shape=(act_b, act_b, jax.ShapeDtypeStruct((CONV_WIDTH, D_RNN), F32), vec, wsd, vec, wsd, vec, vec),
        in_specs=[sp["up"], sp["gr"], sp["act"], sp["act"], sp["cw"], sp["vec"], sp["wblk"], sp["vec"],
                  sp["wblk"], sp["vec"], sp["vec"]],
        out_specs=(sp["act"], sp["act"], sp["cw"], sp["vec"], sp["wblk"], sp["vec"], sp["wblk"], sp["vec"], sp["vec"]),
        compiler_params=_params())(proj3, proj3, h3, dy3, cw, cb, wa, ba, wx, bx, lam)


def _rope_tables(s):
    half = ROPE_DIM // 2
    pos = jnp.arange(s, dtype=F32)
    inv_freq = ROPE_THETA ** (-jnp.arange(0, ROPE_DIM, 2, dtype=F32) / ROPE_DIM)
    ang = pos[:, None] * inv_freq[None, :]
    cos, sin = jnp.cos(ang), jnp.sin(ang)
    rest = HEAD_DIM - ROPE_DIM
    cos64 = jnp.concatenate([cos, cos, jnp.ones((s, rest), F32)], axis=1)
    sin64 = jnp.concatenate([-sin, sin, jnp.zeros((s, rest), F32)], axis=1)
    assert half * 2 == ROPE_DIM
    return jnp.tile(cos64, (1, LANES // HEAD_DIM)), jnp.tile(sin64, (1, LANES // HEAD_DIM))


def _swap_rot_halves(v):
    half = ROPE_DIM // 2
    lane = lax.broadcasted_iota(jnp.int32, v.shape, 1) % HEAD_DIM
    second = jnp.where(lane < ROPE_DIM, pltpu.roll(v, half, axis=1), 0.0)
    return jnp.where(lane < half, pltpu.roll(v, LANES - half, axis=1), second)


def _rope(v, cos, sin):
    tiles = []
    for t in range(v.shape[1] // LANES):
        vt = v[:, t * LANES:(t + 1) * LANES]
        tiles.append(vt * cos + _swap_rot_halves(vt) * sin)
    return tiles[0] if len(tiles) == 1 else jnp.concatenate(tiles, axis=1)


def _unrope(v, cos, sin):
    tiles = []
    for t in range(v.shape[1] // LANES):
        vt = v[:, t * LANES:(t + 1) * LANES]
        tiles.append(vt * cos + _swap_rot_halves(vt * sin))
    return tiles[0] if len(tiles) == 1 else jnp.concatenate(tiles, axis=1)


HEADS_PER_STEP = 8
QW = HEADS_PER_STEP * HEAD_DIM
N_PAIRS = N_Q_HEADS // HEADS_PER_STEP
Q_PER_KV = N_Q_HEADS // N_KV_HEADS
KV_PER_STEP = HEADS_PER_STEP // Q_PER_KV


QT_COLS = Q_PER_KV * ATT_BLK


def _attn_scratch(s, with_vt):
    nb = s // ATT_BLK
    pad = s + ATT_BLK
    shapes = [pltpu.VMEM((nb, LANES, QT_COLS), BF16),
              pltpu.VMEM((KV_PER_STEP, pad, LANES), BF16),
              pltpu.VMEM((KV_PER_STEP, pad, LANES), BF16)]
    if with_vt:
        shapes.append(pltpu.VMEM((LANES, pad), BF16))
    return shapes


def _attn_specs(s, order):
    def mk(width, base, **kw):
        if order == "bp":
            return pl.BlockSpec((None, s, width), lambda b, p: (b, 0, base + p), **kw)
        return pl.BlockSpec((None, s, width), lambda p, b: (b, 0, base + p), **kw)
    one = dict(pipeline_mode=pl.Buffered(1))
    tbl = pl.BlockSpec((s, LANES), lambda *_: (0, 0))
    return dict(q=mk(QW, OFF_Q // QW), k=mk(LANES, OFF_K // LANES), v=mk(LANES, OFF_V // LANES),
                g=mk(QW, OFF_ATTN_G // QW), act=mk(QW, 0), kv=mk(LANES, 0), tbl=tbl,
                q1=mk(QW, OFF_Q // QW, **one), g1=mk(QW, OFF_ATTN_G // QW, **one), act1=mk(QW, 0, **one),
                smem=pl.BlockSpec(memory_space=pltpu.SMEM))


def _to_qt(blk):
    rows = []
    for j in range(KV_PER_STEP):
        cols = []
        for tt in range(2):
            t = 2 * j + tt
            tr = blk[:, t * LANES:(t + 1) * LANES].T
            cols += [tr[0:HEAD_DIM, :], tr[HEAD_DIM:, :]]
        rows.append(jnp.concatenate(cols, axis=1))
    return jnp.concatenate(rows, axis=0)


def _from_qt(xt):
    tiles = []
    for j in range(KV_PER_STEP):
        for tt in range(2):
            g0 = 2 * tt
            pair = jnp.concatenate([xt[j * HEAD_DIM:(j + 1) * HEAD_DIM, (g0 + i) * ATT_BLK:(g0 + i + 1) * ATT_BLK]
                                    for i in range(2)], axis=0)
            tiles.append(pair.T)
    return jnp.concatenate(tiles, axis=1)


def _attn_prep(q_ref, k_ref, v_ref, cos_ref, sin_ref, qt_sc, km_sc, vm_sc, t_sc, transposed, nb):
    zeros = jnp.zeros((ATT_BLK, LANES), BF16)
    for j in range(KV_PER_STEP):
        km_sc[j, 0:ATT_BLK, :] = zeros
        vm_sc[j, 0:ATT_BLK, :] = zeros
    t_sc[:, 0:ATT_BLK] = zeros
    head_of_lane = lax.broadcasted_iota(jnp.int32, (ATT_BLK, LANES), 1) // HEAD_DIM

    def prep(n, carry):
        r0 = pl.multiple_of(n * ATT_BLK, ATT_BLK)
        cs = cos_ref[pl.ds(r0, ATT_BLK), :]
        sn = sin_ref[pl.ds(r0, ATT_BLK), :]
        qt_sc[n] = _to_qt(_rope(q_ref[pl.ds(r0, ATT_BLK), :], cs, sn) * ATTN_SCALE).astype(BF16)
        k = _rope(k_ref[pl.ds(r0, ATT_BLK), :], cs, sn)
        v = v_ref[pl.ds(r0, ATT_BLK), :]
        for j in range(KV_PER_STEP):
            km_sc[j, pl.ds(r0 + ATT_BLK, ATT_BLK), :] = jnp.where(head_of_lane == j, k, 0.0).astype(BF16)
            vm_sc[j, pl.ds(r0 + ATT_BLK, ATT_BLK), :] = jnp.where(head_of_lane == j, v, 0.0).astype(BF16)
        t_sc[:, pl.ds(r0 + ATT_BLK, ATT_BLK)] = (k if transposed == "k" else v).T.astype(BF16)
        return carry

    lax.fori_loop(0, nb, prep, 0)


def _band_mask_t(n):
    shape = (2 * ATT_BLK, QT_COLS)
    key = lax.broadcasted_iota(jnp.int32, shape, 0)
    qry = lax.broadcasted_iota(jnp.int32, shape, 1) % ATT_BLK
    lo = jnp.where(n == 0, ATT_BLK, 0)
    return (key > qry) & (key <= qry + WINDOW) & (key >= lo)


def _sink_row(sink_ref, first):
    return jnp.concatenate([jnp.full((1, ATT_BLK), sink_ref[first + g], F32) for g in range(Q_PER_KV)], axis=1)


def _softmax_cols(scores_t, valid, sink):
    sc = jnp.where(valid, scores_t, NEG_BIG)
    m = jnp.maximum(jnp.max(sc, axis=0, keepdims=True), sink)
    e = jnp.exp(sc - m)
    es = jnp.exp(sink - m)
    inv = 1.0 / (jnp.sum(e, axis=0, keepdims=True) + es)
    return e * inv, es * inv


def _attn_fwd(proj3, sinks, cosf, sinf):
    bsz, s, _ = proj3.shape
    nb = s // ATT_BLK
    sp = _attn_specs(s, "bp")

    def body(sink_ref, q_ref, k_ref, v_ref, g_ref, cos_ref, sin_ref, o_ref, y_ref, qt_sc, km_sc, vm_sc, vt_sc):
        p = pl.program_id(1)
        _attn_prep(q_ref, k_ref, v_ref, cos_ref, sin_ref, qt_sc, km_sc, vm_sc, vt_sc, "v", nb)
        kv_row = lax.broadcasted_iota(jnp.int32, (LANES, QT_COLS), 0) // HEAD_DIM

        def blk(n, carry):
            r0 = pl.multiple_of(n * ATT_BLK, ATT_BLK)
            valid = _band_mask_t(n)
            rq = qt_sc[n]
            vt = vt_sc[:, pl.ds(r0, 2 * ATT_BLK)]
            ots = []
            for j in range(KV_PER_STEP):
                st = _dot(km_sc[j, pl.ds(r0, 2 * ATT_BLK), :], rq)
                pt, _ = _softmax_cols(st, valid, _sink_row(sink_ref, p * HEADS_PER_STEP + j * Q_PER_KV))
                ots.append(_dot(vt, pt.astype(BF16)))
            o = _from_qt(jnp.where(kv_row == 0, ots[0], ots[1]))
            o_ref[pl.ds(r0, ATT_BLK), :] = o
            g = g_ref[pl.ds(r0, ATT_BLK), :]
            y_ref[pl.ds(r0, ATT_BLK), :] = (o * (g * _sigmoid(g))).astype(BF16)
            return carry

        lax.fori_loop(0, nb, blk, 0)

    return pl.pallas_call(
        body, name="attn_fwd", grid=(bsz, N_PAIRS),
        out_shape=(jax.ShapeDtypeStruct((bsz, s, D_ATTN), F32), jax.ShapeDtypeStruct((bsz, s, D_ATTN), BF16)),
        in_specs=[sp["smem"], sp["q"], sp["k"], sp["v"], sp["g"], sp["tbl"], sp["tbl"]],
        out_specs=(sp["act"], sp["act"]),
        scratch_shapes=_attn_scratch(s, True),
        compiler_params=_params())(sinks, proj3, proj3, proj3, proj3, cosf, sinf)


def _attn_bwd(proj3, o3, dy3, sinks, cosf, sinf):
    bsz, s, _ = proj3.shape
    nb = s // ATT_BLK
    sp = _attn_specs(s, "pb")

    def body(sink_ref, q_ref, k_ref, v_ref, g_ref, o_ref, dy_ref, cos_ref, sin_ref,
             dq_ref, dk_ref, dv_ref, dg_ref, ds_ref, qt_sc, km_sc, vm_sc, kt_sc, dot_sc, dqt_sc, dk_sc, dv_sc):
        p = pl.program_id(0)
        b = pl.program_id(1)
        _attn_prep(q_ref, k_ref, v_ref, cos_ref, sin_ref, qt_sc, km_sc, vm_sc, kt_sc, "k", nb)
        dk_sc[...] = jnp.zeros_like(dk_sc)
        dv_sc[...] = jnp.zeros_like(dv_sc)

        def gate(n, carry):
            r0 = pl.multiple_of(n * ATT_BLK, ATT_BLK)
            g = g_ref[pl.ds(r0, ATT_BLK), :]
            dy = dy_ref[pl.ds(r0, ATT_BLK), :]
            sg = _sigmoid(g)
            dg_ref[pl.ds(r0, ATT_BLK), :] = (dy * o_ref[pl.ds(r0, ATT_BLK), :] * (sg * (1.0 + g * (1.0 - sg)))).astype(BF16)
            dot_sc[n] = _to_qt(dy * (g * sg)).astype(BF16)
            return carry

        lax.fori_loop(0, nb, gate, 0)
        kv_lane = lax.broadcasted_iota(jnp.int32, (2 * ATT_BLK, LANES), 1) // HEAD_DIM
        kv_row = lax.broadcasted_iota(jnp.int32, (LANES, QT_COLS), 0) // HEAD_DIM

        def blk(n, acc):
            r0 = pl.multiple_of(n * ATT_BLK, ATT_BLK)
            valid = _band_mask_t(n)
            rq = qt_sc[n]
            rd = dot_sc[n]
            kt = kt_sc[:, pl.ds(r0, 2 * ATT_BLK)]
            dvs, dks, dqs, new_acc = [], [], [], []
            for j in range(KV_PER_STEP):
                st = _dot(km_sc[j, pl.ds(r0, 2 * ATT_BLK), :], rq)
                pt, ps = _softmax_cols(st, valid, _sink_row(sink_ref, p * HEADS_PER_STEP + j * Q_PER_KV))
                dpt = _dot(vm_sc[j, pl.ds(r0, 2 * ATT_BLK), :], rd)
                delta = jnp.sum(pt * dpt, axis=0, keepdims=True)
                dst = (pt * (dpt - delta)).astype(BF16)
                new_acc.append(acc[j] + ps * delta)
                dvs.append(_dot_nt(pt.astype(BF16), rd))
                dks.append(_dot_nt(dst, rq))
                dqs.append(_dot(kt, dst))
            dv_sc[pl.ds(r0, 2 * ATT_BLK), :] += jnp.where(kv_lane == 0, dvs[0], dvs[1])
            dk_sc[pl.ds(r0, 2 * ATT_BLK), :] += jnp.where(kv_lane == 0, dks[0], dks[1])
            dqt_sc[n] = jnp.where(kv_row == 0, dqs[0], dqs[1]) * ATTN_SCALE
            return tuple(new_acc)

        acc = lax.fori_loop(0, nb, blk, tuple(jnp.zeros((1, QT_COLS), F32) for _ in range(KV_PER_STEP)))
        lane1 = lax.broadcasted_iota(jnp.int32, (1, LANES), 1)
        dsink = jnp.zeros((1, LANES), F32)
        for j in range(KV_PER_STEP):
            for i in range(Q_PER_KV):
                part = jnp.sum(acc[j][:, i * ATT_BLK:(i + 1) * ATT_BLK], axis=1, keepdims=True)
                dsink = dsink - jnp.where(lane1 == j * Q_PER_KV + i, part, 0.0)

        @pl.when(b == 0)
        def _():
            ds_ref[...] = jnp.zeros_like(ds_ref)

        ds_ref[...] += dsink

        def post(n, carry):
            r0 = pl.multiple_of(n * ATT_BLK, ATT_BLK)
            cs = cos_ref[pl.ds(r0, ATT_BLK), :]
            sn = sin_ref[pl.ds(r0, ATT_BLK), :]
            dq_ref[pl.ds(r0, ATT_BLK), :] = _unrope(_from_qt(dqt_sc[n]), cs, sn).astype(BF16)
            dk_ref[pl.ds(r0, ATT_BLK), :] = _unrope(dk_sc[pl.ds(r0 + ATT_BLK, ATT_BLK), :], cs, sn).astype(BF16)
            dv_ref[pl.ds(r0, ATT_BLK), :] = dv_sc[pl.ds(r0 + ATT_BLK, ATT_BLK), :].astype(BF16)
            return carry

        lax.fori_loop(0, nb, post, 0)

    act = jax.ShapeDtypeStruct((bsz, s, D_ATTN), BF16)
    kvs = jax.ShapeDtypeStruct((bsz, s, D_KV), BF16)
    return pl.pallas_call(
        body, name="attn_bwd", grid=(N_PAIRS, bsz),
        out_shape=(act, kvs, kvs, act, jax.ShapeDtypeStruct((N_PAIRS, 1, LANES), F32)),
        in_specs=[sp["smem"], sp["q1"], sp["k"], sp["v"], sp["g1"], sp["act1"], sp["act1"], sp["tbl"], sp["tbl"]],
        out_specs=(sp["act"], sp["kv"], sp["kv"], sp["act"], pl.BlockSpec((None, 1, LANES), lambda p, b: (p, 0, 0))),
        scratch_shapes=_attn_scratch(s, True) + [pltpu.VMEM((nb, LANES, QT_COLS), BF16),
                                                 pltpu.VMEM((nb, LANES, QT_COLS), F32),
                                                 pltpu.VMEM((s + ATT_BLK, LANES), F32),
                                                 pltpu.VMEM((s + ATT_BLK, LANES), F32)],
        compiler_params=_params())(sinks, proj3, proj3, proj3, proj3, o3, dy3, cosf, sinf)


def _merge_fwd_bwd(x, tgt, y_rnn, y_attn, proj, w_r, w_a, w_o, gf):
    t, d = x.shape
    tm = min(t, 256)

    hw = d // 2

    def body(x_ref, t_ref, yr_ref, ya_ref, mr0_ref, mr1_ref, ma0_ref, ma1_ref, wr_ref, wa_ref, wo_ref, gf_ref,
             dmg_ref, dyr_ref, dya_ref, mg_ref, dx2_ref, dx2b_ref, dpr_ref, dpa_ref, loss_ref, dgf_ref):
        i = pl.program_id(0)
        wr = wr_ref[...]
        wa = wa_ref[...]
        wo = wo_ref[...]
        gfv = gf_ref[...]
        pr = _dot(yr_ref[...], wr)
        pa = _dot(ya_ref[...], wa)
        sr = _sigmoid(jnp.concatenate([mr0_ref[...], mr1_ref[...]], axis=1))
        sa = _sigmoid(jnp.concatenate([ma0_ref[...], ma1_ref[...]], axis=1))
        mb = (sr * pr + sa * pa).astype(BF16)
        mg_ref[...] = mb
        x2 = x_ref[...] + _dot(mb, wo)
        r2 = lax.rsqrt(jnp.mean(x2 * x2, axis=-1, keepdims=True) + NORM_EPS)
        nrm = x2 * r2
        err = nrm * gfv - t_ref[...]
        dy = err * (1.0 / d)
        dn = dy * gfv
        dx2 = r2 * (dn - nrm * jnp.mean(dn * nrm, axis=-1, keepdims=True))
        dx2_ref[...] = dx2
        dx2b = dx2.astype(BF16)
        dx2b_ref[...] = dx2b
        dmerged = _dot_nt(dx2b, wo)
        dpr = (dmerged * sr).astype(BF16)
        dpa = (dmerged * sa).astype(BF16)
        dpr_ref[...] = dpr
        dpa_ref[...] = dpa
        dmg_ref[:, 0:d] = (dmerged * pr * (sr * (1.0 - sr))).astype(BF16)
        dmg_ref[:, d:2 * d] = (dmerged * pa * (sa * (1.0 - sa))).astype(BF16)
        dyr_ref[...] = _dot_nt(dpr, wr)
        dya_ref[...] = _dot_nt(dpa, wa)

        @pl.when(i == 0)
        def _():
            loss_ref[...] = jnp.zeros_like(loss_ref)
            dgf_ref[...] = jnp.zeros_like(dgf_ref)

        loss_ref[...] += jnp.full((1, LANES), 0.5 / d, F32) * jnp.sum(err * err)
        dgf_ref[...] += jnp.sum(dy * nrm, axis=0, keepdims=True)

    tile = pl.BlockSpec((tm, d), lambda i: (i, 0))
    wsp = pl.BlockSpec((d, d), lambda i: (0, 0))

    def gate(col_blk):
        return pl.BlockSpec((tm, hw), lambda i: (i, col_blk))

    fb = jax.ShapeDtypeStruct((t, d), BF16)
    ff = jax.ShapeDtypeStruct((t, d), F32)
    return pl.pallas_call(
        body, name="merge_fwd_bwd", grid=(t // tm,),
        out_shape=(jax.ShapeDtypeStruct((t, 2 * d), BF16), ff, ff, fb, ff, fb, fb, fb,
                   jax.ShapeDtypeStruct((1, LANES), F32), jax.ShapeDtypeStruct((1, d), F32)),
        in_specs=[tile, tile, tile, tile] + [gate(OFF_MERGE_R // hw + j) for j in range(4)] + [
            wsp, wsp, wsp, pl.BlockSpec((1, d), lambda i: (0, 0))],
        out_specs=(pl.BlockSpec((tm, 2 * d), lambda i: (i, 0)), tile, tile, tile, tile, tile, tile, tile,
                   pl.BlockSpec((1, LANES), lambda i: (0, 0)), pl.BlockSpec((1, d), lambda i: (0, 0))),
        compiler_params=_params())(x, tgt, y_rnn, y_attn, proj, proj, proj, proj, w_r, w_a, w_o, gf)


def _local_grads(x, tgt, norm_g, w_in_bm, conv_w, conv_b, lru_w_a, lru_b_a, lru_w_x, lru_b_x, lam, sinks,
                 w_r, w_a, w_o, gf):
    bsz, s, d = x.shape
    t = bsz * s
    x2 = x.reshape(t, d)
    h = _rmsnorm_fwd(x2, norm_g)
    proj = _in_proj(h, w_in_bm)
    proj3 = proj.reshape(bsz, s, D_IN)
    h_lru, y_rnn = _lru_fwd(proj3, conv_w, conv_b, lru_w_a, lru_b_a, lru_w_x, lru_b_x, lam)
    cosf, sinf = _rope_tables(s)
    o_attn, y_attn = _attn_fwd(proj3, sinks, cosf, sinf)
    y_rnn2 = y_rnn.reshape(t, d)
    y_attn2 = y_attn.reshape(t, d)
    dmg, dyr, dya, merged, dx2, dx2b, dpr, dpa, loss, dgf = _merge_fwd_bwd(
        x2, tgt.reshape(t, d), y_rnn2, y_attn2, proj, w_r, w_a, w_o, gf)
    dup, dgr, dcw, dcb, dwa, dba, dwx, dbx, dlam = _lru_bwd(
        proj3, h_lru, dyr.reshape(bsz, s, d), conv_w, conv_b, lru_w_a, lru_b_a, lru_w_x, lru_b_x, lam)
    dq, dk, dv, dga, dsink = _attn_bwd(proj3, o_attn, dya.reshape(bsz, s, d), sinks, cosf, sinf)
    dproj = jnp.concatenate([dup, dgr, dq, dk, dv, dga, dmg.reshape(bsz, s, 2 * d)], axis=-1).reshape(t, D_IN)
    dh = _grad_h(dproj, w_in_bm)
    grad_x, dng = _rmsnorm_bwd(x2, dh, dx2, norm_g)
    small = dict(norm_g=dng, conv_w=dcw, conv_b=dcb, lru_w_a=dwa, lru_b_a=dba, lru_w_x=dwx, lru_b_x=dbx,
                 lru_lambda=dlam, attn_sinks=dsink[:, 0, :HEADS_PER_STEP].reshape(1, N_Q_HEADS), final_norm_g=dgf)
    squares = [(y_rnn2, dpr), (y_attn2, dpa), (merged, dx2b)]
    return loss[0, 0], grad_x.reshape(bsz, s, d), h, dproj, squares, small


ANY = pl.BlockSpec(memory_space=pl.ANY)


def _mesh_pos():
    return lax.axis_index("x"), lax.axis_index("y"), lax.axis_index("c")


def _chip_peers(x, y, c):
    return [((1 - x, y, c), 2 * (1 - x) + y), ((x, 1 - y, c), 2 * x + (1 - y)),
            ((1 - x, 1 - y, c), 2 * (1 - x) + (1 - y))]


def _remote(src, dst, send_sems, recv_sems, idx, peer):
    return pltpu.make_async_remote_copy(src_ref=src, dst_ref=dst, send_sem=send_sems.at[idx],
                                        recv_sem=recv_sems.at[idx], device_id=peer, device_id_type=MESH)


def _all_gather(bufs, split):
    n = len(bufs)
    n_fwd = 3 * sum(split)

    def body(*refs):
        ins, outs = refs[:n], refs[n:2 * n]
        send_sems, recv_sems, fsend_sems, frecv_sems = refs[2 * n:]
        x, y, c = _mesh_pos()
        me = 2 * x + y
        sib = (x, y, 1 - c)
        peers = _chip_peers(x, y, c)

        def part(ref, slot, t, half):
            if not split[t]:
                return ref.at[slot]
            hr = bufs[t].shape[1] // 2
            return ref.at[slot, pl.ds(pl.multiple_of(half * hr, 8), hr), :]

        sends, recvs = [], []
        for t in range(n):
            for k, (peer, pj) in enumerate(peers):
                src = part(ins[t], me, t, c)
                sends.append(_remote(src, part(outs[t], me, t, c), send_sems, recv_sems, 3 * t + k, peer))
                recvs.append(_remote(src, part(outs[t], pj, t, c), send_sems, recv_sems, 3 * t + k, peer))
        for cp in sends:
            cp.start()
        fwd, fwd_recv = [], []
        for t in range(n):
            for k, (peer, pj) in enumerate(peers):
                recvs[3 * t + k].wait_recv()
                if split[t]:
                    got = part(outs[t], pj, t, c)
                    f = len(fwd)
                    fwd.append(_remote(got, got, fsend_sems, frecv_sems, f, sib))
                    fwd_recv.append(_remote(got, part(outs[t], pj, t, 1 - c), fsend_sems, frecv_sems, f, sib))
                    fwd[-1].start()
        for cp in sends:
            cp.wait_send()
        for snd, rcv in zip(fwd, fwd_recv):
            snd.wait_send()
            rcv.wait_recv()

    return pl.pallas_call(
        body, name="all_gather_weights", out_shape=[jax.ShapeDtypeStruct(a.shape, a.dtype) for a in bufs],
        in_specs=[ANY] * n, out_specs=[ANY] * n, input_output_aliases={t: t for t in range(n)},
        scratch_shapes=[pltpu.SemaphoreType.DMA((3 * n,)), pltpu.SemaphoreType.DMA((3 * n,)),
                        pltpu.SemaphoreType.DMA((n_fwd,)), pltpu.SemaphoreType.DMA((n_fwd,))],
        compiler_params=_params())(*bufs)


def _pair_exchange(bigs, small):
    n = len(bigs)

    def body(*refs):
        ins, outs = refs[:n + 1], refs[n + 1:2 * n + 2]
        send_sems, recv_sems = refs[2 * n + 2:]
        x, y, c = _mesh_pos()
        sib = (x, y, 1 - c)
        cps = []
        for t in range(n):
            hr = bigs[t].shape[1] // 2
            src = ins[t].at[:, pl.ds(pl.multiple_of((1 - c) * hr, 8), hr), :]
            cps.append(_remote(src, outs[t], send_sems, recv_sems, t, sib))
        cps.append(_remote(ins[n], outs[n], send_sems, recv_sems, n, sib))
        for cp in cps:
            cp.start()
        for cp in cps:
            cp.wait()

    out_shape = [jax.ShapeDtypeStruct((a.shape[0], a.shape[1] // 2, a.shape[2]), a.dtype) for a in bigs]
    out_shape.append(jax.ShapeDtypeStruct(small.shape, small.dtype))
    return pl.pallas_call(
        body, name="pair_exchange", out_shape=out_shape, in_specs=[ANY] * (n + 1), out_specs=[ANY] * (n + 1),
        scratch_shapes=[pltpu.SemaphoreType.DMA((n + 1,)), pltpu.SemaphoreType.DMA((n + 1,))],
        compiler_params=_params())(*bigs, small)


def _row_tile(rows, row_bytes, cap_bytes=2 * 1024 * 1024):
    best = None
    for tr in range(8, rows + 1, 8):
        if rows % tr == 0 and tr * row_bytes <= cap_bytes:
            best = tr
    return best if best is not None else rows


def _add2(a, b, name):
    r, w = a.shape
    tr = _row_tile(r, w * 4)

    def body(a_ref, b_ref, o_ref):
        o_ref[...] = a_ref[...] + b_ref[...]

    spec = pl.BlockSpec((tr, w), lambda i: (i, 0))
    return pl.pallas_call(body, name=name, grid=(r // tr,), out_shape=jax.ShapeDtypeStruct((r, w), F32),
                          in_specs=[spec, spec], out_specs=spec, compiler_params=_params())(a, b)


def _chip_exchange(bigs_b, small_slots):
    n = len(bigs_b)

    def body(*refs):
        ins, outs = refs[:n + 1], refs[n + 1:2 * n + 2]
        send_sems, recv_sems = refs[2 * n + 2:]
        x, y, c = _mesh_pos()
        me = 2 * x + y
        sends, recvs = [], []
        for t in range(n + 1):
            for k, (peer, pj) in enumerate(_chip_peers(x, y, c)):
                src = ins[t].at[pj] if t < n else ins[t].at[me]
                sends.append(_remote(src, outs[t].at[me], send_sems, recv_sems, 3 * t + k, peer))
                recvs.append(_remote(src, outs[t].at[pj], send_sems, recv_sems, 3 * t + k, peer))
        for cp in sends:
            cp.start()
        for snd, rcv in zip(sends, recvs):
            snd.wait_send()
            rcv.wait_recv()

    out_shape = [jax.ShapeDtypeStruct(a.shape, a.dtype) for a in bigs_b + [small_slots]]
    return pl.pallas_call(
        body, name="chip_exchange", out_shape=out_shape, in_specs=[ANY] * (n + 1), out_specs=[ANY] * (n + 1),
        input_output_aliases={n: n},
        scratch_shapes=[pltpu.SemaphoreType.DMA((3 * n + 3,)), pltpu.SemaphoreType.DMA((3 * n + 3,))],
        compiler_params=_params())(*bigs_b, small_slots)


def _tree_sum4(a, idx):
    _, r, w = a.shape
    tr = _row_tile(r, w * 4)
    steps = r // tr

    def body(idx_ref, a_ref, o_ref):
        o_ref[...] = (a_ref[0] + a_ref[1]) + (a_ref[2] + a_ref[3])

    grid_spec = pltpu.PrefetchScalarGridSpec(
        num_scalar_prefetch=1, grid=(steps,),
        in_specs=[pl.BlockSpec((N_CHIPS, tr, w), lambda i, idx_ref: (0, i, 0))],
        out_specs=pl.BlockSpec((tr, w), lambda i, idx_ref: (idx_ref[4] * steps + i, 0)))
    return pl.pallas_call(body, name="small_sum", grid_spec=grid_spec, out_shape=jax.ShapeDtypeStruct((2 * r, w), F32),
                          compiler_params=_params())(idx, a)


def _pair_gather(bufs):
    n = len(bufs)

    def body(*refs):
        ins, outs = refs[:n], refs[n:2 * n]
        send_sems, recv_sems = refs[2 * n:]
        x, y, c = _mesh_pos()
        sib = (x, y, 1 - c)
        sends, recvs = [], []
        for t in range(n):
            hr = bufs[t].shape[0] // 2
            mine = pl.ds(pl.multiple_of(c * hr, 8), hr)
            theirs = pl.ds(pl.multiple_of((1 - c) * hr, 8), hr)
            sends.append(_remote(ins[t].at[mine, :], outs[t].at[mine, :], send_sems, recv_sems, t, sib))
            recvs.append(_remote(ins[t].at[mine, :], outs[t].at[theirs, :], send_sems, recv_sems, t, sib))
        for cp in sends:
            cp.start()
        for snd, rcv in zip(sends, recvs):
            snd.wait_send()
            rcv.wait_recv()

    return pl.pallas_call(
        body, name="pair_gather", out_shape=[jax.ShapeDtypeStruct(a.shape, a.dtype) for a in bufs],
        in_specs=[ANY] * n, out_specs=[ANY] * n, input_output_aliases={t: t for t in range(n)},
        scratch_shapes=[pltpu.SemaphoreType.DMA((n,)), pltpu.SemaphoreType.DMA((n,))],
        compiler_params=_params())(*bufs)


def _all_reduce_small(small, x, y, c):
    me = 2 * x + y
    (got,) = _pair_exchange([], small)
    small_pair = _add2(small, got, "pair_sum_small")
    small_slots = _put_slot(small_pair, N_CHIPS, jnp.stack([me, c]).astype(jnp.int32), PK_HALF, F32, "small_slot")
    (slots,) = _chip_exchange([], small_slots)
    idx = jnp.stack([me, me, me, me, c]).astype(jnp.int32)
    (out,) = _pair_gather([_tree_sum4(slots, idx)])
    return out


XOR_ORDER = (3, 2, 1)


def _grads_reduce_scatter(h, dproj, squares, idx):
    t, d = h.shape
    nsq = len(squares)
    hr = d // 2
    qr = ROW_BLK // 2
    tk = min(t, 512)
    nk = t // tk
    last = N_CHIPS - 1
    n_phase = 2 + nsq

    def dest(s, idx_ref):
        xo = jnp.where(s == 0, XOR_ORDER[0], jnp.where(s == 1, XOR_ORDER[1], jnp.where(s == 2, XOR_ORDER[2], 0)))
        return idx_ref[0] ^ xo

    def k_eff(p, k, first, count):
        return jnp.where(p < first, 0, jnp.where(p >= first + count, nk - 1, k))

    in_specs = [
        pl.BlockSpec((tk, hr), lambda s, p, k, idx_ref: (k_eff(p, k, 0, 2), (1 - idx_ref[1] + jnp.minimum(p, 1)) % 2)),
        pl.BlockSpec((tk, W_BLK), lambda s, p, k, idx_ref: (k_eff(p, k, 0, 2), dest(s, idx_ref)))]
    for q in range(nsq):
        in_specs.append(pl.BlockSpec((tk, ROW_BLK), lambda s, p, k, idx_ref, q=q: (k_eff(p, k, 2 + q, 1), dest(s, idx_ref))))
        in_specs.append(pl.BlockSpec((tk, d), lambda s, p, k, idx_ref, q=q: (k_eff(p, k, 2 + q, 1), 0)))

    def body(idx_ref, *refs):
        h_ref, dp_ref = refs[0], refs[1]
        sq_in = refs[2:2 + 2 * nsq]
        outs = refs[2 + 2 * nsq:3 + 3 * nsq]
        sc = refs[3 + 3 * nsq:]
        acc_w, xr_w, sb_w, ir_w = sc[0:4]
        sq_sc = [sc[4 + 4 * q:8 + 4 * q] for q in range(nsq)]
        x_send, x_recv, i_send, i_recv, f_send, f_recv, o_sem = sc[4 + 4 * nsq:]
        s, p, k = pl.program_id(0), pl.program_id(1), pl.program_id(2)
        x, y, c = _mesh_pos()
        sib = (x, y, 1 - c)
        peers = [((1 - x) if xo & 2 else x, (1 - y) if xo & 1 else y, c) for xo in XOR_ORDER]
        slot = s % 2
        mine_w = pl.ds(pl.multiple_of(c * hr, 8), hr)
        theirs_w = pl.ds(pl.multiple_of((1 - c) * hr, 8), hr)
        mine_q = pl.ds(pl.multiple_of(c * qr, 8), qr)
        theirs_q = pl.ds(pl.multiple_of((1 - c) * qr, 8), qr)

        def exch(j, src, dst):
            return _remote(src, dst, x_send, x_recv, 2 * j + slot, sib)

        def ici(j, ss, sbuf, irecv):
            return _remote(sbuf.at[ss], irecv.at[ss], i_send, i_recv, last * j + ss, peers[ss])

        def w_phase(hf):
            @pl.when(k == 0)
            def _():
                acc_w[hf] = jnp.zeros((hr, W_BLK), F32)

            acc_w[hf] += _dot_tn(h_ref[...], dp_ref[...])

            @pl.when(k == nk - 1)
            def _():
                if hf == 0:
                    exch(0, acc_w.at[0], xr_w.at[slot]).start()
                else:
                    cp = exch(0, acc_w.at[0], xr_w.at[slot])
                    cp.wait_recv()
                    cp.wait_send()
                    pair = acc_w[1] + xr_w[slot]
                    for ss in range(last):
                        @pl.when(s == ss)
                        def _():
                            sb_w[ss] = pair.astype(BF16)
                            ici(0, ss, sb_w, ir_w).start()

                    @pl.when(s == last)
                    def _():
                        acc_w[1] = pair

        def finish_step():
            pairs = []
            for q in range(nsq):
                acc, xr, _, _ = sq_sc[q]
                cp = exch(1 + q, acc.at[theirs_q, :], xr.at[slot])
                cp.wait_recv()
                cp.wait_send()
                pairs.append(acc[mine_q, :] + xr[slot])
            for ss in range(last):
                @pl.when(s == ss)
                def _():
                    for q in range(nsq):
                        _, _, sb, ir = sq_sc[q]
                        sb[ss] = pairs[q].astype(BF16)
                        ici(1 + q, ss, sb, ir).start()

            @pl.when(s == last)
            def _():
                total = acc_w[1]
                for ss in range(last):
                    ici(0, ss, sb_w, ir_w).wait_recv()
                    total = total + ir_w[ss].astype(F32)
                acc_w[1] = total
                done = [(acc_w.at[1], outs[0], mine_w, theirs_w)]
                for q in range(nsq):
                    acc, _, sb, ir = sq_sc[q]
                    total = pairs[q]
                    for ss in range(last):
                        ici(1 + q, ss, sb, ir).wait_recv()
                        total = total + ir[ss].astype(F32)
                    acc[mine_q, :] = total
                    done.append((acc.at[mine_q, :], outs[1 + q], mine_q, theirs_q))
                copies = []
                for j, (src, out, mine, theirs) in enumerate(done):
                    keep = pltpu.make_async_copy(src, out.at[mine, :], o_sem.at[j])
                    give = _remote(src, out.at[mine, :], f_send, f_recv, j, sib)
                    take = _remote(src, out.at[theirs, :], f_send, f_recv, j, sib)
                    keep.start()
                    give.start()
                    copies.append((keep, give, take))
                for ss in range(last):
                    ici(0, ss, sb_w, ir_w).wait_send()
                    for q in range(nsq):
                        ici(1 + q, ss, sq_sc[q][2], sq_sc[q][3]).wait_send()
                for keep, give, take in copies:
                    keep.wait()
                    give.wait_send()
                    take.wait_recv()

        def sq_phase(q):
            acc, xr, _, _ = sq_sc[q]

            @pl.when(k == 0)
            def _():
                acc[...] = jnp.zeros((ROW_BLK, d), F32)

            acc[...] += _dot_tn(sq_in[2 * q][...], sq_in[2 * q + 1][...])

            @pl.when(k == nk - 1)
            def _():
                exch(1 + q, acc.at[theirs_q, :], xr.at[slot]).start()
                if q == nsq - 1:
                    finish_step()

        for hf in range(2):
            pl.when(p == hf)(functools.partial(w_phase, hf))
        for q in range(nsq):
            pl.when(p == 2 + q)(functools.partial(sq_phase, q))

    nj = 1 + nsq
    scratch = [pltpu.VMEM((2, hr, W_BLK), F32), pltpu.VMEM((2, hr, W_BLK), F32),
               pltpu.VMEM((last, hr, W_BLK), BF16), pltpu.VMEM((last, hr, W_BLK), BF16)]
    for _ in range(nsq):
        scratch += [pltpu.VMEM((ROW_BLK, d), F32), pltpu.VMEM((2, qr, d), F32),
                    pltpu.VMEM((last, qr, d), BF16), pltpu.VMEM((last, qr, d), BF16)]
    scratch += [pltpu.SemaphoreType.DMA((2 * nj,)), pltpu.SemaphoreType.DMA((2 * nj,)),
                pltpu.SemaphoreType.DMA((last * nj,)), pltpu.SemaphoreType.DMA((last * nj,)),
                pltpu.SemaphoreType.DMA((nj,)), pltpu.SemaphoreType.DMA((nj,)), pltpu.SemaphoreType.DMA((nj,))]
    grid_spec = pltpu.PrefetchScalarGridSpec(
        num_scalar_prefetch=1, grid=(N_CHIPS, n_phase, nk), in_specs=in_specs, out_specs=[ANY] * nj,
        scratch_shapes=scratch)
    out_shape = [jax.ShapeDtypeStruct((d, W_BLK), F32)] + [jax.ShapeDtypeStruct((ROW_BLK, d), F32)] * nsq
    flat = [a for pair in squares for a in pair]
    return pl.pallas_call(body, name="grads_reduce_scatter", grid_spec=grid_spec, out_shape=out_shape,
                          compiler_params=_params())(idx, h, dproj, *flat)


_VEC_NAMES = ("norm_g", "conv_b", "lru_b_a", "lru_b_x", "lru_lambda", "final_norm_g")


def _pack_small(p, conv_full=None):
    rows = [p["lru_w_a"].reshape(PK_WX - PK_WA, LANES), p["lru_w_x"].reshape(PK_VEC - PK_WX, LANES)]
    rows += [p[k].reshape(8, LANES) for k in _VEC_NAMES]
    rows.append(jnp.pad(p["attn_sinks"].reshape(1, N_Q_HEADS), ((0, 7), (0, LANES - N_Q_HEADS))))
    tail = PK_ROWS - PK_CONV
    if conv_full is None:
        rows.append(jnp.zeros((tail, LANES), F32))
    else:
        rows.append(conv_full.reshape(32, LANES))
        rows.append(jnp.zeros((tail - 32, LANES), F32))
    return jnp.concatenate(rows, axis=0)


def _unpack_small(pk, like):
    out = {"lru_w_a": pk[PK_WA:PK_WX].reshape(like["lru_w_a"].shape),
           "lru_w_x": pk[PK_WX:PK_VEC].reshape(like["lru_w_x"].shape)}
    for j, k in enumerate(_VEC_NAMES):
        out[k] = pk[PK_VEC + 8 * j:PK_VEC + 8 * j + 8].reshape(like[k].shape)
    out["attn_sinks"] = pk[PK_SINK:PK_SINK + 1, :N_Q_HEADS].reshape(like["attn_sinks"].shape)
    return out


_WEIGHTS = ("norm_g", "w_in", "conv_w", "conv_b", "lru_w_a", "lru_b_a", "lru_w_x", "lru_b_x", "lru_lambda",
            "attn_sinks", "w_rnn_out", "w_attn_out", "w_o", "final_norm_g")
_SMALL = ("norm_g", "conv_b", "lru_w_a", "lru_b_a", "lru_w_x", "lru_b_x", "lru_lambda", "attn_sinks", "final_norm_g")
_ROW_SHARDED = ("w_rnn_out", "w_attn_out", "w_o")


def kernel(x, norm_g, w_in, conv_w, conv_b, lru_w_a, lru_b_a, lru_w_x, lru_b_x, lru_lambda, attn_sinks, w_rnn_out, w_attn_out, w_o, final_norm_g, loss_target, m_norm_g, m_w_in, m_conv_w, m_conv_b, m_lru_w_a, m_lru_b_a, m_lru_w_x, m_lru_b_x, m_lru_lambda, m_attn_sinks, m_w_rnn_out, m_w_attn_out, m_w_o, m_final_norm_g, v_norm_g, v_w_in, v_conv_w, v_conv_b, v_lru_w_a, v_lru_b_a, v_lru_w_x, v_lru_b_x, v_lru_lambda, v_attn_sinks, v_w_rnn_out, v_w_attn_out, v_w_o, v_final_norm_g):
    w = dict(norm_g=norm_g, w_in=w_in, conv_w=conv_w, conv_b=conv_b, lru_w_a=lru_w_a, lru_b_a=lru_b_a, lru_w_x=lru_w_x,
             lru_b_x=lru_b_x, lru_lambda=lru_lambda, attn_sinks=attn_sinks, w_rnn_out=w_rnn_out, w_attn_out=w_attn_out,
             w_o=w_o, final_norm_g=final_norm_g)
    m = dict(norm_g=m_norm_g, w_in=m_w_in, conv_w=m_conv_w, conv_b=m_conv_b, lru_w_a=m_lru_w_a, lru_b_a=m_lru_b_a,
             lru_w_x=m_lru_w_x, lru_b_x=m_lru_b_x, lru_lambda=m_lru_lambda, attn_sinks=m_attn_sinks,
             w_rnn_out=m_w_rnn_out, w_attn_out=m_w_attn_out, w_o=m_w_o, final_norm_g=m_final_norm_g)
    v = dict(norm_g=v_norm_g, w_in=v_w_in, conv_w=v_conv_w, conv_b=v_conv_b, lru_w_a=v_lru_w_a, lru_b_a=v_lru_b_a,
             lru_w_x=v_lru_w_x, lru_b_x=v_lru_b_x, lru_lambda=v_lru_lambda, attn_sinks=v_attn_sinks,
             w_rnn_out=v_w_rnn_out, w_attn_out=v_w_attn_out, w_o=v_w_o, final_norm_g=v_final_norm_g)
    mx, my, mc = _mesh_pos()
    me = 2 * mx + my
    d = D_MODEL

    slot0 = jnp.stack([me, jnp.zeros_like(me)]).astype(jnp.int32)
    bufs = [_put_slot(w[k][0], N_CHIPS, slot0, w[k].shape[1], BF16, "cast_" + k) for k in ("w_in",) + _ROW_SHARDED]
    bufs.append(_put_slot(w["conv_w"][0], N_CHIPS, slot0, CONV_WIDTH, F32, "slot_conv_w"))
    g_in, g_r, g_a, g_o, g_cw = _all_gather(bufs, [True, True, True, True, False])
    conv_full = g_cw.transpose(1, 0, 2).reshape(CONV_WIDTH, D_RNN)

    loss_local, grad_x, h, dproj, squares, gsmall = _local_grads(
        x, loss_target, w["norm_g"], g_in, conv_full, w["conv_b"], w["lru_w_a"][0], w["lru_b_a"], w["lru_w_x"][0],
        w["lru_b_x"], w["lru_lambda"], w["attn_sinks"][0], g_r.reshape(d, d), g_a.reshape(d, d), g_o.reshape(d, d),
        w["final_norm_g"].reshape(1, d))
    loss = lax.psum(loss_local, ("x", "y", "c"))

    gpack = _pack_small(gsmall, gsmall["conv_w"])
    spack = _all_reduce_small(gpack, mx, my, mc)
    f_in, f_r, f_a, f_o = _grads_reduce_scatter(h, dproj, squares, jnp.stack([me, mc]).astype(jnp.int32))

    grads = _unpack_small(spack, w)
    conv_all = spack[PK_CONV:PK_CONV + 32].reshape(CONV_WIDTH, D_RNN)
    grads["conv_w"] = lax.dynamic_slice_in_dim(conv_all, me * (D_RNN // N_CHIPS), D_RNN // N_CHIPS, axis=1)[None]
    grads["w_in"] = f_in[None]
    grads["w_rnn_out"], grads["w_attn_out"], grads["w_o"] = f_r[None], f_a[None], f_o[None]

    delta, new_m, new_v = {}, {}, {}
    for k in ("w_in",) + _ROW_SHARDED:
        dk, mk, vk = _adamw(w[k][0], grads[k][0], m[k][0], v[k][0], "adamw_" + k)
        delta[k], new_m[k], new_v[k] = dk[None], mk[None], vk[None]
    shp = (2 * CONV_WIDTH, LANES)
    dk, mk, vk = _adamw(w["conv_w"].reshape(shp), grads["conv_w"].reshape(shp), m["conv_w"].reshape(shp),
                        v["conv_w"].reshape(shp), "adamw_conv_w")
    delta["conv_w"], new_m["conv_w"], new_v["conv_w"] = (a.reshape(w["conv_w"].shape) for a in (dk, mk, vk))
    dk, mk, vk = _adamw(_pack_small(w), spack, _pack_small(m), _pack_small(v), "adamw_small")
    for src, dst in ((dk, delta), (mk, new_m), (vk, new_v)):
        dst.update(_unpack_small(src, w))

    return (loss, grad_x, *[grads[k] for k in _WEIGHTS], *[delta[k] for k in _WEIGHTS],
            *[new_m[k] for k in _WEIGHTS], *[new_v[k] for k in _WEIGHTS])
```

```python
import functools
import math

import jax
import jax.numpy as jnp
from jax import lax
from jax.experimental import pallas as pl
from jax.experimental.pallas import tpu as pltpu

F32 = jnp.float32
BF16 = jnp.bfloat16
MESH = pl.DeviceIdType.MESH

D_MODEL = 1024
D_RNN = 1024
N_RNN_BLOCKS = 8
RNN_BLOCK = D_RNN // N_RNN_BLOCKS
CONV_WIDTH = 4
LRU_C = 8.0
HEAD_DIM = 64
N_Q_HEADS = 16
N_KV_HEADS = 4
D_ATTN = N_Q_HEADS * HEAD_DIM
D_KV = N_KV_HEADS * HEAD_DIM
WINDOW = 128
ROPE_DIM = HEAD_DIM // 4
ROPE_THETA = 500000.0
NORM_EPS = 1e-6
OFF_RNN_X = 0
OFF_RNN_G = OFF_RNN_X + D_RNN
OFF_Q = OFF_RNN_G + D_RNN
OFF_K = OFF_Q + D_ATTN
OFF_V = OFF_K + D_KV
OFF_ATTN_G = OFF_V + D_KV
OFF_MERGE_R = OFF_ATTN_G + D_ATTN
OFF_MERGE_A = OFF_MERGE_R + D_MODEL
D_IN = OFF_MERGE_A + D_MODEL

ADAM_LR = 0.001
ADAM_B1 = 0.9
ADAM_B2 = 0.999
ADAM_EPS = 1e-08
ADAM_WD = 0.01
ADAM_STEP = 10

N_CHIPS = 4
W_BLK = D_IN // N_CHIPS
ROW_BLK = D_MODEL // N_CHIPS
LANES = 128
ATT_BLK = 128
VMEM_LIMIT = 56 * 1024 * 1024
NEG_BIG = -1e30
ATTN_SCALE = 1.0 / math.sqrt(HEAD_DIM)

PK_WA = 0
PK_WX = PK_WA + N_RNN_BLOCKS * RNN_BLOCK
PK_VEC = PK_WX + N_RNN_BLOCKS * RNN_BLOCK
PK_SINK = PK_VEC + 6 * 8
PK_CONV = PK_SINK + 8
PK_ROWS = PK_CONV + 32 + 8
PK_HALF = PK_ROWS // 2


def _params(**kw):
    return pltpu.CompilerParams(vmem_limit_bytes=VMEM_LIMIT, **kw)


def _sigmoid(z):
    return 1.0 / (1.0 + jnp.exp(-z))


def _dot(a, b):
    return jnp.dot(a, b, preferred_element_type=F32)


def _dot_nt(a, b):
    return lax.dot_general(a, b, (((1,), (1,)), ((), ())), preferred_element_type=F32)


def _dot_tn(a, b):
    return lax.dot_general(a, b, (((0,), (0,)), ((), ())), preferred_element_type=F32)


def _put_slot(src, n_slots, slot_and_blk, rows, dtype, name):
    _, c = src.shape
    tr = _row_tile(rows, c * 4)
    steps = rows // tr

    def body(idx_ref, s_ref, o_ref):
        o_ref[...] = s_ref[...].astype(dtype)

    grid_spec = pltpu.PrefetchScalarGridSpec(
        num_scalar_prefetch=1, grid=(steps,),
        in_specs=[pl.BlockSpec((tr, c), lambda i, idx_ref: (idx_ref[1] * steps + i, 0))],
        out_specs=pl.BlockSpec((None, tr, c), lambda i, idx_ref: (idx_ref[0], i, 0)))
    return pl.pallas_call(body, name=name, grid_spec=grid_spec,
                          out_shape=jax.ShapeDtypeStruct((n_slots, rows, c), dtype),
                          compiler_params=_params())(slot_and_blk, src)


def _rmsnorm_fwd(x, g):
    t, d = x.shape
    tm = min(t, 512)

    def body(x_ref, g_ref, o_ref):
        xv = x_ref[...]
        r = lax.rsqrt(jnp.mean(xv * xv, axis=-1, keepdims=True) + NORM_EPS)
        o_ref[...] = (xv * r * g_ref[...]).astype(BF16)

    return pl.pallas_call(
        body, name="rmsnorm_fwd", grid=(t // tm,), out_shape=jax.ShapeDtypeStruct((t, d), BF16),
        in_specs=[pl.BlockSpec((tm, d), lambda i: (i, 0)), pl.BlockSpec((1, d), lambda i: (0, 0))],
        out_specs=pl.BlockSpec((tm, d), lambda i: (i, 0)), compiler_params=_params())(x, g)


def _rmsnorm_bwd(x, dh, dx2, g):
    t, d = x.shape
    tm = min(t, 512)

    def body(x_ref, dh_ref, dx2_ref, g_ref, gx_ref, dg_ref):
        i = pl.program_id(0)
        xv = x_ref[...]
        dhv = dh_ref[...]
        r = lax.rsqrt(jnp.mean(xv * xv, axis=-1, keepdims=True) + NORM_EPS)
        nrm = xv * r
        dn = dhv * g_ref[...]
        gx_ref[...] = dx2_ref[...] + r * (dn - nrm * jnp.mean(dn * nrm, axis=-1, keepdims=True))

        @pl.when(i == 0)
        def _():
            dg_ref[...] = jnp.zeros_like(dg_ref)

        dg_ref[...] += jnp.sum(dhv * nrm, axis=0, keepdims=True)

    return pl.pallas_call(
        body, name="rmsnorm_bwd", grid=(t // tm,),
        out_shape=(jax.ShapeDtypeStruct((t, d), F32), jax.ShapeDtypeStruct((1, d), F32)),
        in_specs=[pl.BlockSpec((tm, d), lambda i: (i, 0)), pl.BlockSpec((tm, d), lambda i: (i, 0)),
                  pl.BlockSpec((tm, d), lambda i: (i, 0)), pl.BlockSpec((1, d), lambda i: (0, 0))],
        out_specs=(pl.BlockSpec((tm, d), lambda i: (i, 0)), pl.BlockSpec((1, d), lambda i: (0, 0))),
        compiler_params=_params())(x, dh, dx2, g)


def _adamw(w, g, m, v, name):
    r, c = w.shape
    tr = _row_tile(r, c * 4, 1024 * 1024)
    c1 = 1.0 - ADAM_B1 ** ADAM_STEP
    c2 = 1.0 - ADAM_B2 ** ADAM_STEP

    def body(w_ref, g_ref, m_ref, v_ref, d_ref, nm_ref, nv_ref):
        gv = g_ref[...]
        nm = ADAM_B1 * m_ref[...] + (1.0 - ADAM_B1) * gv
        nv = ADAM_B2 * v_ref[...] + (1.0 - ADAM_B2) * (gv * gv)
        m_hat = nm / c1
        v_hat = nv / c2
        d_ref[...] = -ADAM_LR * (m_hat / (jnp.sqrt(v_hat) + ADAM_EPS) + ADAM_WD * w_ref[...])
        nm_ref[...] = nm
        nv_ref[...] = nv

    spec = pl.BlockSpec((tr, c), lambda i: (i, 0))
    sds = jax.ShapeDtypeStruct((r, c), F32)
    return pl.pallas_call(
        body, name=name, grid=(r // tr,), out_shape=(sds, sds, sds),
        in_specs=[spec, spec, spec, spec], out_specs=(spec, spec, spec), compiler_params=_params())(w, g, m, v)


def _in_proj(h, w_bm):
    t, d = h.shape
    nb, _, wb = w_bm.shape
    tm = min(t, 512)

    def body(h_ref, w_ref, o_ref):
        o_ref[...] = _dot(h_ref[...], w_ref[...])

    return pl.pallas_call(
        body, name="in_proj", grid=(nb, t // tm), out_shape=jax.ShapeDtypeStruct((t, nb * wb), F32),
        in_specs=[pl.BlockSpec((tm, d), lambda j, i: (i, 0)), pl.BlockSpec((None, d, wb), lambda j, i: (j, 0, 0))],
        out_specs=pl.BlockSpec((tm, wb), lambda j, i: (i, j)), compiler_params=_params())(h, w_bm)


def _grad_h(dproj, w_bm):
    t = dproj.shape[0]
    nb, d, wb = w_bm.shape
    tm = min(t, 1024)

    def body(dp_ref, w_ref, o_ref, acc_ref):
        k = pl.program_id(1)

        @pl.when(k == 0)
        def _():
            acc_ref[...] = jnp.zeros_like(acc_ref)

        acc_ref[...] += _dot_nt(dp_ref[...], w_ref[...])

        @pl.when(k == nb - 1)
        def _():
            o_ref[...] = acc_ref[...]

    return pl.pallas_call(
        body, name="grad_h", grid=(t // tm, nb), out_shape=jax.ShapeDtypeStruct((t, d), F32),
        in_specs=[pl.BlockSpec((tm, wb), lambda i, k: (i, k)), pl.BlockSpec((None, d, wb), lambda i, k: (k, 0, 0))],
        out_specs=pl.BlockSpec((tm, d), lambda i, k: (i, 0)),
        scratch_shapes=[pltpu.VMEM((tm, d), F32)], compiler_params=_params())(dproj, w_bm)


def _shift_down(v, d, fill):
    n = v.shape[0]
    if d % 8 == 0:
        return jnp.concatenate([jnp.full((d,) + v.shape[1:], fill, v.dtype), v[: n - d]], axis=0)
    row = lax.broadcasted_iota(jnp.int32, v.shape, 0)
    return jnp.where(row >= d, pltpu.roll(v, d, axis=0), fill)


def _shift_up(v, d, fill):
    n = v.shape[0]
    if d % 8 == 0:
        return jnp.concatenate([v[d:], jnp.full((d,) + v.shape[1:], fill, v.dtype)], axis=0)
    row = lax.broadcasted_iota(jnp.int32, v.shape, 0)
    return jnp.where(row < n - d, pltpu.roll(v, n - d, axis=0), fill)


def _scan(a, b, shift):
    n = a.shape[0]
    d = 1
    while d < n:
        b = a * shift(b, d, 0.0) + b
        if 2 * d < n:
            a = a * shift(a, d, 1.0)
        d *= 2
    return b


def _neg_expm1(y):
    series = -y * (1.0 + y * (1.0 / 2.0) * (1.0 + y * (1.0 / 3.0) * (1.0 + y * (1.0 / 4.0) * (
        1.0 + y * (1.0 / 5.0) * (1.0 + y * (1.0 / 6.0) * (1.0 + y * (1.0 / 7.0)))))))
    return jnp.where(y > -0.25, series, 1.0 - jnp.exp(y))


def _softplus(z):
    e = jnp.exp(-jnp.abs(z))
    w = 1.0 + e
    log1p = jnp.where(w == 1.0, e, jnp.log(w) * (e / jnp.where(w == 1.0, 1.0, w - 1.0)))
    return jnp.maximum(z, 0.0) + log1p


def _conv(up, cw, cb):
    out = cb + cw[CONV_WIDTH - 1:CONV_WIDTH, :] * up
    for j in range(CONV_WIDTH - 1):
        out = out + cw[j:j + 1, :] * _shift_down(up, CONV_WIDTH - 1 - j, 0.0)
    return out


def _lru_gates(u, wa_ref, ba_ref, wx_ref, bx_ref, lam_ref):
    ub = u.astype(BF16)
    r = _sigmoid(_dot(ub, wa_ref[...].astype(BF16)) + ba_ref[...])
    i = _sigmoid(_dot(ub, wx_ref[...].astype(BF16)) + bx_ref[...])
    sp = _softplus(-lam_ref[...])
    log_a = (-LRU_C) * r * sp
    a = jnp.exp(log_a)
    mult = jnp.sqrt(_neg_expm1(2.0 * log_a))
    return r, i, sp, a, mult


def _lru_specs(s):
    cb = RNN_BLOCK
    vec = pl.BlockSpec((1, cb), lambda n, b: (0, n))
    return dict(
        up=pl.BlockSpec((None, s, cb), lambda n, b: (b, 0, OFF_RNN_X // cb + n)),
        gr=pl.BlockSpec((None, s, cb), lambda n, b: (b, 0, OFF_RNN_G // cb + n)),
        act=pl.BlockSpec((None, s, cb), lambda n, b: (b, 0, n)),
        cw=pl.BlockSpec((CONV_WIDTH, cb), lambda n, b: (0, n)),
        vec=vec,
        wblk=pl.BlockSpec((None, cb, cb), lambda n, b: (n, 0, 0)),
    )


def _lru_fwd(proj3, cw, cb, wa, ba, wx, bx, lam):
    bsz, s, _ = proj3.shape
    sp = _lru_specs(s)

    def body(up_ref, gr_ref, cw_ref, cb_ref, wa_ref, ba_ref, wx_ref, bx_ref, lam_ref, h_ref, y_ref):
        u = _conv(up_ref[...], cw_ref[...], cb_ref[...])
        _, i, _, a, mult = _lru_gates(u, wa_ref, ba_ref, wx_ref, bx_ref, lam_ref)
        h = _scan(a, mult * (i * u), _shift_down)
        h_ref[...] = h
        g = gr_ref[...]
        y_ref[...] = (h * (g * _sigmoid(g))).astype(BF16)

    return pl.pallas_call(
        body, name="lru_fwd", grid=(N_RNN_BLOCKS, bsz),
        out_shape=(jax.ShapeDtypeStruct((bsz, s, D_RNN), F32), jax.ShapeDtypeStruct((bsz, s, D_RNN), BF16)),
        in_specs=[sp["up"], sp["gr"], sp["cw"], sp["vec"], sp["wblk"], sp["vec"], sp["wblk"], sp["vec"], sp["vec"]],
        out_specs=(sp["act"], sp["act"]), compiler_params=_params())(proj3, proj3, cw, cb, wa, ba, wx, bx, lam)


def _lru_bwd(proj3, h3, dy3, cw, cb, wa, ba, wx, bx, lam):
    bsz, s, _ = proj3.shape
    sp = _lru_specs(s)

    def body(up_ref, gr_ref, h_ref, dy_ref, cw_ref, cb_ref, wa_ref, ba_ref, wx_ref, bx_ref, lam_ref,
             dup_ref, dgr_ref, dcw_ref, dcb_ref, dwa_ref, dba_ref, dwx_ref, dbx_ref, dlam_ref):
        b = pl.program_id(1)
        up = up_ref[...]
        cwv = cw_ref[...]
        u = _conv(up, cwv, cb_ref[...])
        r, i, spv, a, mult = _lru_gates(u, wa_ref, ba_ref, wx_ref, bx_ref, lam_ref)
        h = h_ref[...]
        g = gr_ref[...]
        dy = dy_ref[...]
        sg = _sigmoid(g)
        dgr_ref[...] = (dy * h * (sg * (1.0 + g * (1.0 - sg)))).astype(BF16)
        dh = dy * (g * sg)
        adj = _scan(_shift_up(a, 1, 0.0), dh, _shift_up)
        da = adj * _shift_down(h, 1, 0.0)
        dmult = adj * (i * u)
        di = adj * mult * u
        du = adj * mult * i
        dla = da * a - dmult * (a * a) / mult
        dr = dla * ((-LRU_C) * spv)
        dsp = jnp.sum(dla * ((-LRU_C) * r), axis=0, keepdims=True)
        dza = dr * r * (1.0 - r)
        dzx = di * i * (1.0 - i)
        ub = u.astype(BF16)
        dzab = dza.astype(BF16)
        dzxb = dzx.astype(BF16)
        du = du + _dot_nt(dzab, wa_ref[...].astype(BF16)) + _dot_nt(dzxb, wx_ref[...].astype(BF16))
        dup = cwv[CONV_WIDTH - 1:CONV_WIDTH, :] * du
        for j in range(CONV_WIDTH - 1):
            dup = dup + cwv[j:j + 1, :] * _shift_up(du, CONV_WIDTH - 1 - j, 0.0)
        dup_ref[...] = dup.astype(BF16)

        @pl.when(b == 0)
        def _():
            for ref in (dcw_ref, dcb_ref, dwa_ref, dba_ref, dwx_ref, dbx_ref, dlam_ref):
                ref[...] = jnp.zeros_like(ref)

        rows = [jnp.sum(du * _shift_down(up, CONV_WIDTH - 1 - j, 0.0), axis=0, keepdims=True)
                for j in range(CONV_WIDTH - 1)]
        rows.append(jnp.sum(du * up, axis=0, keepdims=True))
        dcw_ref[...] += jnp.concatenate(rows, axis=0)
        dcb_ref[...] += jnp.sum(du, axis=0, keepdims=True)
        dwa_ref[...] += _dot_tn(ub, dzab)
        dba_ref[...] += jnp.sum(dza, axis=0, keepdims=True)
        dwx_ref[...] += _dot_tn(ub, dzxb)
        dbx_ref[...] += jnp.sum(dzx, axis=0, keepdims=True)
        dlam_ref[...] += dsp * (-_sigmoid(-lam_ref[...]))

    act_b = jax.ShapeDtypeStruct((bsz, s, D_RNN), BF16)
    vec = jax.ShapeDtypeStruct((1, D_RNN), F32)
    wsd = jax.ShapeDtypeStruct((N_RNN_BLOCKS, RNN_BLOCK, RNN_BLOCK), F32)
    return pl.pallas_call(
        body, name="lru_bwd", grid=(N_RNN_BLOCKS, bsz),
        out_shape=(act_b, act_b, jax.ShapeDtypeStruct((CONV_WIDTH, D_RNN), F32), vec, wsd, vec, wsd, vec, vec),
        in_specs=[sp["up"], sp["gr"], sp["act"], sp["act"], sp["cw"], sp["vec"], sp["wblk"], sp["vec"],
                  sp["wblk"], sp["vec"], sp["vec"]],
        out_specs=(sp["act"], sp["act"], sp["cw"], sp["vec"], sp["wblk"], sp["vec"], sp["wblk"], sp["vec"], sp["vec"]),
        compiler_params=_params())(proj3, proj3, h3, dy3, cw, cb, wa, ba, wx, bx, lam)


def _rope_tables(s):
    half = ROPE_DIM // 2
    pos = jnp.arange(s, dtype=F32)
    inv_freq = ROPE_THETA ** (-jnp.arange(0, ROPE_DIM, 2, dtype=F32) / ROPE_DIM)
    ang = pos[:, None] * inv_freq[None, :]
    cos, sin = jnp.cos(ang), jnp.sin(ang)
    rest = HEAD_DIM - ROPE_DIM
    cos64 = jnp.concatenate([cos, cos, jnp.ones((s, rest), F32)], axis=1)
    sin64 = jnp.concatenate([-sin, sin, jnp.zeros((s, rest), F32)], axis=1)
    assert half * 2 == ROPE_DIM
    return jnp.tile(cos64, (1, LANES // HEAD_DIM)), jnp.tile(sin64, (1, LANES // HEAD_DIM))


def _swap_rot_halves(v):
    half = ROPE_DIM // 2
    lane = lax.broadcasted_iota(jnp.int32, v.shape, 1) % HEAD_DIM
    second = jnp.where(lane < ROPE_DIM, pltpu.roll(v, half, axis=1), 0.0)
    return jnp.where(lane < half, pltpu.roll(v, LANES - half, axis=1), second)


def _rope(v, cos, sin):
    tiles = []
    for t in range(v.shape[1] // LANES):
        vt = v[:, t * LANES:(t + 1) * LANES]
        tiles.append(vt * cos + _swap_rot_halves(vt) * sin)
    return tiles[0] if len(tiles) == 1 else jnp.concatenate(tiles, axis=1)


def _unrope(v, cos, sin):
    tiles = []
    for t in range(v.shape[1] // LANES):
        vt = v[:, t * LANES:(t + 1) * LANES]
        tiles.append(vt * cos + _swap_rot_halves(vt * sin))
    return tiles[0] if len(tiles) == 1 else jnp.concatenate(tiles, axis=1)


HEADS_PER_STEP = 8
QW = HEADS_PER_STEP * HEAD_DIM
N_PAIRS = N_Q_HEADS // HEADS_PER_STEP
Q_PER_KV = N_Q_HEADS // N_KV_HEADS
KV_PER_STEP = HEADS_PER_STEP // Q_PER_KV


QT_COLS = Q_PER_KV * ATT_BLK


def _attn_scratch(s, with_vt):
    nb = s // ATT_BLK
    pad = s + ATT_BLK
    shapes = [pltpu.VMEM((nb, LANES, QT_COLS), BF16),
              pltpu.VMEM((KV_PER_STEP, pad, LANES), BF16),
              pltpu.VMEM((KV_PER_STEP, pad, LANES), BF16)]
    if with_vt:
        shapes.append(pltpu.VMEM((LANES, pad), BF16))
    return shapes


def _attn_specs(s, order):
    def mk(width, base, **kw):
        if order == "bp":
            return pl.BlockSpec((None, s, width), lambda b, p: (b, 0, base + p), **kw)
        return pl.BlockSpec((None, s, width), lambda p, b: (b, 0, base + p), **kw)
    one = dict(pipeline_mode=pl.Buffered(1))
    tbl = pl.BlockSpec((s, LANES), lambda *_: (0, 0))
    return dict(q=mk(QW, OFF_Q // QW), k=mk(LANES, OFF_K // LANES), v=mk(LANES, OFF_V // LANES),
                g=mk(QW, OFF_ATTN_G // QW), act=mk(QW, 0), kv=mk(LANES, 0), tbl=tbl,
                q1=mk(QW, OFF_Q // QW, **one), g1=mk(QW, OFF_ATTN_G // QW, **one), act1=mk(QW, 0, **one),
                smem=pl.BlockSpec(memory_space=pltpu.SMEM))


def _to_qt(blk):
    rows = []
    for j in range(KV_PER_STEP):
        cols = []
        for tt in range(2):
            t = 2 * j + tt
            tr = blk[:, t * LANES:(t + 1) * LANES].T
            cols += [tr[0:HEAD_DIM, :], tr[HEAD_DIM:, :]]
        rows.append(jnp.concatenate(cols, axis=1))
    return jnp.concatenate(rows, axis=0)


def _from_qt(xt):
    tiles = []
    for j in range(KV_PER_STEP):
        for tt in range(2):
            g0 = 2 * tt
            pair = jnp.concatenate([xt[j * HEAD_DIM:(j + 1) * HEAD_DIM, (g0 + i) * ATT_BLK:(g0 + i + 1) * ATT_BLK]
                                    for i in range(2)], axis=0)
            tiles.append(pair.T)
    return jnp.concatenate(tiles, axis=1)


def _attn_prep(q_ref, k_ref, v_ref, cos_ref, sin_ref, qt_sc, km_sc, vm_sc, t_sc, transposed, nb):
    zeros = jnp.zeros((ATT_BLK, LANES), BF16)
    for j in range(KV_PER_STEP):
        km_sc[j, 0:ATT_BLK, :] = zeros
        vm_sc[j, 0:ATT_BLK, :] = zeros
    t_sc[:, 0:ATT_BLK] = zeros
    head_of_lane = lax.broadcasted_iota(jnp.int32, (ATT_BLK, LANES), 1) // HEAD_DIM

    def prep(n, carry):
        r0 = pl.multiple_of(n * ATT_BLK, ATT_BLK)
        cs = cos_ref[pl.ds(r0, ATT_BLK), :]
        sn = sin_ref[pl.ds(r0, ATT_BLK), :]
        qt_sc[n] = _to_qt(_rope(q_ref[pl.ds(r0, ATT_BLK), :], cs, sn) * ATTN_SCALE).astype(BF16)
        k = _rope(k_ref[pl.ds(r0, ATT_BLK), :], cs, sn)
        v = v_ref[pl.ds(r0, ATT_BLK), :]
        for j in range(KV_PER_STEP):
            km_sc[j, pl.ds(r0 + ATT_BLK, ATT_BLK), :] = jnp.where(head_of_lane == j, k, 0.0).astype(BF16)
            vm_sc[j, pl.ds(r0 + ATT_BLK, ATT_BLK), :] = jnp.where(head_of_lane == j, v, 0.0).astype(BF16)
        t_sc[:, pl.ds(r0 + ATT_BLK, ATT_BLK)] = (k if transposed == "k" else v).T.astype(BF16)
        return carry

    lax.fori_loop(0, nb, prep, 0)


def _band_mask_t(n):
    shape = (2 * ATT_BLK, QT_COLS)
    key = lax.broadcasted_iota(jnp.int32, shape, 0)
    qry = lax.broadcasted_iota(jnp.int32, shape, 1) % ATT_BLK
    lo = jnp.where(n == 0, ATT_BLK, 0)
    return (key > qry) & (key <= qry + WINDOW) & (key >= lo)


def _sink_row(sink_ref, first):
    return jnp.concatenate([jnp.full((1, ATT_BLK), sink_ref[first + g], F32) for g in range(Q_PER_KV)], axis=1)


def _softmax_cols(scores_t, valid, sink):
    sc = jnp.where(valid, scores_t, NEG_BIG)
    m = jnp.maximum(jnp.max(sc, axis=0, keepdims=True), sink)
    e = jnp.exp(sc - m)
    es = jnp.exp(sink - m)
    inv = 1.0 / (jnp.sum(e, axis=0, keepdims=True) + es)
    return e * inv, es * inv


def _attn_fwd(proj3, sinks, cosf, sinf):
    bsz, s, _ = proj3.shape
    nb = s // ATT_BLK
    sp = _attn_specs(s, "bp")

    def body(sink_ref, q_ref, k_ref, v_ref, g_ref, cos_ref, sin_ref, o_ref, y_ref, qt_sc, km_sc, vm_sc, vt_sc):
        p = pl.program_id(1)
        _attn_prep(q_ref, k_ref, v_ref, cos_ref, sin_ref, qt_sc, km_sc, vm_sc, vt_sc, "v", nb)
        kv_row = lax.broadcasted_iota(jnp.int32, (LANES, QT_COLS), 0) // HEAD_DIM

        def blk(n, carry):
            r0 = pl.multiple_of(n * ATT_BLK, ATT_BLK)
            valid = _band_mask_t(n)
            rq = qt_sc[n]
            vt = vt_sc[:, pl.ds(r0, 2 * ATT_BLK)]
            ots = []
            for j in range(KV_PER_STEP):
                st = _dot(km_sc[j, pl.ds(r0, 2 * ATT_BLK), :], rq)
                pt, _ = _softmax_cols(st, valid, _sink_row(sink_ref, p * HEADS_PER_STEP + j * Q_PER_KV))
                ots.append(_dot(vt, pt.astype(BF16)))
            o = _from_qt(jnp.where(kv_row == 0, ots[0], ots[1]))
            o_ref[pl.ds(r0, ATT_BLK), :] = o
            g = g_ref[pl.ds(r0, ATT_BLK), :]
            y_ref[pl.ds(r0, ATT_BLK), :] = (o * (g * _sigmoid(g))).astype(BF16)
            return carry

        lax.fori_loop(0, nb, blk, 0)

    return pl.pallas_call(
        body, name="attn_fwd", grid=(bsz, N_PAIRS),
        out_shape=(jax.ShapeDtypeStruct((bsz, s, D_ATTN), F32), jax.ShapeDtypeStruct((bsz, s, D_ATTN), BF16)),
        in_specs=[sp["smem"], sp["q"], sp["k"], sp["v"], sp["g"], sp["tbl"], sp["tbl"]],
        out_specs=(sp["act"], sp["act"]),
        scratch_shapes=_attn_scratch(s, True),
        compiler_params=_params())(sinks, proj3, proj3, proj3, proj3, cosf, sinf)


def _attn_bwd(proj3, o3, dy3, sinks, cosf, sinf):
    bsz, s, _ = proj3.shape
    nb = s // ATT_BLK
    sp = _attn_specs(s, "pb")

    def body(sink_ref, q_ref, k_ref, v_ref, g_ref, o_ref, dy_ref, cos_ref, sin_ref,
             dq_ref, dk_ref, dv_ref, dg_ref, ds_ref, qt_sc, km_sc, vm_sc, kt_sc, dot_sc, dqt_sc, dk_sc, dv_sc):
        p = pl.program_id(0)
        b = pl.program_id(1)
        _attn_prep(q_ref, k_ref, v_ref, cos_ref, sin_ref, qt_sc, km_sc, vm_sc, kt_sc, "k", nb)
        dk_sc[...] = jnp.zeros_like(dk_sc)
        dv_sc[...] = jnp.zeros_like(dv_sc)

        def gate(n, carry):
            r0 = pl.multiple_of(n * ATT_BLK, ATT_BLK)
            g = g_ref[pl.ds(r0, ATT_BLK), :]
            dy = dy_ref[pl.ds(r0, ATT_BLK), :]
            sg = _sigmoid(g)
            dg_ref[pl.ds(r0, ATT_BLK), :] = (dy * o_ref[pl.ds(r0, ATT_BLK), :] * (sg * (1.0 + g * (1.0 - sg)))).astype(BF16)
            dot_sc[n] = _to_qt(dy * (g * sg)).astype(BF16)
            return carry

        lax.fori_loop(0, nb, gate, 0)
        kv_lane = lax.broadcasted_iota(jnp.int32, (2 * ATT_BLK, LANES), 1) // HEAD_DIM
        kv_row = lax.broadcasted_iota(jnp.int32, (LANES, QT_COLS), 0) // HEAD_DIM

        def blk(n, acc):
            r0 = pl.multiple_of(n * ATT_BLK, ATT_BLK)
            valid = _band_mask_t(n)
            rq = qt_sc[n]
            rd = dot_sc[n]
            kt = kt_sc[:, pl.ds(r0, 2 * ATT_BLK)]
            dvs, dks, dqs, new_acc = [], [], [], []
            for j in range(KV_PER_STEP):
                st = _dot(km_sc[j, pl.ds(r0, 2 * ATT_BLK), :], rq)
                pt, ps = _softmax_cols(st, valid, _sink_row(sink_ref, p * HEADS_PER_STEP + j * Q_PER_KV))
                dpt = _dot(vm_sc[j, pl.ds(r0, 2 * ATT_BLK), :], rd)
                delta = jnp.sum(pt * dpt, axis=0, keepdims=True)
                dst = (pt * (dpt - delta)).astype(BF16)
                new_acc.append(acc[j] + ps * delta)
                dvs.append(_dot_nt(pt.astype(BF16), rd))
                dks.append(_dot_nt(dst, rq))
                dqs.append(_dot(kt, dst))
            dv_sc[pl.ds(r0, 2 * ATT_BLK), :] += jnp.where(kv_lane == 0, dvs[0], dvs[1])
            dk_sc[pl.ds(r0, 2 * ATT_BLK), :] += jnp.where(kv_lane == 0, dks[0], dks[1])
            dqt_sc[n] = jnp.where(kv_row == 0, dqs[0], dqs[1]) * ATTN_SCALE
            return tuple(new_acc)

        acc = lax.fori_loop(0, nb, blk, tuple(jnp.zeros((1, QT_COLS), F32) for _ in range(KV_PER_STEP)))
        lane1 = lax.broadcasted_iota(jnp.int32, (1, LANES), 1)
        dsink = jnp.zeros((1, LANES), F32)
        for j in range(KV_PER_STEP):
            for i in range(Q_PER_KV):
                part = jnp.sum(acc[j][:, i * ATT_BLK:(i + 1) * ATT_BLK], axis=1, keepdims=True)
                dsink = dsink - jnp.where(lane1 == j * Q_PER_KV + i, part, 0.0)

        @pl.when(b == 0)
        def _():
            ds_ref[...] = jnp.zeros_like(ds_ref)

        ds_ref[...] += dsink

        def post(n, carry):
            r0 = pl.multiple_of(n * ATT_BLK, ATT_BLK)
            cs = cos_ref[pl.ds(r0, ATT_BLK), :]
            sn = sin_ref[pl.ds(r0, ATT_BLK), :]
            dq_ref[pl.ds(r0, ATT_BLK), :] = _unrope(_from_qt(dqt_sc[n]), cs, sn).astype(BF16)
            dk_ref[pl.ds(r0, ATT_BLK), :] = _unrope(dk_sc[pl.ds(r0 + ATT_BLK, ATT_BLK), :], cs, sn).astype(BF16)
            dv_ref[pl.ds(r0, ATT_BLK), :] = dv_sc[pl.ds(r0 + ATT_BLK, ATT_BLK), :].astype(BF16)
            return carry

        lax.fori_loop(0, nb, post, 0)

    act = jax.ShapeDtypeStruct((bsz, s, D_ATTN), BF16)
    kvs = jax.ShapeDtypeStruct((bsz, s, D_KV), BF16)
    return pl.pallas_call(
        body, name="attn_bwd", grid=(N_PAIRS, bsz),
        out_shape=(act, kvs, kvs, act, jax.ShapeDtypeStruct((N_PAIRS, 1, LANES), F32)),
        in_specs=[sp["smem"], sp["q1"], sp["k"], sp["v"], sp["g1"], sp["act1"], sp["act1"], sp["tbl"], sp["tbl"]],
        out_specs=(sp["act"], sp["kv"], sp["kv"], sp["act"], pl.BlockSpec((None, 1, LANES), lambda p, b: (p, 0, 0))),
        scratch_shapes=_attn_scratch(s, True) + [pltpu.VMEM((nb, LANES, QT_COLS), BF16),
                                                 pltpu.VMEM((nb, LANES, QT_COLS), F32),
                                                 pltpu.VMEM((s + ATT_BLK, LANES), F32),
                                                 pltpu.VMEM((s + ATT_BLK, LANES), F32)],
        compiler_params=_params())(sinks, proj3, proj3, proj3, proj3, o3, dy3, cosf, sinf)


def _merge_fwd_bwd(x, tgt, y_rnn, y_attn, proj, w_r, w_a, w_o, gf):
    t, d = x.shape
    tm = min(t, 256)

    hw = d // 2

    def body(x_ref, t_ref, yr_ref, ya_ref, mr0_ref, mr1_ref, ma0_ref, ma1_ref, wr_ref, wa_ref, wo_ref, gf_ref,
             dmg_ref, dyr_ref, dya_ref, mg_ref, dx2_ref, dx2b_ref, dpr_ref, dpa_ref, loss_ref, dgf_ref):
        i = pl.program_id(0)
        wr = wr_ref[...]
        wa = wa_ref[...]
        wo = wo_ref[...]
        gfv = gf_ref[...]
        pr = _dot(yr_ref[...], wr)
        pa = _dot(ya_ref[...], wa)
        sr = _sigmoid(jnp.concatenate([mr0_ref[...], mr1_ref[...]], axis=1))
        sa = _sigmoid(jnp.concatenate([ma0_ref[...], ma1_ref[...]], axis=1))
        mb = (sr * pr + sa * pa).astype(BF16)
        mg_ref[...] = mb
        x2 = x_ref[...] + _dot(mb, wo)
        r2 = lax.rsqrt(jnp.mean(x2 * x2, axis=-1, keepdims=True) + NORM_EPS)
        nrm = x2 * r2
        err = nrm * gfv - t_ref[...]
        dy = err * (1.0 / d)
        dn = dy * gfv
        dx2 = r2 * (dn - nrm * jnp.mean(dn * nrm, axis=-1, keepdims=True))
        dx2_ref[...] = dx2
        dx2b = dx2.astype(BF16)
        dx2b_ref[...] = dx2b
        dmerged = _dot_nt(dx2b, wo)
        dpr = (dmerged * sr).astype(BF16)
        dpa = (dmerged * sa).astype(BF16)
        dpr_ref[...] = dpr
        dpa_ref[...] = dpa
        dmg_ref[:, 0:d] = (dmerged * pr * (sr * (1.0 - sr))).astype(BF16)
        dmg_ref[:, d:2 * d] = (dmerged * pa * (sa * (1.0 - sa))).astype(BF16)
        dyr_ref[...] = _dot_nt(dpr, wr)
        dya_ref[...] = _dot_nt(dpa, wa)

        @pl.when(i == 0)
        def _():
            loss_ref[...] = jnp.zeros_like(loss_ref)
            dgf_ref[...] = jnp.zeros_like(dgf_ref)

        loss_ref[...] += jnp.full((1, LANES), 0.5 / d, F32) * jnp.sum(err * err)
        dgf_ref[...] += jnp.sum(dy * nrm, axis=0, keepdims=True)

    tile = pl.BlockSpec((tm, d), lambda i: (i, 0))
    wsp = pl.BlockSpec((d, d), lambda i: (0, 0))

    def gate(col_blk):
        return pl.BlockSpec((tm, hw), lambda i: (i, col_blk))

    fb = jax.ShapeDtypeStruct((t, d), BF16)
    ff = jax.ShapeDtypeStruct((t, d), F32)
    return pl.pallas_call(
        body, name="merge_fwd_bwd", grid=(t // tm,),
        out_shape=(jax.ShapeDtypeStruct((t, 2 * d), BF16), ff, ff, fb, ff, fb, fb, fb,
                   jax.ShapeDtypeStruct((1, LANES), F32), jax.ShapeDtypeStruct((1, d), F32)),
        in_specs=[tile, tile, tile, tile] + [gate(OFF_MERGE_R // hw + j) for j in range(4)] + [
            wsp, wsp, wsp, pl.BlockSpec((1, d), lambda i: (0, 0))],
        out_specs=(pl.BlockSpec((tm, 2 * d), lambda i: (i, 0)), tile, tile, tile, tile, tile, tile, tile,
                   pl.BlockSpec((1, LANES), lambda i: (0, 0)), pl.BlockSpec((1, d), lambda i: (0, 0))),
        compiler_params=_params())(x, tgt, y_rnn, y_attn, proj, proj, proj, proj, w_r, w_a, w_o, gf)


def _local_grads(x, tgt, norm_g, w_in_bm, conv_w, conv_b, lru_w_a, lru_b_a, lru_w_x, lru_b_x, lam, sinks,
                 w_r, w_a, w_o, gf):
    bsz, s, d = x.shape
    t = bsz * s
    x2 = x.reshape(t, d)
    h = _rmsnorm_fwd(x2, norm_g)
    proj = _in_proj(h, w_in_bm)
    proj3 = proj.reshape(bsz, s, D_IN)
    h_lru, y_rnn = _lru_fwd(proj3, conv_w, conv_b, lru_w_a, lru_b_a, lru_w_x, lru_b_x, lam)
    cosf, sinf = _rope_tables(s)
    o_attn, y_attn = _attn_fwd(proj3, sinks, cosf, sinf)
    y_rnn2 = y_rnn.reshape(t, d)
    y_attn2 = y_attn.reshape(t, d)
    dmg, dyr, dya, merged, dx2, dx2b, dpr, dpa, loss, dgf = _merge_fwd_bwd(
        x2, tgt.reshape(t, d), y_rnn2, y_attn2, proj, w_r, w_a, w_o, gf)
    dup, dgr, dcw, dcb, dwa, dba, dwx, dbx, dlam = _lru_bwd(
        proj3, h_lru, dyr.reshape(bsz, s, d), conv_w, conv_b, lru_w_a, lru_b_a, lru_w_x, lru_b_x, lam)
    dq, dk, dv, dga, dsink = _attn_bwd(proj3, o_attn, dya.reshape(bsz, s, d), sinks, cosf, sinf)
    dproj = jnp.concatenate([dup, dgr, dq, dk, dv, dga, dmg.reshape(bsz, s, 2 * d)], axis=-1).reshape(t, D_IN)
    dh = _grad_h(dproj, w_in_bm)
    grad_x, dng = _rmsnorm_bwd(x2, dh, dx2, norm_g)
    small = dict(norm_g=dng, conv_w=dcw, conv_b=dcb, lru_w_a=dwa, lru_b_a=dba, lru_w_x=dwx, lru_b_x=dbx,
                 lru_lambda=dlam, attn_sinks=dsink[:, 0, :HEADS_PER_STEP].reshape(1, N_Q_HEADS), final_norm_g=dgf)
    squares = [(y_rnn2, dpr), (y_attn2, dpa), (merged, dx2b)]
    return loss[0, 0], grad_x.reshape(bsz, s, d), h, dproj, squares, small


ANY = pl.BlockSpec(memory_space=pl.ANY)


def _mesh_pos():
    return lax.axis_index("x"), lax.axis_index("y"), lax.axis_index("c")


def _chip_peers(x, y, c):
    return [((1 - x, y, c), 2 * (1 - x) + y), ((x, 1 - y, c), 2 * x + (1 - y)),
            ((1 - x, 1 - y, c), 2 * (1 - x) + (1 - y))]


def _remote(src, dst, send_sems, recv_sems, idx, peer):
    return pltpu.make_async_remote_copy(src_ref=src, dst_ref=dst, send_sem=send_sems.at[idx],
                                        recv_sem=recv_sems.at[idx], device_id=peer, device_id_type=MESH)


def _all_gather(bufs, split):
    n = len(bufs)
    n_fwd = 3 * sum(split)

    def body(*refs):
        ins, outs = refs[:n], refs[n:2 * n]
        send_sems, recv_sems, fsend_sems, frecv_sems = refs[2 * n:]
        x, y, c = _mesh_pos()
        me = 2 * x + y
        sib = (x, y, 1 - c)
        peers = _chip_peers(x, y, c)

        def part(ref, slot, t, half):
            if not split[t]:
                return ref.at[slot]
            hr = bufs[t].shape[1] // 2
            return ref.at[slot, pl.ds(pl.multiple_of(half * hr, 8), hr), :]

        sends, recvs = [], []
        for t in range(n):
            for k, (peer, pj) in enumerate(peers):
                src = part(ins[t], me, t, c)
                sends.append(_remote(src, part(outs[t], me, t, c), send_sems, recv_sems, 3 * t + k, peer))
                recvs.append(_remote(src, part(outs[t], pj, t, c), send_sems, recv_sems, 3 * t + k, peer))
        for cp in sends:
            cp.start()
        fwd, fwd_recv = [], []
        for t in range(n):
            for k, (peer, pj) in enumerate(peers):
                recvs[3 * t + k].wait_recv()
                if split[t]:
                    got = part(outs[t], pj, t, c)
                    f = len(fwd)
                    fwd.append(_remote(got, got, fsend_sems, frecv_sems, f, sib))
                    fwd_recv.append(_remote(got, part(outs[t], pj, t, 1 - c), fsend_sems, frecv_sems, f, sib))
                    fwd[-1].start()
        for cp in sends:
            cp.wait_send()
        for snd, rcv in zip(fwd, fwd_recv):
            snd.wait_send()
            rcv.wait_recv()

    return pl.pallas_call(
        body, name="all_gather_weights", out_shape=[jax.ShapeDtypeStruct(a.shape, a.dtype) for a in bufs],
        in_specs=[ANY] * n, out_specs=[ANY] * n, input_output_aliases={t: t for t in range(n)},
        scratch_shapes=[pltpu.SemaphoreType.DMA((3 * n,)), pltpu.SemaphoreType.DMA((3 * n,)),
                        pltpu.SemaphoreType.DMA((n_fwd,)), pltpu.SemaphoreType.DMA((n_fwd,))],
        compiler_params=_params())(*bufs)


def _pair_exchange(bigs, small):
    n = len(bigs)

    def body(*refs):
        ins, outs = refs[:n + 1], refs[n + 1:2 * n + 2]
        send_sems, recv_sems = refs[2 * n + 2:]
        x, y, c = _mesh_pos()
        sib = (x, y, 1 - c)
        cps = []
        for t in range(n):
            hr = bigs[t].shape[1] // 2
            src = ins[t].at[:, pl.ds(pl.multiple_of((1 - c) * hr, 8), hr), :]
            cps.append(_remote(src, outs[t], send_sems, recv_sems, t, sib))
        cps.append(_remote(ins[n], outs[n], send_sems, recv_sems, n, sib))
        for cp in cps:
            cp.start()
        for cp in cps:
            cp.wait()

    out_shape = [jax.ShapeDtypeStruct((a.shape[0], a.shape[1] // 2, a.shape[2]), a.dtype) for a in bigs]
    out_shape.append(jax.ShapeDtypeStruct(small.shape, small.dtype))
    return pl.pallas_call(
        body, name="pair_exchange", out_shape=out_shape, in_specs=[ANY] * (n + 1), out_specs=[ANY] * (n + 1),
        scratch_shapes=[pltpu.SemaphoreType.DMA((n + 1,)), pltpu.SemaphoreType.DMA((n + 1,))],
        compiler_params=_params())(*bigs, small)


def _row_tile(rows, row_bytes, cap_bytes=2 * 1024 * 1024):
    best = None
    for tr in range(8, rows + 1, 8):
        if rows % tr == 0 and tr * row_bytes <= cap_bytes:
            best = tr
    return best if best is not None else rows


def _add2(a, b, name):
    r, w = a.shape
    tr = _row_tile(r, w * 4)

    def body(a_ref, b_ref, o_ref):
        o_ref[...] = a_ref[...] + b_ref[...]

    spec = pl.BlockSpec((tr, w), lambda i: (i, 0))
    return pl.pallas_call(body, name=name, grid=(r // tr,), out_shape=jax.ShapeDtypeStruct((r, w), F32),
                          in_specs=[spec, spec], out_specs=spec, compiler_params=_params())(a, b)


def _chip_exchange(bigs_b, small_slots):
    n = len(bigs_b)

    def body(*refs):
        ins, outs = refs[:n + 1], refs[n + 1:2 * n + 2]
        send_sems, recv_sems = refs[2 * n + 2:]
        x, y, c = _mesh_pos()
        me = 2 * x + y
        sends, recvs = [], []
        for t in range(n + 1):
            for k, (peer, pj) in enumerate(_chip_peers(x, y, c)):
                src = ins[t].at[pj] if t < n else ins[t].at[me]
                sends.append(_remote(src, outs[t].at[me], send_sems, recv_sems, 3 * t + k, peer))
                recvs.append(_remote(src, outs[t].at[pj], send_sems, recv_sems, 3 * t + k, peer))
        for cp in sends:
            cp.start()
        for snd, rcv in zip(sends, recvs):
            snd.wait_send()
            rcv.wait_recv()

    out_shape = [jax.ShapeDtypeStruct(a.shape, a.dtype) for a in bigs_b + [small_slots]]
    return pl.pallas_call(
        body, name="chip_exchange", out_shape=out_shape, in_specs=[ANY] * (n + 1), out_specs=[ANY] * (n + 1),
        input_output_aliases={n: n},
        scratch_shapes=[pltpu.SemaphoreType.DMA((3 * n + 3,)), pltpu.SemaphoreType.DMA((3 * n + 3,))],
        compiler_params=_params())(*bigs_b, small_slots)


def _tree_sum4(a, idx):
    _, r, w = a.shape
    tr = _row_tile(r, w * 4)
    steps = r // tr

    def body(idx_ref, a_ref, o_ref):
        o_ref[...] = (a_ref[0] + a_ref[1]) + (a_ref[2] + a_ref[3])

    grid_spec = pltpu.PrefetchScalarGridSpec(
        num_scalar_prefetch=1, grid=(steps,),
        in_specs=[pl.BlockSpec((N_CHIPS, tr, w), lambda i, idx_ref: (0, i, 0))],
        out_specs=pl.BlockSpec((tr, w), lambda i, idx_ref: (idx_ref[4] * steps + i, 0)))
    return pl.pallas_call(body, name="small_sum", grid_spec=grid_spec, out_shape=jax.ShapeDtypeStruct((2 * r, w), F32),
                          compiler_params=_params())(idx, a)


def _pair_gather(bufs):
    n = len(bufs)

    def body(*refs):
        ins, outs = refs[:n], refs[n:2 * n]
        send_sems, recv_sems = refs[2 * n:]
        x, y, c = _mesh_pos()
        sib = (x, y, 1 - c)
        sends, recvs = [], []
        for t in range(n):
            hr = bufs[t].shape[0] // 2
            mine = pl.ds(pl.multiple_of(c * hr, 8), hr)
            theirs = pl.ds(pl.multiple_of((1 - c) * hr, 8), hr)
            sends.append(_remote(ins[t].at[mine, :], outs[t].at[mine, :], send_sems, recv_sems, t, sib))
            recvs.append(_remote(ins[t].at[mine, :], outs[t].at[theirs, :], send_sems, recv_sems, t, sib))
        for cp in sends:
            cp.start()
        for snd, rcv in zip(sends, recvs):
            snd.wait_send()
            rcv.wait_recv()

    return pl.pallas_call(
        body, name="pair_gather", out_shape=[jax.ShapeDtypeStruct(a.shape, a.dtype) for a in bufs],
        in_specs=[ANY] * n, out_specs=[ANY] * n, input_output_aliases={t: t for t in range(n)},
        scratch_shapes=[pltpu.SemaphoreType.DMA((n,)), pltpu.SemaphoreType.DMA((n,))],
        compiler_params=_params())(*bufs)


def _all_reduce_small(small, x, y, c):
    me = 2 * x + y
    (got,) = _pair_exchange([], small)
    small_pair = _add2(small, got, "pair_sum_small")
    small_slots = _put_slot(small_pair, N_CHIPS, jnp.stack([me, c]).astype(jnp.int32), PK_HALF, F32, "small_slot")
    (slots,) = _chip_exchange([], small_slots)
    idx = jnp.stack([me, me, me, me, c]).astype(jnp.int32)
    (out,) = _pair_gather([_tree_sum4(slots, idx)])
    return out


XOR_ORDER = (3, 2, 1)


def _grads_reduce_scatter(h, dproj, squares, idx):
    t, d = h.shape
    nsq = len(squares)
    hr = d // 2
    qr = ROW_BLK // 2
    tk = min(t, 1024)
    nk = t // tk
    last = N_CHIPS - 1
    n_phase = 3

    def dest(s, idx_ref):
        xo = jnp.where(s == 0, XOR_ORDER[0], jnp.where(s == 1, XOR_ORDER[1], jnp.where(s == 2, XOR_ORDER[2], 0)))
        return idx_ref[0] ^ xo

    def k_sq(p, k):
        return jnp.where(p == 0, k, nk - 1)

    def k_w(p, k):
        return jnp.where(p == 0, 0, k)

    in_specs = [
        pl.BlockSpec((tk, hr), lambda s, p, k, idx_ref: (k_w(p, k), (1 - idx_ref[1] + jnp.maximum(p - 1, 0)) % 2)),
        pl.BlockSpec((tk, W_BLK), lambda s, p, k, idx_ref: (k_w(p, k), dest(s, idx_ref)))]
    for q in range(nsq):
        in_specs.append(pl.BlockSpec((tk, ROW_BLK), lambda s, p, k, idx_ref: (k_sq(p, k), dest(s, idx_ref))))
        in_specs.append(pl.BlockSpec((tk, d), lambda s, p, k, idx_ref: (k_sq(p, k), 0)))

    def body(idx_ref, *refs):
        nj = 1 + nsq
        h_ref, dp_ref = refs[0], refs[1]
        sq_in = refs[2:2 + 2 * nsq]
        outs = refs[2 * nj:3 * nj]
        landing = refs[3 * nj:4 * nj]
        sc = refs[4 * nj:]
        acc_w, xr_w, sb_w = sc[0:3]
        sq_sc = [sc[3 + 3 * q:6 + 3 * q] for q in range(nsq)]
        x_send, x_recv, i_send, i_recv, f_send, f_recv, o_sem, l_sem = sc[3 * nj:]
        s, p, k = pl.program_id(0), pl.program_id(1), pl.program_id(2)
        x, y, c = _mesh_pos()
        sib = (x, y, 1 - c)
        peers = [((1 - x) if xo & 2 else x, (1 - y) if xo & 1 else y, c) for xo in XOR_ORDER]
        slot = s % 2
        mine_w = pl.ds(pl.multiple_of(c * hr, 8), hr)
        theirs_w = pl.ds(pl.multiple_of((1 - c) * hr, 8), hr)
        mine_q = pl.ds(pl.multiple_of(c * qr, 8), qr)
        theirs_q = pl.ds(pl.multiple_of((1 - c) * qr, 8), qr)

        def exch(j, src, dst):
            return _remote(src, dst, x_send, x_recv, 2 * j + slot, sib)

        sbufs = [sb_w] + [sq_sc[q][2] for q in range(nsq)]

        def ici(j, ss):
            return _remote(sbufs[j].at[ss], landing[j].at[ss], i_send, i_recv, last * j + ss, peers[ss])

        def exchanges():
            cps = [exch(0, acc_w.at[0], xr_w.at[slot])]
            cps += [exch(1 + q, sq_sc[q][0].at[theirs_q, :], sq_sc[q][1].at[slot]) for q in range(nsq)]
            return cps

        def sq_phase():
            for q in range(nsq):
                acc = sq_sc[q][0]

                @pl.when(k == 0)
                def _():
                    acc[...] = jnp.zeros((ROW_BLK, d), F32)

                acc[...] += _dot_tn(sq_in[2 * q][...], sq_in[2 * q + 1][...])

            @pl.when(k == nk - 1)
            def _():
                for cp in exchanges()[1:]:
                    cp.start()

        def w_phase(hf):
            @pl.when(k == 0)
            def _():
                acc_w[hf] = jnp.zeros((hr, W_BLK), F32)

            acc_w[hf] += _dot_tn(h_ref[...], dp_ref[...])

            @pl.when(k == nk - 1)
            def _():
                if hf == 0:
                    exchanges()[0].start()
                else:
                    finish_step()

        def finish_step():
            for cp in exchanges():
                cp.wait_recv()
                cp.wait_send()
            pairs = [acc_w[1] + xr_w[slot]] + [sq_sc[q][0][mine_q, :] + sq_sc[q][1][slot] for q in range(nsq)]
            for ss in range(last):
                @pl.when(s == ss)
                def _():
                    for j in range(nj):
                        sbufs[j][ss] = pairs[j].astype(BF16)
                        ici(j, ss).start()

            @pl.when(s == last)
            def _():
                for j in range(nj):
                    for ss in range(last):
                        ici(j, ss).wait_recv()
                        ici(j, ss).wait_send()
                stage = [pltpu.make_async_copy(landing[j], sbufs[j], l_sem.at[j]) for j in range(nj)]
                for cp in stage:
                    cp.start()
                done = []
                for j in range(nj):
                    stage[j].wait()
                    total = pairs[j]
                    for ss in range(last):
                        total = total + sbufs[j][ss].astype(F32)
                    if j == 0:
                        acc_w[1] = total
                        done.append((acc_w.at[1], mine_w, theirs_w))
                    else:
                        sq_sc[j - 1][0][mine_q, :] = total
                        done.append((sq_sc[j - 1][0].at[mine_q, :], mine_q, theirs_q))
                copies = []
                for j, (src, mine, theirs) in enumerate(done):
                    keep = pltpu.make_async_copy(src, outs[j].at[mine, :], o_sem.at[j])
                    give = _remote(src, outs[j].at[mine, :], f_send, f_recv, j, sib)
                    take = _remote(src, outs[j].at[theirs, :], f_send, f_recv, j, sib)
                    keep.start()
                    give.start()
                    copies.append((keep, give, take))
                for keep, give, take in copies:
                    keep.wait()
                    give.wait_send()
                    take.wait_recv()

        pl.when(p == 0)(sq_phase)
        for hf in range(2):
            pl.when(p == 1 + hf)(functools.partial(w_phase, hf))

    nj = 1 + nsq
    scratch = [pltpu.VMEM((2, hr, W_BLK), F32), pltpu.VMEM((2, hr, W_BLK), F32), pltpu.VMEM((last, hr, W_BLK), BF16)]
    for _ in range(nsq):
        scratch += [pltpu.VMEM((ROW_BLK, d), F32), pltpu.VMEM((2, qr, d), F32), pltpu.VMEM((last, qr, d), BF16)]
    scratch += [pltpu.SemaphoreType.DMA((2 * nj,)), pltpu.SemaphoreType.DMA((2 * nj,)),
                pltpu.SemaphoreType.DMA((last * nj,)), pltpu.SemaphoreType.DMA((last * nj,)),
                pltpu.SemaphoreType.DMA((nj,)), pltpu.SemaphoreType.DMA((nj,)), pltpu.SemaphoreType.DMA((nj,)),
                pltpu.SemaphoreType.DMA((nj,))]
    grid_spec = pltpu.PrefetchScalarGridSpec(
        num_scalar_prefetch=1, grid=(N_CHIPS, n_phase, nk), in_specs=in_specs, out_specs=[ANY] * (2 * nj),
        scratch_shapes=scratch)
    out_shape = [jax.ShapeDtypeStruct((d, W_BLK), F32)] + [jax.ShapeDtypeStruct((ROW_BLK, d), F32)] * nsq
    out_shape += [jax.ShapeDtypeStruct((last, hr, W_BLK), BF16)] + [jax.ShapeDtypeStruct((last, qr, d), BF16)] * nsq
    flat = [a for pair in squares for a in pair]
    res = pl.pallas_call(body, name="grads_reduce_scatter", grid_spec=grid_spec, out_shape=out_shape,
                         compiler_params=_params())(idx, h, dproj, *flat)
    return res[:nj]


_VEC_NAMES = ("norm_g", "conv_b", "lru_b_a", "lru_b_x", "lru_lambda", "final_norm_g")


def _pack_small(p, conv_full=None):
    rows = [p["lru_w_a"].reshape(PK_WX - PK_WA, LANES), p["lru_w_x"].reshape(PK_VEC - PK_WX, LANES)]
    rows += [p[k].reshape(8, LANES) for k in _VEC_NAMES]
    rows.append(jnp.pad(p["attn_sinks"].reshape(1, N_Q_HEADS), ((0, 7), (0, LANES - N_Q_HEADS))))
    tail = PK_ROWS - PK_CONV
    if conv_full is None:
        rows.append(jnp.zeros((tail, LANES), F32))
    else:
        rows.append(conv_full.reshape(32, LANES))
        rows.append(jnp.zeros((tail - 32, LANES), F32))
    return jnp.concatenate(rows, axis=0)


def _unpack_small(pk, like):
    out = {"lru_w_a": pk[PK_WA:PK_WX].reshape(like["lru_w_a"].shape),
           "lru_w_x": pk[PK_WX:PK_VEC].reshape(like["lru_w_x"].shape)}
    for j, k in enumerate(_VEC_NAMES):
        out[k] = pk[PK_VEC + 8 * j:PK_VEC + 8 * j + 8].reshape(like[k].shape)
    out["attn_sinks"] = pk[PK_SINK:PK_SINK + 1, :N_Q_HEADS].reshape(like["attn_sinks"].shape)
    return out


_WEIGHTS = ("norm_g", "w_in", "conv_w", "conv_b", "lru_w_a", "lru_b_a", "lru_w_x", "lru_b_x", "lru_lambda",
            "attn_sinks", "w_rnn_out", "w_attn_out", "w_o", "final_norm_g")
_SMALL = ("norm_g", "conv_b", "lru_w_a", "lru_b_a", "lru_w_x", "lru_b_x", "lru_lambda", "attn_sinks", "final_norm_g")
_ROW_SHARDED = ("w_rnn_out", "w_attn_out", "w_o")


def kernel(x, norm_g, w_in, conv_w, conv_b, lru_w_a, lru_b_a, lru_w_x, lru_b_x, lru_lambda, attn_sinks, w_rnn_out, w_attn_out, w_o, final_norm_g, loss_target, m_norm_g, m_w_in, m_conv_w, m_conv_b, m_lru_w_a, m_lru_b_a, m_lru_w_x, m_lru_b_x, m_lru_lambda, m_attn_sinks, m_w_rnn_out, m_w_attn_out, m_w_o, m_final_norm_g, v_norm_g, v_w_in, v_conv_w, v_conv_b, v_lru_w_a, v_lru_b_a, v_lru_w_x, v_lru_b_x, v_lru_lambda, v_attn_sinks, v_w_rnn_out, v_w_attn_out, v_w_o, v_final_norm_g):
    w = dict(norm_g=norm_g, w_in=w_in, conv_w=conv_w, conv_b=conv_b, lru_w_a=lru_w_a, lru_b_a=lru_b_a, lru_w_x=lru_w_x,
             lru_b_x=lru_b_x, lru_lambda=lru_lambda, attn_sinks=attn_sinks, w_rnn_out=w_rnn_out, w_attn_out=w_attn_out,
             w_o=w_o, final_norm_g=final_norm_g)
    m = dict(norm_g=m_norm_g, w_in=m_w_in, conv_w=m_conv_w, conv_b=m_conv_b, lru_w_a=m_lru_w_a, lru_b_a=m_lru_b_a,
             lru_w_x=m_lru_w_x, lru_b_x=m_lru_b_x, lru_lambda=m_lru_lambda, attn_sinks=m_attn_sinks,
             w_rnn_out=m_w_rnn_out, w_attn_out=m_w_attn_out, w_o=m_w_o, final_norm_g=m_final_norm_g)
    v = dict(norm_g=v_norm_g, w_in=v_w_in, conv_w=v_conv_w, conv_b=v_conv_b, lru_w_a=v_lru_w_a, lru_b_a=v_lru_b_a,
             lru_w_x=v_lru_w_x, lru_b_x=v_lru_b_x, lru_lambda=v_lru_lambda, attn_sinks=v_attn_sinks,
             w_rnn_out=v_w_rnn_out, w_attn_out=v_w_attn_out, w_o=v_w_o, final_norm_g=v_final_norm_g)
    mx, my, mc = _mesh_pos()
    me = 2 * mx + my
    d = D_MODEL

    slot0 = jnp.stack([me, jnp.zeros_like(me)]).astype(jnp.int32)
    bufs = [_put_slot(w[k][0], N_CHIPS, slot0, w[k].shape[1], BF16, "cast_" + k) for k in ("w_in",) + _ROW_SHARDED]
    bufs.append(_put_slot(w["conv_w"][0], N_CHIPS, slot0, CONV_WIDTH, F32, "slot_conv_w"))
    g_in, g_r, g_a, g_o, g_cw = _all_gather(bufs, [True, True, True, True, False])
    conv_full = g_cw.transpose(1, 0, 2).reshape(CONV_WIDTH, D_RNN)

    loss_local, grad_x, h, dproj, squares, gsmall = _local_grads(
        x, loss_target, w["norm_g"], g_in, conv_full, w["conv_b"], w["lru_w_a"][0], w["lru_b_a"], w["lru_w_x"][0],
        w["lru_b_x"], w["lru_lambda"], w["attn_sinks"][0], g_r.reshape(d, d), g_a.reshape(d, d), g_o.reshape(d, d),
        w["final_norm_g"].reshape(1, d))
    loss = lax.psum(loss_local, ("x", "y", "c"))

    gpack = _pack_small(gsmall, gsmall["conv_w"])
    spack = _all_reduce_small(gpack, mx, my, mc)
    f_in, f_r, f_a, f_o = _grads_reduce_scatter(h, dproj, squares, jnp.stack([me, mc]).astype(jnp.int32))

    grads = _unpack_small(spack, w)
    conv_all = spack[PK_CONV:PK_CONV + 32].reshape(CONV_WIDTH, D_RNN)
    grads["conv_w"] = lax.dynamic_slice_in_dim(conv_all, me * (D_RNN // N_CHIPS), D_RNN // N_CHIPS, axis=1)[None]
    grads["w_in"] = f_in[None]
    grads["w_rnn_out"], grads["w_attn_out"], grads["w_o"] = f_r[None], f_a[None], f_o[None]

    delta, new_m, new_v = {}, {}, {}
    for k in ("w_in",) + _ROW_SHARDED:
        dk, mk, vk = _adamw(w[k][0], grads[k][0], m[k][0], v[k][0], "adamw_" + k)
        delta[k], new_m[k], new_v[k] = dk[None], mk[None], vk[None]
    shp = (2 * CONV_WIDTH, LANES)
    dk, mk, vk = _adamw(w["conv_w"].reshape(shp), grads["conv_w"].reshape(shp), m["conv_w"].reshape(shp),
                        v["conv_w"].reshape(shp), "adamw_conv_w")
    delta["conv_w"], new_m["conv_w"], new_v["conv_w"] = (a.reshape(w["conv_w"].shape) for a in (dk, mk, vk))
    dk, mk, vk = _adamw(_pack_small(w), spack, _pack_small(m), _pack_small(v), "adamw_small")
    for src, dst in ((dk, delta), (mk, new_m), (vk, new_v)):
        dst.update(_unpack_small(src, w))

    return (loss, grad_x, *[grads[k] for k in _WEIGHTS], *[delta[k] for k in _WEIGHTS],
            *[new_m[k] for k in _WEIGHTS], *[new_v[k] for k in _WEIGHTS])
```

```python
import functools
import math

import jax
import jax.numpy as jnp
from jax import lax
from jax.experimental import pallas as pl
from jax.experimental.pallas import tpu as pltpu

F32 = jnp.float32
BF16 = jnp.bfloat16
MESH = pl.DeviceIdType.MESH

D_MODEL = 1024
D_RNN = 1024
N_RNN_BLOCKS = 8
RNN_BLOCK = D_RNN // N_RNN_BLOCKS
CONV_WIDTH = 4
LRU_C = 8.0
HEAD_DIM = 64
N_Q_HEADS = 16
N_KV_HEADS = 4
D_ATTN = N_Q_HEADS * HEAD_DIM
D_KV = N_KV_HEADS * HEAD_DIM
WINDOW = 128
ROPE_DIM = HEAD_DIM // 4
ROPE_THETA = 500000.0
NORM_EPS = 1e-6
OFF_RNN_X = 0
OFF_RNN_G = OFF_RNN_X + D_RNN
OFF_Q = OFF_RNN_G + D_RNN
OFF_K = OFF_Q + D_ATTN
OFF_V = OFF_K + D_KV
OFF_ATTN_G = OFF_V + D_KV
OFF_MERGE_R = OFF_ATTN_G + D_ATTN
OFF_MERGE_A = OFF_MERGE_R + D_MODEL
D_IN = OFF_MERGE_A + D_MODEL

ADAM_LR = 0.001
ADAM_B1 = 0.9
ADAM_B2 = 0.999
ADAM_EPS = 1e-08
ADAM_WD = 0.01
ADAM_STEP = 10

N_CHIPS = 4
W_BLK = D_IN // N_CHIPS
ROW_BLK = D_MODEL // N_CHIPS
LANES = 128
ATT_BLK = 128
VMEM_LIMIT = 56 * 1024 * 1024
NEG_BIG = -1e30
ATTN_SCALE = 1.0 / math.sqrt(HEAD_DIM)

PK_WA = 0
PK_WX = PK_WA + N_RNN_BLOCKS * RNN_BLOCK
PK_VEC = PK_WX + N_RNN_BLOCKS * RNN_BLOCK
PK_SINK = PK_VEC + 6 * 8
PK_CONV = PK_SINK + 8
PK_SCALAR = PK_CONV + 32
PK_ROWS = PK_SCALAR + 8
PK_HALF = PK_ROWS // 2


def _params(**kw):
    return pltpu.CompilerParams(vmem_limit_bytes=VMEM_LIMIT, **kw)


def _sigmoid(z):
    return 1.0 / (1.0 + jnp.exp(-z))


def _dot(a, b):
    return jnp.dot(a, b, preferred_element_type=F32)


def _dot_nt(a, b):
    return lax.dot_general(a, b, (((1,), (1,)), ((), ())), preferred_element_type=F32)


def _dot_tn(a, b):
    return lax.dot_general(a, b, (((0,), (0,)), ((), ())), preferred_element_type=F32)


def _put_slot(src, n_slots, slot_and_blk, rows, dtype, name):
    _, c = src.shape
    tr = _row_tile(rows, c * 4)
    steps = rows // tr

    def body(idx_ref, s_ref, o_ref):
        o_ref[...] = s_ref[...].astype(dtype)

    grid_spec = pltpu.PrefetchScalarGridSpec(
        num_scalar_prefetch=1, grid=(steps,),
        in_specs=[pl.BlockSpec((tr, c), lambda i, idx_ref: (idx_ref[1] * steps + i, 0))],
        out_specs=pl.BlockSpec((None, tr, c), lambda i, idx_ref: (idx_ref[0], i, 0)))
    return pl.pallas_call(body, name=name, grid_spec=grid_spec,
                          out_shape=jax.ShapeDtypeStruct((n_slots, rows, c), dtype),
                          compiler_params=_params())(slot_and_blk, src)


def _rmsnorm_fwd(x, g):
    t, d = x.shape
    tm = min(t, 512)

    def body(x_ref, g_ref, o_ref):
        xv = x_ref[...]
        r = lax.rsqrt(jnp.mean(xv * xv, axis=-1, keepdims=True) + NORM_EPS)
        o_ref[...] = (xv * r * g_ref[...]).astype(BF16)

    return pl.pallas_call(
        body, name="rmsnorm_fwd", grid=(t // tm,), out_shape=jax.ShapeDtypeStruct((t, d), BF16),
        in_specs=[pl.BlockSpec((tm, d), lambda i: (i, 0)), pl.BlockSpec((1, d), lambda i: (0, 0))],
        out_specs=pl.BlockSpec((tm, d), lambda i: (i, 0)), compiler_params=_params())(x, g)


def _rmsnorm_bwd(x, dh, dx2, g):
    t, d = x.shape
    tm = min(t, 512)

    def body(x_ref, dh_ref, dx2_ref, g_ref, gx_ref, dg_ref):
        i = pl.program_id(0)
        xv = x_ref[...]
        dhv = dh_ref[...]
        r = lax.rsqrt(jnp.mean(xv * xv, axis=-1, keepdims=True) + NORM_EPS)
        nrm = xv * r
        dn = dhv * g_ref[...]
        gx_ref[...] = dx2_ref[...] + r * (dn - nrm * jnp.mean(dn * nrm, axis=-1, keepdims=True))

        @pl.when(i == 0)
        def _():
            dg_ref[...] = jnp.zeros_like(dg_ref)

        dg_ref[...] += jnp.sum(dhv * nrm, axis=0, keepdims=True)

    return pl.pallas_call(
        body, name="rmsnorm_bwd", grid=(t // tm,),
        out_shape=(jax.ShapeDtypeStruct((t, d), F32), jax.ShapeDtypeStruct((1, d), F32)),
        in_specs=[pl.BlockSpec((tm, d), lambda i: (i, 0)), pl.BlockSpec((tm, d), lambda i: (i, 0)),
                  pl.BlockSpec((tm, d), lambda i: (i, 0)), pl.BlockSpec((1, d), lambda i: (0, 0))],
        out_specs=(pl.BlockSpec((tm, d), lambda i: (i, 0)), pl.BlockSpec((1, d), lambda i: (0, 0))),
        compiler_params=_params())(x, dh, dx2, g)


def _adamw(w, g, m, v, name):
    r, c = w.shape
    tr = _row_tile(r, c * 4, 1024 * 1024)
    c1 = 1.0 - ADAM_B1 ** ADAM_STEP
    c2 = 1.0 - ADAM_B2 ** ADAM_STEP

    def body(w_ref, g_ref, m_ref, v_ref, d_ref, nm_ref, nv_ref):
        gv = g_ref[...]
        nm = ADAM_B1 * m_ref[...] + (1.0 - ADAM_B1) * gv
        nv = ADAM_B2 * v_ref[...] + (1.0 - ADAM_B2) * (gv * gv)
        m_hat = nm / c1
        v_hat = nv / c2
        d_ref[...] = -ADAM_LR * (m_hat / (jnp.sqrt(v_hat) + ADAM_EPS) + ADAM_WD * w_ref[...])
        nm_ref[...] = nm
        nv_ref[...] = nv

    spec = pl.BlockSpec((tr, c), lambda i: (i, 0))
    sds = jax.ShapeDtypeStruct((r, c), F32)
    return pl.pallas_call(
        body, name=name, grid=(r // tr,), out_shape=(sds, sds, sds),
        in_specs=[spec, spec, spec, spec], out_specs=(spec, spec, spec), compiler_params=_params())(w, g, m, v)


def _in_proj(h, w_bm):
    t, d = h.shape
    nb, _, wb = w_bm.shape
    tm = min(t, 512)

    def body(h_ref, w_ref, o_ref):
        o_ref[...] = _dot(h_ref[...], w_ref[...])

    return pl.pallas_call(
        body, name="in_proj", grid=(nb, t // tm), out_shape=jax.ShapeDtypeStruct((t, nb * wb), F32),
        in_specs=[pl.BlockSpec((tm, d), lambda j, i: (i, 0)), pl.BlockSpec((None, d, wb), lambda j, i: (j, 0, 0))],
        out_specs=pl.BlockSpec((tm, wb), lambda j, i: (i, j)), compiler_params=_params())(h, w_bm)


def _grad_h(dproj, w_bm):
    t = dproj.shape[0]
    nb, d, wb = w_bm.shape
    tm = min(t, 1024)

    def body(dp_ref, w_ref, o_ref, acc_ref):
        k = pl.program_id(1)

        @pl.when(k == 0)
        def _():
            acc_ref[...] = jnp.zeros_like(acc_ref)

        acc_ref[...] += _dot_nt(dp_ref[...], w_ref[...])

        @pl.when(k == nb - 1)
        def _():
            o_ref[...] = acc_ref[...]

    return pl.pallas_call(
        body, name="grad_h", grid=(t // tm, nb), out_shape=jax.ShapeDtypeStruct((t, d), F32),
        in_specs=[pl.BlockSpec((tm, wb), lambda i, k: (i, k)), pl.BlockSpec((None, d, wb), lambda i, k: (k, 0, 0))],
        out_specs=pl.BlockSpec((tm, d), lambda i, k: (i, 0)),
        scratch_shapes=[pltpu.VMEM((tm, d), F32)], compiler_params=_params())(dproj, w_bm)


def _shift_down(v, d, fill):
    n = v.shape[0]
    if d % 8 == 0:
        return jnp.concatenate([jnp.full((d,) + v.shape[1:], fill, v.dtype), v[: n - d]], axis=0)
    row = lax.broadcasted_iota(jnp.int32, v.shape, 0)
    return jnp.where(row >= d, pltpu.roll(v, d, axis=0), fill)


def _shift_up(v, d, fill):
    n = v.shape[0]
    if d % 8 == 0:
        return jnp.concatenate([v[d:], jnp.full((d,) + v.shape[1:], fill, v.dtype)], axis=0)
    row = lax.broadcasted_iota(jnp.int32, v.shape, 0)
    return jnp.where(row < n - d, pltpu.roll(v, n - d, axis=0), fill)


def _scan(a, b, shift):
    n = a.shape[0]
    d = 1
    while d < n:
        b = a * shift(b, d, 0.0) + b
        if 2 * d < n:
            a = a * shift(a, d, 1.0)
        d *= 2
    return b


def _neg_expm1(y):
    series = -y * (1.0 + y * (1.0 / 2.0) * (1.0 + y * (1.0 / 3.0) * (1.0 + y * (1.0 / 4.0) * (
        1.0 + y * (1.0 / 5.0) * (1.0 + y * (1.0 / 6.0) * (1.0 + y * (1.0 / 7.0)))))))
    return jnp.where(y > -0.25, series, 1.0 - jnp.exp(y))


def _softplus(z):
    e = jnp.exp(-jnp.abs(z))
    w = 1.0 + e
    log1p = jnp.where(w == 1.0, e, jnp.log(w) * (e / jnp.where(w == 1.0, 1.0, w - 1.0)))
    return jnp.maximum(z, 0.0) + log1p


def _conv(up, cw, cb):
    out = cb + cw[CONV_WIDTH - 1:CONV_WIDTH, :] * up
    for j in range(CONV_WIDTH - 1):
        out = out + cw[j:j + 1, :] * _shift_down(up, CONV_WIDTH - 1 - j, 0.0)
    return out


def _lru_gates(u, wa_ref, ba_ref, wx_ref, bx_ref, lam_ref):
    ub = u.astype(BF16)
    r = _sigmoid(_dot(ub, wa_ref[...].astype(BF16)) + ba_ref[...])
    i = _sigmoid(_dot(ub, wx_ref[...].astype(BF16)) + bx_ref[...])
    sp = _softplus(-lam_ref[...])
    log_a = (-LRU_C) * r * sp
    a = jnp.exp(log_a)
    mult = jnp.sqrt(_neg_expm1(2.0 * log_a))
    return r, i, sp, a, mult


def _lru_specs(s):
    cb = RNN_BLOCK
    vec = pl.BlockSpec((1, cb), lambda n, b: (0, n))
    return dict(
        up=pl.BlockSpec((None, s, cb), lambda n, b: (b, 0, OFF_RNN_X // cb + n)),
        gr=pl.BlockSpec((None, s, cb), lambda n, b: (b, 0, OFF_RNN_G // cb + n)),
        act=pl.BlockSpec((None, s, cb), lambda n, b: (b, 0, n)),
        cw=pl.BlockSpec((CONV_WIDTH, cb), lambda n, b: (0, n)),
        vec=vec,
        wblk=pl.BlockSpec((None, cb, cb), lambda n, b: (n, 0, 0)),
    )


def _lru_fwd(proj3, cw, cb, wa, ba, wx, bx, lam):
    bsz, s, _ = proj3.shape
    sp = _lru_specs(s)

    def body(up_ref, gr_ref, cw_ref, cb_ref, wa_ref, ba_ref, wx_ref, bx_ref, lam_ref, h_ref, y_ref):
        u = _conv(up_ref[...], cw_ref[...], cb_ref[...])
        _, i, _, a, mult = _lru_gates(u, wa_ref, ba_ref, wx_ref, bx_ref, lam_ref)
        h = _scan(a, mult * (i * u), _shift_down)
        h_ref[...] = h
        g = gr_ref[...]
        y_ref[...] = (h * (g * _sigmoid(g))).astype(BF16)

    return pl.pallas_call(
        body, name="lru_fwd", grid=(N_RNN_BLOCKS, bsz),
        out_shape=(jax.ShapeDtypeStruct((bsz, s, D_RNN), F32), jax.ShapeDtypeStruct((bsz, s, D_RNN), BF16)),
        in_specs=[sp["up"], sp["gr"], sp["cw"], sp["vec"], sp["wblk"], sp["vec"], sp["wblk"], sp["vec"], sp["vec"]],
        out_specs=(sp["act"], sp["act"]), compiler_params=_params())(proj3, proj3, cw, cb, wa, ba, wx, bx, lam)


def _lru_bwd(proj3, h3, dy3, cw, cb, wa, ba, wx, bx, lam):
    bsz, s, _ = proj3.shape
    sp = _lru_specs(s)

    def body(up_ref, gr_ref, h_ref, dy_ref, cw_ref, cb_ref, wa_ref, ba_ref, wx_ref, bx_ref, lam_ref,
             dup_ref, dgr_ref, dcw_ref, dcb_ref, dwa_ref, dba_ref, dwx_ref, dbx_ref, dlam_ref):
        b = pl.program_id(1)
        up = up_ref[...]
        cwv = cw_ref[...]
        u = _conv(up, cwv, cb_ref[...])
        r, i, spv, a, mult = _lru_gates(u, wa_ref, ba_ref, wx_ref, bx_ref, lam_ref)
        h = h_ref[...]
        g = gr_ref[...]
        dy = dy_ref[...]
        sg = _sigmoid(g)
        dgr_ref[...] = (dy * h * (sg * (1.0 + g * (1.0 - sg)))).astype(BF16)
        dh = dy * (g * sg)
        adj = _scan(_shift_up(a, 1, 0.0), dh, _shift_up)
        da = adj * _shift_down(h, 1, 0.0)
        dmult = adj * (i * u)
        di = adj * mult * u
        du = adj * mult * i
        dla = da * a - dmult * (a * a) / mult
        dr = dla * ((-LRU_C) * spv)
        dsp = jnp.sum(dla * ((-LRU_C) * r), axis=0, keepdims=True)
        dza = dr * r * (1.0 - r)
        dzx = di * i * (1.0 - i)
        ub = u.astype(BF16)
        dzab = dza.astype(BF16)
        dzxb = dzx.astype(BF16)
        du = du + _dot_nt(dzab, wa_ref[...].astype(BF16)) + _dot_nt(dzxb, wx_ref[...].astype(BF16))
        dup = cwv[CONV_WIDTH - 1:CONV_WIDTH, :] * du
        for j in range(CONV_WIDTH - 1):
            dup = dup + cwv[j:j + 1, :] * _shift_up(du, CONV_WIDTH - 1 - j, 0.0)
        dup_ref[...] = dup.astype(BF16)

        @pl.when(b == 0)
        def _():
            for ref in (dcw_ref, dcb_ref, dwa_ref, dba_ref, dwx_ref, dbx_ref, dlam_ref):
                ref[...] = jnp.zeros_like(ref)

        rows = [jnp.sum(du * _shift_down(up, CONV_WIDTH - 1 - j, 0.0), axis=0, keepdims=True)
                for j in range(CONV_WIDTH - 1)]
        rows.append(jnp.sum(du * up, axis=0, keepdims=True))
        dcw_ref[...] += jnp.concatenate(rows, axis=0)
        dcb_ref[...] += jnp.sum(du, axis=0, keepdims=True)
        dwa_ref[...] += _dot_tn(ub, dzab)
        dba_ref[...] += jnp.sum(dza, axis=0, keepdims=True)
        dwx_ref[...] += _dot_tn(ub, dzxb)
        dbx_ref[...] += jnp.sum(dzx, axis=0, keepdims=True)
        dlam_ref[...] += dsp * (-_sigmoid(-lam_ref[...]))

    act_b = jax.ShapeDtypeStruct((bsz, s, D_RNN), BF16)
    vec = jax.ShapeDtypeStruct((1, D_RNN), F32)
    wsd = jax.ShapeDtypeStruct((N_RNN_BLOCKS, RNN_BLOCK, RNN_BLOCK), F32)
    return pl.pallas_call(
        body, name="lru_bwd", grid=(N_RNN_BLOCKS, bsz),
        out_shape=(act_b, act_b, jax.ShapeDtypeStruct((CONV_WIDTH, D_RNN), F32), vec, wsd, vec, wsd, vec, vec),
        in_specs=[sp["up"], sp["gr"], sp["act"], sp["act"], sp["cw"], sp["vec"], sp["wblk"], sp["vec"],
                  sp["wblk"], sp["vec"], sp["vec"]],
        out_specs=(sp["act"], sp["act"], sp["cw"], sp["vec"], sp["wblk"], sp["vec"], sp["wblk"], sp["vec"], sp["vec"]),
        compiler_params=_params())(proj3, proj3, h3, dy3, cw, cb, wa, ba, wx, bx, lam)


def _rope_tables(s):
    half = ROPE_DIM // 2
    pos = jnp.arange(s, dtype=F32)
    inv_freq = ROPE_THETA ** (-jnp.arange(0, ROPE_DIM, 2, dtype=F32) / ROPE_DIM)
    ang = pos[:, None] * inv_freq[None, :]
    cos, sin = jnp.cos(ang), jnp.sin(ang)
    rest = HEAD_DIM - ROPE_DIM
    cos64 = jnp.concatenate([cos, cos, jnp.ones((s, rest), F32)], axis=1)
    sin64 = jnp.concatenate([-sin, sin, jnp.zeros((s, rest), F32)], axis=1)
    assert half * 2 == ROPE_DIM
    return jnp.tile(cos64, (1, LANES // HEAD_DIM)), jnp.tile(sin64, (1, LANES // HEAD_DIM))


def _swap_rot_halves(v):
    half = ROPE_DIM // 2
    lane = lax.broadcasted_iota(jnp.int32, v.shape, 1) % HEAD_DIM
    second = jnp.where(lane < ROPE_DIM, pltpu.roll(v, half, axis=1), 0.0)
    return jnp.where(lane < half, pltpu.roll(v, LANES - half, axis=1), second)


def _rope(v, cos, sin):
    tiles = []
    for t in range(v.shape[1] // LANES):
        vt = v[:, t * LANES:(t + 1) * LANES]
        tiles.append(vt * cos + _swap_rot_halves(vt) * sin)
    return tiles[0] if len(tiles) == 1 else jnp.concatenate(tiles, axis=1)


def _unrope(v, cos, sin):
    tiles = []
    for t in range(v.shape[1] // LANES):
        vt = v[:, t * LANES:(t + 1) * LANES]
        tiles.append(vt * cos + _swap_rot_halves(vt * sin))
    return tiles[0] if len(tiles) == 1 else jnp.concatenate(tiles, axis=1)


HEADS_PER_STEP = 8
QW = HEADS_PER_STEP * HEAD_DIM
N_PAIRS = N_Q_HEADS // HEADS_PER_STEP
Q_PER_KV = N_Q_HEADS // N_KV_HEADS
KV_PER_STEP = HEADS_PER_STEP // Q_PER_KV


QT_COLS = Q_PER_KV * ATT_BLK


def _attn_scratch(s, with_vt):
    nb = s // ATT_BLK
    pad = s + ATT_BLK
    shapes = [pltpu.VMEM((nb, LANES, QT_COLS), BF16),
              pltpu.VMEM((KV_PER_STEP, pad, LANES), BF16),
              pltpu.VMEM((KV_PER_STEP, pad, LANES), BF16)]
    if with_vt:
        shapes.append(pltpu.VMEM((LANES, pad), BF16))
    return shapes


def _attn_specs(s, order):
    def mk(width, base, **kw):
        if order == "bp":
            return pl.BlockSpec((None, s, width), lambda b, p: (b, 0, base + p), **kw)
        return pl.BlockSpec((None, s, width), lambda p, b: (b, 0, base + p), **kw)
    one = dict(pipeline_mode=pl.Buffered(1))
    tbl = pl.BlockSpec((s, LANES), lambda *_: (0, 0))
    return dict(q=mk(QW, OFF_Q // QW), k=mk(LANES, OFF_K // LANES), v=mk(LANES, OFF_V // LANES),
                g=mk(QW, OFF_ATTN_G // QW), act=mk(QW, 0), kv=mk(LANES, 0), tbl=tbl,
                q1=mk(QW, OFF_Q // QW, **one), g1=mk(QW, OFF_ATTN_G // QW, **one), act1=mk(QW, 0, **one),
                smem=pl.BlockSpec(memory_space=pltpu.SMEM))


def _to_qt(blk):
    rows = []
    for j in range(KV_PER_STEP):
        cols = []
        for tt in range(2):
            t = 2 * j + tt
            tr = blk[:, t * LANES:(t + 1) * LANES].T
            cols += [tr[0:HEAD_DIM, :], tr[HEAD_DIM:, :]]
        rows.append(jnp.concatenate(cols, axis=1))
    return jnp.concatenate(rows, axis=0)


def _from_qt(xt):
    tiles = []
    for j in range(KV_PER_STEP):
        for tt in range(2):
            g0 = 2 * tt
            pair = jnp.concatenate([xt[j * HEAD_DIM:(j + 1) * HEAD_DIM, (g0 + i) * ATT_BLK:(g0 + i + 1) * ATT_BLK]
                                    for i in range(2)], axis=0)
            tiles.append(pair.T)
    return jnp.concatenate(tiles, axis=1)


def _attn_prep(q_ref, k_ref, v_ref, cos_ref, sin_ref, qt_sc, km_sc, vm_sc, t_sc, transposed, nb):
    zeros = jnp.zeros((ATT_BLK, LANES), BF16)
    for j in range(KV_PER_STEP):
        km_sc[j, 0:ATT_BLK, :] = zeros
        vm_sc[j, 0:ATT_BLK, :] = zeros
    t_sc[:, 0:ATT_BLK] = zeros
    head_of_lane = lax.broadcasted_iota(jnp.int32, (ATT_BLK, LANES), 1) // HEAD_DIM

    def prep(n, carry):
        r0 = pl.multiple_of(n * ATT_BLK, ATT_BLK)
        cs = cos_ref[pl.ds(r0, ATT_BLK), :]
        sn = sin_ref[pl.ds(r0, ATT_BLK), :]
        qt_sc[n] = _to_qt(_rope(q_ref[pl.ds(r0, ATT_BLK), :], cs, sn) * ATTN_SCALE).astype(BF16)
        k = _rope(k_ref[pl.ds(r0, ATT_BLK), :], cs, sn)
        v = v_ref[pl.ds(r0, ATT_BLK), :]
        for j in range(KV_PER_STEP):
            km_sc[j, pl.ds(r0 + ATT_BLK, ATT_BLK), :] = jnp.where(head_of_lane == j, k, 0.0).astype(BF16)
            vm_sc[j, pl.ds(r0 + ATT_BLK, ATT_BLK), :] = jnp.where(head_of_lane == j, v, 0.0).astype(BF16)
        t_sc[:, pl.ds(r0 + ATT_BLK, ATT_BLK)] = (k if transposed == "k" else v).T.astype(BF16)
        return carry

    lax.fori_loop(0, nb, prep, 0)


def _band_mask_t(n):
    shape = (2 * ATT_BLK, QT_COLS)
    key = lax.broadcasted_iota(jnp.int32, shape, 0)
    qry = lax.broadcasted_iota(jnp.int32, shape, 1) % ATT_BLK
    lo = jnp.where(n == 0, ATT_BLK, 0)
    return (key > qry) & (key <= qry + WINDOW) & (key >= lo)


def _sink_row(sink_ref, first):
    return jnp.concatenate([jnp.full((1, ATT_BLK), sink_ref[first + g], F32) for g in range(Q_PER_KV)], axis=1)


def _softmax_cols(scores_t, valid, sink):
    sc = jnp.where(valid, scores_t, NEG_BIG)
    m = jnp.maximum(jnp.max(sc, axis=0, keepdims=True), sink)
    e = jnp.exp(sc - m)
    es = jnp.exp(sink - m)
    inv = 1.0 / (jnp.sum(e, axis=0, keepdims=True) + es)
    return e * inv, es * inv


def _attn_fwd(proj3, sinks, cosf, sinf):
    bsz, s, _ = proj3.shape
    nb = s // ATT_BLK
    sp = _attn_specs(s, "bp")

    def body(sink_ref, q_ref, k_ref, v_ref, g_ref, cos_ref, sin_ref, o_ref, y_ref, qt_sc, km_sc, vm_sc, vt_sc):
        p = pl.program_id(1)
        _attn_prep(q_ref, k_ref, v_ref, cos_ref, sin_ref, qt_sc, km_sc, vm_sc, vt_sc, "v", nb)
        kv_row = lax.broadcasted_iota(jnp.int32, (LANES, QT_COLS), 0) // HEAD_DIM

        def blk(n, carry):
            r0 = pl.multiple_of(n * ATT_BLK, ATT_BLK)
            valid = _band_mask_t(n)
            rq = qt_sc[n]
            vt = vt_sc[:, pl.ds(r0, 2 * ATT_BLK)]
            ots = []
            for j in range(KV_PER_STEP):
                st = _dot(km_sc[j, pl.ds(r0, 2 * ATT_BLK), :], rq)
                pt, _ = _softmax_cols(st, valid, _sink_row(sink_ref, p * HEADS_PER_STEP + j * Q_PER_KV))
                ots.append(_dot(vt, pt.astype(BF16)))
            o = _from_qt(jnp.where(kv_row == 0, ots[0], ots[1]))
            o_ref[pl.ds(r0, ATT_BLK), :] = o
            g = g_ref[pl.ds(r0, ATT_BLK), :]
            y_ref[pl.ds(r0, ATT_BLK), :] = (o * (g * _sigmoid(g))).astype(BF16)
            return carry

        lax.fori_loop(0, nb, blk, 0)

    return pl.pallas_call(
        body, name="attn_fwd", grid=(bsz, N_PAIRS),
        out_shape=(jax.ShapeDtypeStruct((bsz, s, D_ATTN), F32), jax.ShapeDtypeStruct((bsz, s, D_ATTN), BF16)),
        in_specs=[sp["smem"], sp["q"], sp["k"], sp["v"], sp["g"], sp["tbl"], sp["tbl"]],
        out_specs=(sp["act"], sp["act"]),
        scratch_shapes=_attn_scratch(s, True),
        compiler_params=_params())(sinks, proj3, proj3, proj3, proj3, cosf, sinf)


def _attn_bwd(proj3, o3, dy3, sinks, cosf, sinf):
    bsz, s, _ = proj3.shape
    nb = s // ATT_BLK
    sp = _attn_specs(s, "pb")

    def body(sink_ref, q_ref, k_ref, v_ref, g_ref, o_ref, dy_ref, cos_ref, sin_ref,
             dq_ref, dk_ref, dv_ref, dg_ref, ds_ref, qt_sc, km_sc, vm_sc, kt_sc, dot_sc, dqt_sc, dk_sc, dv_sc):
        p = pl.program_id(0)
        b = pl.program_id(1)
        _attn_prep(q_ref, k_ref, v_ref, cos_ref, sin_ref, qt_sc, km_sc, vm_sc, kt_sc, "k", nb)
        dk_sc[...] = jnp.zeros_like(dk_sc)
        dv_sc[...] = jnp.zeros_like(dv_sc)

        def gate(n, carry):
            r0 = pl.multiple_of(n * ATT_BLK, ATT_BLK)
            g = g_ref[pl.ds(r0, ATT_BLK), :]
            dy = dy_ref[pl.ds(r0, ATT_BLK), :]
            sg = _sigmoid(g)
            dg_ref[pl.ds(r0, ATT_BLK), :] = (dy * o_ref[pl.ds(r0, ATT_BLK), :] * (sg * (1.0 + g * (1.0 - sg)))).astype(BF16)
            dot_sc[n] = _to_qt(dy * (g * sg)).astype(BF16)
            return carry

        lax.fori_loop(0, nb, gate, 0)
        kv_lane = lax.broadcasted_iota(jnp.int32, (2 * ATT_BLK, LANES), 1) // HEAD_DIM
        kv_row = lax.broadcasted_iota(jnp.int32, (LANES, QT_COLS), 0) // HEAD_DIM

        def blk(n, acc):
            r0 = pl.multiple_of(n * ATT_BLK, ATT_BLK)
            valid = _band_mask_t(n)
            rq = qt_sc[n]
            rd = dot_sc[n]
            kt = kt_sc[:, pl.ds(r0, 2 * ATT_BLK)]
            dvs, dks, dqs, new_acc = [], [], [], []
            for j in range(KV_PER_STEP):
                st = _dot(km_sc[j, pl.ds(r0, 2 * ATT_BLK), :], rq)
                pt, ps = _softmax_cols(st, valid, _sink_row(sink_ref, p * HEADS_PER_STEP + j * Q_PER_KV))
                dpt = _dot(vm_sc[j, pl.ds(r0, 2 * ATT_BLK), :], rd)
                delta = jnp.sum(pt * dpt, axis=0, keepdims=True)
                dst = (pt * (dpt - delta)).astype(BF16)
                new_acc.append(acc[j] + ps * delta)
                dvs.append(_dot_nt(pt.astype(BF16), rd))
                dks.append(_dot_nt(dst, rq))
                dqs.append(_dot(kt, dst))
            dv_sc[pl.ds(r0, 2 * ATT_BLK), :] += jnp.where(kv_lane == 0, dvs[0], dvs[1])
            dk_sc[pl.ds(r0, 2 * ATT_BLK), :] += jnp.where(kv_lane == 0, dks[0], dks[1])
            dqt_sc[n] = jnp.where(kv_row == 0, dqs[0], dqs[1]) * ATTN_SCALE
            return tuple(new_acc)

        acc = lax.fori_loop(0, nb, blk, tuple(jnp.zeros((1, QT_COLS), F32) for _ in range(KV_PER_STEP)))
        lane1 = lax.broadcasted_iota(jnp.int32, (1, LANES), 1)
        dsink = jnp.zeros((1, LANES), F32)
        for j in range(KV_PER_STEP):
            for i in range(Q_PER_KV):
                part = jnp.sum(acc[j][:, i * ATT_BLK:(i + 1) * ATT_BLK], axis=1, keepdims=True)
                dsink = dsink - jnp.where(lane1 == j * Q_PER_KV + i, part, 0.0)

        @pl.when(b == 0)
        def _():
            ds_ref[...] = jnp.zeros_like(ds_ref)

        ds_ref[...] += dsink

        def post(n, carry):
            r0 = pl.multiple_of(n * ATT_BLK, ATT_BLK)
            cs = cos_ref[pl.ds(r0, ATT_BLK), :]
            sn = sin_ref[pl.ds(r0, ATT_BLK), :]
            dq_ref[pl.ds(r0, ATT_BLK), :] = _unrope(_from_qt(dqt_sc[n]), cs, sn).astype(BF16)
            dk_ref[pl.ds(r0, ATT_BLK), :] = _unrope(dk_sc[pl.ds(r0 + ATT_BLK, ATT_BLK), :], cs, sn).astype(BF16)
            dv_ref[pl.ds(r0, ATT_BLK), :] = dv_sc[pl.ds(r0 + ATT_BLK, ATT_BLK), :].astype(BF16)
            return carry

        lax.fori_loop(0, nb, post, 0)

    act = jax.ShapeDtypeStruct((bsz, s, D_ATTN), BF16)
    kvs = jax.ShapeDtypeStruct((bsz, s, D_KV), BF16)
    return pl.pallas_call(
        body, name="attn_bwd", grid=(N_PAIRS, bsz),
        out_shape=(act, kvs, kvs, act, jax.ShapeDtypeStruct((N_PAIRS, 1, LANES), F32)),
        in_specs=[sp["smem"], sp["q1"], sp["k"], sp["v"], sp["g1"], sp["act1"], sp["act1"], sp["tbl"], sp["tbl"]],
        out_specs=(sp["act"], sp["kv"], sp["kv"], sp["act"], pl.BlockSpec((None, 1, LANES), lambda p, b: (p, 0, 0))),
        scratch_shapes=_attn_scratch(s, True) + [pltpu.VMEM((nb, LANES, QT_COLS), BF16),
                                                 pltpu.VMEM((nb, LANES, QT_COLS), F32),
                                                 pltpu.VMEM((s + ATT_BLK, LANES), F32),
                                                 pltpu.VMEM((s + ATT_BLK, LANES), F32)],
        compiler_params=_params())(sinks, proj3, proj3, proj3, proj3, o3, dy3, cosf, sinf)


def _merge_fwd_bwd(x, tgt, y_rnn, y_attn, proj, w_r, w_a, w_o, gf):
    t, d = x.shape
    tm = min(t, 256)

    hw = d // 2

    def body(x_ref, t_ref, yr_ref, ya_ref, mr0_ref, mr1_ref, ma0_ref, ma1_ref, wr_ref, wa_ref, wo_ref, gf_ref,
             dmg_ref, dyr_ref, dya_ref, mg_ref, dx2_ref, dx2b_ref, dpr_ref, dpa_ref, loss_ref, dgf_ref):
        i = pl.program_id(0)
        wr = wr_ref[...]
        wa = wa_ref[...]
        wo = wo_ref[...]
        gfv = gf_ref[...]
        pr = _dot(yr_ref[...], wr)
        pa = _dot(ya_ref[...], wa)
        sr = _sigmoid(jnp.concatenate([mr0_ref[...], mr1_ref[...]], axis=1))
        sa = _sigmoid(jnp.concatenate([ma0_ref[...], ma1_ref[...]], axis=1))
        mb = (sr * pr + sa * pa).astype(BF16)
        mg_ref[...] = mb
        x2 = x_ref[...] + _dot(mb, wo)
        r2 = lax.rsqrt(jnp.mean(x2 * x2, axis=-1, keepdims=True) + NORM_EPS)
        nrm = x2 * r2
        err = nrm * gfv - t_ref[...]
        dy = err * (1.0 / d)
        dn = dy * gfv
        dx2 = r2 * (dn - nrm * jnp.mean(dn * nrm, axis=-1, keepdims=True))
        dx2_ref[...] = dx2
        dx2b = dx2.astype(BF16)
        dx2b_ref[...] = dx2b
        dmerged = _dot_nt(dx2b, wo)
        dpr = (dmerged * sr).astype(BF16)
        dpa = (dmerged * sa).astype(BF16)
        dpr_ref[...] = dpr
        dpa_ref[...] = dpa
        dmg_ref[:, 0:d] = (dmerged * pr * (sr * (1.0 - sr))).astype(BF16)
        dmg_ref[:, d:2 * d] = (dmerged * pa * (sa * (1.0 - sa))).astype(BF16)
        dyr_ref[...] = _dot_nt(dpr, wr)
        dya_ref[...] = _dot_nt(dpa, wa)

        @pl.when(i == 0)
        def _():
            loss_ref[...] = jnp.zeros_like(loss_ref)
            dgf_ref[...] = jnp.zeros_like(dgf_ref)

        loss_ref[...] += jnp.full((1, LANES), 0.5 / d, F32) * jnp.sum(err * err)
        dgf_ref[...] += jnp.sum(dy * nrm, axis=0, keepdims=True)

    tile = pl.BlockSpec((tm, d), lambda i: (i, 0))
    wsp = pl.BlockSpec((d, d), lambda i: (0, 0))

    def gate(col_blk):
        return pl.BlockSpec((tm, hw), lambda i: (i, col_blk))

    fb = jax.ShapeDtypeStruct((t, d), BF16)
    ff = jax.ShapeDtypeStruct((t, d), F32)
    return pl.pallas_call(
        body, name="merge_fwd_bwd", grid=(t // tm,),
        out_shape=(jax.ShapeDtypeStruct((t, 2 * d), BF16), ff, ff, fb, ff, fb, fb, fb,
                   jax.ShapeDtypeStruct((1, LANES), F32), jax.ShapeDtypeStruct((1, d), F32)),
        in_specs=[tile, tile, tile, tile] + [gate(OFF_MERGE_R // hw + j) for j in range(4)] + [
            wsp, wsp, wsp, pl.BlockSpec((1, d), lambda i: (0, 0))],
        out_specs=(pl.BlockSpec((tm, 2 * d), lambda i: (i, 0)), tile, tile, tile, tile, tile, tile, tile,
                   pl.BlockSpec((1, LANES), lambda i: (0, 0)), pl.BlockSpec((1, d), lambda i: (0, 0))),
        compiler_params=_params())(x, tgt, y_rnn, y_attn, proj, proj, proj, proj, w_r, w_a, w_o, gf)


def _local_grads(x, tgt, norm_g, w_in_bm, conv_w, conv_b, lru_w_a, lru_b_a, lru_w_x, lru_b_x, lam, sinks,
                 w_r, w_a, w_o, gf):
    bsz, s, d = x.shape
    t = bsz * s
    x2 = x.reshape(t, d)
    h = _rmsnorm_fwd(x2, norm_g)
    proj = _in_proj(h, w_in_bm)
    proj3 = proj.reshape(bsz, s, D_IN)
    h_lru, y_rnn = _lru_fwd(proj3, conv_w, conv_b, lru_w_a, lru_b_a, lru_w_x, lru_b_x, lam)
    cosf, sinf = _rope_tables(s)
    o_attn, y_attn = _attn_fwd(proj3, sinks, cosf, sinf)
    y_rnn2 = y_rnn.reshape(t, d)
    y_attn2 = y_attn.reshape(t, d)
    dmg, dyr, dya, merged, dx2, dx2b, dpr, dpa, loss, dgf = _merge_fwd_bwd(
        x2, tgt.reshape(t, d), y_rnn2, y_attn2, proj, w_r, w_a, w_o, gf)
    dup, dgr, dcw, dcb, dwa, dba, dwx, dbx, dlam = _lru_bwd(
        proj3, h_lru, dyr.reshape(bsz, s, d), conv_w, conv_b, lru_w_a, lru_b_a, lru_w_x, lru_b_x, lam)
    dq, dk, dv, dga, dsink = _attn_bwd(proj3, o_attn, dya.reshape(bsz, s, d), sinks, cosf, sinf)
    dproj = jnp.concatenate([dup, dgr, dq, dk, dv, dga, dmg.reshape(bsz, s, 2 * d)], axis=-1).reshape(t, D_IN)
    dh = _grad_h(dproj, w_in_bm)
    grad_x, dng = _rmsnorm_bwd(x2, dh, dx2, norm_g)
    small = dict(norm_g=dng, conv_w=dcw, conv_b=dcb, lru_w_a=dwa, lru_b_a=dba, lru_w_x=dwx, lru_b_x=dbx,
                 lru_lambda=dlam, attn_sinks=dsink[:, 0, :HEADS_PER_STEP].reshape(1, N_Q_HEADS), final_norm_g=dgf)
    squares = [(y_rnn2, dpr), (y_attn2, dpa), (merged, dx2b)]
    return loss[0, 0], grad_x.reshape(bsz, s, d), h, dproj, squares, small


ANY = pl.BlockSpec(memory_space=pl.ANY)


def _mesh_pos():
    return lax.axis_index("x"), lax.axis_index("y"), lax.axis_index("c")


def _chip_peers(x, y, c):
    return [((1 - x, y, c), 2 * (1 - x) + y), ((x, 1 - y, c), 2 * x + (1 - y)),
            ((1 - x, 1 - y, c), 2 * (1 - x) + (1 - y))]


def _remote(src, dst, send_sems, recv_sems, idx, peer):
    return pltpu.make_async_remote_copy(src_ref=src, dst_ref=dst, send_sem=send_sems.at[idx],
                                        recv_sem=recv_sems.at[idx], device_id=peer, device_id_type=MESH)


def _all_gather(bufs, split):
    n = len(bufs)
    n_fwd = 3 * sum(split)

    def body(*refs):
        ins, outs = refs[:n], refs[n:2 * n]
        send_sems, recv_sems, fsend_sems, frecv_sems = refs[2 * n:]
        x, y, c = _mesh_pos()
        me = 2 * x + y
        sib = (x, y, 1 - c)
        peers = _chip_peers(x, y, c)

        def part(ref, slot, t, half):
            if not split[t]:
                return ref.at[slot]
            hr = bufs[t].shape[1] // 2
            return ref.at[slot, pl.ds(pl.multiple_of(half * hr, 8), hr), :]

        sends, recvs = [], []
        for t in range(n):
            for k, (peer, pj) in enumerate(peers):
                src = part(ins[t], me, t, c)
                sends.append(_remote(src, part(outs[t], me, t, c), send_sems, recv_sems, 3 * t + k, peer))
                recvs.append(_remote(src, part(outs[t], pj, t, c), send_sems, recv_sems, 3 * t + k, peer))
        for cp in sends:
            cp.start()
        fwd, fwd_recv = [], []
        for t in range(n):
            for k, (peer, pj) in enumerate(peers):
                recvs[3 * t + k].wait_recv()
                if split[t]:
                    got = part(outs[t], pj, t, c)
                    f = len(fwd)
                    fwd.append(_remote(got, got, fsend_sems, frecv_sems, f, sib))
                    fwd_recv.append(_remote(got, part(outs[t], pj, t, 1 - c), fsend_sems, frecv_sems, f, sib))
                    fwd[-1].start()
        for cp in sends:
            cp.wait_send()
        for snd, rcv in zip(fwd, fwd_recv):
            snd.wait_send()
            rcv.wait_recv()

    return pl.pallas_call(
        body, name="all_gather_weights", out_shape=[jax.ShapeDtypeStruct(a.shape, a.dtype) for a in bufs],
        in_specs=[ANY] * n, out_specs=[ANY] * n, input_output_aliases={t: t for t in range(n)},
        scratch_shapes=[pltpu.SemaphoreType.DMA((3 * n,)), pltpu.SemaphoreType.DMA((3 * n,)),
                        pltpu.SemaphoreType.DMA((n_fwd,)), pltpu.SemaphoreType.DMA((n_fwd,))],
        compiler_params=_params())(*bufs)


def _row_tile(rows, row_bytes, cap_bytes=2 * 1024 * 1024):
    best = None
    for tr in range(8, rows + 1, 8):
        if rows % tr == 0 and tr * row_bytes <= cap_bytes:
            best = tr
    return best if best is not None else rows


XOR_ORDER = (3, 2, 1)


def _grads_reduce_scatter(h, dproj, squares, small, idx):
    t, d = h.shape
    nsq = len(squares)
    hr = d // 2
    qr = ROW_BLK // 2
    tk = min(t, 1024)
    nk = t // tk
    last = N_CHIPS - 1
    n_phase = 3

    def dest(s, idx_ref):
        xo = jnp.where(s == 0, XOR_ORDER[0], jnp.where(s == 1, XOR_ORDER[1], jnp.where(s == 2, XOR_ORDER[2], 0)))
        return idx_ref[0] ^ xo

    def k_sq(p, k):
        return jnp.where(p == 0, k, nk - 1)

    def k_w(p, k):
        return jnp.where(p == 0, 0, k)

    in_specs = [
        pl.BlockSpec((tk, hr), lambda s, p, k, idx_ref: (k_w(p, k), (1 - idx_ref[1] + jnp.maximum(p - 1, 0)) % 2)),
        pl.BlockSpec((tk, W_BLK), lambda s, p, k, idx_ref: (k_w(p, k), dest(s, idx_ref)))]
    for q in range(nsq):
        in_specs.append(pl.BlockSpec((tk, ROW_BLK), lambda s, p, k, idx_ref: (k_sq(p, k), dest(s, idx_ref))))
        in_specs.append(pl.BlockSpec((tk, d), lambda s, p, k, idx_ref: (k_sq(p, k), 0)))

    def body(idx_ref, *refs):
        nj = 1 + nsq
        h_ref, dp_ref = refs[0], refs[1]
        sq_in = refs[2:2 + 2 * nsq]
        small_in = refs[2 * nj]
        outs = refs[2 * nj + 1:3 * nj + 2]
        landing = refs[3 * nj + 2:4 * nj + 3]
        sc = refs[4 * nj + 3:]
        acc_w, xr_w, sb_w = sc[0:3]
        sq_sc = [sc[3 + 3 * q:6 + 3 * q] for q in range(nsq)]
        sm, smx = sc[3 * nj:3 * nj + 2]
        x_send, x_recv, i_send, i_recv, f_send, f_recv, o_sem, l_sem = sc[3 * nj + 2:]
        s, p, k = pl.program_id(0), pl.program_id(1), pl.program_id(2)
        x, y, c = _mesh_pos()
        sib = (x, y, 1 - c)
        peers = [((1 - x) if xo & 2 else x, (1 - y) if xo & 1 else y, c) for xo in XOR_ORDER]
        slot = s % 2
        mine_w = pl.ds(pl.multiple_of(c * hr, 8), hr)
        theirs_w = pl.ds(pl.multiple_of((1 - c) * hr, 8), hr)
        mine_q = pl.ds(pl.multiple_of(c * qr, 8), qr)
        theirs_q = pl.ds(pl.multiple_of((1 - c) * qr, 8), qr)

        def exch(j, src, dst):
            return _remote(src, dst, x_send, x_recv, 2 * j + slot, sib)

        sbufs = [sb_w] + [sq_sc[q][2] for q in range(nsq)]

        def ici(j, ss):
            return _remote(sbufs[j].at[ss], landing[j].at[ss], i_send, i_recv, last * j + ss, peers[ss])

        def exchanges():
            cps = [exch(0, acc_w.at[0], xr_w.at[slot])]
            cps += [exch(1 + q, sq_sc[q][0].at[theirs_q, :], sq_sc[q][1].at[slot]) for q in range(nsq)]
            return cps

        def small_send(ss):
            return _remote(sm.at[c], landing[nj].at[ss], i_send, i_recv, last * nj + ss, peers[ss])

        def small_start():
            load = pltpu.make_async_copy(small_in, sm, l_sem.at[nj + 1])
            load.start()
            load.wait()
            swap = _remote(sm, smx.at[pl.ds(0, 2)], x_send, x_recv, 2 * nj, sib)
            swap.start()
            swap.wait_recv()
            swap.wait_send()
            sm[...] = sm[...] + smx[0:2]
            for ss in range(last):
                small_send(ss).start()

        def pair_ref(j):
            return acc_w.at[1] if j == 0 else sq_sc[j - 1][0].at[mine_q, :]

        def sq_phase():
            pl.when((s == 0) & (k == 0))(small_start)
            for q in range(nsq):
                acc = sq_sc[q][0]

                @pl.when(k == 0)
                def _():
                    acc[...] = jnp.zeros((ROW_BLK, d), F32)

                acc[...] += _dot_tn(sq_in[2 * q][...], sq_in[2 * q + 1][...])

            @pl.when(k == nk - 1)
            def _():
                for cp in exchanges()[1:]:
                    cp.start()

        def w_phase(hf):
            @pl.when(k == 0)
            def _():
                acc_w[hf] = jnp.zeros((hr, W_BLK), F32)

            acc_w[hf] += _dot_tn(h_ref[...], dp_ref[...])

            @pl.when(k == nk - 1)
            def _():
                if hf == 0:
                    exchanges()[0].start()
                else:
                    finish_step()

        def finish_step():
            for cp in exchanges():
                cp.wait_recv()
                cp.wait_send()
            acc_w[1] += xr_w[slot]
            for q in range(nsq):
                sq_sc[q][0][mine_q, :] += sq_sc[q][1][slot]
            for ss in range(last):
                @pl.when(s == ss)
                def _():
                    for j in range(nj):
                        sbufs[j][ss] = pair_ref(j)[...].astype(BF16)
                        ici(j, ss).start()

            @pl.when(s == last)
            def _():
                for ss in range(last):
                    for j in range(nj):
                        ici(j, ss).wait_recv()
                        ici(j, ss).wait_send()
                    small_send(ss).wait_recv()
                    small_send(ss).wait_send()
                stage = [pltpu.make_async_copy(landing[j], sbufs[j], l_sem.at[j]) for j in range(nj)]
                stage.append(pltpu.make_async_copy(landing[nj], smx, l_sem.at[nj]))
                for cp in stage:
                    cp.start()
                for j in range(nj):
                    stage[j].wait()
                    total = pair_ref(j)[...]
                    for ss in range(last):
                        total = total + sbufs[j][ss].astype(F32)
                    pair_ref(j)[...] = total
                stage[nj].wait()
                by_xor = {xo: smx[ss] for ss, xo in enumerate(XOR_ORDER)}
                sm[c] = (sm[c] + by_xor[1]) + (by_xor[2] + by_xor[3])
                done = [(acc_w.at[1], outs[0].at[mine_w, :], outs[0].at[theirs_w, :])]
                done += [(pair_ref(1 + q), outs[1 + q].at[mine_q, :], outs[1 + q].at[theirs_q, :]) for q in range(nsq)]
                done.append((sm.at[c], outs[nj].at[c], outs[nj].at[1 - c]))
                copies = []
                for j, (src, mine, theirs) in enumerate(done):
                    keep = pltpu.make_async_copy(src, mine, o_sem.at[j])
                    give = _remote(src, mine, f_send, f_recv, j, sib)
                    take = _remote(src, theirs, f_send, f_recv, j, sib)
                    keep.start()
                    give.start()
                    copies.append((keep, give, take))
                for keep, give, take in copies:
                    keep.wait()
                    give.wait_send()
                    take.wait_recv()

        pl.when(p == 0)(sq_phase)
        for hf in range(2):
            pl.when(p == 1 + hf)(functools.partial(w_phase, hf))

    nj = 1 + nsq
    scratch = [pltpu.VMEM((2, hr, W_BLK), F32), pltpu.VMEM((2, hr, W_BLK), F32), pltpu.VMEM((last, hr, W_BLK), BF16)]
    for _ in range(nsq):
        scratch += [pltpu.VMEM((ROW_BLK, d), F32), pltpu.VMEM((2, qr, d), F32), pltpu.VMEM((last, qr, d), BF16)]
    scratch += [pltpu.VMEM((2, PK_HALF, LANES), F32), pltpu.VMEM((last, PK_HALF, LANES), F32)]
    scratch += [pltpu.SemaphoreType.DMA((2 * nj + 1,)), pltpu.SemaphoreType.DMA((2 * nj + 1,)),
                pltpu.SemaphoreType.DMA((last * (nj + 1),)), pltpu.SemaphoreType.DMA((last * (nj + 1),)),
                pltpu.SemaphoreType.DMA((nj + 1,)), pltpu.SemaphoreType.DMA((nj + 1,)),
                pltpu.SemaphoreType.DMA((nj + 1,)), pltpu.SemaphoreType.DMA((nj + 2,))]
    grid_spec = pltpu.PrefetchScalarGridSpec(
        num_scalar_prefetch=1, grid=(N_CHIPS, n_phase, nk), in_specs=in_specs + [ANY],
        out_specs=[ANY] * (2 * nj + 2), scratch_shapes=scratch)
    out_shape = [jax.ShapeDtypeStruct((d, W_BLK), F32)] + [jax.ShapeDtypeStruct((ROW_BLK, d), F32)] * nsq
    out_shape.append(jax.ShapeDtypeStruct((2, PK_HALF, LANES), F32))
    out_shape += [jax.ShapeDtypeStruct((last, hr, W_BLK), BF16)] + [jax.ShapeDtypeStruct((last, qr, d), BF16)] * nsq
    out_shape.append(jax.ShapeDtypeStruct((last, PK_HALF, LANES), F32))
    flat = [a for pair in squares for a in pair]
    res = pl.pallas_call(body, name="grads_reduce_scatter", grid_spec=grid_spec, out_shape=out_shape,
                         compiler_params=_params())(idx, h, dproj, *flat, small)
    return res[:nj + 1]


_VEC_NAMES = ("norm_g", "conv_b", "lru_b_a", "lru_b_x", "lru_lambda", "final_norm_g")


def _pack_small(p, conv_full=None, scalar=None):
    rows = [p["lru_w_a"].reshape(PK_WX - PK_WA, LANES), p["lru_w_x"].reshape(PK_VEC - PK_WX, LANES)]
    rows += [p[k].reshape(8, LANES) for k in _VEC_NAMES]
    rows.append(jnp.pad(p["attn_sinks"].reshape(1, N_Q_HEADS), ((0, 7), (0, LANES - N_Q_HEADS))))
    rows.append(jnp.zeros((32, LANES), F32) if conv_full is None else conv_full.reshape(32, LANES))
    tail = PK_ROWS - PK_SCALAR
    if scalar is None:
        rows.append(jnp.zeros((tail, LANES), F32))
    else:
        rows.append(jnp.pad(scalar.reshape(1, 1), ((0, tail - 1), (0, LANES - 1))))
    return jnp.concatenate(rows, axis=0)


def _unpack_small(pk, like):
    out = {"lru_w_a": pk[PK_WA:PK_WX].reshape(like["lru_w_a"].shape),
           "lru_w_x": pk[PK_WX:PK_VEC].reshape(like["lru_w_x"].shape)}
    for j, k in enumerate(_VEC_NAMES):
        out[k] = pk[PK_VEC + 8 * j:PK_VEC + 8 * j + 8].reshape(like[k].shape)
    out["attn_sinks"] = pk[PK_SINK:PK_SINK + 1, :N_Q_HEADS].reshape(like["attn_sinks"].shape)
    return out


_WEIGHTS = ("norm_g", "w_in", "conv_w", "conv_b", "lru_w_a", "lru_b_a", "lru_w_x", "lru_b_x", "lru_lambda",
            "attn_sinks", "w_rnn_out", "w_attn_out", "w_o", "final_norm_g")
_SMALL = ("norm_g", "conv_b", "lru_w_a", "lru_b_a", "lru_w_x", "lru_b_x", "lru_lambda", "attn_sinks", "final_norm_g")
_ROW_SHARDED = ("w_rnn_out", "w_attn_out", "w_o")


def kernel(x, norm_g, w_in, conv_w, conv_b, lru_w_a, lru_b_a, lru_w_x, lru_b_x, lru_lambda, attn_sinks, w_rnn_out, w_attn_out, w_o, final_norm_g, loss_target, m_norm_g, m_w_in, m_conv_w, m_conv_b, m_lru_w_a, m_lru_b_a, m_lru_w_x, m_lru_b_x, m_lru_lambda, m_attn_sinks, m_w_rnn_out, m_w_attn_out, m_w_o, m_final_norm_g, v_norm_g, v_w_in, v_conv_w, v_conv_b, v_lru_w_a, v_lru_b_a, v_lru_w_x, v_lru_b_x, v_lru_lambda, v_attn_sinks, v_w_rnn_out, v_w_attn_out, v_w_o, v_final_norm_g):
    w = dict(norm_g=norm_g, w_in=w_in, conv_w=conv_w, conv_b=conv_b, lru_w_a=lru_w_a, lru_b_a=lru_b_a, lru_w_x=lru_w_x,
             lru_b_x=lru_b_x, lru_lambda=lru_lambda, attn_sinks=attn_sinks, w_rnn_out=w_rnn_out, w_attn_out=w_attn_out,
             w_o=w_o, final_norm_g=final_norm_g)
    m = dict(norm_g=m_norm_g, w_in=m_w_in, conv_w=m_conv_w, conv_b=m_conv_b, lru_w_a=m_lru_w_a, lru_b_a=m_lru_b_a,
             lru_w_x=m_lru_w_x, lru_b_x=m_lru_b_x, lru_lambda=m_lru_lambda, attn_sinks=m_attn_sinks,
             w_rnn_out=m_w_rnn_out, w_attn_out=m_w_attn_out, w_o=m_w_o, final_norm_g=m_final_norm_g)
    v = dict(norm_g=v_norm_g, w_in=v_w_in, conv_w=v_conv_w, conv_b=v_conv_b, lru_w_a=v_lru_w_a, lru_b_a=v_lru_b_a,
             lru_w_x=v_lru_w_x, lru_b_x=v_lru_b_x, lru_lambda=v_lru_lambda, attn_sinks=v_attn_sinks,
             w_rnn_out=v_w_rnn_out, w_attn_out=v_w_attn_out, w_o=v_w_o, final_norm_g=v_final_norm_g)
    mx, my, mc = _mesh_pos()
    me = 2 * mx + my
    d = D_MODEL

    slot0 = jnp.stack([me, jnp.zeros_like(me)]).astype(jnp.int32)
    bufs = [_put_slot(w[k][0], N_CHIPS, slot0, w[k].shape[1], BF16, "cast_" + k) for k in ("w_in",) + _ROW_SHARDED]
    bufs.append(_put_slot(w["conv_w"][0], N_CHIPS, slot0, CONV_WIDTH, F32, "slot_conv_w"))
    g_in, g_r, g_a, g_o, g_cw = _all_gather(bufs, [True, True, True, True, False])
    conv_full = g_cw.transpose(1, 0, 2).reshape(CONV_WIDTH, D_RNN)

    loss_local, grad_x, h, dproj, squares, gsmall = _local_grads(
        x, loss_target, w["norm_g"], g_in, conv_full, w["conv_b"], w["lru_w_a"][0], w["lru_b_a"], w["lru_w_x"][0],
        w["lru_b_x"], w["lru_lambda"], w["attn_sinks"][0], g_r.reshape(d, d), g_a.reshape(d, d), g_o.reshape(d, d),
        w["final_norm_g"].reshape(1, d))
    gpack = _pack_small(gsmall, gsmall["conv_w"], loss_local).reshape(2, PK_HALF, LANES)
    f_in, f_r, f_a, f_o, spack = _grads_reduce_scatter(h, dproj, squares, gpack, jnp.stack([me, mc]).astype(jnp.int32))
    spack = spack.reshape(PK_ROWS, LANES)
    loss = spack[PK_SCALAR, 0]

    grads = _unpack_small(spack, w)
    conv_all = spack[PK_CONV:PK_CONV + 32].reshape(CONV_WIDTH, D_RNN)
    grads["conv_w"] = lax.dynamic_slice_in_dim(conv_all, me * (D_RNN // N_CHIPS), D_RNN // N_CHIPS, axis=1)[None]
    grads["w_in"] = f_in[None]
    grads["w_rnn_out"], grads["w_attn_out"], grads["w_o"] = f_r[None], f_a[None], f_o[None]

    delta, new_m, new_v = {}, {}, {}
    for k in ("w_in",) + _ROW_SHARDED:
        dk, mk, vk = _adamw(w[k][0], grads[k][0], m[k][0], v[k][0], "adamw_" + k)
        delta[k], new_m[k], new_v[k] = dk[None], mk[None], vk[None]
    shp = (2 * CONV_WIDTH, LANES)
    dk, mk, vk = _adamw(w["conv_w"].reshape(shp), grads["conv_w"].reshape(shp), m["conv_w"].reshape(shp),
                        v["conv_w"].reshape(shp), "adamw_conv_w")
    delta["conv_w"], new_m["conv_w"], new_v["conv_w"] = (a.reshape(w["conv_w"].shape) for a in (dk, mk, vk))
    dk, mk, vk = _adamw(_pack_small(w), spack, _pack_small(m), _pack_small(v), "adamw_small")
    for src, dst in ((dk, delta), (mk, new_m), (vk, new_v)):
        dst.update(_unpack_small(src, w))

    return (loss, grad_x, *[grads[k] for k in _WEIGHTS], *[delta[k] for k in _WEIGHTS],
            *[new_m[k] for k in _WEIGHTS], *[new_v[k] for k in _WEIGHTS])
```

```python
import functools
import math

import jax
import jax.numpy as jnp
from jax import lax
from jax.experimental import pallas as pl
from jax.experimental.pallas import tpu as pltpu

F32 = jnp.float32
BF16 = jnp.bfloat16
MESH = pl.DeviceIdType.MESH

D_MODEL = 1024
D_RNN = 1024
N_RNN_BLOCKS = 8
RNN_BLOCK = D_RNN // N_RNN_BLOCKS
CONV_WIDTH = 4
LRU_C = 8.0
HEAD_DIM = 64
N_Q_HEADS = 16
N_KV_HEADS = 4
D_ATTN = N_Q_HEADS * HEAD_DIM
D_KV = N_KV_HEADS * HEAD_DIM
WINDOW = 128
ROPE_DIM = HEAD_DIM // 4
ROPE_THETA = 500000.0
NORM_EPS = 1e-6
OFF_RNN_X = 0
OFF_RNN_G = OFF_RNN_X + D_RNN
OFF_Q = OFF_RNN_G + D_RNN
OFF_K = OFF_Q + D_ATTN
OFF_V = OFF_K + D_KV
OFF_ATTN_G = OFF_V + D_KV
OFF_MERGE_R = OFF_ATTN_G + D_ATTN
OFF_MERGE_A = OFF_MERGE_R + D_MODEL
D_IN = OFF_MERGE_A + D_MODEL

ADAM_LR = 0.001
ADAM_B1 = 0.9
ADAM_B2 = 0.999
ADAM_EPS = 1e-08
ADAM_WD = 0.01
ADAM_STEP = 10

N_CHIPS = 4
W_BLK = D_IN // N_CHIPS
ROW_BLK = D_MODEL // N_CHIPS
LANES = 128
ATT_BLK = 128
VMEM_LIMIT = 56 * 1024 * 1024
NEG_BIG = -1e30
ATTN_SCALE = 1.0 / math.sqrt(HEAD_DIM)

PK_WA = 0
PK_WX = PK_WA + N_RNN_BLOCKS * RNN_BLOCK
PK_VEC = PK_WX + N_RNN_BLOCKS * RNN_BLOCK
PK_SINK = PK_VEC + 6 * 8
PK_CONV = PK_SINK + 8
PK_SCALAR = PK_CONV + 32
PK_ROWS = PK_SCALAR + 8
PK_HALF = PK_ROWS // 2


def _params(**kw):
    return pltpu.CompilerParams(vmem_limit_bytes=VMEM_LIMIT, **kw)


def _sigmoid(z):
    return 1.0 / (1.0 + jnp.exp(-z))


def _dot(a, b):
    return jnp.dot(a, b, preferred_element_type=F32)


def _dot_nt(a, b):
    return lax.dot_general(a, b, (((1,), (1,)), ((), ())), preferred_element_type=F32)


def _dot_tn(a, b):
    return lax.dot_general(a, b, (((0,), (0,)), ((), ())), preferred_element_type=F32)


def _put_slot(src, n_slots, slot_and_blk, rows, dtype, name):
    _, c = src.shape
    tr = _row_tile(rows, c * 4)
    steps = rows // tr

    def body(idx_ref, s_ref, o_ref):
        o_ref[...] = s_ref[...].astype(dtype)

    grid_spec = pltpu.PrefetchScalarGridSpec(
        num_scalar_prefetch=1, grid=(steps,),
        in_specs=[pl.BlockSpec((tr, c), lambda i, idx_ref: (idx_ref[1] * steps + i, 0))],
        out_specs=pl.BlockSpec((None, tr, c), lambda i, idx_ref: (idx_ref[0], i, 0)))
    return pl.pallas_call(body, name=name, grid_spec=grid_spec,
                          out_shape=jax.ShapeDtypeStruct((n_slots, rows, c), dtype),
                          compiler_params=_params())(slot_and_blk, src)


def _rmsnorm_fwd(x, g):
    t, d = x.shape
    tm = min(t, 512)

    def body(x_ref, g_ref, o_ref):
        xv = x_ref[...]
        r = lax.rsqrt(jnp.mean(xv * xv, axis=-1, keepdims=True) + NORM_EPS)
        o_ref[...] = (xv * r * g_ref[...]).astype(BF16)

    return pl.pallas_call(
        body, name="rmsnorm_fwd", grid=(t // tm,), out_shape=jax.ShapeDtypeStruct((t, d), BF16),
        in_specs=[pl.BlockSpec((tm, d), lambda i: (i, 0)), pl.BlockSpec((1, d), lambda i: (0, 0))],
        out_specs=pl.BlockSpec((tm, d), lambda i: (i, 0)), compiler_params=_params())(x, g)


def _rmsnorm_bwd(x, dh, dx2, g):
    t, d = x.shape
    tm = min(t, 512)

    def body(x_ref, dh_ref, dx2_ref, g_ref, gx_ref, dg_ref):
        i = pl.program_id(0)
        xv = x_ref[...]
        dhv = dh_ref[...]
        r = lax.rsqrt(jnp.mean(xv * xv, axis=-1, keepdims=True) + NORM_EPS)
        nrm = xv * r
        dn = dhv * g_ref[...]
        gx_ref[...] = dx2_ref[...] + r * (dn - nrm * jnp.mean(dn * nrm, axis=-1, keepdims=True))

        @pl.when(i == 0)
        def _():
            dg_ref[...] = jnp.zeros_like(dg_ref)

        dg_ref[...] += jnp.sum(dhv * nrm, axis=0, keepdims=True)

    return pl.pallas_call(
        body, name="rmsnorm_bwd", grid=(t // tm,),
        out_shape=(jax.ShapeDtypeStruct((t, d), F32), jax.ShapeDtypeStruct((1, d), F32)),
        in_specs=[pl.BlockSpec((tm, d), lambda i: (i, 0)), pl.BlockSpec((tm, d), lambda i: (i, 0)),
                  pl.BlockSpec((tm, d), lambda i: (i, 0)), pl.BlockSpec((1, d), lambda i: (0, 0))],
        out_specs=(pl.BlockSpec((tm, d), lambda i: (i, 0)), pl.BlockSpec((1, d), lambda i: (0, 0))),
        compiler_params=_params())(x, dh, dx2, g)


def _adamw(w, g, m, v, name):
    r, c = w.shape
    tr = _row_tile(r, c * 4, 1024 * 1024)
    c1 = 1.0 - ADAM_B1 ** ADAM_STEP
    c2 = 1.0 - ADAM_B2 ** ADAM_STEP

    def body(w_ref, g_ref, m_ref, v_ref, d_ref, nm_ref, nv_ref):
        gv = g_ref[...]
        nm = ADAM_B1 * m_ref[...] + (1.0 - ADAM_B1) * gv
        nv = ADAM_B2 * v_ref[...] + (1.0 - ADAM_B2) * (gv * gv)
        m_hat = nm / c1
        v_hat = nv / c2
        d_ref[...] = -ADAM_LR * (m_hat / (jnp.sqrt(v_hat) + ADAM_EPS) + ADAM_WD * w_ref[...])
        nm_ref[...] = nm
        nv_ref[...] = nv

    spec = pl.BlockSpec((tr, c), lambda i: (i, 0))
    sds = jax.ShapeDtypeStruct((r, c), F32)
    return pl.pallas_call(
        body, name=name, grid=(r // tr,), out_shape=(sds, sds, sds),
        in_specs=[spec, spec, spec, spec], out_specs=(spec, spec, spec), compiler_params=_params())(w, g, m, v)


def _grad_h(dproj, w_bm):
    t = dproj.shape[0]
    nb, d, wb = w_bm.shape
    tm = min(t, 1024)

    def body(dp_ref, w_ref, o_ref, acc_ref):
        k = pl.program_id(1)

        @pl.when(k == 0)
        def _():
            acc_ref[...] = jnp.zeros_like(acc_ref)

        acc_ref[...] += _dot_nt(dp_ref[...], w_ref[...])

        @pl.when(k == nb - 1)
        def _():
            o_ref[...] = acc_ref[...]

    return pl.pallas_call(
        body, name="grad_h", grid=(t // tm, nb), out_shape=jax.ShapeDtypeStruct((t, d), F32),
        in_specs=[pl.BlockSpec((tm, wb), lambda i, k: (i, k)), pl.BlockSpec((None, d, wb), lambda i, k: (k, 0, 0))],
        out_specs=pl.BlockSpec((tm, d), lambda i, k: (i, 0)),
        scratch_shapes=[pltpu.VMEM((tm, d), F32)], compiler_params=_params())(dproj, w_bm)


def _shift_down(v, d, fill):
    n = v.shape[0]
    if d % 8 == 0:
        return jnp.concatenate([jnp.full((d,) + v.shape[1:], fill, v.dtype), v[: n - d]], axis=0)
    row = lax.broadcasted_iota(jnp.int32, v.shape, 0)
    return jnp.where(row >= d, pltpu.roll(v, d, axis=0), fill)


def _shift_up(v, d, fill):
    n = v.shape[0]
    if d % 8 == 0:
        return jnp.concatenate([v[d:], jnp.full((d,) + v.shape[1:], fill, v.dtype)], axis=0)
    row = lax.broadcasted_iota(jnp.int32, v.shape, 0)
    return jnp.where(row < n - d, pltpu.roll(v, n - d, axis=0), fill)


def _scan(a, b, shift):
    n = a.shape[0]
    d = 1
    while d < n:
        b = a * shift(b, d, 0.0) + b
        if 2 * d < n:
            a = a * shift(a, d, 1.0)
        d *= 2
    return b


def _neg_expm1(y):
    series = -y * (1.0 + y * (1.0 / 2.0) * (1.0 + y * (1.0 / 3.0) * (1.0 + y * (1.0 / 4.0) * (
        1.0 + y * (1.0 / 5.0) * (1.0 + y * (1.0 / 6.0) * (1.0 + y * (1.0 / 7.0)))))))
    return jnp.where(y > -0.25, series, 1.0 - jnp.exp(y))


def _softplus(z):
    e = jnp.exp(-jnp.abs(z))
    w = 1.0 + e
    log1p = jnp.where(w == 1.0, e, jnp.log(w) * (e / jnp.where(w == 1.0, 1.0, w - 1.0)))
    return jnp.maximum(z, 0.0) + log1p


def _conv(up, cw, cb):
    out = cb + cw[CONV_WIDTH - 1:CONV_WIDTH, :] * up
    for j in range(CONV_WIDTH - 1):
        out = out + cw[j:j + 1, :] * _shift_down(up, CONV_WIDTH - 1 - j, 0.0)
    return out


def _lru_gates(u, wa_ref, ba_ref, wx_ref, bx_ref, lam_ref):
    ub = u.astype(BF16)
    r = _sigmoid(_dot(ub, wa_ref[...].astype(BF16)) + ba_ref[...])
    i = _sigmoid(_dot(ub, wx_ref[...].astype(BF16)) + bx_ref[...])
    sp = _softplus(-lam_ref[...])
    log_a = (-LRU_C) * r * sp
    a = jnp.exp(log_a)
    mult = jnp.sqrt(_neg_expm1(2.0 * log_a))
    return r, i, sp, a, mult


def _lru_specs(s):
    cb = RNN_BLOCK
    vec = pl.BlockSpec((1, cb), lambda n, b: (0, n))
    return dict(
        up=pl.BlockSpec((None, s, cb), lambda n, b: (b, 0, OFF_RNN_X // cb + n)),
        gr=pl.BlockSpec((None, s, cb), lambda n, b: (b, 0, OFF_RNN_G // cb + n)),
        act=pl.BlockSpec((None, s, cb), lambda n, b: (b, 0, n)),
        cw=pl.BlockSpec((CONV_WIDTH, cb), lambda n, b: (0, n)),
        vec=vec,
        wblk=pl.BlockSpec((None, cb, cb), lambda n, b: (n, 0, 0)),
    )


def _lru_fwd(proj3, cw, cb, wa, ba, wx, bx, lam):
    bsz, s, _ = proj3.shape
    sp = _lru_specs(s)

    def body(up_ref, gr_ref, cw_ref, cb_ref, wa_ref, ba_ref, wx_ref, bx_ref, lam_ref, h_ref, y_ref):
        u = _conv(up_ref[...], cw_ref[...], cb_ref[...])
        _, i, _, a, mult = _lru_gates(u, wa_ref, ba_ref, wx_ref, bx_ref, lam_ref)
        h = _scan(a, mult * (i * u), _shift_down)
        h_ref[...] = h
        g = gr_ref[...]
        y_ref[...] = (h * (g * _sigmoid(g))).astype(BF16)

    return pl.pallas_call(
        body, name="lru_fwd", grid=(N_RNN_BLOCKS, bsz),
        out_shape=(jax.ShapeDtypeStruct((bsz, s, D_RNN), F32), jax.ShapeDtypeStruct((bsz, s, D_RNN), BF16)),
        in_specs=[sp["up"], sp["gr"], sp["cw"], sp["vec"], sp["wblk"], sp["vec"], sp["wblk"], sp["vec"], sp["vec"]],
        out_specs=(sp["act"], sp["act"]), compiler_params=_params())(proj3, proj3, cw, cb, wa, ba, wx, bx, lam)


def _lru_bwd(proj3, h3, dy3, cw, cb, wa, ba, wx, bx, lam):
    bsz, s, _ = proj3.shape
    sp = _lru_specs(s)

    def body(up_ref, gr_ref, h_ref, dy_ref, cw_ref, cb_ref, wa_ref, ba_ref, wx_ref, bx_ref, lam_ref,
             dup_ref, dgr_ref, dcw_ref, dcb_ref, dwa_ref, dba_ref, dwx_ref, dbx_ref, dlam_ref):
        b = pl.program_id(1)
        up = up_ref[...]
        cwv = cw_ref[...]
        u = _conv(up, cwv, cb_ref[...])
        r, i, spv, a, mult = _lru_gates(u, wa_ref, ba_ref, wx_ref, bx_ref, lam_ref)
        h = h_ref[...]
        g = gr_ref[...]
        dy = dy_ref[...]
        sg = _sigmoid(g)
        dgr_ref[...] = (dy * h * (sg * (1.0 + g * (1.0 - sg)))).astype(BF16)
        dh = dy * (g * sg)
        adj = _scan(_shift_up(a, 1, 0.0), dh, _shift_up)
        da = adj * _shift_down(h, 1, 0.0)
        dmult = adj * (i * u)
        di = adj * mult * u
        du = adj * mult * i
        dla = da * a - dmult * (a * a) / mult
        dr = dla * ((-LRU_C) * spv)
        dsp = jnp.sum(dla * ((-LRU_C) * r), axis=0, keepdims=True)
        dza = dr * r * (1.0 - r)
        dzx = di * i * (1.0 - i)
        ub = u.astype(BF16)
        dzab = dza.astype(BF16)
        dzxb = dzx.astype(BF16)
        du = du + _dot_nt(dzab, wa_ref[...].astype(BF16)) + _dot_nt(dzxb, wx_ref[...].astype(BF16))
        dup = cwv[CONV_WIDTH - 1:CONV_WIDTH, :] * du
        for j in range(CONV_WIDTH - 1):
            dup = dup + cwv[j:j + 1, :] * _shift_up(du, CONV_WIDTH - 1 - j, 0.0)
        dup_ref[...] = dup.astype(BF16)

        @pl.when(b == 0)
        def _():
            for ref in (dcw_ref, dcb_ref, dwa_ref, dba_ref, dwx_ref, dbx_ref, dlam_ref):
                ref[...] = jnp.zeros_like(ref)

        rows = [jnp.sum(du * _shift_down(up, CONV_WIDTH - 1 - j, 0.0), axis=0, keepdims=True)
                for j in range(CONV_WIDTH - 1)]
        rows.append(jnp.sum(du * up, axis=0, keepdims=True))
        dcw_ref[...] += jnp.concatenate(rows, axis=0)
        dcb_ref[...] += jnp.sum(du, axis=0, keepdims=True)
        dwa_ref[...] += _dot_tn(ub, dzab)
        dba_ref[...] += jnp.sum(dza, axis=0, keepdims=True)
        dwx_ref[...] += _dot_tn(ub, dzxb)
        dbx_ref[...] += jnp.sum(dzx, axis=0, keepdims=True)
        dlam_ref[...] += dsp * (-_sigmoid(-lam_ref[...]))

    act_b = jax.ShapeDtypeStruct((bsz, s, D_RNN), BF16)
    vec = jax.ShapeDtypeStruct((1, D_RNN), F32)
    wsd = jax.ShapeDtypeStruct((N_RNN_BLOCKS, RNN_BLOCK, RNN_BLOCK), F32)
    return pl.pallas_call(
        body, name="lru_bwd", grid=(N_RNN_BLOCKS, bsz),
        out_shape=(act_b, act_b, jax.ShapeDtypeStruct((CONV_WIDTH, D_RNN), F32), vec, wsd, vec, wsd, vec, vec),
        in_specs=[sp["up"], sp["gr"], sp["act"], sp["act"], sp["cw"], sp["vec"], sp["wblk"], sp["vec"],
                  sp["wblk"], sp["vec"], sp["vec"]],
        out_specs=(sp["act"], sp["act"], sp["cw"], sp["vec"], sp["wblk"], sp["vec"], sp["wblk"], sp["vec"], sp["vec"]),
        compiler_params=_params())(proj3, proj3, h3, dy3, cw, cb, wa, ba, wx, bx, lam)


def _rope_tables(s):
    half = ROPE_DIM // 2
    pos = jnp.arange(s, dtype=F32)
    inv_freq = ROPE_THETA ** (-jnp.arange(0, ROPE_DIM, 2, dtype=F32) / ROPE_DIM)
    ang = pos[:, None] * inv_freq[None, :]
    cos, sin = jnp.cos(ang), jnp.sin(ang)
    rest = HEAD_DIM - ROPE_DIM
    cos64 = jnp.concatenate([cos, cos, jnp.ones((s, rest), F32)], axis=1)
    sin64 = jnp.concatenate([-sin, sin, jnp.zeros((s, rest), F32)], axis=1)
    assert half * 2 == ROPE_DIM
    return jnp.tile(cos64, (1, LANES // HEAD_DIM)), jnp.tile(sin64, (1, LANES // HEAD_DIM))


def _swap_rot_halves(v):
    half = ROPE_DIM // 2
    lane = lax.broadcasted_iota(jnp.int32, v.shape, 1) % HEAD_DIM
    second = jnp.where(lane < ROPE_DIM, pltpu.roll(v, half, axis=1), 0.0)
    return jnp.where(lane < half, pltpu.roll(v, LANES - half, axis=1), second)


def _rope(v, cos, sin):
    tiles = []
    for t in range(v.shape[1] // LANES):
        vt = v[:, t * LANES:(t + 1) * LANES]
        tiles.append(vt * cos + _swap_rot_halves(vt) * sin)
    return tiles[0] if len(tiles) == 1 else jnp.concatenate(tiles, axis=1)


def _unrope(v, cos, sin):
    tiles = []
    for t in range(v.shape[1] // LANES):
        vt = v[:, t * LANES:(t + 1) * LANES]
        tiles.append(vt * cos + _swap_rot_halves(vt * sin))
    return tiles[0] if len(tiles) == 1 else jnp.concatenate(tiles, axis=1)


HEADS_PER_STEP = 8
QW = HEADS_PER_STEP * HEAD_DIM
N_PAIRS = N_Q_HEADS // HEADS_PER_STEP
Q_PER_KV = N_Q_HEADS // N_KV_HEADS
KV_PER_STEP = HEADS_PER_STEP // Q_PER_KV


QT_COLS = Q_PER_KV * ATT_BLK


def _attn_scratch(s, with_vt):
    nb = s // ATT_BLK
    pad = s + ATT_BLK
    shapes = [pltpu.VMEM((nb, LANES, QT_COLS), BF16),
              pltpu.VMEM((KV_PER_STEP, pad, LANES), BF16),
              pltpu.VMEM((KV_PER_STEP, pad, LANES), BF16)]
    if with_vt:
        shapes.append(pltpu.VMEM((LANES, pad), BF16))
    return shapes


def _attn_specs(s, order):
    def mk(width, base, **kw):
        if order == "bp":
            return pl.BlockSpec((None, s, width), lambda b, p: (b, 0, base + p), **kw)
        return pl.BlockSpec((None, s, width), lambda p, b: (b, 0, base + p), **kw)
    one = dict(pipeline_mode=pl.Buffered(1))
    tbl = pl.BlockSpec((s, LANES), lambda *_: (0, 0))
    return dict(q=mk(QW, OFF_Q // QW), k=mk(LANES, OFF_K // LANES), v=mk(LANES, OFF_V // LANES),
                g=mk(QW, OFF_ATTN_G // QW), act=mk(QW, 0), kv=mk(LANES, 0), tbl=tbl,
                q1=mk(QW, OFF_Q // QW, **one), g1=mk(QW, OFF_ATTN_G // QW, **one), act1=mk(QW, 0, **one),
                smem=pl.BlockSpec(memory_space=pltpu.SMEM))


def _to_qt(blk):
    rows = []
    for j in range(KV_PER_STEP):
        cols = []
        for tt in range(2):
            t = 2 * j + tt
            tr = blk[:, t * LANES:(t + 1) * LANES].T
            cols += [tr[0:HEAD_DIM, :], tr[HEAD_DIM:, :]]
        rows.append(jnp.concatenate(cols, axis=1))
    return jnp.concatenate(rows, axis=0)


def _from_qt(xt):
    tiles = []
    for j in range(KV_PER_STEP):
        for tt in range(2):
            g0 = 2 * tt
            pair = jnp.concatenate([xt[j * HEAD_DIM:(j + 1) * HEAD_DIM, (g0 + i) * ATT_BLK:(g0 + i + 1) * ATT_BLK]
                                    for i in range(2)], axis=0)
            tiles.append(pair.T)
    return jnp.concatenate(tiles, axis=1)


def _attn_prep(q_ref, k_ref, v_ref, cos_ref, sin_ref, qt_sc, km_sc, vm_sc, t_sc, transposed, nb):
    zeros = jnp.zeros((ATT_BLK, LANES), BF16)
    for j in range(KV_PER_STEP):
        km_sc[j, 0:ATT_BLK, :] = zeros
        vm_sc[j, 0:ATT_BLK, :] = zeros
    t_sc[:, 0:ATT_BLK] = zeros
    head_of_lane = lax.broadcasted_iota(jnp.int32, (ATT_BLK, LANES), 1) // HEAD_DIM

    def prep(n, carry):
        r0 = pl.multiple_of(n * ATT_BLK, ATT_BLK)
        cs = cos_ref[pl.ds(r0, ATT_BLK), :]
        sn = sin_ref[pl.ds(r0, ATT_BLK), :]
        qt_sc[n] = _to_qt(_rope(q_ref[pl.ds(r0, ATT_BLK), :], cs, sn) * ATTN_SCALE).astype(BF16)
        k = _rope(k_ref[pl.ds(r0, ATT_BLK), :], cs, sn)
        v = v_ref[pl.ds(r0, ATT_BLK), :]
        for j in range(KV_PER_STEP):
            km_sc[j, pl.ds(r0 + ATT_BLK, ATT_BLK), :] = jnp.where(head_of_lane == j, k, 0.0).astype(BF16)
            vm_sc[j, pl.ds(r0 + ATT_BLK, ATT_BLK), :] = jnp.where(head_of_lane == j, v, 0.0).astype(BF16)
        t_sc[:, pl.ds(r0 + ATT_BLK, ATT_BLK)] = (k if transposed == "k" else v).T.astype(BF16)
        return carry

    lax.fori_loop(0, nb, prep, 0)


def _band_mask_t(n):
    shape = (2 * ATT_BLK, QT_COLS)
    key = lax.broadcasted_iota(jnp.int32, shape, 0)
    qry = lax.broadcasted_iota(jnp.int32, shape, 1) % ATT_BLK
    lo = jnp.where(n == 0, ATT_BLK, 0)
    return (key > qry) & (key <= qry + WINDOW) & (key >= lo)


def _sink_row(sink_ref, first):
    return jnp.concatenate([jnp.full((1, ATT_BLK), sink_ref[first + g], F32) for g in range(Q_PER_KV)], axis=1)


def _softmax_cols(scores_t, valid, sink):
    sc = jnp.where(valid, scores_t, NEG_BIG)
    m = jnp.maximum(jnp.max(sc, axis=0, keepdims=True), sink)
    e = jnp.exp(sc - m)
    es = jnp.exp(sink - m)
    inv = 1.0 / (jnp.sum(e, axis=0, keepdims=True) + es)
    return e * inv, es * inv


def _attn_fwd(proj3, sinks, cosf, sinf):
    bsz, s, _ = proj3.shape
    nb = s // ATT_BLK
    sp = _attn_specs(s, "bp")

    def body(sink_ref, q_ref, k_ref, v_ref, g_ref, cos_ref, sin_ref, o_ref, y_ref, qt_sc, km_sc, vm_sc, vt_sc):
        p = pl.program_id(1)
        _attn_prep(q_ref, k_ref, v_ref, cos_ref, sin_ref, qt_sc, km_sc, vm_sc, vt_sc, "v", nb)
        kv_row = lax.broadcasted_iota(jnp.int32, (LANES, QT_COLS), 0) // HEAD_DIM

        def blk(n, carry):
            r0 = pl.multiple_of(n * ATT_BLK, ATT_BLK)
            valid = _band_mask_t(n)
            rq = qt_sc[n]
            vt = vt_sc[:, pl.ds(r0, 2 * ATT_BLK)]
            ots = []
            for j in range(KV_PER_STEP):
                st = _dot(km_sc[j, pl.ds(r0, 2 * ATT_BLK), :], rq)
                pt, _ = _softmax_cols(st, valid, _sink_row(sink_ref, p * HEADS_PER_STEP + j * Q_PER_KV))
                ots.append(_dot(vt, pt.astype(BF16)))
            o = _from_qt(jnp.where(kv_row == 0, ots[0], ots[1]))
            o_ref[pl.ds(r0, ATT_BLK), :] = o
            g = g_ref[pl.ds(r0, ATT_BLK), :]
            y_ref[pl.ds(r0, ATT_BLK), :] = (o * (g * _sigmoid(g))).astype(BF16)
            return carry

        lax.fori_loop(0, nb, blk, 0)

    return pl.pallas_call(
        body, name="attn_fwd", grid=(bsz, N_PAIRS),
        out_shape=(jax.ShapeDtypeStruct((bsz, s, D_ATTN), F32), jax.ShapeDtypeStruct((bsz, s, D_ATTN), BF16)),
        in_specs=[sp["smem"], sp["q"], sp["k"], sp["v"], sp["g"], sp["tbl"], sp["tbl"]],
        out_specs=(sp["act"], sp["act"]),
        scratch_shapes=_attn_scratch(s, True),
        compiler_params=_params())(sinks, proj3, proj3, proj3, proj3, cosf, sinf)


def _attn_bwd(proj3, o3, dy3, sinks, cosf, sinf):
    bsz, s, _ = proj3.shape
    nb = s // ATT_BLK
    sp = _attn_specs(s, "pb")

    def body(sink_ref, q_ref, k_ref, v_ref, g_ref, o_ref, dy_ref, cos_ref, sin_ref,
             dq_ref, dk_ref, dv_ref, dg_ref, ds_ref, qt_sc, km_sc, vm_sc, kt_sc, dot_sc, dqt_sc, dk_sc, dv_sc):
        p = pl.program_id(0)
        b = pl.program_id(1)
        _attn_prep(q_ref, k_ref, v_ref, cos_ref, sin_ref, qt_sc, km_sc, vm_sc, kt_sc, "k", nb)
        dk_sc[...] = jnp.zeros_like(dk_sc)
        dv_sc[...] = jnp.zeros_like(dv_sc)

        def gate(n, carry):
            r0 = pl.multiple_of(n * ATT_BLK, ATT_BLK)
            g = g_ref[pl.ds(r0, ATT_BLK), :]
            dy = dy_ref[pl.ds(r0, ATT_BLK), :]
            sg = _sigmoid(g)
            dg_ref[pl.ds(r0, ATT_BLK), :] = (dy * o_ref[pl.ds(r0, ATT_BLK), :] * (sg * (1.0 + g * (1.0 - sg)))).astype(BF16)
            dot_sc[n] = _to_qt(dy * (g * sg)).astype(BF16)
            return carry

        lax.fori_loop(0, nb, gate, 0)
        kv_lane = lax.broadcasted_iota(jnp.int32, (2 * ATT_BLK, LANES), 1) // HEAD_DIM
        kv_row = lax.broadcasted_iota(jnp.int32, (LANES, QT_COLS), 0) // HEAD_DIM

        def blk(n, acc):
            r0 = pl.multiple_of(n * ATT_BLK, ATT_BLK)
            valid = _band_mask_t(n)
            rq = qt_sc[n]
            rd = dot_sc[n]
            kt = kt_sc[:, pl.ds(r0, 2 * ATT_BLK)]
            dvs, dks, dqs, new_acc = [], [], [], []
            for j in range(KV_PER_STEP):
                st = _dot(km_sc[j, pl.ds(r0, 2 * ATT_BLK), :], rq)
                pt, ps = _softmax_cols(st, valid, _sink_row(sink_ref, p * HEADS_PER_STEP + j * Q_PER_KV))
                dpt = _dot(vm_sc[j, pl.ds(r0, 2 * ATT_BLK), :], rd)
                delta = jnp.sum(pt * dpt, axis=0, keepdims=True)
                dst = (pt * (dpt - delta)).astype(BF16)
                new_acc.append(acc[j] + ps * delta)
                dvs.append(_dot_nt(pt.astype(BF16), rd))
                dks.append(_dot_nt(dst, rq))
                dqs.append(_dot(kt, dst))
            dv_sc[pl.ds(r0, 2 * ATT_BLK), :] += jnp.where(kv_lane == 0, dvs[0], dvs[1])
            dk_sc[pl.ds(r0, 2 * ATT_BLK), :] += jnp.where(kv_lane == 0, dks[0], dks[1])
            dqt_sc[n] = jnp.where(kv_row == 0, dqs[0], dqs[1]) * ATTN_SCALE
            return tuple(new_acc)

        acc = lax.fori_loop(0, nb, blk, tuple(jnp.zeros((1, QT_COLS), F32) for _ in range(KV_PER_STEP)))
        lane1 = lax.broadcasted_iota(jnp.int32, (1, LANES), 1)
        dsink = jnp.zeros((1, LANES), F32)
        for j in range(KV_PER_STEP):
            for i in range(Q_PER_KV):
                part = jnp.sum(acc[j][:, i * ATT_BLK:(i + 1) * ATT_BLK], axis=1, keepdims=True)
                dsink = dsink - jnp.where(lane1 == j * Q_PER_KV + i, part, 0.0)

        @pl.when(b == 0)
        def _():
            ds_ref[...] = jnp.zeros_like(ds_ref)

        ds_ref[...] += dsink

        def post(n, carry):
            r0 = pl.multiple_of(n * ATT_BLK, ATT_BLK)
            cs = cos_ref[pl.ds(r0, ATT_BLK), :]
            sn = sin_ref[pl.ds(r0, ATT_BLK), :]
            dq_ref[pl.ds(r0, ATT_BLK), :] = _unrope(_from_qt(dqt_sc[n]), cs, sn).astype(BF16)
            dk_ref[pl.ds(r0, ATT_BLK), :] = _unrope(dk_sc[pl.ds(r0 + ATT_BLK, ATT_BLK), :], cs, sn).astype(BF16)
            dv_ref[pl.ds(r0, ATT_BLK), :] = dv_sc[pl.ds(r0 + ATT_BLK, ATT_BLK), :].astype(BF16)
            return carry

        lax.fori_loop(0, nb, post, 0)

    act = jax.ShapeDtypeStruct((bsz, s, D_ATTN), BF16)
    kvs = jax.ShapeDtypeStruct((bsz, s, D_KV), BF16)
    return pl.pallas_call(
        body, name="attn_bwd", grid=(N_PAIRS, bsz),
        out_shape=(act, kvs, kvs, act, jax.ShapeDtypeStruct((N_PAIRS, 1, LANES), F32)),
        in_specs=[sp["smem"], sp["q1"], sp["k"], sp["v"], sp["g1"], sp["act1"], sp["act1"], sp["tbl"], sp["tbl"]],
        out_specs=(sp["act"], sp["kv"], sp["kv"], sp["act"], pl.BlockSpec((None, 1, LANES), lambda p, b: (p, 0, 0))),
        scratch_shapes=_attn_scratch(s, True) + [pltpu.VMEM((nb, LANES, QT_COLS), BF16),
                                                 pltpu.VMEM((nb, LANES, QT_COLS), F32),
                                                 pltpu.VMEM((s + ATT_BLK, LANES), F32),
                                                 pltpu.VMEM((s + ATT_BLK, LANES), F32)],
        compiler_params=_params())(sinks, proj3, proj3, proj3, proj3, o3, dy3, cosf, sinf)


def _merge_fwd_bwd(x, tgt, y_rnn, y_attn, proj, w_r, w_a, w_o, gf):
    t, d = x.shape
    tm = min(t, 256)

    hw = d // 2

    def body(x_ref, t_ref, yr_ref, ya_ref, mr0_ref, mr1_ref, ma0_ref, ma1_ref, wr_ref, wa_ref, wo_ref, gf_ref,
             dmg_ref, dyr_ref, dya_ref, mg_ref, dx2_ref, dx2b_ref, dpr_ref, dpa_ref, loss_ref, dgf_ref):
        i = pl.program_id(0)
        wr = wr_ref[...]
        wa = wa_ref[...]
        wo = wo_ref[...]
        gfv = gf_ref[...]
        pr = _dot(yr_ref[...], wr)
        pa = _dot(ya_ref[...], wa)
        sr = _sigmoid(jnp.concatenate([mr0_ref[...], mr1_ref[...]], axis=1))
        sa = _sigmoid(jnp.concatenate([ma0_ref[...], ma1_ref[...]], axis=1))
        mb = (sr * pr + sa * pa).astype(BF16)
        mg_ref[...] = mb
        x2 = x_ref[...] + _dot(mb, wo)
        r2 = lax.rsqrt(jnp.mean(x2 * x2, axis=-1, keepdims=True) + NORM_EPS)
        nrm = x2 * r2
        err = nrm * gfv - t_ref[...]
        dy = err * (1.0 / d)
        dn = dy * gfv
        dx2 = r2 * (dn - nrm * jnp.mean(dn * nrm, axis=-1, keepdims=True))
        dx2_ref[...] = dx2
        dx2b = dx2.astype(BF16)
        dx2b_ref[...] = dx2b
        dmerged = _dot_nt(dx2b, wo)
        dpr = (dmerged * sr).astype(BF16)
        dpa = (dmerged * sa).astype(BF16)
        dpr_ref[...] = dpr
        dpa_ref[...] = dpa
        dmg_ref[:, 0:d] = (dmerged * pr * (sr * (1.0 - sr))).astype(BF16)
        dmg_ref[:, d:2 * d] = (dmerged * pa * (sa * (1.0 - sa))).astype(BF16)
        dyr_ref[...] = _dot_nt(dpr, wr)
        dya_ref[...] = _dot_nt(dpa, wa)

        @pl.when(i == 0)
        def _():
            loss_ref[...] = jnp.zeros_like(loss_ref)
            dgf_ref[...] = jnp.zeros_like(dgf_ref)

        loss_ref[...] += jnp.full((1, LANES), 0.5 / d, F32) * jnp.sum(err * err)
        dgf_ref[...] += jnp.sum(dy * nrm, axis=0, keepdims=True)

    tile = pl.BlockSpec((tm, d), lambda i: (i, 0))
    wsp = pl.BlockSpec((d, d), lambda i: (0, 0))

    def gate(col_blk):
        return pl.BlockSpec((tm, hw), lambda i: (i, col_blk))

    fb = jax.ShapeDtypeStruct((t, d), BF16)
    ff = jax.ShapeDtypeStruct((t, d), F32)
    return pl.pallas_call(
        body, name="merge_fwd_bwd", grid=(t // tm,),
        out_shape=(jax.ShapeDtypeStruct((t, 2 * d), BF16), ff, ff, fb, ff, fb, fb, fb,
                   jax.ShapeDtypeStruct((1, LANES), F32), jax.ShapeDtypeStruct((1, d), F32)),
        in_specs=[tile, tile, tile, tile] + [gate(OFF_MERGE_R // hw + j) for j in range(4)] + [
            wsp, wsp, wsp, pl.BlockSpec((1, d), lambda i: (0, 0))],
        out_specs=(pl.BlockSpec((tm, 2 * d), lambda i: (i, 0)), tile, tile, tile, tile, tile, tile, tile,
                   pl.BlockSpec((1, LANES), lambda i: (0, 0)), pl.BlockSpec((1, d), lambda i: (0, 0))),
        compiler_params=_params())(x, tgt, y_rnn, y_attn, proj, proj, proj, proj, w_r, w_a, w_o, gf)


def _local_grads(x, tgt, h, proj, norm_g, w_in_bm, conv_w, conv_b, lru_w_a, lru_b_a, lru_w_x, lru_b_x, lam, sinks,
                 w_r, w_a, w_o, gf):
    bsz, s, d = x.shape
    t = bsz * s
    x2 = x.reshape(t, d)
    proj3 = proj.reshape(bsz, s, D_IN)
    h_lru, y_rnn = _lru_fwd(proj3, conv_w, conv_b, lru_w_a, lru_b_a, lru_w_x, lru_b_x, lam)
    cosf, sinf = _rope_tables(s)
    o_attn, y_attn = _attn_fwd(proj3, sinks, cosf, sinf)
    y_rnn2 = y_rnn.reshape(t, d)
    y_attn2 = y_attn.reshape(t, d)
    dmg, dyr, dya, merged, dx2, dx2b, dpr, dpa, loss, dgf = _merge_fwd_bwd(
        x2, tgt.reshape(t, d), y_rnn2, y_attn2, proj, w_r, w_a, w_o, gf)
    dup, dgr, dcw, dcb, dwa, dba, dwx, dbx, dlam = _lru_bwd(
        proj3, h_lru, dyr.reshape(bsz, s, d), conv_w, conv_b, lru_w_a, lru_b_a, lru_w_x, lru_b_x, lam)
    dq, dk, dv, dga, dsink = _attn_bwd(proj3, o_attn, dya.reshape(bsz, s, d), sinks, cosf, sinf)
    dproj = jnp.concatenate([dup, dgr, dq, dk, dv, dga, dmg.reshape(bsz, s, 2 * d)], axis=-1).reshape(t, D_IN)
    dh = _grad_h(dproj, w_in_bm)
    grad_x, dng = _rmsnorm_bwd(x2, dh, dx2, norm_g)
    small = dict(norm_g=dng, conv_w=dcw, conv_b=dcb, lru_w_a=dwa, lru_b_a=dba, lru_w_x=dwx, lru_b_x=dbx,
                 lru_lambda=dlam, attn_sinks=dsink[:, 0, :HEADS_PER_STEP].reshape(1, N_Q_HEADS), final_norm_g=dgf)
    squares = [(y_rnn2, dpr), (y_attn2, dpa), (merged, dx2b)]
    return loss[0, 0], grad_x.reshape(bsz, s, d), h, dproj, squares, small


ANY = pl.BlockSpec(memory_space=pl.ANY)


def _mesh_pos():
    return lax.axis_index("x"), lax.axis_index("y"), lax.axis_index("c")


def _remote(src, dst, send_sems, recv_sems, idx, peer):
    return pltpu.make_async_remote_copy(src_ref=src, dst_ref=dst, send_sem=send_sems.at[idx],
                                        recv_sem=recv_sems.at[idx], device_id=peer, device_id_type=MESH)


def _gather_in_proj(h, bufs, split, idx):
    t_tok, d = h.shape
    n = len(bufs)
    tm = min(t_tok, 512)
    nt = t_tok // tm
    n_fwd = 3 * sum(split)
    assert split[0]

    def body(idx_ref, h_ref, *refs):
        ins, proj_ref, outs = refs[:n], refs[n], refs[n + 1:2 * n + 1]
        wbuf, send_sems, recv_sems, fsend_sems, frecv_sems, l_sems = refs[2 * n + 1:]
        j, i = pl.program_id(0), pl.program_id(1)
        x, y, c = _mesh_pos()
        me = 2 * x + y
        sib = (x, y, 1 - c)
        peers = [((x, 1 - y, c), me ^ 1), ((1 - x, y, c), me ^ 2), ((1 - x, 1 - y, c), me ^ 3)]

        def part(ref, slot, t, half):
            if not split[t]:
                return ref.at[slot]
            hr = bufs[t].shape[1] // 2
            return ref.at[slot, pl.ds(pl.multiple_of(half * hr, 8), hr), :]

        def land(t):
            return wbuf if t == 0 else outs[t]

        def ici(t, k):
            peer, pj = peers[k]
            src = part(ins[t], me, t, c)
            return (_remote(src, part(land(t), me, t, c), send_sems, recv_sems, 3 * t + k, peer),
                    _remote(src, part(land(t), pj, t, c), send_sems, recv_sems, 3 * t + k, peer))

        fwd_index = {}
        for t in range(n):
            if split[t]:
                for k in range(3):
                    fwd_index[(t, k)] = len(fwd_index)

        def forward(t, k):
            pj = peers[k][1]
            got = part(land(t), pj, t, c)
            f = fwd_index[(t, k)]
            return (_remote(got, got, fsend_sems, frecv_sems, f, sib),
                    _remote(got, part(land(t), pj, t, 1 - c), fsend_sems, frecv_sems, f, sib))

        def write_back(k):
            pj = peers[k][1]
            return pltpu.make_async_copy(wbuf.at[pj], outs[0].at[pj], l_sems.at[1 + k])

        @pl.when((j == 0) & (i == 0))
        def _():
            for t in range(n):
                for k in range(3):
                    ici(t, k)[0].start()
            own = pltpu.make_async_copy(ins[0].at[me], wbuf.at[me], l_sems.at[0])
            own.start()
            own.wait()

        for k in range(3):
            @pl.when((j == k + 1) & (i == 0))
            def _():
                ici(0, k)[1].wait_recv()
                forward(0, k)[0].start()
                forward(0, k)[1].wait_recv()
                write_back(k).start()

        proj_ref[...] = _dot(h_ref[...], wbuf[me ^ j])

        @pl.when((j == N_CHIPS - 1) & (i == nt - 1))
        def _():
            for t in range(1, n):
                for k in range(3):
                    ici(t, k)[1].wait_recv()
                    if split[t]:
                        forward(t, k)[0].start()
            for t in range(n):
                for k in range(3):
                    ici(t, k)[0].wait_send()
                    if split[t]:
                        forward(t, k)[0].wait_send()
                        if t > 0:
                            forward(t, k)[1].wait_recv()
            for k in range(3):
                write_back(k).wait()

    grid_spec = pltpu.PrefetchScalarGridSpec(
        num_scalar_prefetch=1, grid=(N_CHIPS, nt),
        in_specs=[pl.BlockSpec((tm, d), lambda j, i, idx_ref: (i, 0))] + [ANY] * n,
        out_specs=[pl.BlockSpec((tm, W_BLK), lambda j, i, idx_ref: (i, idx_ref[0] ^ j))] + [ANY] * n,
        scratch_shapes=[pltpu.VMEM(bufs[0].shape, bufs[0].dtype),
                        pltpu.SemaphoreType.DMA((3 * n,)), pltpu.SemaphoreType.DMA((3 * n,)),
                        pltpu.SemaphoreType.DMA((n_fwd,)), pltpu.SemaphoreType.DMA((n_fwd,)),
                        pltpu.SemaphoreType.DMA((4,))])
    out_shape = [jax.ShapeDtypeStruct((t_tok, D_IN), F32)] + [jax.ShapeDtypeStruct(a.shape, a.dtype) for a in bufs]
    res = pl.pallas_call(
        body, name="gather_in_proj", grid_spec=grid_spec, out_shape=out_shape,
        input_output_aliases={2 + t: 1 + t for t in range(n)}, compiler_params=_params())(idx, h, *bufs)
    return res[0], res[1:]


def _row_tile(rows, row_bytes, cap_bytes=2 * 1024 * 1024):
    best = None
    for tr in range(8, rows + 1, 8):
        if rows % tr == 0 and tr * row_bytes <= cap_bytes:
            best = tr
    return best if best is not None else rows


XOR_ORDER = (3, 2, 1)


def _grads_reduce_scatter(h, dproj, squares, small, idx):
    t, d = h.shape
    nsq = len(squares)
    hr = d // 2
    qr = ROW_BLK // 2
    tk = min(t, 1024)
    nk = t // tk
    last = N_CHIPS - 1
    n_phase = 3

    def dest(s, idx_ref):
        xo = jnp.where(s == 0, XOR_ORDER[0], jnp.where(s == 1, XOR_ORDER[1], jnp.where(s == 2, XOR_ORDER[2], 0)))
        return idx_ref[0] ^ xo

    def k_sq(p, k):
        return jnp.where(p == 0, k, nk - 1)

    def k_w(p, k):
        return jnp.where(p == 0, 0, k)

    in_specs = [
        pl.BlockSpec((tk, hr), lambda s, p, k, idx_ref: (k_w(p, k), (1 - idx_ref[1] + jnp.maximum(p - 1, 0)) % 2)),
        pl.BlockSpec((tk, W_BLK), lambda s, p, k, idx_ref: (k_w(p, k), dest(s, idx_ref)))]
    for q in range(nsq):
        in_specs.append(pl.BlockSpec((tk, ROW_BLK), lambda s, p, k, idx_ref: (k_sq(p, k), dest(s, idx_ref))))
        in_specs.append(pl.BlockSpec((tk, d), lambda s, p, k, idx_ref: (k_sq(p, k), 0)))

    def body(idx_ref, *refs):
        nj = 1 + nsq
        h_ref, dp_ref = refs[0], refs[1]
        sq_in = refs[2:2 + 2 * nsq]
        small_in = refs[2 * nj]
        outs = refs[2 * nj + 1:3 * nj + 2]
        landing = refs[3 * nj + 2:4 * nj + 3]
        sc = refs[4 * nj + 3:]
        acc_w, xr_w, sb_w = sc[0:3]
        sq_sc = [sc[3 + 3 * q:6 + 3 * q] for q in range(nsq)]
        sm, smx = sc[3 * nj:3 * nj + 2]
        x_send, x_recv, i_send, i_recv, f_send, f_recv, o_sem, l_sem = sc[3 * nj + 2:]
        s, p, k = pl.program_id(0), pl.program_id(1), pl.program_id(2)
        x, y, c = _mesh_pos()
        sib = (x, y, 1 - c)
        peers = [((1 - x) if xo & 2 else x, (1 - y) if xo & 1 else y, c) for xo in XOR_ORDER]
        slot = s % 2
        mine_w = pl.ds(pl.multiple_of(c * hr, 8), hr)
        theirs_w = pl.ds(pl.multiple_of((1 - c) * hr, 8), hr)
        mine_q = pl.ds(pl.multiple_of(c * qr, 8), qr)
        theirs_q = pl.ds(pl.multiple_of((1 - c) * qr, 8), qr)

        def exch(j, src, dst):
            return _remote(src, dst, x_send, x_recv, 2 * j + slot, sib)

        sbufs = [sb_w] + [sq_sc[q][2] for q in range(nsq)]

        def ici(j, ss):
            return _remote(sbufs[j].at[ss], landing[j].at[ss], i_send, i_recv, last * j + ss, peers[ss])

        def exchanges():
            cps = [exch(0, acc_w.at[0], xr_w.at[slot])]
            cps += [exch(1 + q, sq_sc[q][0].at[theirs_q, :], sq_sc[q][1].at[slot]) for q in range(nsq)]
            return cps

        def small_send(ss):
            return _remote(sm.at[c], landing[nj].at[ss], i_send, i_recv, last * nj + ss, peers[ss])

        def small_start():
            load = pltpu.make_async_copy(small_in, sm, l_sem.at[nj + 1])
            load.start()
            load.wait()
            swap = _remote(sm, smx.at[pl.ds(0, 2)], x_send, x_recv, 2 * nj, sib)
            swap.start()
            swap.wait_recv()
            swap.wait_send()
            sm[...] = sm[...] + smx[0:2]
            for ss in range(last):
                small_send(ss).start()

        def pair_ref(j):
            return acc_w.at[1] if j == 0 else sq_sc[j - 1][0].at[mine_q, :]

        def sq_phase():
            pl.when((s == 0) & (k == 0))(small_start)
            for q in range(nsq):
                acc = sq_sc[q][0]

                @pl.when(k == 0)
                def _():
                    acc[...] = jnp.zeros((ROW_BLK, d), F32)

                acc[...] += _dot_tn(sq_in[2 * q][...], sq_in[2 * q + 1][...])

            @pl.when(k == nk - 1)
            def _():
                for cp in exchanges()[1:]:
                    cp.start()

        def w_phase(hf):
            @pl.when(k == 0)
            def _():
                acc_w[hf] = jnp.zeros((hr, W_BLK), F32)

            acc_w[hf] += _dot_tn(h_ref[...], dp_ref[...])

            @pl.when(k == nk - 1)
            def _():
                if hf == 0:
                    exchanges()[0].start()
                else:
                    finish_step()

        def finish_step():
            for cp in exchanges():
                cp.wait_recv()
                cp.wait_send()
            acc_w[1] += xr_w[slot]
            for q in range(nsq):
                sq_sc[q][0][mine_q, :] += sq_sc[q][1][slot]
            for ss in range(last):
                @pl.when(s == ss)
                def _():
                    for j in range(nj):
                        sbufs[j][ss] = pair_ref(j)[...].astype(BF16)
                        ici(j, ss).start()

            @pl.when(s == last)
            def _():
                for ss in range(last):
                    for j in range(nj):
                        ici(j, ss).wait_recv()
                        ici(j, ss).wait_send()
                    small_send(ss).wait_recv()
                    small_send(ss).wait_send()
                stage = [pltpu.make_async_copy(landing[j], sbufs[j], l_sem.at[j]) for j in range(nj)]
                stage.append(pltpu.make_async_copy(landing[nj], smx, l_sem.at[nj]))
                for cp in stage:
                    cp.start()
                for j in range(nj):
                    stage[j].wait()
                    total = pair_ref(j)[...]
                    for ss in range(last):
                        total = total + sbufs[j][ss].astype(F32)
                    pair_ref(j)[...] = total
                stage[nj].wait()
                by_xor = {xo: smx[ss] for ss, xo in enumerate(XOR_ORDER)}
                sm[c] = (sm[c] + by_xor[1]) + (by_xor[2] + by_xor[3])
                done = [(acc_w.at[1], outs[0].at[mine_w, :], outs[0].at[theirs_w, :])]
                done += [(pair_ref(1 + q), outs[1 + q].at[mine_q, :], outs[1 + q].at[theirs_q, :]) for q in range(nsq)]
                done.append((sm.at[c], outs[nj].at[c], outs[nj].at[1 - c]))
                copies = []
                for j, (src, mine, theirs) in enumerate(done):
                    keep = pltpu.make_async_copy(src, mine, o_sem.at[j])
                    give = _remote(src, mine, f_send, f_recv, j, sib)
                    take = _remote(src, theirs, f_send, f_recv, j, sib)
                    keep.start()
                    give.start()
                    copies.append((keep, give, take))
                for keep, give, take in copies:
                    keep.wait()
                    give.wait_send()
                    take.wait_recv()

        pl.when(p == 0)(sq_phase)
        for hf in range(2):
            pl.when(p == 1 + hf)(functools.partial(w_phase, hf))

    nj = 1 + nsq
    scratch = [pltpu.VMEM((2, hr, W_BLK), F32), pltpu.VMEM((2, hr, W_BLK), F32), pltpu.VMEM((last, hr, W_BLK), BF16)]
    for _ in range(nsq):
        scratch += [pltpu.VMEM((ROW_BLK, d), F32), pltpu.VMEM((2, qr, d), F32), pltpu.VMEM((last, qr, d), BF16)]
    scratch += [pltpu.VMEM((2, PK_HALF, LANES), F32), pltpu.VMEM((last, PK_HALF, LANES), F32)]
    scratch += [pltpu.SemaphoreType.DMA((2 * nj + 1,)), pltpu.SemaphoreType.DMA((2 * nj + 1,)),
                pltpu.SemaphoreType.DMA((last * (nj + 1),)), pltpu.SemaphoreType.DMA((last * (nj + 1),)),
                pltpu.SemaphoreType.DMA((nj + 1,)), pltpu.SemaphoreType.DMA((nj + 1,)),
                pltpu.SemaphoreType.DMA((nj + 1,)), pltpu.SemaphoreType.DMA((nj + 2,))]
    grid_spec = pltpu.PrefetchScalarGridSpec(
        num_scalar_prefetch=1, grid=(N_CHIPS, n_phase, nk), in_specs=in_specs + [ANY],
        out_specs=[ANY] * (2 * nj + 2), scratch_shapes=scratch)
    out_shape = [jax.ShapeDtypeStruct((d, W_BLK), F32)] + [jax.ShapeDtypeStruct((ROW_BLK, d), F32)] * nsq
    out_shape.append(jax.ShapeDtypeStruct((2, PK_HALF, LANES), F32))
    out_shape += [jax.ShapeDtypeStruct((last, hr, W_BLK), BF16)] + [jax.ShapeDtypeStruct((last, qr, d), BF16)] * nsq
    out_shape.append(jax.ShapeDtypeStruct((last, PK_HALF, LANES), F32))
    flat = [a for pair in squares for a in pair]
    res = pl.pallas_call(body, name="grads_reduce_scatter", grid_spec=grid_spec, out_shape=out_shape,
                         compiler_params=_params())(idx, h, dproj, *flat, small)
    return res[:nj + 1]


_VEC_NAMES = ("norm_g", "conv_b", "lru_b_a", "lru_b_x", "lru_lambda", "final_norm_g")


def _pack_small(p, conv_full=None, scalar=None):
    rows = [p["lru_w_a"].reshape(PK_WX - PK_WA, LANES), p["lru_w_x"].reshape(PK_VEC - PK_WX, LANES)]
    rows += [p[k].reshape(8, LANES) for k in _VEC_NAMES]
    rows.append(jnp.pad(p["attn_sinks"].reshape(1, N_Q_HEADS), ((0, 7), (0, LANES - N_Q_HEADS))))
    rows.append(jnp.zeros((32, LANES), F32) if conv_full is None else conv_full.reshape(32, LANES))
    tail = PK_ROWS - PK_SCALAR
    if scalar is None:
        rows.append(jnp.zeros((tail, LANES), F32))
    else:
        rows.append(jnp.pad(scalar.reshape(1, 1), ((0, tail - 1), (0, LANES - 1))))
    return jnp.concatenate(rows, axis=0)


def _unpack_small(pk, like):
    out = {"lru_w_a": pk[PK_WA:PK_WX].reshape(like["lru_w_a"].shape),
           "lru_w_x": pk[PK_WX:PK_VEC].reshape(like["lru_w_x"].shape)}
    for j, k in enumerate(_VEC_NAMES):
        out[k] = pk[PK_VEC + 8 * j:PK_VEC + 8 * j + 8].reshape(like[k].shape)
    out["attn_sinks"] = pk[PK_SINK:PK_SINK + 1, :N_Q_HEADS].reshape(like["attn_sinks"].shape)
    return out


_WEIGHTS = ("norm_g", "w_in", "conv_w", "conv_b", "lru_w_a", "lru_b_a", "lru_w_x", "lru_b_x", "lru_lambda",
            "attn_sinks", "w_rnn_out", "w_attn_out", "w_o", "final_norm_g")
_SMALL = ("norm_g", "conv_b", "lru_w_a", "lru_b_a", "lru_w_x", "lru_b_x", "lru_lambda", "attn_sinks", "final_norm_g")
_ROW_SHARDED = ("w_rnn_out", "w_attn_out", "w_o")


def kernel(x, norm_g, w_in, conv_w, conv_b, lru_w_a, lru_b_a, lru_w_x, lru_b_x, lru_lambda, attn_sinks, w_rnn_out, w_attn_out, w_o, final_norm_g, loss_target, m_norm_g, m_w_in, m_conv_w, m_conv_b, m_lru_w_a, m_lru_b_a, m_lru_w_x, m_lru_b_x, m_lru_lambda, m_attn_sinks, m_w_rnn_out, m_w_attn_out, m_w_o, m_final_norm_g, v_norm_g, v_w_in, v_conv_w, v_conv_b, v_lru_w_a, v_lru_b_a, v_lru_w_x, v_lru_b_x, v_lru_lambda, v_attn_sinks, v_w_rnn_out, v_w_attn_out, v_w_o, v_final_norm_g):
    w = dict(norm_g=norm_g, w_in=w_in, conv_w=conv_w, conv_b=conv_b, lru_w_a=lru_w_a, lru_b_a=lru_b_a, lru_w_x=lru_w_x,
             lru_b_x=lru_b_x, lru_lambda=lru_lambda, attn_sinks=attn_sinks, w_rnn_out=w_rnn_out, w_attn_out=w_attn_out,
             w_o=w_o, final_norm_g=final_norm_g)
    m = dict(norm_g=m_norm_g, w_in=m_w_in, conv_w=m_conv_w, conv_b=m_conv_b, lru_w_a=m_lru_w_a, lru_b_a=m_lru_b_a,
             lru_w_x=m_lru_w_x, lru_b_x=m_lru_b_x, lru_lambda=m_lru_lambda, attn_sinks=m_attn_sinks,
             w_rnn_out=m_w_rnn_out, w_attn_out=m_w_attn_out, w_o=m_w_o, final_norm_g=m_final_norm_g)
    v = dict(norm_g=v_norm_g, w_in=v_w_in, conv_w=v_conv_w, conv_b=v_conv_b, lru_w_a=v_lru_w_a, lru_b_a=v_lru_b_a,
             lru_w_x=v_lru_w_x, lru_b_x=v_lru_b_x, lru_lambda=v_lru_lambda, attn_sinks=v_attn_sinks,
             w_rnn_out=v_w_rnn_out, w_attn_out=v_w_attn_out, w_o=v_w_o, final_norm_g=v_final_norm_g)
    mx, my, mc = _mesh_pos()
    me = 2 * mx + my
    d = D_MODEL

    slot0 = jnp.stack([me, jnp.zeros_like(me)]).astype(jnp.int32)
    bufs = [_put_slot(w[k][0], N_CHIPS, slot0, w[k].shape[1], BF16, "cast_" + k) for k in ("w_in",) + _ROW_SHARDED]
    bufs.append(_put_slot(w["conv_w"][0], N_CHIPS, slot0, CONV_WIDTH, F32, "slot_conv_w"))
    h = _rmsnorm_fwd(x.reshape(-1, d), w["norm_g"])
    proj, (g_in, g_r, g_a, g_o, g_cw) = _gather_in_proj(h, bufs, [True, True, True, True, False],
                                                        jnp.reshape(me, (1,)).astype(jnp.int32))
    conv_full = g_cw.transpose(1, 0, 2).reshape(CONV_WIDTH, D_RNN)

    loss_local, grad_x, h, dproj, squares, gsmall = _local_grads(
        x, loss_target, h, proj, w["norm_g"], g_in, conv_full, w["conv_b"], w["lru_w_a"][0], w["lru_b_a"], w["lru_w_x"][0],
        w["lru_b_x"], w["lru_lambda"], w["attn_sinks"][0], g_r.reshape(d, d), g_a.reshape(d, d), g_o.reshape(d, d),
        w["final_norm_g"].reshape(1, d))
    gpack = _pack_small(gsmall, gsmall["conv_w"], loss_local).reshape(2, PK_HALF, LANES)
    f_in, f_r, f_a, f_o, spack = _grads_reduce_scatter(h, dproj, squares, gpack, jnp.stack([me, mc]).astype(jnp.int32))
    spack = spack.reshape(PK_ROWS, LANES)
    loss = spack[PK_SCALAR, 0]

    grads = _unpack_small(spack, w)
    conv_all = spack[PK_CONV:PK_CONV + 32].reshape(CONV_WIDTH, D_RNN)
    grads["conv_w"] = lax.dynamic_slice_in_dim(conv_all, me * (D_RNN // N_CHIPS), D_RNN // N_CHIPS, axis=1)[None]
    grads["w_in"] = f_in[None]
    grads["w_rnn_out"], grads["w_attn_out"], grads["w_o"] = f_r[None], f_a[None], f_o[None]

    delta, new_m, new_v = {}, {}, {}
    for k in ("w_in",) + _ROW_SHARDED:
        dk, mk, vk = _adamw(w[k][0], grads[k][0], m[k][0], v[k][0], "adamw_" + k)
        delta[k], new_m[k], new_v[k] = dk[None], mk[None], vk[None]
    shp = (2 * CONV_WIDTH, LANES)
    dk, mk, vk = _adamw(w["conv_w"].reshape(shp), grads["conv_w"].reshape(shp), m["conv_w"].reshape(shp),
                        v["conv_w"].reshape(shp), "adamw_conv_w")
    delta["conv_w"], new_m["conv_w"], new_v["conv_w"] = (a.reshape(w["conv_w"].shape) for a in (dk, mk, vk))
    dk, mk, vk = _adamw(_pack_small(w), spack, _pack_small(m), _pack_small(v), "adamw_small")
    for src, dst in ((dk, delta), (mk, new_m), (vk, new_v)):
        dst.update(_unpack_small(src, w))

    return (loss, grad_x, *[grads[k] for k in _WEIGHTS], *[delta[k] for k in _WEIGHTS],
            *[new_m[k] for k in _WEIGHTS], *[new_v[k] for k in _WEIGHTS])
```

```python
import functools
import math

import jax
import jax.numpy as jnp
from jax import lax
from jax.experimental import pallas as pl
from jax.experimental.pallas import tpu as pltpu

F32 = jnp.float32
BF16 = jnp.bfloat16
MESH = pl.DeviceIdType.MESH

D_MODEL = 1024
D_RNN = 1024
N_RNN_BLOCKS = 8
RNN_BLOCK = D_RNN // N_RNN_BLOCKS
CONV_WIDTH = 4
LRU_C = 8.0
HEAD_DIM = 64
N_Q_HEADS = 16
N_KV_HEADS = 4
D_ATTN = N_Q_HEADS * HEAD_DIM
D_KV = N_KV_HEADS * HEAD_DIM
WINDOW = 128
ROPE_DIM = HEAD_DIM // 4
ROPE_THETA = 500000.0
NORM_EPS = 1e-6
OFF_RNN_X = 0
OFF_RNN_G = OFF_RNN_X + D_RNN
OFF_Q = OFF_RNN_G + D_RNN
OFF_K = OFF_Q + D_ATTN
OFF_V = OFF_K + D_KV
OFF_ATTN_G = OFF_V + D_KV
OFF_MERGE_R = OFF_ATTN_G + D_ATTN
OFF_MERGE_A = OFF_MERGE_R + D_MODEL
D_IN = OFF_MERGE_A + D_MODEL

ADAM_LR = 0.001
ADAM_B1 = 0.9
ADAM_B2 = 0.999
ADAM_EPS = 1e-08
ADAM_WD = 0.01
ADAM_STEP = 10

N_CHIPS = 4
W_BLK = D_IN // N_CHIPS
ROW_BLK = D_MODEL // N_CHIPS
LANES = 128
ATT_BLK = 128
VMEM_LIMIT = 56 * 1024 * 1024
NEG_BIG = -1e30
ATTN_SCALE = 1.0 / math.sqrt(HEAD_DIM)

PK_WA = 0
PK_WX = PK_WA + N_RNN_BLOCKS * RNN_BLOCK
PK_VEC = PK_WX + N_RNN_BLOCKS * RNN_BLOCK
PK_SINK = PK_VEC + 6 * 8
PK_CONV = PK_SINK + 8
PK_SCALAR = PK_CONV + 32
PK_ROWS = PK_SCALAR + 8
PK_HALF = PK_ROWS // 2


def _params(**kw):
    return pltpu.CompilerParams(vmem_limit_bytes=VMEM_LIMIT, **kw)


def _sigmoid(z):
    return 1.0 / (1.0 + jnp.exp(-z))


def _dot(a, b):
    return jnp.dot(a, b, preferred_element_type=F32)


def _dot_nt(a, b):
    return lax.dot_general(a, b, (((1,), (1,)), ((), ())), preferred_element_type=F32)


def _dot_tn(a, b):
    return lax.dot_general(a, b, (((0,), (0,)), ((), ())), preferred_element_type=F32)


def _put_slot(src, n_slots, slot_and_blk, rows, dtype, name):
    _, c = src.shape
    tr = _row_tile(rows, c * 4)
    steps = rows // tr

    def body(idx_ref, s_ref, o_ref):
        o_ref[...] = s_ref[...].astype(dtype)

    grid_spec = pltpu.PrefetchScalarGridSpec(
        num_scalar_prefetch=1, grid=(steps,),
        in_specs=[pl.BlockSpec((tr, c), lambda i, idx_ref: (idx_ref[1] * steps + i, 0))],
        out_specs=pl.BlockSpec((None, tr, c), lambda i, idx_ref: (idx_ref[0], i, 0)))
    return pl.pallas_call(body, name=name, grid_spec=grid_spec,
                          out_shape=jax.ShapeDtypeStruct((n_slots, rows, c), dtype),
                          compiler_params=_params())(slot_and_blk, src)


def _rmsnorm_fwd(x, g):
    t, d = x.shape
    tm = min(t, 512)

    def body(x_ref, g_ref, o_ref):
        xv = x_ref[...]
        r = lax.rsqrt(jnp.mean(xv * xv, axis=-1, keepdims=True) + NORM_EPS)
        o_ref[...] = (xv * r * g_ref[...]).astype(BF16)

    return pl.pallas_call(
        body, name="rmsnorm_fwd", grid=(t // tm,), out_shape=jax.ShapeDtypeStruct((t, d), BF16),
        in_specs=[pl.BlockSpec((tm, d), lambda i: (i, 0)), pl.BlockSpec((1, d), lambda i: (0, 0))],
        out_specs=pl.BlockSpec((tm, d), lambda i: (i, 0)), compiler_params=_params())(x, g)


def _rmsnorm_bwd(x, dh, dx2, g):
    t, d = x.shape
    tm = min(t, 512)

    def body(x_ref, dh_ref, dx2_ref, g_ref, gx_ref, dg_ref):
        i = pl.program_id(0)
        xv = x_ref[...]
        dhv = dh_ref[...]
        r = lax.rsqrt(jnp.mean(xv * xv, axis=-1, keepdims=True) + NORM_EPS)
        nrm = xv * r
        dn = dhv * g_ref[...]
        gx_ref[...] = dx2_ref[...] + r * (dn - nrm * jnp.mean(dn * nrm, axis=-1, keepdims=True))

        @pl.when(i == 0)
        def _():
            dg_ref[...] = jnp.zeros_like(dg_ref)

        dg_ref[...] += jnp.sum(dhv * nrm, axis=0, keepdims=True)

    return pl.pallas_call(
        body, name="rmsnorm_bwd", grid=(t // tm,),
        out_shape=(jax.ShapeDtypeStruct((t, d), F32), jax.ShapeDtypeStruct((1, d), F32)),
        in_specs=[pl.BlockSpec((tm, d), lambda i: (i, 0)), pl.BlockSpec((tm, d), lambda i: (i, 0)),
                  pl.BlockSpec((tm, d), lambda i: (i, 0)), pl.BlockSpec((1, d), lambda i: (0, 0))],
        out_specs=(pl.BlockSpec((tm, d), lambda i: (i, 0)), pl.BlockSpec((1, d), lambda i: (0, 0))),
        compiler_params=_params())(x, dh, dx2, g)


def _adamw(w, g, m, v, name):
    r, c = w.shape
    tr = _row_tile(r, c * 4, 1024 * 1024)
    c1 = 1.0 - ADAM_B1 ** ADAM_STEP
    c2 = 1.0 - ADAM_B2 ** ADAM_STEP

    def body(w_ref, g_ref, m_ref, v_ref, d_ref, nm_ref, nv_ref):
        gv = g_ref[...]
        nm = ADAM_B1 * m_ref[...] + (1.0 - ADAM_B1) * gv
        nv = ADAM_B2 * v_ref[...] + (1.0 - ADAM_B2) * (gv * gv)
        m_hat = nm / c1
        v_hat = nv / c2
        d_ref[...] = -ADAM_LR * (m_hat / (jnp.sqrt(v_hat) + ADAM_EPS) + ADAM_WD * w_ref[...])
        nm_ref[...] = nm
        nv_ref[...] = nv

    spec = pl.BlockSpec((tr, c), lambda i: (i, 0))
    sds = jax.ShapeDtypeStruct((r, c), F32)
    return pl.pallas_call(
        body, name=name, grid=(r // tr,), out_shape=(sds, sds, sds),
        in_specs=[spec, spec, spec, spec], out_specs=(spec, spec, spec), compiler_params=_params())(w, g, m, v)


def _grad_h(dproj, w_bm):
    t = dproj.shape[0]
    nb, d, wb = w_bm.shape
    tm = min(t, 1024)

    def body(dp_ref, w_ref, o_ref, acc_ref):
        k = pl.program_id(1)

        @pl.when(k == 0)
        def _():
            acc_ref[...] = jnp.zeros_like(acc_ref)

        acc_ref[...] += _dot_nt(dp_ref[...], w_ref[...])

        @pl.when(k == nb - 1)
        def _():
            o_ref[...] = acc_ref[...]

    return pl.pallas_call(
        body, name="grad_h", grid=(t // tm, nb), out_shape=jax.ShapeDtypeStruct((t, d), F32),
        in_specs=[pl.BlockSpec((tm, wb), lambda i, k: (i, k)), pl.BlockSpec((None, d, wb), lambda i, k: (k, 0, 0))],
        out_specs=pl.BlockSpec((tm, d), lambda i, k: (i, 0)),
        scratch_shapes=[pltpu.VMEM((tm, d), F32)], compiler_params=_params())(dproj, w_bm)


def _shift_down(v, d, fill):
    n = v.shape[0]
    if d % 8 == 0:
        return jnp.concatenate([jnp.full((d,) + v.shape[1:], fill, v.dtype), v[: n - d]], axis=0)
    row = lax.broadcasted_iota(jnp.int32, v.shape, 0)
    return jnp.where(row >= d, pltpu.roll(v, d, axis=0), fill)


def _shift_up(v, d, fill):
    n = v.shape[0]
    if d % 8 == 0:
        return jnp.concatenate([v[d:], jnp.full((d,) + v.shape[1:], fill, v.dtype)], axis=0)
    row = lax.broadcasted_iota(jnp.int32, v.shape, 0)
    return jnp.where(row < n - d, pltpu.roll(v, n - d, axis=0), fill)


def _scan(a, b, shift):
    n = a.shape[0]
    d = 1
    while d < n:
        b = a * shift(b, d, 0.0) + b
        if 2 * d < n:
            a = a * shift(a, d, 1.0)
        d *= 2
    return b


def _neg_expm1(y):
    series = -y * (1.0 + y * (1.0 / 2.0) * (1.0 + y * (1.0 / 3.0) * (1.0 + y * (1.0 / 4.0) * (
        1.0 + y * (1.0 / 5.0) * (1.0 + y * (1.0 / 6.0) * (1.0 + y * (1.0 / 7.0)))))))
    return jnp.where(y > -0.25, series, 1.0 - jnp.exp(y))


def _softplus(z):
    e = jnp.exp(-jnp.abs(z))
    w = 1.0 + e
    log1p = jnp.where(w == 1.0, e, jnp.log(w) * (e / jnp.where(w == 1.0, 1.0, w - 1.0)))
    return jnp.maximum(z, 0.0) + log1p


def _conv(up, cw, cb):
    out = cb + cw[CONV_WIDTH - 1:CONV_WIDTH, :] * up
    for j in range(CONV_WIDTH - 1):
        out = out + cw[j:j + 1, :] * _shift_down(up, CONV_WIDTH - 1 - j, 0.0)
    return out


def _lru_gates(u, wa_ref, ba_ref, wx_ref, bx_ref, lam_ref):
    ub = u.astype(BF16)
    r = _sigmoid(_dot(ub, wa_ref[...].astype(BF16)) + ba_ref[...])
    i = _sigmoid(_dot(ub, wx_ref[...].astype(BF16)) + bx_ref[...])
    sp = _softplus(-lam_ref[...])
    log_a = (-LRU_C) * r * sp
    a = jnp.exp(log_a)
    mult = jnp.sqrt(_neg_expm1(2.0 * log_a))
    return r, i, sp, a, mult


def _lru_specs(s):
    cb = RNN_BLOCK
    vec = pl.BlockSpec((1, cb), lambda n, b: (0, n))
    return dict(
        up=pl.BlockSpec((None, s, cb), lambda n, b: (b, 0, OFF_RNN_X // cb + n)),
        gr=pl.BlockSpec((None, s, cb), lambda n, b: (b, 0, OFF_RNN_G // cb + n)),
        act=pl.BlockSpec((None, s, cb), lambda n, b: (b, 0, n)),
        cw=pl.BlockSpec((CONV_WIDTH, cb), lambda n, b: (0, n)),
        vec=vec,
        wblk=pl.BlockSpec((None, cb, cb), lambda n, b: (n, 0, 0)),
    )


def _lru_fwd(proj3, cw, cb, wa, ba, wx, bx, lam):
    bsz, s, _ = proj3.shape
    sp = _lru_specs(s)

    def body(up_ref, gr_ref, cw_ref, cb_ref, wa_ref, ba_ref, wx_ref, bx_ref, lam_ref, h_ref, y_ref):
        u = _conv(up_ref[...], cw_ref[...], cb_ref[...])
        _, i, _, a, mult = _lru_gates(u, wa_ref, ba_ref, wx_ref, bx_ref, lam_ref)
        h = _scan(a, mult * (i * u), _shift_down)
        h_ref[...] = h
        g = gr_ref[...]
        y_ref[...] = (h * (g * _sigmoid(g))).astype(BF16)

    return pl.pallas_call(
        body, name="lru_fwd", grid=(N_RNN_BLOCKS, bsz),
        out_shape=(jax.ShapeDtypeStruct((bsz, s, D_RNN), F32), jax.ShapeDtypeStruct((bsz, s, D_RNN), BF16)),
        in_specs=[sp["up"], sp["gr"], sp["cw"], sp["vec"], sp["wblk"], sp["vec"], sp["wblk"], sp["vec"], sp["vec"]],
        out_specs=(sp["act"], sp["act"]), compiler_params=_params())(proj3, proj3, cw, cb, wa, ba, wx, bx, lam)


def _lru_bwd(proj3, h3, dy3, cw, cb, wa, ba, wx, bx, lam):
    bsz, s, _ = proj3.shape
    sp = _lru_specs(s)

    def body(up_ref, gr_ref, h_ref, dy_ref, cw_ref, cb_ref, wa_ref, ba_ref, wx_ref, bx_ref, lam_ref,
             dup_ref, dgr_ref, dcw_ref, dcb_ref, dwa_ref, dba_ref, dwx_ref, dbx_ref, dlam_ref):
        b = pl.program_id(1)
        up = up_ref[...]
        cwv = cw_ref[...]
        u = _conv(up, cwv, cb_ref[...])
        r, i, spv, a, mult = _lru_gates(u, wa_ref, ba_ref, wx_ref, bx_ref, lam_ref)
        h = h_ref[...]
        g = gr_ref[...]
        dy = dy_ref[...]
        sg = _sigmoid(g)
        dgr_ref[...] = (dy * h * (sg * (1.0 + g * (1.0 - sg)))).astype(BF16)
        dh = dy * (g * sg)
        adj = _scan(_shift_up(a, 1, 0.0), dh, _shift_up)
        da = adj * _shift_down(h, 1, 0.0)
        dmult = adj * (i * u)
        di = adj * mult * u
        du = adj * mult * i
        dla = da * a - dmult * (a * a) / mult
        dr = dla * ((-LRU_C) * spv)
        dsp = jnp.sum(dla * ((-LRU_C) * r), axis=0, keepdims=True)
        dza = dr * r * (1.0 - r)
        dzx = di * i * (1.0 - i)
        ub = u.astype(BF16)
        dzab = dza.astype(BF16)
        dzxb = dzx.astype(BF16)
        du = du + _dot_nt(dzab, wa_ref[...].astype(BF16)) + _dot_nt(dzxb, wx_ref[...].astype(BF16))
        dup = cwv[CONV_WIDTH - 1:CONV_WIDTH, :] * du
        for j in range(CONV_WIDTH - 1):
            dup = dup + cwv[j:j + 1, :] * _shift_up(du, CONV_WIDTH - 1 - j, 0.0)
        dup_ref[...] = dup.astype(BF16)

        @pl.when(b == 0)
        def _():
            for ref in (dcw_ref, dcb_ref, dwa_ref, dba_ref, dwx_ref, dbx_ref, dlam_ref):
                ref[...] = jnp.zeros_like(ref)

        rows = [jnp.sum(du * _shift_down(up, CONV_WIDTH - 1 - j, 0.0), axis=0, keepdims=True)
                for j in range(CONV_WIDTH - 1)]
        rows.append(jnp.sum(du * up, axis=0, keepdims=True))
        dcw_ref[...] += jnp.concatenate(rows, axis=0)
        dcb_ref[...] += jnp.sum(du, axis=0, keepdims=True)
        dwa_ref[...] += _dot_tn(ub, dzab)
        dba_ref[...] += jnp.sum(dza, axis=0, keepdims=True)
        dwx_ref[...] += _dot_tn(ub, dzxb)
        dbx_ref[...] += jnp.sum(dzx, axis=0, keepdims=True)
        dlam_ref[...] += dsp * (-_sigmoid(-lam_ref[...]))

    act_b = jax.ShapeDtypeStruct((bsz, s, D_RNN), BF16)
    vec = jax.ShapeDtypeStruct((1, D_RNN), F32)
    wsd = jax.ShapeDtypeStruct((N_RNN_BLOCKS, RNN_BLOCK, RNN_BLOCK), F32)
    return pl.pallas_call(
        body, name="lru_bwd", grid=(N_RNN_BLOCKS, bsz),
        out_shape=(act_b, act_b, jax.ShapeDtypeStruct((CONV_WIDTH, D_RNN), F32), vec, wsd, vec, wsd, vec, vec),
        in_specs=[sp["up"], sp["gr"], sp["act"], sp["act"], sp["cw"], sp["vec"], sp["wblk"], sp["vec"],
                  sp["wblk"], sp["vec"], sp["vec"]],
        out_specs=(sp["act"], sp["act"], sp["cw"], sp["vec"], sp["wblk"], sp["vec"], sp["wblk"], sp["vec"], sp["vec"]),
        compiler_params=_params())(proj3, proj3, h3, dy3, cw, cb, wa, ba, wx, bx, lam)


def _rope_tables(s):
    half = ROPE_DIM // 2
    pos = jnp.arange(s, dtype=F32)
    inv_freq = ROPE_THETA ** (-jnp.arange(0, ROPE_DIM, 2, dtype=F32) / ROPE_DIM)
    ang = pos[:, None] * inv_freq[None, :]
    cos, sin = jnp.cos(ang), jnp.sin(ang)
    rest = HEAD_DIM - ROPE_DIM
    cos64 = jnp.concatenate([cos, cos, jnp.ones((s, rest), F32)], axis=1)
    sin64 = jnp.concatenate([-sin, sin, jnp.zeros((s, rest), F32)], axis=1)
    assert half * 2 == ROPE_DIM
    return jnp.tile(cos64, (1, LANES // HEAD_DIM)), jnp.tile(sin64, (1, LANES // HEAD_DIM))


def _swap_rot_halves(v):
    half = ROPE_DIM // 2
    lane = lax.broadcasted_iota(jnp.int32, v.shape, 1) % HEAD_DIM
    second = jnp.where(lane < ROPE_DIM, pltpu.roll(v, half, axis=1), 0.0)
    return jnp.where(lane < half, pltpu.roll(v, LANES - half, axis=1), second)


def _rope(v, cos, sin):
    tiles = []
    for t in range(v.shape[1] // LANES):
        vt = v[:, t * LANES:(t + 1) * LANES]
        tiles.append(vt * cos + _swap_rot_halves(vt) * sin)
    return tiles[0] if len(tiles) == 1 else jnp.concatenate(tiles, axis=1)


def _unrope(v, cos, sin):
    tiles = []
    for t in range(v.shape[1] // LANES):
        vt = v[:, t * LANES:(t + 1) * LANES]
        tiles.append(vt * cos + _swap_rot_halves(vt * sin))
    return tiles[0] if len(tiles) == 1 else jnp.concatenate(tiles, axis=1)


HEADS_PER_STEP = 8
QW = HEADS_PER_STEP * HEAD_DIM
N_PAIRS = N_Q_HEADS // HEADS_PER_STEP
Q_PER_KV = N_Q_HEADS // N_KV_HEADS
KV_PER_STEP = HEADS_PER_STEP // Q_PER_KV


QT_COLS = Q_PER_KV * ATT_BLK


def _attn_scratch(s, with_vt):
    nb = s // ATT_BLK
    pad = s + ATT_BLK
    shapes = [pltpu.VMEM((nb, LANES, QT_COLS), BF16),
              pltpu.VMEM((KV_PER_STEP, pad, LANES), BF16),
              pltpu.VMEM((KV_PER_STEP, pad, LANES), BF16)]
    if with_vt:
        shapes.append(pltpu.VMEM((LANES, pad), BF16))
    return shapes


def _attn_specs(s, order):
    def mk(width, base, **kw):
        if order == "bp":
            return pl.BlockSpec((None, s, width), lambda b, p: (b, 0, base + p), **kw)
        return pl.BlockSpec((None, s, width), lambda p, b: (b, 0, base + p), **kw)
    one = dict(pipeline_mode=pl.Buffered(1))
    tbl = pl.BlockSpec((s, LANES), lambda *_: (0, 0))
    return dict(q=mk(QW, OFF_Q // QW), k=mk(LANES, OFF_K // LANES), v=mk(LANES, OFF_V // LANES),
                g=mk(QW, OFF_ATTN_G // QW), act=mk(QW, 0), kv=mk(LANES, 0), tbl=tbl,
                q1=mk(QW, OFF_Q // QW, **one), g1=mk(QW, OFF_ATTN_G // QW, **one), act1=mk(QW, 0, **one),
                smem=pl.BlockSpec(memory_space=pltpu.SMEM))


def _to_qt(blk):
    rows = []
    for j in range(KV_PER_STEP):
        cols = []
        for tt in range(2):
            t = 2 * j + tt
            tr = blk[:, t * LANES:(t + 1) * LANES].T
            cols += [tr[0:HEAD_DIM, :], tr[HEAD_DIM:, :]]
        rows.append(jnp.concatenate(cols, axis=1))
    return jnp.concatenate(rows, axis=0)


def _from_qt(xt):
    tiles = []
    for j in range(KV_PER_STEP):
        for tt in range(2):
            g0 = 2 * tt
            pair = jnp.concatenate([xt[j * HEAD_DIM:(j + 1) * HEAD_DIM, (g0 + i) * ATT_BLK:(g0 + i + 1) * ATT_BLK]
                                    for i in range(2)], axis=0)
            tiles.append(pair.T)
    return jnp.concatenate(tiles, axis=1)


def _attn_prep(q_ref, k_ref, v_ref, cos_ref, sin_ref, qt_sc, km_sc, vm_sc, t_sc, transposed, nb):
    zeros = jnp.zeros((ATT_BLK, LANES), BF16)
    for j in range(KV_PER_STEP):
        km_sc[j, 0:ATT_BLK, :] = zeros
        vm_sc[j, 0:ATT_BLK, :] = zeros
    t_sc[:, 0:ATT_BLK] = zeros
    head_of_lane = lax.broadcasted_iota(jnp.int32, (ATT_BLK, LANES), 1) // HEAD_DIM

    def prep(n, carry):
        r0 = pl.multiple_of(n * ATT_BLK, ATT_BLK)
        cs = cos_ref[pl.ds(r0, ATT_BLK), :]
        sn = sin_ref[pl.ds(r0, ATT_BLK), :]
        qt_sc[n] = _to_qt(_rope(q_ref[pl.ds(r0, ATT_BLK), :], cs, sn) * ATTN_SCALE).astype(BF16)
        k = _rope(k_ref[pl.ds(r0, ATT_BLK), :], cs, sn)
        v = v_ref[pl.ds(r0, ATT_BLK), :]
        for j in range(KV_PER_STEP):
            km_sc[j, pl.ds(r0 + ATT_BLK, ATT_BLK), :] = jnp.where(head_of_lane == j, k, 0.0).astype(BF16)
            vm_sc[j, pl.ds(r0 + ATT_BLK, ATT_BLK), :] = jnp.where(head_of_lane == j, v, 0.0).astype(BF16)
        t_sc[:, pl.ds(r0 + ATT_BLK, ATT_BLK)] = (k if transposed == "k" else v).T.astype(BF16)
        return carry

    lax.fori_loop(0, nb, prep, 0)


def _band_mask_t(n):
    shape = (2 * ATT_BLK, QT_COLS)
    key = lax.broadcasted_iota(jnp.int32, shape, 0)
    qry = lax.broadcasted_iota(jnp.int32, shape, 1) % ATT_BLK
    lo = jnp.where(n == 0, ATT_BLK, 0)
    return (key > qry) & (key <= qry + WINDOW) & (key >= lo)


def _sink_row(sink_ref, first):
    return jnp.concatenate([jnp.full((1, ATT_BLK), sink_ref[first + g], F32) for g in range(Q_PER_KV)], axis=1)


def _softmax_cols(scores_t, valid, sink):
    sc = jnp.where(valid, scores_t, NEG_BIG)
    m = jnp.maximum(jnp.max(sc, axis=0, keepdims=True), sink)
    e = jnp.exp(sc - m)
    es = jnp.exp(sink - m)
    inv = 1.0 / (jnp.sum(e, axis=0, keepdims=True) + es)
    return e * inv, es * inv


def _attn_fwd(proj3, sinks, cosf, sinf):
    bsz, s, _ = proj3.shape
    nb = s // ATT_BLK
    sp = _attn_specs(s, "bp")

    def body(sink_ref, q_ref, k_ref, v_ref, g_ref, cos_ref, sin_ref, o_ref, y_ref, qt_sc, km_sc, vm_sc, vt_sc):
        p = pl.program_id(1)
        _attn_prep(q_ref, k_ref, v_ref, cos_ref, sin_ref, qt_sc, km_sc, vm_sc, vt_sc, "v", nb)
        kv_row = lax.broadcasted_iota(jnp.int32, (LANES, QT_COLS), 0) // HEAD_DIM

        def blk(n, carry):
            r0 = pl.multiple_of(n * ATT_BLK, ATT_BLK)
            valid = _band_mask_t(n)
            rq = qt_sc[n]
            vt = vt_sc[:, pl.ds(r0, 2 * ATT_BLK)]
            ots = []
            for j in range(KV_PER_STEP):
                st = _dot(km_sc[j, pl.ds(r0, 2 * ATT_BLK), :], rq)
                pt, _ = _softmax_cols(st, valid, _sink_row(sink_ref, p * HEADS_PER_STEP + j * Q_PER_KV))
                ots.append(_dot(vt, pt.astype(BF16)))
            o = _from_qt(jnp.where(kv_row == 0, ots[0], ots[1]))
            o_ref[pl.ds(r0, ATT_BLK), :] = o
            g = g_ref[pl.ds(r0, ATT_BLK), :]
            y_ref[pl.ds(r0, ATT_BLK), :] = (o * (g * _sigmoid(g))).astype(BF16)
            return carry

        lax.fori_loop(0, nb, blk, 0)

    return pl.pallas_call(
        body, name="attn_fwd", grid=(bsz, N_PAIRS),
        out_shape=(jax.ShapeDtypeStruct((bsz, s, D_ATTN), F32), jax.ShapeDtypeStruct((bsz, s, D_ATTN), BF16)),
        in_specs=[sp["smem"], sp["q"], sp["k"], sp["v"], sp["g"], sp["tbl"], sp["tbl"]],
        out_specs=(sp["act"], sp["act"]),
        scratch_shapes=_attn_scratch(s, True),
        compiler_params=_params())(sinks, proj3, proj3, proj3, proj3, cosf, sinf)


def _attn_bwd(proj3, o3, dy3, sinks, cosf, sinf):
    bsz, s, _ = proj3.shape
    nb = s // ATT_BLK
    sp = _attn_specs(s, "pb")

    def body(sink_ref, q_ref, k_ref, v_ref, g_ref, o_ref, dy_ref, cos_ref, sin_ref,
             dq_ref, dk_ref, dv_ref, dg_ref, ds_ref, qt_sc, km_sc, vm_sc, kt_sc, dot_sc, dqt_sc, dk_sc, dv_sc):
        p = pl.program_id(0)
        b = pl.program_id(1)
        _attn_prep(q_ref, k_ref, v_ref, cos_ref, sin_ref, qt_sc, km_sc, vm_sc, kt_sc, "k", nb)
        dk_sc[...] = jnp.zeros_like(dk_sc)
        dv_sc[...] = jnp.zeros_like(dv_sc)

        def gate(n, carry):
            r0 = pl.multiple_of(n * ATT_BLK, ATT_BLK)
            g = g_ref[pl.ds(r0, ATT_BLK), :]
            dy = dy_ref[pl.ds(r0, ATT_BLK), :]
            sg = _sigmoid(g)
            dg_ref[pl.ds(r0, ATT_BLK), :] = (dy * o_ref[pl.ds(r0, ATT_BLK), :] * (sg * (1.0 + g * (1.0 - sg)))).astype(BF16)
            dot_sc[n] = _to_qt(dy * (g * sg)).astype(BF16)
            return carry

        lax.fori_loop(0, nb, gate, 0)
        kv_lane = lax.broadcasted_iota(jnp.int32, (2 * ATT_BLK, LANES), 1) // HEAD_DIM
        kv_row = lax.broadcasted_iota(jnp.int32, (LANES, QT_COLS), 0) // HEAD_DIM

        def blk(n, acc):
            r0 = pl.multiple_of(n * ATT_BLK, ATT_BLK)
            valid = _band_mask_t(n)
            rq = qt_sc[n]
            rd = dot_sc[n]
            kt = kt_sc[:, pl.ds(r0, 2 * ATT_BLK)]
            dvs, dks, dqs, new_acc = [], [], [], []
            for j in range(KV_PER_STEP):
                st = _dot(km_sc[j, pl.ds(r0, 2 * ATT_BLK), :], rq)
                pt, ps = _softmax_cols(st, valid, _sink_row(sink_ref, p * HEADS_PER_STEP + j * Q_PER_KV))
                dpt = _dot(vm_sc[j, pl.ds(r0, 2 * ATT_BLK), :], rd)
                delta = jnp.sum(pt * dpt, axis=0, keepdims=True)
                dst = (pt * (dpt - delta)).astype(BF16)
                new_acc.append(acc[j] + ps * delta)
                dvs.append(_dot_nt(pt.astype(BF16), rd))
                dks.append(_dot_nt(dst, rq))
                dqs.append(_dot(kt, dst))
            dv_sc[pl.ds(r0, 2 * ATT_BLK), :] += jnp.where(kv_lane == 0, dvs[0], dvs[1])
            dk_sc[pl.ds(r0, 2 * ATT_BLK), :] += jnp.where(kv_lane == 0, dks[0], dks[1])
            dqt_sc[n] = jnp.where(kv_row == 0, dqs[0], dqs[1]) * ATTN_SCALE
            return tuple(new_acc)

        acc = lax.fori_loop(0, nb, blk, tuple(jnp.zeros((1, QT_COLS), F32) for _ in range(KV_PER_STEP)))
        lane1 = lax.broadcasted_iota(jnp.int32, (1, LANES), 1)
        dsink = jnp.zeros((1, LANES), F32)
        for j in range(KV_PER_STEP):
            for i in range(Q_PER_KV):
                part = jnp.sum(acc[j][:, i * ATT_BLK:(i + 1) * ATT_BLK], axis=1, keepdims=True)
                dsink = dsink - jnp.where(lane1 == j * Q_PER_KV + i, part, 0.0)

        @pl.when(b == 0)
        def _():
            ds_ref[...] = jnp.zeros_like(ds_ref)

        ds_ref[...] += dsink

        def post(n, carry):
            r0 = pl.multiple_of(n * ATT_BLK, ATT_BLK)
            cs = cos_ref[pl.ds(r0, ATT_BLK), :]
            sn = sin_ref[pl.ds(r0, ATT_BLK), :]
            dq_ref[pl.ds(r0, ATT_BLK), :] = _unrope(_from_qt(dqt_sc[n]), cs, sn).astype(BF16)
            dk_ref[pl.ds(r0, ATT_BLK), :] = _unrope(dk_sc[pl.ds(r0 + ATT_BLK, ATT_BLK), :], cs, sn).astype(BF16)
            dv_ref[pl.ds(r0, ATT_BLK), :] = dv_sc[pl.ds(r0 + ATT_BLK, ATT_BLK), :].astype(BF16)
            return carry

        lax.fori_loop(0, nb, post, 0)

    act = jax.ShapeDtypeStruct((bsz, s, D_ATTN), BF16)
    kvs = jax.ShapeDtypeStruct((bsz, s, D_KV), BF16)
    return pl.pallas_call(
        body, name="attn_bwd", grid=(N_PAIRS, bsz),
        out_shape=(act, kvs, kvs, act, jax.ShapeDtypeStruct((N_PAIRS, 1, LANES), F32)),
        in_specs=[sp["smem"], sp["q1"], sp["k"], sp["v"], sp["g1"], sp["act1"], sp["act1"], sp["tbl"], sp["tbl"]],
        out_specs=(sp["act"], sp["kv"], sp["kv"], sp["act"], pl.BlockSpec((None, 1, LANES), lambda p, b: (p, 0, 0))),
        scratch_shapes=_attn_scratch(s, True) + [pltpu.VMEM((nb, LANES, QT_COLS), BF16),
                                                 pltpu.VMEM((nb, LANES, QT_COLS), F32),
                                                 pltpu.VMEM((s + ATT_BLK, LANES), F32),
                                                 pltpu.VMEM((s + ATT_BLK, LANES), F32)],
        compiler_params=_params())(sinks, proj3, proj3, proj3, proj3, o3, dy3, cosf, sinf)


def _merge_fwd_bwd(x, tgt, y_rnn, y_attn, proj, w_r, w_a, w_o, gf):
    t, d = x.shape
    tm = min(t, 256)

    hw = d // 2

    def body(x_ref, t_ref, yr_ref, ya_ref, mr0_ref, mr1_ref, ma0_ref, ma1_ref, wr_ref, wa_ref, wo_ref, gf_ref,
             dmg_ref, dyr_ref, dya_ref, mg_ref, dx2_ref, dx2b_ref, dpr_ref, dpa_ref, loss_ref, dgf_ref):
        i = pl.program_id(0)
        wr = wr_ref[...]
        wa = wa_ref[...]
        wo = wo_ref[...]
        gfv = gf_ref[...]
        pr = _dot(yr_ref[...], wr)
        pa = _dot(ya_ref[...], wa)
        sr = _sigmoid(jnp.concatenate([mr0_ref[...], mr1_ref[...]], axis=1))
        sa = _sigmoid(jnp.concatenate([ma0_ref[...], ma1_ref[...]], axis=1))
        mb = (sr * pr + sa * pa).astype(BF16)
        mg_ref[...] = mb
        x2 = x_ref[...] + _dot(mb, wo)
        r2 = lax.rsqrt(jnp.mean(x2 * x2, axis=-1, keepdims=True) + NORM_EPS)
        nrm = x2 * r2
        err = nrm * gfv - t_ref[...]
        dy = err * (1.0 / d)
        dn = dy * gfv
        dx2 = r2 * (dn - nrm * jnp.mean(dn * nrm, axis=-1, keepdims=True))
        dx2_ref[...] = dx2
        dx2b = dx2.astype(BF16)
        dx2b_ref[...] = dx2b
        dmerged = _dot_nt(dx2b, wo)
        dpr = (dmerged * sr).astype(BF16)
        dpa = (dmerged * sa).astype(BF16)
        dpr_ref[...] = dpr
        dpa_ref[...] = dpa
        dmg_ref[:, 0:d] = (dmerged * pr * (sr * (1.0 - sr))).astype(BF16)
        dmg_ref[:, d:2 * d] = (dmerged * pa * (sa * (1.0 - sa))).astype(BF16)
        dyr_ref[...] = _dot_nt(dpr, wr)
        dya_ref[...] = _dot_nt(dpa, wa)

        @pl.when(i == 0)
        def _():
            loss_ref[...] = jnp.zeros_like(loss_ref)
            dgf_ref[...] = jnp.zeros_like(dgf_ref)

        loss_ref[...] += jnp.full((1, LANES), 0.5 / d, F32) * jnp.sum(err * err)
        dgf_ref[...] += jnp.sum(dy * nrm, axis=0, keepdims=True)

    tile = pl.BlockSpec((tm, d), lambda i: (i, 0))
    wsp = pl.BlockSpec((d, d), lambda i: (0, 0))

    def gate(col_blk):
        return pl.BlockSpec((tm, hw), lambda i: (i, col_blk))

    fb = jax.ShapeDtypeStruct((t, d), BF16)
    ff = jax.ShapeDtypeStruct((t, d), F32)
    return pl.pallas_call(
        body, name="merge_fwd_bwd", grid=(t // tm,),
        out_shape=(jax.ShapeDtypeStruct((t, 2 * d), BF16), ff, ff, fb, ff, fb, fb, fb,
                   jax.ShapeDtypeStruct((1, LANES), F32), jax.ShapeDtypeStruct((1, d), F32)),
        in_specs=[tile, tile, tile, tile] + [gate(OFF_MERGE_R // hw + j) for j in range(4)] + [
            wsp, wsp, wsp, pl.BlockSpec((1, d), lambda i: (0, 0))],
        out_specs=(pl.BlockSpec((tm, 2 * d), lambda i: (i, 0)), tile, tile, tile, tile, tile, tile, tile,
                   pl.BlockSpec((1, LANES), lambda i: (0, 0)), pl.BlockSpec((1, d), lambda i: (0, 0))),
        compiler_params=_params())(x, tgt, y_rnn, y_attn, proj, proj, proj, proj, w_r, w_a, w_o, gf)


def _local_grads(x, tgt, h, proj, norm_g, w_in_bm, conv_w, conv_b, lru_w_a, lru_b_a, lru_w_x, lru_b_x, lam, sinks,
                 w_r, w_a, w_o, gf):
    bsz, s, d = x.shape
    t = bsz * s
    x2 = x.reshape(t, d)
    proj3 = proj.reshape(bsz, s, D_IN)
    h_lru, y_rnn = _lru_fwd(proj3, conv_w, conv_b, lru_w_a, lru_b_a, lru_w_x, lru_b_x, lam)
    cosf, sinf = _rope_tables(s)
    o_attn, y_attn = _attn_fwd(proj3, sinks, cosf, sinf)
    y_rnn2 = y_rnn.reshape(t, d)
    y_attn2 = y_attn.reshape(t, d)
    dmg, dyr, dya, merged, dx2, dx2b, dpr, dpa, loss, dgf = _merge_fwd_bwd(
        x2, tgt.reshape(t, d), y_rnn2, y_attn2, proj, w_r, w_a, w_o, gf)
    dup, dgr, dcw, dcb, dwa, dba, dwx, dbx, dlam = _lru_bwd(
        proj3, h_lru, dyr.reshape(bsz, s, d), conv_w, conv_b, lru_w_a, lru_b_a, lru_w_x, lru_b_x, lam)
    dq, dk, dv, dga, dsink = _attn_bwd(proj3, o_attn, dya.reshape(bsz, s, d), sinks, cosf, sinf)
    dproj = jnp.concatenate([dup, dgr, dq, dk, dv, dga, dmg.reshape(bsz, s, 2 * d)], axis=-1).reshape(t, D_IN)
    dh = _grad_h(dproj, w_in_bm)
    grad_x, dng = _rmsnorm_bwd(x2, dh, dx2, norm_g)
    small = dict(norm_g=dng, conv_w=dcw, conv_b=dcb, lru_w_a=dwa, lru_b_a=dba, lru_w_x=dwx, lru_b_x=dbx,
                 lru_lambda=dlam, attn_sinks=dsink[:, 0, :HEADS_PER_STEP].reshape(1, N_Q_HEADS), final_norm_g=dgf)
    squares = [(y_rnn2, dpr), (y_attn2, dpa), (merged, dx2b)]
    return loss[0, 0], grad_x.reshape(bsz, s, d), h, dproj, squares, small


ANY = pl.BlockSpec(memory_space=pl.ANY)


def _mesh_pos():
    return lax.axis_index("x"), lax.axis_index("y"), lax.axis_index("c")


def _remote(src, dst, send_sems, recv_sems, idx, peer):
    return pltpu.make_async_remote_copy(src_ref=src, dst_ref=dst, send_sem=send_sems.at[idx],
                                        recv_sem=recv_sems.at[idx], device_id=peer, device_id_type=MESH)


def _gather_in_proj(h, bufs, split, idx):
    t_tok, d = h.shape
    n = len(bufs)
    tm = min(t_tok, 512)
    nt = t_tok // tm
    n_fwd = 3 * sum(split)
    assert split[0]

    def body(idx_ref, h_ref, *refs):
        ins, proj_ref, outs = refs[:n], refs[n], refs[n + 1:2 * n + 1]
        wbuf, send_sems, recv_sems, fsend_sems, frecv_sems, l_sems = refs[2 * n + 1:]
        j, i = pl.program_id(0), pl.program_id(1)
        x, y, c = _mesh_pos()
        me = 2 * x + y
        sib = (x, y, 1 - c)
        peers = [((x, 1 - y, c), me ^ 1), ((1 - x, y, c), me ^ 2), ((1 - x, 1 - y, c), me ^ 3)]

        def part(ref, slot, t, half):
            if not split[t]:
                return ref.at[slot]
            hr = bufs[t].shape[1] // 2
            return ref.at[slot, pl.ds(pl.multiple_of(half * hr, 8), hr), :]

        def land(t):
            return wbuf if t == 0 else outs[t]

        def ici(t, k):
            peer, pj = peers[k]
            src = part(ins[t], me, t, c)
            return (_remote(src, part(land(t), me, t, c), send_sems, recv_sems, 3 * t + k, peer),
                    _remote(src, part(land(t), pj, t, c), send_sems, recv_sems, 3 * t + k, peer))

        fwd_index = {}
        for t in range(n):
            if split[t]:
                for k in range(3):
                    fwd_index[(t, k)] = len(fwd_index)

        def forward(t, k):
            pj = peers[k][1]
            got = part(land(t), pj, t, c)
            f = fwd_index[(t, k)]
            return (_remote(got, got, fsend_sems, frecv_sems, f, sib),
                    _remote(got, part(land(t), pj, t, 1 - c), fsend_sems, frecv_sems, f, sib))

        def write_back(k):
            pj = peers[k][1]
            return pltpu.make_async_copy(wbuf.at[pj], outs[0].at[pj], l_sems.at[1 + k])

        relay_peer = ((x + c) % 2, (y + 1 - c) % 2, c)

        def relay():
            got = part(wbuf, me ^ (2 - c), 0, c)
            return (_remote(got, got, send_sems, recv_sems, 2, relay_peer),
                    _remote(got, part(wbuf, me ^ 3, 0, c), send_sems, recv_sems, 2, relay_peer))

        direct = [(t, k) for t in range(n) for k in range(3) if (t, k) != (0, 2)]

        @pl.when((j == 0) & (i == 0))
        def _():
            for t, k in direct:
                ici(t, k)[0].start()
            own = pltpu.make_async_copy(ins[0].at[me], wbuf.at[me], l_sems.at[0])
            own.start()
            own.wait()

        @pl.when((j == 1) & (i == 0))
        def _():
            for k in range(2):
                ici(0, k)[1].wait_recv()
            relay()[0].start()
            for k in range(2):
                forward(0, k)[0].start()
            forward(0, 0)[1].wait_recv()
            write_back(0).start()

        @pl.when((j == 2) & (i == 0))
        def _():
            forward(0, 1)[1].wait_recv()
            write_back(1).start()

        @pl.when((j == 3) & (i == 0))
        def _():
            relay()[1].wait_recv()
            forward(0, 2)[0].start()
            forward(0, 2)[1].wait_recv()
            write_back(2).start()

        proj_ref[...] = _dot(h_ref[...], wbuf[me ^ j])

        @pl.when((j == N_CHIPS - 1) & (i == nt - 1))
        def _():
            for t in range(1, n):
                for k in range(3):
                    ici(t, k)[1].wait_recv()
                    if split[t]:
                        forward(t, k)[0].start()
            relay()[0].wait_send()
            for t, k in direct:
                ici(t, k)[0].wait_send()
            for t in range(n):
                if split[t]:
                    for k in range(3):
                        forward(t, k)[0].wait_send()
                        if t > 0:
                            forward(t, k)[1].wait_recv()
            for k in range(3):
                write_back(k).wait()

    grid_spec = pltpu.PrefetchScalarGridSpec(
        num_scalar_prefetch=1, grid=(N_CHIPS, nt),
        in_specs=[pl.BlockSpec((tm, d), lambda j, i, idx_ref: (i, 0))] + [ANY] * n,
        out_specs=[pl.BlockSpec((tm, W_BLK), lambda j, i, idx_ref: (i, idx_ref[0] ^ j))] + [ANY] * n,
        scratch_shapes=[pltpu.VMEM(bufs[0].shape, bufs[0].dtype),
                        pltpu.SemaphoreType.DMA((3 * n,)), pltpu.SemaphoreType.DMA((3 * n,)),
                        pltpu.SemaphoreType.DMA((n_fwd,)), pltpu.SemaphoreType.DMA((n_fwd,)),
                        pltpu.SemaphoreType.DMA((4,))])
    out_shape = [jax.ShapeDtypeStruct((t_tok, D_IN), F32)] + [jax.ShapeDtypeStruct(a.shape, a.dtype) for a in bufs]
    res = pl.pallas_call(
        body, name="gather_in_proj", grid_spec=grid_spec, out_shape=out_shape,
        input_output_aliases={2 + t: 1 + t for t in range(n)}, compiler_params=_params())(idx, h, *bufs)
    return res[0], res[1:]


def _row_tile(rows, row_bytes, cap_bytes=2 * 1024 * 1024):
    best = None
    for tr in range(8, rows + 1, 8):
        if rows % tr == 0 and tr * row_bytes <= cap_bytes:
            best = tr
    return best if best is not None else rows


XOR_ORDER = (3, 2, 1)


def _grads_reduce_scatter(h, dproj, squares, small, idx):
    t, d = h.shape
    nsq = len(squares)
    hr = d // 2
    qr = ROW_BLK // 2
    tk = min(t, 1024)
    nk = t // tk
    last = N_CHIPS - 1
    n_phase = 3

    def dest(s, idx_ref):
        xo = jnp.where(s == 0, XOR_ORDER[0], jnp.where(s == 1, XOR_ORDER[1], jnp.where(s == 2, XOR_ORDER[2], 0)))
        return idx_ref[0] ^ xo

    def k_sq(p, k):
        return jnp.where(p == 0, k, nk - 1)

    def k_w(p, k):
        return jnp.where(p == 0, 0, k)

    in_specs = [
        pl.BlockSpec((tk, hr), lambda s, p, k, idx_ref: (k_w(p, k), (1 - idx_ref[1] + jnp.maximum(p - 1, 0)) % 2)),
        pl.BlockSpec((tk, W_BLK), lambda s, p, k, idx_ref: (k_w(p, k), dest(s, idx_ref)))]
    for q in range(nsq):
        in_specs.append(pl.BlockSpec((tk, ROW_BLK), lambda s, p, k, idx_ref: (k_sq(p, k), dest(s, idx_ref))))
        in_specs.append(pl.BlockSpec((tk, d), lambda s, p, k, idx_ref: (k_sq(p, k), 0)))

    def body(idx_ref, *refs):
        nj = 1 + nsq
        h_ref, dp_ref = refs[0], refs[1]
        sq_in = refs[2:2 + 2 * nsq]
        small_in = refs[2 * nj]
        outs = refs[2 * nj + 1:3 * nj + 2]
        landing = refs[3 * nj + 2:4 * nj + 3]
        sc = refs[4 * nj + 3:]
        acc_w, xr_w, sb_w = sc[0:3]
        sq_sc = [sc[3 + 3 * q:6 + 3 * q] for q in range(nsq)]
        sm, smx = sc[3 * nj:3 * nj + 2]
        x_send, x_recv, i_send, i_recv, f_send, f_recv, o_sem, l_sem = sc[3 * nj + 2:]
        s, p, k = pl.program_id(0), pl.program_id(1), pl.program_id(2)
        x, y, c = _mesh_pos()
        sib = (x, y, 1 - c)
        peers = [((1 - x) if xo & 2 else x, (1 - y) if xo & 1 else y, c) for xo in XOR_ORDER]
        slot = s % 2
        mine_w = pl.ds(pl.multiple_of(c * hr, 8), hr)
        theirs_w = pl.ds(pl.multiple_of((1 - c) * hr, 8), hr)
        mine_q = pl.ds(pl.multiple_of(c * qr, 8), qr)
        theirs_q = pl.ds(pl.multiple_of((1 - c) * qr, 8), qr)

        def exch(j, src, dst):
            return _remote(src, dst, x_send, x_recv, 2 * j + slot, sib)

        sbufs = [sb_w] + [sq_sc[q][2] for q in range(nsq)]

        def ici(j, ss):
            return _remote(sbufs[j].at[ss], landing[j].at[ss], i_send, i_recv, last * j + ss, peers[ss])

        def exchanges():
            cps = [exch(0, acc_w.at[0], xr_w.at[slot])]
            cps += [exch(1 + q, sq_sc[q][0].at[theirs_q, :], sq_sc[q][1].at[slot]) for q in range(nsq)]
            return cps

        def small_send(ss):
            return _remote(sm.at[c], landing[nj].at[ss], i_send, i_recv, last * nj + ss, peers[ss])

        def small_start():
            load = pltpu.make_async_copy(small_in, sm, l_sem.at[nj + 1])
            load.start()
            load.wait()
            swap = _remote(sm, smx.at[pl.ds(0, 2)], x_send, x_recv, 2 * nj, sib)
            swap.start()
            swap.wait_recv()
            swap.wait_send()
            sm[...] = sm[...] + smx[0:2]
            for ss in range(last):
                small_send(ss).start()

        def pair_ref(j):
            return acc_w.at[1] if j == 0 else sq_sc[j - 1][0].at[mine_q, :]

        def sq_phase():
            pl.when((s == 0) & (k == 0))(small_start)
            for q in range(nsq):
                acc = sq_sc[q][0]

                @pl.when(k == 0)
                def _():
                    acc[...] = jnp.zeros((ROW_BLK, d), F32)

                acc[...] += _dot_tn(sq_in[2 * q][...], sq_in[2 * q + 1][...])

            @pl.when(k == nk - 1)
            def _():
                for cp in exchanges()[1:]:
                    cp.start()

        def w_phase(hf):
            @pl.when(k == 0)
            def _():
                acc_w[hf] = jnp.zeros((hr, W_BLK), F32)

            acc_w[hf] += _dot_tn(h_ref[...], dp_ref[...])

            @pl.when(k == nk - 1)
            def _():
                if hf == 0:
                    exchanges()[0].start()
                else:
                    finish_step()

        def finish_step():
            for cp in exchanges():
                cp.wait_recv()
                cp.wait_send()
            acc_w[1] += xr_w[slot]
            for q in range(nsq):
                sq_sc[q][0][mine_q, :] += sq_sc[q][1][slot]
            for ss in range(last):
                @pl.when(s == ss)
                def _():
                    for j in range(nj):
                        sbufs[j][ss] = pair_ref(j)[...].astype(BF16)
                        ici(j, ss).start()

            @pl.when(s == last)
            def _():
                for ss in range(last):
                    for j in range(nj):
                        ici(j, ss).wait_recv()
                        ici(j, ss).wait_send()
                    small_send(ss).wait_recv()
                    small_send(ss).wait_send()
                stage = [pltpu.make_async_copy(landing[j], sbufs[j], l_sem.at[j]) for j in range(nj)]
                stage.append(pltpu.make_async_copy(landing[nj], smx, l_sem.at[nj]))
                for cp in stage:
                    cp.start()
                for j in range(nj):
                    stage[j].wait()
                    total = pair_ref(j)[...]
                    for ss in range(last):
                        total = total + sbufs[j][ss].astype(F32)
                    pair_ref(j)[...] = total
                stage[nj].wait()
                by_xor = {xo: smx[ss] for ss, xo in enumerate(XOR_ORDER)}
                sm[c] = (sm[c] + by_xor[1]) + (by_xor[2] + by_xor[3])
                done = [(acc_w.at[1], outs[0].at[mine_w, :], outs[0].at[theirs_w, :])]
                done += [(pair_ref(1 + q), outs[1 + q].at[mine_q, :], outs[1 + q].at[theirs_q, :]) for q in range(nsq)]
                done.append((sm.at[c], outs[nj].at[c], outs[nj].at[1 - c]))
                copies = []
                for j, (src, mine, theirs) in enumerate(done):
                    keep = pltpu.make_async_copy(src, mine, o_sem.at[j])
                    give = _remote(src, mine, f_send, f_recv, j, sib)
                    take = _remote(src, theirs, f_send, f_recv, j, sib)
                    keep.start()
                    give.start()
                    copies.append((keep, give, take))
                for keep, give, take in copies:
                    keep.wait()
                    give.wait_send()
                    take.wait_recv()

        pl.when(p == 0)(sq_phase)
        for hf in range(2):
            pl.when(p == 1 + hf)(functools.partial(w_phase, hf))

    nj = 1 + nsq
    scratch = [pltpu.VMEM((2, hr, W_BLK), F32), pltpu.VMEM((2, hr, W_BLK), F32), pltpu.VMEM((last, hr, W_BLK), BF16)]
    for _ in range(nsq):
        scratch += [pltpu.VMEM((ROW_BLK, d), F32), pltpu.VMEM((2, qr, d), F32), pltpu.VMEM((last, qr, d), BF16)]
    scratch += [pltpu.VMEM((2, PK_HALF, LANES), F32), pltpu.VMEM((last, PK_HALF, LANES), F32)]
    scratch += [pltpu.SemaphoreType.DMA((2 * nj + 1,)), pltpu.SemaphoreType.DMA((2 * nj + 1,)),
                pltpu.SemaphoreType.DMA((last * (nj + 1),)), pltpu.SemaphoreType.DMA((last * (nj + 1),)),
                pltpu.SemaphoreType.DMA((nj + 1,)), pltpu.SemaphoreType.DMA((nj + 1,)),
                pltpu.SemaphoreType.DMA((nj + 1,)), pltpu.SemaphoreType.DMA((nj + 2,))]
    grid_spec = pltpu.PrefetchScalarGridSpec(
        num_scalar_prefetch=1, grid=(N_CHIPS, n_phase, nk), in_specs=in_specs + [ANY],
        out_specs=[ANY] * (2 * nj + 2), scratch_shapes=scratch)
    out_shape = [jax.ShapeDtypeStruct((d, W_BLK), F32)] + [jax.ShapeDtypeStruct((ROW_BLK, d), F32)] * nsq
    out_shape.append(jax.ShapeDtypeStruct((2, PK_HALF, LANES), F32))
    out_shape += [jax.ShapeDtypeStruct((last, hr, W_BLK), BF16)] + [jax.ShapeDtypeStruct((last, qr, d), BF16)] * nsq
    out_shape.append(jax.ShapeDtypeStruct((last, PK_HALF, LANES), F32))
    flat = [a for pair in squares for a in pair]
    res = pl.pallas_call(body, name="grads_reduce_scatter", grid_spec=grid_spec, out_shape=out_shape,
                         compiler_params=_params())(idx, h, dproj, *flat, small)
    return res[:nj + 1]


_VEC_NAMES = ("norm_g", "conv_b", "lru_b_a", "lru_b_x", "lru_lambda", "final_norm_g")


def _pack_small(p, conv_full=None, scalar=None):
    rows = [p["lru_w_a"].reshape(PK_WX - PK_WA, LANES), p["lru_w_x"].reshape(PK_VEC - PK_WX, LANES)]
    rows += [p[k].reshape(8, LANES) for k in _VEC_NAMES]
    rows.append(jnp.pad(p["attn_sinks"].reshape(1, N_Q_HEADS), ((0, 7), (0, LANES - N_Q_HEADS))))
    rows.append(jnp.zeros((32, LANES), F32) if conv_full is None else conv_full.reshape(32, LANES))
    tail = PK_ROWS - PK_SCALAR
    if scalar is None:
        rows.append(jnp.zeros((tail, LANES), F32))
    else:
        rows.append(jnp.pad(scalar.reshape(1, 1), ((0, tail - 1), (0, LANES - 1))))
    return jnp.concatenate(rows, axis=0)


def _unpack_small(pk, like):
    out = {"lru_w_a": pk[PK_WA:PK_WX].reshape(like["lru_w_a"].shape),
           "lru_w_x": pk[PK_WX:PK_VEC].reshape(like["lru_w_x"].shape)}
    for j, k in enumerate(_VEC_NAMES):
        out[k] = pk[PK_VEC + 8 * j:PK_VEC + 8 * j + 8].reshape(like[k].shape)
    out["attn_sinks"] = pk[PK_SINK:PK_SINK + 1, :N_Q_HEADS].reshape(like["attn_sinks"].shape)
    return out


_WEIGHTS = ("norm_g", "w_in", "conv_w", "conv_b", "lru_w_a", "lru_b_a", "lru_w_x", "lru_b_x", "lru_lambda",
            "attn_sinks", "w_rnn_out", "w_attn_out", "w_o", "final_norm_g")
_SMALL = ("norm_g", "conv_b", "lru_w_a", "lru_b_a", "lru_w_x", "lru_b_x", "lru_lambda", "attn_sinks", "final_norm_g")
_ROW_SHARDED = ("w_rnn_out", "w_attn_out", "w_o")


def kernel(x, norm_g, w_in, conv_w, conv_b, lru_w_a, lru_b_a, lru_w_x, lru_b_x, lru_lambda, attn_sinks, w_rnn_out, w_attn_out, w_o, final_norm_g, loss_target, m_norm_g, m_w_in, m_conv_w, m_conv_b, m_lru_w_a, m_lru_b_a, m_lru_w_x, m_lru_b_x, m_lru_lambda, m_attn_sinks, m_w_rnn_out, m_w_attn_out, m_w_o, m_final_norm_g, v_norm_g, v_w_in, v_conv_w, v_conv_b, v_lru_w_a, v_lru_b_a, v_lru_w_x, v_lru_b_x, v_lru_lambda, v_attn_sinks, v_w_rnn_out, v_w_attn_out, v_w_o, v_final_norm_g):
    w = dict(norm_g=norm_g, w_in=w_in, conv_w=conv_w, conv_b=conv_b, lru_w_a=lru_w_a, lru_b_a=lru_b_a, lru_w_x=lru_w_x,
             lru_b_x=lru_b_x, lru_lambda=lru_lambda, attn_sinks=attn_sinks, w_rnn_out=w_rnn_out, w_attn_out=w_attn_out,
             w_o=w_o, final_norm_g=final_norm_g)
    m = dict(norm_g=m_norm_g, w_in=m_w_in, conv_w=m_conv_w, conv_b=m_conv_b, lru_w_a=m_lru_w_a, lru_b_a=m_lru_b_a,
             lru_w_x=m_lru_w_x, lru_b_x=m_lru_b_x, lru_lambda=m_lru_lambda, attn_sinks=m_attn_sinks,
             w_rnn_out=m_w_rnn_out, w_attn_out=m_w_attn_out, w_o=m_w_o, final_norm_g=m_final_norm_g)
    v = dict(norm_g=v_norm_g, w_in=v_w_in, conv_w=v_conv_w, conv_b=v_conv_b, lru_w_a=v_lru_w_a, lru_b_a=v_lru_b_a,
             lru_w_x=v_lru_w_x, lru_b_x=v_lru_b_x, lru_lambda=v_lru_lambda, attn_sinks=v_attn_sinks,
             w_rnn_out=v_w_rnn_out, w_attn_out=v_w_attn_out, w_o=v_w_o, final_norm_g=v_final_norm_g)
    mx, my, mc = _mesh_pos()
    me = 2 * mx + my
    d = D_MODEL

    slot0 = jnp.stack([me, jnp.zeros_like(me)]).astype(jnp.int32)
    bufs = [_put_slot(w[k][0], N_CHIPS, slot0, w[k].shape[1], BF16, "cast_" + k) for k in ("w_in",) + _ROW_SHARDED]
    bufs.append(_put_slot(w["conv_w"][0], N_CHIPS, slot0, CONV_WIDTH, F32, "slot_conv_w"))
    h = _rmsnorm_fwd(x.reshape(-1, d), w["norm_g"])
    proj, (g_in, g_r, g_a, g_o, g_cw) = _gather_in_proj(h, bufs, [True, True, True, True, False],
                                                        jnp.reshape(me, (1,)).astype(jnp.int32))
    conv_full = g_cw.transpose(1, 0, 2).reshape(CONV_WIDTH, D_RNN)

    loss_local, grad_x, h, dproj, squares, gsmall = _local_grads(
        x, loss_target, h, proj, w["norm_g"], g_in, conv_full, w["conv_b"], w["lru_w_a"][0], w["lru_b_a"], w["lru_w_x"][0],
        w["lru_b_x"], w["lru_lambda"], w["attn_sinks"][0], g_r.reshape(d, d), g_a.reshape(d, d), g_o.reshape(d, d),
        w["final_norm_g"].reshape(1, d))
    gpack = _pack_small(gsmall, gsmall["conv_w"], loss_local).reshape(2, PK_HALF, LANES)
    f_in, f_r, f_a, f_o, spack = _grads_reduce_scatter(h, dproj, squares, gpack, jnp.stack([me, mc]).astype(jnp.int32))
    spack = spack.reshape(PK_ROWS, LANES)
    loss = spack[PK_SCALAR, 0]

    grads = _unpack_small(spack, w)
    conv_all = spack[PK_CONV:PK_CONV + 32].reshape(CONV_WIDTH, D_RNN)
    grads["conv_w"] = lax.dynamic_slice_in_dim(conv_all, me * (D_RNN // N_CHIPS), D_RNN // N_CHIPS, axis=1)[None]
    grads["w_in"] = f_in[None]
    grads["w_rnn_out"], grads["w_attn_out"], grads["w_o"] = f_r[None], f_a[None], f_o[None]

    delta, new_m, new_v = {}, {}, {}
    for k in ("w_in",) + _ROW_SHARDED:
        dk, mk, vk = _adamw(w[k][0], grads[k][0], m[k][0], v[k][0], "adamw_" + k)
        delta[k], new_m[k], new_v[k] = dk[None], mk[None], vk[None]
    shp = (2 * CONV_WIDTH, LANES)
    dk, mk, vk = _adamw(w["conv_w"].reshape(shp), grads["conv_w"].reshape(shp), m["conv_w"].reshape(shp),
                        v["conv_w"].reshape(shp), "adamw_conv_w")
    delta["conv_w"], new_m["conv_w"], new_v["conv_w"] = (a.reshape(w["conv_w"].shape) for a in (dk, mk, vk))
    dk, mk, vk = _adamw(_pack_small(w), spack, _pack_small(m), _pack_small(v), "adamw_small")
    for src, dst in ((dk, delta), (mk, new_m), (vk, new_v)):
        dst.update(_unpack_small(src, w))

    return (loss, grad_x, *[grads[k] for k in _WEIGHTS], *[delta[k] for k in _WEIGHTS],
            *[new_m[k] for k in _WEIGHTS], *[new_v[k] for k in _WEIGHTS])
```

```python
import functools
import math

import jax
import jax.numpy as jnp
from jax import lax
from jax.experimental import pallas as pl
from jax.experimental.pallas import tpu as pltpu

F32 = jnp.float32
BF16 = jnp.bfloat16
MESH = pl.DeviceIdType.MESH

D_MODEL = 1024
D_RNN = 1024
N_RNN_BLOCKS = 8
RNN_BLOCK = D_RNN // N_RNN_BLOCKS
CONV_WIDTH = 4
LRU_C = 8.0
HEAD_DIM = 64
N_Q_HEADS = 16
N_KV_HEADS = 4
D_ATTN = N_Q_HEADS * HEAD_DIM
D_KV = N_KV_HEADS * HEAD_DIM
WINDOW = 128
ROPE_DIM = HEAD_DIM // 4
ROPE_THETA = 500000.0
NORM_EPS = 1e-6
OFF_RNN_X = 0
OFF_RNN_G = OFF_RNN_X + D_RNN
OFF_Q = OFF_RNN_G + D_RNN
OFF_K = OFF_Q + D_ATTN
OFF_V = OFF_K + D_KV
OFF_ATTN_G = OFF_V + D_KV
OFF_MERGE_R = OFF_ATTN_G + D_ATTN
OFF_MERGE_A = OFF_MERGE_R + D_MODEL
D_IN = OFF_MERGE_A + D_MODEL

ADAM_LR = 0.001
ADAM_B1 = 0.9
ADAM_B2 = 0.999
ADAM_EPS = 1e-08
ADAM_WD = 0.01
ADAM_STEP = 10

N_CHIPS = 4
W_BLK = D_IN // N_CHIPS
ROW_BLK = D_MODEL // N_CHIPS
LANES = 128
ATT_BLK = 128
VMEM_LIMIT = 56 * 1024 * 1024
NEG_BIG = -1e30
ATTN_SCALE = 1.0 / math.sqrt(HEAD_DIM)

PK_WA = 0
PK_WX = PK_WA + N_RNN_BLOCKS * RNN_BLOCK
PK_VEC = PK_WX + N_RNN_BLOCKS * RNN_BLOCK
PK_SINK = PK_VEC + 6 * 8
PK_CONV = PK_SINK + 8
PK_SCALAR = PK_CONV + 32
PK_ROWS = PK_SCALAR + 8
PK_HALF = PK_ROWS // 2


def _params(**kw):
    return pltpu.CompilerParams(vmem_limit_bytes=VMEM_LIMIT, **kw)


def _sigmoid(z):
    return 1.0 / (1.0 + jnp.exp(-z))


def _dot(a, b):
    return jnp.dot(a, b, preferred_element_type=F32)


def _dot_nt(a, b):
    return lax.dot_general(a, b, (((1,), (1,)), ((), ())), preferred_element_type=F32)


def _dot_tn(a, b):
    return lax.dot_general(a, b, (((0,), (0,)), ((), ())), preferred_element_type=F32)


def _put_slot(src, n_slots, slot_and_blk, rows, dtype, name):
    _, c = src.shape
    tr = _row_tile(rows, c * 4)
    steps = rows // tr

    def body(idx_ref, s_ref, o_ref):
        o_ref[...] = s_ref[...].astype(dtype)

    grid_spec = pltpu.PrefetchScalarGridSpec(
        num_scalar_prefetch=1, grid=(steps,),
        in_specs=[pl.BlockSpec((tr, c), lambda i, idx_ref: (idx_ref[1] * steps + i, 0))],
        out_specs=pl.BlockSpec((None, tr, c), lambda i, idx_ref: (idx_ref[0], i, 0)))
    return pl.pallas_call(body, name=name, grid_spec=grid_spec,
                          out_shape=jax.ShapeDtypeStruct((n_slots, rows, c), dtype),
                          compiler_params=_params())(slot_and_blk, src)


def _rmsnorm_fwd(x, g):
    t, d = x.shape
    tm = min(t, 512)

    def body(x_ref, g_ref, o_ref):
        xv = x_ref[...]
        r = lax.rsqrt(jnp.mean(xv * xv, axis=-1, keepdims=True) + NORM_EPS)
        o_ref[...] = (xv * r * g_ref[...]).astype(BF16)

    return pl.pallas_call(
        body, name="rmsnorm_fwd", grid=(t // tm,), out_shape=jax.ShapeDtypeStruct((t, d), BF16),
        in_specs=[pl.BlockSpec((tm, d), lambda i: (i, 0)), pl.BlockSpec((1, d), lambda i: (0, 0))],
        out_specs=pl.BlockSpec((tm, d), lambda i: (i, 0)), compiler_params=_params())(x, g)


def _rmsnorm_bwd(x, dh, dx2, g):
    t, d = x.shape
    tm = min(t, 512)

    def body(x_ref, dh_ref, dx2_ref, g_ref, gx_ref, dg_ref):
        i = pl.program_id(0)
        xv = x_ref[...]
        dhv = dh_ref[...]
        r = lax.rsqrt(jnp.mean(xv * xv, axis=-1, keepdims=True) + NORM_EPS)
        nrm = xv * r
        dn = dhv * g_ref[...]
        gx_ref[...] = dx2_ref[...] + r * (dn - nrm * jnp.mean(dn * nrm, axis=-1, keepdims=True))

        @pl.when(i == 0)
        def _():
            dg_ref[...] = jnp.zeros_like(dg_ref)

        dg_ref[...] += jnp.sum(dhv * nrm, axis=0, keepdims=True)

    return pl.pallas_call(
        body, name="rmsnorm_bwd", grid=(t // tm,),
        out_shape=(jax.ShapeDtypeStruct((t, d), F32), jax.ShapeDtypeStruct((1, d), F32)),
        in_specs=[pl.BlockSpec((tm, d), lambda i: (i, 0)), pl.BlockSpec((tm, d), lambda i: (i, 0)),
                  pl.BlockSpec((tm, d), lambda i: (i, 0)), pl.BlockSpec((1, d), lambda i: (0, 0))],
        out_specs=(pl.BlockSpec((tm, d), lambda i: (i, 0)), pl.BlockSpec((1, d), lambda i: (0, 0))),
        compiler_params=_params())(x, dh, dx2, g)


def _adamw(w, g, m, v, name):
    r, c = w.shape
    tr = _row_tile(r, c * 4, 1024 * 1024)
    c1 = 1.0 - ADAM_B1 ** ADAM_STEP
    c2 = 1.0 - ADAM_B2 ** ADAM_STEP

    def body(w_ref, g_ref, m_ref, v_ref, d_ref, nm_ref, nv_ref):
        gv = g_ref[...]
        nm = ADAM_B1 * m_ref[...] + (1.0 - ADAM_B1) * gv
        nv = ADAM_B2 * v_ref[...] + (1.0 - ADAM_B2) * (gv * gv)
        m_hat = nm / c1
        v_hat = nv / c2
        d_ref[...] = -ADAM_LR * (m_hat / (jnp.sqrt(v_hat) + ADAM_EPS) + ADAM_WD * w_ref[...])
        nm_ref[...] = nm
        nv_ref[...] = nv

    spec = pl.BlockSpec((tr, c), lambda i: (i, 0))
    sds = jax.ShapeDtypeStruct((r, c), F32)
    return pl.pallas_call(
        body, name=name, grid=(r // tr,), out_shape=(sds, sds, sds),
        in_specs=[spec, spec, spec, spec], out_specs=(spec, spec, spec), compiler_params=_params())(w, g, m, v)


def _grad_h(dproj, w_bm):
    t = dproj.shape[0]
    nb, d, wb = w_bm.shape
    tm = min(t, 1024)

    def body(dp_ref, w_ref, o_ref, acc_ref):
        k = pl.program_id(1)

        @pl.when(k == 0)
        def _():
            acc_ref[...] = jnp.zeros_like(acc_ref)

        acc_ref[...] += _dot_nt(dp_ref[...], w_ref[...])

        @pl.when(k == nb - 1)
        def _():
            o_ref[...] = acc_ref[...]

    return pl.pallas_call(
        body, name="grad_h", grid=(t // tm, nb), out_shape=jax.ShapeDtypeStruct((t, d), F32),
        in_specs=[pl.BlockSpec((tm, wb), lambda i, k: (i, k)), pl.BlockSpec((None, d, wb), lambda i, k: (k, 0, 0))],
        out_specs=pl.BlockSpec((tm, d), lambda i, k: (i, 0)),
        scratch_shapes=[pltpu.VMEM((tm, d), F32)], compiler_params=_params())(dproj, w_bm)


def _shift_down(v, d, fill):
    n = v.shape[0]
    if d % 8 == 0:
        return jnp.concatenate([jnp.full((d,) + v.shape[1:], fill, v.dtype), v[: n - d]], axis=0)
    row = lax.broadcasted_iota(jnp.int32, v.shape, 0)
    return jnp.where(row >= d, pltpu.roll(v, d, axis=0), fill)


def _shift_up(v, d, fill):
    n = v.shape[0]
    if d % 8 == 0:
        return jnp.concatenate([v[d:], jnp.full((d,) + v.shape[1:], fill, v.dtype)], axis=0)
    row = lax.broadcasted_iota(jnp.int32, v.shape, 0)
    return jnp.where(row < n - d, pltpu.roll(v, n - d, axis=0), fill)


def _scan_log(a, b, shift):
    n = a.shape[0]
    d = 1
    while d < n:
        b = a * shift(b, d, 0.0) + b
        if 2 * d < n:
            a = a * shift(a, d, 1.0)
        d *= 2
    return b


SUBLANES = 8


def _scan(a, b, sa_ref, sb_ref, reverse):
    n, c = a.shape
    g = n // SUBLANES
    a3, b3 = a.reshape(g, SUBLANES, c), b.reshape(g, SUBLANES, c)
    sub = lax.broadcasted_iota(jnp.int32, a3.shape, 1)
    d = 1
    while d < SUBLANES:
        keep = (sub < SUBLANES - d) if reverse else (sub >= d)
        amount = SUBLANES - d if reverse else d
        b3 = a3 * jnp.where(keep, pltpu.roll(b3, amount, axis=1), 0.0) + b3
        a3 = a3 * jnp.where(keep, pltpu.roll(a3, amount, axis=1), 1.0)
        d *= 2
    sa_ref[...] = a3.reshape(n, c)
    sb_ref[...] = b3.reshape(n, c)
    edge = 0 if reverse else SUBLANES - 1
    shift = _shift_up if reverse else _shift_down
    totals = _scan_log(sa_ref[pl.ds(edge, g, stride=SUBLANES), :], sb_ref[pl.ds(edge, g, stride=SUBLANES), :], shift)
    carry = shift(totals, 1, 0.0)
    return (a3 * carry[:, None, :] + b3).reshape(n, c)


def _neg_expm1_twice(log_a, a):
    return -jnp.tanh(log_a) * (a * a + 1.0)


def _softplus(z):
    e = jnp.exp(-jnp.abs(z))
    w = 1.0 + e
    log1p = jnp.where(w == 1.0, e, jnp.log(w) * (e / jnp.where(w == 1.0, 1.0, w - 1.0)))
    return jnp.maximum(z, 0.0) + log1p


def _conv(up, cw, cb):
    out = cb + cw[CONV_WIDTH - 1:CONV_WIDTH, :] * up
    for j in range(CONV_WIDTH - 1):
        out = out + cw[j:j + 1, :] * _shift_down(up, CONV_WIDTH - 1 - j, 0.0)
    return out


def _lru_gates(u, wa_ref, ba_ref, wx_ref, bx_ref, lam_ref):
    ub = u.astype(BF16)
    r = _sigmoid(_dot(ub, wa_ref[...].astype(BF16)) + ba_ref[...])
    i = _sigmoid(_dot(ub, wx_ref[...].astype(BF16)) + bx_ref[...])
    sp = _softplus(-lam_ref[...])
    log_a = (-LRU_C) * r * sp
    a = jnp.exp(log_a)
    mult = jnp.sqrt(_neg_expm1_twice(log_a, a))
    return r, i, sp, a, mult


def _lru_specs(s):
    cb = RNN_BLOCK
    vec = pl.BlockSpec((1, cb), lambda n, b: (0, n))
    return dict(
        up=pl.BlockSpec((None, s, cb), lambda n, b: (b, 0, OFF_RNN_X // cb + n)),
        gr=pl.BlockSpec((None, s, cb), lambda n, b: (b, 0, OFF_RNN_G // cb + n)),
        act=pl.BlockSpec((None, s, cb), lambda n, b: (b, 0, n)),
        cw=pl.BlockSpec((CONV_WIDTH, cb), lambda n, b: (0, n)),
        vec=vec,
        wblk=pl.BlockSpec((None, cb, cb), lambda n, b: (n, 0, 0)),
    )


def _lru_fwd(proj3, cw, cb, wa, ba, wx, bx, lam):
    bsz, s, _ = proj3.shape
    sp = _lru_specs(s)

    def body(up_ref, gr_ref, cw_ref, cb_ref, wa_ref, ba_ref, wx_ref, bx_ref, lam_ref, h_ref, y_ref, sa_ref, sb_ref):
        u = _conv(up_ref[...], cw_ref[...], cb_ref[...])
        _, i, _, a, mult = _lru_gates(u, wa_ref, ba_ref, wx_ref, bx_ref, lam_ref)
        h = _scan(a, mult * (i * u), sa_ref, sb_ref, reverse=False)
        h_ref[...] = h
        g = gr_ref[...]
        y_ref[...] = (h * (g * _sigmoid(g))).astype(BF16)

    return pl.pallas_call(
        body, name="lru_fwd", grid=(N_RNN_BLOCKS, bsz),
        out_shape=(jax.ShapeDtypeStruct((bsz, s, D_RNN), F32), jax.ShapeDtypeStruct((bsz, s, D_RNN), BF16)),
        in_specs=[sp["up"], sp["gr"], sp["cw"], sp["vec"], sp["wblk"], sp["vec"], sp["wblk"], sp["vec"], sp["vec"]],
        out_specs=(sp["act"], sp["act"]), scratch_shapes=[pltpu.VMEM((s, RNN_BLOCK), F32)] * 2,
        compiler_params=_params())(proj3, proj3, cw, cb, wa, ba, wx, bx, lam)


def _lru_bwd(proj3, h3, dy3, cw, cb, wa, ba, wx, bx, lam):
    bsz, s, _ = proj3.shape
    sp = _lru_specs(s)

    def body(up_ref, gr_ref, h_ref, dy_ref, cw_ref, cb_ref, wa_ref, ba_ref, wx_ref, bx_ref, lam_ref,
             dup_ref, dgr_ref, dcw_ref, dcb_ref, dwa_ref, dba_ref, dwx_ref, dbx_ref, dlam_ref, sa_ref, sb_ref):
        b = pl.program_id(1)
        up = up_ref[...]
        cwv = cw_ref[...]
        u = _conv(up, cwv, cb_ref[...])
        r, i, spv, a, mult = _lru_gates(u, wa_ref, ba_ref, wx_ref, bx_ref, lam_ref)
        h = h_ref[...]
        g = gr_ref[...]
        dy = dy_ref[...]
        sg = _sigmoid(g)
        dgr_ref[...] = (dy * h * (sg * (1.0 + g * (1.0 - sg)))).astype(BF16)
        dh = dy * (g * sg)
        adj = _scan(_shift_up(a, 1, 0.0), dh, sa_ref, sb_ref, reverse=True)
        da = adj * _shift_down(h, 1, 0.0)
        dmult = adj * (i * u)
        di = adj * mult * u
        du = adj * mult * i
        dla = da * a - dmult * (a * a) / mult
        dr = dla * ((-LRU_C) * spv)
        dsp = jnp.sum(dla * ((-LRU_C) * r), axis=0, keepdims=True)
        dza = dr * r * (1.0 - r)
        dzx = di * i * (1.0 - i)
        ub = u.astype(BF16)
        dzab = dza.astype(BF16)
        dzxb = dzx.astype(BF16)
        du = du + _dot_nt(dzab, wa_ref[...].astype(BF16)) + _dot_nt(dzxb, wx_ref[...].astype(BF16))
        dup = cwv[CONV_WIDTH - 1:CONV_WIDTH, :] * du
        for j in range(CONV_WIDTH - 1):
            dup = dup + cwv[j:j + 1, :] * _shift_up(du, CONV_WIDTH - 1 - j, 0.0)
        dup_ref[...] = dup.astype(BF16)

        @pl.when(b == 0)
        def _():
            for ref in (dcw_ref, dcb_ref, dwa_ref, dba_ref, dwx_ref, dbx_ref, dlam_ref):
                ref[...] = jnp.zeros_like(ref)

        rows = [jnp.sum(du * _shift_down(up, CONV_WIDTH - 1 - j, 0.0), axis=0, keepdims=True)
                for j in range(CONV_WIDTH - 1)]
        rows.append(jnp.sum(du * up, axis=0, keepdims=True))
        dcw_ref[...] += jnp.concatenate(rows, axis=0)
        dcb_ref[...] += jnp.sum(du, axis=0, keepdims=True)
        dwa_ref[...] += _dot_tn(ub, dzab)
        dba_ref[...] += jnp.sum(dza, axis=0, keepdims=True)
        dwx_ref[...] += _dot_tn(ub, dzxb)
        dbx_ref[...] += jnp.sum(dzx, axis=0, keepdims=True)
        dlam_ref[...] += dsp * (-_sigmoid(-lam_ref[...]))

    act_b = jax.ShapeDtypeStruct((bsz, s, D_RNN), BF16)
    vec = jax.ShapeDtypeStruct((1, D_RNN), F32)
    wsd = jax.ShapeDtypeStruct((N_RNN_BLOCKS, RNN_BLOCK, RNN_BLOCK), F32)
    return pl.pallas_call(
        body, name="lru_bwd", grid=(N_RNN_BLOCKS, bsz),
        out_shape=(act_b, act_b, jax.ShapeDtypeStruct((CONV_WIDTH, D_RNN), F32), vec, wsd, vec, wsd, vec, vec),
        in_specs=[sp["up"], sp["gr"], sp["act"], sp["act"], sp["cw"], sp["vec"], sp["wblk"], sp["vec"],
                  sp["wblk"], sp["vec"], sp["vec"]],
        out_specs=(sp["act"], sp["act"], sp["cw"], sp["vec"], sp["wblk"], sp["vec"], sp["wblk"], sp["vec"], sp["vec"]),
        scratch_shapes=[pltpu.VMEM((s, RNN_BLOCK), F32)] * 2,
        compiler_params=_params())(proj3, proj3, h3, dy3, cw, cb, wa, ba, wx, bx, lam)


def _rope_tables(s):
    half = ROPE_DIM // 2
    pos = jnp.arange(s, dtype=F32)
    inv_freq = ROPE_THETA ** (-jnp.arange(0, ROPE_DIM, 2, dtype=F32) / ROPE_DIM)
    ang = pos[:, None] * inv_freq[None, :]
    cos, sin = jnp.cos(ang), jnp.sin(ang)
    rest = HEAD_DIM - ROPE_DIM
    cos64 = jnp.concatenate([cos, cos, jnp.ones((s, rest), F32)], axis=1)
    sin64 = jnp.concatenate([-sin, sin, jnp.zeros((s, rest), F32)], axis=1)
    assert half * 2 == ROPE_DIM
    return jnp.tile(cos64, (1, LANES // HEAD_DIM)), jnp.tile(sin64, (1, LANES // HEAD_DIM))


def _swap_rot_halves(v):
    half = ROPE_DIM // 2
    lane = lax.broadcasted_iota(jnp.int32, v.shape, 1) % HEAD_DIM
    second = jnp.where(lane < ROPE_DIM, pltpu.roll(v, half, axis=1), 0.0)
    return jnp.where(lane < half, pltpu.roll(v, LANES - half, axis=1), second)


def _rope(v, cos, sin):
    tiles = []
    for t in range(v.shape[1] // LANES):
        vt = v[:, t * LANES:(t + 1) * LANES]
        tiles.append(vt * cos + _swap_rot_halves(vt) * sin)
    return tiles[0] if len(tiles) == 1 else jnp.concatenate(tiles, axis=1)


def _unrope(v, cos, sin):
    tiles = []
    for t in range(v.shape[1] // LANES):
        vt = v[:, t * LANES:(t + 1) * LANES]
        tiles.append(vt * cos + _swap_rot_halves(vt * sin))
    return tiles[0] if len(tiles) == 1 else jnp.concatenate(tiles, axis=1)


HEADS_PER_STEP = 8
QW = HEADS_PER_STEP * HEAD_DIM
N_PAIRS = N_Q_HEADS // HEADS_PER_STEP
Q_PER_KV = N_Q_HEADS // N_KV_HEADS
KV_PER_STEP = HEADS_PER_STEP // Q_PER_KV


QT_COLS = Q_PER_KV * ATT_BLK


def _attn_scratch(s, with_vt):
    nb = s // ATT_BLK
    pad = s + ATT_BLK
    shapes = [pltpu.VMEM((nb, LANES, QT_COLS), BF16),
              pltpu.VMEM((KV_PER_STEP, pad, LANES), BF16),
              pltpu.VMEM((KV_PER_STEP, pad, LANES), BF16)]
    if with_vt:
        shapes.append(pltpu.VMEM((LANES, pad), BF16))
    return shapes


def _attn_specs(s, order):
    def mk(width, base, **kw):
        if order == "bp":
            return pl.BlockSpec((None, s, width), lambda b, p: (b, 0, base + p), **kw)
        return pl.BlockSpec((None, s, width), lambda p, b: (b, 0, base + p), **kw)
    one = dict(pipeline_mode=pl.Buffered(1))
    tbl = pl.BlockSpec((s, LANES), lambda *_: (0, 0))
    return dict(q=mk(QW, OFF_Q // QW), k=mk(LANES, OFF_K // LANES), v=mk(LANES, OFF_V // LANES),
                g=mk(QW, OFF_ATTN_G // QW), act=mk(QW, 0), kv=mk(LANES, 0), tbl=tbl,
                q1=mk(QW, OFF_Q // QW, **one), g1=mk(QW, OFF_ATTN_G // QW, **one), act1=mk(QW, 0, **one),
                smem=pl.BlockSpec(memory_space=pltpu.SMEM))


def _to_qt(blk):
    rows = []
    for j in range(KV_PER_STEP):
        cols = []
        for tt in range(2):
            t = 2 * j + tt
            tr = blk[:, t * LANES:(t + 1) * LANES].T
            cols += [tr[0:HEAD_DIM, :], tr[HEAD_DIM:, :]]
        rows.append(jnp.concatenate(cols, axis=1))
    return jnp.concatenate(rows, axis=0)


def _from_qt(xt):
    tiles = []
    for j in range(KV_PER_STEP):
        for tt in range(2):
            g0 = 2 * tt
            pair = jnp.concatenate([xt[j * HEAD_DIM:(j + 1) * HEAD_DIM, (g0 + i) * ATT_BLK:(g0 + i + 1) * ATT_BLK]
                                    for i in range(2)], axis=0)
            tiles.append(pair.T)
    return jnp.concatenate(tiles, axis=1)


def _attn_prep(q_ref, k_ref, v_ref, cos_ref, sin_ref, qt_sc, km_sc, vm_sc, t_sc, transposed, nb):
    zeros = jnp.zeros((ATT_BLK, LANES), BF16)
    for j in range(KV_PER_STEP):
        km_sc[j, 0:ATT_BLK, :] = zeros
        vm_sc[j, 0:ATT_BLK, :] = zeros
    t_sc[:, 0:ATT_BLK] = zeros
    head_of_lane = lax.broadcasted_iota(jnp.int32, (ATT_BLK, LANES), 1) // HEAD_DIM

    def prep(n, carry):
        r0 = pl.multiple_of(n * ATT_BLK, ATT_BLK)
        cs = cos_ref[pl.ds(r0, ATT_BLK), :]
        sn = sin_ref[pl.ds(r0, ATT_BLK), :]
        qt_sc[n] = _to_qt(_rope(q_ref[pl.ds(r0, ATT_BLK), :], cs, sn) * ATTN_SCALE).astype(BF16)
        k = _rope(k_ref[pl.ds(r0, ATT_BLK), :], cs, sn)
        v = v_ref[pl.ds(r0, ATT_BLK), :]
        for j in range(KV_PER_STEP):
            km_sc[j, pl.ds(r0 + ATT_BLK, ATT_BLK), :] = jnp.where(head_of_lane == j, k, 0.0).astype(BF16)
            vm_sc[j, pl.ds(r0 + ATT_BLK, ATT_BLK), :] = jnp.where(head_of_lane == j, v, 0.0).astype(BF16)
        t_sc[:, pl.ds(r0 + ATT_BLK, ATT_BLK)] = (k if transposed == "k" else v).T.astype(BF16)
        return carry

    lax.fori_loop(0, nb, prep, 0)


def _band_mask_t(n):
    shape = (2 * ATT_BLK, QT_COLS)
    key = lax.broadcasted_iota(jnp.int32, shape, 0)
    qry = lax.broadcasted_iota(jnp.int32, shape, 1) % ATT_BLK
    lo = jnp.where(n == 0, ATT_BLK, 0)
    return (key > qry) & (key <= qry + WINDOW) & (key >= lo)


def _sink_row(sink_ref, first):
    return jnp.concatenate([jnp.full((1, ATT_BLK), sink_ref[first + g], F32) for g in range(Q_PER_KV)], axis=1)


def _softmax_cols(scores_t, valid, sink):
    sc = jnp.where(valid, scores_t, NEG_BIG)
    m = jnp.maximum(jnp.max(sc, axis=0, keepdims=True), sink)
    e = jnp.exp(sc - m)
    es = jnp.exp(sink - m)
    inv = 1.0 / (jnp.sum(e, axis=0, keepdims=True) + es)
    return e * inv, es * inv


def _attn_fwd(proj3, sinks, cosf, sinf):
    bsz, s, _ = proj3.shape
    nb = s // ATT_BLK
    sp = _attn_specs(s, "bp")

    def body(sink_ref, q_ref, k_ref, v_ref, g_ref, cos_ref, sin_ref, o_ref, y_ref, qt_sc, km_sc, vm_sc, vt_sc):
        p = pl.program_id(1)
        _attn_prep(q_ref, k_ref, v_ref, cos_ref, sin_ref, qt_sc, km_sc, vm_sc, vt_sc, "v", nb)
        kv_row = lax.broadcasted_iota(jnp.int32, (LANES, QT_COLS), 0) // HEAD_DIM

        def blk(n, carry):
            r0 = pl.multiple_of(n * ATT_BLK, ATT_BLK)
            valid = _band_mask_t(n)
            rq = qt_sc[n]
            vt = vt_sc[:, pl.ds(r0, 2 * ATT_BLK)]
            ots = []
            for j in range(KV_PER_STEP):
                st = _dot(km_sc[j, pl.ds(r0, 2 * ATT_BLK), :], rq)
                pt, _ = _softmax_cols(st, valid, _sink_row(sink_ref, p * HEADS_PER_STEP + j * Q_PER_KV))
                ots.append(_dot(vt, pt.astype(BF16)))
            o = _from_qt(jnp.where(kv_row == 0, ots[0], ots[1]))
            o_ref[pl.ds(r0, ATT_BLK), :] = o
            g = g_ref[pl.ds(r0, ATT_BLK), :]
            y_ref[pl.ds(r0, ATT_BLK), :] = (o * (g * _sigmoid(g))).astype(BF16)
            return carry

        lax.fori_loop(0, nb, blk, 0)

    return pl.pallas_call(
        body, name="attn_fwd", grid=(bsz, N_PAIRS),
        out_shape=(jax.ShapeDtypeStruct((bsz, s, D_ATTN), F32), jax.ShapeDtypeStruct((bsz, s, D_ATTN), BF16)),
        in_specs=[sp["smem"], sp["q"], sp["k"], sp["v"], sp["g"], sp["tbl"], sp["tbl"]],
        out_specs=(sp["act"], sp["act"]),
        scratch_shapes=_attn_scratch(s, True),
        compiler_params=_params())(sinks, proj3, proj3, proj3, proj3, cosf, sinf)


def _attn_bwd(proj3, o3, dy3, sinks, cosf, sinf):
    bsz, s, _ = proj3.shape
    nb = s // ATT_BLK
    sp = _attn_specs(s, "pb")

    def body(sink_ref, q_ref, k_ref, v_ref, g_ref, o_ref, dy_ref, cos_ref, sin_ref,
             dq_ref, dk_ref, dv_ref, dg_ref, ds_ref, qt_sc, km_sc, vm_sc, kt_sc, dot_sc, dqt_sc, dk_sc, dv_sc):
        p = pl.program_id(0)
        b = pl.program_id(1)
        _attn_prep(q_ref, k_ref, v_ref, cos_ref, sin_ref, qt_sc, km_sc, vm_sc, kt_sc, "k", nb)
        dk_sc[...] = jnp.zeros_like(dk_sc)
        dv_sc[...] = jnp.zeros_like(dv_sc)

        def gate(n, carry):
            r0 = pl.multiple_of(n * ATT_BLK, ATT_BLK)
            g = g_ref[pl.ds(r0, ATT_BLK), :]
            dy = dy_ref[pl.ds(r0, ATT_BLK), :]
            sg = _sigmoid(g)
            dg_ref[pl.ds(r0, ATT_BLK), :] = (dy * o_ref[pl.ds(r0, ATT_BLK), :] * (sg * (1.0 + g * (1.0 - sg)))).astype(BF16)
            dot_sc[n] = _to_qt(dy * (g * sg)).astype(BF16)
            return carry

        lax.fori_loop(0, nb, gate, 0)
        kv_lane = lax.broadcasted_iota(jnp.int32, (2 * ATT_BLK, LANES), 1) // HEAD_DIM
        kv_row = lax.broadcasted_iota(jnp.int32, (LANES, QT_COLS), 0) // HEAD_DIM

        def blk(n, acc):
            r0 = pl.multiple_of(n * ATT_BLK, ATT_BLK)
            valid = _band_mask_t(n)
            rq = qt_sc[n]
            rd = dot_sc[n]
            kt = kt_sc[:, pl.ds(r0, 2 * ATT_BLK)]
            dvs, dks, dqs, new_acc = [], [], [], []
            for j in range(KV_PER_STEP):
                st = _dot(km_sc[j, pl.ds(r0, 2 * ATT_BLK), :], rq)
                pt, ps = _softmax_cols(st, valid, _sink_row(sink_ref, p * HEADS_PER_STEP + j * Q_PER_KV))
                dpt = _dot(vm_sc[j, pl.ds(r0, 2 * ATT_BLK), :], rd)
                delta = jnp.sum(pt * dpt, axis=0, keepdims=True)
                dst = (pt * (dpt - delta)).astype(BF16)
                new_acc.append(acc[j] + ps * delta)
                dvs.append(_dot_nt(pt.astype(BF16), rd))
                dks.append(_dot_nt(dst, rq))
                dqs.append(_dot(kt, dst))
            dv_sc[pl.ds(r0, 2 * ATT_BLK), :] += jnp.where(kv_lane == 0, dvs[0], dvs[1])
            dk_sc[pl.ds(r0, 2 * ATT_BLK), :] += jnp.where(kv_lane == 0, dks[0], dks[1])
            dqt_sc[n] = jnp.where(kv_row == 0, dqs[0], dqs[1]) * ATTN_SCALE
            return tuple(new_acc)

        acc = lax.fori_loop(0, nb, blk, tuple(jnp.zeros((1, QT_COLS), F32) for _ in range(KV_PER_STEP)))
        lane1 = lax.broadcasted_iota(jnp.int32, (1, LANES), 1)
        dsink = jnp.zeros((1, LANES), F32)
        for j in range(KV_PER_STEP):
            for i in range(Q_PER_KV):
                part = jnp.sum(acc[j][:, i * ATT_BLK:(i + 1) * ATT_BLK], axis=1, keepdims=True)
                dsink = dsink - jnp.where(lane1 == j * Q_PER_KV + i, part, 0.0)

        @pl.when(b == 0)
        def _():
            ds_ref[...] = jnp.zeros_like(ds_ref)

        ds_ref[...] += dsink

        def post(n, carry):
            r0 = pl.multiple_of(n * ATT_BLK, ATT_BLK)
            cs = cos_ref[pl.ds(r0, ATT_BLK), :]
            sn = sin_ref[pl.ds(r0, ATT_BLK), :]
            dq_ref[pl.ds(r0, ATT_BLK), :] = _unrope(_from_qt(dqt_sc[n]), cs, sn).astype(BF16)
            dk_ref[pl.ds(r0, ATT_BLK), :] = _unrope(dk_sc[pl.ds(r0 + ATT_BLK, ATT_BLK), :], cs, sn).astype(BF16)
            dv_ref[pl.ds(r0, ATT_BLK), :] = dv_sc[pl.ds(r0 + ATT_BLK, ATT_BLK), :].astype(BF16)
            return carry

        lax.fori_loop(0, nb, post, 0)

    act = jax.ShapeDtypeStruct((bsz, s, D_ATTN), BF16)
    kvs = jax.ShapeDtypeStruct((bsz, s, D_KV), BF16)
    return pl.pallas_call(
        body, name="attn_bwd", grid=(N_PAIRS, bsz),
        out_shape=(act, kvs, kvs, act, jax.ShapeDtypeStruct((N_PAIRS, 1, LANES), F32)),
        in_specs=[sp["smem"], sp["q1"], sp["k"], sp["v"], sp["g1"], sp["act1"], sp["act1"], sp["tbl"], sp["tbl"]],
        out_specs=(sp["act"], sp["kv"], sp["kv"], sp["act"], pl.BlockSpec((None, 1, LANES), lambda p, b: (p, 0, 0))),
        scratch_shapes=_attn_scratch(s, True) + [pltpu.VMEM((nb, LANES, QT_COLS), BF16),
                                                 pltpu.VMEM((nb, LANES, QT_COLS), F32),
                                                 pltpu.VMEM((s + ATT_BLK, LANES), F32),
                                                 pltpu.VMEM((s + ATT_BLK, LANES), F32)],
        compiler_params=_params())(sinks, proj3, proj3, proj3, proj3, o3, dy3, cosf, sinf)


def _merge_fwd_bwd(x, tgt, y_rnn, y_attn, proj, w_r, w_a, w_o, gf):
    t, d = x.shape
    tm = min(t, 256)

    hw = d // 2

    def body(x_ref, t_ref, yr_ref, ya_ref, mr0_ref, mr1_ref, ma0_ref, ma1_ref, wr_ref, wa_ref, wo_ref, gf_ref,
             dmg_ref, dyr_ref, dya_ref, mg_ref, dx2_ref, dx2b_ref, dpr_ref, dpa_ref, loss_ref, dgf_ref):
        i = pl.program_id(0)
        wr = wr_ref[...]
        wa = wa_ref[...]
        wo = wo_ref[...]
        gfv = gf_ref[...]
        pr = _dot(yr_ref[...], wr)
        pa = _dot(ya_ref[...], wa)
        sr = _sigmoid(jnp.concatenate([mr0_ref[...], mr1_ref[...]], axis=1))
        sa = _sigmoid(jnp.concatenate([ma0_ref[...], ma1_ref[...]], axis=1))
        mb = (sr * pr + sa * pa).astype(BF16)
        mg_ref[...] = mb
        x2 = x_ref[...] + _dot(mb, wo)
        r2 = lax.rsqrt(jnp.mean(x2 * x2, axis=-1, keepdims=True) + NORM_EPS)
        nrm = x2 * r2
        err = nrm * gfv - t_ref[...]
        dy = err * (1.0 / d)
        dn = dy * gfv
        dx2 = r2 * (dn - nrm * jnp.mean(dn * nrm, axis=-1, keepdims=True))
        dx2_ref[...] = dx2
        dx2b = dx2.astype(BF16)
        dx2b_ref[...] = dx2b
        dmerged = _dot_nt(dx2b, wo)
        dpr = (dmerged * sr).astype(BF16)
        dpa = (dmerged * sa).astype(BF16)
        dpr_ref[...] = dpr
        dpa_ref[...] = dpa
        dmg_ref[:, 0:d] = (dmerged * pr * (sr * (1.0 - sr))).astype(BF16)
        dmg_ref[:, d:2 * d] = (dmerged * pa * (sa * (1.0 - sa))).astype(BF16)
        dyr_ref[...] = _dot_nt(dpr, wr)
        dya_ref[...] = _dot_nt(dpa, wa)

        @pl.when(i == 0)
        def _():
            loss_ref[...] = jnp.zeros_like(loss_ref)
            dgf_ref[...] = jnp.zeros_like(dgf_ref)

        loss_ref[...] += jnp.full((1, LANES), 0.5 / d, F32) * jnp.sum(err * err)
        dgf_ref[...] += jnp.sum(dy * nrm, axis=0, keepdims=True)

    tile = pl.BlockSpec((tm, d), lambda i: (i, 0))
    wsp = pl.BlockSpec((d, d), lambda i: (0, 0))

    def gate(col_blk):
        return pl.BlockSpec((tm, hw), lambda i: (i, col_blk))

    fb = jax.ShapeDtypeStruct((t, d), BF16)
    ff = jax.ShapeDtypeStruct((t, d), F32)
    return pl.pallas_call(
        body, name="merge_fwd_bwd", grid=(t // tm,),
        out_shape=(jax.ShapeDtypeStruct((t, 2 * d), BF16), ff, ff, fb, ff, fb, fb, fb,
                   jax.ShapeDtypeStruct((1, LANES), F32), jax.ShapeDtypeStruct((1, d), F32)),
        in_specs=[tile, tile, tile, tile] + [gate(OFF_MERGE_R // hw + j) for j in range(4)] + [
            wsp, wsp, wsp, pl.BlockSpec((1, d), lambda i: (0, 0))],
        out_specs=(pl.BlockSpec((tm, 2 * d), lambda i: (i, 0)), tile, tile, tile, tile, tile, tile, tile,
                   pl.BlockSpec((1, LANES), lambda i: (0, 0)), pl.BlockSpec((1, d), lambda i: (0, 0))),
        compiler_params=_params())(x, tgt, y_rnn, y_attn, proj, proj, proj, proj, w_r, w_a, w_o, gf)


def _local_grads(x, tgt, h, proj, norm_g, w_in_bm, conv_w, conv_b, lru_w_a, lru_b_a, lru_w_x, lru_b_x, lam, sinks,
                 w_r, w_a, w_o, gf):
    bsz, s, d = x.shape
    t = bsz * s
    x2 = x.reshape(t, d)
    proj3 = proj.reshape(bsz, s, D_IN)
    h_lru, y_rnn = _lru_fwd(proj3, conv_w, conv_b, lru_w_a, lru_b_a, lru_w_x, lru_b_x, lam)
    cosf, sinf = _rope_tables(s)
    o_attn, y_attn = _attn_fwd(proj3, sinks, cosf, sinf)
    y_rnn2 = y_rnn.reshape(t, d)
    y_attn2 = y_attn.reshape(t, d)
    dmg, dyr, dya, merged, dx2, dx2b, dpr, dpa, loss, dgf = _merge_fwd_bwd(
        x2, tgt.reshape(t, d), y_rnn2, y_attn2, proj, w_r, w_a, w_o, gf)
    dup, dgr, dcw, dcb, dwa, dba, dwx, dbx, dlam = _lru_bwd(
        proj3, h_lru, dyr.reshape(bsz, s, d), conv_w, conv_b, lru_w_a, lru_b_a, lru_w_x, lru_b_x, lam)
    dq, dk, dv, dga, dsink = _attn_bwd(proj3, o_attn, dya.reshape(bsz, s, d), sinks, cosf, sinf)
    dproj = jnp.concatenate([dup, dgr, dq, dk, dv, dga, dmg.reshape(bsz, s, 2 * d)], axis=-1).reshape(t, D_IN)
    dh = _grad_h(dproj, w_in_bm)
    grad_x, dng = _rmsnorm_bwd(x2, dh, dx2, norm_g)
    small = dict(norm_g=dng, conv_w=dcw, conv_b=dcb, lru_w_a=dwa, lru_b_a=dba, lru_w_x=dwx, lru_b_x=dbx,
                 lru_lambda=dlam, attn_sinks=dsink[:, 0, :HEADS_PER_STEP].reshape(1, N_Q_HEADS), final_norm_g=dgf)
    squares = [(y_rnn2, dpr), (y_attn2, dpa), (merged, dx2b)]
    return loss[0, 0], grad_x.reshape(bsz, s, d), h, dproj, squares, small


ANY = pl.BlockSpec(memory_space=pl.ANY)


def _mesh_pos():
    return lax.axis_index("x"), lax.axis_index("y"), lax.axis_index("c")


def _remote(src, dst, send_sems, recv_sems, idx, peer):
    return pltpu.make_async_remote_copy(src_ref=src, dst_ref=dst, send_sem=send_sems.at[idx],
                                        recv_sem=recv_sems.at[idx], device_id=peer, device_id_type=MESH)


def _gather_in_proj(h, bufs, split, idx):
    t_tok, d = h.shape
    n = len(bufs)
    tm = min(t_tok, 512)
    nt = t_tok // tm
    n_fwd = 3 * sum(split)
    assert split[0]

    def body(idx_ref, h_ref, *refs):
        ins, proj_ref, outs = refs[:n], refs[n], refs[n + 1:2 * n + 1]
        wbuf, send_sems, recv_sems, fsend_sems, frecv_sems, l_sems = refs[2 * n + 1:]
        j, i = pl.program_id(0), pl.program_id(1)
        x, y, c = _mesh_pos()
        me = 2 * x + y
        sib = (x, y, 1 - c)
        peers = [((x, 1 - y, c), me ^ 1), ((1 - x, y, c), me ^ 2), ((1 - x, 1 - y, c), me ^ 3)]

        def part(ref, slot, t, half):
            if not split[t]:
                return ref.at[slot]
            hr = bufs[t].shape[1] // 2
            return ref.at[slot, pl.ds(pl.multiple_of(half * hr, 8), hr), :]

        def land(t):
            return wbuf if t == 0 else outs[t]

        def ici(t, k):
            peer, pj = peers[k]
            src = part(ins[t], me, t, c)
            return (_remote(src, part(land(t), me, t, c), send_sems, recv_sems, 3 * t + k, peer),
                    _remote(src, part(land(t), pj, t, c), send_sems, recv_sems, 3 * t + k, peer))

        fwd_index = {}
        for t in range(n):
            if split[t]:
                for k in range(3):
                    fwd_index[(t, k)] = len(fwd_index)

        def forward(t, k):
            pj = peers[k][1]
            got = part(land(t), pj, t, c)
            f = fwd_index[(t, k)]
            return (_remote(got, got, fsend_sems, frecv_sems, f, sib),
                    _remote(got, part(land(t), pj, t, 1 - c), fsend_sems, frecv_sems, f, sib))

        def write_back(k):
            pj = peers[k][1]
            return pltpu.make_async_copy(wbuf.at[pj], outs[0].at[pj], l_sems.at[1 + k])

        relay_peer = ((x + c) % 2, (y + 1 - c) % 2, c)

        def relay():
            got = part(wbuf, me ^ (2 - c), 0, c)
            return (_remote(got, got, send_sems, recv_sems, 2, relay_peer),
                    _remote(got, part(wbuf, me ^ 3, 0, c), send_sems, recv_sems, 2, relay_peer))

        direct = [(t, k) for t in range(n) for k in range(3) if (t, k) != (0, 2)]

        @pl.when((j == 0) & (i == 0))
        def _():
            for t, k in direct:
                ici(t, k)[0].start()
            own = pltpu.make_async_copy(ins[0].at[me], wbuf.at[me], l_sems.at[0])
            own.start()
            own.wait()

        @pl.when((j == 1) & (i == 0))
        def _():
            for k in range(2):
                ici(0, k)[1].wait_recv()
            relay()[0].start()
            for k in range(2):
                forward(0, k)[0].start()
            forward(0, 0)[1].wait_recv()
            write_back(0).start()

        @pl.when((j == 2) & (i == 0))
        def _():
            forward(0, 1)[1].wait_recv()
            write_back(1).start()

        @pl.when((j == 3) & (i == 0))
        def _():
            relay()[1].wait_recv()
            forward(0, 2)[0].start()
            forward(0, 2)[1].wait_recv()
            write_back(2).start()

        proj_ref[...] = _dot(h_ref[...], wbuf[me ^ j])

        @pl.when((j == N_CHIPS - 1) & (i == nt - 1))
        def _():
            for t in range(1, n):
                for k in range(3):
                    ici(t, k)[1].wait_recv()
                    if split[t]:
                        forward(t, k)[0].start()
            relay()[0].wait_send()
            for t, k in direct:
                ici(t, k)[0].wait_send()
            for t in range(n):
                if split[t]:
                    for k in range(3):
                        forward(t, k)[0].wait_send()
                        if t > 0:
                            forward(t, k)[1].wait_recv()
            for k in range(3):
                write_back(k).wait()

    grid_spec = pltpu.PrefetchScalarGridSpec(
        num_scalar_prefetch=1, grid=(N_CHIPS, nt),
        in_specs=[pl.BlockSpec((tm, d), lambda j, i, idx_ref: (i, 0))] + [ANY] * n,
        out_specs=[pl.BlockSpec((tm, W_BLK), lambda j, i, idx_ref: (i, idx_ref[0] ^ j))] + [ANY] * n,
        scratch_shapes=[pltpu.VMEM(bufs[0].shape, bufs[0].dtype),
                        pltpu.SemaphoreType.DMA((3 * n,)), pltpu.SemaphoreType.DMA((3 * n,)),
                        pltpu.SemaphoreType.DMA((n_fwd,)), pltpu.SemaphoreType.DMA((n_fwd,)),
                        pltpu.SemaphoreType.DMA((4,))])
    out_shape = [jax.ShapeDtypeStruct((t_tok, D_IN), F32)] + [jax.ShapeDtypeStruct(a.shape, a.dtype) for a in bufs]
    res = pl.pallas_call(
        body, name="gather_in_proj", grid_spec=grid_spec, out_shape=out_shape,
        input_output_aliases={2 + t: 1 + t for t in range(n)}, compiler_params=_params())(idx, h, *bufs)
    return res[0], res[1:]


def _row_tile(rows, row_bytes, cap_bytes=2 * 1024 * 1024):
    best = None
    for tr in range(8, rows + 1, 8):
        if rows % tr == 0 and tr * row_bytes <= cap_bytes:
            best = tr
    return best if best is not None else rows


XOR_ORDER = (3, 2, 1)


def _grads_reduce_scatter(h, dproj, squares, small, idx):
    t, d = h.shape
    nsq = len(squares)
    hr = d // 2
    qr = ROW_BLK // 2
    tk = min(t, 1024)
    nk = t // tk
    last = N_CHIPS - 1
    n_phase = 3

    def dest(s, idx_ref):
        xo = jnp.where(s == 0, XOR_ORDER[0], jnp.where(s == 1, XOR_ORDER[1], jnp.where(s == 2, XOR_ORDER[2], 0)))
        return idx_ref[0] ^ xo

    def k_sq(p, k):
        return jnp.where(p == 0, k, nk - 1)

    def k_w(p, k):
        return jnp.where(p == 0, 0, k)

    in_specs = [
        pl.BlockSpec((tk, hr), lambda s, p, k, idx_ref: (k_w(p, k), (1 - idx_ref[1] + jnp.maximum(p - 1, 0)) % 2)),
        pl.BlockSpec((tk, W_BLK), lambda s, p, k, idx_ref: (k_w(p, k), dest(s, idx_ref)))]
    for q in range(nsq):
        in_specs.append(pl.BlockSpec((tk, ROW_BLK), lambda s, p, k, idx_ref: (k_sq(p, k), dest(s, idx_ref))))
        in_specs.append(pl.BlockSpec((tk, d), lambda s, p, k, idx_ref: (k_sq(p, k), 0)))

    def body(idx_ref, *refs):
        nj = 1 + nsq
        h_ref, dp_ref = refs[0], refs[1]
        sq_in = refs[2:2 + 2 * nsq]
        small_in = refs[2 * nj]
        outs = refs[2 * nj + 1:3 * nj + 2]
        landing = refs[3 * nj + 2:4 * nj + 3]
        sc = refs[4 * nj + 3:]
        acc_w, xr_w, sb_w = sc[0:3]
        sq_sc = [sc[3 + 3 * q:6 + 3 * q] for q in range(nsq)]
        sm, smx = sc[3 * nj:3 * nj + 2]
        x_send, x_recv, i_send, i_recv, f_send, f_recv, o_sem, l_sem = sc[3 * nj + 2:]
        s, p, k = pl.program_id(0), pl.program_id(1), pl.program_id(2)
        x, y, c = _mesh_pos()
        sib = (x, y, 1 - c)
        peers = [((1 - x) if xo & 2 else x, (1 - y) if xo & 1 else y, c) for xo in XOR_ORDER]
        slot = s % 2
        mine_w = pl.ds(pl.multiple_of(c * hr, 8), hr)
        theirs_w = pl.ds(pl.multiple_of((1 - c) * hr, 8), hr)
        mine_q = pl.ds(pl.multiple_of(c * qr, 8), qr)
        theirs_q = pl.ds(pl.multiple_of((1 - c) * qr, 8), qr)

        def exch(j, src, dst):
            return _remote(src, dst, x_send, x_recv, 2 * j + slot, sib)

        sbufs = [sb_w] + [sq_sc[q][2] for q in range(nsq)]

        def ici(j, ss):
            return _remote(sbufs[j].at[ss], landing[j].at[ss], i_send, i_recv, last * j + ss, peers[ss])

        def exchanges():
            cps = [exch(0, acc_w.at[0], xr_w.at[slot])]
            cps += [exch(1 + q, sq_sc[q][0].at[theirs_q, :], sq_sc[q][1].at[slot]) for q in range(nsq)]
            return cps

        def small_send(ss):
            return _remote(sm.at[c], landing[nj].at[ss], i_send, i_recv, last * nj + ss, peers[ss])

        def small_start():
            load = pltpu.make_async_copy(small_in, sm, l_sem.at[nj + 1])
            load.start()
            load.wait()
            swap = _remote(sm, smx.at[pl.ds(0, 2)], x_send, x_recv, 2 * nj, sib)
            swap.start()
            swap.wait_recv()
            swap.wait_send()
            sm[...] = sm[...] + smx[0:2]
            for ss in range(last):
                small_send(ss).start()

        def pair_ref(j):
            return acc_w.at[1] if j == 0 else sq_sc[j - 1][0].at[mine_q, :]

        def sq_phase():
            pl.when((s == 0) & (k == 0))(small_start)
            for q in range(nsq):
                acc = sq_sc[q][0]

                @pl.when(k == 0)
                def _():
                    acc[...] = jnp.zeros((ROW_BLK, d), F32)

                acc[...] += _dot_tn(sq_in[2 * q][...], sq_in[2 * q + 1][...])

            @pl.when(k == nk - 1)
            def _():
                for cp in exchanges()[1:]:
                    cp.start()

        def w_phase(hf):
            @pl.when(k == 0)
            def _():
                acc_w[hf] = jnp.zeros((hr, W_BLK), F32)

            acc_w[hf] += _dot_tn(h_ref[...], dp_ref[...])

            @pl.when(k == nk - 1)
            def _():
                if hf == 0:
                    exchanges()[0].start()
                else:
                    finish_step()

        def finish_step():
            for cp in exchanges():
                cp.wait_recv()
                cp.wait_send()
            acc_w[1] += xr_w[slot]
            for q in range(nsq):
                sq_sc[q][0][mine_q, :] += sq_sc[q][1][slot]
            for ss in range(last):
                @pl.when(s == ss)
                def _():
                    for j in range(nj):
                        sbufs[j][ss] = pair_ref(j)[...].astype(BF16)
                        ici(j, ss).start()

            @pl.when(s == last)
            def _():
                for ss in range(last):
                    for j in range(nj):
                        ici(j, ss).wait_recv()
                        ici(j, ss).wait_send()
                    small_send(ss).wait_recv()
                    small_send(ss).wait_send()
                stage = [pltpu.make_async_copy(landing[j], sbufs[j], l_sem.at[j]) for j in range(nj)]
                stage.append(pltpu.make_async_copy(landing[nj], smx, l_sem.at[nj]))
                for cp in stage:
                    cp.start()
                for j in range(nj):
                    stage[j].wait()
                    total = pair_ref(j)[...]
                    for ss in range(last):
                        total = total + sbufs[j][ss].astype(F32)
                    pair_ref(j)[...] = total
                stage[nj].wait()
                by_xor = {xo: smx[ss] for ss, xo in enumerate(XOR_ORDER)}
                sm[c] = (sm[c] + by_xor[1]) + (by_xor[2] + by_xor[3])
                done = [(acc_w.at[1], outs[0].at[mine_w, :], outs[0].at[theirs_w, :])]
                done += [(pair_ref(1 + q), outs[1 + q].at[mine_q, :], outs[1 + q].at[theirs_q, :]) for q in range(nsq)]
                done.append((sm.at[c], outs[nj].at[c], outs[nj].at[1 - c]))
                copies = []
                for j, (src, mine, theirs) in enumerate(done):
                    keep = pltpu.make_async_copy(src, mine, o_sem.at[j])
                    give = _remote(src, mine, f_send, f_recv, j, sib)
                    take = _remote(src, theirs, f_send, f_recv, j, sib)
                    keep.start()
                    give.start()
                    copies.append((keep, give, take))
                for keep, give, take in copies:
                    keep.wait()
                    give.wait_send()
                    take.wait_recv()

        pl.when(p == 0)(sq_phase)
        for hf in range(2):
            pl.when(p == 1 + hf)(functools.partial(w_phase, hf))

    nj = 1 + nsq
    scratch = [pltpu.VMEM((2, hr, W_BLK), F32), pltpu.VMEM((2, hr, W_BLK), F32), pltpu.VMEM((last, hr, W_BLK), BF16)]
    for _ in range(nsq):
        scratch += [pltpu.VMEM((ROW_BLK, d), F32), pltpu.VMEM((2, qr, d), F32), pltpu.VMEM((last, qr, d), BF16)]
    scratch += [pltpu.VMEM((2, PK_HALF, LANES), F32), pltpu.VMEM((last, PK_HALF, LANES), F32)]
    scratch += [pltpu.SemaphoreType.DMA((2 * nj + 1,)), pltpu.SemaphoreType.DMA((2 * nj + 1,)),
                pltpu.SemaphoreType.DMA((last * (nj + 1),)), pltpu.SemaphoreType.DMA((last * (nj + 1),)),
                pltpu.SemaphoreType.DMA((nj + 1,)), pltpu.SemaphoreType.DMA((nj + 1,)),
                pltpu.SemaphoreType.DMA((nj + 1,)), pltpu.SemaphoreType.DMA((nj + 2,))]
    grid_spec = pltpu.PrefetchScalarGridSpec(
        num_scalar_prefetch=1, grid=(N_CHIPS, n_phase, nk), in_specs=in_specs + [ANY],
        out_specs=[ANY] * (2 * nj + 2), scratch_shapes=scratch)
    out_shape = [jax.ShapeDtypeStruct((d, W_BLK), F32)] + [jax.ShapeDtypeStruct((ROW_BLK, d), F32)] * nsq
    out_shape.append(jax.ShapeDtypeStruct((2, PK_HALF, LANES), F32))
    out_shape += [jax.ShapeDtypeStruct((last, hr, W_BLK), BF16)] + [jax.ShapeDtypeStruct((last, qr, d), BF16)] * nsq
    out_shape.append(jax.ShapeDtypeStruct((last, PK_HALF, LANES), F32))
    flat = [a for pair in squares for a in pair]
    res = pl.pallas_call(body, name="grads_reduce_scatter", grid_spec=grid_spec, out_shape=out_shape,
                         compiler_params=_params())(idx, h, dproj, *flat, small)
    return res[:nj + 1]


_VEC_NAMES = ("norm_g", "conv_b", "lru_b_a", "lru_b_x", "lru_lambda", "final_norm_g")


def _pack_small(p, conv_full=None, scalar=None):
    rows = [p["lru_w_a"].reshape(PK_WX - PK_WA, LANES), p["lru_w_x"].reshape(PK_VEC - PK_WX, LANES)]
    rows += [p[k].reshape(8, LANES) for k in _VEC_NAMES]
    rows.append(jnp.pad(p["attn_sinks"].reshape(1, N_Q_HEADS), ((0, 7), (0, LANES - N_Q_HEADS))))
    rows.append(jnp.zeros((32, LANES), F32) if conv_full is None else conv_full.reshape(32, LANES))
    tail = PK_ROWS - PK_SCALAR
    if scalar is None:
        rows.append(jnp.zeros((tail, LANES), F32))
    else:
        rows.append(jnp.pad(scalar.reshape(1, 1), ((0, tail - 1), (0, LANES - 1))))
    return jnp.concatenate(rows, axis=0)


def _unpack_small(pk, like):
    out = {"lru_w_a": pk[PK_WA:PK_WX].reshape(like["lru_w_a"].shape),
           "lru_w_x": pk[PK_WX:PK_VEC].reshape(like["lru_w_x"].shape)}
    for j, k in enumerate(_VEC_NAMES):
        out[k] = pk[PK_VEC + 8 * j:PK_VEC + 8 * j + 8].reshape(like[k].shape)
    out["attn_sinks"] = pk[PK_SINK:PK_SINK + 1, :N_Q_HEADS].reshape(like["attn_sinks"].shape)
    return out


_WEIGHTS = ("norm_g", "w_in", "conv_w", "conv_b", "lru_w_a", "lru_b_a", "lru_w_x", "lru_b_x", "lru_lambda",
            "attn_sinks", "w_rnn_out", "w_attn_out", "w_o", "final_norm_g")
_SMALL = ("norm_g", "conv_b", "lru_w_a", "lru_b_a", "lru_w_x", "lru_b_x", "lru_lambda", "attn_sinks", "final_norm_g")
_ROW_SHARDED = ("w_rnn_out", "w_attn_out", "w_o")


def kernel(x, norm_g, w_in, conv_w, conv_b, lru_w_a, lru_b_a, lru_w_x, lru_b_x, lru_lambda, attn_sinks, w_rnn_out, w_attn_out, w_o, final_norm_g, loss_target, m_norm_g, m_w_in, m_conv_w, m_conv_b, m_lru_w_a, m_lru_b_a, m_lru_w_x, m_lru_b_x, m_lru_lambda, m_attn_sinks, m_w_rnn_out, m_w_attn_out, m_w_o, m_final_norm_g, v_norm_g, v_w_in, v_conv_w, v_conv_b, v_lru_w_a, v_lru_b_a, v_lru_w_x, v_lru_b_x, v_lru_lambda, v_attn_sinks, v_w_rnn_out, v_w_attn_out, v_w_o, v_final_norm_g):
    w = dict(norm_g=norm_g, w_in=w_in, conv_w=conv_w, conv_b=conv_b, lru_w_a=lru_w_a, lru_b_a=lru_b_a, lru_w_x=lru_w_x,
             lru_b_x=lru_b_x, lru_lambda=lru_lambda, attn_sinks=attn_sinks, w_rnn_out=w_rnn_out, w_attn_out=w_attn_out,
             w_o=w_o, final_norm_g=final_norm_g)
    m = dict(norm_g=m_norm_g, w_in=m_w_in, conv_w=m_conv_w, conv_b=m_conv_b, lru_w_a=m_lru_w_a, lru_b_a=m_lru_b_a,
             lru_w_x=m_lru_w_x, lru_b_x=m_lru_b_x, lru_lambda=m_lru_lambda, attn_sinks=m_attn_sinks,
             w_rnn_out=m_w_rnn_out, w_attn_out=m_w_attn_out, w_o=m_w_o, final_norm_g=m_final_norm_g)
    v = dict(norm_g=v_norm_g, w_in=v_w_in, conv_w=v_conv_w, conv_b=v_conv_b, lru_w_a=v_lru_w_a, lru_b_a=v_lru_b_a,
             lru_w_x=v_lru_w_x, lru_b_x=v_lru_b_x, lru_lambda=v_lru_lambda, attn_sinks=v_attn_sinks,
             w_rnn_out=v_w_rnn_out, w_attn_out=v_w_attn_out, w_o=v_w_o, final_norm_g=v_final_norm_g)
    mx, my, mc = _mesh_pos()
    me = 2 * mx + my
    d = D_MODEL

    slot0 = jnp.stack([me, jnp.zeros_like(me)]).astype(jnp.int32)
    bufs = [_put_slot(w[k][0], N_CHIPS, slot0, w[k].shape[1], BF16, "cast_" + k) for k in ("w_in",) + _ROW_SHARDED]
    bufs.append(_put_slot(w["conv_w"][0], N_CHIPS, slot0, CONV_WIDTH, F32, "slot_conv_w"))
    h = _rmsnorm_fwd(x.reshape(-1, d), w["norm_g"])
    proj, (g_in, g_r, g_a, g_o, g_cw) = _gather_in_proj(h, bufs, [True, True, True, True, False],
                                                        jnp.reshape(me, (1,)).astype(jnp.int32))
    conv_full = g_cw.transpose(1, 0, 2).reshape(CONV_WIDTH, D_RNN)

    loss_local, grad_x, h, dproj, squares, gsmall = _local_grads(
        x, loss_target, h, proj, w["norm_g"], g_in, conv_full, w["conv_b"], w["lru_w_a"][0], w["lru_b_a"], w["lru_w_x"][0],
        w["lru_b_x"], w["lru_lambda"], w["attn_sinks"][0], g_r.reshape(d, d), g_a.reshape(d, d), g_o.reshape(d, d),
        w["final_norm_g"].reshape(1, d))
    gpack = _pack_small(gsmall, gsmall["conv_w"], loss_local).reshape(2, PK_HALF, LANES)
    f_in, f_r, f_a, f_o, spack = _grads_reduce_scatter(h, dproj, squares, gpack, jnp.stack([me, mc]).astype(jnp.int32))
    spack = spack.reshape(PK_ROWS, LANES)
    loss = spack[PK_SCALAR, 0]

    grads = _unpack_small(spack, w)
    conv_all = spack[PK_CONV:PK_CONV + 32].reshape(CONV_WIDTH, D_RNN)
    grads["conv_w"] = lax.dynamic_slice_in_dim(conv_all, me * (D_RNN // N_CHIPS), D_RNN // N_CHIPS, axis=1)[None]
    grads["w_in"] = f_in[None]
    grads["w_rnn_out"], grads["w_attn_out"], grads["w_o"] = f_r[None], f_a[None], f_o[None]

    delta, new_m, new_v = {}, {}, {}
    for k in ("w_in",) + _ROW_SHARDED:
        dk, mk, vk = _adamw(w[k][0], grads[k][0], m[k][0], v[k][0], "adamw_" + k)
        delta[k], new_m[k], new_v[k] = dk[None], mk[None], vk[None]
    shp = (2 * CONV_WIDTH, LANES)
    dk, mk, vk = _adamw(w["conv_w"].reshape(shp), grads["conv_w"].reshape(shp), m["conv_w"].reshape(shp),
                        v["conv_w"].reshape(shp), "adamw_conv_w")
    delta["conv_w"], new_m["conv_w"], new_v["conv_w"] = (a.reshape(w["conv_w"].shape) for a in (dk, mk, vk))
    dk, mk, vk = _adamw(_pack_small(w), spack, _pack_small(m), _pack_small(v), "adamw_small")
    for src, dst in ((dk, delta), (mk, new_m), (vk, new_v)):
        dst.update(_unpack_small(src, w))

    return (loss, grad_x, *[grads[k] for k in _WEIGHTS], *[delta[k] for k in _WEIGHTS],
            *[new_m[k] for k in _WEIGHTS], *[new_v[k] for k in _WEIGHTS])
```

```python
import functools
import math

import jax
import jax.numpy as jnp
from jax import lax
from jax.experimental import pallas as pl
from jax.experimental.pallas import tpu as pltpu

F32 = jnp.float32
BF16 = jnp.bfloat16
MESH = pl.DeviceIdType.MESH

D_MODEL = 1024
D_RNN = 1024
N_RNN_BLOCKS = 8
RNN_BLOCK = D_RNN // N_RNN_BLOCKS
CONV_WIDTH = 4
LRU_C = 8.0
HEAD_DIM = 64
N_Q_HEADS = 16
N_KV_HEADS = 4
D_ATTN = N_Q_HEADS * HEAD_DIM
D_KV = N_KV_HEADS * HEAD_DIM
WINDOW = 128
ROPE_DIM = HEAD_DIM // 4
ROPE_THETA = 500000.0
NORM_EPS = 1e-6
OFF_RNN_X = 0
OFF_RNN_G = OFF_RNN_X + D_RNN
OFF_Q = OFF_RNN_G + D_RNN
OFF_K = OFF_Q + D_ATTN
OFF_V = OFF_K + D_KV
OFF_ATTN_G = OFF_V + D_KV
OFF_MERGE_R = OFF_ATTN_G + D_ATTN
OFF_MERGE_A = OFF_MERGE_R + D_MODEL
D_IN = OFF_MERGE_A + D_MODEL

ADAM_LR = 0.001
ADAM_B1 = 0.9
ADAM_B2 = 0.999
ADAM_EPS = 1e-08
ADAM_WD = 0.01
ADAM_STEP = 10

N_CHIPS = 4
W_BLK = D_IN // N_CHIPS
ROW_BLK = D_MODEL // N_CHIPS
LANES = 128
ATT_BLK = 128
VMEM_LIMIT = 56 * 1024 * 1024
NEG_BIG = -1e30
ATTN_SCALE = 1.0 / math.sqrt(HEAD_DIM)

PK_WA = 0
PK_WX = PK_WA + N_RNN_BLOCKS * RNN_BLOCK
PK_VEC = PK_WX + N_RNN_BLOCKS * RNN_BLOCK
PK_SINK = PK_VEC + 6 * 8
PK_CONV = PK_SINK + 8
PK_SCALAR = PK_CONV + 32
PK_ROWS = PK_SCALAR + 8
PK_HALF = PK_ROWS // 2


def _params(**kw):
    return pltpu.CompilerParams(vmem_limit_bytes=VMEM_LIMIT, **kw)


def _sigmoid(z):
    return 1.0 / (1.0 + jnp.exp(-z))


def _dot(a, b):
    return jnp.dot(a, b, preferred_element_type=F32)


def _dot_nt(a, b):
    return lax.dot_general(a, b, (((1,), (1,)), ((), ())), preferred_element_type=F32)


def _dot_tn(a, b):
    return lax.dot_general(a, b, (((0,), (0,)), ((), ())), preferred_element_type=F32)


def _put_slot(src, n_slots, slot_and_blk, rows, dtype, name):
    _, c = src.shape
    tr = _row_tile(rows, c * 4)
    steps = rows // tr

    def body(idx_ref, s_ref, o_ref):
        o_ref[...] = s_ref[...].astype(dtype)

    grid_spec = pltpu.PrefetchScalarGridSpec(
        num_scalar_prefetch=1, grid=(steps,),
        in_specs=[pl.BlockSpec((tr, c), lambda i, idx_ref: (idx_ref[1] * steps + i, 0))],
        out_specs=pl.BlockSpec((None, tr, c), lambda i, idx_ref: (idx_ref[0], i, 0)))
    return pl.pallas_call(body, name=name, grid_spec=grid_spec,
                          out_shape=jax.ShapeDtypeStruct((n_slots, rows, c), dtype),
                          compiler_params=_params())(slot_and_blk, src)


def _rmsnorm_fwd(x, g):
    t, d = x.shape
    tm = min(t, 512)

    def body(x_ref, g_ref, o_ref):
        xv = x_ref[...]
        r = lax.rsqrt(jnp.mean(xv * xv, axis=-1, keepdims=True) + NORM_EPS)
        o_ref[...] = (xv * r * g_ref[...]).astype(BF16)

    return pl.pallas_call(
        body, name="rmsnorm_fwd", grid=(t // tm,), out_shape=jax.ShapeDtypeStruct((t, d), BF16),
        in_specs=[pl.BlockSpec((tm, d), lambda i: (i, 0)), pl.BlockSpec((1, d), lambda i: (0, 0))],
        out_specs=pl.BlockSpec((tm, d), lambda i: (i, 0)), compiler_params=_params())(x, g)


def _rmsnorm_bwd(x, dh, dx2, g):
    t, d = x.shape
    tm = min(t, 512)

    def body(x_ref, dh_ref, dx2_ref, g_ref, gx_ref, dg_ref):
        i = pl.program_id(0)
        xv = x_ref[...]
        dhv = dh_ref[...]
        r = lax.rsqrt(jnp.mean(xv * xv, axis=-1, keepdims=True) + NORM_EPS)
        nrm = xv * r
        dn = dhv * g_ref[...]
        gx_ref[...] = dx2_ref[...] + r * (dn - nrm * jnp.mean(dn * nrm, axis=-1, keepdims=True))

        @pl.when(i == 0)
        def _():
            dg_ref[...] = jnp.zeros_like(dg_ref)

        dg_ref[...] += jnp.sum(dhv * nrm, axis=0, keepdims=True)

    return pl.pallas_call(
        body, name="rmsnorm_bwd", grid=(t // tm,),
        out_shape=(jax.ShapeDtypeStruct((t, d), F32), jax.ShapeDtypeStruct((1, d), F32)),
        in_specs=[pl.BlockSpec((tm, d), lambda i: (i, 0)), pl.BlockSpec((tm, d), lambda i: (i, 0)),
                  pl.BlockSpec((tm, d), lambda i: (i, 0)), pl.BlockSpec((1, d), lambda i: (0, 0))],
        out_specs=(pl.BlockSpec((tm, d), lambda i: (i, 0)), pl.BlockSpec((1, d), lambda i: (0, 0))),
        compiler_params=_params())(x, dh, dx2, g)


def _adamw(w, g, m, v, name):
    r, c = w.shape
    tr = _row_tile(r, c * 4, 1024 * 1024)
    c1 = 1.0 - ADAM_B1 ** ADAM_STEP
    c2 = 1.0 - ADAM_B2 ** ADAM_STEP

    def body(w_ref, g_ref, m_ref, v_ref, d_ref, nm_ref, nv_ref):
        gv = g_ref[...]
        nm = ADAM_B1 * m_ref[...] + (1.0 - ADAM_B1) * gv
        nv = ADAM_B2 * v_ref[...] + (1.0 - ADAM_B2) * (gv * gv)
        m_hat = nm / c1
        v_hat = nv / c2
        d_ref[...] = -ADAM_LR * (m_hat / (jnp.sqrt(v_hat) + ADAM_EPS) + ADAM_WD * w_ref[...])
        nm_ref[...] = nm
        nv_ref[...] = nv

    spec = pl.BlockSpec((tr, c), lambda i: (i, 0))
    sds = jax.ShapeDtypeStruct((r, c), F32)
    return pl.pallas_call(
        body, name=name, grid=(r // tr,), out_shape=(sds, sds, sds),
        in_specs=[spec, spec, spec, spec], out_specs=(spec, spec, spec), compiler_params=_params())(w, g, m, v)


def _grad_h(dproj, w_bm):
    t = dproj.shape[0]
    nb, d, wb = w_bm.shape
    tm = min(t, 1024)

    def body(dp_ref, w_ref, o_ref, acc_ref):
        k = pl.program_id(1)

        @pl.when(k == 0)
        def _():
            acc_ref[...] = jnp.zeros_like(acc_ref)

        acc_ref[...] += _dot_nt(dp_ref[...], w_ref[...])

        @pl.when(k == nb - 1)
        def _():
            o_ref[...] = acc_ref[...]

    return pl.pallas_call(
        body, name="grad_h", grid=(t // tm, nb), out_shape=jax.ShapeDtypeStruct((t, d), F32),
        in_specs=[pl.BlockSpec((tm, wb), lambda i, k: (i, k)), pl.BlockSpec((None, d, wb), lambda i, k: (k, 0, 0))],
        out_specs=pl.BlockSpec((tm, d), lambda i, k: (i, 0)),
        scratch_shapes=[pltpu.VMEM((tm, d), F32)], compiler_params=_params())(dproj, w_bm)


def _shift_down(v, d, fill):
    n = v.shape[0]
    if d % 8 == 0:
        return jnp.concatenate([jnp.full((d,) + v.shape[1:], fill, v.dtype), v[: n - d]], axis=0)
    row = lax.broadcasted_iota(jnp.int32, v.shape, 0)
    return jnp.where(row >= d, pltpu.roll(v, d, axis=0), fill)


def _shift_up(v, d, fill):
    n = v.shape[0]
    if d % 8 == 0:
        return jnp.concatenate([v[d:], jnp.full((d,) + v.shape[1:], fill, v.dtype)], axis=0)
    row = lax.broadcasted_iota(jnp.int32, v.shape, 0)
    return jnp.where(row < n - d, pltpu.roll(v, n - d, axis=0), fill)


def _scan_log(a, b, shift):
    n = a.shape[0]
    d = 1
    while d < n:
        b = a * shift(b, d, 0.0) + b
        if 2 * d < n:
            a = a * shift(a, d, 1.0)
        d *= 2
    return b


SUBLANES = 8


def _scan(a, b, sa_ref, sb_ref, reverse):
    n, c = a.shape
    g = n // SUBLANES
    a3, b3 = a.reshape(g, SUBLANES, c), b.reshape(g, SUBLANES, c)
    sub = lax.broadcasted_iota(jnp.int32, a3.shape, 1)
    d = 1
    while d < SUBLANES:
        keep = (sub < SUBLANES - d) if reverse else (sub >= d)
        amount = SUBLANES - d if reverse else d
        b3 = a3 * jnp.where(keep, pltpu.roll(b3, amount, axis=1), 0.0) + b3
        a3 = a3 * jnp.where(keep, pltpu.roll(a3, amount, axis=1), 1.0)
        d *= 2
    sa_ref[...] = a3.reshape(n, c)
    sb_ref[...] = b3.reshape(n, c)
    edge = 0 if reverse else SUBLANES - 1
    shift = _shift_up if reverse else _shift_down
    totals = _scan_log(sa_ref[pl.ds(edge, g, stride=SUBLANES), :], sb_ref[pl.ds(edge, g, stride=SUBLANES), :], shift)
    carry = shift(totals, 1, 0.0)
    return (a3 * carry[:, None, :] + b3).reshape(n, c)


def _neg_expm1_twice(log_a, a):
    return -jnp.tanh(log_a) * (a * a + 1.0)


def _softplus(z):
    e = jnp.exp(-jnp.abs(z))
    w = 1.0 + e
    log1p = jnp.where(w == 1.0, e, jnp.log(w) * (e / jnp.where(w == 1.0, 1.0, w - 1.0)))
    return jnp.maximum(z, 0.0) + log1p


def _conv(up, cw, cb):
    out = cb + cw[CONV_WIDTH - 1:CONV_WIDTH, :] * up
    for j in range(CONV_WIDTH - 1):
        out = out + cw[j:j + 1, :] * _shift_down(up, CONV_WIDTH - 1 - j, 0.0)
    return out


def _lru_gates(u, wa_ref, ba_ref, wx_ref, bx_ref, lam_ref):
    ub = u.astype(BF16)
    r = _sigmoid(_dot(ub, wa_ref[...].astype(BF16)) + ba_ref[...])
    i = _sigmoid(_dot(ub, wx_ref[...].astype(BF16)) + bx_ref[...])
    sp = _softplus(-lam_ref[...])
    log_a = (-LRU_C) * r * sp
    a = jnp.exp(log_a)
    mult = jnp.sqrt(_neg_expm1_twice(log_a, a))
    return r, i, sp, a, mult


def _lru_specs(s):
    cb = RNN_BLOCK
    vec = pl.BlockSpec((1, cb), lambda n, b: (0, n))
    return dict(
        up=pl.BlockSpec((None, s, cb), lambda n, b: (b, 0, OFF_RNN_X // cb + n)),
        gr=pl.BlockSpec((None, s, cb), lambda n, b: (b, 0, OFF_RNN_G // cb + n)),
        act=pl.BlockSpec((None, s, cb), lambda n, b: (b, 0, n)),
        cw=pl.BlockSpec((CONV_WIDTH, cb), lambda n, b: (0, n)),
        vec=vec,
        wblk=pl.BlockSpec((None, cb, cb), lambda n, b: (n, 0, 0)),
    )


def _lru_fwd(proj3, cw, cb, wa, ba, wx, bx, lam):
    bsz, s, _ = proj3.shape
    sp = _lru_specs(s)

    def body(up_ref, gr_ref, cw_ref, cb_ref, wa_ref, ba_ref, wx_ref, bx_ref, lam_ref, h_ref, y_ref, sa_ref, sb_ref):
        u = _conv(up_ref[...], cw_ref[...], cb_ref[...])
        _, i, _, a, mult = _lru_gates(u, wa_ref, ba_ref, wx_ref, bx_ref, lam_ref)
        h = _scan(a, mult * (i * u), sa_ref, sb_ref, reverse=False)
        h_ref[...] = h
        g = gr_ref[...]
        y_ref[...] = (h * (g * _sigmoid(g))).astype(BF16)

    return pl.pallas_call(
        body, name="lru_fwd", grid=(N_RNN_BLOCKS, bsz),
        out_shape=(jax.ShapeDtypeStruct((bsz, s, D_RNN), F32), jax.ShapeDtypeStruct((bsz, s, D_RNN), BF16)),
        in_specs=[sp["up"], sp["gr"], sp["cw"], sp["vec"], sp["wblk"], sp["vec"], sp["wblk"], sp["vec"], sp["vec"]],
        out_specs=(sp["act"], sp["act"]), scratch_shapes=[pltpu.VMEM((s, RNN_BLOCK), F32)] * 2,
        compiler_params=_params())(proj3, proj3, cw, cb, wa, ba, wx, bx, lam)


def _lru_bwd(proj3, h3, dy3, cw, cb, wa, ba, wx, bx, lam):
    bsz, s, _ = proj3.shape
    sp = _lru_specs(s)

    def body(up_ref, gr_ref, h_ref, dy_ref, cw_ref, cb_ref, wa_ref, ba_ref, wx_ref, bx_ref, lam_ref,
             dup_ref, dgr_ref, dcw_ref, dcb_ref, dwa_ref, dba_ref, dwx_ref, dbx_ref, dlam_ref, sa_ref, sb_ref):
        b = pl.program_id(1)
        up = up_ref[...]
        cwv = cw_ref[...]
        u = _conv(up, cwv, cb_ref[...])
        r, i, spv, a, mult = _lru_gates(u, wa_ref, ba_ref, wx_ref, bx_ref, lam_ref)
        h = h_ref[...]
        g = gr_ref[...]
        dy = dy_ref[...]
        sg = _sigmoid(g)
        dgr_ref[...] = (dy * h * (sg * (1.0 + g * (1.0 - sg)))).astype(BF16)
        dh = dy * (g * sg)
        adj = _scan(_shift_up(a, 1, 0.0), dh, sa_ref, sb_ref, reverse=True)
        da = adj * _shift_down(h, 1, 0.0)
        dmult = adj * (i * u)
        di = adj * mult * u
        du = adj * mult * i
        dla = da * a - dmult * (a * a) / mult
        dr = dla * ((-LRU_C) * spv)
        dsp = jnp.sum(dla * ((-LRU_C) * r), axis=0, keepdims=True)
        dza = dr * r * (1.0 - r)
        dzx = di * i * (1.0 - i)
        ub = u.astype(BF16)
        dzab = dza.astype(BF16)
        dzxb = dzx.astype(BF16)
        du = du + _dot_nt(dzab, wa_ref[...].astype(BF16)) + _dot_nt(dzxb, wx_ref[...].astype(BF16))
        dup = cwv[CONV_WIDTH - 1:CONV_WIDTH, :] * du
        for j in range(CONV_WIDTH - 1):
            dup = dup + cwv[j:j + 1, :] * _shift_up(du, CONV_WIDTH - 1 - j, 0.0)
        dup_ref[...] = dup.astype(BF16)

        @pl.when(b == 0)
        def _():
            for ref in (dcw_ref, dcb_ref, dwa_ref, dba_ref, dwx_ref, dbx_ref, dlam_ref):
                ref[...] = jnp.zeros_like(ref)

        rows = [jnp.sum(du * _shift_down(up, CONV_WIDTH - 1 - j, 0.0), axis=0, keepdims=True)
                for j in range(CONV_WIDTH - 1)]
        rows.append(jnp.sum(du * up, axis=0, keepdims=True))
        dcw_ref[...] += jnp.concatenate(rows, axis=0)
        dcb_ref[...] += jnp.sum(du, axis=0, keepdims=True)
        dwa_ref[...] += _dot_tn(ub, dzab)
        dba_ref[...] += jnp.sum(dza, axis=0, keepdims=True)
        dwx_ref[...] += _dot_tn(ub, dzxb)
        dbx_ref[...] += jnp.sum(dzx, axis=0, keepdims=True)
        dlam_ref[...] += dsp * (-_sigmoid(-lam_ref[...]))

    act_b = jax.ShapeDtypeStruct((bsz, s, D_RNN), BF16)
    vec = jax.ShapeDtypeStruct((1, D_RNN), F32)
    wsd = jax.ShapeDtypeStruct((N_RNN_BLOCKS, RNN_BLOCK, RNN_BLOCK), F32)
    return pl.pallas_call(
        body, name="lru_bwd", grid=(N_RNN_BLOCKS, bsz),
        out_shape=(act_b, act_b, jax.ShapeDtypeStruct((CONV_WIDTH, D_RNN), F32), vec, wsd, vec, wsd, vec, vec),
        in_specs=[sp["up"], sp["gr"], sp["act"], sp["act"], sp["cw"], sp["vec"], sp["wblk"], sp["vec"],
                  sp["wblk"], sp["vec"], sp["vec"]],
        out_specs=(sp["act"], sp["act"], sp["cw"], sp["vec"], sp["wblk"], sp["vec"], sp["wblk"], sp["vec"], sp["vec"]),
        scratch_shapes=[pltpu.VMEM((s, RNN_BLOCK), F32)] * 2,
        compiler_params=_params())(proj3, proj3, h3, dy3, cw, cb, wa, ba, wx, bx, lam)


def _rope_tables(s):
    half = ROPE_DIM // 2
    pos = jnp.arange(s, dtype=F32)
    inv_freq = ROPE_THETA ** (-jnp.arange(0, ROPE_DIM, 2, dtype=F32) / ROPE_DIM)
    ang = pos[:, None] * inv_freq[None, :]
    cos, sin = jnp.cos(ang), jnp.sin(ang)
    rest = HEAD_DIM - ROPE_DIM
    cos64 = jnp.concatenate([cos, cos, jnp.ones((s, rest), F32)], axis=1)
    sin64 = jnp.concatenate([-sin, sin, jnp.zeros((s, rest), F32)], axis=1)
    assert half * 2 == ROPE_DIM
    return jnp.tile(cos64, (1, LANES // HEAD_DIM)), jnp.tile(sin64, (1, LANES // HEAD_DIM))


def _swap_rot_halves(v):
    half = ROPE_DIM // 2
    lane = lax.broadcasted_iota(jnp.int32, v.shape, 1) % HEAD_DIM
    second = jnp.where(lane < ROPE_DIM, pltpu.roll(v, half, axis=1), 0.0)
    return jnp.where(lane < half, pltpu.roll(v, LANES - half, axis=1), second)


def _rope(v, cos, sin):
    tiles = []
    for t in range(v.shape[1] // LANES):
        vt = v[:, t * LANES:(t + 1) * LANES]
        tiles.append(vt * cos + _swap_rot_halves(vt) * sin)
    return tiles[0] if len(tiles) == 1 else jnp.concatenate(tiles, axis=1)


def _unrope(v, cos, sin):
    tiles = []
    for t in range(v.shape[1] // LANES):
        vt = v[:, t * LANES:(t + 1) * LANES]
        tiles.append(vt * cos + _swap_rot_halves(vt * sin))
    return tiles[0] if len(tiles) == 1 else jnp.concatenate(tiles, axis=1)


HEADS_PER_STEP = 8
QW = HEADS_PER_STEP * HEAD_DIM
N_PAIRS = N_Q_HEADS // HEADS_PER_STEP
Q_PER_KV = N_Q_HEADS // N_KV_HEADS
KV_PER_STEP = HEADS_PER_STEP // Q_PER_KV


QT_COLS = Q_PER_KV * ATT_BLK


def _attn_saved_shapes(bsz, s):
    nb = s // ATT_BLK
    pad = s + ATT_BLK
    return [(bsz, N_PAIRS, nb, LANES, QT_COLS), (bsz, N_PAIRS, KV_PER_STEP, pad, LANES),
            (bsz, N_PAIRS, KV_PER_STEP, pad, LANES), (bsz, N_PAIRS, LANES, pad)]


def _attn_specs(s, order):
    def mk(width, base, **kw):
        if order == "bp":
            return pl.BlockSpec((None, s, width), lambda b, p: (b, 0, base + p), **kw)
        return pl.BlockSpec((None, s, width), lambda p, b: (b, 0, base + p), **kw)

    def saved(shape, **kw):
        blk = (None, None) + shape[2:]
        zeros = (0,) * (len(shape) - 2)
        if order == "bp":
            return pl.BlockSpec(blk, lambda b, p: (b, p) + zeros, **kw)
        return pl.BlockSpec(blk, lambda p, b: (b, p) + zeros, **kw)

    one = dict(pipeline_mode=pl.Buffered(1))
    tbl = pl.BlockSpec((s, LANES), lambda *_: (0, 0))
    shapes = _attn_saved_shapes(1, s)
    return dict(q=mk(QW, OFF_Q // QW), k=mk(LANES, OFF_K // LANES), v=mk(LANES, OFF_V // LANES),
                g=mk(QW, OFF_ATTN_G // QW), act=mk(QW, 0), kv=mk(LANES, 0), tbl=tbl,
                g1=mk(QW, OFF_ATTN_G // QW, **one), act1=mk(QW, 0, **one),
                saved=[saved(sh) for sh in shapes], saved1=[saved(sh, **one) for sh in shapes],
                smem=pl.BlockSpec(memory_space=pltpu.SMEM))


def _to_qt(blk):
    rows = []
    for j in range(KV_PER_STEP):
        cols = []
        for tt in range(2):
            t = 2 * j + tt
            tr = blk[:, t * LANES:(t + 1) * LANES].T
            cols += [tr[0:HEAD_DIM, :], tr[HEAD_DIM:, :]]
        rows.append(jnp.concatenate(cols, axis=1))
    return jnp.concatenate(rows, axis=0)


def _from_qt(xt):
    tiles = []
    for j in range(KV_PER_STEP):
        for tt in range(2):
            g0 = 2 * tt
            pair = jnp.concatenate([xt[j * HEAD_DIM:(j + 1) * HEAD_DIM, (g0 + i) * ATT_BLK:(g0 + i + 1) * ATT_BLK]
                                    for i in range(2)], axis=0)
            tiles.append(pair.T)
    return jnp.concatenate(tiles, axis=1)


def _attn_prep(q_ref, k_ref, v_ref, cos_ref, sin_ref, qt_ref, km_ref, vm_ref, kt_ref, vt_ref, nb):
    zeros = jnp.zeros((ATT_BLK, LANES), BF16)
    for j in range(KV_PER_STEP):
        km_ref[j, 0:ATT_BLK, :] = zeros
        vm_ref[j, 0:ATT_BLK, :] = zeros
    kt_ref[:, 0:ATT_BLK] = zeros
    vt_ref[:, 0:ATT_BLK] = zeros
    head_of_lane = lax.broadcasted_iota(jnp.int32, (ATT_BLK, LANES), 1) // HEAD_DIM

    def prep(n, carry):
        r0 = pl.multiple_of(n * ATT_BLK, ATT_BLK)
        cs = cos_ref[pl.ds(r0, ATT_BLK), :]
        sn = sin_ref[pl.ds(r0, ATT_BLK), :]
        qt_ref[n] = _to_qt(_rope(q_ref[pl.ds(r0, ATT_BLK), :], cs, sn) * ATTN_SCALE).astype(BF16)
        k = _rope(k_ref[pl.ds(r0, ATT_BLK), :], cs, sn)
        v = v_ref[pl.ds(r0, ATT_BLK), :]
        for j in range(KV_PER_STEP):
            km_ref[j, pl.ds(r0 + ATT_BLK, ATT_BLK), :] = jnp.where(head_of_lane == j, k, 0.0).astype(BF16)
            vm_ref[j, pl.ds(r0 + ATT_BLK, ATT_BLK), :] = jnp.where(head_of_lane == j, v, 0.0).astype(BF16)
        kt_ref[:, pl.ds(r0 + ATT_BLK, ATT_BLK)] = k.T.astype(BF16)
        vt_ref[:, pl.ds(r0 + ATT_BLK, ATT_BLK)] = v.T.astype(BF16)
        return carry

    lax.fori_loop(0, nb, prep, 0)


def _from_prev_block():
    key = lax.broadcasted_iota(jnp.int32, (ATT_BLK, QT_COLS), 0)
    qry = lax.broadcasted_iota(jnp.int32, (ATT_BLK, QT_COLS), 1) % ATT_BLK
    return key > qry


def _fold(tile, prev, prev_bias=None):
    top = tile[:ATT_BLK] if prev_bias is None else tile[:ATT_BLK] + prev_bias
    return jnp.where(prev, top, tile[ATT_BLK:])


def _unfold(folded, prev):
    zero = jnp.zeros_like(folded)
    return jnp.concatenate([jnp.where(prev, folded, zero), jnp.where(prev, zero, folded)], axis=0).astype(BF16)


def _no_prev_bias(n):
    return jnp.where(n == 0, NEG_BIG, 0.0).astype(F32)


def _sink_row(sink_ref, first):
    return jnp.concatenate([jnp.full((1, ATT_BLK), sink_ref[first + g], F32) for g in range(Q_PER_KV)], axis=1)


def _softmax_cols(sc, sink):
    m = jnp.maximum(jnp.max(sc, axis=0, keepdims=True), sink)
    e = jnp.exp(sc - m)
    es = jnp.exp(sink - m)
    inv = 1.0 / (jnp.sum(e, axis=0, keepdims=True) + es)
    return e * inv, es * inv


def _attn_fwd(proj3, sinks, cosf, sinf):
    bsz, s, _ = proj3.shape
    nb = s // ATT_BLK
    sp = _attn_specs(s, "bp")

    def body(sink_ref, q_ref, k_ref, v_ref, g_ref, cos_ref, sin_ref, o_ref, y_ref, qt_sc, km_sc, vm_sc, kt_ref, vt_sc):
        p = pl.program_id(1)
        _attn_prep(q_ref, k_ref, v_ref, cos_ref, sin_ref, qt_sc, km_sc, vm_sc, kt_ref, vt_sc, nb)
        kv_row = lax.broadcasted_iota(jnp.int32, (LANES, QT_COLS), 0) // HEAD_DIM
        prev = _from_prev_block()

        def blk(n, carry):
            r0 = pl.multiple_of(n * ATT_BLK, ATT_BLK)
            bias = _no_prev_bias(n)
            rq = qt_sc[n]
            vt = vt_sc[:, pl.ds(r0, 2 * ATT_BLK)]
            ots = []
            for j in range(KV_PER_STEP):
                st = _dot(km_sc[j, pl.ds(r0, 2 * ATT_BLK), :], rq)
                pc, _ = _softmax_cols(_fold(st, prev, bias), _sink_row(sink_ref, p * HEADS_PER_STEP + j * Q_PER_KV))
                ots.append(_dot(vt, _unfold(pc, prev)))
            o = _from_qt(jnp.where(kv_row == 0, ots[0], ots[1]))
            o_ref[pl.ds(r0, ATT_BLK), :] = o
            g = g_ref[pl.ds(r0, ATT_BLK), :]
            y_ref[pl.ds(r0, ATT_BLK), :] = (o * (g * _sigmoid(g))).astype(BF16)
            return carry

        lax.fori_loop(0, nb, blk, 0, unroll=2)

    res = pl.pallas_call(
        body, name="attn_fwd", grid=(bsz, N_PAIRS),
        out_shape=[jax.ShapeDtypeStruct((bsz, s, D_ATTN), F32), jax.ShapeDtypeStruct((bsz, s, D_ATTN), BF16)] + [
            jax.ShapeDtypeStruct(sh, BF16) for sh in _attn_saved_shapes(bsz, s)],
        in_specs=[sp["smem"], sp["q"], sp["k"], sp["v"], sp["g"], sp["tbl"], sp["tbl"]],
        out_specs=[sp["act"], sp["act"]] + sp["saved"],
        scratch_shapes=[pltpu.VMEM((LANES, s + ATT_BLK), BF16)],
        compiler_params=_params())(sinks, proj3, proj3, proj3, proj3, cosf, sinf)
    return res[0], res[1], res[2:]


def _attn_bwd(proj3, saved, o3, dy3, sinks, cosf, sinf):
    bsz, s, _ = proj3.shape
    nb = s // ATT_BLK
    assert nb % 2 == 0
    sp = _attn_specs(s, "pb")

    def body(sink_ref, qt_sc, km_sc, vm_sc, kt_sc, g_ref, o_ref, dy_ref, cos_ref, sin_ref,
             dq_ref, dk_ref, dv_ref, dg_ref, ds_ref, dot_sc, dqt_sc, dk_sc, dv_sc):
        p = pl.program_id(0)
        b = pl.program_id(1)
        dk_sc[...] = jnp.zeros_like(dk_sc)
        dv_sc[...] = jnp.zeros_like(dv_sc)

        def gate(n, carry):
            r0 = pl.multiple_of(n * ATT_BLK, ATT_BLK)
            g = g_ref[pl.ds(r0, ATT_BLK), :]
            dy = dy_ref[pl.ds(r0, ATT_BLK), :]
            sg = _sigmoid(g)
            dg_ref[pl.ds(r0, ATT_BLK), :] = (dy * o_ref[pl.ds(r0, ATT_BLK), :] * (sg * (1.0 + g * (1.0 - sg)))).astype(BF16)
            dot_sc[n] = _to_qt(dy * (g * sg)).astype(BF16)
            return carry

        lax.fori_loop(0, nb, gate, 0)
        kv_lane = lax.broadcasted_iota(jnp.int32, (2 * ATT_BLK, LANES), 1) // HEAD_DIM
        kv_row = lax.broadcasted_iota(jnp.int32, (LANES, QT_COLS), 0) // HEAD_DIM
        prev = _from_prev_block()

        def blk(n, acc):
            r0 = pl.multiple_of(n * ATT_BLK, ATT_BLK)
            bias = _no_prev_bias(n)
            rq = qt_sc[n]
            rd = dot_sc[n]
            kt = kt_sc[:, pl.ds(r0, 2 * ATT_BLK)]
            dvs, dks, dqs, new_acc = [], [], [], []
            for j in range(KV_PER_STEP):
                st = _dot(km_sc[j, pl.ds(r0, 2 * ATT_BLK), :], rq)
                pc, ps = _softmax_cols(_fold(st, prev, bias), _sink_row(sink_ref, p * HEADS_PER_STEP + j * Q_PER_KV))
                dpc = _fold(_dot(vm_sc[j, pl.ds(r0, 2 * ATT_BLK), :], rd), prev)
                delta = jnp.sum(pc * dpc, axis=0, keepdims=True)
                dst = _unfold(pc * (dpc - delta), prev)
                new_acc.append(acc[j] + ps * delta)
                dvs.append(_dot_nt(_unfold(pc, prev), rd))
                dks.append(_dot_nt(dst, rq))
                dqs.append(_dot(kt, dst))
            dv_sc[pl.ds(r0, 2 * ATT_BLK), :] += jnp.where(kv_lane == 0, dvs[0], dvs[1])
            dk_sc[pl.ds(r0, 2 * ATT_BLK), :] += jnp.where(kv_lane == 0, dks[0], dks[1])
            dqt_sc[n] = jnp.where(kv_row == 0, dqs[0], dqs[1]) * ATTN_SCALE
            return tuple(new_acc)

        def blk_pair(m, acc):
            return blk(2 * m + 1, blk(2 * m, acc))

        acc = lax.fori_loop(0, nb // 2, blk_pair, tuple(jnp.zeros((1, QT_COLS), F32) for _ in range(KV_PER_STEP)))
        lane1 = lax.broadcasted_iota(jnp.int32, (1, LANES), 1)
        dsink = jnp.zeros((1, LANES), F32)
        for j in range(KV_PER_STEP):
            for i in range(Q_PER_KV):
                part = jnp.sum(acc[j][:, i * ATT_BLK:(i + 1) * ATT_BLK], axis=1, keepdims=True)
                dsink = dsink - jnp.where(lane1 == j * Q_PER_KV + i, part, 0.0)

        @pl.when(b == 0)
        def _():
            ds_ref[...] = jnp.zeros_like(ds_ref)

        ds_ref[...] += dsink

        def post(n, carry):
            r0 = pl.multiple_of(n * ATT_BLK, ATT_BLK)
            cs = cos_ref[pl.ds(r0, ATT_BLK), :]
            sn = sin_ref[pl.ds(r0, ATT_BLK), :]
            dq_ref[pl.ds(r0, ATT_BLK), :] = _unrope(_from_qt(dqt_sc[n]), cs, sn).astype(BF16)
            dk_ref[pl.ds(r0, ATT_BLK), :] = _unrope(dk_sc[pl.ds(r0 + ATT_BLK, ATT_BLK), :], cs, sn).astype(BF16)
            dv_ref[pl.ds(r0, ATT_BLK), :] = dv_sc[pl.ds(r0 + ATT_BLK, ATT_BLK), :].astype(BF16)
            return carry

        lax.fori_loop(0, nb, post, 0)

    act = jax.ShapeDtypeStruct((bsz, s, D_ATTN), BF16)
    kvs = jax.ShapeDtypeStruct((bsz, s, D_KV), BF16)
    return pl.pallas_call(
        body, name="attn_bwd", grid=(N_PAIRS, bsz),
        out_shape=(act, kvs, kvs, act, jax.ShapeDtypeStruct((N_PAIRS, 1, LANES), F32)),
        in_specs=[sp["smem"]] + sp["saved1"] + [sp["g1"], sp["act1"], sp["act1"], sp["tbl"], sp["tbl"]],
        out_specs=(sp["act"], sp["kv"], sp["kv"], sp["act"], pl.BlockSpec((None, 1, LANES), lambda p, b: (p, 0, 0))),
        scratch_shapes=[pltpu.VMEM((nb, LANES, QT_COLS), BF16),
                        pltpu.VMEM((nb, LANES, QT_COLS), F32),
                        pltpu.VMEM((s + ATT_BLK, LANES), F32),
                        pltpu.VMEM((s + ATT_BLK, LANES), F32)],
        compiler_params=_params())(sinks, *saved, proj3, o3, dy3, cosf, sinf)


def _merge_fwd_bwd(x, tgt, y_rnn, y_attn, proj, w_r, w_a, w_o, gf):
    t, d = x.shape
    tm = min(t, 256)

    hw = d // 2

    def body(x_ref, t_ref, yr_ref, ya_ref, mr0_ref, mr1_ref, ma0_ref, ma1_ref, wr_ref, wa_ref, wo_ref, gf_ref,
             dmg_ref, dyr_ref, dya_ref, mg_ref, dx2_ref, dx2b_ref, dpr_ref, dpa_ref, loss_ref, dgf_ref):
        i = pl.program_id(0)
        wr = wr_ref[...]
        wa = wa_ref[...]
        wo = wo_ref[...]
        gfv = gf_ref[...]
        pr = _dot(yr_ref[...], wr)
        pa = _dot(ya_ref[...], wa)
        sr = _sigmoid(jnp.concatenate([mr0_ref[...], mr1_ref[...]], axis=1))
        sa = _sigmoid(jnp.concatenate([ma0_ref[...], ma1_ref[...]], axis=1))
        mb = (sr * pr + sa * pa).astype(BF16)
        mg_ref[...] = mb
        x2 = x_ref[...] + _dot(mb, wo)
        r2 = lax.rsqrt(jnp.mean(x2 * x2, axis=-1, keepdims=True) + NORM_EPS)
        nrm = x2 * r2
        err = nrm * gfv - t_ref[...]
        dy = err * (1.0 / d)
        dn = dy * gfv
        dx2 = r2 * (dn - nrm * jnp.mean(dn * nrm, axis=-1, keepdims=True))
        dx2_ref[...] = dx2
        dx2b = dx2.astype(BF16)
        dx2b_ref[...] = dx2b
        dmerged = _dot_nt(dx2b, wo)
        dpr = (dmerged * sr).astype(BF16)
        dpa = (dmerged * sa).astype(BF16)
        dpr_ref[...] = dpr
        dpa_ref[...] = dpa
        dmg_ref[:, 0:d] = (dmerged * pr * (sr * (1.0 - sr))).astype(BF16)
        dmg_ref[:, d:2 * d] = (dmerged * pa * (sa * (1.0 - sa))).astype(BF16)
        dyr_ref[...] = _dot_nt(dpr, wr)
        dya_ref[...] = _dot_nt(dpa, wa)

        @pl.when(i == 0)
        def _():
            loss_ref[...] = jnp.zeros_like(loss_ref)
            dgf_ref[...] = jnp.zeros_like(dgf_ref)

        loss_ref[...] += jnp.full((1, LANES), 0.5 / d, F32) * jnp.sum(err * err)
        dgf_ref[...] += jnp.sum(dy * nrm, axis=0, keepdims=True)

    tile = pl.BlockSpec((tm, d), lambda i: (i, 0))
    wsp = pl.BlockSpec((d, d), lambda i: (0, 0))

    def gate(col_blk):
        return pl.BlockSpec((tm, hw), lambda i: (i, col_blk))

    fb = jax.ShapeDtypeStruct((t, d), BF16)
    ff = jax.ShapeDtypeStruct((t, d), F32)
    return pl.pallas_call(
        body, name="merge_fwd_bwd", grid=(t // tm,),
        out_shape=(jax.ShapeDtypeStruct((t, 2 * d), BF16), ff, ff, fb, ff, fb, fb, fb,
                   jax.ShapeDtypeStruct((1, LANES), F32), jax.ShapeDtypeStruct((1, d), F32)),
        in_specs=[tile, tile, tile, tile] + [gate(OFF_MERGE_R // hw + j) for j in range(4)] + [
            wsp, wsp, wsp, pl.BlockSpec((1, d), lambda i: (0, 0))],
        out_specs=(pl.BlockSpec((tm, 2 * d), lambda i: (i, 0)), tile, tile, tile, tile, tile, tile, tile,
                   pl.BlockSpec((1, LANES), lambda i: (0, 0)), pl.BlockSpec((1, d), lambda i: (0, 0))),
        compiler_params=_params())(x, tgt, y_rnn, y_attn, proj, proj, proj, proj, w_r, w_a, w_o, gf)


def _local_grads(x, tgt, h, proj, norm_g, w_in_bm, conv_w, conv_b, lru_w_a, lru_b_a, lru_w_x, lru_b_x, lam, sinks,
                 w_r, w_a, w_o, gf):
    bsz, s, d = x.shape
    t = bsz * s
    x2 = x.reshape(t, d)
    proj3 = proj.reshape(bsz, s, D_IN)
    h_lru, y_rnn = _lru_fwd(proj3, conv_w, conv_b, lru_w_a, lru_b_a, lru_w_x, lru_b_x, lam)
    cosf, sinf = _rope_tables(s)
    o_attn, y_attn, attn_saved = _attn_fwd(proj3, sinks, cosf, sinf)
    y_rnn2 = y_rnn.reshape(t, d)
    y_attn2 = y_attn.reshape(t, d)
    dmg, dyr, dya, merged, dx2, dx2b, dpr, dpa, loss, dgf = _merge_fwd_bwd(
        x2, tgt.reshape(t, d), y_rnn2, y_attn2, proj, w_r, w_a, w_o, gf)
    dup, dgr, dcw, dcb, dwa, dba, dwx, dbx, dlam = _lru_bwd(
        proj3, h_lru, dyr.reshape(bsz, s, d), conv_w, conv_b, lru_w_a, lru_b_a, lru_w_x, lru_b_x, lam)
    dq, dk, dv, dga, dsink = _attn_bwd(proj3, attn_saved, o_attn, dya.reshape(bsz, s, d), sinks, cosf, sinf)
    dproj = jnp.concatenate([dup, dgr, dq, dk, dv, dga, dmg.reshape(bsz, s, 2 * d)], axis=-1).reshape(t, D_IN)
    dh = _grad_h(dproj, w_in_bm)
    grad_x, dng = _rmsnorm_bwd(x2, dh, dx2, norm_g)
    small = dict(norm_g=dng, conv_w=dcw, conv_b=dcb, lru_w_a=dwa, lru_b_a=dba, lru_w_x=dwx, lru_b_x=dbx,
                 lru_lambda=dlam, attn_sinks=dsink[:, 0, :HEADS_PER_STEP].reshape(1, N_Q_HEADS), final_norm_g=dgf)
    squares = [(y_rnn2, dpr), (y_attn2, dpa), (merged, dx2b)]
    return loss[0, 0], grad_x.reshape(bsz, s, d), h, dproj, squares, small


ANY = pl.BlockSpec(memory_space=pl.ANY)


def _mesh_pos():
    return lax.axis_index("x"), lax.axis_index("y"), lax.axis_index("c")


def _remote(src, dst, send_sems, recv_sems, idx, peer):
    return pltpu.make_async_remote_copy(src_ref=src, dst_ref=dst, send_sem=send_sems.at[idx],
                                        recv_sem=recv_sems.at[idx], device_id=peer, device_id_type=MESH)


def _gather_in_proj(h, bufs, split, idx):
    t_tok, d = h.shape
    n = len(bufs)
    tm = min(t_tok, 512)
    nt = t_tok // tm
    n_fwd = 3 * sum(split)
    assert split[0]

    def body(idx_ref, h_ref, *refs):
        ins, proj_ref, outs = refs[:n], refs[n], refs[n + 1:2 * n + 1]
        wbuf, send_sems, recv_sems, fsend_sems, frecv_sems, l_sems = refs[2 * n + 1:]
        j, i = pl.program_id(0), pl.program_id(1)
        x, y, c = _mesh_pos()
        me = 2 * x + y
        sib = (x, y, 1 - c)
        peers = [((x, 1 - y, c), me ^ 1), ((1 - x, y, c), me ^ 2), ((1 - x, 1 - y, c), me ^ 3)]

        def part(ref, slot, t, half):
            if not split[t]:
                return ref.at[slot]
            hr = bufs[t].shape[1] // 2
            return ref.at[slot, pl.ds(pl.multiple_of(half * hr, 8), hr), :]

        def land(t):
            return wbuf if t == 0 else outs[t]

        def ici(t, k):
            peer, pj = peers[k]
            src = part(ins[t], me, t, c)
            return (_remote(src, part(land(t), me, t, c), send_sems, recv_sems, 3 * t + k, peer),
                    _remote(src, part(land(t), pj, t, c), send_sems, recv_sems, 3 * t + k, peer))

        fwd_index = {}
        for t in range(n):
            if split[t]:
                for k in range(3):
                    fwd_index[(t, k)] = len(fwd_index)

        def forward(t, k):
            pj = peers[k][1]
            got = part(land(t), pj, t, c)
            f = fwd_index[(t, k)]
            return (_remote(got, got, fsend_sems, frecv_sems, f, sib),
                    _remote(got, part(land(t), pj, t, 1 - c), fsend_sems, frecv_sems, f, sib))

        def write_back(k):
            pj = peers[k][1]
            return pltpu.make_async_copy(wbuf.at[pj], outs[0].at[pj], l_sems.at[1 + k])

        relay_peer = ((x + c) % 2, (y + 1 - c) % 2, c)

        def relay():
            got = part(wbuf, me ^ (2 - c), 0, c)
            return (_remote(got, got, send_sems, recv_sems, 2, relay_peer),
                    _remote(got, part(wbuf, me ^ 3, 0, c), send_sems, recv_sems, 2, relay_peer))

        direct = [(t, k) for t in range(n) for k in range(3) if (t, k) != (0, 2)]

        @pl.when((j == 0) & (i == 0))
        def _():
            for t, k in direct:
                ici(t, k)[0].start()
            own = pltpu.make_async_copy(ins[0].at[me], wbuf.at[me], l_sems.at[0])
            own.start()
            own.wait()

        @pl.when((j == 1) & (i == 0))
        def _():
            for k in range(2):
                ici(0, k)[1].wait_recv()
            relay()[0].start()
            for k in range(2):
                forward(0, k)[0].start()
            forward(0, 0)[1].wait_recv()
            write_back(0).start()

        @pl.when((j == 2) & (i == 0))
        def _():
            forward(0, 1)[1].wait_recv()
            write_back(1).start()

        @pl.when((j == 3) & (i == 0))
        def _():
            relay()[1].wait_recv()
            forward(0, 2)[0].start()
            forward(0, 2)[1].wait_recv()
            write_back(2).start()

        proj_ref[...] = _dot(h_ref[...], wbuf[me ^ j])

        @pl.when((j == N_CHIPS - 1) & (i == nt - 1))
        def _():
            for t in range(1, n):
                for k in range(3):
                    ici(t, k)[1].wait_recv()
                    if split[t]:
                        forward(t, k)[0].start()
            relay()[0].wait_send()
            for t, k in direct:
                ici(t, k)[0].wait_send()
            for t in range(n):
                if split[t]:
                    for k in range(3):
                        forward(t, k)[0].wait_send()
                        if t > 0:
                            forward(t, k)[1].wait_recv()
            for k in range(3):
                write_back(k).wait()

    grid_spec = pltpu.PrefetchScalarGridSpec(
        num_scalar_prefetch=1, grid=(N_CHIPS, nt),
        in_specs=[pl.BlockSpec((tm, d), lambda j, i, idx_ref: (i, 0))] + [ANY] * n,
        out_specs=[pl.BlockSpec((tm, W_BLK), lambda j, i, idx_ref: (i, idx_ref[0] ^ j))] + [ANY] * n,
        scratch_shapes=[pltpu.VMEM(bufs[0].shape, bufs[0].dtype),
                        pltpu.SemaphoreType.DMA((3 * n,)), pltpu.SemaphoreType.DMA((3 * n,)),
                        pltpu.SemaphoreType.DMA((n_fwd,)), pltpu.SemaphoreType.DMA((n_fwd,)),
                        pltpu.SemaphoreType.DMA((4,))])
    out_shape = [jax.ShapeDtypeStruct((t_tok, D_IN), F32)] + [jax.ShapeDtypeStruct(a.shape, a.dtype) for a in bufs]
    res = pl.pallas_call(
        body, name="gather_in_proj", grid_spec=grid_spec, out_shape=out_shape,
        input_output_aliases={2 + t: 1 + t for t in range(n)}, compiler_params=_params())(idx, h, *bufs)
    return res[0], res[1:]


def _row_tile(rows, row_bytes, cap_bytes=2 * 1024 * 1024):
    best = None
    for tr in range(8, rows + 1, 8):
        if rows % tr == 0 and tr * row_bytes <= cap_bytes:
            best = tr
    return best if best is not None else rows


XOR_ORDER = (3, 2, 1)


def _grads_reduce_scatter(h, dproj, squares, small, idx):
    t, d = h.shape
    nsq = len(squares)
    hr = d // 2
    qr = ROW_BLK // 2
    tk = min(t, 1024)
    nk = t // tk
    last = N_CHIPS - 1
    n_phase = 3

    def dest(s, idx_ref):
        xo = jnp.where(s == 0, XOR_ORDER[0], jnp.where(s == 1, XOR_ORDER[1], jnp.where(s == 2, XOR_ORDER[2], 0)))
        return idx_ref[0] ^ xo

    def k_sq(p, k):
        return jnp.where(p == 0, k, nk - 1)

    def k_w(p, k):
        return jnp.where(p == 0, 0, k)

    in_specs = [
        pl.BlockSpec((tk, hr), lambda s, p, k, idx_ref: (k_w(p, k), (1 - idx_ref[1] + jnp.maximum(p - 1, 0)) % 2)),
        pl.BlockSpec((tk, W_BLK), lambda s, p, k, idx_ref: (k_w(p, k), dest(s, idx_ref)))]
    for q in range(nsq):
        in_specs.append(pl.BlockSpec((tk, ROW_BLK), lambda s, p, k, idx_ref: (k_sq(p, k), dest(s, idx_ref))))
        in_specs.append(pl.BlockSpec((tk, d), lambda s, p, k, idx_ref: (k_sq(p, k), 0)))

    def body(idx_ref, *refs):
        nj = 1 + nsq
        h_ref, dp_ref = refs[0], refs[1]
        sq_in = refs[2:2 + 2 * nsq]
        small_in = refs[2 * nj]
        outs = refs[2 * nj + 1:3 * nj + 2]
        landing = refs[3 * nj + 2:4 * nj + 3]
        sc = refs[4 * nj + 3:]
        acc_w, xr_w, sb_w = sc[0:3]
        sq_sc = [sc[3 + 3 * q:6 + 3 * q] for q in range(nsq)]
        sm, smx = sc[3 * nj:3 * nj + 2]
        x_send, x_recv, i_send, i_recv, f_send, f_recv, o_sem, l_sem = sc[3 * nj + 2:]
        s, p, k = pl.program_id(0), pl.program_id(1), pl.program_id(2)
        x, y, c = _mesh_pos()
        sib = (x, y, 1 - c)
        peers = [((1 - x) if xo & 2 else x, (1 - y) if xo & 1 else y, c) for xo in XOR_ORDER]
        slot = s % 2
        mine_w = pl.ds(pl.multiple_of(c * hr, 8), hr)
        theirs_w = pl.ds(pl.multiple_of((1 - c) * hr, 8), hr)
        mine_q = pl.ds(pl.multiple_of(c * qr, 8), qr)
        theirs_q = pl.ds(pl.multiple_of((1 - c) * qr, 8), qr)

        def exch(j, src, dst):
            return _remote(src, dst, x_send, x_recv, 2 * j + slot, sib)

        sbufs = [sb_w] + [sq_sc[q][2] for q in range(nsq)]

        def ici(j, ss):
            return _remote(sbufs[j].at[ss], landing[j].at[ss], i_send, i_recv, last * j + ss, peers[ss])

        def exchanges():
            cps = [exch(0, acc_w.at[0], xr_w.at[slot])]
            cps += [exch(1 + q, sq_sc[q][0].at[theirs_q, :], sq_sc[q][1].at[slot]) for q in range(nsq)]
            return cps

        def small_send(ss):
            return _remote(sm.at[c], landing[nj].at[ss], i_send, i_recv, last * nj + ss, peers[ss])

        def small_start():
            load = pltpu.make_async_copy(small_in, sm, l_sem.at[nj + 1])
            load.start()
            load.wait()
            swap = _remote(sm, smx.at[pl.ds(0, 2)], x_send, x_recv, 2 * nj, sib)
            swap.start()
            swap.wait_recv()
            swap.wait_send()
            sm[...] = sm[...] + smx[0:2]
            for ss in range(last):
                small_send(ss).start()

        def pair_ref(j):
            return acc_w.at[1] if j == 0 else sq_sc[j - 1][0].at[mine_q, :]

        def sq_phase():
            pl.when((s == 0) & (k == 0))(small_start)
            for q in range(nsq):
                acc = sq_sc[q][0]

                @pl.when(k == 0)
                def _():
                    acc[...] = jnp.zeros((ROW_BLK, d), F32)

                acc[...] += _dot_tn(sq_in[2 * q][...], sq_in[2 * q + 1][...])

            @pl.when(k == nk - 1)
            def _():
                for cp in exchanges()[1:]:
                    cp.start()

        def w_phase(hf):
            @pl.when(k == 0)
            def _():
                acc_w[hf] = jnp.zeros((hr, W_BLK), F32)

            acc_w[hf] += _dot_tn(h_ref[...], dp_ref[...])

            @pl.when(k == nk - 1)
            def _():
                if hf == 0:
                    exchanges()[0].start()
                else:
                    finish_step()

        def finish_step():
            for cp in exchanges():
                cp.wait_recv()
                cp.wait_send()
            acc_w[1] += xr_w[slot]
            for q in range(nsq):
                sq_sc[q][0][mine_q, :] += sq_sc[q][1][slot]
            for ss in range(last):
                @pl.when(s == ss)
                def _():
                    for j in range(nj):
                        sbufs[j][ss] = pair_ref(j)[...].astype(BF16)
                        ici(j, ss).start()

            @pl.when(s == last)
            def _():
                for ss in range(last):
                    for j in range(nj):
                        ici(j, ss).wait_recv()
                        ici(j, ss).wait_send()
                    small_send(ss).wait_recv()
                    small_send(ss).wait_send()
                stage = [pltpu.make_async_copy(landing[j], sbufs[j], l_sem.at[j]) for j in range(nj)]
                stage.append(pltpu.make_async_copy(landing[nj], smx, l_sem.at[nj]))
                for cp in stage:
                    cp.start()
                for j in range(nj):
                    stage[j].wait()
                    total = pair_ref(j)[...]
                    for ss in range(last):
                        total = total + sbufs[j][ss].astype(F32)
                    pair_ref(j)[...] = total
                stage[nj].wait()
                by_xor = {xo: smx[ss] for ss, xo in enumerate(XOR_ORDER)}
                sm[c] = (sm[c] + by_xor[1]) + (by_xor[2] + by_xor[3])
                done = [(acc_w.at[1], outs[0].at[mine_w, :], outs[0].at[theirs_w, :])]
                done += [(pair_ref(1 + q), outs[1 + q].at[mine_q, :], outs[1 + q].at[theirs_q, :]) for q in range(nsq)]
                done.append((sm.at[c], outs[nj].at[c], outs[nj].at[1 - c]))
                copies = []
                for j, (src, mine, theirs) in enumerate(done):
                    keep = pltpu.make_async_copy(src, mine, o_sem.at[j])
                    give = _remote(src, mine, f_send, f_recv, j, sib)
                    take = _remote(src, theirs, f_send, f_recv, j, sib)
                    keep.start()
                    give.start()
                    copies.append((keep, give, take))
                for keep, give, take in copies:
                    keep.wait()
                    give.wait_send()
                    take.wait_recv()

        pl.when(p == 0)(sq_phase)
        for hf in range(2):
            pl.when(p == 1 + hf)(functools.partial(w_phase, hf))

    nj = 1 + nsq
    scratch = [pltpu.VMEM((2, hr, W_BLK), F32), pltpu.VMEM((2, hr, W_BLK), F32), pltpu.VMEM((last, hr, W_BLK), BF16)]
    for _ in range(nsq):
        scratch += [pltpu.VMEM((ROW_BLK, d), F32), pltpu.VMEM((2, qr, d), F32), pltpu.VMEM((last, qr, d), BF16)]
    scratch += [pltpu.VMEM((2, PK_HALF, LANES), F32), pltpu.VMEM((last, PK_HALF, LANES), F32)]
    scratch += [pltpu.SemaphoreType.DMA((2 * nj + 1,)), pltpu.SemaphoreType.DMA((2 * nj + 1,)),
                pltpu.SemaphoreType.DMA((last * (nj + 1),)), pltpu.SemaphoreType.DMA((last * (nj + 1),)),
                pltpu.SemaphoreType.DMA((nj + 1,)), pltpu.SemaphoreType.DMA((nj + 1,)),
                pltpu.SemaphoreType.DMA((nj + 1,)), pltpu.SemaphoreType.DMA((nj + 2,))]
    grid_spec = pltpu.PrefetchScalarGridSpec(
        num_scalar_prefetch=1, grid=(N_CHIPS, n_phase, nk), in_specs=in_specs + [ANY],
        out_specs=[ANY] * (2 * nj + 2), scratch_shapes=scratch)
    out_shape = [jax.ShapeDtypeStruct((d, W_BLK), F32)] + [jax.ShapeDtypeStruct((ROW_BLK, d), F32)] * nsq
    out_shape.append(jax.ShapeDtypeStruct((2, PK_HALF, LANES), F32))
    out_shape += [jax.ShapeDtypeStruct((last, hr, W_BLK), BF16)] + [jax.ShapeDtypeStruct((last, qr, d), BF16)] * nsq
    out_shape.append(jax.ShapeDtypeStruct((last, PK_HALF, LANES), F32))
    flat = [a for pair in squares for a in pair]
    res = pl.pallas_call(body, name="grads_reduce_scatter", grid_spec=grid_spec, out_shape=out_shape,
                         compiler_params=_params())(idx, h, dproj, *flat, small)
    return res[:nj + 1]


_VEC_NAMES = ("norm_g", "conv_b", "lru_b_a", "lru_b_x", "lru_lambda", "final_norm_g")


def _pack_small(p, conv_full=None, scalar=None):
    rows = [p["lru_w_a"].reshape(PK_WX - PK_WA, LANES), p["lru_w_x"].reshape(PK_VEC - PK_WX, LANES)]
    rows += [p[k].reshape(8, LANES) for k in _VEC_NAMES]
    rows.append(jnp.pad(p["attn_sinks"].reshape(1, N_Q_HEADS), ((0, 7), (0, LANES - N_Q_HEADS))))
    rows.append(jnp.zeros((32, LANES), F32) if conv_full is None else conv_full.reshape(32, LANES))
    tail = PK_ROWS - PK_SCALAR
    if scalar is None:
        rows.append(jnp.zeros((tail, LANES), F32))
    else:
        rows.append(jnp.pad(scalar.reshape(1, 1), ((0, tail - 1), (0, LANES - 1))))
    return jnp.concatenate(rows, axis=0)


def _unpack_small(pk, like):
    out = {"lru_w_a": pk[PK_WA:PK_WX].reshape(like["lru_w_a"].shape),
           "lru_w_x": pk[PK_WX:PK_VEC].reshape(like["lru_w_x"].shape)}
    for j, k in enumerate(_VEC_NAMES):
        out[k] = pk[PK_VEC + 8 * j:PK_VEC + 8 * j + 8].reshape(like[k].shape)
    out["attn_sinks"] = pk[PK_SINK:PK_SINK + 1, :N_Q_HEADS].reshape(like["attn_sinks"].shape)
    return out


_WEIGHTS = ("norm_g", "w_in", "conv_w", "conv_b", "lru_w_a", "lru_b_a", "lru_w_x", "lru_b_x", "lru_lambda",
            "attn_sinks", "w_rnn_out", "w_attn_out", "w_o", "final_norm_g")
_SMALL = ("norm_g", "conv_b", "lru_w_a", "lru_b_a", "lru_w_x", "lru_b_x", "lru_lambda", "attn_sinks", "final_norm_g")
_ROW_SHARDED = ("w_rnn_out", "w_attn_out", "w_o")


def kernel(x, norm_g, w_in, conv_w, conv_b, lru_w_a, lru_b_a, lru_w_x, lru_b_x, lru_lambda, attn_sinks, w_rnn_out, w_attn_out, w_o, final_norm_g, loss_target, m_norm_g, m_w_in, m_conv_w, m_conv_b, m_lru_w_a, m_lru_b_a, m_lru_w_x, m_lru_b_x, m_lru_lambda, m_attn_sinks, m_w_rnn_out, m_w_attn_out, m_w_o, m_final_norm_g, v_norm_g, v_w_in, v_conv_w, v_conv_b, v_lru_w_a, v_lru_b_a, v_lru_w_x, v_lru_b_x, v_lru_lambda, v_attn_sinks, v_w_rnn_out, v_w_attn_out, v_w_o, v_final_norm_g):
    w = dict(norm_g=norm_g, w_in=w_in, conv_w=conv_w, conv_b=conv_b, lru_w_a=lru_w_a, lru_b_a=lru_b_a, lru_w_x=lru_w_x,
             lru_b_x=lru_b_x, lru_lambda=lru_lambda, attn_sinks=attn_sinks, w_rnn_out=w_rnn_out, w_attn_out=w_attn_out,
             w_o=w_o, final_norm_g=final_norm_g)
    m = dict(norm_g=m_norm_g, w_in=m_w_in, conv_w=m_conv_w, conv_b=m_conv_b, lru_w_a=m_lru_w_a, lru_b_a=m_lru_b_a,
             lru_w_x=m_lru_w_x, lru_b_x=m_lru_b_x, lru_lambda=m_lru_lambda, attn_sinks=m_attn_sinks,
             w_rnn_out=m_w_rnn_out, w_attn_out=m_w_attn_out, w_o=m_w_o, final_norm_g=m_final_norm_g)
    v = dict(norm_g=v_norm_g, w_in=v_w_in, conv_w=v_conv_w, conv_b=v_conv_b, lru_w_a=v_lru_w_a, lru_b_a=v_lru_b_a,
             lru_w_x=v_lru_w_x, lru_b_x=v_lru_b_x, lru_lambda=v_lru_lambda, attn_sinks=v_attn_sinks,
             w_rnn_out=v_w_rnn_out, w_attn_out=v_w_attn_out, w_o=v_w_o, final_norm_g=v_final_norm_g)
    mx, my, mc = _mesh_pos()
    me = 2 * mx + my
    d = D_MODEL

    slot0 = jnp.stack([me, jnp.zeros_like(me)]).astype(jnp.int32)
    bufs = [_put_slot(w[k][0], N_CHIPS, slot0, w[k].shape[1], BF16, "cast_" + k) for k in ("w_in",) + _ROW_SHARDED]
    bufs.append(_put_slot(w["conv_w"][0], N_CHIPS, slot0, CONV_WIDTH, F32, "slot_conv_w"))
    h = _rmsnorm_fwd(x.reshape(-1, d), w["norm_g"])
    proj, (g_in, g_r, g_a, g_o, g_cw) = _gather_in_proj(h, bufs, [True, True, True, True, False],
                                                        jnp.reshape(me, (1,)).astype(jnp.int32))
    conv_full = g_cw.transpose(1, 0, 2).reshape(CONV_WIDTH, D_RNN)

    loss_local, grad_x, h, dproj, squares, gsmall = _local_grads(
        x, loss_target, h, proj, w["norm_g"], g_in, conv_full, w["conv_b"], w["lru_w_a"][0], w["lru_b_a"], w["lru_w_x"][0],
        w["lru_b_x"], w["lru_lambda"], w["attn_sinks"][0], g_r.reshape(d, d), g_a.reshape(d, d), g_o.reshape(d, d),
        w["final_norm_g"].reshape(1, d))
    gpack = _pack_small(gsmall, gsmall["conv_w"], loss_local).reshape(2, PK_HALF, LANES)
    f_in, f_r, f_a, f_o, spack = _grads_reduce_scatter(h, dproj, squares, gpack, jnp.stack([me, mc]).astype(jnp.int32))
    spack = spack.reshape(PK_ROWS, LANES)
    loss = spack[PK_SCALAR, 0]

    grads = _unpack_small(spack, w)
    conv_all = spack[PK_CONV:PK_CONV + 32].reshape(CONV_WIDTH, D_RNN)
    grads["conv_w"] = lax.dynamic_slice_in_dim(conv_all, me * (D_RNN // N_CHIPS), D_RNN // N_CHIPS, axis=1)[None]
    grads["w_in"] = f_in[None]
    grads["w_rnn_out"], grads["w_attn_out"], grads["w_o"] = f_r[None], f_a[None], f_o[None]

    delta, new_m, new_v = {}, {}, {}
    for k in ("w_in",) + _ROW_SHARDED:
        dk, mk, vk = _adamw(w[k][0], grads[k][0], m[k][0], v[k][0], "adamw_" + k)
        delta[k], new_m[k], new_v[k] = dk[None], mk[None], vk[None]
    shp = (2 * CONV_WIDTH, LANES)
    dk, mk, vk = _adamw(w["conv_w"].reshape(shp), grads["conv_w"].reshape(shp), m["conv_w"].reshape(shp),
                        v["conv_w"].reshape(shp), "adamw_conv_w")
    delta["conv_w"], new_m["conv_w"], new_v["conv_w"] = (a.reshape(w["conv_w"].shape) for a in (dk, mk, vk))
    dk, mk, vk = _adamw(_pack_small(w), spack, _pack_small(m), _pack_small(v), "adamw_small")
    for src, dst in ((dk, delta), (mk, new_m), (vk, new_v)):
        dst.update(_unpack_small(src, w))

    return (loss, grad_x, *[grads[k] for k in _WEIGHTS], *[delta[k] for k in _WEIGHTS],
            *[new_m[k] for k in _WEIGHTS], *[new_v[k] for k in _WEIGHTS])
```

```python
import functools
import math

import jax
import jax.numpy as jnp
from jax import lax
from jax.experimental import pallas as pl
from jax.experimental.pallas import tpu as pltpu

F32 = jnp.float32
BF16 = jnp.bfloat16
MESH = pl.DeviceIdType.MESH

D_MODEL = 1024
D_RNN = 1024
N_RNN_BLOCKS = 8
RNN_BLOCK = D_RNN // N_RNN_BLOCKS
CONV_WIDTH = 4
LRU_C = 8.0
HEAD_DIM = 64
N_Q_HEADS = 16
N_KV_HEADS = 4
D_ATTN = N_Q_HEADS * HEAD_DIM
D_KV = N_KV_HEADS * HEAD_DIM
WINDOW = 128
ROPE_DIM = HEAD_DIM // 4
ROPE_THETA = 500000.0
NORM_EPS = 1e-6
OFF_RNN_X = 0
OFF_RNN_G = OFF_RNN_X + D_RNN
OFF_Q = OFF_RNN_G + D_RNN
OFF_K = OFF_Q + D_ATTN
OFF_V = OFF_K + D_KV
OFF_ATTN_G = OFF_V + D_KV
OFF_MERGE_R = OFF_ATTN_G + D_ATTN
OFF_MERGE_A = OFF_MERGE_R + D_MODEL
D_IN = OFF_MERGE_A + D_MODEL

ADAM_LR = 0.001
ADAM_B1 = 0.9
ADAM_B2 = 0.999
ADAM_EPS = 1e-08
ADAM_WD = 0.01
ADAM_STEP = 10

N_CHIPS = 4
W_BLK = D_IN // N_CHIPS
ROW_BLK = D_MODEL // N_CHIPS
LANES = 128
ATT_BLK = 128
VMEM_LIMIT = 56 * 1024 * 1024
NEG_BIG = -1e30
ATTN_SCALE = 1.0 / math.sqrt(HEAD_DIM)

PK_WA = 0
PK_WX = PK_WA + N_RNN_BLOCKS * RNN_BLOCK
PK_VEC = PK_WX + N_RNN_BLOCKS * RNN_BLOCK
PK_SINK = PK_VEC + 6 * 8
PK_CONV = PK_SINK + 8
PK_SCALAR = PK_CONV + 32
PK_ROWS = PK_SCALAR + 8
PK_HALF = PK_ROWS // 2


def _params(**kw):
    return pltpu.CompilerParams(vmem_limit_bytes=VMEM_LIMIT, **kw)


def _sigmoid(z):
    return 1.0 / (1.0 + jnp.exp(-z))


def _dot(a, b):
    return jnp.dot(a, b, preferred_element_type=F32)


def _dot_nt(a, b):
    return lax.dot_general(a, b, (((1,), (1,)), ((), ())), preferred_element_type=F32)


def _dot_tn(a, b):
    return lax.dot_general(a, b, (((0,), (0,)), ((), ())), preferred_element_type=F32)


def _put_slot(src, n_slots, slot_and_blk, rows, dtype, name):
    _, c = src.shape
    tr = _row_tile(rows, c * 4)
    steps = rows // tr

    def body(idx_ref, s_ref, o_ref):
        o_ref[...] = s_ref[...].astype(dtype)

    grid_spec = pltpu.PrefetchScalarGridSpec(
        num_scalar_prefetch=1, grid=(steps,),
        in_specs=[pl.BlockSpec((tr, c), lambda i, idx_ref: (idx_ref[1] * steps + i, 0))],
        out_specs=pl.BlockSpec((None, tr, c), lambda i, idx_ref: (idx_ref[0], i, 0)))
    return pl.pallas_call(body, name=name, grid_spec=grid_spec,
                          out_shape=jax.ShapeDtypeStruct((n_slots, rows, c), dtype),
                          compiler_params=_params())(slot_and_blk, src)


def _rmsnorm_fwd(x, g):
    t, d = x.shape
    tm = min(t, 512)

    def body(x_ref, g_ref, o_ref):
        xv = x_ref[...]
        r = lax.rsqrt(jnp.mean(xv * xv, axis=-1, keepdims=True) + NORM_EPS)
        o_ref[...] = (xv * r * g_ref[...]).astype(BF16)

    return pl.pallas_call(
        body, name="rmsnorm_fwd", grid=(t // tm,), out_shape=jax.ShapeDtypeStruct((t, d), BF16),
        in_specs=[pl.BlockSpec((tm, d), lambda i: (i, 0)), pl.BlockSpec((1, d), lambda i: (0, 0))],
        out_specs=pl.BlockSpec((tm, d), lambda i: (i, 0)), compiler_params=_params())(x, g)


def _rmsnorm_bwd(x, dh, dx2, g):
    t, d = x.shape
    tm = min(t, 512)

    def body(x_ref, dh_ref, dx2_ref, g_ref, gx_ref, dg_ref):
        i = pl.program_id(0)
        xv = x_ref[...]
        dhv = dh_ref[...]
        r = lax.rsqrt(jnp.mean(xv * xv, axis=-1, keepdims=True) + NORM_EPS)
        nrm = xv * r
        dn = dhv * g_ref[...]
        gx_ref[...] = dx2_ref[...] + r * (dn - nrm * jnp.mean(dn * nrm, axis=-1, keepdims=True))

        @pl.when(i == 0)
        def _():
            dg_ref[...] = jnp.zeros_like(dg_ref)

        dg_ref[...] += jnp.sum(dhv * nrm, axis=0, keepdims=True)

    return pl.pallas_call(
        body, name="rmsnorm_bwd", grid=(t // tm,),
        out_shape=(jax.ShapeDtypeStruct((t, d), F32), jax.ShapeDtypeStruct((1, d), F32)),
        in_specs=[pl.BlockSpec((tm, d), lambda i: (i, 0)), pl.BlockSpec((tm, d), lambda i: (i, 0)),
                  pl.BlockSpec((tm, d), lambda i: (i, 0)), pl.BlockSpec((1, d), lambda i: (0, 0))],
        out_specs=(pl.BlockSpec((tm, d), lambda i: (i, 0)), pl.BlockSpec((1, d), lambda i: (0, 0))),
        compiler_params=_params())(x, dh, dx2, g)


def _adamw(w, g, m, v, name):
    r, c = w.shape
    tr = _row_tile(r, c * 4, 1024 * 1024)
    c1 = 1.0 - ADAM_B1 ** ADAM_STEP
    c2 = 1.0 - ADAM_B2 ** ADAM_STEP

    def body(w_ref, g_ref, m_ref, v_ref, d_ref, nm_ref, nv_ref):
        gv = g_ref[...]
        nm = ADAM_B1 * m_ref[...] + (1.0 - ADAM_B1) * gv
        nv = ADAM_B2 * v_ref[...] + (1.0 - ADAM_B2) * (gv * gv)
        m_hat = nm / c1
        v_hat = nv / c2
        d_ref[...] = -ADAM_LR * (m_hat / (jnp.sqrt(v_hat) + ADAM_EPS) + ADAM_WD * w_ref[...])
        nm_ref[...] = nm
        nv_ref[...] = nv

    spec = pl.BlockSpec((tr, c), lambda i: (i, 0))
    sds = jax.ShapeDtypeStruct((r, c), F32)
    return pl.pallas_call(
        body, name=name, grid=(r // tr,), out_shape=(sds, sds, sds),
        in_specs=[spec, spec, spec, spec], out_specs=(spec, spec, spec), compiler_params=_params())(w, g, m, v)


def _grad_h(dproj, w_bm):
    t = dproj.shape[0]
    nb, d, wb = w_bm.shape
    tm = min(t, 1024)

    def body(dp_ref, w_ref, o_ref, acc_ref):
        k = pl.program_id(1)

        @pl.when(k == 0)
        def _():
            acc_ref[...] = jnp.zeros_like(acc_ref)

        acc_ref[...] += _dot_nt(dp_ref[...], w_ref[...])

        @pl.when(k == nb - 1)
        def _():
            o_ref[...] = acc_ref[...]

    return pl.pallas_call(
        body, name="grad_h", grid=(t // tm, nb), out_shape=jax.ShapeDtypeStruct((t, d), F32),
        in_specs=[pl.BlockSpec((tm, wb), lambda i, k: (i, k)), pl.BlockSpec((None, d, wb), lambda i, k: (k, 0, 0))],
        out_specs=pl.BlockSpec((tm, d), lambda i, k: (i, 0)),
        scratch_shapes=[pltpu.VMEM((tm, d), F32)], compiler_params=_params())(dproj, w_bm)


def _shift_down(v, d, fill):
    n = v.shape[0]
    if d % 8 == 0:
        return jnp.concatenate([jnp.full((d,) + v.shape[1:], fill, v.dtype), v[: n - d]], axis=0)
    row = lax.broadcasted_iota(jnp.int32, v.shape, 0)
    return jnp.where(row >= d, pltpu.roll(v, d, axis=0), fill)


def _shift_up(v, d, fill):
    n = v.shape[0]
    if d % 8 == 0:
        return jnp.concatenate([v[d:], jnp.full((d,) + v.shape[1:], fill, v.dtype)], axis=0)
    row = lax.broadcasted_iota(jnp.int32, v.shape, 0)
    return jnp.where(row < n - d, pltpu.roll(v, n - d, axis=0), fill)


def _scan_log(a, b, shift):
    n = a.shape[0]
    d = 1
    while d < n:
        b = a * shift(b, d, 0.0) + b
        if 2 * d < n:
            a = a * shift(a, d, 1.0)
        d *= 2
    return b


SUBLANES = 8


def _scan(a, b, sa_ref, sb_ref, reverse):
    n, c = a.shape
    g = n // SUBLANES
    a3, b3 = a.reshape(g, SUBLANES, c), b.reshape(g, SUBLANES, c)
    sub = lax.broadcasted_iota(jnp.int32, a3.shape, 1)
    d = 1
    while d < SUBLANES:
        keep = (sub < SUBLANES - d) if reverse else (sub >= d)
        amount = SUBLANES - d if reverse else d
        b3 = a3 * jnp.where(keep, pltpu.roll(b3, amount, axis=1), 0.0) + b3
        a3 = a3 * jnp.where(keep, pltpu.roll(a3, amount, axis=1), 1.0)
        d *= 2
    sa_ref[...] = a3.reshape(n, c)
    sb_ref[...] = b3.reshape(n, c)
    edge = 0 if reverse else SUBLANES - 1
    shift = _shift_up if reverse else _shift_down
    totals = _scan_log(sa_ref[pl.ds(edge, g, stride=SUBLANES), :], sb_ref[pl.ds(edge, g, stride=SUBLANES), :], shift)
    carry = shift(totals, 1, 0.0)
    return (a3 * carry[:, None, :] + b3).reshape(n, c)


def _neg_expm1_twice(log_a, a):
    return -jnp.tanh(log_a) * (a * a + 1.0)


def _softplus(z):
    e = jnp.exp(-jnp.abs(z))
    w = 1.0 + e
    log1p = jnp.where(w == 1.0, e, jnp.log(w) * (e / jnp.where(w == 1.0, 1.0, w - 1.0)))
    return jnp.maximum(z, 0.0) + log1p


def _conv(up, cw, cb):
    out = cb + cw[CONV_WIDTH - 1:CONV_WIDTH, :] * up
    for j in range(CONV_WIDTH - 1):
        out = out + cw[j:j + 1, :] * _shift_down(up, CONV_WIDTH - 1 - j, 0.0)
    return out


def _lru_gates(u, wa_ref, ba_ref, wx_ref, bx_ref, lam_ref):
    ub = u.astype(BF16)
    r = _sigmoid(_dot(ub, wa_ref[...].astype(BF16)) + ba_ref[...])
    i = _sigmoid(_dot(ub, wx_ref[...].astype(BF16)) + bx_ref[...])
    sp = _softplus(-lam_ref[...])
    log_a = (-LRU_C) * r * sp
    a = jnp.exp(log_a)
    mult = jnp.sqrt(_neg_expm1_twice(log_a, a))
    return r, i, sp, a, mult


def _lru_specs(s):
    cb = RNN_BLOCK
    vec = pl.BlockSpec((1, cb), lambda n, b: (0, n))
    return dict(
        up=pl.BlockSpec((None, s, cb), lambda n, b: (b, 0, OFF_RNN_X // cb + n)),
        gr=pl.BlockSpec((None, s, cb), lambda n, b: (b, 0, OFF_RNN_G // cb + n)),
        act=pl.BlockSpec((None, s, cb), lambda n, b: (b, 0, n)),
        cw=pl.BlockSpec((CONV_WIDTH, cb), lambda n, b: (0, n)),
        vec=vec,
        wblk=pl.BlockSpec((None, cb, cb), lambda n, b: (n, 0, 0)),
    )


def _lru_fwd(proj3, cw, cb, wa, ba, wx, bx, lam):
    bsz, s, _ = proj3.shape
    sp = _lru_specs(s)

    def body(up_ref, gr_ref, cw_ref, cb_ref, wa_ref, ba_ref, wx_ref, bx_ref, lam_ref, h_ref, y_ref, sa_ref, sb_ref):
        u = _conv(up_ref[...], cw_ref[...], cb_ref[...])
        _, i, _, a, mult = _lru_gates(u, wa_ref, ba_ref, wx_ref, bx_ref, lam_ref)
        h = _scan(a, mult * (i * u), sa_ref, sb_ref, reverse=False)
        h_ref[...] = h
        g = gr_ref[...]
        y_ref[...] = (h * (g * _sigmoid(g))).astype(BF16)

    return pl.pallas_call(
        body, name="lru_fwd", grid=(N_RNN_BLOCKS, bsz),
        out_shape=(jax.ShapeDtypeStruct((bsz, s, D_RNN), F32), jax.ShapeDtypeStruct((bsz, s, D_RNN), BF16)),
        in_specs=[sp["up"], sp["gr"], sp["cw"], sp["vec"], sp["wblk"], sp["vec"], sp["wblk"], sp["vec"], sp["vec"]],
        out_specs=(sp["act"], sp["act"]), scratch_shapes=[pltpu.VMEM((s, RNN_BLOCK), F32)] * 2,
        compiler_params=_params())(proj3, proj3, cw, cb, wa, ba, wx, bx, lam)


def _lru_bwd(proj3, h3, dy3, dproj3, cw, cb, wa, ba, wx, bx, lam):
    bsz, s, _ = proj3.shape
    sp = _lru_specs(s)
    n_steps = N_RNN_BLOCKS * bsz

    def body(up_ref, gr_ref, h_ref, dy_ref, cw_ref, cb_ref, wa_ref, ba_ref, wx_ref, bx_ref, lam_ref, dp_in,
             dp_ref, dcw_ref, dcb_ref, dwa_ref, dba_ref, dwx_ref, dbx_ref, dlam_ref, sa_ref, sb_ref,
             dup_st, dgr_st, o_sems):
        del dp_in
        blk = pl.program_id(0)
        b = pl.program_id(1)
        step = blk * bsz + b
        slot = step % 2
        stages = [dup_st, dgr_st]
        dsts = [dp_ref.at[b, :, pl.ds(pl.multiple_of(OFF_RNN_X + blk * RNN_BLOCK, LANES), RNN_BLOCK)],
                dp_ref.at[b, :, pl.ds(pl.multiple_of(OFF_RNN_G + blk * RNN_BLOCK, LANES), RNN_BLOCK)]]
        _staged_reuse(step, stages, dsts, o_sems, slot)
        up = up_ref[...]
        cwv = cw_ref[...]
        u = _conv(up, cwv, cb_ref[...])
        r, i, spv, a, mult = _lru_gates(u, wa_ref, ba_ref, wx_ref, bx_ref, lam_ref)
        h = h_ref[...]
        g = gr_ref[...]
        dy = dy_ref[...]
        sg = _sigmoid(g)
        dgr_st[slot] = (dy * h * (sg * (1.0 + g * (1.0 - sg)))).astype(BF16)
        dh = dy * (g * sg)
        adj = _scan(_shift_up(a, 1, 0.0), dh, sa_ref, sb_ref, reverse=True)
        da = adj * _shift_down(h, 1, 0.0)
        dmult = adj * (i * u)
        di = adj * mult * u
        du = adj * mult * i
        dla = da * a - dmult * (a * a) / mult
        dr = dla * ((-LRU_C) * spv)
        dsp = jnp.sum(dla * ((-LRU_C) * r), axis=0, keepdims=True)
        dza = dr * r * (1.0 - r)
        dzx = di * i * (1.0 - i)
        ub = u.astype(BF16)
        dzab = dza.astype(BF16)
        dzxb = dzx.astype(BF16)
        du = du + _dot_nt(dzab, wa_ref[...].astype(BF16)) + _dot_nt(dzxb, wx_ref[...].astype(BF16))
        dup = cwv[CONV_WIDTH - 1:CONV_WIDTH, :] * du
        for j in range(CONV_WIDTH - 1):
            dup = dup + cwv[j:j + 1, :] * _shift_up(du, CONV_WIDTH - 1 - j, 0.0)
        dup_st[slot] = dup.astype(BF16)
        _staged_flush(step, n_steps, stages, dsts, o_sems, slot)

        @pl.when(b == 0)
        def _():
            for ref in (dcw_ref, dcb_ref, dwa_ref, dba_ref, dwx_ref, dbx_ref, dlam_ref):
                ref[...] = jnp.zeros_like(ref)

        rows = [jnp.sum(du * _shift_down(up, CONV_WIDTH - 1 - j, 0.0), axis=0, keepdims=True)
                for j in range(CONV_WIDTH - 1)]
        rows.append(jnp.sum(du * up, axis=0, keepdims=True))
        dcw_ref[...] += jnp.concatenate(rows, axis=0)
        dcb_ref[...] += jnp.sum(du, axis=0, keepdims=True)
        dwa_ref[...] += _dot_tn(ub, dzab)
        dba_ref[...] += jnp.sum(dza, axis=0, keepdims=True)
        dwx_ref[...] += _dot_tn(ub, dzxb)
        dbx_ref[...] += jnp.sum(dzx, axis=0, keepdims=True)
        dlam_ref[...] += dsp * (-_sigmoid(-lam_ref[...]))

    vec = jax.ShapeDtypeStruct((1, D_RNN), F32)
    wsd = jax.ShapeDtypeStruct((N_RNN_BLOCKS, RNN_BLOCK, RNN_BLOCK), F32)
    return pl.pallas_call(
        body, name="lru_bwd", grid=(N_RNN_BLOCKS, bsz),
        out_shape=(jax.ShapeDtypeStruct(dproj3.shape, dproj3.dtype), jax.ShapeDtypeStruct((CONV_WIDTH, D_RNN), F32),
                   vec, wsd, vec, wsd, vec, vec),
        in_specs=[sp["up"], sp["gr"], sp["act"], sp["act"], sp["cw"], sp["vec"], sp["wblk"], sp["vec"],
                  sp["wblk"], sp["vec"], sp["vec"], ANY],
        out_specs=(ANY, sp["cw"], sp["vec"], sp["wblk"], sp["vec"], sp["wblk"], sp["vec"], sp["vec"]),
        input_output_aliases={11: 0},
        scratch_shapes=[pltpu.VMEM((s, RNN_BLOCK), F32)] * 2 + [pltpu.VMEM((2, s, RNN_BLOCK), BF16)] * 2 + [
            pltpu.SemaphoreType.DMA((4,))],
        compiler_params=_params())(proj3, proj3, h3, dy3, cw, cb, wa, ba, wx, bx, lam, dproj3)


def _rope_tables(s):
    half = ROPE_DIM // 2
    pos = jnp.arange(s, dtype=F32)
    inv_freq = ROPE_THETA ** (-jnp.arange(0, ROPE_DIM, 2, dtype=F32) / ROPE_DIM)
    ang = pos[:, None] * inv_freq[None, :]
    cos, sin = jnp.cos(ang), jnp.sin(ang)
    rest = HEAD_DIM - ROPE_DIM
    cos64 = jnp.concatenate([cos, cos, jnp.ones((s, rest), F32)], axis=1)
    sin64 = jnp.concatenate([-sin, sin, jnp.zeros((s, rest), F32)], axis=1)
    assert half * 2 == ROPE_DIM
    return jnp.tile(cos64, (1, LANES // HEAD_DIM)), jnp.tile(sin64, (1, LANES // HEAD_DIM))


def _swap_rot_halves(v):
    half = ROPE_DIM // 2
    lane = lax.broadcasted_iota(jnp.int32, v.shape, 1) % HEAD_DIM
    second = jnp.where(lane < ROPE_DIM, pltpu.roll(v, half, axis=1), 0.0)
    return jnp.where(lane < half, pltpu.roll(v, LANES - half, axis=1), second)


def _rope(v, cos, sin):
    tiles = []
    for t in range(v.shape[1] // LANES):
        vt = v[:, t * LANES:(t + 1) * LANES]
        tiles.append(vt * cos + _swap_rot_halves(vt) * sin)
    return tiles[0] if len(tiles) == 1 else jnp.concatenate(tiles, axis=1)


def _unrope(v, cos, sin):
    tiles = []
    for t in range(v.shape[1] // LANES):
        vt = v[:, t * LANES:(t + 1) * LANES]
        tiles.append(vt * cos + _swap_rot_halves(vt * sin))
    return tiles[0] if len(tiles) == 1 else jnp.concatenate(tiles, axis=1)


HEADS_PER_STEP = 8
QW = HEADS_PER_STEP * HEAD_DIM
N_PAIRS = N_Q_HEADS // HEADS_PER_STEP
Q_PER_KV = N_Q_HEADS // N_KV_HEADS
KV_PER_STEP = HEADS_PER_STEP // Q_PER_KV


QT_COLS = Q_PER_KV * ATT_BLK


def _attn_saved_shapes(bsz, s):
    nb = s // ATT_BLK
    pad = s + ATT_BLK
    return [(bsz, N_PAIRS, nb, LANES, QT_COLS), (bsz, N_PAIRS, KV_PER_STEP, pad, LANES),
            (bsz, N_PAIRS, KV_PER_STEP, pad, LANES), (bsz, N_PAIRS, LANES, pad)]


def _attn_specs(s, order):
    def mk(width, base, **kw):
        if order == "bp":
            return pl.BlockSpec((None, s, width), lambda b, p: (b, 0, base + p), **kw)
        return pl.BlockSpec((None, s, width), lambda p, b: (b, 0, base + p), **kw)

    def saved(shape, **kw):
        blk = (None, None) + shape[2:]
        zeros = (0,) * (len(shape) - 2)
        if order == "bp":
            return pl.BlockSpec(blk, lambda b, p: (b, p) + zeros, **kw)
        return pl.BlockSpec(blk, lambda p, b: (b, p) + zeros, **kw)

    one = dict(pipeline_mode=pl.Buffered(1))
    tbl = pl.BlockSpec((s, LANES), lambda *_: (0, 0))
    shapes = _attn_saved_shapes(1, s)
    return dict(q=mk(QW, OFF_Q // QW), k=mk(LANES, OFF_K // LANES), v=mk(LANES, OFF_V // LANES),
                g=mk(QW, OFF_ATTN_G // QW), act=mk(QW, 0), kv=mk(LANES, 0), tbl=tbl,
                g1=mk(QW, OFF_ATTN_G // QW, **one), act1=mk(QW, 0, **one),
                saved=[saved(sh) for sh in shapes], saved1=[saved(sh, **one) for sh in shapes],
                smem=pl.BlockSpec(memory_space=pltpu.SMEM))


def _to_qt(blk):
    rows = []
    for j in range(KV_PER_STEP):
        cols = []
        for tt in range(2):
            t = 2 * j + tt
            tr = blk[:, t * LANES:(t + 1) * LANES].T
            cols += [tr[0:HEAD_DIM, :], tr[HEAD_DIM:, :]]
        rows.append(jnp.concatenate(cols, axis=1))
    return jnp.concatenate(rows, axis=0)


def _from_qt(xt):
    tiles = []
    for j in range(KV_PER_STEP):
        for tt in range(2):
            g0 = 2 * tt
            pair = jnp.concatenate([xt[j * HEAD_DIM:(j + 1) * HEAD_DIM, (g0 + i) * ATT_BLK:(g0 + i + 1) * ATT_BLK]
                                    for i in range(2)], axis=0)
            tiles.append(pair.T)
    return jnp.concatenate(tiles, axis=1)


def _attn_prep(q_ref, k_ref, v_ref, cos_ref, sin_ref, qt_ref, km_ref, vm_ref, kt_ref, vt_ref, nb):
    zeros = jnp.zeros((ATT_BLK, LANES), BF16)
    for j in range(KV_PER_STEP):
        km_ref[j, 0:ATT_BLK, :] = zeros
        vm_ref[j, 0:ATT_BLK, :] = zeros
    kt_ref[:, 0:ATT_BLK] = zeros
    vt_ref[:, 0:ATT_BLK] = zeros
    head_of_lane = lax.broadcasted_iota(jnp.int32, (ATT_BLK, LANES), 1) // HEAD_DIM

    def prep(n, carry):
        r0 = pl.multiple_of(n * ATT_BLK, ATT_BLK)
        cs = cos_ref[pl.ds(r0, ATT_BLK), :]
        sn = sin_ref[pl.ds(r0, ATT_BLK), :]
        qt_ref[n] = _to_qt(_rope(q_ref[pl.ds(r0, ATT_BLK), :], cs, sn) * ATTN_SCALE).astype(BF16)
        k = _rope(k_ref[pl.ds(r0, ATT_BLK), :], cs, sn)
        v = v_ref[pl.ds(r0, ATT_BLK), :]
        for j in range(KV_PER_STEP):
            km_ref[j, pl.ds(r0 + ATT_BLK, ATT_BLK), :] = jnp.where(head_of_lane == j, k, 0.0).astype(BF16)
            vm_ref[j, pl.ds(r0 + ATT_BLK, ATT_BLK), :] = jnp.where(head_of_lane == j, v, 0.0).astype(BF16)
        kt_ref[:, pl.ds(r0 + ATT_BLK, ATT_BLK)] = k.T.astype(BF16)
        vt_ref[:, pl.ds(r0 + ATT_BLK, ATT_BLK)] = v.T.astype(BF16)
        return carry

    lax.fori_loop(0, nb, prep, 0)


def _from_prev_block():
    key = lax.broadcasted_iota(jnp.int32, (ATT_BLK, QT_COLS), 0)
    qry = lax.broadcasted_iota(jnp.int32, (ATT_BLK, QT_COLS), 1) % ATT_BLK
    return key > qry


def _fold(tile, prev, prev_bias=None):
    top = tile[:ATT_BLK] if prev_bias is None else tile[:ATT_BLK] + prev_bias
    return jnp.where(prev, top, tile[ATT_BLK:])


def _unfold(folded, prev):
    zero = jnp.zeros_like(folded)
    return jnp.concatenate([jnp.where(prev, folded, zero), jnp.where(prev, zero, folded)], axis=0).astype(BF16)


def _no_prev_bias(n):
    return jnp.where(n == 0, NEG_BIG, 0.0).astype(F32)


def _sink_row(sink_ref, first):
    return jnp.concatenate([jnp.full((1, ATT_BLK), sink_ref[first + g], F32) for g in range(Q_PER_KV)], axis=1)


def _softmax_cols(sc, sink):
    m = jnp.maximum(jnp.max(sc, axis=0, keepdims=True), sink)
    e = jnp.exp(sc - m)
    es = jnp.exp(sink - m)
    inv = 1.0 / (jnp.sum(e, axis=0, keepdims=True) + es)
    return e * inv, es * inv


def _attn_fwd(proj3, sinks, cosf, sinf):
    bsz, s, _ = proj3.shape
    nb = s // ATT_BLK
    sp = _attn_specs(s, "bp")

    def body(sink_ref, q_ref, k_ref, v_ref, g_ref, cos_ref, sin_ref, o_ref, y_ref, qt_sc, km_sc, vm_sc, kt_ref, vt_sc):
        p = pl.program_id(1)
        _attn_prep(q_ref, k_ref, v_ref, cos_ref, sin_ref, qt_sc, km_sc, vm_sc, kt_ref, vt_sc, nb)
        kv_row = lax.broadcasted_iota(jnp.int32, (LANES, QT_COLS), 0) // HEAD_DIM
        prev = _from_prev_block()

        def blk(n, carry):
            r0 = pl.multiple_of(n * ATT_BLK, ATT_BLK)
            bias = _no_prev_bias(n)
            rq = qt_sc[n]
            vt = vt_sc[:, pl.ds(r0, 2 * ATT_BLK)]
            ots = []
            for j in range(KV_PER_STEP):
                st = _dot(km_sc[j, pl.ds(r0, 2 * ATT_BLK), :], rq)
                pc, _ = _softmax_cols(_fold(st, prev, bias), _sink_row(sink_ref, p * HEADS_PER_STEP + j * Q_PER_KV))
                ots.append(_dot(vt, _unfold(pc, prev)))
            o = _from_qt(jnp.where(kv_row == 0, ots[0], ots[1]))
            o_ref[pl.ds(r0, ATT_BLK), :] = o
            g = g_ref[pl.ds(r0, ATT_BLK), :]
            y_ref[pl.ds(r0, ATT_BLK), :] = (o * (g * _sigmoid(g))).astype(BF16)
            return carry

        lax.fori_loop(0, nb, blk, 0, unroll=2)

    res = pl.pallas_call(
        body, name="attn_fwd", grid=(bsz, N_PAIRS),
        out_shape=[jax.ShapeDtypeStruct((bsz, s, D_ATTN), F32), jax.ShapeDtypeStruct((bsz, s, D_ATTN), BF16)] + [
            jax.ShapeDtypeStruct(sh, BF16) for sh in _attn_saved_shapes(bsz, s)],
        in_specs=[sp["smem"], sp["q"], sp["k"], sp["v"], sp["g"], sp["tbl"], sp["tbl"]],
        out_specs=[sp["act"], sp["act"]] + sp["saved"],
        scratch_shapes=[pltpu.VMEM((LANES, s + ATT_BLK), BF16)],
        compiler_params=_params())(sinks, proj3, proj3, proj3, proj3, cosf, sinf)
    return res[0], res[1], res[2:]


def _attn_bwd(proj3, saved, o3, dy3, dproj3, sinks, cosf, sinf):
    bsz, s, _ = proj3.shape
    nb = s // ATT_BLK
    assert nb % 2 == 0
    sp = _attn_specs(s, "pb")
    n_steps = N_PAIRS * bsz

    def body(sink_ref, qt_sc, km_sc, vm_sc, kt_sc, g_ref, o_ref, dy_ref, cos_ref, sin_ref, dp_in,
             dp_ref, ds_ref, dot_sc, dqt_sc, dk_sc, dv_sc, dq_st, dk_st, dv_st, dg_st, o_sems):
        del dp_in
        p = pl.program_id(0)
        b = pl.program_id(1)
        step = p * bsz + b
        slot = step % 2
        stages = [dq_st, dk_st, dv_st, dg_st]
        dsts = [dp_ref.at[b, :, pl.ds(pl.multiple_of(OFF_Q + p * QW, LANES), QW)],
                dp_ref.at[b, :, pl.ds(pl.multiple_of(OFF_K + p * LANES, LANES), LANES)],
                dp_ref.at[b, :, pl.ds(pl.multiple_of(OFF_V + p * LANES, LANES), LANES)],
                dp_ref.at[b, :, pl.ds(pl.multiple_of(OFF_ATTN_G + p * QW, LANES), QW)]]
        _staged_reuse(step, stages, dsts, o_sems, slot)
        dk_sc[...] = jnp.zeros_like(dk_sc)
        dv_sc[...] = jnp.zeros_like(dv_sc)

        def gate(n, carry):
            r0 = pl.multiple_of(n * ATT_BLK, ATT_BLK)
            g = g_ref[pl.ds(r0, ATT_BLK), :]
            dy = dy_ref[pl.ds(r0, ATT_BLK), :]
            sg = _sigmoid(g)
            dg_st[slot, pl.ds(r0, ATT_BLK), :] = (dy * o_ref[pl.ds(r0, ATT_BLK), :] * (sg * (1.0 + g * (1.0 - sg)))).astype(BF16)
            dot_sc[n] = _to_qt(dy * (g * sg)).astype(BF16)
            return carry

        lax.fori_loop(0, nb, gate, 0)
        kv_lane = lax.broadcasted_iota(jnp.int32, (2 * ATT_BLK, LANES), 1) // HEAD_DIM
        kv_row = lax.broadcasted_iota(jnp.int32, (LANES, QT_COLS), 0) // HEAD_DIM
        prev = _from_prev_block()

        def blk(n, acc):
            r0 = pl.multiple_of(n * ATT_BLK, ATT_BLK)
            bias = _no_prev_bias(n)
            rq = qt_sc[n]
            rd = dot_sc[n]
            kt = kt_sc[:, pl.ds(r0, 2 * ATT_BLK)]
            dvs, dks, dqs, new_acc = [], [], [], []
            for j in range(KV_PER_STEP):
                st = _dot(km_sc[j, pl.ds(r0, 2 * ATT_BLK), :], rq)
                pc, ps = _softmax_cols(_fold(st, prev, bias), _sink_row(sink_ref, p * HEADS_PER_STEP + j * Q_PER_KV))
                dpc = _fold(_dot(vm_sc[j, pl.ds(r0, 2 * ATT_BLK), :], rd), prev)
                delta = jnp.sum(pc * dpc, axis=0, keepdims=True)
                dst = _unfold(pc * (dpc - delta), prev)
                new_acc.append(acc[j] + ps * delta)
                dvs.append(_dot_nt(_unfold(pc, prev), rd))
                dks.append(_dot_nt(dst, rq))
                dqs.append(_dot(kt, dst))
            dv_sc[pl.ds(r0, 2 * ATT_BLK), :] += jnp.where(kv_lane == 0, dvs[0], dvs[1])
            dk_sc[pl.ds(r0, 2 * ATT_BLK), :] += jnp.where(kv_lane == 0, dks[0], dks[1])
            dqt_sc[n] = jnp.where(kv_row == 0, dqs[0], dqs[1]) * ATTN_SCALE
            return tuple(new_acc)

        def blk_pair(m, acc):
            return blk(2 * m + 1, blk(2 * m, acc))

        acc = lax.fori_loop(0, nb // 2, blk_pair, tuple(jnp.zeros((1, QT_COLS), F32) for _ in range(KV_PER_STEP)))
        lane1 = lax.broadcasted_iota(jnp.int32, (1, LANES), 1)
        dsink = jnp.zeros((1, LANES), F32)
        for j in range(KV_PER_STEP):
            for i in range(Q_PER_KV):
                part = jnp.sum(acc[j][:, i * ATT_BLK:(i + 1) * ATT_BLK], axis=1, keepdims=True)
                dsink = dsink - jnp.where(lane1 == j * Q_PER_KV + i, part, 0.0)

        @pl.when(b == 0)
        def _():
            ds_ref[...] = jnp.zeros_like(ds_ref)

        ds_ref[...] += dsink

        def post(n, carry):
            r0 = pl.multiple_of(n * ATT_BLK, ATT_BLK)
            cs = cos_ref[pl.ds(r0, ATT_BLK), :]
            sn = sin_ref[pl.ds(r0, ATT_BLK), :]
            dq_st[slot, pl.ds(r0, ATT_BLK), :] = _unrope(_from_qt(dqt_sc[n]), cs, sn).astype(BF16)
            dk_st[slot, pl.ds(r0, ATT_BLK), :] = _unrope(dk_sc[pl.ds(r0 + ATT_BLK, ATT_BLK), :], cs, sn).astype(BF16)
            dv_st[slot, pl.ds(r0, ATT_BLK), :] = dv_sc[pl.ds(r0 + ATT_BLK, ATT_BLK), :].astype(BF16)
            return carry

        lax.fori_loop(0, nb, post, 0)
        _staged_flush(step, n_steps, stages, dsts, o_sems, slot)

    n_in = 1 + len(saved) + 5
    return pl.pallas_call(
        body, name="attn_bwd", grid=(N_PAIRS, bsz),
        out_shape=(jax.ShapeDtypeStruct(dproj3.shape, dproj3.dtype), jax.ShapeDtypeStruct((N_PAIRS, 1, LANES), F32)),
        in_specs=[sp["smem"]] + sp["saved1"] + [sp["g1"], sp["act1"], sp["act1"], sp["tbl"], sp["tbl"], ANY],
        out_specs=(ANY, pl.BlockSpec((None, 1, LANES), lambda p, b: (p, 0, 0))),
        input_output_aliases={n_in: 0},
        scratch_shapes=[pltpu.VMEM((nb, LANES, QT_COLS), BF16),
                        pltpu.VMEM((nb, LANES, QT_COLS), F32),
                        pltpu.VMEM((s + ATT_BLK, LANES), F32),
                        pltpu.VMEM((s + ATT_BLK, LANES), F32),
                        pltpu.VMEM((2, s, QW), BF16), pltpu.VMEM((2, s, LANES), BF16),
                        pltpu.VMEM((2, s, LANES), BF16), pltpu.VMEM((2, s, QW), BF16),
                        pltpu.SemaphoreType.DMA((8,))],
        compiler_params=_params())(sinks, *saved, proj3, o3, dy3, cosf, sinf, dproj3)


def _staged_copies(stages, dsts, sems, slot):
    return [pltpu.make_async_copy(st.at[slot], dst, sems.at[slot * len(stages) + t])
            for t, (st, dst) in enumerate(zip(stages, dsts))]


def _staged_reuse(step, stages, dsts, sems, slot):
    @pl.when(step >= 2)
    def _():
        for cp in _staged_copies(stages, dsts, sems, slot):
            cp.wait()


def _staged_flush(step, n_steps, stages, dsts, sems, slot):
    for cp in _staged_copies(stages, dsts, sems, slot):
        cp.start()

    @pl.when(step == n_steps - 1)
    def _():
        for cp in _staged_copies(stages, dsts, sems, slot):
            cp.wait()
        if n_steps >= 2:
            for cp in _staged_copies(stages, dsts, sems, 1 - slot):
                cp.wait()


def _merge_fwd_bwd(x, tgt, y_rnn, y_attn, proj, w_r, w_a, w_o, gf):
    t, d = x.shape
    tm = min(t, 256)
    nt = t // tm

    hw = d // 2

    def body(x_ref, t_ref, yr_ref, ya_ref, mr0_ref, mr1_ref, ma0_ref, ma1_ref, wr_ref, wa_ref, wo_ref, gf_ref,
             dp_ref, dyr_ref, dya_ref, mg_ref, dx2_ref, dx2b_ref, dpr_ref, dpa_ref, loss_ref, dgf_ref, dmg_st, o_sems):
        i = pl.program_id(0)
        slot = i % 2
        dsts = [dp_ref.at[pl.ds(pl.multiple_of(i * tm, tm), tm), pl.ds(OFF_MERGE_R, 2 * d)]]
        _staged_reuse(i, [dmg_st], dsts, o_sems, slot)
        wr = wr_ref[...]
        wa = wa_ref[...]
        wo = wo_ref[...]
        gfv = gf_ref[...]
        pr = _dot(yr_ref[...], wr)
        pa = _dot(ya_ref[...], wa)
        sr = _sigmoid(jnp.concatenate([mr0_ref[...], mr1_ref[...]], axis=1))
        sa = _sigmoid(jnp.concatenate([ma0_ref[...], ma1_ref[...]], axis=1))
        mb = (sr * pr + sa * pa).astype(BF16)
        mg_ref[...] = mb
        x2 = x_ref[...] + _dot(mb, wo)
        r2 = lax.rsqrt(jnp.mean(x2 * x2, axis=-1, keepdims=True) + NORM_EPS)
        nrm = x2 * r2
        err = nrm * gfv - t_ref[...]
        dy = err * (1.0 / d)
        dn = dy * gfv
        dx2 = r2 * (dn - nrm * jnp.mean(dn * nrm, axis=-1, keepdims=True))
        dx2_ref[...] = dx2
        dx2b = dx2.astype(BF16)
        dx2b_ref[...] = dx2b
        dmerged = _dot_nt(dx2b, wo)
        dpr = (dmerged * sr).astype(BF16)
        dpa = (dmerged * sa).astype(BF16)
        dpr_ref[...] = dpr
        dpa_ref[...] = dpa
        dmg_st[slot, :, 0:d] = (dmerged * pr * (sr * (1.0 - sr))).astype(BF16)
        dmg_st[slot, :, d:2 * d] = (dmerged * pa * (sa * (1.0 - sa))).astype(BF16)
        _staged_flush(i, nt, [dmg_st], dsts, o_sems, slot)
        dyr_ref[...] = _dot_nt(dpr, wr)
        dya_ref[...] = _dot_nt(dpa, wa)

        @pl.when(i == 0)
        def _():
            loss_ref[...] = jnp.zeros_like(loss_ref)
            dgf_ref[...] = jnp.zeros_like(dgf_ref)

        loss_ref[...] += jnp.full((1, LANES), 0.5 / d, F32) * jnp.sum(err * err)
        dgf_ref[...] += jnp.sum(dy * nrm, axis=0, keepdims=True)

    tile = pl.BlockSpec((tm, d), lambda i: (i, 0))
    wsp = pl.BlockSpec((d, d), lambda i: (0, 0))

    def gate(col_blk):
        return pl.BlockSpec((tm, hw), lambda i: (i, col_blk))

    fb = jax.ShapeDtypeStruct((t, d), BF16)
    ff = jax.ShapeDtypeStruct((t, d), F32)
    return pl.pallas_call(
        body, name="merge_fwd_bwd", grid=(nt,),
        out_shape=(jax.ShapeDtypeStruct((t, D_IN), BF16), ff, ff, fb, ff, fb, fb, fb,
                   jax.ShapeDtypeStruct((1, LANES), F32), jax.ShapeDtypeStruct((1, d), F32)),
        in_specs=[tile, tile, tile, tile] + [gate(OFF_MERGE_R // hw + j) for j in range(4)] + [
            wsp, wsp, wsp, pl.BlockSpec((1, d), lambda i: (0, 0))],
        out_specs=(ANY, tile, tile, tile, tile, tile, tile, tile,
                   pl.BlockSpec((1, LANES), lambda i: (0, 0)), pl.BlockSpec((1, d), lambda i: (0, 0))),
        scratch_shapes=[pltpu.VMEM((2, tm, 2 * d), BF16), pltpu.SemaphoreType.DMA((2,))],
        compiler_params=_params())(x, tgt, y_rnn, y_attn, proj, proj, proj, proj, w_r, w_a, w_o, gf)


def _local_grads(x, tgt, h, proj, norm_g, w_in_bm, conv_w, conv_b, lru_w_a, lru_b_a, lru_w_x, lru_b_x, lam, sinks,
                 w_r, w_a, w_o, gf):
    bsz, s, d = x.shape
    t = bsz * s
    x2 = x.reshape(t, d)
    proj3 = proj.reshape(bsz, s, D_IN)
    h_lru, y_rnn = _lru_fwd(proj3, conv_w, conv_b, lru_w_a, lru_b_a, lru_w_x, lru_b_x, lam)
    cosf, sinf = _rope_tables(s)
    o_attn, y_attn, attn_saved = _attn_fwd(proj3, sinks, cosf, sinf)
    y_rnn2 = y_rnn.reshape(t, d)
    y_attn2 = y_attn.reshape(t, d)
    dproj, dyr, dya, merged, dx2, dx2b, dpr, dpa, loss, dgf = _merge_fwd_bwd(
        x2, tgt.reshape(t, d), y_rnn2, y_attn2, proj, w_r, w_a, w_o, gf)
    dproj3, dsink = _attn_bwd(proj3, attn_saved, o_attn, dya.reshape(bsz, s, d), dproj.reshape(bsz, s, D_IN),
                              sinks, cosf, sinf)
    dproj3, dcw, dcb, dwa, dba, dwx, dbx, dlam = _lru_bwd(
        proj3, h_lru, dyr.reshape(bsz, s, d), dproj3, conv_w, conv_b, lru_w_a, lru_b_a, lru_w_x, lru_b_x, lam)
    dproj = dproj3.reshape(t, D_IN)
    dh = _grad_h(dproj, w_in_bm)
    grad_x, dng = _rmsnorm_bwd(x2, dh, dx2, norm_g)
    small = dict(norm_g=dng, conv_w=dcw, conv_b=dcb, lru_w_a=dwa, lru_b_a=dba, lru_w_x=dwx, lru_b_x=dbx,
                 lru_lambda=dlam, attn_sinks=dsink[:, 0, :HEADS_PER_STEP].reshape(1, N_Q_HEADS), final_norm_g=dgf)
    squares = [(y_rnn2, dpr), (y_attn2, dpa), (merged, dx2b)]
    return loss[0, 0], grad_x.reshape(bsz, s, d), h, dproj, squares, small


ANY = pl.BlockSpec(memory_space=pl.ANY)


def _mesh_pos():
    return lax.axis_index("x"), lax.axis_index("y"), lax.axis_index("c")


def _remote(src, dst, send_sems, recv_sems, idx, peer):
    return pltpu.make_async_remote_copy(src_ref=src, dst_ref=dst, send_sem=send_sems.at[idx],
                                        recv_sem=recv_sems.at[idx], device_id=peer, device_id_type=MESH)


def _gather_in_proj(h, bufs, split, idx):
    t_tok, d = h.shape
    n = len(bufs)
    tm = min(t_tok, 512)
    nt = t_tok // tm
    n_fwd = 3 * sum(split)
    assert split[0]

    def body(idx_ref, h_ref, *refs):
        ins, proj_ref, outs = refs[:n], refs[n], refs[n + 1:2 * n + 1]
        wbuf, send_sems, recv_sems, fsend_sems, frecv_sems, l_sems = refs[2 * n + 1:]
        j, i = pl.program_id(0), pl.program_id(1)
        x, y, c = _mesh_pos()
        me = 2 * x + y
        sib = (x, y, 1 - c)
        peers = [((x, 1 - y, c), me ^ 1), ((1 - x, y, c), me ^ 2), ((1 - x, 1 - y, c), me ^ 3)]

        def part(ref, slot, t, half):
            if not split[t]:
                return ref.at[slot]
            hr = bufs[t].shape[1] // 2
            return ref.at[slot, pl.ds(pl.multiple_of(half * hr, 8), hr), :]

        def land(t):
            return wbuf if t == 0 else outs[t]

        def ici(t, k):
            peer, pj = peers[k]
            src = part(ins[t], me, t, c)
            return (_remote(src, part(land(t), me, t, c), send_sems, recv_sems, 3 * t + k, peer),
                    _remote(src, part(land(t), pj, t, c), send_sems, recv_sems, 3 * t + k, peer))

        fwd_index = {}
        for t in range(n):
            if split[t]:
                for k in range(3):
                    fwd_index[(t, k)] = len(fwd_index)

        def forward(t, k):
            pj = peers[k][1]
            got = part(land(t), pj, t, c)
            f = fwd_index[(t, k)]
            return (_remote(got, got, fsend_sems, frecv_sems, f, sib),
                    _remote(got, part(land(t), pj, t, 1 - c), fsend_sems, frecv_sems, f, sib))

        def write_back(k):
            pj = peers[k][1]
            return pltpu.make_async_copy(wbuf.at[pj], outs[0].at[pj], l_sems.at[1 + k])

        relay_peer = ((x + c) % 2, (y + 1 - c) % 2, c)

        def relay():
            got = part(wbuf, me ^ (2 - c), 0, c)
            return (_remote(got, got, send_sems, recv_sems, 2, relay_peer),
                    _remote(got, part(wbuf, me ^ 3, 0, c), send_sems, recv_sems, 2, relay_peer))

        direct = [(t, k) for t in range(n) for k in range(3) if (t, k) != (0, 2)]

        @pl.when((j == 0) & (i == 0))
        def _():
            for t, k in direct:
                ici(t, k)[0].start()
            own = pltpu.make_async_copy(ins[0].at[me], wbuf.at[me], l_sems.at[0])
            own.start()
            own.wait()

        @pl.when((j == 1) & (i == 0))
        def _():
            for k in range(2):
                ici(0, k)[1].wait_recv()
            relay()[0].start()
            for k in range(2):
                forward(0, k)[0].start()
            forward(0, 0)[1].wait_recv()
            write_back(0).start()

        @pl.when((j == 2) & (i == 0))
        def _():
            forward(0, 1)[1].wait_recv()
            write_back(1).start()

        @pl.when((j == 3) & (i == 0))
        def _():
            relay()[1].wait_recv()
            forward(0, 2)[0].start()
            forward(0, 2)[1].wait_recv()
            write_back(2).start()

        proj_ref[...] = _dot(h_ref[...], wbuf[me ^ j])

        @pl.when((j == N_CHIPS - 1) & (i == nt - 1))
        def _():
            for t in range(1, n):
                for k in range(3):
                    ici(t, k)[1].wait_recv()
                    if split[t]:
                        forward(t, k)[0].start()
            relay()[0].wait_send()
            for t, k in direct:
                ici(t, k)[0].wait_send()
            for t in range(n):
                if split[t]:
                    for k in range(3):
                        forward(t, k)[0].wait_send()
                        if t > 0:
                            forward(t, k)[1].wait_recv()
            for k in range(3):
                write_back(k).wait()

    grid_spec = pltpu.PrefetchScalarGridSpec(
        num_scalar_prefetch=1, grid=(N_CHIPS, nt),
        in_specs=[pl.BlockSpec((tm, d), lambda j, i, idx_ref: (i, 0))] + [ANY] * n,
        out_specs=[pl.BlockSpec((tm, W_BLK), lambda j, i, idx_ref: (i, idx_ref[0] ^ j))] + [ANY] * n,
        scratch_shapes=[pltpu.VMEM(bufs[0].shape, bufs[0].dtype),
                        pltpu.SemaphoreType.DMA((3 * n,)), pltpu.SemaphoreType.DMA((3 * n,)),
                        pltpu.SemaphoreType.DMA((n_fwd,)), pltpu.SemaphoreType.DMA((n_fwd,)),
                        pltpu.SemaphoreType.DMA((4,))])
    out_shape = [jax.ShapeDtypeStruct((t_tok, D_IN), F32)] + [jax.ShapeDtypeStruct(a.shape, a.dtype) for a in bufs]
    res = pl.pallas_call(
        body, name="gather_in_proj", grid_spec=grid_spec, out_shape=out_shape,
        input_output_aliases={2 + t: 1 + t for t in range(n)}, compiler_params=_params())(idx, h, *bufs)
    return res[0], res[1:]


def _row_tile(rows, row_bytes, cap_bytes=2 * 1024 * 1024):
    best = None
    for tr in range(8, rows + 1, 8):
        if rows % tr == 0 and tr * row_bytes <= cap_bytes:
            best = tr
    return best if best is not None else rows


XOR_ORDER = (3, 2, 1)


def _grads_reduce_scatter(h, dproj, squares, small, idx):
    t, d = h.shape
    nsq = len(squares)
    hr = d // 2
    qr = ROW_BLK // 2
    tk = min(t, 1024)
    nk = t // tk
    last = N_CHIPS - 1
    n_phase = 3

    def dest(s, idx_ref):
        xo = jnp.where(s == 0, XOR_ORDER[0], jnp.where(s == 1, XOR_ORDER[1], jnp.where(s == 2, XOR_ORDER[2], 0)))
        return idx_ref[0] ^ xo

    def k_sq(p, k):
        return jnp.where(p == 0, k, nk - 1)

    def k_w(p, k):
        return jnp.where(p == 0, 0, k)

    in_specs = [
        pl.BlockSpec((tk, hr), lambda s, p, k, idx_ref: (k_w(p, k), (1 - idx_ref[1] + jnp.maximum(p - 1, 0)) % 2)),
        pl.BlockSpec((tk, W_BLK), lambda s, p, k, idx_ref: (k_w(p, k), dest(s, idx_ref)))]
    for q in range(nsq):
        in_specs.append(pl.BlockSpec((tk, ROW_BLK), lambda s, p, k, idx_ref: (k_sq(p, k), dest(s, idx_ref))))
        in_specs.append(pl.BlockSpec((tk, d), lambda s, p, k, idx_ref: (k_sq(p, k), 0)))

    def body(idx_ref, *refs):
        nj = 1 + nsq
        h_ref, dp_ref = refs[0], refs[1]
        sq_in = refs[2:2 + 2 * nsq]
        small_in = refs[2 * nj]
        outs = refs[2 * nj + 1:3 * nj + 2]
        landing = refs[3 * nj + 2:4 * nj + 3]
        sc = refs[4 * nj + 3:]
        acc_w, xr_w, sb_w = sc[0:3]
        sq_sc = [sc[3 + 3 * q:6 + 3 * q] for q in range(nsq)]
        sm, smx = sc[3 * nj:3 * nj + 2]
        x_send, x_recv, i_send, i_recv, f_send, f_recv, o_sem, l_sem = sc[3 * nj + 2:]
        s, p, k = pl.program_id(0), pl.program_id(1), pl.program_id(2)
        x, y, c = _mesh_pos()
        sib = (x, y, 1 - c)
        peers = [((1 - x) if xo & 2 else x, (1 - y) if xo & 1 else y, c) for xo in XOR_ORDER]
        slot = s % 2
        mine_w = pl.ds(pl.multiple_of(c * hr, 8), hr)
        theirs_w = pl.ds(pl.multiple_of((1 - c) * hr, 8), hr)
        mine_q = pl.ds(pl.multiple_of(c * qr, 8), qr)
        theirs_q = pl.ds(pl.multiple_of((1 - c) * qr, 8), qr)

        def exch(j, src, dst):
            return _remote(src, dst, x_send, x_recv, 2 * j + slot, sib)

        sbufs = [sb_w] + [sq_sc[q][2] for q in range(nsq)]

        def ici(j, ss):
            return _remote(sbufs[j].at[ss], landing[j].at[ss], i_send, i_recv, last * j + ss, peers[ss])

        def exchanges():
            cps = [exch(0, acc_w.at[0], xr_w.at[slot])]
            cps += [exch(1 + q, sq_sc[q][0].at[theirs_q, :], sq_sc[q][1].at[slot]) for q in range(nsq)]
            return cps

        def small_send(ss):
            return _remote(sm.at[c], landing[nj].at[ss], i_send, i_recv, last * nj + ss, peers[ss])

        def small_start():
            load = pltpu.make_async_copy(small_in, sm, l_sem.at[nj + 1])
            load.start()
            load.wait()
            swap = _remote(sm, smx.at[pl.ds(0, 2)], x_send, x_recv, 2 * nj, sib)
            swap.start()
            swap.wait_recv()
            swap.wait_send()
            sm[...] = sm[...] + smx[0:2]
            for ss in range(last):
                small_send(ss).start()

        def pair_ref(j):
            return acc_w.at[1] if j == 0 else sq_sc[j - 1][0].at[mine_q, :]

        def sq_phase():
            pl.when((s == 0) & (k == 0))(small_start)
            for q in range(nsq):
                acc = sq_sc[q][0]

                @pl.when(k == 0)
                def _():
                    acc[...] = jnp.zeros((ROW_BLK, d), F32)

                acc[...] += _dot_tn(sq_in[2 * q][...], sq_in[2 * q + 1][...])

            @pl.when(k == nk - 1)
            def _():
                for cp in exchanges()[1:]:
                    cp.start()

        def w_phase(hf):
            @pl.when(k == 0)
            def _():
                acc_w[hf] = jnp.zeros((hr, W_BLK), F32)

            acc_w[hf] += _dot_tn(h_ref[...], dp_ref[...])

            @pl.when(k == nk - 1)
            def _():
                if hf == 0:
                    exchanges()[0].start()
                else:
                    finish_step()

        def finish_step():
            for cp in exchanges():
                cp.wait_recv()
                cp.wait_send()
            acc_w[1] += xr_w[slot]
            for q in range(nsq):
                sq_sc[q][0][mine_q, :] += sq_sc[q][1][slot]
            for ss in range(last):
                @pl.when(s == ss)
                def _():
                    for j in range(nj):
                        sbufs[j][ss] = pair_ref(j)[...].astype(BF16)
                        ici(j, ss).start()

            @pl.when(s == last)
            def _():
                for ss in range(last):
                    for j in range(nj):
                        ici(j, ss).wait_recv()
                        ici(j, ss).wait_send()
                    small_send(ss).wait_recv()
                    small_send(ss).wait_send()
                stage = [pltpu.make_async_copy(landing[j], sbufs[j], l_sem.at[j]) for j in range(nj)]
                stage.append(pltpu.make_async_copy(landing[nj], smx, l_sem.at[nj]))
                for cp in stage:
                    cp.start()
                for j in range(nj):
                    stage[j].wait()
                    total = pair_ref(j)[...]
                    for ss in range(last):
                        total = total + sbufs[j][ss].astype(F32)
                    pair_ref(j)[...] = total
                stage[nj].wait()
                by_xor = {xo: smx[ss] for ss, xo in enumerate(XOR_ORDER)}
                sm[c] = (sm[c] + by_xor[1]) + (by_xor[2] + by_xor[3])
                done = [(acc_w.at[1], outs[0].at[mine_w, :], outs[0].at[theirs_w, :])]
                done += [(pair_ref(1 + q), outs[1 + q].at[mine_q, :], outs[1 + q].at[theirs_q, :]) for q in range(nsq)]
                done.append((sm.at[c], outs[nj].at[c], outs[nj].at[1 - c]))
                copies = []
                for j, (src, mine, theirs) in enumerate(done):
                    keep = pltpu.make_async_copy(src, mine, o_sem.at[j])
                    give = _remote(src, mine, f_send, f_recv, j, sib)
                    take = _remote(src, theirs, f_send, f_recv, j, sib)
                    keep.start()
                    give.start()
                    copies.append((keep, give, take))
                for keep, give, take in copies:
                    keep.wait()
                    give.wait_send()
                    take.wait_recv()

        pl.when(p == 0)(sq_phase)
        for hf in range(2):
            pl.when(p == 1 + hf)(functools.partial(w_phase, hf))

    nj = 1 + nsq
    scratch = [pltpu.VMEM((2, hr, W_BLK), F32), pltpu.VMEM((2, hr, W_BLK), F32), pltpu.VMEM((last, hr, W_BLK), BF16)]
    for _ in range(nsq):
        scratch += [pltpu.VMEM((ROW_BLK, d), F32), pltpu.VMEM((2, qr, d), F32), pltpu.VMEM((last, qr, d), BF16)]
    scratch += [pltpu.VMEM((2, PK_HALF, LANES), F32), pltpu.VMEM((last, PK_HALF, LANES), F32)]
    scratch += [pltpu.SemaphoreType.DMA((2 * nj + 1,)), pltpu.SemaphoreType.DMA((2 * nj + 1,)),
                pltpu.SemaphoreType.DMA((last * (nj + 1),)), pltpu.SemaphoreType.DMA((last * (nj + 1),)),
                pltpu.SemaphoreType.DMA((nj + 1,)), pltpu.SemaphoreType.DMA((nj + 1,)),
                pltpu.SemaphoreType.DMA((nj + 1,)), pltpu.SemaphoreType.DMA((nj + 2,))]
    grid_spec = pltpu.PrefetchScalarGridSpec(
        num_scalar_prefetch=1, grid=(N_CHIPS, n_phase, nk), in_specs=in_specs + [ANY],
        out_specs=[ANY] * (2 * nj + 2), scratch_shapes=scratch)
    out_shape = [jax.ShapeDtypeStruct((d, W_BLK), F32)] + [jax.ShapeDtypeStruct((ROW_BLK, d), F32)] * nsq
    out_shape.append(jax.ShapeDtypeStruct((2, PK_HALF, LANES), F32))
    out_shape += [jax.ShapeDtypeStruct((last, hr, W_BLK), BF16)] + [jax.ShapeDtypeStruct((last, qr, d), BF16)] * nsq
    out_shape.append(jax.ShapeDtypeStruct((last, PK_HALF, LANES), F32))
    flat = [a for pair in squares for a in pair]
    res = pl.pallas_call(body, name="grads_reduce_scatter", grid_spec=grid_spec, out_shape=out_shape,
                         compiler_params=_params())(idx, h, dproj, *flat, small)
    return res[:nj + 1]


_VEC_NAMES = ("norm_g", "conv_b", "lru_b_a", "lru_b_x", "lru_lambda", "final_norm_g")


def _pack_small(p, conv_full=None, scalar=None):
    rows = [p["lru_w_a"].reshape(PK_WX - PK_WA, LANES), p["lru_w_x"].reshape(PK_VEC - PK_WX, LANES)]
    rows += [p[k].reshape(8, LANES) for k in _VEC_NAMES]
    rows.append(jnp.pad(p["attn_sinks"].reshape(1, N_Q_HEADS), ((0, 7), (0, LANES - N_Q_HEADS))))
    rows.append(jnp.zeros((32, LANES), F32) if conv_full is None else conv_full.reshape(32, LANES))
    tail = PK_ROWS - PK_SCALAR
    if scalar is None:
        rows.append(jnp.zeros((tail, LANES), F32))
    else:
        rows.append(jnp.pad(scalar.reshape(1, 1), ((0, tail - 1), (0, LANES - 1))))
    return jnp.concatenate(rows, axis=0)


def _unpack_small(pk, like):
    out = {"lru_w_a": pk[PK_WA:PK_WX].reshape(like["lru_w_a"].shape),
           "lru_w_x": pk[PK_WX:PK_VEC].reshape(like["lru_w_x"].shape)}
    for j, k in enumerate(_VEC_NAMES):
        out[k] = pk[PK_VEC + 8 * j:PK_VEC + 8 * j + 8].reshape(like[k].shape)
    out["attn_sinks"] = pk[PK_SINK:PK_SINK + 1, :N_Q_HEADS].reshape(like["attn_sinks"].shape)
    return out


_WEIGHTS = ("norm_g", "w_in", "conv_w", "conv_b", "lru_w_a", "lru_b_a", "lru_w_x", "lru_b_x", "lru_lambda",
            "attn_sinks", "w_rnn_out", "w_attn_out", "w_o", "final_norm_g")
_SMALL = ("norm_g", "conv_b", "lru_w_a", "lru_b_a", "lru_w_x", "lru_b_x", "lru_lambda", "attn_sinks", "final_norm_g")
_ROW_SHARDED = ("w_rnn_out", "w_attn_out", "w_o")


def kernel(x, norm_g, w_in, conv_w, conv_b, lru_w_a, lru_b_a, lru_w_x, lru_b_x, lru_lambda, attn_sinks, w_rnn_out, w_attn_out, w_o, final_norm_g, loss_target, m_norm_g, m_w_in, m_conv_w, m_conv_b, m_lru_w_a, m_lru_b_a, m_lru_w_x, m_lru_b_x, m_lru_lambda, m_attn_sinks, m_w_rnn_out, m_w_attn_out, m_w_o, m_final_norm_g, v_norm_g, v_w_in, v_conv_w, v_conv_b, v_lru_w_a, v_lru_b_a, v_lru_w_x, v_lru_b_x, v_lru_lambda, v_attn_sinks, v_w_rnn_out, v_w_attn_out, v_w_o, v_final_norm_g):
    w = dict(norm_g=norm_g, w_in=w_in, conv_w=conv_w, conv_b=conv_b, lru_w_a=lru_w_a, lru_b_a=lru_b_a, lru_w_x=lru_w_x,
             lru_b_x=lru_b_x, lru_lambda=lru_lambda, attn_sinks=attn_sinks, w_rnn_out=w_rnn_out, w_attn_out=w_attn_out,
             w_o=w_o, final_norm_g=final_norm_g)
    m = dict(norm_g=m_norm_g, w_in=m_w_in, conv_w=m_conv_w, conv_b=m_conv_b, lru_w_a=m_lru_w_a, lru_b_a=m_lru_b_a,
             lru_w_x=m_lru_w_x, lru_b_x=m_lru_b_x, lru_lambda=m_lru_lambda, attn_sinks=m_attn_sinks,
             w_rnn_out=m_w_rnn_out, w_attn_out=m_w_attn_out, w_o=m_w_o, final_norm_g=m_final_norm_g)
    v = dict(norm_g=v_norm_g, w_in=v_w_in, conv_w=v_conv_w, conv_b=v_conv_b, lru_w_a=v_lru_w_a, lru_b_a=v_lru_b_a,
             lru_w_x=v_lru_w_x, lru_b_x=v_lru_b_x, lru_lambda=v_lru_lambda, attn_sinks=v_attn_sinks,
             w_rnn_out=v_w_rnn_out, w_attn_out=v_w_attn_out, w_o=v_w_o, final_norm_g=v_final_norm_g)
    mx, my, mc = _mesh_pos()
    me = 2 * mx + my
    d = D_MODEL

    slot0 = jnp.stack([me, jnp.zeros_like(me)]).astype(jnp.int32)
    bufs = [_put_slot(w[k][0], N_CHIPS, slot0, w[k].shape[1], BF16, "cast_" + k) for k in ("w_in",) + _ROW_SHARDED]
    bufs.append(_put_slot(w["conv_w"][0], N_CHIPS, slot0, CONV_WIDTH, F32, "slot_conv_w"))
    h = _rmsnorm_fwd(x.reshape(-1, d), w["norm_g"])
    proj, (g_in, g_r, g_a, g_o, g_cw) = _gather_in_proj(h, bufs, [True, True, True, True, False],
                                                        jnp.reshape(me, (1,)).astype(jnp.int32))
    conv_full = g_cw.transpose(1, 0, 2).reshape(CONV_WIDTH, D_RNN)

    loss_local, grad_x, h, dproj, squares, gsmall = _local_grads(
        x, loss_target, h, proj, w["norm_g"], g_in, conv_full, w["conv_b"], w["lru_w_a"][0], w["lru_b_a"], w["lru_w_x"][0],
        w["lru_b_x"], w["lru_lambda"], w["attn_sinks"][0], g_r.reshape(d, d), g_a.reshape(d, d), g_o.reshape(d, d),
        w["final_norm_g"].reshape(1, d))
    gpack = _pack_small(gsmall, gsmall["conv_w"], loss_local).reshape(2, PK_HALF, LANES)
    f_in, f_r, f_a, f_o, spack = _grads_reduce_scatter(h, dproj, squares, gpack, jnp.stack([me, mc]).astype(jnp.int32))
    spack = spack.reshape(PK_ROWS, LANES)
    loss = spack[PK_SCALAR, 0]

    grads = _unpack_small(spack, w)
    conv_all = spack[PK_CONV:PK_CONV + 32].reshape(CONV_WIDTH, D_RNN)
    grads["conv_w"] = lax.dynamic_slice_in_dim(conv_all, me * (D_RNN // N_CHIPS), D_RNN // N_CHIPS, axis=1)[None]
    grads["w_in"] = f_in[None]
    grads["w_rnn_out"], grads["w_attn_out"], grads["w_o"] = f_r[None], f_a[None], f_o[None]

    delta, new_m, new_v = {}, {}, {}
    for k in ("w_in",) + _ROW_SHARDED:
        dk, mk, vk = _adamw(w[k][0], grads[k][0], m[k][0], v[k][0], "adamw_" + k)
        delta[k], new_m[k], new_v[k] = dk[None], mk[None], vk[None]
    shp = (2 * CONV_WIDTH, LANES)
    dk, mk, vk = _adamw(w["conv_w"].reshape(shp), grads["conv_w"].reshape(shp), m["conv_w"].reshape(shp),
                        v["conv_w"].reshape(shp), "adamw_conv_w")
    delta["conv_w"], new_m["conv_w"], new_v["conv_w"] = (a.reshape(w["conv_w"].shape) for a in (dk, mk, vk))
    dk, mk, vk = _adamw(_pack_small(w), spack, _pack_small(m), _pack_small(v), "adamw_small")
    for src, dst in ((dk, delta), (mk, new_m), (vk, new_v)):
        dst.update(_unpack_small(src, w))

    return (loss, grad_x, *[grads[k] for k in _WEIGHTS], *[delta[k] for k in _WEIGHTS],
            *[new_m[k] for k in _WEIGHTS], *[new_v[k] for k in _WEIGHTS])
```

```python
import functools
import math

import jax
import jax.numpy as jnp
from jax import lax
from jax.experimental import pallas as pl
from jax.experimental.pallas import tpu as pltpu

F32 = jnp.float32
BF16 = jnp.bfloat16
MESH = pl.DeviceIdType.MESH

D_MODEL = 1024
D_RNN = 1024
N_RNN_BLOCKS = 8
RNN_BLOCK = D_RNN // N_RNN_BLOCKS
CONV_WIDTH = 4
LRU_C = 8.0
HEAD_DIM = 64
N_Q_HEADS = 16
N_KV_HEADS = 4
D_ATTN = N_Q_HEADS * HEAD_DIM
D_KV = N_KV_HEADS * HEAD_DIM
WINDOW = 128
ROPE_DIM = HEAD_DIM // 4
ROPE_THETA = 500000.0
NORM_EPS = 1e-6
OFF_RNN_X = 0
OFF_RNN_G = OFF_RNN_X + D_RNN
OFF_Q = OFF_RNN_G + D_RNN
OFF_K = OFF_Q + D_ATTN
OFF_V = OFF_K + D_KV
OFF_ATTN_G = OFF_V + D_KV
OFF_MERGE_R = OFF_ATTN_G + D_ATTN
OFF_MERGE_A = OFF_MERGE_R + D_MODEL
D_IN = OFF_MERGE_A + D_MODEL

ADAM_LR = 0.001
ADAM_B1 = 0.9
ADAM_B2 = 0.999
ADAM_EPS = 1e-08
ADAM_WD = 0.01
ADAM_STEP = 10

N_CHIPS = 4
W_BLK = D_IN // N_CHIPS
ROW_BLK = D_MODEL // N_CHIPS
LANES = 128
ATT_BLK = 128
VMEM_LIMIT = 56 * 1024 * 1024
NEG_BIG = -1e30
ATTN_SCALE = 1.0 / math.sqrt(HEAD_DIM)

PK_WA = 0
PK_WX = PK_WA + N_RNN_BLOCKS * RNN_BLOCK
PK_VEC = PK_WX + N_RNN_BLOCKS * RNN_BLOCK
PK_SINK = PK_VEC + 6 * 8
PK_CONV = PK_SINK + 8
PK_SCALAR = PK_CONV + 32
PK_ROWS = PK_SCALAR + 8
PK_HALF = PK_ROWS // 2


def _params(**kw):
    return pltpu.CompilerParams(vmem_limit_bytes=VMEM_LIMIT, **kw)


def _sigmoid(z):
    return 1.0 / (1.0 + jnp.exp(-z))


def _dot(a, b):
    return jnp.dot(a, b, preferred_element_type=F32)


def _dot_nt(a, b):
    return lax.dot_general(a, b, (((1,), (1,)), ((), ())), preferred_element_type=F32)


def _dot_tn(a, b):
    return lax.dot_general(a, b, (((0,), (0,)), ((), ())), preferred_element_type=F32)


def _put_slot(src, n_slots, slot_and_blk, rows, dtype, name):
    _, c = src.shape
    tr = _row_tile(rows, c * 4)
    steps = rows // tr

    def body(idx_ref, s_ref, o_ref):
        o_ref[...] = s_ref[...].astype(dtype)

    grid_spec = pltpu.PrefetchScalarGridSpec(
        num_scalar_prefetch=1, grid=(steps,),
        in_specs=[pl.BlockSpec((tr, c), lambda i, idx_ref: (idx_ref[1] * steps + i, 0))],
        out_specs=pl.BlockSpec((None, tr, c), lambda i, idx_ref: (idx_ref[0], i, 0)))
    return pl.pallas_call(body, name=name, grid_spec=grid_spec,
                          out_shape=jax.ShapeDtypeStruct((n_slots, rows, c), dtype),
                          compiler_params=_params())(slot_and_blk, src)


def _rmsnorm_fwd(x, g):
    t, d = x.shape
    tm = min(t, 512)

    def body(x_ref, g_ref, o_ref):
        xv = x_ref[...]
        r = lax.rsqrt(jnp.mean(xv * xv, axis=-1, keepdims=True) + NORM_EPS)
        o_ref[...] = (xv * r * g_ref[...]).astype(BF16)

    return pl.pallas_call(
        body, name="rmsnorm_fwd", grid=(t // tm,), out_shape=jax.ShapeDtypeStruct((t, d), BF16),
        in_specs=[pl.BlockSpec((tm, d), lambda i: (i, 0)), pl.BlockSpec((1, d), lambda i: (0, 0))],
        out_specs=pl.BlockSpec((tm, d), lambda i: (i, 0)), compiler_params=_params())(x, g)


def _rmsnorm_bwd(x, dh, dx2, g):
    t, d = x.shape
    tm = min(t, 512)

    def body(x_ref, dh_ref, dx2_ref, g_ref, gx_ref, dg_ref):
        i = pl.program_id(0)
        xv = x_ref[...]
        dhv = dh_ref[...]
        r = lax.rsqrt(jnp.mean(xv * xv, axis=-1, keepdims=True) + NORM_EPS)
        nrm = xv * r
        dn = dhv * g_ref[...]
        gx_ref[...] = dx2_ref[...] + r * (dn - nrm * jnp.mean(dn * nrm, axis=-1, keepdims=True))

        @pl.when(i == 0)
        def _():
            dg_ref[...] = jnp.zeros_like(dg_ref)

        dg_ref[...] += jnp.sum(dhv * nrm, axis=0, keepdims=True)

    return pl.pallas_call(
        body, name="rmsnorm_bwd", grid=(t // tm,),
        out_shape=(jax.ShapeDtypeStruct((t, d), F32), jax.ShapeDtypeStruct((1, d), F32)),
        in_specs=[pl.BlockSpec((tm, d), lambda i: (i, 0)), pl.BlockSpec((tm, d), lambda i: (i, 0)),
                  pl.BlockSpec((tm, d), lambda i: (i, 0)), pl.BlockSpec((1, d), lambda i: (0, 0))],
        out_specs=(pl.BlockSpec((tm, d), lambda i: (i, 0)), pl.BlockSpec((1, d), lambda i: (0, 0))),
        compiler_params=_params())(x, dh, dx2, g)


def _adamw(w, g, m, v, name):
    r, c = w.shape
    tr = _row_tile(r, c * 4, 1024 * 1024)
    c1 = 1.0 - ADAM_B1 ** ADAM_STEP
    c2 = 1.0 - ADAM_B2 ** ADAM_STEP

    def body(w_ref, g_ref, m_ref, v_ref, d_ref, nm_ref, nv_ref):
        gv = g_ref[...]
        nm = ADAM_B1 * m_ref[...] + (1.0 - ADAM_B1) * gv
        nv = ADAM_B2 * v_ref[...] + (1.0 - ADAM_B2) * (gv * gv)
        m_hat = nm / c1
        v_hat = nv / c2
        d_ref[...] = -ADAM_LR * (m_hat / (jnp.sqrt(v_hat) + ADAM_EPS) + ADAM_WD * w_ref[...])
        nm_ref[...] = nm
        nv_ref[...] = nv

    spec = pl.BlockSpec((tr, c), lambda i: (i, 0))
    sds = jax.ShapeDtypeStruct((r, c), F32)
    return pl.pallas_call(
        body, name=name, grid=(r // tr,), out_shape=(sds, sds, sds),
        in_specs=[spec, spec, spec, spec], out_specs=(spec, spec, spec), compiler_params=_params())(w, g, m, v)


def _grad_h(dproj, w_bm):
    t = dproj.shape[0]
    nb, d, wb = w_bm.shape
    tm = min(t, 1024)

    def body(dp_ref, w_ref, o_ref, acc_ref):
        k = pl.program_id(1)

        @pl.when(k == 0)
        def _():
            acc_ref[...] = jnp.zeros_like(acc_ref)

        acc_ref[...] += _dot_nt(dp_ref[...], w_ref[...])

        @pl.when(k == nb - 1)
        def _():
            o_ref[...] = acc_ref[...]

    return pl.pallas_call(
        body, name="grad_h", grid=(t // tm, nb), out_shape=jax.ShapeDtypeStruct((t, d), F32),
        in_specs=[pl.BlockSpec((tm, wb), lambda i, k: (i, k)), pl.BlockSpec((None, d, wb), lambda i, k: (k, 0, 0))],
        out_specs=pl.BlockSpec((tm, d), lambda i, k: (i, 0)),
        scratch_shapes=[pltpu.VMEM((tm, d), F32)], compiler_params=_params())(dproj, w_bm)


def _shift_down(v, d, fill):
    n = v.shape[0]
    if d % 8 == 0:
        return jnp.concatenate([jnp.full((d,) + v.shape[1:], fill, v.dtype), v[: n - d]], axis=0)
    row = lax.broadcasted_iota(jnp.int32, v.shape, 0)
    return jnp.where(row >= d, pltpu.roll(v, d, axis=0), fill)


def _shift_up(v, d, fill):
    n = v.shape[0]
    if d % 8 == 0:
        return jnp.concatenate([v[d:], jnp.full((d,) + v.shape[1:], fill, v.dtype)], axis=0)
    row = lax.broadcasted_iota(jnp.int32, v.shape, 0)
    return jnp.where(row < n - d, pltpu.roll(v, n - d, axis=0), fill)


def _scan_log(a, b, shift):
    n = a.shape[0]
    d = 1
    while d < n:
        b = a * shift(b, d, 0.0) + b
        if 2 * d < n:
            a = a * shift(a, d, 1.0)
        d *= 2
    return b


SUBLANES = 8


def _scan(a, b, sa_ref, sb_ref, reverse):
    n, c = a.shape
    g = n // SUBLANES
    a3, b3 = a.reshape(g, SUBLANES, c), b.reshape(g, SUBLANES, c)
    sub = lax.broadcasted_iota(jnp.int32, a3.shape, 1)
    d = 1
    while d < SUBLANES:
        keep = (sub < SUBLANES - d) if reverse else (sub >= d)
        amount = SUBLANES - d if reverse else d
        b3 = a3 * jnp.where(keep, pltpu.roll(b3, amount, axis=1), 0.0) + b3
        a3 = a3 * jnp.where(keep, pltpu.roll(a3, amount, axis=1), 1.0)
        d *= 2
    sa_ref[...] = a3.reshape(n, c)
    sb_ref[...] = b3.reshape(n, c)
    edge = 0 if reverse else SUBLANES - 1
    shift = _shift_up if reverse else _shift_down
    totals = _scan_log(sa_ref[pl.ds(edge, g, stride=SUBLANES), :], sb_ref[pl.ds(edge, g, stride=SUBLANES), :], shift)
    carry = shift(totals, 1, 0.0)
    return (a3 * carry[:, None, :] + b3).reshape(n, c)


def _neg_expm1_twice(log_a, a):
    return -jnp.tanh(log_a) * (a * a + 1.0)


def _softplus(z):
    e = jnp.exp(-jnp.abs(z))
    w = 1.0 + e
    log1p = jnp.where(w == 1.0, e, jnp.log(w) * (e / jnp.where(w == 1.0, 1.0, w - 1.0)))
    return jnp.maximum(z, 0.0) + log1p


def _conv(up, cw, cb):
    out = cb + cw[CONV_WIDTH - 1:CONV_WIDTH, :] * up
    for j in range(CONV_WIDTH - 1):
        out = out + cw[j:j + 1, :] * _shift_down(up, CONV_WIDTH - 1 - j, 0.0)
    return out


def _lru_gates(u, wa_ref, ba_ref, wx_ref, bx_ref, lam_ref):
    ub = u.astype(BF16)
    r = _sigmoid(_dot(ub, wa_ref[...].astype(BF16)) + ba_ref[...])
    i = _sigmoid(_dot(ub, wx_ref[...].astype(BF16)) + bx_ref[...])
    sp = _softplus(-lam_ref[...])
    log_a = (-LRU_C) * r * sp
    a = jnp.exp(log_a)
    mult = jnp.sqrt(_neg_expm1_twice(log_a, a))
    return r, i, sp, a, mult


def _lru_specs(s):
    cb = RNN_BLOCK
    vec = pl.BlockSpec((1, cb), lambda n, b: (0, n))
    return dict(
        up=pl.BlockSpec((None, s, cb), lambda n, b: (b, 0, OFF_RNN_X // cb + n)),
        gr=pl.BlockSpec((None, s, cb), lambda n, b: (b, 0, OFF_RNN_G // cb + n)),
        act=pl.BlockSpec((None, s, cb), lambda n, b: (b, 0, n)),
        cw=pl.BlockSpec((CONV_WIDTH, cb), lambda n, b: (0, n)),
        vec=vec,
        wblk=pl.BlockSpec((None, cb, cb), lambda n, b: (n, 0, 0)),
    )


def _lru_fwd(proj3, cw, cb, wa, ba, wx, bx, lam, riders):
    bsz, s, _ = proj3.shape
    sp = _lru_specs(s)
    nr = len(riders)

    def body(up_ref, gr_ref, cw_ref, cb_ref, wa_ref, ba_ref, wx_ref, bx_ref, lam_ref, *refs):
        rider_in, (h_ref, y_ref), rider_out = refs[:nr], refs[nr:nr + 2], refs[nr + 2:2 * nr + 2]
        sa_ref, sb_ref = refs[2 * nr + 2:2 * nr + 4]
        start, finish = _row_gather(rider_in, rider_out, *refs[2 * nr + 4:])
        first = (pl.program_id(0) == 0) & (pl.program_id(1) == 0)
        last = (pl.program_id(0) == N_RNN_BLOCKS - 1) & (pl.program_id(1) == bsz - 1)
        pl.when(first)(start)
        u = _conv(up_ref[...], cw_ref[...], cb_ref[...])
        _, i, _, a, mult = _lru_gates(u, wa_ref, ba_ref, wx_ref, bx_ref, lam_ref)
        h = _scan(a, mult * (i * u), sa_ref, sb_ref, reverse=False)
        h_ref[...] = h
        g = gr_ref[...]
        y_ref[...] = (h * (g * _sigmoid(g))).astype(BF16)
        pl.when(last)(finish)

    res = pl.pallas_call(
        body, name="lru_fwd", grid=(N_RNN_BLOCKS, bsz),
        out_shape=[jax.ShapeDtypeStruct((bsz, s, D_RNN), F32), jax.ShapeDtypeStruct((bsz, s, D_RNN), BF16)] + [
            jax.ShapeDtypeStruct(r.shape, r.dtype) for r in riders],
        in_specs=[sp["up"], sp["gr"], sp["cw"], sp["vec"], sp["wblk"], sp["vec"], sp["wblk"], sp["vec"], sp["vec"]] + [
            ANY] * nr,
        out_specs=[sp["act"], sp["act"]] + [ANY] * nr, input_output_aliases={9 + t: 2 + t for t in range(nr)},
        scratch_shapes=[pltpu.VMEM((s, RNN_BLOCK), F32)] * 2 + [pltpu.SemaphoreType.DMA((3 * nr,))] * 4,
        compiler_params=_params())(proj3, proj3, cw, cb, wa, ba, wx, bx, lam, *riders)
    return res[0], res[1], res[2:]


def _lru_bwd(proj3, h3, dy3, dproj3, cw, cb, wa, ba, wx, bx, lam):
    bsz, s, _ = proj3.shape
    sp = _lru_specs(s)
    n_steps = N_RNN_BLOCKS * bsz

    def body(up_ref, gr_ref, h_ref, dy_ref, cw_ref, cb_ref, wa_ref, ba_ref, wx_ref, bx_ref, lam_ref, dp_in,
             dp_ref, dcw_ref, dcb_ref, dwa_ref, dba_ref, dwx_ref, dbx_ref, dlam_ref, sa_ref, sb_ref,
             dup_st, dgr_st, o_sems):
        del dp_in
        blk = pl.program_id(0)
        b = pl.program_id(1)
        step = blk * bsz + b
        slot = step % 2
        stages = [dup_st, dgr_st]
        dsts = [dp_ref.at[b, :, pl.ds(pl.multiple_of(OFF_RNN_X + blk * RNN_BLOCK, LANES), RNN_BLOCK)],
                dp_ref.at[b, :, pl.ds(pl.multiple_of(OFF_RNN_G + blk * RNN_BLOCK, LANES), RNN_BLOCK)]]
        _staged_reuse(step, stages, dsts, o_sems, slot)
        up = up_ref[...]
        cwv = cw_ref[...]
        u = _conv(up, cwv, cb_ref[...])
        r, i, spv, a, mult = _lru_gates(u, wa_ref, ba_ref, wx_ref, bx_ref, lam_ref)
        h = h_ref[...]
        g = gr_ref[...]
        dy = dy_ref[...]
        sg = _sigmoid(g)
        dgr_st[slot] = (dy * h * (sg * (1.0 + g * (1.0 - sg)))).astype(BF16)
        dh = dy * (g * sg)
        adj = _scan(_shift_up(a, 1, 0.0), dh, sa_ref, sb_ref, reverse=True)
        da = adj * _shift_down(h, 1, 0.0)
        dmult = adj * (i * u)
        di = adj * mult * u
        du = adj * mult * i
        dla = da * a - dmult * (a * a) / mult
        dr = dla * ((-LRU_C) * spv)
        dsp = jnp.sum(dla * ((-LRU_C) * r), axis=0, keepdims=True)
        dza = dr * r * (1.0 - r)
        dzx = di * i * (1.0 - i)
        ub = u.astype(BF16)
        dzab = dza.astype(BF16)
        dzxb = dzx.astype(BF16)
        du = du + _dot_nt(dzab, wa_ref[...].astype(BF16)) + _dot_nt(dzxb, wx_ref[...].astype(BF16))
        dup = cwv[CONV_WIDTH - 1:CONV_WIDTH, :] * du
        for j in range(CONV_WIDTH - 1):
            dup = dup + cwv[j:j + 1, :] * _shift_up(du, CONV_WIDTH - 1 - j, 0.0)
        dup_st[slot] = dup.astype(BF16)
        _staged_flush(step, n_steps, stages, dsts, o_sems, slot)

        @pl.when(b == 0)
        def _():
            for ref in (dcw_ref, dcb_ref, dwa_ref, dba_ref, dwx_ref, dbx_ref, dlam_ref):
                ref[...] = jnp.zeros_like(ref)

        rows = [jnp.sum(du * _shift_down(up, CONV_WIDTH - 1 - j, 0.0), axis=0, keepdims=True)
                for j in range(CONV_WIDTH - 1)]
        rows.append(jnp.sum(du * up, axis=0, keepdims=True))
        dcw_ref[...] += jnp.concatenate(rows, axis=0)
        dcb_ref[...] += jnp.sum(du, axis=0, keepdims=True)
        dwa_ref[...] += _dot_tn(ub, dzab)
        dba_ref[...] += jnp.sum(dza, axis=0, keepdims=True)
        dwx_ref[...] += _dot_tn(ub, dzxb)
        dbx_ref[...] += jnp.sum(dzx, axis=0, keepdims=True)
        dlam_ref[...] += dsp * (-_sigmoid(-lam_ref[...]))

    vec = jax.ShapeDtypeStruct((1, D_RNN), F32)
    wsd = jax.ShapeDtypeStruct((N_RNN_BLOCKS, RNN_BLOCK, RNN_BLOCK), F32)
    return pl.pallas_call(
        body, name="lru_bwd", grid=(N_RNN_BLOCKS, bsz),
        out_shape=(jax.ShapeDtypeStruct(dproj3.shape, dproj3.dtype), jax.ShapeDtypeStruct((CONV_WIDTH, D_RNN), F32),
                   vec, wsd, vec, wsd, vec, vec),
        in_specs=[sp["up"], sp["gr"], sp["act"], sp["act"], sp["cw"], sp["vec"], sp["wblk"], sp["vec"],
                  sp["wblk"], sp["vec"], sp["vec"], ANY],
        out_specs=(ANY, sp["cw"], sp["vec"], sp["wblk"], sp["vec"], sp["wblk"], sp["vec"], sp["vec"]),
        input_output_aliases={11: 0},
        scratch_shapes=[pltpu.VMEM((s, RNN_BLOCK), F32)] * 2 + [pltpu.VMEM((2, s, RNN_BLOCK), BF16)] * 2 + [
            pltpu.SemaphoreType.DMA((4,))],
        compiler_params=_params())(proj3, proj3, h3, dy3, cw, cb, wa, ba, wx, bx, lam, dproj3)


def _rope_tables(s):
    half = ROPE_DIM // 2
    pos = jnp.arange(s, dtype=F32)
    inv_freq = ROPE_THETA ** (-jnp.arange(0, ROPE_DIM, 2, dtype=F32) / ROPE_DIM)
    ang = pos[:, None] * inv_freq[None, :]
    cos, sin = jnp.cos(ang), jnp.sin(ang)
    rest = HEAD_DIM - ROPE_DIM
    cos64 = jnp.concatenate([cos, cos, jnp.ones((s, rest), F32)], axis=1)
    sin64 = jnp.concatenate([-sin, sin, jnp.zeros((s, rest), F32)], axis=1)
    assert half * 2 == ROPE_DIM
    return jnp.tile(cos64, (1, LANES // HEAD_DIM)), jnp.tile(sin64, (1, LANES // HEAD_DIM))


def _swap_rot_halves(v):
    half = ROPE_DIM // 2
    lane = lax.broadcasted_iota(jnp.int32, v.shape, 1) % HEAD_DIM
    second = jnp.where(lane < ROPE_DIM, pltpu.roll(v, half, axis=1), 0.0)
    return jnp.where(lane < half, pltpu.roll(v, LANES - half, axis=1), second)


def _rope(v, cos, sin):
    tiles = []
    for t in range(v.shape[1] // LANES):
        vt = v[:, t * LANES:(t + 1) * LANES]
        tiles.append(vt * cos + _swap_rot_halves(vt) * sin)
    return tiles[0] if len(tiles) == 1 else jnp.concatenate(tiles, axis=1)


def _unrope(v, cos, sin):
    tiles = []
    for t in range(v.shape[1] // LANES):
        vt = v[:, t * LANES:(t + 1) * LANES]
        tiles.append(vt * cos + _swap_rot_halves(vt * sin))
    return tiles[0] if len(tiles) == 1 else jnp.concatenate(tiles, axis=1)


HEADS_PER_STEP = 8
QW = HEADS_PER_STEP * HEAD_DIM
N_PAIRS = N_Q_HEADS // HEADS_PER_STEP
Q_PER_KV = N_Q_HEADS // N_KV_HEADS
KV_PER_STEP = HEADS_PER_STEP // Q_PER_KV


QT_COLS = Q_PER_KV * ATT_BLK


def _attn_saved_shapes(bsz, s):
    nb = s // ATT_BLK
    pad = s + ATT_BLK
    return [(bsz, N_PAIRS, nb, LANES, QT_COLS), (bsz, N_PAIRS, KV_PER_STEP, pad, LANES),
            (bsz, N_PAIRS, KV_PER_STEP, pad, LANES), (bsz, N_PAIRS, LANES, pad)]


def _attn_specs(s, order):
    def mk(width, base, **kw):
        if order == "bp":
            return pl.BlockSpec((None, s, width), lambda b, p: (b, 0, base + p), **kw)
        return pl.BlockSpec((None, s, width), lambda p, b: (b, 0, base + p), **kw)

    def saved(shape, **kw):
        blk = (None, None) + shape[2:]
        zeros = (0,) * (len(shape) - 2)
        if order == "bp":
            return pl.BlockSpec(blk, lambda b, p: (b, p) + zeros, **kw)
        return pl.BlockSpec(blk, lambda p, b: (b, p) + zeros, **kw)

    one = dict(pipeline_mode=pl.Buffered(1))
    tbl = pl.BlockSpec((s, LANES), lambda *_: (0, 0))
    shapes = _attn_saved_shapes(1, s)
    return dict(q=mk(QW, OFF_Q // QW), k=mk(LANES, OFF_K // LANES), v=mk(LANES, OFF_V // LANES),
                g=mk(QW, OFF_ATTN_G // QW), act=mk(QW, 0), kv=mk(LANES, 0), tbl=tbl,
                g1=mk(QW, OFF_ATTN_G // QW, **one), act1=mk(QW, 0, **one),
                saved=[saved(sh) for sh in shapes], saved1=[saved(sh, **one) for sh in shapes],
                smem=pl.BlockSpec(memory_space=pltpu.SMEM))


def _to_qt(blk):
    rows = []
    for j in range(KV_PER_STEP):
        cols = []
        for tt in range(2):
            t = 2 * j + tt
            tr = blk[:, t * LANES:(t + 1) * LANES].T
            cols += [tr[0:HEAD_DIM, :], tr[HEAD_DIM:, :]]
        rows.append(jnp.concatenate(cols, axis=1))
    return jnp.concatenate(rows, axis=0)


def _from_qt(xt):
    tiles = []
    for j in range(KV_PER_STEP):
        for tt in range(2):
            g0 = 2 * tt
            pair = jnp.concatenate([xt[j * HEAD_DIM:(j + 1) * HEAD_DIM, (g0 + i) * ATT_BLK:(g0 + i + 1) * ATT_BLK]
                                    for i in range(2)], axis=0)
            tiles.append(pair.T)
    return jnp.concatenate(tiles, axis=1)


def _attn_prep(q_ref, k_ref, v_ref, cos_ref, sin_ref, qt_ref, km_ref, vm_ref, kt_ref, vt_ref, nb):
    zeros = jnp.zeros((ATT_BLK, LANES), BF16)
    for j in range(KV_PER_STEP):
        km_ref[j, 0:ATT_BLK, :] = zeros
        vm_ref[j, 0:ATT_BLK, :] = zeros
    kt_ref[:, 0:ATT_BLK] = zeros
    vt_ref[:, 0:ATT_BLK] = zeros
    head_of_lane = lax.broadcasted_iota(jnp.int32, (ATT_BLK, LANES), 1) // HEAD_DIM

    def prep(n, carry):
        r0 = pl.multiple_of(n * ATT_BLK, ATT_BLK)
        cs = cos_ref[pl.ds(r0, ATT_BLK), :]
        sn = sin_ref[pl.ds(r0, ATT_BLK), :]
        qt_ref[n] = _to_qt(_rope(q_ref[pl.ds(r0, ATT_BLK), :], cs, sn) * ATTN_SCALE).astype(BF16)
        k = _rope(k_ref[pl.ds(r0, ATT_BLK), :], cs, sn)
        v = v_ref[pl.ds(r0, ATT_BLK), :]
        for j in range(KV_PER_STEP):
            km_ref[j, pl.ds(r0 + ATT_BLK, ATT_BLK), :] = jnp.where(head_of_lane == j, k, 0.0).astype(BF16)
            vm_ref[j, pl.ds(r0 + ATT_BLK, ATT_BLK), :] = jnp.where(head_of_lane == j, v, 0.0).astype(BF16)
        kt_ref[:, pl.ds(r0 + ATT_BLK, ATT_BLK)] = k.T.astype(BF16)
        vt_ref[:, pl.ds(r0 + ATT_BLK, ATT_BLK)] = v.T.astype(BF16)
        return carry

    lax.fori_loop(0, nb, prep, 0)


def _from_prev_block():
    key = lax.broadcasted_iota(jnp.int32, (ATT_BLK, QT_COLS), 0)
    qry = lax.broadcasted_iota(jnp.int32, (ATT_BLK, QT_COLS), 1) % ATT_BLK
    return key > qry


def _fold(tile, prev, prev_bias=None):
    top = tile[:ATT_BLK] if prev_bias is None else tile[:ATT_BLK] + prev_bias
    return jnp.where(prev, top, tile[ATT_BLK:])


def _unfold(folded, prev):
    zero = jnp.zeros_like(folded)
    return jnp.concatenate([jnp.where(prev, folded, zero), jnp.where(prev, zero, folded)], axis=0).astype(BF16)


def _no_prev_bias(n):
    return jnp.where(n == 0, NEG_BIG, 0.0).astype(F32)


def _sink_row(sink_ref, first):
    return jnp.concatenate([jnp.full((1, ATT_BLK), sink_ref[first + g], F32) for g in range(Q_PER_KV)], axis=1)


def _softmax_cols(sc, sink):
    m = jnp.maximum(jnp.max(sc, axis=0, keepdims=True), sink)
    e = jnp.exp(sc - m)
    es = jnp.exp(sink - m)
    inv = 1.0 / (jnp.sum(e, axis=0, keepdims=True) + es)
    return e * inv, es * inv


def _attn_fwd(proj3, sinks, cosf, sinf):
    bsz, s, _ = proj3.shape
    nb = s // ATT_BLK
    sp = _attn_specs(s, "bp")

    def body(sink_ref, q_ref, k_ref, v_ref, g_ref, cos_ref, sin_ref, o_ref, y_ref, qt_sc, km_sc, vm_sc, kt_ref, vt_sc):
        p = pl.program_id(1)
        _attn_prep(q_ref, k_ref, v_ref, cos_ref, sin_ref, qt_sc, km_sc, vm_sc, kt_ref, vt_sc, nb)
        kv_row = lax.broadcasted_iota(jnp.int32, (LANES, QT_COLS), 0) // HEAD_DIM
        prev = _from_prev_block()

        def blk(n, carry):
            r0 = pl.multiple_of(n * ATT_BLK, ATT_BLK)
            bias = _no_prev_bias(n)
            rq = qt_sc[n]
            vt = vt_sc[:, pl.ds(r0, 2 * ATT_BLK)]
            ots = []
            for j in range(KV_PER_STEP):
                st = _dot(km_sc[j, pl.ds(r0, 2 * ATT_BLK), :], rq)
                pc, _ = _softmax_cols(_fold(st, prev, bias), _sink_row(sink_ref, p * HEADS_PER_STEP + j * Q_PER_KV))
                ots.append(_dot(vt, _unfold(pc, prev)))
            o = _from_qt(jnp.where(kv_row == 0, ots[0], ots[1]))
            o_ref[pl.ds(r0, ATT_BLK), :] = o
            g = g_ref[pl.ds(r0, ATT_BLK), :]
            y_ref[pl.ds(r0, ATT_BLK), :] = (o * (g * _sigmoid(g))).astype(BF16)
            return carry

        lax.fori_loop(0, nb, blk, 0, unroll=2)

    res = pl.pallas_call(
        body, name="attn_fwd", grid=(bsz, N_PAIRS),
        out_shape=[jax.ShapeDtypeStruct((bsz, s, D_ATTN), F32), jax.ShapeDtypeStruct((bsz, s, D_ATTN), BF16)] + [
            jax.ShapeDtypeStruct(sh, BF16) for sh in _attn_saved_shapes(bsz, s)],
        in_specs=[sp["smem"], sp["q"], sp["k"], sp["v"], sp["g"], sp["tbl"], sp["tbl"]],
        out_specs=[sp["act"], sp["act"]] + sp["saved"],
        scratch_shapes=[pltpu.VMEM((LANES, s + ATT_BLK), BF16)],
        compiler_params=_params())(sinks, proj3, proj3, proj3, proj3, cosf, sinf)
    return res[0], res[1], res[2:]


def _attn_bwd(proj3, saved, o3, dy3, dproj3, sinks, cosf, sinf):
    bsz, s, _ = proj3.shape
    nb = s // ATT_BLK
    assert nb % 2 == 0
    sp = _attn_specs(s, "pb")
    n_steps = N_PAIRS * bsz

    def body(sink_ref, qt_sc, km_sc, vm_sc, kt_sc, g_ref, o_ref, dy_ref, cos_ref, sin_ref, dp_in,
             dp_ref, ds_ref, dot_sc, dqt_sc, dk_sc, dv_sc, dq_st, dk_st, dv_st, dg_st, o_sems):
        del dp_in
        p = pl.program_id(0)
        b = pl.program_id(1)
        step = p * bsz + b
        slot = step % 2
        stages = [dq_st, dk_st, dv_st, dg_st]
        dsts = [dp_ref.at[b, :, pl.ds(pl.multiple_of(OFF_Q + p * QW, LANES), QW)],
                dp_ref.at[b, :, pl.ds(pl.multiple_of(OFF_K + p * LANES, LANES), LANES)],
                dp_ref.at[b, :, pl.ds(pl.multiple_of(OFF_V + p * LANES, LANES), LANES)],
                dp_ref.at[b, :, pl.ds(pl.multiple_of(OFF_ATTN_G + p * QW, LANES), QW)]]
        _staged_reuse(step, stages, dsts, o_sems, slot)
        dk_sc[...] = jnp.zeros_like(dk_sc)
        dv_sc[...] = jnp.zeros_like(dv_sc)

        def gate(n, carry):
            r0 = pl.multiple_of(n * ATT_BLK, ATT_BLK)
            g = g_ref[pl.ds(r0, ATT_BLK), :]
            dy = dy_ref[pl.ds(r0, ATT_BLK), :]
            sg = _sigmoid(g)
            dg_st[slot, pl.ds(r0, ATT_BLK), :] = (dy * o_ref[pl.ds(r0, ATT_BLK), :] * (sg * (1.0 + g * (1.0 - sg)))).astype(BF16)
            dot_sc[n] = _to_qt(dy * (g * sg)).astype(BF16)
            return carry

        lax.fori_loop(0, nb, gate, 0)
        kv_lane = lax.broadcasted_iota(jnp.int32, (2 * ATT_BLK, LANES), 1) // HEAD_DIM
        kv_row = lax.broadcasted_iota(jnp.int32, (LANES, QT_COLS), 0) // HEAD_DIM
        prev = _from_prev_block()

        def blk(n, acc):
            r0 = pl.multiple_of(n * ATT_BLK, ATT_BLK)
            bias = _no_prev_bias(n)
            rq = qt_sc[n]
            rd = dot_sc[n]
            kt = kt_sc[:, pl.ds(r0, 2 * ATT_BLK)]
            dvs, dks, dqs, new_acc = [], [], [], []
            for j in range(KV_PER_STEP):
                st = _dot(km_sc[j, pl.ds(r0, 2 * ATT_BLK), :], rq)
                pc, ps = _softmax_cols(_fold(st, prev, bias), _sink_row(sink_ref, p * HEADS_PER_STEP + j * Q_PER_KV))
                dpc = _fold(_dot(vm_sc[j, pl.ds(r0, 2 * ATT_BLK), :], rd), prev)
                delta = jnp.sum(pc * dpc, axis=0, keepdims=True)
                dst = _unfold(pc * (dpc - delta), prev)
                new_acc.append(acc[j] + ps * delta)
                dvs.append(_dot_nt(_unfold(pc, prev), rd))
                dks.append(_dot_nt(dst, rq))
                dqs.append(_dot(kt, dst))
            dv_sc[pl.ds(r0, 2 * ATT_BLK), :] += jnp.where(kv_lane == 0, dvs[0], dvs[1])
            dk_sc[pl.ds(r0, 2 * ATT_BLK), :] += jnp.where(kv_lane == 0, dks[0], dks[1])
            dqt_sc[n] = jnp.where(kv_row == 0, dqs[0], dqs[1]) * ATTN_SCALE
            return tuple(new_acc)

        def blk_pair(m, acc):
            return blk(2 * m + 1, blk(2 * m, acc))

        acc = lax.fori_loop(0, nb // 2, blk_pair, tuple(jnp.zeros((1, QT_COLS), F32) for _ in range(KV_PER_STEP)))
        lane1 = lax.broadcasted_iota(jnp.int32, (1, LANES), 1)
        dsink = jnp.zeros((1, LANES), F32)
        for j in range(KV_PER_STEP):
            for i in range(Q_PER_KV):
                part = jnp.sum(acc[j][:, i * ATT_BLK:(i + 1) * ATT_BLK], axis=1, keepdims=True)
                dsink = dsink - jnp.where(lane1 == j * Q_PER_KV + i, part, 0.0)

        @pl.when(b == 0)
        def _():
            ds_ref[...] = jnp.zeros_like(ds_ref)

        ds_ref[...] += dsink

        def post(n, carry):
            r0 = pl.multiple_of(n * ATT_BLK, ATT_BLK)
            cs = cos_ref[pl.ds(r0, ATT_BLK), :]
            sn = sin_ref[pl.ds(r0, ATT_BLK), :]
            dq_st[slot, pl.ds(r0, ATT_BLK), :] = _unrope(_from_qt(dqt_sc[n]), cs, sn).astype(BF16)
            dk_st[slot, pl.ds(r0, ATT_BLK), :] = _unrope(dk_sc[pl.ds(r0 + ATT_BLK, ATT_BLK), :], cs, sn).astype(BF16)
            dv_st[slot, pl.ds(r0, ATT_BLK), :] = dv_sc[pl.ds(r0 + ATT_BLK, ATT_BLK), :].astype(BF16)
            return carry

        lax.fori_loop(0, nb, post, 0)
        _staged_flush(step, n_steps, stages, dsts, o_sems, slot)

    n_in = 1 + len(saved) + 5
    return pl.pallas_call(
        body, name="attn_bwd", grid=(N_PAIRS, bsz),
        out_shape=(jax.ShapeDtypeStruct(dproj3.shape, dproj3.dtype), jax.ShapeDtypeStruct((N_PAIRS, 1, LANES), F32)),
        in_specs=[sp["smem"]] + sp["saved1"] + [sp["g1"], sp["act1"], sp["act1"], sp["tbl"], sp["tbl"], ANY],
        out_specs=(ANY, pl.BlockSpec((None, 1, LANES), lambda p, b: (p, 0, 0))),
        input_output_aliases={n_in: 0},
        scratch_shapes=[pltpu.VMEM((nb, LANES, QT_COLS), BF16),
                        pltpu.VMEM((nb, LANES, QT_COLS), F32),
                        pltpu.VMEM((s + ATT_BLK, LANES), F32),
                        pltpu.VMEM((s + ATT_BLK, LANES), F32),
                        pltpu.VMEM((2, s, QW), BF16), pltpu.VMEM((2, s, LANES), BF16),
                        pltpu.VMEM((2, s, LANES), BF16), pltpu.VMEM((2, s, QW), BF16),
                        pltpu.SemaphoreType.DMA((8,))],
        compiler_params=_params())(sinks, *saved, proj3, o3, dy3, cosf, sinf, dproj3)


def _staged_copies(stages, dsts, sems, slot):
    return [pltpu.make_async_copy(st.at[slot], dst, sems.at[slot * len(stages) + t])
            for t, (st, dst) in enumerate(zip(stages, dsts))]


def _staged_reuse(step, stages, dsts, sems, slot):
    @pl.when(step >= 2)
    def _():
        for cp in _staged_copies(stages, dsts, sems, slot):
            cp.wait()


def _staged_flush(step, n_steps, stages, dsts, sems, slot):
    for cp in _staged_copies(stages, dsts, sems, slot):
        cp.start()

    @pl.when(step == n_steps - 1)
    def _():
        for cp in _staged_copies(stages, dsts, sems, slot):
            cp.wait()
        if n_steps >= 2:
            for cp in _staged_copies(stages, dsts, sems, 1 - slot):
                cp.wait()


def _merge_fwd_bwd(x, tgt, y_rnn, y_attn, proj, w_r, w_a, w_o, gf):
    t, d = x.shape
    tm = min(t, 256)
    nt = t // tm

    hw = d // 2

    def body(x_ref, t_ref, yr_ref, ya_ref, mr0_ref, mr1_ref, ma0_ref, ma1_ref, wr_ref, wa_ref, wo_ref, gf_ref,
             dp_ref, dyr_ref, dya_ref, mg_ref, dx2_ref, dx2b_ref, dpr_ref, dpa_ref, loss_ref, dgf_ref, dmg_st, o_sems):
        i = pl.program_id(0)
        slot = i % 2
        dsts = [dp_ref.at[pl.ds(pl.multiple_of(i * tm, tm), tm), pl.ds(OFF_MERGE_R, 2 * d)]]
        _staged_reuse(i, [dmg_st], dsts, o_sems, slot)
        wr = wr_ref[...]
        wa = wa_ref[...]
        wo = wo_ref[...]
        gfv = gf_ref[...]
        pr = _dot(yr_ref[...], wr)
        pa = _dot(ya_ref[...], wa)
        sr = _sigmoid(jnp.concatenate([mr0_ref[...], mr1_ref[...]], axis=1))
        sa = _sigmoid(jnp.concatenate([ma0_ref[...], ma1_ref[...]], axis=1))
        mb = (sr * pr + sa * pa).astype(BF16)
        mg_ref[...] = mb
        x2 = x_ref[...] + _dot(mb, wo)
        r2 = lax.rsqrt(jnp.mean(x2 * x2, axis=-1, keepdims=True) + NORM_EPS)
        nrm = x2 * r2
        err = nrm * gfv - t_ref[...]
        dy = err * (1.0 / d)
        dn = dy * gfv
        dx2 = r2 * (dn - nrm * jnp.mean(dn * nrm, axis=-1, keepdims=True))
        dx2_ref[...] = dx2
        dx2b = dx2.astype(BF16)
        dx2b_ref[...] = dx2b
        dmerged = _dot_nt(dx2b, wo)
        dpr = (dmerged * sr).astype(BF16)
        dpa = (dmerged * sa).astype(BF16)
        dpr_ref[...] = dpr
        dpa_ref[...] = dpa
        dmg_st[slot, :, 0:d] = (dmerged * pr * (sr * (1.0 - sr))).astype(BF16)
        dmg_st[slot, :, d:2 * d] = (dmerged * pa * (sa * (1.0 - sa))).astype(BF16)
        _staged_flush(i, nt, [dmg_st], dsts, o_sems, slot)
        dyr_ref[...] = _dot_nt(dpr, wr)
        dya_ref[...] = _dot_nt(dpa, wa)

        @pl.when(i == 0)
        def _():
            loss_ref[...] = jnp.zeros_like(loss_ref)
            dgf_ref[...] = jnp.zeros_like(dgf_ref)

        loss_ref[...] += jnp.full((1, LANES), 0.5 / d, F32) * jnp.sum(err * err)
        dgf_ref[...] += jnp.sum(dy * nrm, axis=0, keepdims=True)

    tile = pl.BlockSpec((tm, d), lambda i: (i, 0))
    wsp = pl.BlockSpec((d, d), lambda i: (0, 0))

    def gate(col_blk):
        return pl.BlockSpec((tm, hw), lambda i: (i, col_blk))

    fb = jax.ShapeDtypeStruct((t, d), BF16)
    ff = jax.ShapeDtypeStruct((t, d), F32)
    return pl.pallas_call(
        body, name="merge_fwd_bwd", grid=(nt,),
        out_shape=(jax.ShapeDtypeStruct((t, D_IN), BF16), ff, ff, fb, ff, fb, fb, fb,
                   jax.ShapeDtypeStruct((1, LANES), F32), jax.ShapeDtypeStruct((1, d), F32)),
        in_specs=[tile, tile, tile, tile] + [gate(OFF_MERGE_R // hw + j) for j in range(4)] + [
            wsp, wsp, wsp, pl.BlockSpec((1, d), lambda i: (0, 0))],
        out_specs=(ANY, tile, tile, tile, tile, tile, tile, tile,
                   pl.BlockSpec((1, LANES), lambda i: (0, 0)), pl.BlockSpec((1, d), lambda i: (0, 0))),
        scratch_shapes=[pltpu.VMEM((2, tm, 2 * d), BF16), pltpu.SemaphoreType.DMA((2,))],
        compiler_params=_params())(x, tgt, y_rnn, y_attn, proj, proj, proj, proj, w_r, w_a, w_o, gf)


def _local_grads(x, tgt, h, proj, norm_g, w_in_bm, conv_w, conv_b, lru_w_a, lru_b_a, lru_w_x, lru_b_x, lam, sinks,
                 row_sharded, gf):
    bsz, s, d = x.shape
    t = bsz * s
    x2 = x.reshape(t, d)
    proj3 = proj.reshape(bsz, s, D_IN)
    h_lru, y_rnn, gathered = _lru_fwd(proj3, conv_w, conv_b, lru_w_a, lru_b_a, lru_w_x, lru_b_x, lam, row_sharded)
    w_r, w_a, w_o = (g.reshape(d, d) for g in gathered)
    cosf, sinf = _rope_tables(s)
    o_attn, y_attn, attn_saved = _attn_fwd(proj3, sinks, cosf, sinf)
    y_rnn2 = y_rnn.reshape(t, d)
    y_attn2 = y_attn.reshape(t, d)
    dproj, dyr, dya, merged, dx2, dx2b, dpr, dpa, loss, dgf = _merge_fwd_bwd(
        x2, tgt.reshape(t, d), y_rnn2, y_attn2, proj, w_r, w_a, w_o, gf)
    dproj3, dsink = _attn_bwd(proj3, attn_saved, o_attn, dya.reshape(bsz, s, d), dproj.reshape(bsz, s, D_IN),
                              sinks, cosf, sinf)
    dproj3, dcw, dcb, dwa, dba, dwx, dbx, dlam = _lru_bwd(
        proj3, h_lru, dyr.reshape(bsz, s, d), dproj3, conv_w, conv_b, lru_w_a, lru_b_a, lru_w_x, lru_b_x, lam)
    dproj = dproj3.reshape(t, D_IN)
    dh = _grad_h(dproj, w_in_bm)
    grad_x, dng = _rmsnorm_bwd(x2, dh, dx2, norm_g)
    small = dict(norm_g=dng, conv_w=dcw, conv_b=dcb, lru_w_a=dwa, lru_b_a=dba, lru_w_x=dwx, lru_b_x=dbx,
                 lru_lambda=dlam, attn_sinks=dsink[:, 0, :HEADS_PER_STEP].reshape(1, N_Q_HEADS), final_norm_g=dgf)
    squares = [(y_rnn2, dpr), (y_attn2, dpa), (merged, dx2b)]
    return loss[0, 0], grad_x.reshape(bsz, s, d), h, dproj, squares, small


ANY = pl.BlockSpec(memory_space=pl.ANY)


def _mesh_pos():
    return lax.axis_index("x"), lax.axis_index("y"), lax.axis_index("c")


def _remote(src, dst, send_sems, recv_sems, idx, peer):
    return pltpu.make_async_remote_copy(src_ref=src, dst_ref=dst, send_sem=send_sems.at[idx],
                                        recv_sem=recv_sems.at[idx], device_id=peer, device_id_type=MESH)


def _row_gather(ins, outs, send_sems, recv_sems, fsend_sems, frecv_sems):
    n = len(ins)
    x, y, c = _mesh_pos()
    me = 2 * x + y
    sib = (x, y, 1 - c)
    peers = [((x, 1 - y, c), me ^ 1), ((1 - x, y, c), me ^ 2), ((1 - x, 1 - y, c), me ^ 3)]

    def half(ref, slot, t, which):
        hr = ins[t].shape[1] // 2
        return ref.at[slot, pl.ds(pl.multiple_of(which * hr, 8), hr), :]

    def ici(t, k):
        peer, pj = peers[k]
        src = half(ins[t], me, t, c)
        return (_remote(src, half(outs[t], me, t, c), send_sems, recv_sems, 3 * t + k, peer),
                _remote(src, half(outs[t], pj, t, c), send_sems, recv_sems, 3 * t + k, peer))

    def forward(t, k):
        got = half(outs[t], peers[k][1], t, c)
        return (_remote(got, got, fsend_sems, frecv_sems, 3 * t + k, sib),
                _remote(got, half(outs[t], peers[k][1], t, 1 - c), fsend_sems, frecv_sems, 3 * t + k, sib))

    pairs = [(t, k) for t in range(n) for k in range(3)]

    def start():
        for t, k in pairs:
            ici(t, k)[0].start()

    def finish():
        for t, k in pairs:
            ici(t, k)[1].wait_recv()
            forward(t, k)[0].start()
        for t, k in pairs:
            ici(t, k)[0].wait_send()
            forward(t, k)[0].wait_send()
            forward(t, k)[1].wait_recv()

    return start, finish


def _gather_in_proj(h, bufs, split, idx):
    t_tok, d = h.shape
    n = len(bufs)
    tm = min(t_tok, 512)
    nt = t_tok // tm
    n_fwd = 3 * sum(split)
    assert split[0]

    def body(idx_ref, h_ref, *refs):
        ins, proj_ref, outs = refs[:n], refs[n], refs[n + 1:2 * n + 1]
        wbuf, send_sems, recv_sems, fsend_sems, frecv_sems, l_sems = refs[2 * n + 1:]
        j, i = pl.program_id(0), pl.program_id(1)
        x, y, c = _mesh_pos()
        me = 2 * x + y
        sib = (x, y, 1 - c)
        peers = [((x, 1 - y, c), me ^ 1), ((1 - x, y, c), me ^ 2), ((1 - x, 1 - y, c), me ^ 3)]

        def part(ref, slot, t, half):
            if not split[t]:
                return ref.at[slot]
            hr = bufs[t].shape[1] // 2
            return ref.at[slot, pl.ds(pl.multiple_of(half * hr, 8), hr), :]

        def land(t):
            return wbuf if t == 0 else outs[t]

        def ici(t, k):
            peer, pj = peers[k]
            src = part(ins[t], me, t, c)
            return (_remote(src, part(land(t), me, t, c), send_sems, recv_sems, 3 * t + k, peer),
                    _remote(src, part(land(t), pj, t, c), send_sems, recv_sems, 3 * t + k, peer))

        fwd_index = {}
        for t in range(n):
            if split[t]:
                for k in range(3):
                    fwd_index[(t, k)] = len(fwd_index)

        def forward(t, k):
            pj = peers[k][1]
            got = part(land(t), pj, t, c)
            f = fwd_index[(t, k)]
            return (_remote(got, got, fsend_sems, frecv_sems, f, sib),
                    _remote(got, part(land(t), pj, t, 1 - c), fsend_sems, frecv_sems, f, sib))

        def write_back(k):
            pj = peers[k][1]
            return pltpu.make_async_copy(wbuf.at[pj], outs[0].at[pj], l_sems.at[1 + k])

        relay_peer = ((x + c) % 2, (y + 1 - c) % 2, c)

        def relay():
            got = part(wbuf, me ^ (2 - c), 0, c)
            return (_remote(got, got, send_sems, recv_sems, 2, relay_peer),
                    _remote(got, part(wbuf, me ^ 3, 0, c), send_sems, recv_sems, 2, relay_peer))

        direct = [(t, k) for t in range(n) for k in range(3) if (t, k) != (0, 2)]

        @pl.when((j == 0) & (i == 0))
        def _():
            for t, k in direct:
                ici(t, k)[0].start()
            own = pltpu.make_async_copy(ins[0].at[me], wbuf.at[me], l_sems.at[0])
            own.start()
            own.wait()

        @pl.when((j == 1) & (i == 0))
        def _():
            for k in range(2):
                ici(0, k)[1].wait_recv()
            relay()[0].start()
            for k in range(2):
                forward(0, k)[0].start()
            forward(0, 0)[1].wait_recv()
            write_back(0).start()

        @pl.when((j == 2) & (i == 0))
        def _():
            forward(0, 1)[1].wait_recv()
            write_back(1).start()

        @pl.when((j == 3) & (i == 0))
        def _():
            relay()[1].wait_recv()
            forward(0, 2)[0].start()
            forward(0, 2)[1].wait_recv()
            write_back(2).start()

        proj_ref[...] = _dot(h_ref[...], wbuf[me ^ j])

        @pl.when((j == N_CHIPS - 1) & (i == nt - 1))
        def _():
            for t in range(1, n):
                for k in range(3):
                    ici(t, k)[1].wait_recv()
                    if split[t]:
                        forward(t, k)[0].start()
            relay()[0].wait_send()
            for t, k in direct:
                ici(t, k)[0].wait_send()
            for t in range(n):
                if split[t]:
                    for k in range(3):
                        forward(t, k)[0].wait_send()
                        if t > 0:
                            forward(t, k)[1].wait_recv()
            for k in range(3):
                write_back(k).wait()

    grid_spec = pltpu.PrefetchScalarGridSpec(
        num_scalar_prefetch=1, grid=(N_CHIPS, nt),
        in_specs=[pl.BlockSpec((tm, d), lambda j, i, idx_ref: (i, 0))] + [ANY] * n,
        out_specs=[pl.BlockSpec((tm, W_BLK), lambda j, i, idx_ref: (i, idx_ref[0] ^ j))] + [ANY] * n,
        scratch_shapes=[pltpu.VMEM(bufs[0].shape, bufs[0].dtype),
                        pltpu.SemaphoreType.DMA((3 * n,)), pltpu.SemaphoreType.DMA((3 * n,)),
                        pltpu.SemaphoreType.DMA((n_fwd,)), pltpu.SemaphoreType.DMA((n_fwd,)),
                        pltpu.SemaphoreType.DMA((4,))])
    out_shape = [jax.ShapeDtypeStruct((t_tok, D_IN), F32)] + [jax.ShapeDtypeStruct(a.shape, a.dtype) for a in bufs]
    res = pl.pallas_call(
        body, name="gather_in_proj", grid_spec=grid_spec, out_shape=out_shape,
        input_output_aliases={2 + t: 1 + t for t in range(n)}, compiler_params=_params())(idx, h, *bufs)
    return res[0], res[1:]


def _row_tile(rows, row_bytes, cap_bytes=2 * 1024 * 1024):
    best = None
    for tr in range(8, rows + 1, 8):
        if rows % tr == 0 and tr * row_bytes <= cap_bytes:
            best = tr
    return best if best is not None else rows


XOR_ORDER = (3, 2, 1)


def _grads_reduce_scatter(h, dproj, squares, small, idx):
    t, d = h.shape
    nsq = len(squares)
    hr = d // 2
    qr = ROW_BLK // 2
    tk = min(t, 1024)
    nk = t // tk
    last = N_CHIPS - 1
    n_phase = 3

    def dest(s, idx_ref):
        xo = jnp.where(s == 0, XOR_ORDER[0], jnp.where(s == 1, XOR_ORDER[1], jnp.where(s == 2, XOR_ORDER[2], 0)))
        return idx_ref[0] ^ xo

    def k_sq(p, k):
        return jnp.where(p == 0, k, nk - 1)

    def k_w(p, k):
        return jnp.where(p == 0, 0, k)

    in_specs = [
        pl.BlockSpec((tk, hr), lambda s, p, k, idx_ref: (k_w(p, k), (1 - idx_ref[1] + jnp.maximum(p - 1, 0)) % 2)),
        pl.BlockSpec((tk, W_BLK), lambda s, p, k, idx_ref: (k_w(p, k), dest(s, idx_ref)))]
    for q in range(nsq):
        in_specs.append(pl.BlockSpec((tk, ROW_BLK), lambda s, p, k, idx_ref: (k_sq(p, k), dest(s, idx_ref))))
        in_specs.append(pl.BlockSpec((tk, d), lambda s, p, k, idx_ref: (k_sq(p, k), 0)))

    def body(idx_ref, *refs):
        nj = 1 + nsq
        h_ref, dp_ref = refs[0], refs[1]
        sq_in = refs[2:2 + 2 * nsq]
        small_in = refs[2 * nj]
        outs = refs[2 * nj + 1:3 * nj + 2]
        landing = refs[3 * nj + 2:4 * nj + 3]
        sc = refs[4 * nj + 3:]
        acc_w, xr_w, sb_w = sc[0:3]
        sq_sc = [sc[3 + 3 * q:6 + 3 * q] for q in range(nsq)]
        sm, smx = sc[3 * nj:3 * nj + 2]
        x_send, x_recv, i_send, i_recv, f_send, f_recv, o_sem, l_sem = sc[3 * nj + 2:]
        s, p, k = pl.program_id(0), pl.program_id(1), pl.program_id(2)
        x, y, c = _mesh_pos()
        sib = (x, y, 1 - c)
        peers = [((1 - x) if xo & 2 else x, (1 - y) if xo & 1 else y, c) for xo in XOR_ORDER]
        slot = s % 2
        mine_w = pl.ds(pl.multiple_of(c * hr, 8), hr)
        theirs_w = pl.ds(pl.multiple_of((1 - c) * hr, 8), hr)
        mine_q = pl.ds(pl.multiple_of(c * qr, 8), qr)
        theirs_q = pl.ds(pl.multiple_of((1 - c) * qr, 8), qr)

        def exch(j, src, dst):
            return _remote(src, dst, x_send, x_recv, 2 * j + slot, sib)

        sbufs = [sb_w] + [sq_sc[q][2] for q in range(nsq)]

        def ici(j, ss):
            return _remote(sbufs[j].at[ss], landing[j].at[ss], i_send, i_recv, last * j + ss, peers[ss])

        def exchanges():
            cps = [exch(0, acc_w.at[0], xr_w.at[slot])]
            cps += [exch(1 + q, sq_sc[q][0].at[theirs_q, :], sq_sc[q][1].at[slot]) for q in range(nsq)]
            return cps

        def small_send(ss):
            return _remote(sm.at[c], landing[nj].at[ss], i_send, i_recv, last * nj + ss, peers[ss])

        def small_start():
            load = pltpu.make_async_copy(small_in, sm, l_sem.at[nj + 1])
            load.start()
            load.wait()
            swap = _remote(sm, smx.at[pl.ds(0, 2)], x_send, x_recv, 2 * nj, sib)
            swap.start()
            swap.wait_recv()
            swap.wait_send()
            sm[...] = sm[...] + smx[0:2]
            for ss in range(last):
                small_send(ss).start()

        def pair_ref(j):
            return acc_w.at[1] if j == 0 else sq_sc[j - 1][0].at[mine_q, :]

        def sq_phase():
            pl.when((s == 0) & (k == 0))(small_start)
            for q in range(nsq):
                acc = sq_sc[q][0]

                @pl.when(k == 0)
                def _():
                    acc[...] = jnp.zeros((ROW_BLK, d), F32)

                acc[...] += _dot_tn(sq_in[2 * q][...], sq_in[2 * q + 1][...])

            @pl.when(k == nk - 1)
            def _():
                for cp in exchanges()[1:]:
                    cp.start()

        def w_phase(hf):
            @pl.when(k == 0)
            def _():
                acc_w[hf] = jnp.zeros((hr, W_BLK), F32)

            acc_w[hf] += _dot_tn(h_ref[...], dp_ref[...])

            @pl.when(k == nk - 1)
            def _():
                if hf == 0:
                    exchanges()[0].start()
                else:
                    finish_step()

        def finish_step():
            for cp in exchanges():
                cp.wait_recv()
                cp.wait_send()
            acc_w[1] += xr_w[slot]
            for q in range(nsq):
                sq_sc[q][0][mine_q, :] += sq_sc[q][1][slot]
            for ss in range(last):
                @pl.when(s == ss)
                def _():
                    for j in range(nj):
                        sbufs[j][ss] = pair_ref(j)[...].astype(BF16)
                        ici(j, ss).start()

            @pl.when(s == last)
            def _():
                for ss in range(last):
                    for j in range(nj):
                        ici(j, ss).wait_recv()
                        ici(j, ss).wait_send()
                    small_send(ss).wait_recv()
                    small_send(ss).wait_send()
                stage = [pltpu.make_async_copy(landing[j], sbufs[j], l_sem.at[j]) for j in range(nj)]
                stage.append(pltpu.make_async_copy(landing[nj], smx, l_sem.at[nj]))
                for cp in stage:
                    cp.start()
                for j in range(nj):
                    stage[j].wait()
                    total = pair_ref(j)[...]
                    for ss in range(last):
                        total = total + sbufs[j][ss].astype(F32)
                    pair_ref(j)[...] = total
                stage[nj].wait()
                by_xor = {xo: smx[ss] for ss, xo in enumerate(XOR_ORDER)}
                sm[c] = (sm[c] + by_xor[1]) + (by_xor[2] + by_xor[3])
                done = [(acc_w.at[1], outs[0].at[mine_w, :], outs[0].at[theirs_w, :])]
                done += [(pair_ref(1 + q), outs[1 + q].at[mine_q, :], outs[1 + q].at[theirs_q, :]) for q in range(nsq)]
                done.append((sm.at[c], outs[nj].at[c], outs[nj].at[1 - c]))
                copies = []
                for j, (src, mine, theirs) in enumerate(done):
                    keep = pltpu.make_async_copy(src, mine, o_sem.at[j])
                    give = _remote(src, mine, f_send, f_recv, j, sib)
                    take = _remote(src, theirs, f_send, f_recv, j, sib)
                    keep.start()
                    give.start()
                    copies.append((keep, give, take))
                for keep, give, take in copies:
                    keep.wait()
                    give.wait_send()
                    take.wait_recv()

        pl.when(p == 0)(sq_phase)
        for hf in range(2):
            pl.when(p == 1 + hf)(functools.partial(w_phase, hf))

    nj = 1 + nsq
    scratch = [pltpu.VMEM((2, hr, W_BLK), F32), pltpu.VMEM((2, hr, W_BLK), F32), pltpu.VMEM((last, hr, W_BLK), BF16)]
    for _ in range(nsq):
        scratch += [pltpu.VMEM((ROW_BLK, d), F32), pltpu.VMEM((2, qr, d), F32), pltpu.VMEM((last, qr, d), BF16)]
    scratch += [pltpu.VMEM((2, PK_HALF, LANES), F32), pltpu.VMEM((last, PK_HALF, LANES), F32)]
    scratch += [pltpu.SemaphoreType.DMA((2 * nj + 1,)), pltpu.SemaphoreType.DMA((2 * nj + 1,)),
                pltpu.SemaphoreType.DMA((last * (nj + 1),)), pltpu.SemaphoreType.DMA((last * (nj + 1),)),
                pltpu.SemaphoreType.DMA((nj + 1,)), pltpu.SemaphoreType.DMA((nj + 1,)),
                pltpu.SemaphoreType.DMA((nj + 1,)), pltpu.SemaphoreType.DMA((nj + 2,))]
    grid_spec = pltpu.PrefetchScalarGridSpec(
        num_scalar_prefetch=1, grid=(N_CHIPS, n_phase, nk), in_specs=in_specs + [ANY],
        out_specs=[ANY] * (2 * nj + 2), scratch_shapes=scratch)
    out_shape = [jax.ShapeDtypeStruct((d, W_BLK), F32)] + [jax.ShapeDtypeStruct((ROW_BLK, d), F32)] * nsq
    out_shape.append(jax.ShapeDtypeStruct((2, PK_HALF, LANES), F32))
    out_shape += [jax.ShapeDtypeStruct((last, hr, W_BLK), BF16)] + [jax.ShapeDtypeStruct((last, qr, d), BF16)] * nsq
    out_shape.append(jax.ShapeDtypeStruct((last, PK_HALF, LANES), F32))
    flat = [a for pair in squares for a in pair]
    res = pl.pallas_call(body, name="grads_reduce_scatter", grid_spec=grid_spec, out_shape=out_shape,
                         compiler_params=_params())(idx, h, dproj, *flat, small)
    return res[:nj + 1]


_VEC_NAMES = ("norm_g", "conv_b", "lru_b_a", "lru_b_x", "lru_lambda", "final_norm_g")


def _pack_small(p, conv_full=None, scalar=None):
    rows = [p["lru_w_a"].reshape(PK_WX - PK_WA, LANES), p["lru_w_x"].reshape(PK_VEC - PK_WX, LANES)]
    rows += [p[k].reshape(8, LANES) for k in _VEC_NAMES]
    rows.append(jnp.pad(p["attn_sinks"].reshape(1, N_Q_HEADS), ((0, 7), (0, LANES - N_Q_HEADS))))
    rows.append(jnp.zeros((32, LANES), F32) if conv_full is None else conv_full.reshape(32, LANES))
    tail = PK_ROWS - PK_SCALAR
    if scalar is None:
        rows.append(jnp.zeros((tail, LANES), F32))
    else:
        rows.append(jnp.pad(scalar.reshape(1, 1), ((0, tail - 1), (0, LANES - 1))))
    return jnp.concatenate(rows, axis=0)


def _unpack_small(pk, like):
    out = {"lru_w_a": pk[PK_WA:PK_WX].reshape(like["lru_w_a"].shape),
           "lru_w_x": pk[PK_WX:PK_VEC].reshape(like["lru_w_x"].shape)}
    for j, k in enumerate(_VEC_NAMES):
        out[k] = pk[PK_VEC + 8 * j:PK_VEC + 8 * j + 8].reshape(like[k].shape)
    out["attn_sinks"] = pk[PK_SINK:PK_SINK + 1, :N_Q_HEADS].reshape(like["attn_sinks"].shape)
    return out


_WEIGHTS = ("norm_g", "w_in", "conv_w", "conv_b", "lru_w_a", "lru_b_a", "lru_w_x", "lru_b_x", "lru_lambda",
            "attn_sinks", "w_rnn_out", "w_attn_out", "w_o", "final_norm_g")
_SMALL = ("norm_g", "conv_b", "lru_w_a", "lru_b_a", "lru_w_x", "lru_b_x", "lru_lambda", "attn_sinks", "final_norm_g")
_ROW_SHARDED = ("w_rnn_out", "w_attn_out", "w_o")


def kernel(x, norm_g, w_in, conv_w, conv_b, lru_w_a, lru_b_a, lru_w_x, lru_b_x, lru_lambda, attn_sinks, w_rnn_out, w_attn_out, w_o, final_norm_g, loss_target, m_norm_g, m_w_in, m_conv_w, m_conv_b, m_lru_w_a, m_lru_b_a, m_lru_w_x, m_lru_b_x, m_lru_lambda, m_attn_sinks, m_w_rnn_out, m_w_attn_out, m_w_o, m_final_norm_g, v_norm_g, v_w_in, v_conv_w, v_conv_b, v_lru_w_a, v_lru_b_a, v_lru_w_x, v_lru_b_x, v_lru_lambda, v_attn_sinks, v_w_rnn_out, v_w_attn_out, v_w_o, v_final_norm_g):
    w = dict(norm_g=norm_g, w_in=w_in, conv_w=conv_w, conv_b=conv_b, lru_w_a=lru_w_a, lru_b_a=lru_b_a, lru_w_x=lru_w_x,
             lru_b_x=lru_b_x, lru_lambda=lru_lambda, attn_sinks=attn_sinks, w_rnn_out=w_rnn_out, w_attn_out=w_attn_out,
             w_o=w_o, final_norm_g=final_norm_g)
    m = dict(norm_g=m_norm_g, w_in=m_w_in, conv_w=m_conv_w, conv_b=m_conv_b, lru_w_a=m_lru_w_a, lru_b_a=m_lru_b_a,
             lru_w_x=m_lru_w_x, lru_b_x=m_lru_b_x, lru_lambda=m_lru_lambda, attn_sinks=m_attn_sinks,
             w_rnn_out=m_w_rnn_out, w_attn_out=m_w_attn_out, w_o=m_w_o, final_norm_g=m_final_norm_g)
    v = dict(norm_g=v_norm_g, w_in=v_w_in, conv_w=v_conv_w, conv_b=v_conv_b, lru_w_a=v_lru_w_a, lru_b_a=v_lru_b_a,
             lru_w_x=v_lru_w_x, lru_b_x=v_lru_b_x, lru_lambda=v_lru_lambda, attn_sinks=v_attn_sinks,
             w_rnn_out=v_w_rnn_out, w_attn_out=v_w_attn_out, w_o=v_w_o, final_norm_g=v_final_norm_g)
    mx, my, mc = _mesh_pos()
    me = 2 * mx + my
    d = D_MODEL

    slot0 = jnp.stack([me, jnp.zeros_like(me)]).astype(jnp.int32)
    buf_in = _put_slot(w["w_in"][0], N_CHIPS, slot0, d, BF16, "cast_w_in")
    buf_cw = _put_slot(w["conv_w"][0], N_CHIPS, slot0, CONV_WIDTH, F32, "slot_conv_w")
    row_sharded = [_put_slot(w[k][0], N_CHIPS, slot0, ROW_BLK, BF16, "cast_" + k) for k in _ROW_SHARDED]
    h = _rmsnorm_fwd(x.reshape(-1, d), w["norm_g"])
    proj, (g_in, g_cw) = _gather_in_proj(h, [buf_in, buf_cw], [True, False], jnp.reshape(me, (1,)).astype(jnp.int32))
    conv_full = g_cw.transpose(1, 0, 2).reshape(CONV_WIDTH, D_RNN)

    loss_local, grad_x, h, dproj, squares, gsmall = _local_grads(
        x, loss_target, h, proj, w["norm_g"], g_in, conv_full, w["conv_b"], w["lru_w_a"][0], w["lru_b_a"], w["lru_w_x"][0],
        w["lru_b_x"], w["lru_lambda"], w["attn_sinks"][0], row_sharded, w["final_norm_g"].reshape(1, d))
    gpack = _pack_small(gsmall, gsmall["conv_w"], loss_local).reshape(2, PK_HALF, LANES)
    f_in, f_r, f_a, f_o, spack = _grads_reduce_scatter(h, dproj, squares, gpack, jnp.stack([me, mc]).astype(jnp.int32))
    spack = spack.reshape(PK_ROWS, LANES)
    loss = spack[PK_SCALAR, 0]

    grads = _unpack_small(spack, w)
    conv_all = spack[PK_CONV:PK_CONV + 32].reshape(CONV_WIDTH, D_RNN)
    grads["conv_w"] = lax.dynamic_slice_in_dim(conv_all, me * (D_RNN // N_CHIPS), D_RNN // N_CHIPS, axis=1)[None]
    grads["w_in"] = f_in[None]
    grads["w_rnn_out"], grads["w_attn_out"], grads["w_o"] = f_r[None], f_a[None], f_o[None]

    delta, new_m, new_v = {}, {}, {}
    for k in ("w_in",) + _ROW_SHARDED:
        dk, mk, vk = _adamw(w[k][0], grads[k][0], m[k][0], v[k][0], "adamw_" + k)
        delta[k], new_m[k], new_v[k] = dk[None], mk[None], vk[None]
    shp = (2 * CONV_WIDTH, LANES)
    dk, mk, vk = _adamw(w["conv_w"].reshape(shp), grads["conv_w"].reshape(shp), m["conv_w"].reshape(shp),
                        v["conv_w"].reshape(shp), "adamw_conv_w")
    delta["conv_w"], new_m["conv_w"], new_v["conv_w"] = (a.reshape(w["conv_w"].shape) for a in (dk, mk, vk))
    dk, mk, vk = _adamw(_pack_small(w), spack, _pack_small(m), _pack_small(v), "adamw_small")
    for src, dst in ((dk, delta), (mk, new_m), (vk, new_v)):
        dst.update(_unpack_small(src, w))

    return (loss, grad_x, *[grads[k] for k in _WEIGHTS], *[delta[k] for k in _WEIGHTS],
            *[new_m[k] for k in _WEIGHTS], *[new_v[k] for k in _WEIGHTS])
```

```python
import functools
import math

import jax
import jax.numpy as jnp
from jax import lax
from jax.experimental import pallas as pl
from jax.experimental.pallas import tpu as pltpu

F32 = jnp.float32
BF16 = jnp.bfloat16
MESH = pl.DeviceIdType.MESH

D_MODEL = 1024
D_RNN = 1024
N_RNN_BLOCKS = 8
RNN_BLOCK = D_RNN // N_RNN_BLOCKS
CONV_WIDTH = 4
LRU_C = 8.0
HEAD_DIM = 64
N_Q_HEADS = 16
N_KV_HEADS = 4
D_ATTN = N_Q_HEADS * HEAD_DIM
D_KV = N_KV_HEADS * HEAD_DIM
WINDOW = 128
ROPE_DIM = HEAD_DIM // 4
ROPE_THETA = 500000.0
NORM_EPS = 1e-6
OFF_RNN_X = 0
OFF_RNN_G = OFF_RNN_X + D_RNN
OFF_Q = OFF_RNN_G + D_RNN
OFF_K = OFF_Q + D_ATTN
OFF_V = OFF_K + D_KV
OFF_ATTN_G = OFF_V + D_KV
OFF_MERGE_R = OFF_ATTN_G + D_ATTN
OFF_MERGE_A = OFF_MERGE_R + D_MODEL
D_IN = OFF_MERGE_A + D_MODEL

ADAM_LR = 0.001
ADAM_B1 = 0.9
ADAM_B2 = 0.999
ADAM_EPS = 1e-08
ADAM_WD = 0.01
ADAM_STEP = 10

N_CHIPS = 4
W_BLK = D_IN // N_CHIPS
ROW_BLK = D_MODEL // N_CHIPS
LANES = 128
ATT_BLK = 128
VMEM_LIMIT = 56 * 1024 * 1024
NEG_BIG = -1e30
ATTN_SCALE = 1.0 / math.sqrt(HEAD_DIM)

PK_WA = 0
PK_WX = PK_WA + N_RNN_BLOCKS * RNN_BLOCK
PK_VEC = PK_WX + N_RNN_BLOCKS * RNN_BLOCK
PK_SINK = PK_VEC + 6 * 8
PK_CONV = PK_SINK + 8
PK_SCALAR = PK_CONV + 32
PK_ROWS = PK_SCALAR + 8
PK_HALF = PK_ROWS // 2


def _params(**kw):
    return pltpu.CompilerParams(vmem_limit_bytes=VMEM_LIMIT, **kw)


def _sigmoid(z):
    return 1.0 / (1.0 + jnp.exp(-z))


def _dot(a, b):
    return jnp.dot(a, b, preferred_element_type=F32)


def _dot_nt(a, b):
    return lax.dot_general(a, b, (((1,), (1,)), ((), ())), preferred_element_type=F32)


def _dot_tn(a, b):
    return lax.dot_general(a, b, (((0,), (0,)), ((), ())), preferred_element_type=F32)


def _put_slot(src, n_slots, slot_and_blk, rows, dtype, name):
    _, c = src.shape
    tr = _row_tile(rows, c * 4)
    steps = rows // tr

    def body(idx_ref, s_ref, o_ref):
        o_ref[...] = s_ref[...].astype(dtype)

    grid_spec = pltpu.PrefetchScalarGridSpec(
        num_scalar_prefetch=1, grid=(steps,),
        in_specs=[pl.BlockSpec((tr, c), lambda i, idx_ref: (idx_ref[1] * steps + i, 0))],
        out_specs=pl.BlockSpec((None, tr, c), lambda i, idx_ref: (idx_ref[0], i, 0)))
    return pl.pallas_call(body, name=name, grid_spec=grid_spec,
                          out_shape=jax.ShapeDtypeStruct((n_slots, rows, c), dtype),
                          compiler_params=_params())(slot_and_blk, src)


def _adamw(w, g, m, v, name):
    r, c = w.shape
    tr = _row_tile(r, c * 4, 1024 * 1024)
    c1 = 1.0 - ADAM_B1 ** ADAM_STEP
    c2 = 1.0 - ADAM_B2 ** ADAM_STEP

    def body(w_ref, g_ref, m_ref, v_ref, d_ref, nm_ref, nv_ref):
        gv = g_ref[...]
        nm = ADAM_B1 * m_ref[...] + (1.0 - ADAM_B1) * gv
        nv = ADAM_B2 * v_ref[...] + (1.0 - ADAM_B2) * (gv * gv)
        m_hat = nm / c1
        v_hat = nv / c2
        d_ref[...] = -ADAM_LR * (m_hat / (jnp.sqrt(v_hat) + ADAM_EPS) + ADAM_WD * w_ref[...])
        nm_ref[...] = nm
        nv_ref[...] = nv

    spec = pl.BlockSpec((tr, c), lambda i: (i, 0))
    sds = jax.ShapeDtypeStruct((r, c), F32)
    return pl.pallas_call(
        body, name=name, grid=(r // tr,), out_shape=(sds, sds, sds),
        in_specs=[spec, spec, spec, spec], out_specs=(spec, spec, spec), compiler_params=_params())(w, g, m, v)


def _grad_x(dproj, w_bm, x, dx2, g):
    t = dproj.shape[0]
    nb, d, wb = w_bm.shape
    tm = min(t, 512)

    def body(dp_ref, w_ref, x_ref, dx2_ref, g_ref, gx_ref, dg_ref, acc_ref):
        i, k = pl.program_id(0), pl.program_id(1)

        @pl.when(k == 0)
        def _():
            acc_ref[...] = jnp.zeros_like(acc_ref)

        acc_ref[...] += _dot_nt(dp_ref[...], w_ref[...])

        @pl.when((i == 0) & (k == 0))
        def _():
            dg_ref[...] = jnp.zeros_like(dg_ref)

        @pl.when(k == nb - 1)
        def _():
            dhv = acc_ref[...]
            xv = x_ref[...]
            r = lax.rsqrt(jnp.mean(xv * xv, axis=-1, keepdims=True) + NORM_EPS)
            nrm = xv * r
            dn = dhv * g_ref[...]
            gx_ref[...] = dx2_ref[...] + r * (dn - nrm * jnp.mean(dn * nrm, axis=-1, keepdims=True))
            dg_ref[...] += jnp.sum(dhv * nrm, axis=0, keepdims=True)

    tile = pl.BlockSpec((tm, d), lambda i, k: (i, 0))
    vec = pl.BlockSpec((1, d), lambda i, k: (0, 0))
    return pl.pallas_call(
        body, name="grad_x", grid=(t // tm, nb),
        out_shape=(jax.ShapeDtypeStruct((t, d), F32), jax.ShapeDtypeStruct((1, d), F32)),
        in_specs=[pl.BlockSpec((tm, wb), lambda i, k: (i, k)), pl.BlockSpec((None, d, wb), lambda i, k: (k, 0, 0)),
                  tile, tile, vec],
        out_specs=(tile, vec), scratch_shapes=[pltpu.VMEM((tm, d), F32)], compiler_params=_params())(dproj, w_bm, x, dx2, g)


def _shift_down(v, d, fill):
    n = v.shape[0]
    if d % 8 == 0:
        return jnp.concatenate([jnp.full((d,) + v.shape[1:], fill, v.dtype), v[: n - d]], axis=0)
    row = lax.broadcasted_iota(jnp.int32, v.shape, 0)
    return jnp.where(row >= d, pltpu.roll(v, d, axis=0), fill)


def _shift_up(v, d, fill):
    n = v.shape[0]
    if d % 8 == 0:
        return jnp.concatenate([v[d:], jnp.full((d,) + v.shape[1:], fill, v.dtype)], axis=0)
    row = lax.broadcasted_iota(jnp.int32, v.shape, 0)
    return jnp.where(row < n - d, pltpu.roll(v, n - d, axis=0), fill)


def _scan_log(a, b, shift):
    n = a.shape[0]
    d = 1
    while d < n:
        b = a * shift(b, d, 0.0) + b
        if 2 * d < n:
            a = a * shift(a, d, 1.0)
        d *= 2
    return b


SUBLANES = 8


def _scan(a, b, sa_ref, sb_ref, reverse):
    n, c = a.shape
    g = n // SUBLANES
    a3, b3 = a.reshape(g, SUBLANES, c), b.reshape(g, SUBLANES, c)
    sub = lax.broadcasted_iota(jnp.int32, a3.shape, 1)
    d = 1
    while d < SUBLANES:
        keep = (sub < SUBLANES - d) if reverse else (sub >= d)
        amount = SUBLANES - d if reverse else d
        b3 = a3 * jnp.where(keep, pltpu.roll(b3, amount, axis=1), 0.0) + b3
        a3 = a3 * jnp.where(keep, pltpu.roll(a3, amount, axis=1), 1.0)
        d *= 2
    sa_ref[...] = a3.reshape(n, c)
    sb_ref[...] = b3.reshape(n, c)
    edge = 0 if reverse else SUBLANES - 1
    shift = _shift_up if reverse else _shift_down
    totals = _scan_log(sa_ref[pl.ds(edge, g, stride=SUBLANES), :], sb_ref[pl.ds(edge, g, stride=SUBLANES), :], shift)
    carry = shift(totals, 1, 0.0)
    return (a3 * carry[:, None, :] + b3).reshape(n, c)


def _neg_expm1_twice(log_a, a):
    return -jnp.tanh(log_a) * (a * a + 1.0)


def _softplus(z):
    e = jnp.exp(-jnp.abs(z))
    w = 1.0 + e
    log1p = jnp.where(w == 1.0, e, jnp.log(w) * (e / jnp.where(w == 1.0, 1.0, w - 1.0)))
    return jnp.maximum(z, 0.0) + log1p


def _conv(up, cw, cb):
    out = cb + cw[CONV_WIDTH - 1:CONV_WIDTH, :] * up
    for j in range(CONV_WIDTH - 1):
        out = out + cw[j:j + 1, :] * _shift_down(up, CONV_WIDTH - 1 - j, 0.0)
    return out


def _lru_gates(u, wa_ref, ba_ref, wx_ref, bx_ref, lam_ref):
    ub = u.astype(BF16)
    r = _sigmoid(_dot(ub, wa_ref[...].astype(BF16)) + ba_ref[...])
    i = _sigmoid(_dot(ub, wx_ref[...].astype(BF16)) + bx_ref[...])
    sp = _softplus(-lam_ref[...])
    log_a = (-LRU_C) * r * sp
    a = jnp.exp(log_a)
    mult = jnp.sqrt(_neg_expm1_twice(log_a, a))
    return r, i, sp, a, mult


def _lru_specs(s):
    cb = RNN_BLOCK
    vec = pl.BlockSpec((1, cb), lambda n, b: (0, n))
    return dict(
        up=pl.BlockSpec((None, s, cb), lambda n, b: (b, 0, OFF_RNN_X // cb + n)),
        gr=pl.BlockSpec((None, s, cb), lambda n, b: (b, 0, OFF_RNN_G // cb + n)),
        act=pl.BlockSpec((None, s, cb), lambda n, b: (b, 0, n)),
        cw=pl.BlockSpec((CONV_WIDTH, cb), lambda n, b: (0, n)),
        vec=vec,
        wblk=pl.BlockSpec((None, cb, cb), lambda n, b: (n, 0, 0)),
    )


def _lru_fwd(proj3, cw, cb, wa, ba, wx, bx, lam, riders):
    bsz, s, _ = proj3.shape
    sp = _lru_specs(s)
    nr = len(riders)

    def body(up_ref, gr_ref, cw_ref, cb_ref, wa_ref, ba_ref, wx_ref, bx_ref, lam_ref, *refs):
        rider_in, (h_ref, y_ref), rider_out = refs[:nr], refs[nr:nr + 2], refs[nr + 2:2 * nr + 2]
        sa_ref, sb_ref = refs[2 * nr + 2:2 * nr + 4]
        start, finish = _row_gather(rider_in, rider_out, *refs[2 * nr + 4:])
        first = (pl.program_id(0) == 0) & (pl.program_id(1) == 0)
        last = (pl.program_id(0) == N_RNN_BLOCKS - 1) & (pl.program_id(1) == bsz - 1)
        pl.when(first)(start)
        u = _conv(up_ref[...], cw_ref[...], cb_ref[...])
        _, i, _, a, mult = _lru_gates(u, wa_ref, ba_ref, wx_ref, bx_ref, lam_ref)
        h = _scan(a, mult * (i * u), sa_ref, sb_ref, reverse=False)
        h_ref[...] = h
        g = gr_ref[...]
        y_ref[...] = (h * (g * _sigmoid(g))).astype(BF16)
        pl.when(last)(finish)

    res = pl.pallas_call(
        body, name="lru_fwd", grid=(N_RNN_BLOCKS, bsz),
        out_shape=[jax.ShapeDtypeStruct((bsz, s, D_RNN), F32), jax.ShapeDtypeStruct((bsz, s, D_RNN), BF16)] + [
            jax.ShapeDtypeStruct(r.shape, r.dtype) for r in riders],
        in_specs=[sp["up"], sp["gr"], sp["cw"], sp["vec"], sp["wblk"], sp["vec"], sp["wblk"], sp["vec"], sp["vec"]] + [
            ANY] * nr,
        out_specs=[sp["act"], sp["act"]] + [ANY] * nr, input_output_aliases={9 + t: 2 + t for t in range(nr)},
        scratch_shapes=[pltpu.VMEM((s, RNN_BLOCK), F32)] * 2 + [pltpu.SemaphoreType.DMA((3 * nr,))] * 4,
        compiler_params=_params())(proj3, proj3, cw, cb, wa, ba, wx, bx, lam, *riders)
    return res[0], res[1], res[2:]


def _lru_bwd(proj3, h3, dy3, dproj3, cw, cb, wa, ba, wx, bx, lam):
    bsz, s, _ = proj3.shape
    sp = _lru_specs(s)
    n_steps = N_RNN_BLOCKS * bsz

    def body(up_ref, gr_ref, h_ref, dy_ref, cw_ref, cb_ref, wa_ref, ba_ref, wx_ref, bx_ref, lam_ref, dp_in,
             dp_ref, dcw_ref, dcb_ref, dwa_ref, dba_ref, dwx_ref, dbx_ref, dlam_ref, sa_ref, sb_ref,
             dup_st, dgr_st, o_sems):
        del dp_in
        blk = pl.program_id(0)
        b = pl.program_id(1)
        step = blk * bsz + b
        slot = step % 2
        stages = [dup_st, dgr_st]
        dsts = [dp_ref.at[b, :, pl.ds(pl.multiple_of(OFF_RNN_X + blk * RNN_BLOCK, LANES), RNN_BLOCK)],
                dp_ref.at[b, :, pl.ds(pl.multiple_of(OFF_RNN_G + blk * RNN_BLOCK, LANES), RNN_BLOCK)]]
        _staged_reuse(step, stages, dsts, o_sems, slot)
        up = up_ref[...]
        cwv = cw_ref[...]
        u = _conv(up, cwv, cb_ref[...])
        r, i, spv, a, mult = _lru_gates(u, wa_ref, ba_ref, wx_ref, bx_ref, lam_ref)
        h = h_ref[...]
        g = gr_ref[...]
        dy = dy_ref[...]
        sg = _sigmoid(g)
        dgr_st[slot] = (dy * h * (sg * (1.0 + g * (1.0 - sg)))).astype(BF16)
        dh = dy * (g * sg)
        adj = _scan(_shift_up(a, 1, 0.0), dh, sa_ref, sb_ref, reverse=True)
        da = adj * _shift_down(h, 1, 0.0)
        dmult = adj * (i * u)
        di = adj * mult * u
        du = adj * mult * i
        dla = da * a - dmult * (a * a) / mult
        dr = dla * ((-LRU_C) * spv)
        dsp = jnp.sum(dla * ((-LRU_C) * r), axis=0, keepdims=True)
        dza = dr * r * (1.0 - r)
        dzx = di * i * (1.0 - i)
        ub = u.astype(BF16)
        dzab = dza.astype(BF16)
        dzxb = dzx.astype(BF16)
        du = du + _dot_nt(dzab, wa_ref[...].astype(BF16)) + _dot_nt(dzxb, wx_ref[...].astype(BF16))
        dup = cwv[CONV_WIDTH - 1:CONV_WIDTH, :] * du
        for j in range(CONV_WIDTH - 1):
            dup = dup + cwv[j:j + 1, :] * _shift_up(du, CONV_WIDTH - 1 - j, 0.0)
        dup_st[slot] = dup.astype(BF16)
        _staged_flush(step, n_steps, stages, dsts, o_sems, slot)

        @pl.when(b == 0)
        def _():
            for ref in (dcw_ref, dcb_ref, dwa_ref, dba_ref, dwx_ref, dbx_ref, dlam_ref):
                ref[...] = jnp.zeros_like(ref)

        rows = [jnp.sum(du * _shift_down(up, CONV_WIDTH - 1 - j, 0.0), axis=0, keepdims=True)
                for j in range(CONV_WIDTH - 1)]
        rows.append(jnp.sum(du * up, axis=0, keepdims=True))
        dcw_ref[...] += jnp.concatenate(rows, axis=0)
        dcb_ref[...] += jnp.sum(du, axis=0, keepdims=True)
        dwa_ref[...] += _dot_tn(ub, dzab)
        dba_ref[...] += jnp.sum(dza, axis=0, keepdims=True)
        dwx_ref[...] += _dot_tn(ub, dzxb)
        dbx_ref[...] += jnp.sum(dzx, axis=0, keepdims=True)
        dlam_ref[...] += dsp * (-_sigmoid(-lam_ref[...]))

    vec = jax.ShapeDtypeStruct((1, D_RNN), F32)
    wsd = jax.ShapeDtypeStruct((N_RNN_BLOCKS, RNN_BLOCK, RNN_BLOCK), F32)
    return pl.pallas_call(
        body, name="lru_bwd", grid=(N_RNN_BLOCKS, bsz),
        out_shape=(jax.ShapeDtypeStruct(dproj3.shape, dproj3.dtype), jax.ShapeDtypeStruct((CONV_WIDTH, D_RNN), F32),
                   vec, wsd, vec, wsd, vec, vec),
        in_specs=[sp["up"], sp["gr"], sp["act"], sp["act"], sp["cw"], sp["vec"], sp["wblk"], sp["vec"],
                  sp["wblk"], sp["vec"], sp["vec"], ANY],
        out_specs=(ANY, sp["cw"], sp["vec"], sp["wblk"], sp["vec"], sp["wblk"], sp["vec"], sp["vec"]),
        input_output_aliases={11: 0},
        scratch_shapes=[pltpu.VMEM((s, RNN_BLOCK), F32)] * 2 + [pltpu.VMEM((2, s, RNN_BLOCK), BF16)] * 2 + [
            pltpu.SemaphoreType.DMA((4,))],
        compiler_params=_params())(proj3, proj3, h3, dy3, cw, cb, wa, ba, wx, bx, lam, dproj3)


def _rope_tables(s):
    half = ROPE_DIM // 2
    pos = jnp.arange(s, dtype=F32)
    inv_freq = ROPE_THETA ** (-jnp.arange(0, ROPE_DIM, 2, dtype=F32) / ROPE_DIM)
    ang = pos[:, None] * inv_freq[None, :]
    cos, sin = jnp.cos(ang), jnp.sin(ang)
    rest = HEAD_DIM - ROPE_DIM
    cos64 = jnp.concatenate([cos, cos, jnp.ones((s, rest), F32)], axis=1)
    sin64 = jnp.concatenate([-sin, sin, jnp.zeros((s, rest), F32)], axis=1)
    assert half * 2 == ROPE_DIM
    return jnp.tile(cos64, (1, LANES // HEAD_DIM)), jnp.tile(sin64, (1, LANES // HEAD_DIM))


def _swap_rot_halves(v):
    half = ROPE_DIM // 2
    lane = lax.broadcasted_iota(jnp.int32, v.shape, 1) % HEAD_DIM
    second = jnp.where(lane < ROPE_DIM, pltpu.roll(v, half, axis=1), 0.0)
    return jnp.where(lane < half, pltpu.roll(v, LANES - half, axis=1), second)


def _rope(v, cos, sin):
    tiles = []
    for t in range(v.shape[1] // LANES):
        vt = v[:, t * LANES:(t + 1) * LANES]
        tiles.append(vt * cos + _swap_rot_halves(vt) * sin)
    return tiles[0] if len(tiles) == 1 else jnp.concatenate(tiles, axis=1)


def _unrope(v, cos, sin):
    tiles = []
    for t in range(v.shape[1] // LANES):
        vt = v[:, t * LANES:(t + 1) * LANES]
        tiles.append(vt * cos + _swap_rot_halves(vt * sin))
    return tiles[0] if len(tiles) == 1 else jnp.concatenate(tiles, axis=1)


HEADS_PER_STEP = 8
QW = HEADS_PER_STEP * HEAD_DIM
N_PAIRS = N_Q_HEADS // HEADS_PER_STEP
Q_PER_KV = N_Q_HEADS // N_KV_HEADS
KV_PER_STEP = HEADS_PER_STEP // Q_PER_KV


QT_COLS = Q_PER_KV * ATT_BLK


def _attn_saved_shapes(bsz, s):
    nb = s // ATT_BLK
    pad = s + ATT_BLK
    return [(bsz, N_PAIRS, nb, LANES, QT_COLS), (bsz, N_PAIRS, KV_PER_STEP, pad, LANES),
            (bsz, N_PAIRS, KV_PER_STEP, pad, LANES), (bsz, N_PAIRS, LANES, pad)]


def _attn_specs(s, order):
    def mk(width, base, **kw):
        if order == "bp":
            return pl.BlockSpec((None, s, width), lambda b, p: (b, 0, base + p), **kw)
        return pl.BlockSpec((None, s, width), lambda p, b: (b, 0, base + p), **kw)

    def saved(shape, **kw):
        blk = (None, None) + shape[2:]
        zeros = (0,) * (len(shape) - 2)
        if order == "bp":
            return pl.BlockSpec(blk, lambda b, p: (b, p) + zeros, **kw)
        return pl.BlockSpec(blk, lambda p, b: (b, p) + zeros, **kw)

    one = dict(pipeline_mode=pl.Buffered(1))
    tbl = pl.BlockSpec((s, LANES), lambda *_: (0, 0))
    shapes = _attn_saved_shapes(1, s)
    return dict(q=mk(QW, OFF_Q // QW), k=mk(LANES, OFF_K // LANES), v=mk(LANES, OFF_V // LANES),
                g=mk(QW, OFF_ATTN_G // QW), act=mk(QW, 0), kv=mk(LANES, 0), tbl=tbl,
                g1=mk(QW, OFF_ATTN_G // QW, **one), act1=mk(QW, 0, **one),
                saved=[saved(sh) for sh in shapes], saved1=[saved(sh, **one) for sh in shapes],
                smem=pl.BlockSpec(memory_space=pltpu.SMEM))


def _to_qt(blk):
    rows = []
    for j in range(KV_PER_STEP):
        cols = []
        for tt in range(2):
            t = 2 * j + tt
            tr = blk[:, t * LANES:(t + 1) * LANES].T
            cols += [tr[0:HEAD_DIM, :], tr[HEAD_DIM:, :]]
        rows.append(jnp.concatenate(cols, axis=1))
    return jnp.concatenate(rows, axis=0)


def _from_qt(xt):
    tiles = []
    for j in range(KV_PER_STEP):
        for tt in range(2):
            g0 = 2 * tt
            pair = jnp.concatenate([xt[j * HEAD_DIM:(j + 1) * HEAD_DIM, (g0 + i) * ATT_BLK:(g0 + i + 1) * ATT_BLK]
                                    for i in range(2)], axis=0)
            tiles.append(pair.T)
    return jnp.concatenate(tiles, axis=1)


def _attn_prep(q_ref, k_ref, v_ref, cos_ref, sin_ref, qt_ref, km_ref, vm_ref, kt_ref, vt_ref, nb):
    zeros = jnp.zeros((ATT_BLK, LANES), BF16)
    for j in range(KV_PER_STEP):
        km_ref[j, 0:ATT_BLK, :] = zeros
        vm_ref[j, 0:ATT_BLK, :] = zeros
    kt_ref[:, 0:ATT_BLK] = zeros
    vt_ref[:, 0:ATT_BLK] = zeros
    head_of_lane = lax.broadcasted_iota(jnp.int32, (ATT_BLK, LANES), 1) // HEAD_DIM

    def prep(n, carry):
        r0 = pl.multiple_of(n * ATT_BLK, ATT_BLK)
        cs = cos_ref[pl.ds(r0, ATT_BLK), :]
        sn = sin_ref[pl.ds(r0, ATT_BLK), :]
        qt_ref[n] = _to_qt(_rope(q_ref[pl.ds(r0, ATT_BLK), :], cs, sn) * ATTN_SCALE).astype(BF16)
        k = _rope(k_ref[pl.ds(r0, ATT_BLK), :], cs, sn)
        v = v_ref[pl.ds(r0, ATT_BLK), :]
        for j in range(KV_PER_STEP):
            km_ref[j, pl.ds(r0 + ATT_BLK, ATT_BLK), :] = jnp.where(head_of_lane == j, k, 0.0).astype(BF16)
            vm_ref[j, pl.ds(r0 + ATT_BLK, ATT_BLK), :] = jnp.where(head_of_lane == j, v, 0.0).astype(BF16)
        kt_ref[:, pl.ds(r0 + ATT_BLK, ATT_BLK)] = k.T.astype(BF16)
        vt_ref[:, pl.ds(r0 + ATT_BLK, ATT_BLK)] = v.T.astype(BF16)
        return carry

    lax.fori_loop(0, nb, prep, 0)


def _from_prev_block():
    key = lax.broadcasted_iota(jnp.int32, (ATT_BLK, QT_COLS), 0)
    qry = lax.broadcasted_iota(jnp.int32, (ATT_BLK, QT_COLS), 1) % ATT_BLK
    return key > qry


def _fold(tile, prev, prev_bias=None):
    top = tile[:ATT_BLK] if prev_bias is None else tile[:ATT_BLK] + prev_bias
    return jnp.where(prev, top, tile[ATT_BLK:])


def _unfold(folded, prev):
    zero = jnp.zeros_like(folded)
    return jnp.concatenate([jnp.where(prev, folded, zero), jnp.where(prev, zero, folded)], axis=0).astype(BF16)


def _no_prev_bias(n):
    return jnp.where(n == 0, NEG_BIG, 0.0).astype(F32)


def _sink_row(sink_ref, first):
    return jnp.concatenate([jnp.full((1, ATT_BLK), sink_ref[first + g], F32) for g in range(Q_PER_KV)], axis=1)


def _softmax_cols(sc, sink):
    m = jnp.maximum(jnp.max(sc, axis=0, keepdims=True), sink)
    e = jnp.exp(sc - m)
    es = jnp.exp(sink - m)
    inv = 1.0 / (jnp.sum(e, axis=0, keepdims=True) + es)
    return e * inv, es * inv


def _attn_fwd(proj3, sinks, cosf, sinf):
    bsz, s, _ = proj3.shape
    nb = s // ATT_BLK
    sp = _attn_specs(s, "bp")

    def body(sink_ref, q_ref, k_ref, v_ref, g_ref, cos_ref, sin_ref, o_ref, y_ref, qt_sc, km_sc, vm_sc, kt_ref, vt_sc):
        p = pl.program_id(1)
        _attn_prep(q_ref, k_ref, v_ref, cos_ref, sin_ref, qt_sc, km_sc, vm_sc, kt_ref, vt_sc, nb)
        kv_row = lax.broadcasted_iota(jnp.int32, (LANES, QT_COLS), 0) // HEAD_DIM
        prev = _from_prev_block()

        def blk(n, carry):
            r0 = pl.multiple_of(n * ATT_BLK, ATT_BLK)
            bias = _no_prev_bias(n)
            rq = qt_sc[n]
            vt = vt_sc[:, pl.ds(r0, 2 * ATT_BLK)]
            ots = []
            for j in range(KV_PER_STEP):
                st = _dot(km_sc[j, pl.ds(r0, 2 * ATT_BLK), :], rq)
                pc, _ = _softmax_cols(_fold(st, prev, bias), _sink_row(sink_ref, p * HEADS_PER_STEP + j * Q_PER_KV))
                ots.append(_dot(vt, _unfold(pc, prev)))
            o = _from_qt(jnp.where(kv_row == 0, ots[0], ots[1]))
            o_ref[pl.ds(r0, ATT_BLK), :] = o
            g = g_ref[pl.ds(r0, ATT_BLK), :]
            y_ref[pl.ds(r0, ATT_BLK), :] = (o * (g * _sigmoid(g))).astype(BF16)
            return carry

        lax.fori_loop(0, nb, blk, 0, unroll=2)

    res = pl.pallas_call(
        body, name="attn_fwd", grid=(bsz, N_PAIRS),
        out_shape=[jax.ShapeDtypeStruct((bsz, s, D_ATTN), F32), jax.ShapeDtypeStruct((bsz, s, D_ATTN), BF16)] + [
            jax.ShapeDtypeStruct(sh, BF16) for sh in _attn_saved_shapes(bsz, s)],
        in_specs=[sp["smem"], sp["q"], sp["k"], sp["v"], sp["g"], sp["tbl"], sp["tbl"]],
        out_specs=[sp["act"], sp["act"]] + sp["saved"],
        scratch_shapes=[pltpu.VMEM((LANES, s + ATT_BLK), BF16)],
        compiler_params=_params())(sinks, proj3, proj3, proj3, proj3, cosf, sinf)
    return res[0], res[1], res[2:]


def _attn_bwd(proj3, saved, o3, dy3, dproj3, sinks, cosf, sinf):
    bsz, s, _ = proj3.shape
    nb = s // ATT_BLK
    assert nb % 2 == 0
    sp = _attn_specs(s, "pb")
    n_steps = N_PAIRS * bsz

    def body(sink_ref, qt_sc, km_sc, vm_sc, kt_sc, g_ref, o_ref, dy_ref, cos_ref, sin_ref, dp_in,
             dp_ref, ds_ref, dot_sc, dqt_sc, dk_sc, dv_sc, dq_st, dk_st, dv_st, dg_st, o_sems):
        del dp_in
        p = pl.program_id(0)
        b = pl.program_id(1)
        step = p * bsz + b
        slot = step % 2
        stages = [dq_st, dk_st, dv_st, dg_st]
        dsts = [dp_ref.at[b, :, pl.ds(pl.multiple_of(OFF_Q + p * QW, LANES), QW)],
                dp_ref.at[b, :, pl.ds(pl.multiple_of(OFF_K + p * LANES, LANES), LANES)],
                dp_ref.at[b, :, pl.ds(pl.multiple_of(OFF_V + p * LANES, LANES), LANES)],
                dp_ref.at[b, :, pl.ds(pl.multiple_of(OFF_ATTN_G + p * QW, LANES), QW)]]
        _staged_reuse(step, stages, dsts, o_sems, slot)
        dk_sc[...] = jnp.zeros_like(dk_sc)
        dv_sc[...] = jnp.zeros_like(dv_sc)

        def gate(n, carry):
            r0 = pl.multiple_of(n * ATT_BLK, ATT_BLK)
            g = g_ref[pl.ds(r0, ATT_BLK), :]
            dy = dy_ref[pl.ds(r0, ATT_BLK), :]
            sg = _sigmoid(g)
            dg_st[slot, pl.ds(r0, ATT_BLK), :] = (dy * o_ref[pl.ds(r0, ATT_BLK), :] * (sg * (1.0 + g * (1.0 - sg)))).astype(BF16)
            dot_sc[n] = _to_qt(dy * (g * sg)).astype(BF16)
            return carry

        lax.fori_loop(0, nb, gate, 0)
        kv_lane = lax.broadcasted_iota(jnp.int32, (2 * ATT_BLK, LANES), 1) // HEAD_DIM
        kv_row = lax.broadcasted_iota(jnp.int32, (LANES, QT_COLS), 0) // HEAD_DIM
        prev = _from_prev_block()

        def blk(n, acc):
            r0 = pl.multiple_of(n * ATT_BLK, ATT_BLK)
            bias = _no_prev_bias(n)
            rq = qt_sc[n]
            rd = dot_sc[n]
            kt = kt_sc[:, pl.ds(r0, 2 * ATT_BLK)]
            dvs, dks, dqs, new_acc = [], [], [], []
            for j in range(KV_PER_STEP):
                st = _dot(km_sc[j, pl.ds(r0, 2 * ATT_BLK), :], rq)
                pc, ps = _softmax_cols(_fold(st, prev, bias), _sink_row(sink_ref, p * HEADS_PER_STEP + j * Q_PER_KV))
                dpc = _fold(_dot(vm_sc[j, pl.ds(r0, 2 * ATT_BLK), :], rd), prev)
                delta = jnp.sum(pc * dpc, axis=0, keepdims=True)
                dst = _unfold(pc * (dpc - delta), prev)
                new_acc.append(acc[j] + ps * delta)
                dvs.append(_dot_nt(_unfold(pc, prev), rd))
                dks.append(_dot_nt(dst, rq))
                dqs.append(_dot(kt, dst))
            dv_sc[pl.ds(r0, 2 * ATT_BLK), :] += jnp.where(kv_lane == 0, dvs[0], dvs[1])
            dk_sc[pl.ds(r0, 2 * ATT_BLK), :] += jnp.where(kv_lane == 0, dks[0], dks[1])
            dqt_sc[n] = jnp.where(kv_row == 0, dqs[0], dqs[1]) * ATTN_SCALE
            return tuple(new_acc)

        def blk_pair(m, acc):
            return blk(2 * m + 1, blk(2 * m, acc))

        acc = lax.fori_loop(0, nb // 2, blk_pair, tuple(jnp.zeros((1, QT_COLS), F32) for _ in range(KV_PER_STEP)))
        lane1 = lax.broadcasted_iota(jnp.int32, (1, LANES), 1)
        dsink = jnp.zeros((1, LANES), F32)
        for j in range(KV_PER_STEP):
            for i in range(Q_PER_KV):
                part = jnp.sum(acc[j][:, i * ATT_BLK:(i + 1) * ATT_BLK], axis=1, keepdims=True)
                dsink = dsink - jnp.where(lane1 == j * Q_PER_KV + i, part, 0.0)

        @pl.when(b == 0)
        def _():
            ds_ref[...] = jnp.zeros_like(ds_ref)

        ds_ref[...] += dsink

        def post(n, carry):
            r0 = pl.multiple_of(n * ATT_BLK, ATT_BLK)
            cs = cos_ref[pl.ds(r0, ATT_BLK), :]
            sn = sin_ref[pl.ds(r0, ATT_BLK), :]
            dq_st[slot, pl.ds(r0, ATT_BLK), :] = _unrope(_from_qt(dqt_sc[n]), cs, sn).astype(BF16)
            dk_st[slot, pl.ds(r0, ATT_BLK), :] = _unrope(dk_sc[pl.ds(r0 + ATT_BLK, ATT_BLK), :], cs, sn).astype(BF16)
            dv_st[slot, pl.ds(r0, ATT_BLK), :] = dv_sc[pl.ds(r0 + ATT_BLK, ATT_BLK), :].astype(BF16)
            return carry

        lax.fori_loop(0, nb, post, 0)
        _staged_flush(step, n_steps, stages, dsts, o_sems, slot)

    n_in = 1 + len(saved) + 5
    return pl.pallas_call(
        body, name="attn_bwd", grid=(N_PAIRS, bsz),
        out_shape=(jax.ShapeDtypeStruct(dproj3.shape, dproj3.dtype), jax.ShapeDtypeStruct((N_PAIRS, 1, LANES), F32)),
        in_specs=[sp["smem"]] + sp["saved1"] + [sp["g1"], sp["act1"], sp["act1"], sp["tbl"], sp["tbl"], ANY],
        out_specs=(ANY, pl.BlockSpec((None, 1, LANES), lambda p, b: (p, 0, 0))),
        input_output_aliases={n_in: 0},
        scratch_shapes=[pltpu.VMEM((nb, LANES, QT_COLS), BF16),
                        pltpu.VMEM((nb, LANES, QT_COLS), F32),
                        pltpu.VMEM((s + ATT_BLK, LANES), F32),
                        pltpu.VMEM((s + ATT_BLK, LANES), F32),
                        pltpu.VMEM((2, s, QW), BF16), pltpu.VMEM((2, s, LANES), BF16),
                        pltpu.VMEM((2, s, LANES), BF16), pltpu.VMEM((2, s, QW), BF16),
                        pltpu.SemaphoreType.DMA((8,))],
        compiler_params=_params())(sinks, *saved, proj3, o3, dy3, cosf, sinf, dproj3)


def _staged_copies(stages, dsts, sems, slot):
    return [pltpu.make_async_copy(st.at[slot], dst, sems.at[slot * len(stages) + t])
            for t, (st, dst) in enumerate(zip(stages, dsts))]


def _staged_reuse(step, stages, dsts, sems, slot):
    @pl.when(step >= 2)
    def _():
        for cp in _staged_copies(stages, dsts, sems, slot):
            cp.wait()


def _staged_flush(step, n_steps, stages, dsts, sems, slot):
    for cp in _staged_copies(stages, dsts, sems, slot):
        cp.start()

    @pl.when(step == n_steps - 1)
    def _():
        for cp in _staged_copies(stages, dsts, sems, slot):
            cp.wait()
        if n_steps >= 2:
            for cp in _staged_copies(stages, dsts, sems, 1 - slot):
                cp.wait()


def _merge_fwd_bwd(x, tgt, y_rnn, y_attn, proj, w_r, w_a, w_o, gf):
    t, d = x.shape
    tm = min(t, 256)
    nt = t // tm

    hw = d // 2

    def body(x_ref, t_ref, yr_ref, ya_ref, mr0_ref, mr1_ref, ma0_ref, ma1_ref, wr_ref, wa_ref, wo_ref, gf_ref,
             dp_ref, dyr_ref, dya_ref, mg_ref, dx2_ref, dx2b_ref, dpr_ref, dpa_ref, loss_ref, dgf_ref, dmg_st, o_sems):
        i = pl.program_id(0)
        slot = i % 2
        dsts = [dp_ref.at[pl.ds(pl.multiple_of(i * tm, tm), tm), pl.ds(OFF_MERGE_R, 2 * d)]]
        _staged_reuse(i, [dmg_st], dsts, o_sems, slot)
        wr = wr_ref[...]
        wa = wa_ref[...]
        wo = wo_ref[...]
        gfv = gf_ref[...]
        pr = _dot(yr_ref[...], wr)
        pa = _dot(ya_ref[...], wa)
        sr = _sigmoid(jnp.concatenate([mr0_ref[...], mr1_ref[...]], axis=1))
        sa = _sigmoid(jnp.concatenate([ma0_ref[...], ma1_ref[...]], axis=1))
        mb = (sr * pr + sa * pa).astype(BF16)
        mg_ref[...] = mb
        x2 = x_ref[...] + _dot(mb, wo)
        r2 = lax.rsqrt(jnp.mean(x2 * x2, axis=-1, keepdims=True) + NORM_EPS)
        nrm = x2 * r2
        err = nrm * gfv - t_ref[...]
        dy = err * (1.0 / d)
        dn = dy * gfv
        dx2 = r2 * (dn - nrm * jnp.mean(dn * nrm, axis=-1, keepdims=True))
        dx2_ref[...] = dx2
        dx2b = dx2.astype(BF16)
        dx2b_ref[...] = dx2b
        dmerged = _dot_nt(dx2b, wo)
        dpr = (dmerged * sr).astype(BF16)
        dpa = (dmerged * sa).astype(BF16)
        dpr_ref[...] = dpr
        dpa_ref[...] = dpa
        dmg_st[slot, :, 0:d] = (dmerged * pr * (sr * (1.0 - sr))).astype(BF16)
        dmg_st[slot, :, d:2 * d] = (dmerged * pa * (sa * (1.0 - sa))).astype(BF16)
        _staged_flush(i, nt, [dmg_st], dsts, o_sems, slot)
        dyr_ref[...] = _dot_nt(dpr, wr)
        dya_ref[...] = _dot_nt(dpa, wa)

        @pl.when(i == 0)
        def _():
            loss_ref[...] = jnp.zeros_like(loss_ref)
            dgf_ref[...] = jnp.zeros_like(dgf_ref)

        loss_ref[...] += jnp.full((1, LANES), 0.5 / d, F32) * jnp.sum(err * err)
        dgf_ref[...] += jnp.sum(dy * nrm, axis=0, keepdims=True)

    tile = pl.BlockSpec((tm, d), lambda i: (i, 0))
    wsp = pl.BlockSpec((d, d), lambda i: (0, 0))

    def gate(col_blk):
        return pl.BlockSpec((tm, hw), lambda i: (i, col_blk))

    fb = jax.ShapeDtypeStruct((t, d), BF16)
    ff = jax.ShapeDtypeStruct((t, d), F32)
    return pl.pallas_call(
        body, name="merge_fwd_bwd", grid=(nt,),
        out_shape=(jax.ShapeDtypeStruct((t, D_IN), BF16), ff, ff, fb, ff, fb, fb, fb,
                   jax.ShapeDtypeStruct((1, LANES), F32), jax.ShapeDtypeStruct((1, d), F32)),
        in_specs=[tile, tile, tile, tile] + [gate(OFF_MERGE_R // hw + j) for j in range(4)] + [
            wsp, wsp, wsp, pl.BlockSpec((1, d), lambda i: (0, 0))],
        out_specs=(ANY, tile, tile, tile, tile, tile, tile, tile,
                   pl.BlockSpec((1, LANES), lambda i: (0, 0)), pl.BlockSpec((1, d), lambda i: (0, 0))),
        scratch_shapes=[pltpu.VMEM((2, tm, 2 * d), BF16), pltpu.SemaphoreType.DMA((2,))],
        compiler_params=_params())(x, tgt, y_rnn, y_attn, proj, proj, proj, proj, w_r, w_a, w_o, gf)


def _local_grads(x, tgt, h, proj, norm_g, w_in_bm, conv_w, conv_b, lru_w_a, lru_b_a, lru_w_x, lru_b_x, lam, sinks,
                 row_sharded, gf):
    bsz, s, d = x.shape
    t = bsz * s
    x2 = x.reshape(t, d)
    proj3 = proj.reshape(bsz, s, D_IN)
    h_lru, y_rnn, gathered = _lru_fwd(proj3, conv_w, conv_b, lru_w_a, lru_b_a, lru_w_x, lru_b_x, lam, row_sharded)
    w_r, w_a, w_o = (g.reshape(d, d) for g in gathered)
    cosf, sinf = _rope_tables(s)
    o_attn, y_attn, attn_saved = _attn_fwd(proj3, sinks, cosf, sinf)
    y_rnn2 = y_rnn.reshape(t, d)
    y_attn2 = y_attn.reshape(t, d)
    dproj, dyr, dya, merged, dx2, dx2b, dpr, dpa, loss, dgf = _merge_fwd_bwd(
        x2, tgt.reshape(t, d), y_rnn2, y_attn2, proj, w_r, w_a, w_o, gf)
    dproj3, dsink = _attn_bwd(proj3, attn_saved, o_attn, dya.reshape(bsz, s, d), dproj.reshape(bsz, s, D_IN),
                              sinks, cosf, sinf)
    dproj3, dcw, dcb, dwa, dba, dwx, dbx, dlam = _lru_bwd(
        proj3, h_lru, dyr.reshape(bsz, s, d), dproj3, conv_w, conv_b, lru_w_a, lru_b_a, lru_w_x, lru_b_x, lam)
    dproj = dproj3.reshape(t, D_IN)
    grad_x, dng = _grad_x(dproj, w_in_bm, x2, dx2, norm_g)
    small = dict(norm_g=dng, conv_w=dcw, conv_b=dcb, lru_w_a=dwa, lru_b_a=dba, lru_w_x=dwx, lru_b_x=dbx,
                 lru_lambda=dlam, attn_sinks=dsink[:, 0, :HEADS_PER_STEP].reshape(1, N_Q_HEADS), final_norm_g=dgf)
    squares = [(y_rnn2, dpr), (y_attn2, dpa), (merged, dx2b)]
    return loss[0, 0], grad_x.reshape(bsz, s, d), h, dproj, squares, small


ANY = pl.BlockSpec(memory_space=pl.ANY)


def _mesh_pos():
    return lax.axis_index("x"), lax.axis_index("y"), lax.axis_index("c")


def _remote(src, dst, send_sems, recv_sems, idx, peer):
    return pltpu.make_async_remote_copy(src_ref=src, dst_ref=dst, send_sem=send_sems.at[idx],
                                        recv_sem=recv_sems.at[idx], device_id=peer, device_id_type=MESH)


def _row_gather(ins, outs, send_sems, recv_sems, fsend_sems, frecv_sems):
    n = len(ins)
    x, y, c = _mesh_pos()
    me = 2 * x + y
    sib = (x, y, 1 - c)
    peers = [((x, 1 - y, c), me ^ 1), ((1 - x, y, c), me ^ 2), ((1 - x, 1 - y, c), me ^ 3)]

    def half(ref, slot, t, which):
        hr = ins[t].shape[1] // 2
        return ref.at[slot, pl.ds(pl.multiple_of(which * hr, 8), hr), :]

    def ici(t, k):
        peer, pj = peers[k]
        src = half(ins[t], me, t, c)
        return (_remote(src, half(outs[t], me, t, c), send_sems, recv_sems, 3 * t + k, peer),
                _remote(src, half(outs[t], pj, t, c), send_sems, recv_sems, 3 * t + k, peer))

    def forward(t, k):
        got = half(outs[t], peers[k][1], t, c)
        return (_remote(got, got, fsend_sems, frecv_sems, 3 * t + k, sib),
                _remote(got, half(outs[t], peers[k][1], t, 1 - c), fsend_sems, frecv_sems, 3 * t + k, sib))

    pairs = [(t, k) for t in range(n) for k in range(3)]

    def start():
        for t, k in pairs:
            ici(t, k)[0].start()

    def finish():
        for t, k in pairs:
            ici(t, k)[1].wait_recv()
            forward(t, k)[0].start()
        for t, k in pairs:
            ici(t, k)[0].wait_send()
            forward(t, k)[0].wait_send()
            forward(t, k)[1].wait_recv()

    return start, finish


def _gather_in_proj(x, g, bufs, split, idx):
    t_tok, d = x.shape
    n = len(bufs)
    tm = min(t_tok, 512)
    nt = t_tok // tm
    n_fwd = 3 * sum(split)
    assert split[0]

    def body(idx_ref, x_ref, g_ref, *refs):
        ins, proj_ref, h_out, outs = refs[:n], refs[n], refs[n + 1], refs[n + 2:2 * n + 2]
        wbuf, h_all, send_sems, recv_sems, fsend_sems, frecv_sems, l_sems = refs[2 * n + 2:]
        j, i = pl.program_id(0), pl.program_id(1)
        rows = pl.ds(pl.multiple_of(i * tm, tm), tm)
        x, y, c = _mesh_pos()
        me = 2 * x + y
        sib = (x, y, 1 - c)
        peers = [((x, 1 - y, c), me ^ 1), ((1 - x, y, c), me ^ 2), ((1 - x, 1 - y, c), me ^ 3)]

        def part(ref, slot, t, half):
            if not split[t]:
                return ref.at[slot]
            hr = bufs[t].shape[1] // 2
            return ref.at[slot, pl.ds(pl.multiple_of(half * hr, 8), hr), :]

        def land(t):
            return wbuf if t == 0 else outs[t]

        def ici(t, k):
            peer, pj = peers[k]
            src = part(ins[t], me, t, c)
            return (_remote(src, part(land(t), me, t, c), send_sems, recv_sems, 3 * t + k, peer),
                    _remote(src, part(land(t), pj, t, c), send_sems, recv_sems, 3 * t + k, peer))

        fwd_index = {}
        for t in range(n):
            if split[t]:
                for k in range(3):
                    fwd_index[(t, k)] = len(fwd_index)

        def forward(t, k):
            pj = peers[k][1]
            got = part(land(t), pj, t, c)
            f = fwd_index[(t, k)]
            return (_remote(got, got, fsend_sems, frecv_sems, f, sib),
                    _remote(got, part(land(t), pj, t, 1 - c), fsend_sems, frecv_sems, f, sib))

        def write_back(k):
            pj = peers[k][1]
            return pltpu.make_async_copy(wbuf.at[pj], outs[0].at[pj], l_sems.at[1 + k])

        relay_peer = ((x + c) % 2, (y + 1 - c) % 2, c)

        def relay():
            got = part(wbuf, me ^ (2 - c), 0, c)
            return (_remote(got, got, send_sems, recv_sems, 2, relay_peer),
                    _remote(got, part(wbuf, me ^ 3, 0, c), send_sems, recv_sems, 2, relay_peer))

        direct = [(t, k) for t in range(n) for k in range(3) if (t, k) != (0, 2)]

        @pl.when((j == 0) & (i == 0))
        def _():
            for t, k in direct:
                ici(t, k)[0].start()
            own = pltpu.make_async_copy(ins[0].at[me], wbuf.at[me], l_sems.at[0])
            own.start()
            own.wait()

        @pl.when((j == 1) & (i == 0))
        def _():
            pltpu.make_async_copy(h_all, h_out, l_sems.at[4]).start()
            for k in range(2):
                ici(0, k)[1].wait_recv()
            relay()[0].start()
            for k in range(2):
                forward(0, k)[0].start()
            forward(0, 0)[1].wait_recv()
            write_back(0).start()

        @pl.when((j == 2) & (i == 0))
        def _():
            forward(0, 1)[1].wait_recv()
            write_back(1).start()

        @pl.when((j == 3) & (i == 0))
        def _():
            relay()[1].wait_recv()
            forward(0, 2)[0].start()
            forward(0, 2)[1].wait_recv()
            write_back(2).start()

        @pl.when(j == 0)
        def _():
            xv = x_ref[...]
            r = lax.rsqrt(jnp.mean(xv * xv, axis=-1, keepdims=True) + NORM_EPS)
            h_all[rows, :] = (xv * r * g_ref[...]).astype(BF16)

        proj_ref[...] = _dot(h_all[rows, :], wbuf[me ^ j])

        @pl.when((j == N_CHIPS - 1) & (i == nt - 1))
        def _():
            pltpu.make_async_copy(h_all, h_out, l_sems.at[4]).wait()
            for t in range(1, n):
                for k in range(3):
                    ici(t, k)[1].wait_recv()
                    if split[t]:
                        forward(t, k)[0].start()
            relay()[0].wait_send()
            for t, k in direct:
                ici(t, k)[0].wait_send()
            for t in range(n):
                if split[t]:
                    for k in range(3):
                        forward(t, k)[0].wait_send()
                        if t > 0:
                            forward(t, k)[1].wait_recv()
            for k in range(3):
                write_back(k).wait()

    grid_spec = pltpu.PrefetchScalarGridSpec(
        num_scalar_prefetch=1, grid=(N_CHIPS, nt),
        in_specs=[pl.BlockSpec((tm, d), lambda j, i, idx_ref: (jnp.where(j == 0, i, nt - 1), 0)),
                  pl.BlockSpec((1, d), lambda j, i, idx_ref: (0, 0))] + [ANY] * n,
        out_specs=[pl.BlockSpec((tm, W_BLK), lambda j, i, idx_ref: (i, idx_ref[0] ^ j)), ANY] + [ANY] * n,
        scratch_shapes=[pltpu.VMEM(bufs[0].shape, bufs[0].dtype), pltpu.VMEM((t_tok, d), BF16),
                        pltpu.SemaphoreType.DMA((3 * n,)), pltpu.SemaphoreType.DMA((3 * n,)),
                        pltpu.SemaphoreType.DMA((n_fwd,)), pltpu.SemaphoreType.DMA((n_fwd,)),
                        pltpu.SemaphoreType.DMA((5,))])
    out_shape = [jax.ShapeDtypeStruct((t_tok, D_IN), F32), jax.ShapeDtypeStruct((t_tok, d), BF16)] + [
        jax.ShapeDtypeStruct(a.shape, a.dtype) for a in bufs]
    res = pl.pallas_call(
        body, name="gather_in_proj", grid_spec=grid_spec, out_shape=out_shape,
        input_output_aliases={3 + t: 2 + t for t in range(n)}, compiler_params=_params())(idx, x, g, *bufs)
    return res[1], res[0], res[2:]


def _row_tile(rows, row_bytes, cap_bytes=2 * 1024 * 1024):
    best = None
    for tr in range(8, rows + 1, 8):
        if rows % tr == 0 and tr * row_bytes <= cap_bytes:
            best = tr
    return best if best is not None else rows


XOR_ORDER = (3, 2, 1)


def _grads_reduce_scatter(h, dproj, squares, small, idx):
    t, d = h.shape
    nsq = len(squares)
    hr = d // 2
    qr = ROW_BLK // 2
    tk = min(t, 1024)
    nk = t // tk
    last = N_CHIPS - 1
    n_phase = 3

    def dest(s, idx_ref):
        xo = jnp.where(s == 0, XOR_ORDER[0], jnp.where(s == 1, XOR_ORDER[1], jnp.where(s == 2, XOR_ORDER[2], 0)))
        return idx_ref[0] ^ xo

    def k_sq(p, k):
        return jnp.where(p == 0, k, nk - 1)

    def k_w(p, k):
        return jnp.where(p == 0, 0, k)

    in_specs = [
        pl.BlockSpec((tk, hr), lambda s, p, k, idx_ref: (k_w(p, k), (1 - idx_ref[1] + jnp.maximum(p - 1, 0)) % 2)),
        pl.BlockSpec((tk, W_BLK), lambda s, p, k, idx_ref: (k_w(p, k), dest(s, idx_ref)))]
    for q in range(nsq):
        in_specs.append(pl.BlockSpec((tk, ROW_BLK), lambda s, p, k, idx_ref: (k_sq(p, k), dest(s, idx_ref))))
        in_specs.append(pl.BlockSpec((tk, d), lambda s, p, k, idx_ref: (k_sq(p, k), 0)))

    def body(idx_ref, *refs):
        nj = 1 + nsq
        h_ref, dp_ref = refs[0], refs[1]
        sq_in = refs[2:2 + 2 * nsq]
        small_in = refs[2 * nj]
        outs = refs[2 * nj + 1:3 * nj + 2]
        landing = refs[3 * nj + 2:4 * nj + 3]
        sc = refs[4 * nj + 3:]
        acc_w, xr_w, sb_w = sc[0:3]
        sq_sc = [sc[3 + 3 * q:6 + 3 * q] for q in range(nsq)]
        sm, smx = sc[3 * nj:3 * nj + 2]
        x_send, x_recv, i_send, i_recv, f_send, f_recv, o_sem, l_sem = sc[3 * nj + 2:]
        s, p, k = pl.program_id(0), pl.program_id(1), pl.program_id(2)
        x, y, c = _mesh_pos()
        sib = (x, y, 1 - c)
        peers = [((1 - x) if xo & 2 else x, (1 - y) if xo & 1 else y, c) for xo in XOR_ORDER]
        slot = s % 2
        mine_w = pl.ds(pl.multiple_of(c * hr, 8), hr)
        theirs_w = pl.ds(pl.multiple_of((1 - c) * hr, 8), hr)
        mine_q = pl.ds(pl.multiple_of(c * qr, 8), qr)
        theirs_q = pl.ds(pl.multiple_of((1 - c) * qr, 8), qr)

        def exch(j, src, dst):
            return _remote(src, dst, x_send, x_recv, 2 * j + slot, sib)

        sbufs = [sb_w] + [sq_sc[q][2] for q in range(nsq)]

        def ici(j, ss):
            return _remote(sbufs[j].at[ss], landing[j].at[ss], i_send, i_recv, last * j + ss, peers[ss])

        def exchanges():
            cps = [exch(0, acc_w.at[0], xr_w.at[slot])]
            cps += [exch(1 + q, sq_sc[q][0].at[theirs_q, :], sq_sc[q][1].at[slot]) for q in range(nsq)]
            return cps

        def small_send(ss):
            return _remote(sm.at[c], landing[nj].at[ss], i_send, i_recv, last * nj + ss, peers[ss])

        def small_start():
            load = pltpu.make_async_copy(small_in, sm, l_sem.at[nj + 1])
            load.start()
            load.wait()
            swap = _remote(sm, smx.at[pl.ds(0, 2)], x_send, x_recv, 2 * nj, sib)
            swap.start()
            swap.wait_recv()
            swap.wait_send()
            sm[...] = sm[...] + smx[0:2]
            for ss in range(last):
                small_send(ss).start()

        def pair_ref(j):
            return acc_w.at[1] if j == 0 else sq_sc[j - 1][0].at[mine_q, :]

        def sq_phase():
            pl.when((s == 0) & (k == 0))(small_start)
            for q in range(nsq):
                acc = sq_sc[q][0]

                @pl.when(k == 0)
                def _():
                    acc[...] = jnp.zeros((ROW_BLK, d), F32)

                acc[...] += _dot_tn(sq_in[2 * q][...], sq_in[2 * q + 1][...])

            @pl.when(k == nk - 1)
            def _():
                for cp in exchanges()[1:]:
                    cp.start()

        def w_phase(hf):
            @pl.when(k == 0)
            def _():
                acc_w[hf] = jnp.zeros((hr, W_BLK), F32)

            acc_w[hf] += _dot_tn(h_ref[...], dp_ref[...])

            @pl.when(k == nk - 1)
            def _():
                if hf == 0:
                    exchanges()[0].start()
                else:
                    finish_step()

        def finish_step():
            for cp in exchanges():
                cp.wait_recv()
                cp.wait_send()
            acc_w[1] += xr_w[slot]
            for q in range(nsq):
                sq_sc[q][0][mine_q, :] += sq_sc[q][1][slot]
            for ss in range(last):
                @pl.when(s == ss)
                def _():
                    for j in range(nj):
                        sbufs[j][ss] = pair_ref(j)[...].astype(BF16)
                        ici(j, ss).start()

            @pl.when(s == last)
            def _():
                for ss in range(last):
                    for j in range(nj):
                        ici(j, ss).wait_recv()
                        ici(j, ss).wait_send()
                    small_send(ss).wait_recv()
                    small_send(ss).wait_send()
                stage = [pltpu.make_async_copy(landing[j], sbufs[j], l_sem.at[j]) for j in range(nj)]
                stage.append(pltpu.make_async_copy(landing[nj], smx, l_sem.at[nj]))
                for cp in stage:
                    cp.start()
                for j in range(nj):
                    stage[j].wait()
                    total = pair_ref(j)[...]
                    for ss in range(last):
                        total = total + sbufs[j][ss].astype(F32)
                    pair_ref(j)[...] = total
                stage[nj].wait()
                by_xor = {xo: smx[ss] for ss, xo in enumerate(XOR_ORDER)}
                sm[c] = (sm[c] + by_xor[1]) + (by_xor[2] + by_xor[3])
                done = [(acc_w.at[1], outs[0].at[mine_w, :], outs[0].at[theirs_w, :])]
                done += [(pair_ref(1 + q), outs[1 + q].at[mine_q, :], outs[1 + q].at[theirs_q, :]) for q in range(nsq)]
                done.append((sm.at[c], outs[nj].at[c], outs[nj].at[1 - c]))
                copies = []
                for j, (src, mine, theirs) in enumerate(done):
                    keep = pltpu.make_async_copy(src, mine, o_sem.at[j])
                    give = _remote(src, mine, f_send, f_recv, j, sib)
                    take = _remote(src, theirs, f_send, f_recv, j, sib)
                    keep.start()
                    give.start()
                    copies.append((keep, give, take))
                for keep, give, take in copies:
                    keep.wait()
                    give.wait_send()
                    take.wait_recv()

        pl.when(p == 0)(sq_phase)
        for hf in range(2):
            pl.when(p == 1 + hf)(functools.partial(w_phase, hf))

    nj = 1 + nsq
    scratch = [pltpu.VMEM((2, hr, W_BLK), F32), pltpu.VMEM((2, hr, W_BLK), F32), pltpu.VMEM((last, hr, W_BLK), BF16)]
    for _ in range(nsq):
        scratch += [pltpu.VMEM((ROW_BLK, d), F32), pltpu.VMEM((2, qr, d), F32), pltpu.VMEM((last, qr, d), BF16)]
    scratch += [pltpu.VMEM((2, PK_HALF, LANES), F32), pltpu.VMEM((last, PK_HALF, LANES), F32)]
    scratch += [pltpu.SemaphoreType.DMA((2 * nj + 1,)), pltpu.SemaphoreType.DMA((2 * nj + 1,)),
                pltpu.SemaphoreType.DMA((last * (nj + 1),)), pltpu.SemaphoreType.DMA((last * (nj + 1),)),
                pltpu.SemaphoreType.DMA((nj + 1,)), pltpu.SemaphoreType.DMA((nj + 1,)),
                pltpu.SemaphoreType.DMA((nj + 1,)), pltpu.SemaphoreType.DMA((nj + 2,))]
    grid_spec = pltpu.PrefetchScalarGridSpec(
        num_scalar_prefetch=1, grid=(N_CHIPS, n_phase, nk), in_specs=in_specs + [ANY],
        out_specs=[ANY] * (2 * nj + 2), scratch_shapes=scratch)
    out_shape = [jax.ShapeDtypeStruct((d, W_BLK), F32)] + [jax.ShapeDtypeStruct((ROW_BLK, d), F32)] * nsq
    out_shape.append(jax.ShapeDtypeStruct((2, PK_HALF, LANES), F32))
    out_shape += [jax.ShapeDtypeStruct((last, hr, W_BLK), BF16)] + [jax.ShapeDtypeStruct((last, qr, d), BF16)] * nsq
    out_shape.append(jax.ShapeDtypeStruct((last, PK_HALF, LANES), F32))
    flat = [a for pair in squares for a in pair]
    res = pl.pallas_call(body, name="grads_reduce_scatter", grid_spec=grid_spec, out_shape=out_shape,
                         compiler_params=_params())(idx, h, dproj, *flat, small)
    return res[:nj + 1]


_VEC_NAMES = ("norm_g", "conv_b", "lru_b_a", "lru_b_x", "lru_lambda", "final_norm_g")


def _pack_small(p, conv_full=None, scalar=None):
    rows = [p["lru_w_a"].reshape(PK_WX - PK_WA, LANES), p["lru_w_x"].reshape(PK_VEC - PK_WX, LANES)]
    rows += [p[k].reshape(8, LANES) for k in _VEC_NAMES]
    rows.append(jnp.pad(p["attn_sinks"].reshape(1, N_Q_HEADS), ((0, 7), (0, LANES - N_Q_HEADS))))
    rows.append(jnp.zeros((32, LANES), F32) if conv_full is None else conv_full.reshape(32, LANES))
    tail = PK_ROWS - PK_SCALAR
    if scalar is None:
        rows.append(jnp.zeros((tail, LANES), F32))
    else:
        rows.append(jnp.pad(scalar.reshape(1, 1), ((0, tail - 1), (0, LANES - 1))))
    return jnp.concatenate(rows, axis=0)


def _unpack_small(pk, like):
    out = {"lru_w_a": pk[PK_WA:PK_WX].reshape(like["lru_w_a"].shape),
           "lru_w_x": pk[PK_WX:PK_VEC].reshape(like["lru_w_x"].shape)}
    for j, k in enumerate(_VEC_NAMES):
        out[k] = pk[PK_VEC + 8 * j:PK_VEC + 8 * j + 8].reshape(like[k].shape)
    out["attn_sinks"] = pk[PK_SINK:PK_SINK + 1, :N_Q_HEADS].reshape(like["attn_sinks"].shape)
    return out


_WEIGHTS = ("norm_g", "w_in", "conv_w", "conv_b", "lru_w_a", "lru_b_a", "lru_w_x", "lru_b_x", "lru_lambda",
            "attn_sinks", "w_rnn_out", "w_attn_out", "w_o", "final_norm_g")
_SMALL = ("norm_g", "conv_b", "lru_w_a", "lru_b_a", "lru_w_x", "lru_b_x", "lru_lambda", "attn_sinks", "final_norm_g")
_ROW_SHARDED = ("w_rnn_out", "w_attn_out", "w_o")


def kernel(x, norm_g, w_in, conv_w, conv_b, lru_w_a, lru_b_a, lru_w_x, lru_b_x, lru_lambda, attn_sinks, w_rnn_out, w_attn_out, w_o, final_norm_g, loss_target, m_norm_g, m_w_in, m_conv_w, m_conv_b, m_lru_w_a, m_lru_b_a, m_lru_w_x, m_lru_b_x, m_lru_lambda, m_attn_sinks, m_w_rnn_out, m_w_attn_out, m_w_o, m_final_norm_g, v_norm_g, v_w_in, v_conv_w, v_conv_b, v_lru_w_a, v_lru_b_a, v_lru_w_x, v_lru_b_x, v_lru_lambda, v_attn_sinks, v_w_rnn_out, v_w_attn_out, v_w_o, v_final_norm_g):
    w = dict(norm_g=norm_g, w_in=w_in, conv_w=conv_w, conv_b=conv_b, lru_w_a=lru_w_a, lru_b_a=lru_b_a, lru_w_x=lru_w_x,
             lru_b_x=lru_b_x, lru_lambda=lru_lambda, attn_sinks=attn_sinks, w_rnn_out=w_rnn_out, w_attn_out=w_attn_out,
             w_o=w_o, final_norm_g=final_norm_g)
    m = dict(norm_g=m_norm_g, w_in=m_w_in, conv_w=m_conv_w, conv_b=m_conv_b, lru_w_a=m_lru_w_a, lru_b_a=m_lru_b_a,
             lru_w_x=m_lru_w_x, lru_b_x=m_lru_b_x, lru_lambda=m_lru_lambda, attn_sinks=m_attn_sinks,
             w_rnn_out=m_w_rnn_out, w_attn_out=m_w_attn_out, w_o=m_w_o, final_norm_g=m_final_norm_g)
    v = dict(norm_g=v_norm_g, w_in=v_w_in, conv_w=v_conv_w, conv_b=v_conv_b, lru_w_a=v_lru_w_a, lru_b_a=v_lru_b_a,
             lru_w_x=v_lru_w_x, lru_b_x=v_lru_b_x, lru_lambda=v_lru_lambda, attn_sinks=v_attn_sinks,
             w_rnn_out=v_w_rnn_out, w_attn_out=v_w_attn_out, w_o=v_w_o, final_norm_g=v_final_norm_g)
    mx, my, mc = _mesh_pos()
    me = 2 * mx + my
    d = D_MODEL

    slot0 = jnp.stack([me, jnp.zeros_like(me)]).astype(jnp.int32)
    buf_in = _put_slot(w["w_in"][0], N_CHIPS, slot0, d, BF16, "cast_w_in")
    buf_cw = _put_slot(w["conv_w"][0], N_CHIPS, slot0, CONV_WIDTH, F32, "slot_conv_w")
    row_sharded = [_put_slot(w[k][0], N_CHIPS, slot0, ROW_BLK, BF16, "cast_" + k) for k in _ROW_SHARDED]
    h, proj, (g_in, g_cw) = _gather_in_proj(x.reshape(-1, d), w["norm_g"], [buf_in, buf_cw], [True, False],
                                            jnp.reshape(me, (1,)).astype(jnp.int32))
    conv_full = g_cw.transpose(1, 0, 2).reshape(CONV_WIDTH, D_RNN)

    loss_local, grad_x, h, dproj, squares, gsmall = _local_grads(
        x, loss_target, h, proj, w["norm_g"], g_in, conv_full, w["conv_b"], w["lru_w_a"][0], w["lru_b_a"], w["lru_w_x"][0],
        w["lru_b_x"], w["lru_lambda"], w["attn_sinks"][0], row_sharded, w["final_norm_g"].reshape(1, d))
    gpack = _pack_small(gsmall, gsmall["conv_w"], loss_local).reshape(2, PK_HALF, LANES)
    f_in, f_r, f_a, f_o, spack = _grads_reduce_scatter(h, dproj, squares, gpack, jnp.stack([me, mc]).astype(jnp.int32))
    spack = spack.reshape(PK_ROWS, LANES)
    loss = spack[PK_SCALAR, 0]

    grads = _unpack_small(spack, w)
    conv_all = spack[PK_CONV:PK_CONV + 32].reshape(CONV_WIDTH, D_RNN)
    grads["conv_w"] = lax.dynamic_slice_in_dim(conv_all, me * (D_RNN // N_CHIPS), D_RNN // N_CHIPS, axis=1)[None]
    grads["w_in"] = f_in[None]
    grads["w_rnn_out"], grads["w_attn_out"], grads["w_o"] = f_r[None], f_a[None], f_o[None]

    delta, new_m, new_v = {}, {}, {}
    for k in ("w_in",) + _ROW_SHARDED:
        dk, mk, vk = _adamw(w[k][0], grads[k][0], m[k][0], v[k][0], "adamw_" + k)
        delta[k], new_m[k], new_v[k] = dk[None], mk[None], vk[None]
    shp = (2 * CONV_WIDTH, LANES)
    dk, mk, vk = _adamw(w["conv_w"].reshape(shp), grads["conv_w"].reshape(shp), m["conv_w"].reshape(shp),
                        v["conv_w"].reshape(shp), "adamw_conv_w")
    delta["conv_w"], new_m["conv_w"], new_v["conv_w"] = (a.reshape(w["conv_w"].shape) for a in (dk, mk, vk))
    dk, mk, vk = _adamw(_pack_small(w), spack, _pack_small(m), _pack_small(v), "adamw_small")
    for src, dst in ((dk, delta), (mk, new_m), (vk, new_v)):
        dst.update(_unpack_small(src, w))

    return (loss, grad_x, *[grads[k] for k in _WEIGHTS], *[delta[k] for k in _WEIGHTS],
            *[new_m[k] for k in _WEIGHTS], *[new_v[k] for k in _WEIGHTS])
```

```python
import functools
import math

import jax
import jax.numpy as jnp
from jax import lax
from jax.experimental import pallas as pl
from jax.experimental.pallas import tpu as pltpu

F32 = jnp.float32
BF16 = jnp.bfloat16
MESH = pl.DeviceIdType.MESH

D_MODEL = 1024
D_RNN = 1024
N_RNN_BLOCKS = 8
RNN_BLOCK = D_RNN // N_RNN_BLOCKS
CONV_WIDTH = 4
LRU_C = 8.0
HEAD_DIM = 64
N_Q_HEADS = 16
N_KV_HEADS = 4
D_ATTN = N_Q_HEADS * HEAD_DIM
D_KV = N_KV_HEADS * HEAD_DIM
WINDOW = 128
ROPE_DIM = HEAD_DIM // 4
ROPE_THETA = 500000.0
NORM_EPS = 1e-6
OFF_RNN_X = 0
OFF_RNN_G = OFF_RNN_X + D_RNN
OFF_Q = OFF_RNN_G + D_RNN
OFF_K = OFF_Q + D_ATTN
OFF_V = OFF_K + D_KV
OFF_ATTN_G = OFF_V + D_KV
OFF_MERGE_R = OFF_ATTN_G + D_ATTN
OFF_MERGE_A = OFF_MERGE_R + D_MODEL
D_IN = OFF_MERGE_A + D_MODEL

ADAM_LR = 0.001
ADAM_B1 = 0.9
ADAM_B2 = 0.999
ADAM_EPS = 1e-08
ADAM_WD = 0.01
ADAM_STEP = 10

N_CHIPS = 4
W_BLK = D_IN // N_CHIPS
ROW_BLK = D_MODEL // N_CHIPS
LANES = 128
ATT_BLK = 128
VMEM_LIMIT = 56 * 1024 * 1024
NEG_BIG = -1e30
ATTN_SCALE = 1.0 / math.sqrt(HEAD_DIM)

PK_WA = 0
PK_WX = PK_WA + N_RNN_BLOCKS * RNN_BLOCK
PK_VEC = PK_WX + N_RNN_BLOCKS * RNN_BLOCK
PK_SINK = PK_VEC + 6 * 8
PK_CONV = PK_SINK + 8
PK_SCALAR = PK_CONV + 32
PK_ROWS = PK_SCALAR + 8
PK_HALF = PK_ROWS // 2


def _params(**kw):
    return pltpu.CompilerParams(vmem_limit_bytes=VMEM_LIMIT, **kw)


def _sigmoid(z):
    return 1.0 / (1.0 + jnp.exp(-z))


def _dot(a, b):
    return jnp.dot(a, b, preferred_element_type=F32)


def _dot_nt(a, b):
    return lax.dot_general(a, b, (((1,), (1,)), ((), ())), preferred_element_type=F32)


def _dot_tn(a, b):
    return lax.dot_general(a, b, (((0,), (0,)), ((), ())), preferred_element_type=F32)


def _put_slots(srcs, slot, dtype, name):
    rows, c = srcs[0].shape
    n = len(srcs)
    tr = _row_tile(rows, c * 4)

    def body(idx_ref, *refs):
        for s_ref, o_ref in zip(refs[:n], refs[n:]):
            o_ref[...] = s_ref[...].astype(dtype)

    grid_spec = pltpu.PrefetchScalarGridSpec(
        num_scalar_prefetch=1, grid=(rows // tr,),
        in_specs=[pl.BlockSpec((tr, c), lambda i, idx_ref: (i, 0))] * n,
        out_specs=[pl.BlockSpec((None, tr, c), lambda i, idx_ref: (idx_ref[0], i, 0))] * n)
    return pl.pallas_call(body, name=name, grid_spec=grid_spec,
                          out_shape=[jax.ShapeDtypeStruct((N_CHIPS, rows, c), dtype)] * n,
                          compiler_params=_params())(slot, *srcs)


def _adamw(w, g, m, v, name):
    r, c = w.shape
    tr = _row_tile(r, c * 4, 1024 * 1024)
    c1 = 1.0 - ADAM_B1 ** ADAM_STEP
    c2 = 1.0 - ADAM_B2 ** ADAM_STEP

    def body(w_ref, g_ref, m_ref, v_ref, d_ref, nm_ref, nv_ref):
        gv = g_ref[...]
        nm = ADAM_B1 * m_ref[...] + (1.0 - ADAM_B1) * gv
        nv = ADAM_B2 * v_ref[...] + (1.0 - ADAM_B2) * (gv * gv)
        m_hat = nm / c1
        v_hat = nv / c2
        d_ref[...] = -ADAM_LR * (m_hat / (jnp.sqrt(v_hat) + ADAM_EPS) + ADAM_WD * w_ref[...])
        nm_ref[...] = nm
        nv_ref[...] = nv

    spec = pl.BlockSpec((tr, c), lambda i: (i, 0))
    sds = jax.ShapeDtypeStruct((r, c), F32)
    return pl.pallas_call(
        body, name=name, grid=(r // tr,), out_shape=(sds, sds, sds),
        in_specs=[spec, spec, spec, spec], out_specs=(spec, spec, spec), compiler_params=_params())(w, g, m, v)


def _grad_x(dproj, w_bm, x, dx2, g):
    t = dproj.shape[0]
    nb, d, wb = w_bm.shape
    tm = min(t, 512)

    def body(dp_ref, w_ref, x_ref, dx2_ref, g_ref, gx_ref, dg_ref, acc_ref):
        i, k = pl.program_id(0), pl.program_id(1)

        @pl.when(k == 0)
        def _():
            acc_ref[...] = jnp.zeros_like(acc_ref)

        acc_ref[...] += _dot_nt(dp_ref[...], w_ref[...])

        @pl.when((i == 0) & (k == 0))
        def _():
            dg_ref[...] = jnp.zeros_like(dg_ref)

        @pl.when(k == nb - 1)
        def _():
            dhv = acc_ref[...]
            xv = x_ref[...]
            r = lax.rsqrt(jnp.mean(xv * xv, axis=-1, keepdims=True) + NORM_EPS)
            nrm = xv * r
            dn = dhv * g_ref[...]
            gx_ref[...] = dx2_ref[...] + r * (dn - nrm * jnp.mean(dn * nrm, axis=-1, keepdims=True))
            dg_ref[...] += jnp.sum(dhv * nrm, axis=0, keepdims=True)

    tile = pl.BlockSpec((tm, d), lambda i, k: (i, 0))
    vec = pl.BlockSpec((1, d), lambda i, k: (0, 0))
    return pl.pallas_call(
        body, name="grad_x", grid=(t // tm, nb),
        out_shape=(jax.ShapeDtypeStruct((t, d), F32), jax.ShapeDtypeStruct((1, d), F32)),
        in_specs=[pl.BlockSpec((tm, wb), lambda i, k: (i, k)), pl.BlockSpec((None, d, wb), lambda i, k: (k, 0, 0)),
                  tile, tile, vec],
        out_specs=(tile, vec), scratch_shapes=[pltpu.VMEM((tm, d), F32)], compiler_params=_params())(dproj, w_bm, x, dx2, g)


def _shift_down(v, d, fill):
    n = v.shape[0]
    if d % 8 == 0:
        return jnp.concatenate([jnp.full((d,) + v.shape[1:], fill, v.dtype), v[: n - d]], axis=0)
    row = lax.broadcasted_iota(jnp.int32, v.shape, 0)
    return jnp.where(row >= d, pltpu.roll(v, d, axis=0), fill)


def _shift_up(v, d, fill):
    n = v.shape[0]
    if d % 8 == 0:
        return jnp.concatenate([v[d:], jnp.full((d,) + v.shape[1:], fill, v.dtype)], axis=0)
    row = lax.broadcasted_iota(jnp.int32, v.shape, 0)
    return jnp.where(row < n - d, pltpu.roll(v, n - d, axis=0), fill)


def _scan_log(a, b, shift):
    n = a.shape[0]
    d = 1
    while d < n:
        b = a * shift(b, d, 0.0) + b
        if 2 * d < n:
            a = a * shift(a, d, 1.0)
        d *= 2
    return b


SUBLANES = 8


def _scan(a, b, sa_ref, sb_ref, reverse):
    n, c = a.shape
    g = n // SUBLANES
    a3, b3 = a.reshape(g, SUBLANES, c), b.reshape(g, SUBLANES, c)
    sub = lax.broadcasted_iota(jnp.int32, a3.shape, 1)
    d = 1
    while d < SUBLANES:
        keep = (sub < SUBLANES - d) if reverse else (sub >= d)
        amount = SUBLANES - d if reverse else d
        b3 = a3 * jnp.where(keep, pltpu.roll(b3, amount, axis=1), 0.0) + b3
        a3 = a3 * jnp.where(keep, pltpu.roll(a3, amount, axis=1), 1.0)
        d *= 2
    sa_ref[...] = a3.reshape(n, c)
    sb_ref[...] = b3.reshape(n, c)
    edge = 0 if reverse else SUBLANES - 1
    shift = _shift_up if reverse else _shift_down
    totals = _scan_log(sa_ref[pl.ds(edge, g, stride=SUBLANES), :], sb_ref[pl.ds(edge, g, stride=SUBLANES), :], shift)
    carry = shift(totals, 1, 0.0)
    return (a3 * carry[:, None, :] + b3).reshape(n, c)


def _neg_expm1_twice(log_a, a):
    return -jnp.tanh(log_a) * (a * a + 1.0)


def _softplus(z):
    e = jnp.exp(-jnp.abs(z))
    w = 1.0 + e
    log1p = jnp.where(w == 1.0, e, jnp.log(w) * (e / jnp.where(w == 1.0, 1.0, w - 1.0)))
    return jnp.maximum(z, 0.0) + log1p


def _conv(up, cw, cb):
    out = cb + cw[CONV_WIDTH - 1:CONV_WIDTH, :] * up
    for j in range(CONV_WIDTH - 1):
        out = out + cw[j:j + 1, :] * _shift_down(up, CONV_WIDTH - 1 - j, 0.0)
    return out


def _lru_gates(u, wa_ref, ba_ref, wx_ref, bx_ref, lam_ref):
    ub = u.astype(BF16)
    r = _sigmoid(_dot(ub, wa_ref[...].astype(BF16)) + ba_ref[...])
    i = _sigmoid(_dot(ub, wx_ref[...].astype(BF16)) + bx_ref[...])
    sp = _softplus(-lam_ref[...])
    log_a = (-LRU_C) * r * sp
    a = jnp.exp(log_a)
    mult = jnp.sqrt(_neg_expm1_twice(log_a, a))
    return r, i, sp, a, mult


def _lru_specs(s):
    cb = RNN_BLOCK
    vec = pl.BlockSpec((1, cb), lambda n, b: (0, n))
    return dict(
        up=pl.BlockSpec((None, s, cb), lambda n, b: (b, 0, OFF_RNN_X // cb + n)),
        gr=pl.BlockSpec((None, s, cb), lambda n, b: (b, 0, OFF_RNN_G // cb + n)),
        act=pl.BlockSpec((None, s, cb), lambda n, b: (b, 0, n)),
        cw=pl.BlockSpec((CONV_WIDTH, cb), lambda n, b: (0, n)),
        vec=vec,
        wblk=pl.BlockSpec((None, cb, cb), lambda n, b: (n, 0, 0)),
    )


def _lru_fwd(proj3, cw, cb, wa, ba, wx, bx, lam, riders):
    bsz, s, _ = proj3.shape
    sp = _lru_specs(s)
    nr = len(riders)

    def body(up_ref, gr_ref, cw_ref, cb_ref, wa_ref, ba_ref, wx_ref, bx_ref, lam_ref, *refs):
        rider_in, (h_ref, y_ref), rider_out = refs[:nr], refs[nr:nr + 2], refs[nr + 2:2 * nr + 2]
        sa_ref, sb_ref = refs[2 * nr + 2:2 * nr + 4]
        start, pass_on, finish = _row_gather(rider_in, rider_out, *refs[2 * nr + 4:])
        step = pl.program_id(0) * bsz + pl.program_id(1)
        first, last = step == 0, step == N_RNN_BLOCKS * bsz - 1
        pl.when(first)(start)
        pl.when(step == (3 * N_RNN_BLOCKS * bsz) // 4)(pass_on)
        u = _conv(up_ref[...], cw_ref[...], cb_ref[...])
        _, i, _, a, mult = _lru_gates(u, wa_ref, ba_ref, wx_ref, bx_ref, lam_ref)
        h = _scan(a, mult * (i * u), sa_ref, sb_ref, reverse=False)
        h_ref[...] = h
        g = gr_ref[...]
        y_ref[...] = (h * (g * _sigmoid(g))).astype(BF16)
        pl.when(last)(finish)

    res = pl.pallas_call(
        body, name="lru_fwd", grid=(N_RNN_BLOCKS, bsz),
        out_shape=[jax.ShapeDtypeStruct((bsz, s, D_RNN), F32), jax.ShapeDtypeStruct((bsz, s, D_RNN), BF16)] + [
            jax.ShapeDtypeStruct(r.shape, r.dtype) for r in riders],
        in_specs=[sp["up"], sp["gr"], sp["cw"], sp["vec"], sp["wblk"], sp["vec"], sp["wblk"], sp["vec"], sp["vec"]] + [
            ANY] * nr,
        out_specs=[sp["act"], sp["act"]] + [ANY] * nr, input_output_aliases={9 + t: 2 + t for t in range(nr)},
        scratch_shapes=[pltpu.VMEM((s, RNN_BLOCK), F32)] * 2 + [pltpu.SemaphoreType.DMA((3 * nr,))] * 4,
        compiler_params=_params())(proj3, proj3, cw, cb, wa, ba, wx, bx, lam, *riders)
    return res[0], res[1], res[2:]


def _lru_bwd(proj3, h3, dy3, dproj3, cw, cb, wa, ba, wx, bx, lam):
    bsz, s, _ = proj3.shape
    sp = _lru_specs(s)
    n_steps = N_RNN_BLOCKS * bsz

    def body(up_ref, gr_ref, h_ref, dy_ref, cw_ref, cb_ref, wa_ref, ba_ref, wx_ref, bx_ref, lam_ref, dp_in,
             dp_ref, dcw_ref, dcb_ref, dwa_ref, dba_ref, dwx_ref, dbx_ref, dlam_ref, sa_ref, sb_ref,
             dup_st, dgr_st, o_sems):
        del dp_in
        blk = pl.program_id(0)
        b = pl.program_id(1)
        step = blk * bsz + b
        slot = step % 2
        stages = [dup_st, dgr_st]
        dsts = [dp_ref.at[b, :, pl.ds(pl.multiple_of(OFF_RNN_X + blk * RNN_BLOCK, LANES), RNN_BLOCK)],
                dp_ref.at[b, :, pl.ds(pl.multiple_of(OFF_RNN_G + blk * RNN_BLOCK, LANES), RNN_BLOCK)]]
        _staged_reuse(step, stages, dsts, o_sems, slot)
        up = up_ref[...]
        cwv = cw_ref[...]
        u = _conv(up, cwv, cb_ref[...])
        r, i, spv, a, mult = _lru_gates(u, wa_ref, ba_ref, wx_ref, bx_ref, lam_ref)
        h = h_ref[...]
        g = gr_ref[...]
        dy = dy_ref[...]
        sg = _sigmoid(g)
        dgr_st[slot] = (dy * h * (sg * (1.0 + g * (1.0 - sg)))).astype(BF16)
        dh = dy * (g * sg)
        adj = _scan(_shift_up(a, 1, 0.0), dh, sa_ref, sb_ref, reverse=True)
        da = adj * _shift_down(h, 1, 0.0)
        dmult = adj * (i * u)
        di = adj * mult * u
        du = adj * mult * i
        dla = da * a - dmult * (a * a) / mult
        dr = dla * ((-LRU_C) * spv)
        dsp = jnp.sum(dla * ((-LRU_C) * r), axis=0, keepdims=True)
        dza = dr * r * (1.0 - r)
        dzx = di * i * (1.0 - i)
        ub = u.astype(BF16)
        dzab = dza.astype(BF16)
        dzxb = dzx.astype(BF16)
        du = du + _dot_nt(dzab, wa_ref[...].astype(BF16)) + _dot_nt(dzxb, wx_ref[...].astype(BF16))
        dup = cwv[CONV_WIDTH - 1:CONV_WIDTH, :] * du
        for j in range(CONV_WIDTH - 1):
            dup = dup + cwv[j:j + 1, :] * _shift_up(du, CONV_WIDTH - 1 - j, 0.0)
        dup_st[slot] = dup.astype(BF16)
        _staged_flush(step, n_steps, stages, dsts, o_sems, slot)

        @pl.when(b == 0)
        def _():
            for ref in (dcw_ref, dcb_ref, dwa_ref, dba_ref, dwx_ref, dbx_ref, dlam_ref):
                ref[...] = jnp.zeros_like(ref)

        rows = [jnp.sum(du * _shift_down(up, CONV_WIDTH - 1 - j, 0.0), axis=0, keepdims=True)
                for j in range(CONV_WIDTH - 1)]
        rows.append(jnp.sum(du * up, axis=0, keepdims=True))
        dcw_ref[...] += jnp.concatenate(rows, axis=0)
        dcb_ref[...] += jnp.sum(du, axis=0, keepdims=True)
        dwa_ref[...] += _dot_tn(ub, dzab)
        dba_ref[...] += jnp.sum(dza, axis=0, keepdims=True)
        dwx_ref[...] += _dot_tn(ub, dzxb)
        dbx_ref[...] += jnp.sum(dzx, axis=0, keepdims=True)
        dlam_ref[...] += dsp * (-_sigmoid(-lam_ref[...]))

    vec = jax.ShapeDtypeStruct((1, D_RNN), F32)
    wsd = jax.ShapeDtypeStruct((N_RNN_BLOCKS, RNN_BLOCK, RNN_BLOCK), F32)
    return pl.pallas_call(
        body, name="lru_bwd", grid=(N_RNN_BLOCKS, bsz),
        out_shape=(jax.ShapeDtypeStruct(dproj3.shape, dproj3.dtype), jax.ShapeDtypeStruct((CONV_WIDTH, D_RNN), F32),
                   vec, wsd, vec, wsd, vec, vec),
        in_specs=[sp["up"], sp["gr"], sp["act"], sp["act"], sp["cw"], sp["vec"], sp["wblk"], sp["vec"],
                  sp["wblk"], sp["vec"], sp["vec"], ANY],
        out_specs=(ANY, sp["cw"], sp["vec"], sp["wblk"], sp["vec"], sp["wblk"], sp["vec"], sp["vec"]),
        input_output_aliases={11: 0},
        scratch_shapes=[pltpu.VMEM((s, RNN_BLOCK), F32)] * 2 + [pltpu.VMEM((2, s, RNN_BLOCK), BF16)] * 2 + [
            pltpu.SemaphoreType.DMA((4,))],
        compiler_params=_params())(proj3, proj3, h3, dy3, cw, cb, wa, ba, wx, bx, lam, dproj3)


def _rope_tables(s):
    half = ROPE_DIM // 2
    pos = jnp.arange(s, dtype=F32)
    inv_freq = ROPE_THETA ** (-jnp.arange(0, ROPE_DIM, 2, dtype=F32) / ROPE_DIM)
    ang = pos[:, None] * inv_freq[None, :]
    cos, sin = jnp.cos(ang), jnp.sin(ang)
    rest = HEAD_DIM - ROPE_DIM
    cos64 = jnp.concatenate([cos, cos, jnp.ones((s, rest), F32)], axis=1)
    sin64 = jnp.concatenate([-sin, sin, jnp.zeros((s, rest), F32)], axis=1)
    assert half * 2 == ROPE_DIM
    return jnp.tile(cos64, (1, LANES // HEAD_DIM)), jnp.tile(sin64, (1, LANES // HEAD_DIM))


def _swap_rot_halves(v):
    half = ROPE_DIM // 2
    lane = lax.broadcasted_iota(jnp.int32, v.shape, 1) % HEAD_DIM
    second = jnp.where(lane < ROPE_DIM, pltpu.roll(v, half, axis=1), 0.0)
    return jnp.where(lane < half, pltpu.roll(v, LANES - half, axis=1), second)


def _rope(v, cos, sin):
    tiles = []
    for t in range(v.shape[1] // LANES):
        vt = v[:, t * LANES:(t + 1) * LANES]
        tiles.append(vt * cos + _swap_rot_halves(vt) * sin)
    return tiles[0] if len(tiles) == 1 else jnp.concatenate(tiles, axis=1)


def _unrope(v, cos, sin):
    tiles = []
    for t in range(v.shape[1] // LANES):
        vt = v[:, t * LANES:(t + 1) * LANES]
        tiles.append(vt * cos + _swap_rot_halves(vt * sin))
    return tiles[0] if len(tiles) == 1 else jnp.concatenate(tiles, axis=1)


HEADS_PER_STEP = 8
QW = HEADS_PER_STEP * HEAD_DIM
N_PAIRS = N_Q_HEADS // HEADS_PER_STEP
Q_PER_KV = N_Q_HEADS // N_KV_HEADS
KV_PER_STEP = HEADS_PER_STEP // Q_PER_KV


QT_COLS = Q_PER_KV * ATT_BLK


def _attn_saved_shapes(bsz, s):
    nb = s // ATT_BLK
    pad = s + ATT_BLK
    return [(bsz, N_PAIRS, nb, LANES, QT_COLS), (bsz, N_PAIRS, KV_PER_STEP, pad, LANES),
            (bsz, N_PAIRS, KV_PER_STEP, pad, LANES), (bsz, N_PAIRS, LANES, pad)]


def _attn_specs(s, order):
    def mk(width, base, **kw):
        if order == "bp":
            return pl.BlockSpec((None, s, width), lambda b, p: (b, 0, base + p), **kw)
        return pl.BlockSpec((None, s, width), lambda p, b: (b, 0, base + p), **kw)

    def saved(shape, **kw):
        blk = (None, None) + shape[2:]
        zeros = (0,) * (len(shape) - 2)
        if order == "bp":
            return pl.BlockSpec(blk, lambda b, p: (b, p) + zeros, **kw)
        return pl.BlockSpec(blk, lambda p, b: (b, p) + zeros, **kw)

    one = dict(pipeline_mode=pl.Buffered(1))
    tbl = pl.BlockSpec((s, LANES), lambda *_: (0, 0))
    shapes = _attn_saved_shapes(1, s)
    return dict(q=mk(QW, OFF_Q // QW), k=mk(LANES, OFF_K // LANES), v=mk(LANES, OFF_V // LANES),
                g=mk(QW, OFF_ATTN_G // QW), act=mk(QW, 0), kv=mk(LANES, 0), tbl=tbl,
                g1=mk(QW, OFF_ATTN_G // QW, **one), act1=mk(QW, 0, **one),
                saved=[saved(sh) for sh in shapes], saved1=[saved(sh, **one) for sh in shapes],
                smem=pl.BlockSpec(memory_space=pltpu.SMEM))


def _to_qt(blk):
    rows = []
    for j in range(KV_PER_STEP):
        cols = []
        for tt in range(2):
            t = 2 * j + tt
            tr = blk[:, t * LANES:(t + 1) * LANES].T
            cols += [tr[0:HEAD_DIM, :], tr[HEAD_DIM:, :]]
        rows.append(jnp.concatenate(cols, axis=1))
    return jnp.concatenate(rows, axis=0)


def _from_qt(xt):
    tiles = []
    for j in range(KV_PER_STEP):
        for tt in range(2):
            g0 = 2 * tt
            pair = jnp.concatenate([xt[j * HEAD_DIM:(j + 1) * HEAD_DIM, (g0 + i) * ATT_BLK:(g0 + i + 1) * ATT_BLK]
                                    for i in range(2)], axis=0)
            tiles.append(pair.T)
    return jnp.concatenate(tiles, axis=1)


def _attn_prep(q_ref, k_ref, v_ref, cos_ref, sin_ref, qt_ref, km_ref, vm_ref, kt_ref, vt_ref, nb):
    zeros = jnp.zeros((ATT_BLK, LANES), BF16)
    for j in range(KV_PER_STEP):
        km_ref[j, 0:ATT_BLK, :] = zeros
        vm_ref[j, 0:ATT_BLK, :] = zeros
    kt_ref[:, 0:ATT_BLK] = zeros
    vt_ref[:, 0:ATT_BLK] = zeros
    head_of_lane = lax.broadcasted_iota(jnp.int32, (ATT_BLK, LANES), 1) // HEAD_DIM

    def prep(n, carry):
        r0 = pl.multiple_of(n * ATT_BLK, ATT_BLK)
        cs = cos_ref[pl.ds(r0, ATT_BLK), :]
        sn = sin_ref[pl.ds(r0, ATT_BLK), :]
        qt_ref[n] = _to_qt(_rope(q_ref[pl.ds(r0, ATT_BLK), :], cs, sn) * ATTN_SCALE).astype(BF16)
        k = _rope(k_ref[pl.ds(r0, ATT_BLK), :], cs, sn)
        v = v_ref[pl.ds(r0, ATT_BLK), :]
        for j in range(KV_PER_STEP):
            km_ref[j, pl.ds(r0 + ATT_BLK, ATT_BLK), :] = jnp.where(head_of_lane == j, k, 0.0).astype(BF16)
            vm_ref[j, pl.ds(r0 + ATT_BLK, ATT_BLK), :] = jnp.where(head_of_lane == j, v, 0.0).astype(BF16)
        kt_ref[:, pl.ds(r0 + ATT_BLK, ATT_BLK)] = k.T.astype(BF16)
        vt_ref[:, pl.ds(r0 + ATT_BLK, ATT_BLK)] = v.T.astype(BF16)
        return carry

    lax.fori_loop(0, nb, prep, 0)


def _from_prev_block():
    key = lax.broadcasted_iota(jnp.int32, (ATT_BLK, QT_COLS), 0)
    qry = lax.broadcasted_iota(jnp.int32, (ATT_BLK, QT_COLS), 1) % ATT_BLK
    return key > qry


def _fold(tile, prev, prev_bias=None):
    top = tile[:ATT_BLK] if prev_bias is None else tile[:ATT_BLK] + prev_bias
    return jnp.where(prev, top, tile[ATT_BLK:])


def _unfold(folded, prev):
    zero = jnp.zeros_like(folded)
    return jnp.concatenate([jnp.where(prev, folded, zero), jnp.where(prev, zero, folded)], axis=0).astype(BF16)


def _no_prev_bias(n):
    return jnp.where(n == 0, NEG_BIG, 0.0).astype(F32)


def _sink_row(sink_ref, first):
    return jnp.concatenate([jnp.full((1, ATT_BLK), sink_ref[first + g], F32) for g in range(Q_PER_KV)], axis=1)


def _softmax_cols(sc, sink):
    m = jnp.maximum(jnp.max(sc, axis=0, keepdims=True), sink)
    e = jnp.exp(sc - m)
    es = jnp.exp(sink - m)
    inv = 1.0 / (jnp.sum(e, axis=0, keepdims=True) + es)
    return e * inv, es * inv


def _attn_fwd(proj3, sinks, cosf, sinf):
    bsz, s, _ = proj3.shape
    nb = s // ATT_BLK
    sp = _attn_specs(s, "bp")

    def body(sink_ref, q_ref, k_ref, v_ref, g_ref, cos_ref, sin_ref, o_ref, y_ref, qt_sc, km_sc, vm_sc, kt_ref, vt_sc):
        p = pl.program_id(1)
        _attn_prep(q_ref, k_ref, v_ref, cos_ref, sin_ref, qt_sc, km_sc, vm_sc, kt_ref, vt_sc, nb)
        kv_row = lax.broadcasted_iota(jnp.int32, (LANES, QT_COLS), 0) // HEAD_DIM
        prev = _from_prev_block()

        def blk(n, carry):
            r0 = pl.multiple_of(n * ATT_BLK, ATT_BLK)
            bias = _no_prev_bias(n)
            rq = qt_sc[n]
            vt = vt_sc[:, pl.ds(r0, 2 * ATT_BLK)]
            ots = []
            for j in range(KV_PER_STEP):
                st = _dot(km_sc[j, pl.ds(r0, 2 * ATT_BLK), :], rq)
                pc, _ = _softmax_cols(_fold(st, prev, bias), _sink_row(sink_ref, p * HEADS_PER_STEP + j * Q_PER_KV))
                ots.append(_dot(vt, _unfold(pc, prev)))
            o = _from_qt(jnp.where(kv_row == 0, ots[0], ots[1]))
            o_ref[pl.ds(r0, ATT_BLK), :] = o
            g = g_ref[pl.ds(r0, ATT_BLK), :]
            y_ref[pl.ds(r0, ATT_BLK), :] = (o * (g * _sigmoid(g))).astype(BF16)
            return carry

        lax.fori_loop(0, nb, blk, 0, unroll=2)

    res = pl.pallas_call(
        body, name="attn_fwd", grid=(bsz, N_PAIRS),
        out_shape=[jax.ShapeDtypeStruct((bsz, s, D_ATTN), F32), jax.ShapeDtypeStruct((bsz, s, D_ATTN), BF16)] + [
            jax.ShapeDtypeStruct(sh, BF16) for sh in _attn_saved_shapes(bsz, s)],
        in_specs=[sp["smem"], sp["q"], sp["k"], sp["v"], sp["g"], sp["tbl"], sp["tbl"]],
        out_specs=[sp["act"], sp["act"]] + sp["saved"],
        scratch_shapes=[pltpu.VMEM((LANES, s + ATT_BLK), BF16)],
        compiler_params=_params())(sinks, proj3, proj3, proj3, proj3, cosf, sinf)
    return res[0], res[1], res[2:]


def _attn_bwd(proj3, saved, o3, dy3, dproj3, sinks, cosf, sinf):
    bsz, s, _ = proj3.shape
    nb = s // ATT_BLK
    assert nb % 2 == 0
    sp = _attn_specs(s, "pb")
    n_steps = N_PAIRS * bsz

    def body(sink_ref, qt_sc, km_sc, vm_sc, kt_sc, g_ref, o_ref, dy_ref, cos_ref, sin_ref, dp_in,
             dp_ref, ds_ref, dot_sc, dqt_sc, dk_sc, dv_sc, dq_st, dk_st, dv_st, dg_st, o_sems):
        del dp_in
        p = pl.program_id(0)
        b = pl.program_id(1)
        step = p * bsz + b
        slot = step % 2
        stages = [dq_st, dk_st, dv_st, dg_st]
        dsts = [dp_ref.at[b, :, pl.ds(pl.multiple_of(OFF_Q + p * QW, LANES), QW)],
                dp_ref.at[b, :, pl.ds(pl.multiple_of(OFF_K + p * LANES, LANES), LANES)],
                dp_ref.at[b, :, pl.ds(pl.multiple_of(OFF_V + p * LANES, LANES), LANES)],
                dp_ref.at[b, :, pl.ds(pl.multiple_of(OFF_ATTN_G + p * QW, LANES), QW)]]
        _staged_reuse(step, stages, dsts, o_sems, slot)
        dk_sc[...] = jnp.zeros_like(dk_sc)
        dv_sc[...] = jnp.zeros_like(dv_sc)

        def gate(n, carry):
            r0 = pl.multiple_of(n * ATT_BLK, ATT_BLK)
            g = g_ref[pl.ds(r0, ATT_BLK), :]
            dy = dy_ref[pl.ds(r0, ATT_BLK), :]
            sg = _sigmoid(g)
            dg_st[slot, pl.ds(r0, ATT_BLK), :] = (dy * o_ref[pl.ds(r0, ATT_BLK), :] * (sg * (1.0 + g * (1.0 - sg)))).astype(BF16)
            dot_sc[n] = _to_qt(dy * (g * sg)).astype(BF16)
            return carry

        lax.fori_loop(0, nb, gate, 0)
        kv_lane = lax.broadcasted_iota(jnp.int32, (2 * ATT_BLK, LANES), 1) // HEAD_DIM
        kv_row = lax.broadcasted_iota(jnp.int32, (LANES, QT_COLS), 0) // HEAD_DIM
        prev = _from_prev_block()

        def blk(n, acc):
            r0 = pl.multiple_of(n * ATT_BLK, ATT_BLK)
            bias = _no_prev_bias(n)
            rq = qt_sc[n]
            rd = dot_sc[n]
            kt = kt_sc[:, pl.ds(r0, 2 * ATT_BLK)]
            dvs, dks, dqs, new_acc = [], [], [], []
            for j in range(KV_PER_STEP):
                st = _dot(km_sc[j, pl.ds(r0, 2 * ATT_BLK), :], rq)
                pc, ps = _softmax_cols(_fold(st, prev, bias), _sink_row(sink_ref, p * HEADS_PER_STEP + j * Q_PER_KV))
                dpc = _fold(_dot(vm_sc[j, pl.ds(r0, 2 * ATT_BLK), :], rd), prev)
                delta = jnp.sum(pc * dpc, axis=0, keepdims=True)
                dst = _unfold(pc * (dpc - delta), prev)
                new_acc.append(acc[j] + ps * delta)
                dvs.append(_dot_nt(_unfold(pc, prev), rd))
                dks.append(_dot_nt(dst, rq))
                dqs.append(_dot(kt, dst))
            dv_sc[pl.ds(r0, 2 * ATT_BLK), :] += jnp.where(kv_lane == 0, dvs[0], dvs[1])
            dk_sc[pl.ds(r0, 2 * ATT_BLK), :] += jnp.where(kv_lane == 0, dks[0], dks[1])
            dqt_sc[n] = jnp.where(kv_row == 0, dqs[0], dqs[1]) * ATTN_SCALE
            return tuple(new_acc)

        def blk_pair(m, acc):
            return blk(2 * m + 1, blk(2 * m, acc))

        acc = lax.fori_loop(0, nb // 2, blk_pair, tuple(jnp.zeros((1, QT_COLS), F32) for _ in range(KV_PER_STEP)))
        lane1 = lax.broadcasted_iota(jnp.int32, (1, LANES), 1)
        dsink = jnp.zeros((1, LANES), F32)
        for j in range(KV_PER_STEP):
            for i in range(Q_PER_KV):
                part = jnp.sum(acc[j][:, i * ATT_BLK:(i + 1) * ATT_BLK], axis=1, keepdims=True)
                dsink = dsink - jnp.where(lane1 == j * Q_PER_KV + i, part, 0.0)

        @pl.when(b == 0)
        def _():
            ds_ref[...] = jnp.zeros_like(ds_ref)

        ds_ref[...] += dsink

        def post(n, carry):
            r0 = pl.multiple_of(n * ATT_BLK, ATT_BLK)
            cs = cos_ref[pl.ds(r0, ATT_BLK), :]
            sn = sin_ref[pl.ds(r0, ATT_BLK), :]
            dq_st[slot, pl.ds(r0, ATT_BLK), :] = _unrope(_from_qt(dqt_sc[n]), cs, sn).astype(BF16)
            dk_st[slot, pl.ds(r0, ATT_BLK), :] = _unrope(dk_sc[pl.ds(r0 + ATT_BLK, ATT_BLK), :], cs, sn).astype(BF16)
            dv_st[slot, pl.ds(r0, ATT_BLK), :] = dv_sc[pl.ds(r0 + ATT_BLK, ATT_BLK), :].astype(BF16)
            return carry

        lax.fori_loop(0, nb, post, 0)
        _staged_flush(step, n_steps, stages, dsts, o_sems, slot)

    n_in = 1 + len(saved) + 5
    return pl.pallas_call(
        body, name="attn_bwd", grid=(N_PAIRS, bsz),
        out_shape=(jax.ShapeDtypeStruct(dproj3.shape, dproj3.dtype), jax.ShapeDtypeStruct((N_PAIRS, 1, LANES), F32)),
        in_specs=[sp["smem"]] + sp["saved1"] + [sp["g1"], sp["act1"], sp["act1"], sp["tbl"], sp["tbl"], ANY],
        out_specs=(ANY, pl.BlockSpec((None, 1, LANES), lambda p, b: (p, 0, 0))),
        input_output_aliases={n_in: 0},
        scratch_shapes=[pltpu.VMEM((nb, LANES, QT_COLS), BF16),
                        pltpu.VMEM((nb, LANES, QT_COLS), F32),
                        pltpu.VMEM((s + ATT_BLK, LANES), F32),
                        pltpu.VMEM((s + ATT_BLK, LANES), F32),
                        pltpu.VMEM((2, s, QW), BF16), pltpu.VMEM((2, s, LANES), BF16),
                        pltpu.VMEM((2, s, LANES), BF16), pltpu.VMEM((2, s, QW), BF16),
                        pltpu.SemaphoreType.DMA((8,))],
        compiler_params=_params())(sinks, *saved, proj3, o3, dy3, cosf, sinf, dproj3)


def _staged_copies(stages, dsts, sems, slot):
    return [pltpu.make_async_copy(st.at[slot], dst, sems.at[slot * len(stages) + t])
            for t, (st, dst) in enumerate(zip(stages, dsts))]


def _staged_reuse(step, stages, dsts, sems, slot):
    @pl.when(step >= 2)
    def _():
        for cp in _staged_copies(stages, dsts, sems, slot):
            cp.wait()


def _staged_flush(step, n_steps, stages, dsts, sems, slot):
    for cp in _staged_copies(stages, dsts, sems, slot):
        cp.start()

    @pl.when(step == n_steps - 1)
    def _():
        for cp in _staged_copies(stages, dsts, sems, slot):
            cp.wait()
        if n_steps >= 2:
            for cp in _staged_copies(stages, dsts, sems, 1 - slot):
                cp.wait()


def _merge_fwd_bwd(x, tgt, y_rnn, y_attn, proj, w_r, w_a, w_o, gf):
    t, d = x.shape
    tm = min(t, 256)
    nt = t // tm

    hw = d // 2

    def body(x_ref, t_ref, yr_ref, ya_ref, mr0_ref, mr1_ref, ma0_ref, ma1_ref, wr_ref, wa_ref, wo_ref, gf_ref,
             dp_ref, dyr_ref, dya_ref, mg_ref, dx2_ref, dx2b_ref, dpr_ref, dpa_ref, loss_ref, dgf_ref, dmg_st, o_sems):
        i = pl.program_id(0)
        slot = i % 2
        dsts = [dp_ref.at[pl.ds(pl.multiple_of(i * tm, tm), tm), pl.ds(OFF_MERGE_R, 2 * d)]]
        _staged_reuse(i, [dmg_st], dsts, o_sems, slot)
        wr = wr_ref[...]
        wa = wa_ref[...]
        wo = wo_ref[...]
        gfv = gf_ref[...]
        pr = _dot(yr_ref[...], wr)
        pa = _dot(ya_ref[...], wa)
        sr = _sigmoid(jnp.concatenate([mr0_ref[...], mr1_ref[...]], axis=1))
        sa = _sigmoid(jnp.concatenate([ma0_ref[...], ma1_ref[...]], axis=1))
        mb = (sr * pr + sa * pa).astype(BF16)
        mg_ref[...] = mb
        x2 = x_ref[...] + _dot(mb, wo)
        r2 = lax.rsqrt(jnp.mean(x2 * x2, axis=-1, keepdims=True) + NORM_EPS)
        nrm = x2 * r2
        err = nrm * gfv - t_ref[...]
        dy = err * (1.0 / d)
        dn = dy * gfv
        dx2 = r2 * (dn - nrm * jnp.mean(dn * nrm, axis=-1, keepdims=True))
        dx2_ref[...] = dx2
        dx2b = dx2.astype(BF16)
        dx2b_ref[...] = dx2b
        dmerged = _dot_nt(dx2b, wo)
        dpr = (dmerged * sr).astype(BF16)
        dpa = (dmerged * sa).astype(BF16)
        dpr_ref[...] = dpr
        dpa_ref[...] = dpa
        dmg_st[slot, :, 0:d] = (dmerged * pr * (sr * (1.0 - sr))).astype(BF16)
        dmg_st[slot, :, d:2 * d] = (dmerged * pa * (sa * (1.0 - sa))).astype(BF16)
        _staged_flush(i, nt, [dmg_st], dsts, o_sems, slot)
        dyr_ref[...] = _dot_nt(dpr, wr)
        dya_ref[...] = _dot_nt(dpa, wa)

        @pl.when(i == 0)
        def _():
            loss_ref[...] = jnp.zeros_like(loss_ref)
            dgf_ref[...] = jnp.zeros_like(dgf_ref)

        loss_ref[...] += jnp.full((1, LANES), 0.5 / d, F32) * jnp.sum(err * err)
        dgf_ref[...] += jnp.sum(dy * nrm, axis=0, keepdims=True)

    tile = pl.BlockSpec((tm, d), lambda i: (i, 0))
    wsp = pl.BlockSpec((d, d), lambda i: (0, 0))

    def gate(col_blk):
        return pl.BlockSpec((tm, hw), lambda i: (i, col_blk))

    fb = jax.ShapeDtypeStruct((t, d), BF16)
    ff = jax.ShapeDtypeStruct((t, d), F32)
    return pl.pallas_call(
        body, name="merge_fwd_bwd", grid=(nt,),
        out_shape=(jax.ShapeDtypeStruct((t, D_IN), BF16), ff, ff, fb, ff, fb, fb, fb,
                   jax.ShapeDtypeStruct((1, LANES), F32), jax.ShapeDtypeStruct((1, d), F32)),
        in_specs=[tile, tile, tile, tile] + [gate(OFF_MERGE_R // hw + j) for j in range(4)] + [
            wsp, wsp, wsp, pl.BlockSpec((1, d), lambda i: (0, 0))],
        out_specs=(ANY, tile, tile, tile, tile, tile, tile, tile,
                   pl.BlockSpec((1, LANES), lambda i: (0, 0)), pl.BlockSpec((1, d), lambda i: (0, 0))),
        scratch_shapes=[pltpu.VMEM((2, tm, 2 * d), BF16), pltpu.SemaphoreType.DMA((2,))],
        compiler_params=_params())(x, tgt, y_rnn, y_attn, proj, proj, proj, proj, w_r, w_a, w_o, gf)


def _local_grads(x, tgt, h, proj, norm_g, w_in_bm, conv_w, conv_b, lru_w_a, lru_b_a, lru_w_x, lru_b_x, lam, sinks,
                 row_sharded, gf):
    bsz, s, d = x.shape
    t = bsz * s
    x2 = x.reshape(t, d)
    proj3 = proj.reshape(bsz, s, D_IN)
    h_lru, y_rnn, gathered = _lru_fwd(proj3, conv_w, conv_b, lru_w_a, lru_b_a, lru_w_x, lru_b_x, lam, row_sharded)
    w_r, w_a, w_o = (g.reshape(d, d) for g in gathered)
    cosf, sinf = _rope_tables(s)
    o_attn, y_attn, attn_saved = _attn_fwd(proj3, sinks, cosf, sinf)
    y_rnn2 = y_rnn.reshape(t, d)
    y_attn2 = y_attn.reshape(t, d)
    dproj, dyr, dya, merged, dx2, dx2b, dpr, dpa, loss, dgf = _merge_fwd_bwd(
        x2, tgt.reshape(t, d), y_rnn2, y_attn2, proj, w_r, w_a, w_o, gf)
    dproj3, dsink = _attn_bwd(proj3, attn_saved, o_attn, dya.reshape(bsz, s, d), dproj.reshape(bsz, s, D_IN),
                              sinks, cosf, sinf)
    dproj3, dcw, dcb, dwa, dba, dwx, dbx, dlam = _lru_bwd(
        proj3, h_lru, dyr.reshape(bsz, s, d), dproj3, conv_w, conv_b, lru_w_a, lru_b_a, lru_w_x, lru_b_x, lam)
    dproj = dproj3.reshape(t, D_IN)
    grad_x, dng = _grad_x(dproj, w_in_bm, x2, dx2, norm_g)
    small = dict(norm_g=dng, conv_w=dcw, conv_b=dcb, lru_w_a=dwa, lru_b_a=dba, lru_w_x=dwx, lru_b_x=dbx,
                 lru_lambda=dlam, attn_sinks=dsink[:, 0, :HEADS_PER_STEP].reshape(1, N_Q_HEADS), final_norm_g=dgf)
    squares = [(y_rnn2, dpr), (y_attn2, dpa), (merged, dx2b)]
    return loss[0, 0], grad_x.reshape(bsz, s, d), h, dproj, squares, small


ANY = pl.BlockSpec(memory_space=pl.ANY)


def _mesh_pos():
    return lax.axis_index("x"), lax.axis_index("y"), lax.axis_index("c")


def _remote(src, dst, send_sems, recv_sems, idx, peer):
    return pltpu.make_async_remote_copy(src_ref=src, dst_ref=dst, send_sem=send_sems.at[idx],
                                        recv_sem=recv_sems.at[idx], device_id=peer, device_id_type=MESH)


def _row_gather(ins, outs, send_sems, recv_sems, fsend_sems, frecv_sems):
    n = len(ins)
    x, y, c = _mesh_pos()
    me = 2 * x + y
    sib = (x, y, 1 - c)
    peers = [((x, 1 - y, c), me ^ 1), ((1 - x, y, c), me ^ 2), ((1 - x, 1 - y, c), me ^ 3)]

    def half(ref, slot, t, which):
        hr = ins[t].shape[1] // 2
        return ref.at[slot, pl.ds(pl.multiple_of(which * hr, 8), hr), :]

    def ici(t, k):
        peer, pj = peers[k]
        src = half(ins[t], me, t, c)
        return (_remote(src, half(outs[t], me, t, c), send_sems, recv_sems, 3 * t + k, peer),
                _remote(src, half(outs[t], pj, t, c), send_sems, recv_sems, 3 * t + k, peer))

    def forward(t, k):
        got = half(outs[t], peers[k][1], t, c)
        return (_remote(got, got, fsend_sems, frecv_sems, 3 * t + k, sib),
                _remote(got, half(outs[t], peers[k][1], t, 1 - c), fsend_sems, frecv_sems, 3 * t + k, sib))

    pairs = [(t, k) for t in range(n) for k in range(3)]

    def start():
        for t, k in pairs:
            ici(t, k)[0].start()

    def pass_on():
        for t, k in pairs:
            ici(t, k)[1].wait_recv()
            forward(t, k)[0].start()

    def finish():
        for t, k in pairs:
            ici(t, k)[0].wait_send()
            forward(t, k)[0].wait_send()
            forward(t, k)[1].wait_recv()

    return start, pass_on, finish


def _gather_in_proj(x, g, bufs, split, idx):
    t_tok, d = x.shape
    n = len(bufs)
    tm = min(t_tok, 1024)
    nt = t_tok // tm
    n_fwd = 3 * sum(split)
    assert split[0]

    def body(idx_ref, x_ref, g_ref, *refs):
        ins, proj_ref, h_out, outs = refs[:n], refs[n], refs[n + 1], refs[n + 2:2 * n + 2]
        wbuf, h_all, send_sems, recv_sems, fsend_sems, frecv_sems, l_sems = refs[2 * n + 2:]
        j, i = pl.program_id(0), pl.program_id(1)
        rows = pl.ds(pl.multiple_of(i * tm, tm), tm)
        x, y, c = _mesh_pos()
        me = 2 * x + y
        sib = (x, y, 1 - c)
        peers = [((x, 1 - y, c), me ^ 1), ((1 - x, y, c), me ^ 2), ((1 - x, 1 - y, c), me ^ 3)]

        def part(ref, slot, t, half):
            if not split[t]:
                return ref.at[slot]
            hr = bufs[t].shape[1] // 2
            return ref.at[slot, pl.ds(pl.multiple_of(half * hr, 8), hr), :]

        def land(t):
            return wbuf if t == 0 else outs[t]

        def ici(t, k):
            peer, pj = peers[k]
            src = part(ins[t], me, t, c)
            return (_remote(src, part(land(t), me, t, c), send_sems, recv_sems, 3 * t + k, peer),
                    _remote(src, part(land(t), pj, t, c), send_sems, recv_sems, 3 * t + k, peer))

        fwd_index = {}
        for t in range(n):
            if split[t]:
                for k in range(3):
                    fwd_index[(t, k)] = len(fwd_index)

        def forward(t, k):
            pj = peers[k][1]
            got = part(land(t), pj, t, c)
            f = fwd_index[(t, k)]
            return (_remote(got, got, fsend_sems, frecv_sems, f, sib),
                    _remote(got, part(land(t), pj, t, 1 - c), fsend_sems, frecv_sems, f, sib))

        def write_back(k):
            pj = peers[k][1]
            return pltpu.make_async_copy(wbuf.at[pj], outs[0].at[pj], l_sems.at[1 + k])

        relay_peer = ((x + c) % 2, (y + 1 - c) % 2, c)

        def relay():
            got = part(wbuf, me ^ (2 - c), 0, c)
            return (_remote(got, got, send_sems, recv_sems, 2, relay_peer),
                    _remote(got, part(wbuf, me ^ 3, 0, c), send_sems, recv_sems, 2, relay_peer))

        direct = [(t, k) for t in range(n) for k in range(3) if (t, k) != (0, 2)]

        @pl.when((j == 0) & (i == 0))
        def _():
            for t, k in direct:
                ici(t, k)[0].start()
            own = pltpu.make_async_copy(ins[0].at[me], wbuf.at[me], l_sems.at[0])
            own.start()
            own.wait()

        @pl.when((j == 1) & (i == 0))
        def _():
            pltpu.make_async_copy(h_all, h_out, l_sems.at[4]).start()
            for k in range(2):
                ici(0, k)[1].wait_recv()
            relay()[0].start()
            for k in range(2):
                forward(0, k)[0].start()
            forward(0, 0)[1].wait_recv()
            write_back(0).start()

        @pl.when((j == 2) & (i == 0))
        def _():
            forward(0, 1)[1].wait_recv()
            write_back(1).start()

        @pl.when((j == 3) & (i == 0))
        def _():
            relay()[1].wait_recv()
            forward(0, 2)[0].start()
            forward(0, 2)[1].wait_recv()
            write_back(2).start()

        @pl.when(j == 0)
        def _():
            xv = x_ref[...]
            r = lax.rsqrt(jnp.mean(xv * xv, axis=-1, keepdims=True) + NORM_EPS)
            h_all[rows, :] = (xv * r * g_ref[...]).astype(BF16)

        proj_ref[...] = _dot(h_all[rows, :], wbuf[me ^ j])

        @pl.when((j == N_CHIPS - 1) & (i == nt - 1))
        def _():
            pltpu.make_async_copy(h_all, h_out, l_sems.at[4]).wait()
            for t in range(1, n):
                for k in range(3):
                    ici(t, k)[1].wait_recv()
                    if split[t]:
                        forward(t, k)[0].start()
            relay()[0].wait_send()
            for t, k in direct:
                ici(t, k)[0].wait_send()
            for t in range(n):
                if split[t]:
                    for k in range(3):
                        forward(t, k)[0].wait_send()
                        if t > 0:
                            forward(t, k)[1].wait_recv()
            for k in range(3):
                write_back(k).wait()

    grid_spec = pltpu.PrefetchScalarGridSpec(
        num_scalar_prefetch=1, grid=(N_CHIPS, nt),
        in_specs=[pl.BlockSpec((tm, d), lambda j, i, idx_ref: (jnp.where(j == 0, i, nt - 1), 0)),
                  pl.BlockSpec((1, d), lambda j, i, idx_ref: (0, 0))] + [ANY] * n,
        out_specs=[pl.BlockSpec((tm, W_BLK), lambda j, i, idx_ref: (i, idx_ref[0] ^ j)), ANY] + [ANY] * n,
        scratch_shapes=[pltpu.VMEM(bufs[0].shape, bufs[0].dtype), pltpu.VMEM((t_tok, d), BF16),
                        pltpu.SemaphoreType.DMA((3 * n,)), pltpu.SemaphoreType.DMA((3 * n,)),
                        pltpu.SemaphoreType.DMA((n_fwd,)), pltpu.SemaphoreType.DMA((n_fwd,)),
                        pltpu.SemaphoreType.DMA((5,))])
    out_shape = [jax.ShapeDtypeStruct((t_tok, D_IN), F32), jax.ShapeDtypeStruct((t_tok, d), BF16)] + [
        jax.ShapeDtypeStruct(a.shape, a.dtype) for a in bufs]
    res = pl.pallas_call(
        body, name="gather_in_proj", grid_spec=grid_spec, out_shape=out_shape,
        input_output_aliases={3 + t: 2 + t for t in range(n)}, compiler_params=_params())(idx, x, g, *bufs)
    return res[1], res[0], res[2:]


def _row_tile(rows, row_bytes, cap_bytes=2 * 1024 * 1024):
    best = None
    for tr in range(8, rows + 1, 8):
        if rows % tr == 0 and tr * row_bytes <= cap_bytes:
            best = tr
    return best if best is not None else rows


XOR_ORDER = (3, 2, 1)


def _grads_reduce_scatter(h, dproj, squares, small, idx):
    t, d = h.shape
    nsq = len(squares)
    hr = d // 2
    qr = ROW_BLK // 2
    tk = min(t, 1024)
    nk = t // tk
    last = N_CHIPS - 1
    n_phase = 3

    def dest(s, idx_ref):
        xo = jnp.where(s == 0, XOR_ORDER[0], jnp.where(s == 1, XOR_ORDER[1], jnp.where(s == 2, XOR_ORDER[2], 0)))
        return idx_ref[0] ^ xo

    def k_sq(p, k):
        return jnp.where(p == 0, k, nk - 1)

    def k_w(p, k):
        return jnp.where(p == 0, 0, k)

    in_specs = [
        pl.BlockSpec((tk, hr), lambda s, p, k, idx_ref: (k_w(p, k), (1 - idx_ref[1] + jnp.maximum(p - 1, 0)) % 2)),
        pl.BlockSpec((tk, W_BLK), lambda s, p, k, idx_ref: (k_w(p, k), dest(s, idx_ref)))]
    for q in range(nsq):
        in_specs.append(pl.BlockSpec((tk, ROW_BLK), lambda s, p, k, idx_ref: (k_sq(p, k), dest(s, idx_ref))))
        in_specs.append(pl.BlockSpec((tk, d), lambda s, p, k, idx_ref: (k_sq(p, k), 0)))

    def body(idx_ref, *refs):
        nj = 1 + nsq
        h_ref, dp_ref = refs[0], refs[1]
        sq_in = refs[2:2 + 2 * nsq]
        small_in = refs[2 * nj]
        outs = refs[2 * nj + 1:3 * nj + 2]
        landing = refs[3 * nj + 2:4 * nj + 3]
        sc = refs[4 * nj + 3:]
        acc_w, xr_w, sb_w = sc[0:3]
        sq_sc = [sc[3 + 3 * q:6 + 3 * q] for q in range(nsq)]
        sm, smx = sc[3 * nj:3 * nj + 2]
        x_send, x_recv, i_send, i_recv, f_send, f_recv, o_sem, l_sem = sc[3 * nj + 2:]
        s, p, k = pl.program_id(0), pl.program_id(1), pl.program_id(2)
        x, y, c = _mesh_pos()
        sib = (x, y, 1 - c)
        peers = [((1 - x) if xo & 2 else x, (1 - y) if xo & 1 else y, c) for xo in XOR_ORDER]
        slot = s % 2
        mine_w = pl.ds(pl.multiple_of(c * hr, 8), hr)
        theirs_w = pl.ds(pl.multiple_of((1 - c) * hr, 8), hr)
        mine_q = pl.ds(pl.multiple_of(c * qr, 8), qr)
        theirs_q = pl.ds(pl.multiple_of((1 - c) * qr, 8), qr)

        def exch(j, src, dst):
            return _remote(src, dst, x_send, x_recv, 2 * j + slot, sib)

        sbufs = [sb_w] + [sq_sc[q][2] for q in range(nsq)]

        def ici(j, ss):
            return _remote(sbufs[j].at[ss], landing[j].at[ss], i_send, i_recv, last * j + ss, peers[ss])

        def exchanges():
            cps = [exch(0, acc_w.at[0], xr_w.at[slot])]
            cps += [exch(1 + q, sq_sc[q][0].at[theirs_q, :], sq_sc[q][1].at[slot]) for q in range(nsq)]
            return cps

        def small_send(ss):
            return _remote(sm.at[c], landing[nj].at[ss], i_send, i_recv, last * nj + ss, peers[ss])

        def small_start():
            load = pltpu.make_async_copy(small_in, sm, l_sem.at[nj + 1])
            load.start()
            load.wait()
            swap = _remote(sm, smx.at[pl.ds(0, 2)], x_send, x_recv, 2 * nj, sib)
            swap.start()
            swap.wait_recv()
            swap.wait_send()
            sm[...] = sm[...] + smx[0:2]
            for ss in range(last):
                small_send(ss).start()

        def pair_ref(j):
            return acc_w.at[1] if j == 0 else sq_sc[j - 1][0].at[mine_q, :]

        def sq_phase():
            pl.when((s == 0) & (k == 0))(small_start)
            for q in range(nsq):
                acc = sq_sc[q][0]

                @pl.when(k == 0)
                def _():
                    acc[...] = jnp.zeros((ROW_BLK, d), F32)

                acc[...] += _dot_tn(sq_in[2 * q][...], sq_in[2 * q + 1][...])

            @pl.when(k == nk - 1)
            def _():
                for cp in exchanges()[1:]:
                    cp.start()

        def w_phase(hf):
            @pl.when(k == 0)
            def _():
                acc_w[hf] = jnp.zeros((hr, W_BLK), F32)

            acc_w[hf] += _dot_tn(h_ref[...], dp_ref[...])

            @pl.when(k == nk - 1)
            def _():
                if hf == 0:
                    exchanges()[0].start()
                else:
                    finish_step()

        def finish_step():
            for cp in exchanges():
                cp.wait_recv()
                cp.wait_send()
            acc_w[1] += xr_w[slot]
            for q in range(nsq):
                sq_sc[q][0][mine_q, :] += sq_sc[q][1][slot]
            for ss in range(last):
                @pl.when(s == ss)
                def _():
                    for j in range(nj):
                        sbufs[j][ss] = pair_ref(j)[...].astype(BF16)
                        ici(j, ss).start()

            @pl.when(s == last)
            def _():
                for ss in range(last):
                    for j in range(nj):
                        ici(j, ss).wait_recv()
                        ici(j, ss).wait_send()
                    small_send(ss).wait_recv()
                    small_send(ss).wait_send()
                stage = [pltpu.make_async_copy(landing[j], sbufs[j], l_sem.at[j]) for j in range(nj)]
                stage.append(pltpu.make_async_copy(landing[nj], smx, l_sem.at[nj]))
                for cp in stage:
                    cp.start()
                for j in range(nj):
                    stage[j].wait()
                    total = pair_ref(j)[...]
                    for ss in range(last):
                        total = total + sbufs[j][ss].astype(F32)
                    pair_ref(j)[...] = total
                stage[nj].wait()
                by_xor = {xo: smx[ss] for ss, xo in enumerate(XOR_ORDER)}
                sm[c] = (sm[c] + by_xor[1]) + (by_xor[2] + by_xor[3])
                done = [(acc_w.at[1], outs[0].at[mine_w, :], outs[0].at[theirs_w, :])]
                done += [(pair_ref(1 + q), outs[1 + q].at[mine_q, :], outs[1 + q].at[theirs_q, :]) for q in range(nsq)]
                done.append((sm.at[c], outs[nj].at[c], outs[nj].at[1 - c]))
                copies = []
                for j, (src, mine, theirs) in enumerate(done):
                    keep = pltpu.make_async_copy(src, mine, o_sem.at[j])
                    give = _remote(src, mine, f_send, f_recv, j, sib)
                    take = _remote(src, theirs, f_send, f_recv, j, sib)
                    keep.start()
                    give.start()
                    copies.append((keep, give, take))
                for keep, give, take in copies:
                    keep.wait()
                    give.wait_send()
                    take.wait_recv()

        pl.when(p == 0)(sq_phase)
        for hf in range(2):
            pl.when(p == 1 + hf)(functools.partial(w_phase, hf))

    nj = 1 + nsq
    scratch = [pltpu.VMEM((2, hr, W_BLK), F32), pltpu.VMEM((2, hr, W_BLK), F32), pltpu.VMEM((last, hr, W_BLK), BF16)]
    for _ in range(nsq):
        scratch += [pltpu.VMEM((ROW_BLK, d), F32), pltpu.VMEM((2, qr, d), F32), pltpu.VMEM((last, qr, d), BF16)]
    scratch += [pltpu.VMEM((2, PK_HALF, LANES), F32), pltpu.VMEM((last, PK_HALF, LANES), F32)]
    scratch += [pltpu.SemaphoreType.DMA((2 * nj + 1,)), pltpu.SemaphoreType.DMA((2 * nj + 1,)),
                pltpu.SemaphoreType.DMA((last * (nj + 1),)), pltpu.SemaphoreType.DMA((last * (nj + 1),)),
                pltpu.SemaphoreType.DMA((nj + 1,)), pltpu.SemaphoreType.DMA((nj + 1,)),
                pltpu.SemaphoreType.DMA((nj + 1,)), pltpu.SemaphoreType.DMA((nj + 2,))]
    grid_spec = pltpu.PrefetchScalarGridSpec(
        num_scalar_prefetch=1, grid=(N_CHIPS, n_phase, nk), in_specs=in_specs + [ANY],
        out_specs=[ANY] * (2 * nj + 2), scratch_shapes=scratch)
    out_shape = [jax.ShapeDtypeStruct((d, W_BLK), F32)] + [jax.ShapeDtypeStruct((ROW_BLK, d), F32)] * nsq
    out_shape.append(jax.ShapeDtypeStruct((2, PK_HALF, LANES), F32))
    out_shape += [jax.ShapeDtypeStruct((last, hr, W_BLK), BF16)] + [jax.ShapeDtypeStruct((last, qr, d), BF16)] * nsq
    out_shape.append(jax.ShapeDtypeStruct((last, PK_HALF, LANES), F32))
    flat = [a for pair in squares for a in pair]
    res = pl.pallas_call(body, name="grads_reduce_scatter", grid_spec=grid_spec, out_shape=out_shape,
                         compiler_params=_params())(idx, h, dproj, *flat, small)
    return res[:nj + 1]


_VEC_NAMES = ("norm_g", "conv_b", "lru_b_a", "lru_b_x", "lru_lambda", "final_norm_g")


def _pack_small(p, conv_full=None, scalar=None):
    rows = [p["lru_w_a"].reshape(PK_WX - PK_WA, LANES), p["lru_w_x"].reshape(PK_VEC - PK_WX, LANES)]
    rows += [p[k].reshape(8, LANES) for k in _VEC_NAMES]
    rows.append(jnp.pad(p["attn_sinks"].reshape(1, N_Q_HEADS), ((0, 7), (0, LANES - N_Q_HEADS))))
    rows.append(jnp.zeros((32, LANES), F32) if conv_full is None else conv_full.reshape(32, LANES))
    tail = PK_ROWS - PK_SCALAR
    if scalar is None:
        rows.append(jnp.zeros((tail, LANES), F32))
    else:
        rows.append(jnp.pad(scalar.reshape(1, 1), ((0, tail - 1), (0, LANES - 1))))
    return jnp.concatenate(rows, axis=0)


def _unpack_small(pk, like):
    out = {"lru_w_a": pk[PK_WA:PK_WX].reshape(like["lru_w_a"].shape),
           "lru_w_x": pk[PK_WX:PK_VEC].reshape(like["lru_w_x"].shape)}
    for j, k in enumerate(_VEC_NAMES):
        out[k] = pk[PK_VEC + 8 * j:PK_VEC + 8 * j + 8].reshape(like[k].shape)
    out["attn_sinks"] = pk[PK_SINK:PK_SINK + 1, :N_Q_HEADS].reshape(like["attn_sinks"].shape)
    return out


_WEIGHTS = ("norm_g", "w_in", "conv_w", "conv_b", "lru_w_a", "lru_b_a", "lru_w_x", "lru_b_x", "lru_lambda",
            "attn_sinks", "w_rnn_out", "w_attn_out", "w_o", "final_norm_g")
_SMALL = ("norm_g", "conv_b", "lru_w_a", "lru_b_a", "lru_w_x", "lru_b_x", "lru_lambda", "attn_sinks", "final_norm_g")
_ROW_SHARDED = ("w_rnn_out", "w_attn_out", "w_o")


def kernel(x, norm_g, w_in, conv_w, conv_b, lru_w_a, lru_b_a, lru_w_x, lru_b_x, lru_lambda, attn_sinks, w_rnn_out, w_attn_out, w_o, final_norm_g, loss_target, m_norm_g, m_w_in, m_conv_w, m_conv_b, m_lru_w_a, m_lru_b_a, m_lru_w_x, m_lru_b_x, m_lru_lambda, m_attn_sinks, m_w_rnn_out, m_w_attn_out, m_w_o, m_final_norm_g, v_norm_g, v_w_in, v_conv_w, v_conv_b, v_lru_w_a, v_lru_b_a, v_lru_w_x, v_lru_b_x, v_lru_lambda, v_attn_sinks, v_w_rnn_out, v_w_attn_out, v_w_o, v_final_norm_g):
    w = dict(norm_g=norm_g, w_in=w_in, conv_w=conv_w, conv_b=conv_b, lru_w_a=lru_w_a, lru_b_a=lru_b_a, lru_w_x=lru_w_x,
             lru_b_x=lru_b_x, lru_lambda=lru_lambda, attn_sinks=attn_sinks, w_rnn_out=w_rnn_out, w_attn_out=w_attn_out,
             w_o=w_o, final_norm_g=final_norm_g)
    m = dict(norm_g=m_norm_g, w_in=m_w_in, conv_w=m_conv_w, conv_b=m_conv_b, lru_w_a=m_lru_w_a, lru_b_a=m_lru_b_a,
             lru_w_x=m_lru_w_x, lru_b_x=m_lru_b_x, lru_lambda=m_lru_lambda, attn_sinks=m_attn_sinks,
             w_rnn_out=m_w_rnn_out, w_attn_out=m_w_attn_out, w_o=m_w_o, final_norm_g=m_final_norm_g)
    v = dict(norm_g=v_norm_g, w_in=v_w_in, conv_w=v_conv_w, conv_b=v_conv_b, lru_w_a=v_lru_w_a, lru_b_a=v_lru_b_a,
             lru_w_x=v_lru_w_x, lru_b_x=v_lru_b_x, lru_lambda=v_lru_lambda, attn_sinks=v_attn_sinks,
             w_rnn_out=v_w_rnn_out, w_attn_out=v_w_attn_out, w_o=v_w_o, final_norm_g=v_final_norm_g)
    mx, my, mc = _mesh_pos()
    me = 2 * mx + my
    d = D_MODEL

    my_chip = jnp.reshape(me, (1,)).astype(jnp.int32)
    (buf_in,) = _put_slots([w["w_in"][0]], my_chip, BF16, "cast_w_in")
    (buf_cw,) = _put_slots([w["conv_w"][0]], my_chip, F32, "slot_conv_w")
    row_sharded = _put_slots([w[k][0] for k in _ROW_SHARDED], my_chip, BF16, "cast_row_sharded")
    h, proj, (g_in, g_cw) = _gather_in_proj(x.reshape(-1, d), w["norm_g"], [buf_in, buf_cw], [True, False], my_chip)
    conv_full = g_cw.transpose(1, 0, 2).reshape(CONV_WIDTH, D_RNN)

    loss_local, grad_x, h, dproj, squares, gsmall = _local_grads(
        x, loss_target, h, proj, w["norm_g"], g_in, conv_full, w["conv_b"], w["lru_w_a"][0], w["lru_b_a"], w["lru_w_x"][0],
        w["lru_b_x"], w["lru_lambda"], w["attn_sinks"][0], row_sharded, w["final_norm_g"].reshape(1, d))
    gpack = _pack_small(gsmall, gsmall["conv_w"], loss_local).reshape(2, PK_HALF, LANES)
    f_in, f_r, f_a, f_o, spack = _grads_reduce_scatter(h, dproj, squares, gpack, jnp.stack([me, mc]).astype(jnp.int32))
    spack = spack.reshape(PK_ROWS, LANES)
    loss = spack[PK_SCALAR, 0]

    grads = _unpack_small(spack, w)
    conv_all = spack[PK_CONV:PK_CONV + 32].reshape(CONV_WIDTH, D_RNN)
    grads["conv_w"] = lax.dynamic_slice_in_dim(conv_all, me * (D_RNN // N_CHIPS), D_RNN // N_CHIPS, axis=1)[None]
    grads["w_in"] = f_in[None]
    grads["w_rnn_out"], grads["w_attn_out"], grads["w_o"] = f_r[None], f_a[None], f_o[None]

    delta, new_m, new_v = {}, {}, {}
    for k in ("w_in",) + _ROW_SHARDED:
        dk, mk, vk = _adamw(w[k][0], grads[k][0], m[k][0], v[k][0], "adamw_" + k)
        delta[k], new_m[k], new_v[k] = dk[None], mk[None], vk[None]
    shp = (2 * CONV_WIDTH, LANES)
    dk, mk, vk = _adamw(w["conv_w"].reshape(shp), grads["conv_w"].reshape(shp), m["conv_w"].reshape(shp),
                        v["conv_w"].reshape(shp), "adamw_conv_w")
    delta["conv_w"], new_m["conv_w"], new_v["conv_w"] = (a.reshape(w["conv_w"].shape) for a in (dk, mk, vk))
    dk, mk, vk = _adamw(_pack_small(w), spack, _pack_small(m), _pack_small(v), "adamw_small")
    for src, dst in ((dk, delta), (mk, new_m), (vk, new_v)):
        dst.update(_unpack_small(src, w))

    return (loss, grad_x, *[grads[k] for k in _WEIGHTS], *[delta[k] for k in _WEIGHTS],
            *[new_m[k] for k in _WEIGHTS], *[new_v[k] for k in _WEIGHTS])
```

```python
import functools
import math

import jax
import jax.numpy as jnp
from jax import lax
from jax.experimental import pallas as pl
from jax.experimental.pallas import tpu as pltpu

F32 = jnp.float32
BF16 = jnp.bfloat16
MESH = pl.DeviceIdType.MESH

D_MODEL = 1024
D_RNN = 1024
N_RNN_BLOCKS = 8
RNN_BLOCK = D_RNN // N_RNN_BLOCKS
CONV_WIDTH = 4
LRU_C = 8.0
HEAD_DIM = 64
N_Q_HEADS = 16
N_KV_HEADS = 4
D_ATTN = N_Q_HEADS * HEAD_DIM
D_KV = N_KV_HEADS * HEAD_DIM
WINDOW = 128
ROPE_DIM = HEAD_DIM // 4
ROPE_THETA = 500000.0
NORM_EPS = 1e-6
OFF_RNN_X = 0
OFF_RNN_G = OFF_RNN_X + D_RNN
OFF_Q = OFF_RNN_G + D_RNN
OFF_K = OFF_Q + D_ATTN
OFF_V = OFF_K + D_KV
OFF_ATTN_G = OFF_V + D_KV
OFF_MERGE_R = OFF_ATTN_G + D_ATTN
OFF_MERGE_A = OFF_MERGE_R + D_MODEL
D_IN = OFF_MERGE_A + D_MODEL

ADAM_LR = 0.001
ADAM_B1 = 0.9
ADAM_B2 = 0.999
ADAM_EPS = 1e-08
ADAM_WD = 0.01
ADAM_STEP = 10

N_CHIPS = 4
W_BLK = D_IN // N_CHIPS
ROW_BLK = D_MODEL // N_CHIPS
LANES = 128
ATT_BLK = 128
VMEM_LIMIT = 56 * 1024 * 1024
NEG_BIG = -1e30
ATTN_SCALE = 1.0 / math.sqrt(HEAD_DIM)

PK_WA = 0
PK_WX = PK_WA + N_RNN_BLOCKS * RNN_BLOCK
PK_VEC = PK_WX + N_RNN_BLOCKS * RNN_BLOCK
PK_SINK = PK_VEC + 6 * 8
PK_CONV = PK_SINK + 8
PK_SCALAR = PK_CONV + 32
PK_ROWS = PK_SCALAR + 8
PK_HALF = PK_ROWS // 2


def _params(**kw):
    return pltpu.CompilerParams(vmem_limit_bytes=VMEM_LIMIT, **kw)


def _sigmoid(z):
    return 1.0 / (1.0 + jnp.exp(-z))


def _dot(a, b):
    return jnp.dot(a, b, preferred_element_type=F32)


def _dot_nt(a, b):
    return lax.dot_general(a, b, (((1,), (1,)), ((), ())), preferred_element_type=F32)


def _dot_tn(a, b):
    return lax.dot_general(a, b, (((0,), (0,)), ((), ())), preferred_element_type=F32)


def _put_slots(srcs, slot, dtype, name):
    rows, c = srcs[0].shape
    n = len(srcs)
    tr = _row_tile(rows, c * 4)

    def body(idx_ref, *refs):
        for s_ref, o_ref in zip(refs[:n], refs[n:]):
            o_ref[...] = s_ref[...].astype(dtype)

    grid_spec = pltpu.PrefetchScalarGridSpec(
        num_scalar_prefetch=1, grid=(rows // tr,),
        in_specs=[pl.BlockSpec((tr, c), lambda i, idx_ref: (i, 0))] * n,
        out_specs=[pl.BlockSpec((None, tr, c), lambda i, idx_ref: (idx_ref[0], i, 0))] * n)
    return pl.pallas_call(body, name=name, grid_spec=grid_spec,
                          out_shape=[jax.ShapeDtypeStruct((N_CHIPS, rows, c), dtype)] * n,
                          compiler_params=_params())(slot, *srcs)


def _adamw(groups, name):
    r, c = groups[0][0].shape
    n = len(groups)
    tr = _row_tile(r, c * 4, 1024 * 1024 // n)
    c1 = 1.0 - ADAM_B1 ** ADAM_STEP
    c2 = 1.0 - ADAM_B2 ** ADAM_STEP

    def body(*refs):
        for q in range(n):
            w_ref, g_ref, m_ref, v_ref = refs[4 * q:4 * q + 4]
            d_ref, nm_ref, nv_ref = refs[4 * n + 3 * q:4 * n + 3 * q + 3]
            gv = g_ref[...]
            nm = ADAM_B1 * m_ref[...] + (1.0 - ADAM_B1) * gv
            nv = ADAM_B2 * v_ref[...] + (1.0 - ADAM_B2) * (gv * gv)
            m_hat = nm / c1
            v_hat = nv / c2
            d_ref[...] = -ADAM_LR * (m_hat / (jnp.sqrt(v_hat) + ADAM_EPS) + ADAM_WD * w_ref[...])
            nm_ref[...] = nm
            nv_ref[...] = nv

    spec = pl.BlockSpec((tr, c), lambda i: (i, 0))
    sds = jax.ShapeDtypeStruct((r, c), F32)
    res = pl.pallas_call(
        body, name=name, grid=(r // tr,), out_shape=[sds] * (3 * n), in_specs=[spec] * (4 * n),
        out_specs=[spec] * (3 * n), compiler_params=_params())(*[a for grp in groups for a in grp])
    return [tuple(res[3 * q:3 * q + 3]) for q in range(n)]


def _grad_x(dproj, w_bm, x, dx2, g):
    t = dproj.shape[0]
    nb, d, wb = w_bm.shape
    tm = min(t, 1024)
    chunk = min(tm, 256)

    def body(dp_ref, w_ref, x_ref, dx2_ref, g_ref, gx_ref, dg_ref, acc_ref):
        i, k = pl.program_id(0), pl.program_id(1)

        @pl.when(k == 0)
        def _():
            acc_ref[...] = jnp.zeros_like(acc_ref)

        acc_ref[...] += _dot_nt(dp_ref[...], w_ref[...])

        @pl.when((i == 0) & (k == 0))
        def _():
            dg_ref[...] = jnp.zeros_like(dg_ref)

        @pl.when(k == nb - 1)
        def _():
            gv = g_ref[...]
            dg = jnp.zeros((1, d), F32)
            for r0 in range(0, tm, chunk):
                rows = slice(r0, r0 + chunk)
                dhv = acc_ref[rows, :]
                xv = x_ref[rows, :]
                r = lax.rsqrt(jnp.mean(xv * xv, axis=-1, keepdims=True) + NORM_EPS)
                nrm = xv * r
                dn = dhv * gv
                gx_ref[rows, :] = dx2_ref[rows, :] + r * (dn - nrm * jnp.mean(dn * nrm, axis=-1, keepdims=True))
                dg = dg + jnp.sum(dhv * nrm, axis=0, keepdims=True)
            dg_ref[...] += dg

    tile = pl.BlockSpec((tm, d), lambda i, k: (i, 0))
    tile1 = pl.BlockSpec((tm, d), lambda i, k: (i, 0), pipeline_mode=pl.Buffered(1))
    vec = pl.BlockSpec((1, d), lambda i, k: (0, 0))
    return pl.pallas_call(
        body, name="grad_x", grid=(t // tm, nb),
        out_shape=(jax.ShapeDtypeStruct((t, d), F32), jax.ShapeDtypeStruct((1, d), F32)),
        in_specs=[pl.BlockSpec((tm, wb), lambda i, k: (i, k)), pl.BlockSpec((None, d, wb), lambda i, k: (k, 0, 0)),
                  tile1, tile1, vec],
        out_specs=(tile, vec), scratch_shapes=[pltpu.VMEM((tm, d), F32)], compiler_params=_params())(dproj, w_bm, x, dx2, g)


def _shift_down(v, d, fill):
    n = v.shape[0]
    if d % 8 == 0:
        return jnp.concatenate([jnp.full((d,) + v.shape[1:], fill, v.dtype), v[: n - d]], axis=0)
    row = lax.broadcasted_iota(jnp.int32, v.shape, 0)
    return jnp.where(row >= d, pltpu.roll(v, d, axis=0), fill)


def _shift_up(v, d, fill):
    n = v.shape[0]
    if d % 8 == 0:
        return jnp.concatenate([v[d:], jnp.full((d,) + v.shape[1:], fill, v.dtype)], axis=0)
    row = lax.broadcasted_iota(jnp.int32, v.shape, 0)
    return jnp.where(row < n - d, pltpu.roll(v, n - d, axis=0), fill)


def _scan_log(a, b, shift):
    n = a.shape[0]
    d = 1
    while d < n:
        b = a * shift(b, d, 0.0) + b
        if 2 * d < n:
            a = a * shift(a, d, 1.0)
        d *= 2
    return b


SUBLANES = 8


def _scan(a, b, sa_ref, sb_ref, reverse):
    n, c = a.shape
    g = n // SUBLANES
    a3, b3 = a.reshape(g, SUBLANES, c), b.reshape(g, SUBLANES, c)
    sub = lax.broadcasted_iota(jnp.int32, a3.shape, 1)
    d = 1
    while d < SUBLANES:
        keep = (sub < SUBLANES - d) if reverse else (sub >= d)
        amount = SUBLANES - d if reverse else d
        b3 = a3 * jnp.where(keep, pltpu.roll(b3, amount, axis=1), 0.0) + b3
        a3 = a3 * jnp.where(keep, pltpu.roll(a3, amount, axis=1), 1.0)
        d *= 2
    sa_ref[...] = a3.reshape(n, c)
    sb_ref[...] = b3.reshape(n, c)
    edge = 0 if reverse else SUBLANES - 1
    shift = _shift_up if reverse else _shift_down
    totals = _scan_log(sa_ref[pl.ds(edge, g, stride=SUBLANES), :], sb_ref[pl.ds(edge, g, stride=SUBLANES), :], shift)
    carry = shift(totals, 1, 0.0)
    return (a3 * carry[:, None, :] + b3).reshape(n, c)


def _neg_expm1_twice(log_a, a):
    return -jnp.tanh(log_a) * (a * a + 1.0)


def _softplus(z):
    e = jnp.exp(-jnp.abs(z))
    w = 1.0 + e
    log1p = jnp.where(w == 1.0, e, jnp.log(w) * (e / jnp.where(w == 1.0, 1.0, w - 1.0)))
    return jnp.maximum(z, 0.0) + log1p


def _conv(up, cw, cb):
    out = cb + cw[CONV_WIDTH - 1:CONV_WIDTH, :] * up
    for j in range(CONV_WIDTH - 1):
        out = out + cw[j:j + 1, :] * _shift_down(up, CONV_WIDTH - 1 - j, 0.0)
    return out


def _lru_gates(u, wa_ref, ba_ref, wx_ref, bx_ref, lam_ref):
    ub = u.astype(BF16)
    r = _sigmoid(_dot(ub, wa_ref[...].astype(BF16)) + ba_ref[...])
    i = _sigmoid(_dot(ub, wx_ref[...].astype(BF16)) + bx_ref[...])
    sp = _softplus(-lam_ref[...])
    log_a = (-LRU_C) * r * sp
    a = jnp.exp(log_a)
    mult = jnp.sqrt(_neg_expm1_twice(log_a, a))
    return r, i, sp, a, mult


def _lru_specs(s):
    cb = RNN_BLOCK
    vec = pl.BlockSpec((1, cb), lambda n, b: (0, n))
    return dict(
        up=pl.BlockSpec((None, s, cb), lambda n, b: (b, 0, OFF_RNN_X // cb + n)),
        gr=pl.BlockSpec((None, s, cb), lambda n, b: (b, 0, OFF_RNN_G // cb + n)),
        act=pl.BlockSpec((None, s, cb), lambda n, b: (b, 0, n)),
        cw=pl.BlockSpec((CONV_WIDTH, cb), lambda n, b: (0, n)),
        vec=vec,
        wblk=pl.BlockSpec((None, cb, cb), lambda n, b: (n, 0, 0)),
    )


def _lru_fwd(proj3, cw, cb, wa, ba, wx, bx, lam, riders):
    bsz, s, _ = proj3.shape
    sp = _lru_specs(s)
    nr = len(riders)

    def body(up_ref, gr_ref, cw_ref, cb_ref, wa_ref, ba_ref, wx_ref, bx_ref, lam_ref, *refs):
        rider_in, (h_ref, y_ref), rider_out = refs[:nr], refs[nr:nr + 2], refs[nr + 2:2 * nr + 2]
        sa_ref, sb_ref = refs[2 * nr + 2:2 * nr + 4]
        start, pass_on, finish = _row_gather(rider_in, rider_out, *refs[2 * nr + 4:])
        step = pl.program_id(0) * bsz + pl.program_id(1)
        first, last = step == 0, step == N_RNN_BLOCKS * bsz - 1
        pl.when(first)(start)
        pl.when(step == (3 * N_RNN_BLOCKS * bsz) // 4)(pass_on)
        u = _conv(up_ref[...], cw_ref[...], cb_ref[...])
        _, i, _, a, mult = _lru_gates(u, wa_ref, ba_ref, wx_ref, bx_ref, lam_ref)
        h = _scan(a, mult * (i * u), sa_ref, sb_ref, reverse=False)
        h_ref[...] = h
        g = gr_ref[...]
        y_ref[...] = (h * (g * _sigmoid(g))).astype(BF16)
        pl.when(last)(finish)

    res = pl.pallas_call(
        body, name="lru_fwd", grid=(N_RNN_BLOCKS, bsz),
        out_shape=[jax.ShapeDtypeStruct((bsz, s, D_RNN), F32), jax.ShapeDtypeStruct((bsz, s, D_RNN), BF16)] + [
            jax.ShapeDtypeStruct(r.shape, r.dtype) for r in riders],
        in_specs=[sp["up"], sp["gr"], sp["cw"], sp["vec"], sp["wblk"], sp["vec"], sp["wblk"], sp["vec"], sp["vec"]] + [
            ANY] * nr,
        out_specs=[sp["act"], sp["act"]] + [ANY] * nr, input_output_aliases={9 + t: 2 + t for t in range(nr)},
        scratch_shapes=[pltpu.VMEM((s, RNN_BLOCK), F32)] * 2 + [pltpu.SemaphoreType.DMA((3 * nr,))] * 4,
        compiler_params=_params())(proj3, proj3, cw, cb, wa, ba, wx, bx, lam, *riders)
    return res[0], res[1], res[2:]


def _lru_bwd(proj3, h3, dy3, dproj3, cw, cb, wa, ba, wx, bx, lam):
    bsz, s, _ = proj3.shape
    sp = _lru_specs(s)
    n_steps = N_RNN_BLOCKS * bsz

    def body(up_ref, gr_ref, h_ref, dy_ref, cw_ref, cb_ref, wa_ref, ba_ref, wx_ref, bx_ref, lam_ref, dp_in,
             dp_ref, dcw_ref, dcb_ref, dwa_ref, dba_ref, dwx_ref, dbx_ref, dlam_ref, sa_ref, sb_ref,
             dup_st, dgr_st, o_sems):
        del dp_in
        blk = pl.program_id(0)
        b = pl.program_id(1)
        step = blk * bsz + b
        slot = step % 2
        stages = [dup_st, dgr_st]
        dsts = [dp_ref.at[b, :, pl.ds(pl.multiple_of(OFF_RNN_X + blk * RNN_BLOCK, LANES), RNN_BLOCK)],
                dp_ref.at[b, :, pl.ds(pl.multiple_of(OFF_RNN_G + blk * RNN_BLOCK, LANES), RNN_BLOCK)]]
        _staged_reuse(step, stages, dsts, o_sems, slot)
        up = up_ref[...]
        cwv = cw_ref[...]
        u = _conv(up, cwv, cb_ref[...])
        r, i, spv, a, mult = _lru_gates(u, wa_ref, ba_ref, wx_ref, bx_ref, lam_ref)
        h = h_ref[...]
        g = gr_ref[...]
        dy = dy_ref[...]
        sg = _sigmoid(g)
        dgr_st[slot] = (dy * h * (sg * (1.0 + g * (1.0 - sg)))).astype(BF16)
        dh = dy * (g * sg)
        adj = _scan(_shift_up(a, 1, 0.0), dh, sa_ref, sb_ref, reverse=True)
        da = adj * _shift_down(h, 1, 0.0)
        dmult = adj * (i * u)
        di = adj * mult * u
        du = adj * mult * i
        dla = da * a - dmult * (a * a) / mult
        dr = dla * ((-LRU_C) * spv)
        dsp = jnp.sum(dla * ((-LRU_C) * r), axis=0, keepdims=True)
        dza = dr * r * (1.0 - r)
        dzx = di * i * (1.0 - i)
        ub = u.astype(BF16)
        dzab = dza.astype(BF16)
        dzxb = dzx.astype(BF16)
        du = du + _dot_nt(dzab, wa_ref[...].astype(BF16)) + _dot_nt(dzxb, wx_ref[...].astype(BF16))
        dup = cwv[CONV_WIDTH - 1:CONV_WIDTH, :] * du
        for j in range(CONV_WIDTH - 1):
            dup = dup + cwv[j:j + 1, :] * _shift_up(du, CONV_WIDTH - 1 - j, 0.0)
        dup_st[slot] = dup.astype(BF16)
        _staged_flush(step, n_steps, stages, dsts, o_sems, slot)

        @pl.when(b == 0)
        def _():
            for ref in (dcw_ref, dcb_ref, dwa_ref, dba_ref, dwx_ref, dbx_ref, dlam_ref):
                ref[...] = jnp.zeros_like(ref)

        rows = [jnp.sum(du * _shift_down(up, CONV_WIDTH - 1 - j, 0.0), axis=0, keepdims=True)
                for j in range(CONV_WIDTH - 1)]
        rows.append(jnp.sum(du * up, axis=0, keepdims=True))
        dcw_ref[...] += jnp.concatenate(rows, axis=0)
        dcb_ref[...] += jnp.sum(du, axis=0, keepdims=True)
        dwa_ref[...] += _dot_tn(ub, dzab)
        dba_ref[...] += jnp.sum(dza, axis=0, keepdims=True)
        dwx_ref[...] += _dot_tn(ub, dzxb)
        dbx_ref[...] += jnp.sum(dzx, axis=0, keepdims=True)
        dlam_ref[...] += dsp * (-_sigmoid(-lam_ref[...]))

    vec = jax.ShapeDtypeStruct((1, D_RNN), F32)
    wsd = jax.ShapeDtypeStruct((N_RNN_BLOCKS, RNN_BLOCK, RNN_BLOCK), F32)
    return pl.pallas_call(
        body, name="lru_bwd", grid=(N_RNN_BLOCKS, bsz),
        out_shape=(jax.ShapeDtypeStruct(dproj3.shape, dproj3.dtype), jax.ShapeDtypeStruct((CONV_WIDTH, D_RNN), F32),
                   vec, wsd, vec, wsd, vec, vec),
        in_specs=[sp["up"], sp["gr"], sp["act"], sp["act"], sp["cw"], sp["vec"], sp["wblk"], sp["vec"],
                  sp["wblk"], sp["vec"], sp["vec"], ANY],
        out_specs=(ANY, sp["cw"], sp["vec"], sp["wblk"], sp["vec"], sp["wblk"], sp["vec"], sp["vec"]),
        input_output_aliases={11: 0},
        scratch_shapes=[pltpu.VMEM((s, RNN_BLOCK), F32)] * 2 + [pltpu.VMEM((2, s, RNN_BLOCK), BF16)] * 2 + [
            pltpu.SemaphoreType.DMA((4,))],
        compiler_params=_params())(proj3, proj3, h3, dy3, cw, cb, wa, ba, wx, bx, lam, dproj3)


def _rope_tables(s):
    half = ROPE_DIM // 2
    pos = jnp.arange(s, dtype=F32)
    inv_freq = ROPE_THETA ** (-jnp.arange(0, ROPE_DIM, 2, dtype=F32) / ROPE_DIM)
    ang = pos[:, None] * inv_freq[None, :]
    cos, sin = jnp.cos(ang), jnp.sin(ang)
    rest = HEAD_DIM - ROPE_DIM
    cos64 = jnp.concatenate([cos, cos, jnp.ones((s, rest), F32)], axis=1)
    sin64 = jnp.concatenate([-sin, sin, jnp.zeros((s, rest), F32)], axis=1)
    assert half * 2 == ROPE_DIM
    return jnp.tile(cos64, (1, LANES // HEAD_DIM)), jnp.tile(sin64, (1, LANES // HEAD_DIM))


def _swap_rot_halves(v):
    half = ROPE_DIM // 2
    lane = lax.broadcasted_iota(jnp.int32, v.shape, 1) % HEAD_DIM
    second = jnp.where(lane < ROPE_DIM, pltpu.roll(v, half, axis=1), 0.0)
    return jnp.where(lane < half, pltpu.roll(v, LANES - half, axis=1), second)


def _rope(v, cos, sin):
    tiles = []
    for t in range(v.shape[1] // LANES):
        vt = v[:, t * LANES:(t + 1) * LANES]
        tiles.append(vt * cos + _swap_rot_halves(vt) * sin)
    return tiles[0] if len(tiles) == 1 else jnp.concatenate(tiles, axis=1)


def _unrope(v, cos, sin):
    tiles = []
    for t in range(v.shape[1] // LANES):
        vt = v[:, t * LANES:(t + 1) * LANES]
        tiles.append(vt * cos + _swap_rot_halves(vt * sin))
    return tiles[0] if len(tiles) == 1 else jnp.concatenate(tiles, axis=1)


HEADS_PER_STEP = 8
QW = HEADS_PER_STEP * HEAD_DIM
N_PAIRS = N_Q_HEADS // HEADS_PER_STEP
Q_PER_KV = N_Q_HEADS // N_KV_HEADS
KV_PER_STEP = HEADS_PER_STEP // Q_PER_KV


QT_COLS = Q_PER_KV * ATT_BLK


def _attn_saved_shapes(bsz, s):
    nb = s // ATT_BLK
    pad = s + ATT_BLK
    return [(bsz, N_PAIRS, nb, LANES, QT_COLS), (bsz, N_PAIRS, KV_PER_STEP, pad, LANES),
            (bsz, N_PAIRS, KV_PER_STEP, pad, LANES), (bsz, N_PAIRS, LANES, pad)]


def _attn_specs(s, order):
    def mk(width, base, **kw):
        if order == "bp":
            return pl.BlockSpec((None, s, width), lambda b, p: (b, 0, base + p), **kw)
        return pl.BlockSpec((None, s, width), lambda p, b: (b, 0, base + p), **kw)

    def saved(shape, **kw):
        blk = (None, None) + shape[2:]
        zeros = (0,) * (len(shape) - 2)
        if order == "bp":
            return pl.BlockSpec(blk, lambda b, p: (b, p) + zeros, **kw)
        return pl.BlockSpec(blk, lambda p, b: (b, p) + zeros, **kw)

    one = dict(pipeline_mode=pl.Buffered(1))
    tbl = pl.BlockSpec((s, LANES), lambda *_: (0, 0))
    shapes = _attn_saved_shapes(1, s)
    return dict(q=mk(QW, OFF_Q // QW), k=mk(LANES, OFF_K // LANES), v=mk(LANES, OFF_V // LANES),
                g=mk(QW, OFF_ATTN_G // QW), act=mk(QW, 0), kv=mk(LANES, 0), tbl=tbl,
                g1=mk(QW, OFF_ATTN_G // QW, **one), act1=mk(QW, 0, **one),
                saved=[saved(sh) for sh in shapes], saved1=[saved(sh, **one) for sh in shapes],
                smem=pl.BlockSpec(memory_space=pltpu.SMEM))


def _to_qt(blk):
    rows = []
    for j in range(KV_PER_STEP):
        cols = []
        for tt in range(2):
            t = 2 * j + tt
            tr = blk[:, t * LANES:(t + 1) * LANES].T
            cols += [tr[0:HEAD_DIM, :], tr[HEAD_DIM:, :]]
        rows.append(jnp.concatenate(cols, axis=1))
    return jnp.concatenate(rows, axis=0)


def _from_qt(xt):
    tiles = []
    for j in range(KV_PER_STEP):
        for tt in range(2):
            g0 = 2 * tt
            pair = jnp.concatenate([xt[j * HEAD_DIM:(j + 1) * HEAD_DIM, (g0 + i) * ATT_BLK:(g0 + i + 1) * ATT_BLK]
                                    for i in range(2)], axis=0)
            tiles.append(pair.T)
    return jnp.concatenate(tiles, axis=1)


def _attn_prep(q_ref, k_ref, v_ref, cos_ref, sin_ref, qt_ref, km_ref, vm_ref, kt_ref, vt_ref, nb):
    zeros = jnp.zeros((ATT_BLK, LANES), BF16)
    for j in range(KV_PER_STEP):
        km_ref[j, 0:ATT_BLK, :] = zeros
        vm_ref[j, 0:ATT_BLK, :] = zeros
    kt_ref[:, 0:ATT_BLK] = zeros
    vt_ref[:, 0:ATT_BLK] = zeros
    head_of_lane = lax.broadcasted_iota(jnp.int32, (ATT_BLK, LANES), 1) // HEAD_DIM

    def prep(n, carry):
        r0 = pl.multiple_of(n * ATT_BLK, ATT_BLK)
        cs = cos_ref[pl.ds(r0, ATT_BLK), :]
        sn = sin_ref[pl.ds(r0, ATT_BLK), :]
        qt_ref[n] = _to_qt(_rope(q_ref[pl.ds(r0, ATT_BLK), :], cs, sn) * ATTN_SCALE).astype(BF16)
        k = _rope(k_ref[pl.ds(r0, ATT_BLK), :], cs, sn)
        v = v_ref[pl.ds(r0, ATT_BLK), :]
        for j in range(KV_PER_STEP):
            km_ref[j, pl.ds(r0 + ATT_BLK, ATT_BLK), :] = jnp.where(head_of_lane == j, k, 0.0).astype(BF16)
            vm_ref[j, pl.ds(r0 + ATT_BLK, ATT_BLK), :] = jnp.where(head_of_lane == j, v, 0.0).astype(BF16)
        kt_ref[:, pl.ds(r0 + ATT_BLK, ATT_BLK)] = k.T.astype(BF16)
        vt_ref[:, pl.ds(r0 + ATT_BLK, ATT_BLK)] = v.T.astype(BF16)
        return carry

    lax.fori_loop(0, nb, prep, 0)


def _from_prev_block():
    key = lax.broadcasted_iota(jnp.int32, (ATT_BLK, QT_COLS), 0)
    qry = lax.broadcasted_iota(jnp.int32, (ATT_BLK, QT_COLS), 1) % ATT_BLK
    return key > qry


def _fold(tile, prev, prev_bias=None):
    top = tile[:ATT_BLK] if prev_bias is None else tile[:ATT_BLK] + prev_bias
    return jnp.where(prev, top, tile[ATT_BLK:])


def _unfold(folded, prev):
    zero = jnp.zeros_like(folded)
    return jnp.concatenate([jnp.where(prev, folded, zero), jnp.where(prev, zero, folded)], axis=0).astype(BF16)


def _no_prev_bias(n):
    return jnp.where(n == 0, NEG_BIG, 0.0).astype(F32)


def _sink_row(sink_ref, first):
    return jnp.concatenate([jnp.full((1, ATT_BLK), sink_ref[first + g], F32) for g in range(Q_PER_KV)], axis=1)


def _softmax_cols(sc, sink):
    m = jnp.maximum(jnp.max(sc, axis=0, keepdims=True), sink)
    e = jnp.exp(sc - m)
    es = jnp.exp(sink - m)
    inv = 1.0 / (jnp.sum(e, axis=0, keepdims=True) + es)
    return e * inv, es * inv


def _attn_fwd(proj3, sinks, cosf, sinf):
    bsz, s, _ = proj3.shape
    nb = s // ATT_BLK
    sp = _attn_specs(s, "bp")

    def body(sink_ref, q_ref, k_ref, v_ref, g_ref, cos_ref, sin_ref, o_ref, y_ref, qt_sc, km_sc, vm_sc, kt_ref, vt_sc):
        p = pl.program_id(1)
        _attn_prep(q_ref, k_ref, v_ref, cos_ref, sin_ref, qt_sc, km_sc, vm_sc, kt_ref, vt_sc, nb)
        kv_row = lax.broadcasted_iota(jnp.int32, (LANES, QT_COLS), 0) // HEAD_DIM
        prev = _from_prev_block()

        def blk(n, carry):
            r0 = pl.multiple_of(n * ATT_BLK, ATT_BLK)
            bias = _no_prev_bias(n)
            rq = qt_sc[n]
            vt = vt_sc[:, pl.ds(r0, 2 * ATT_BLK)]
            ots = []
            for j in range(KV_PER_STEP):
                st = _dot(km_sc[j, pl.ds(r0, 2 * ATT_BLK), :], rq)
                pc, _ = _softmax_cols(_fold(st, prev, bias), _sink_row(sink_ref, p * HEADS_PER_STEP + j * Q_PER_KV))
                ots.append(_dot(vt, _unfold(pc, prev)))
            o = _from_qt(jnp.where(kv_row == 0, ots[0], ots[1]))
            o_ref[pl.ds(r0, ATT_BLK), :] = o
            g = g_ref[pl.ds(r0, ATT_BLK), :]
            y_ref[pl.ds(r0, ATT_BLK), :] = (o * (g * _sigmoid(g))).astype(BF16)
            return carry

        lax.fori_loop(0, nb, blk, 0, unroll=2)

    res = pl.pallas_call(
        body, name="attn_fwd", grid=(bsz, N_PAIRS),
        out_shape=[jax.ShapeDtypeStruct((bsz, s, D_ATTN), F32), jax.ShapeDtypeStruct((bsz, s, D_ATTN), BF16)] + [
            jax.ShapeDtypeStruct(sh, BF16) for sh in _attn_saved_shapes(bsz, s)],
        in_specs=[sp["smem"], sp["q"], sp["k"], sp["v"], sp["g"], sp["tbl"], sp["tbl"]],
        out_specs=[sp["act"], sp["act"]] + sp["saved"],
        scratch_shapes=[pltpu.VMEM((LANES, s + ATT_BLK), BF16)],
        compiler_params=_params())(sinks, proj3, proj3, proj3, proj3, cosf, sinf)
    return res[0], res[1], res[2:]


def _attn_bwd(proj3, saved, o3, dy3, dproj3, sinks, cosf, sinf):
    bsz, s, _ = proj3.shape
    nb = s // ATT_BLK
    assert nb % 2 == 0
    sp = _attn_specs(s, "pb")
    n_steps = N_PAIRS * bsz

    def body(sink_ref, qt_sc, km_sc, vm_sc, kt_sc, g_ref, o_ref, dy_ref, cos_ref, sin_ref, dp_in,
             dp_ref, ds_ref, dot_sc, dqt_sc, dk_sc, dv_sc, dq_st, dk_st, dv_st, dg_st, o_sems):
        del dp_in
        p = pl.program_id(0)
        b = pl.program_id(1)
        step = p * bsz + b
        slot = step % 2
        stages = [dq_st, dk_st, dv_st, dg_st]
        dsts = [dp_ref.at[b, :, pl.ds(pl.multiple_of(OFF_Q + p * QW, LANES), QW)],
                dp_ref.at[b, :, pl.ds(pl.multiple_of(OFF_K + p * LANES, LANES), LANES)],
                dp_ref.at[b, :, pl.ds(pl.multiple_of(OFF_V + p * LANES, LANES), LANES)],
                dp_ref.at[b, :, pl.ds(pl.multiple_of(OFF_ATTN_G + p * QW, LANES), QW)]]
        _staged_reuse(step, stages, dsts, o_sems, slot)
        dk_sc[...] = jnp.zeros_like(dk_sc)
        dv_sc[...] = jnp.zeros_like(dv_sc)

        def gate(n, carry):
            r0 = pl.multiple_of(n * ATT_BLK, ATT_BLK)
            g = g_ref[pl.ds(r0, ATT_BLK), :]
            dy = dy_ref[pl.ds(r0, ATT_BLK), :]
            sg = _sigmoid(g)
            dg_st[slot, pl.ds(r0, ATT_BLK), :] = (dy * o_ref[pl.ds(r0, ATT_BLK), :] * (sg * (1.0 + g * (1.0 - sg)))).astype(BF16)
            dot_sc[n] = _to_qt(dy * (g * sg)).astype(BF16)
            return carry

        lax.fori_loop(0, nb, gate, 0)
        kv_lane = lax.broadcasted_iota(jnp.int32, (2 * ATT_BLK, LANES), 1) // HEAD_DIM
        kv_row = lax.broadcasted_iota(jnp.int32, (LANES, QT_COLS), 0) // HEAD_DIM
        prev = _from_prev_block()

        def blk(n, acc):
            r0 = pl.multiple_of(n * ATT_BLK, ATT_BLK)
            bias = _no_prev_bias(n)
            rq = qt_sc[n]
            rd = dot_sc[n]
            kt = kt_sc[:, pl.ds(r0, 2 * ATT_BLK)]
            dvs, dks, dqs, new_acc = [], [], [], []
            for j in range(KV_PER_STEP):
                st = _dot(km_sc[j, pl.ds(r0, 2 * ATT_BLK), :], rq)
                pc, ps = _softmax_cols(_fold(st, prev, bias), _sink_row(sink_ref, p * HEADS_PER_STEP + j * Q_PER_KV))
                dpc = _fold(_dot(vm_sc[j, pl.ds(r0, 2 * ATT_BLK), :], rd), prev)
                delta = jnp.sum(pc * dpc, axis=0, keepdims=True)
                dst = _unfold(pc * (dpc - delta), prev)
                new_acc.append(acc[j] + ps * delta)
                dvs.append(_dot_nt(_unfold(pc, prev), rd))
                dks.append(_dot_nt(dst, rq))
                dqs.append(_dot(kt, dst))
            dv_sc[pl.ds(r0, 2 * ATT_BLK), :] += jnp.where(kv_lane == 0, dvs[0], dvs[1])
            dk_sc[pl.ds(r0, 2 * ATT_BLK), :] += jnp.where(kv_lane == 0, dks[0], dks[1])
            dqt_sc[n] = jnp.where(kv_row == 0, dqs[0], dqs[1]) * ATTN_SCALE
            return tuple(new_acc)

        def blk_pair(m, acc):
            return blk(2 * m + 1, blk(2 * m, acc))

        acc = lax.fori_loop(0, nb // 2, blk_pair, tuple(jnp.zeros((1, QT_COLS), F32) for _ in range(KV_PER_STEP)))
        lane1 = lax.broadcasted_iota(jnp.int32, (1, LANES), 1)
        dsink = jnp.zeros((1, LANES), F32)
        for j in range(KV_PER_STEP):
            for i in range(Q_PER_KV):
                part = jnp.sum(acc[j][:, i * ATT_BLK:(i + 1) * ATT_BLK], axis=1, keepdims=True)
                dsink = dsink - jnp.where(lane1 == j * Q_PER_KV + i, part, 0.0)

        @pl.when(b == 0)
        def _():
            ds_ref[...] = jnp.zeros_like(ds_ref)

        ds_ref[...] += dsink

        def post(n, carry):
            r0 = pl.multiple_of(n * ATT_BLK, ATT_BLK)
            cs = cos_ref[pl.ds(r0, ATT_BLK), :]
            sn = sin_ref[pl.ds(r0, ATT_BLK), :]
            dq_st[slot, pl.ds(r0, ATT_BLK), :] = _unrope(_from_qt(dqt_sc[n]), cs, sn).astype(BF16)
            dk_st[slot, pl.ds(r0, ATT_BLK), :] = _unrope(dk_sc[pl.ds(r0 + ATT_BLK, ATT_BLK), :], cs, sn).astype(BF16)
            dv_st[slot, pl.ds(r0, ATT_BLK), :] = dv_sc[pl.ds(r0 + ATT_BLK, ATT_BLK), :].astype(BF16)
            return carry

        lax.fori_loop(0, nb, post, 0)
        _staged_flush(step, n_steps, stages, dsts, o_sems, slot)

    n_in = 1 + len(saved) + 5
    return pl.pallas_call(
        body, name="attn_bwd", grid=(N_PAIRS, bsz),
        out_shape=(jax.ShapeDtypeStruct(dproj3.shape, dproj3.dtype), jax.ShapeDtypeStruct((N_PAIRS, 1, LANES), F32)),
        in_specs=[sp["smem"]] + sp["saved1"] + [sp["g1"], sp["act1"], sp["act1"], sp["tbl"], sp["tbl"], ANY],
        out_specs=(ANY, pl.BlockSpec((None, 1, LANES), lambda p, b: (p, 0, 0))),
        input_output_aliases={n_in: 0},
        scratch_shapes=[pltpu.VMEM((nb, LANES, QT_COLS), BF16),
                        pltpu.VMEM((nb, LANES, QT_COLS), F32),
                        pltpu.VMEM((s + ATT_BLK, LANES), F32),
                        pltpu.VMEM((s + ATT_BLK, LANES), F32),
                        pltpu.VMEM((2, s, QW), BF16), pltpu.VMEM((2, s, LANES), BF16),
                        pltpu.VMEM((2, s, LANES), BF16), pltpu.VMEM((2, s, QW), BF16),
                        pltpu.SemaphoreType.DMA((8,))],
        compiler_params=_params())(sinks, *saved, proj3, o3, dy3, cosf, sinf, dproj3)


def _staged_copies(stages, dsts, sems, slot):
    return [pltpu.make_async_copy(st.at[slot], dst, sems.at[slot * len(stages) + t])
            for t, (st, dst) in enumerate(zip(stages, dsts))]


def _staged_reuse(step, stages, dsts, sems, slot):
    @pl.when(step >= 2)
    def _():
        for cp in _staged_copies(stages, dsts, sems, slot):
            cp.wait()


def _staged_flush(step, n_steps, stages, dsts, sems, slot):
    for cp in _staged_copies(stages, dsts, sems, slot):
        cp.start()

    @pl.when(step == n_steps - 1)
    def _():
        for cp in _staged_copies(stages, dsts, sems, slot):
            cp.wait()
        if n_steps >= 2:
            for cp in _staged_copies(stages, dsts, sems, 1 - slot):
                cp.wait()


def _merge_fwd_bwd(x, tgt, y_rnn, y_attn, proj, w_r, w_a, w_o, gf):
    t, d = x.shape
    tm = min(t, 256)
    nt = t // tm

    hw = d // 2

    def body(x_ref, t_ref, yr_ref, ya_ref, mr0_ref, mr1_ref, ma0_ref, ma1_ref, wr_ref, wa_ref, wo_ref, gf_ref,
             dp_ref, dyr_ref, dya_ref, mg_ref, dx2_ref, dx2b_ref, dpr_ref, dpa_ref, loss_ref, dgf_ref, dmg_st, o_sems):
        i = pl.program_id(0)
        slot = i % 2
        dsts = [dp_ref.at[pl.ds(pl.multiple_of(i * tm, tm), tm), pl.ds(OFF_MERGE_R, 2 * d)]]
        _staged_reuse(i, [dmg_st], dsts, o_sems, slot)
        wr = wr_ref[...]
        wa = wa_ref[...]
        wo = wo_ref[...]
        gfv = gf_ref[...]
        pr = _dot(yr_ref[...], wr)
        pa = _dot(ya_ref[...], wa)
        sr = _sigmoid(jnp.concatenate([mr0_ref[...], mr1_ref[...]], axis=1))
        sa = _sigmoid(jnp.concatenate([ma0_ref[...], ma1_ref[...]], axis=1))
        mb = (sr * pr + sa * pa).astype(BF16)
        mg_ref[...] = mb
        x2 = x_ref[...] + _dot(mb, wo)
        r2 = lax.rsqrt(jnp.mean(x2 * x2, axis=-1, keepdims=True) + NORM_EPS)
        nrm = x2 * r2
        err = nrm * gfv - t_ref[...]
        dy = err * (1.0 / d)
        dn = dy * gfv
        dx2 = r2 * (dn - nrm * jnp.mean(dn * nrm, axis=-1, keepdims=True))
        dx2_ref[...] = dx2
        dx2b = dx2.astype(BF16)
        dx2b_ref[...] = dx2b
        dmerged = _dot_nt(dx2b, wo)
        dpr = (dmerged * sr).astype(BF16)
        dpa = (dmerged * sa).astype(BF16)
        dpr_ref[...] = dpr
        dpa_ref[...] = dpa
        dmg_st[slot, :, 0:d] = (dmerged * pr * (sr * (1.0 - sr))).astype(BF16)
        dmg_st[slot, :, d:2 * d] = (dmerged * pa * (sa * (1.0 - sa))).astype(BF16)
        _staged_flush(i, nt, [dmg_st], dsts, o_sems, slot)
        dyr_ref[...] = _dot_nt(dpr, wr)
        dya_ref[...] = _dot_nt(dpa, wa)

        @pl.when(i == 0)
        def _():
            loss_ref[...] = jnp.zeros_like(loss_ref)
            dgf_ref[...] = jnp.zeros_like(dgf_ref)

        loss_ref[...] += jnp.full((1, LANES), 0.5 / d, F32) * jnp.sum(err * err)
        dgf_ref[...] += jnp.sum(dy * nrm, axis=0, keepdims=True)

    tile = pl.BlockSpec((tm, d), lambda i: (i, 0))
    wsp = pl.BlockSpec((d, d), lambda i: (0, 0))

    def gate(col_blk):
        return pl.BlockSpec((tm, hw), lambda i: (i, col_blk))

    fb = jax.ShapeDtypeStruct((t, d), BF16)
    ff = jax.ShapeDtypeStruct((t, d), F32)
    return pl.pallas_call(
        body, name="merge_fwd_bwd", grid=(nt,),
        out_shape=(jax.ShapeDtypeStruct((t, D_IN), BF16), ff, ff, fb, ff, fb, fb, fb,
                   jax.ShapeDtypeStruct((1, LANES), F32), jax.ShapeDtypeStruct((1, d), F32)),
        in_specs=[tile, tile, tile, tile] + [gate(OFF_MERGE_R // hw + j) for j in range(4)] + [
            wsp, wsp, wsp, pl.BlockSpec((1, d), lambda i: (0, 0))],
        out_specs=(ANY, tile, tile, tile, tile, tile, tile, tile,
                   pl.BlockSpec((1, LANES), lambda i: (0, 0)), pl.BlockSpec((1, d), lambda i: (0, 0))),
        scratch_shapes=[pltpu.VMEM((2, tm, 2 * d), BF16), pltpu.SemaphoreType.DMA((2,))],
        compiler_params=_params())(x, tgt, y_rnn, y_attn, proj, proj, proj, proj, w_r, w_a, w_o, gf)


def _local_grads(x, tgt, h, proj, norm_g, w_in_bm, conv_w, conv_b, lru_w_a, lru_b_a, lru_w_x, lru_b_x, lam, sinks,
                 row_sharded, gf):
    bsz, s, d = x.shape
    t = bsz * s
    x2 = x.reshape(t, d)
    proj3 = proj.reshape(bsz, s, D_IN)
    h_lru, y_rnn, gathered = _lru_fwd(proj3, conv_w, conv_b, lru_w_a, lru_b_a, lru_w_x, lru_b_x, lam, row_sharded)
    w_r, w_a, w_o = (g.reshape(d, d) for g in gathered)
    cosf, sinf = _rope_tables(s)
    o_attn, y_attn, attn_saved = _attn_fwd(proj3, sinks, cosf, sinf)
    y_rnn2 = y_rnn.reshape(t, d)
    y_attn2 = y_attn.reshape(t, d)
    dproj, dyr, dya, merged, dx2, dx2b, dpr, dpa, loss, dgf = _merge_fwd_bwd(
        x2, tgt.reshape(t, d), y_rnn2, y_attn2, proj, w_r, w_a, w_o, gf)
    dproj3, dsink = _attn_bwd(proj3, attn_saved, o_attn, dya.reshape(bsz, s, d), dproj.reshape(bsz, s, D_IN),
                              sinks, cosf, sinf)
    dproj3, dcw, dcb, dwa, dba, dwx, dbx, dlam = _lru_bwd(
        proj3, h_lru, dyr.reshape(bsz, s, d), dproj3, conv_w, conv_b, lru_w_a, lru_b_a, lru_w_x, lru_b_x, lam)
    dproj = dproj3.reshape(t, D_IN)
    grad_x, dng = _grad_x(dproj, w_in_bm, x2, dx2, norm_g)
    small = dict(norm_g=dng, conv_w=dcw, conv_b=dcb, lru_w_a=dwa, lru_b_a=dba, lru_w_x=dwx, lru_b_x=dbx,
                 lru_lambda=dlam, attn_sinks=dsink[:, 0, :HEADS_PER_STEP].reshape(1, N_Q_HEADS), final_norm_g=dgf)
    squares = [(y_rnn2, dpr), (y_attn2, dpa), (merged, dx2b)]
    return loss[0, 0], grad_x.reshape(bsz, s, d), h, dproj, squares, small


ANY = pl.BlockSpec(memory_space=pl.ANY)


def _mesh_pos():
    return lax.axis_index("x"), lax.axis_index("y"), lax.axis_index("c")


def _remote(src, dst, send_sems, recv_sems, idx, peer):
    return pltpu.make_async_remote_copy(src_ref=src, dst_ref=dst, send_sem=send_sems.at[idx],
                                        recv_sem=recv_sems.at[idx], device_id=peer, device_id_type=MESH)


def _row_gather(ins, outs, send_sems, recv_sems, fsend_sems, frecv_sems):
    n = len(ins)
    x, y, c = _mesh_pos()
    me = 2 * x + y
    sib = (x, y, 1 - c)
    peers = [((x, 1 - y, c), me ^ 1), ((1 - x, y, c), me ^ 2), ((1 - x, 1 - y, c), me ^ 3)]

    def half(ref, slot, t, which):
        hr = ins[t].shape[1] // 2
        return ref.at[slot, pl.ds(pl.multiple_of(which * hr, 8), hr), :]

    def ici(t, k):
        peer, pj = peers[k]
        src = half(ins[t], me, t, c)
        return (_remote(src, half(outs[t], me, t, c), send_sems, recv_sems, 3 * t + k, peer),
                _remote(src, half(outs[t], pj, t, c), send_sems, recv_sems, 3 * t + k, peer))

    def forward(t, k):
        got = half(outs[t], peers[k][1], t, c)
        return (_remote(got, got, fsend_sems, frecv_sems, 3 * t + k, sib),
                _remote(got, half(outs[t], peers[k][1], t, 1 - c), fsend_sems, frecv_sems, 3 * t + k, sib))

    pairs = [(t, k) for t in range(n) for k in range(3)]

    def start():
        for t, k in pairs:
            ici(t, k)[0].start()

    def pass_on():
        for t, k in pairs:
            ici(t, k)[1].wait_recv()
            forward(t, k)[0].start()

    def finish():
        for t, k in pairs:
            ici(t, k)[0].wait_send()
            forward(t, k)[0].wait_send()
            forward(t, k)[1].wait_recv()

    return start, pass_on, finish


def _gather_in_proj(x, g, bufs, split, idx):
    t_tok, d = x.shape
    n = len(bufs)
    tm = min(t_tok, 1024)
    nt = t_tok // tm
    n_fwd = 3 * sum(split)
    assert split[0]

    def body(idx_ref, x_ref, g_ref, *refs):
        ins, proj_ref, h_out, outs = refs[:n], refs[n], refs[n + 1], refs[n + 2:2 * n + 2]
        wbuf, h_all, send_sems, recv_sems, fsend_sems, frecv_sems, l_sems = refs[2 * n + 2:]
        j, i = pl.program_id(0), pl.program_id(1)
        rows = pl.ds(pl.multiple_of(i * tm, tm), tm)
        x, y, c = _mesh_pos()
        me = 2 * x + y
        sib = (x, y, 1 - c)
        peers = [((x, 1 - y, c), me ^ 1), ((1 - x, y, c), me ^ 2), ((1 - x, 1 - y, c), me ^ 3)]

        def part(ref, slot, t, half):
            if not split[t]:
                return ref.at[slot]
            hr = bufs[t].shape[1] // 2
            return ref.at[slot, pl.ds(pl.multiple_of(half * hr, 8), hr), :]

        def land(t):
            return wbuf if t == 0 else outs[t]

        def ici(t, k):
            peer, pj = peers[k]
            src = part(ins[t], me, t, c)
            return (_remote(src, part(land(t), me, t, c), send_sems, recv_sems, 3 * t + k, peer),
                    _remote(src, part(land(t), pj, t, c), send_sems, recv_sems, 3 * t + k, peer))

        fwd_index = {}
        for t in range(n):
            if split[t]:
                for k in range(3):
                    fwd_index[(t, k)] = len(fwd_index)

        def forward(t, k):
            pj = peers[k][1]
            got = part(land(t), pj, t, c)
            f = fwd_index[(t, k)]
            return (_remote(got, got, fsend_sems, frecv_sems, f, sib),
                    _remote(got, part(land(t), pj, t, 1 - c), fsend_sems, frecv_sems, f, sib))

        def write_back(k):
            pj = peers[k][1]
            return pltpu.make_async_copy(wbuf.at[pj], outs[0].at[pj], l_sems.at[1 + k])

        relay_peer = ((x + c) % 2, (y + 1 - c) % 2, c)

        def relay():
            got = part(wbuf, me ^ (2 - c), 0, c)
            return (_remote(got, got, send_sems, recv_sems, 2, relay_peer),
                    _remote(got, part(wbuf, me ^ 3, 0, c), send_sems, recv_sems, 2, relay_peer))

        direct = [(t, k) for t in range(n) for k in range(3) if (t, k) != (0, 2)]

        @pl.when((j == 0) & (i == 0))
        def _():
            for t, k in direct:
                ici(t, k)[0].start()
            own = pltpu.make_async_copy(ins[0].at[me], wbuf.at[me], l_sems.at[0])
            own.start()
            own.wait()

        @pl.when((j == 1) & (i == 0))
        def _():
            pltpu.make_async_copy(h_all, h_out, l_sems.at[4]).start()
            for k in range(2):
                ici(0, k)[1].wait_recv()
            relay()[0].start()
            for k in range(2):
                forward(0, k)[0].start()
            forward(0, 0)[1].wait_recv()
            write_back(0).start()

        @pl.when((j == 2) & (i == 0))
        def _():
            forward(0, 1)[1].wait_recv()
            write_back(1).start()

        @pl.when((j == 3) & (i == 0))
        def _():
            relay()[1].wait_recv()
            forward(0, 2)[0].start()
            forward(0, 2)[1].wait_recv()
            write_back(2).start()

        @pl.when(j == 0)
        def _():
            xv = x_ref[...]
            r = lax.rsqrt(jnp.mean(xv * xv, axis=-1, keepdims=True) + NORM_EPS)
            h_all[rows, :] = (xv * r * g_ref[...]).astype(BF16)

        proj_ref[...] = _dot(h_all[rows, :], wbuf[me ^ j])

        @pl.when((j == N_CHIPS - 1) & (i == nt - 1))
        def _():
            pltpu.make_async_copy(h_all, h_out, l_sems.at[4]).wait()
            for t in range(1, n):
                for k in range(3):
                    ici(t, k)[1].wait_recv()
                    if split[t]:
                        forward(t, k)[0].start()
            relay()[0].wait_send()
            for t, k in direct:
                ici(t, k)[0].wait_send()
            for t in range(n):
                if split[t]:
                    for k in range(3):
                        forward(t, k)[0].wait_send()
                        if t > 0:
                            forward(t, k)[1].wait_recv()
            for k in range(3):
                write_back(k).wait()

    grid_spec = pltpu.PrefetchScalarGridSpec(
        num_scalar_prefetch=1, grid=(N_CHIPS, nt),
        in_specs=[pl.BlockSpec((tm, d), lambda j, i, idx_ref: (jnp.where(j == 0, i, nt - 1), 0)),
                  pl.BlockSpec((1, d), lambda j, i, idx_ref: (0, 0))] + [ANY] * n,
        out_specs=[pl.BlockSpec((tm, W_BLK), lambda j, i, idx_ref: (i, idx_ref[0] ^ j)), ANY] + [ANY] * n,
        scratch_shapes=[pltpu.VMEM(bufs[0].shape, bufs[0].dtype), pltpu.VMEM((t_tok, d), BF16),
                        pltpu.SemaphoreType.DMA((3 * n,)), pltpu.SemaphoreType.DMA((3 * n,)),
                        pltpu.SemaphoreType.DMA((n_fwd,)), pltpu.SemaphoreType.DMA((n_fwd,)),
                        pltpu.SemaphoreType.DMA((5,))])
    out_shape = [jax.ShapeDtypeStruct((t_tok, D_IN), F32), jax.ShapeDtypeStruct((t_tok, d), BF16)] + [
        jax.ShapeDtypeStruct(a.shape, a.dtype) for a in bufs]
    res = pl.pallas_call(
        body, name="gather_in_proj", grid_spec=grid_spec, out_shape=out_shape,
        input_output_aliases={3 + t: 2 + t for t in range(n)}, compiler_params=_params())(idx, x, g, *bufs)
    return res[1], res[0], res[2:]


def _row_tile(rows, row_bytes, cap_bytes=2 * 1024 * 1024):
    best = None
    for tr in range(8, rows + 1, 8):
        if rows % tr == 0 and tr * row_bytes <= cap_bytes:
            best = tr
    return best if best is not None else rows


XOR_ORDER = (3, 2, 1)


def _grads_reduce_scatter(h, dproj, squares, small, idx):
    t, d = h.shape
    nsq = len(squares)
    hr = d // 2
    qr = ROW_BLK // 2
    tk = min(t, 1024)
    nk = t // tk
    last = N_CHIPS - 1
    n_phase = 3

    def dest(s, idx_ref):
        xo = jnp.where(s == 0, XOR_ORDER[0], jnp.where(s == 1, XOR_ORDER[1], jnp.where(s == 2, XOR_ORDER[2], 0)))
        return idx_ref[0] ^ xo

    def k_sq(p, k):
        return jnp.where(p == 0, k, nk - 1)

    def k_w(p, k):
        return jnp.where(p == 0, 0, k)

    in_specs = [
        pl.BlockSpec((tk, hr), lambda s, p, k, idx_ref: (k_w(p, k), (1 - idx_ref[1] + jnp.maximum(p - 1, 0)) % 2)),
        pl.BlockSpec((tk, W_BLK), lambda s, p, k, idx_ref: (k_w(p, k), dest(s, idx_ref)))]
    for q in range(nsq):
        in_specs.append(pl.BlockSpec((tk, ROW_BLK), lambda s, p, k, idx_ref: (k_sq(p, k), dest(s, idx_ref))))
        in_specs.append(pl.BlockSpec((tk, d), lambda s, p, k, idx_ref: (k_sq(p, k), 0)))

    def body(idx_ref, *refs):
        nj = 1 + nsq
        h_ref, dp_ref = refs[0], refs[1]
        sq_in = refs[2:2 + 2 * nsq]
        small_in = refs[2 * nj]
        outs = refs[2 * nj + 1:3 * nj + 2]
        landing = refs[3 * nj + 2:4 * nj + 3]
        sc = refs[4 * nj + 3:]
        acc_w, xr_w, sb_w = sc[0:3]
        sq_sc = [sc[3 + 3 * q:6 + 3 * q] for q in range(nsq)]
        sm, smx = sc[3 * nj:3 * nj + 2]
        x_send, x_recv, i_send, i_recv, f_send, f_recv, o_sem, l_sem = sc[3 * nj + 2:]
        s, p, k = pl.program_id(0), pl.program_id(1), pl.program_id(2)
        x, y, c = _mesh_pos()
        sib = (x, y, 1 - c)
        peers = [((1 - x) if xo & 2 else x, (1 - y) if xo & 1 else y, c) for xo in XOR_ORDER]
        slot = s % 2
        mine_w = pl.ds(pl.multiple_of(c * hr, 8), hr)
        theirs_w = pl.ds(pl.multiple_of((1 - c) * hr, 8), hr)
        mine_q = pl.ds(pl.multiple_of(c * qr, 8), qr)
        theirs_q = pl.ds(pl.multiple_of((1 - c) * qr, 8), qr)

        def exch(j, src, dst):
            return _remote(src, dst, x_send, x_recv, 2 * j + slot, sib)

        sbufs = [sb_w] + [sq_sc[q][2] for q in range(nsq)]

        def ici(j, ss):
            return _remote(sbufs[j].at[ss], landing[j].at[ss], i_send, i_recv, last * j + ss, peers[ss])

        def exchanges():
            cps = [exch(0, acc_w.at[0], xr_w.at[slot])]
            cps += [exch(1 + q, sq_sc[q][0].at[theirs_q, :], sq_sc[q][1].at[slot]) for q in range(nsq)]
            return cps

        def small_send(ss):
            return _remote(sm.at[c], landing[nj].at[ss], i_send, i_recv, last * nj + ss, peers[ss])

        def small_start():
            load = pltpu.make_async_copy(small_in, sm, l_sem.at[nj + 1])
            load.start()
            load.wait()
            swap = _remote(sm, smx.at[pl.ds(0, 2)], x_send, x_recv, 2 * nj, sib)
            swap.start()
            swap.wait_recv()
            swap.wait_send()
            sm[...] = sm[...] + smx[0:2]
            for ss in range(last):
                small_send(ss).start()

        def pair_ref(j):
            return acc_w.at[1] if j == 0 else sq_sc[j - 1][0].at[mine_q, :]

        def sq_phase():
            pl.when((s == 0) & (k == 0))(small_start)
            for q in range(nsq):
                acc = sq_sc[q][0]

                @pl.when(k == 0)
                def _():
                    acc[...] = jnp.zeros((ROW_BLK, d), F32)

                acc[...] += _dot_tn(sq_in[2 * q][...], sq_in[2 * q + 1][...])

            @pl.when(k == nk - 1)
            def _():
                for cp in exchanges()[1:]:
                    cp.start()

        def w_phase(hf):
            @pl.when(k == 0)
            def _():
                acc_w[hf] = jnp.zeros((hr, W_BLK), F32)

            acc_w[hf] += _dot_tn(h_ref[...], dp_ref[...])

            @pl.when(k == nk - 1)
            def _():
                if hf == 0:
                    exchanges()[0].start()
                else:
                    finish_step()

        def finish_step():
            for cp in exchanges():
                cp.wait_recv()
                cp.wait_send()
            acc_w[1] += xr_w[slot]
            for q in range(nsq):
                sq_sc[q][0][mine_q, :] += sq_sc[q][1][slot]
            for ss in range(last):
                @pl.when(s == ss)
                def _():
                    for j in range(nj):
                        sbufs[j][ss] = pair_ref(j)[...].astype(BF16)
                        ici(j, ss).start()

            @pl.when(s == last)
            def _():
                for ss in range(last):
                    for j in range(nj):
                        ici(j, ss).wait_recv()
                        ici(j, ss).wait_send()
                    small_send(ss).wait_recv()
                    small_send(ss).wait_send()
                stage = [pltpu.make_async_copy(landing[j], sbufs[j], l_sem.at[j]) for j in range(nj)]
                stage.append(pltpu.make_async_copy(landing[nj], smx, l_sem.at[nj]))
                for cp in stage:
                    cp.start()
                for j in range(nj):
                    stage[j].wait()
                    total = pair_ref(j)[...]
                    for ss in range(last):
                        total = total + sbufs[j][ss].astype(F32)
                    pair_ref(j)[...] = total
                stage[nj].wait()
                by_xor = {xo: smx[ss] for ss, xo in enumerate(XOR_ORDER)}
                sm[c] = (sm[c] + by_xor[1]) + (by_xor[2] + by_xor[3])
                done = [(acc_w.at[1], outs[0].at[mine_w, :], outs[0].at[theirs_w, :])]
                done += [(pair_ref(1 + q), outs[1 + q].at[mine_q, :], outs[1 + q].at[theirs_q, :]) for q in range(nsq)]
                done.append((sm.at[c], outs[nj].at[c], outs[nj].at[1 - c]))
                copies = []
                for j, (src, mine, theirs) in enumerate(done):
                    keep = pltpu.make_async_copy(src, mine, o_sem.at[j])
                    give = _remote(src, mine, f_send, f_recv, j, sib)
                    take = _remote(src, theirs, f_send, f_recv, j, sib)
                    keep.start()
                    give.start()
                    copies.append((keep, give, take))
                for keep, give, take in copies:
                    keep.wait()
                    give.wait_send()
                    take.wait_recv()

        pl.when(p == 0)(sq_phase)
        for hf in range(2):
            pl.when(p == 1 + hf)(functools.partial(w_phase, hf))

    nj = 1 + nsq
    scratch = [pltpu.VMEM((2, hr, W_BLK), F32), pltpu.VMEM((2, hr, W_BLK), F32), pltpu.VMEM((last, hr, W_BLK), BF16)]
    for _ in range(nsq):
        scratch += [pltpu.VMEM((ROW_BLK, d), F32), pltpu.VMEM((2, qr, d), F32), pltpu.VMEM((last, qr, d), BF16)]
    scratch += [pltpu.VMEM((2, PK_HALF, LANES), F32), pltpu.VMEM((last, PK_HALF, LANES), F32)]
    scratch += [pltpu.SemaphoreType.DMA((2 * nj + 1,)), pltpu.SemaphoreType.DMA((2 * nj + 1,)),
                pltpu.SemaphoreType.DMA((last * (nj + 1),)), pltpu.SemaphoreType.DMA((last * (nj + 1),)),
                pltpu.SemaphoreType.DMA((nj + 1,)), pltpu.SemaphoreType.DMA((nj + 1,)),
                pltpu.SemaphoreType.DMA((nj + 1,)), pltpu.SemaphoreType.DMA((nj + 2,))]
    grid_spec = pltpu.PrefetchScalarGridSpec(
        num_scalar_prefetch=1, grid=(N_CHIPS, n_phase, nk), in_specs=in_specs + [ANY],
        out_specs=[ANY] * (2 * nj + 2), scratch_shapes=scratch)
    out_shape = [jax.ShapeDtypeStruct((d, W_BLK), F32)] + [jax.ShapeDtypeStruct((ROW_BLK, d), F32)] * nsq
    out_shape.append(jax.ShapeDtypeStruct((2, PK_HALF, LANES), F32))
    out_shape += [jax.ShapeDtypeStruct((last, hr, W_BLK), BF16)] + [jax.ShapeDtypeStruct((last, qr, d), BF16)] * nsq
    out_shape.append(jax.ShapeDtypeStruct((last, PK_HALF, LANES), F32))
    flat = [a for pair in squares for a in pair]
    res = pl.pallas_call(body, name="grads_reduce_scatter", grid_spec=grid_spec, out_shape=out_shape,
                         compiler_params=_params())(idx, h, dproj, *flat, small)
    return res[:nj + 1]


_VEC_NAMES = ("norm_g", "conv_b", "lru_b_a", "lru_b_x", "lru_lambda", "final_norm_g")


def _pack_small(p, conv_full=None, scalar=None):
    rows = [p["lru_w_a"].reshape(PK_WX - PK_WA, LANES), p["lru_w_x"].reshape(PK_VEC - PK_WX, LANES)]
    rows += [p[k].reshape(8, LANES) for k in _VEC_NAMES]
    rows.append(jnp.pad(p["attn_sinks"].reshape(1, N_Q_HEADS), ((0, 7), (0, LANES - N_Q_HEADS))))
    rows.append(jnp.zeros((32, LANES), F32) if conv_full is None else conv_full.reshape(32, LANES))
    tail = PK_ROWS - PK_SCALAR
    if scalar is None:
        rows.append(jnp.zeros((tail, LANES), F32))
    else:
        rows.append(jnp.pad(scalar.reshape(1, 1), ((0, tail - 1), (0, LANES - 1))))
    return jnp.concatenate(rows, axis=0)


def _unpack_small(pk, like):
    out = {"lru_w_a": pk[PK_WA:PK_WX].reshape(like["lru_w_a"].shape),
           "lru_w_x": pk[PK_WX:PK_VEC].reshape(like["lru_w_x"].shape)}
    for j, k in enumerate(_VEC_NAMES):
        out[k] = pk[PK_VEC + 8 * j:PK_VEC + 8 * j + 8].reshape(like[k].shape)
    out["attn_sinks"] = pk[PK_SINK:PK_SINK + 1, :N_Q_HEADS].reshape(like["attn_sinks"].shape)
    return out


_WEIGHTS = ("norm_g", "w_in", "conv_w", "conv_b", "lru_w_a", "lru_b_a", "lru_w_x", "lru_b_x", "lru_lambda",
            "attn_sinks", "w_rnn_out", "w_attn_out", "w_o", "final_norm_g")
_SMALL = ("norm_g", "conv_b", "lru_w_a", "lru_b_a", "lru_w_x", "lru_b_x", "lru_lambda", "attn_sinks", "final_norm_g")
_ROW_SHARDED = ("w_rnn_out", "w_attn_out", "w_o")


def kernel(x, norm_g, w_in, conv_w, conv_b, lru_w_a, lru_b_a, lru_w_x, lru_b_x, lru_lambda, attn_sinks, w_rnn_out, w_attn_out, w_o, final_norm_g, loss_target, m_norm_g, m_w_in, m_conv_w, m_conv_b, m_lru_w_a, m_lru_b_a, m_lru_w_x, m_lru_b_x, m_lru_lambda, m_attn_sinks, m_w_rnn_out, m_w_attn_out, m_w_o, m_final_norm_g, v_norm_g, v_w_in, v_conv_w, v_conv_b, v_lru_w_a, v_lru_b_a, v_lru_w_x, v_lru_b_x, v_lru_lambda, v_attn_sinks, v_w_rnn_out, v_w_attn_out, v_w_o, v_final_norm_g):
    w = dict(norm_g=norm_g, w_in=w_in, conv_w=conv_w, conv_b=conv_b, lru_w_a=lru_w_a, lru_b_a=lru_b_a, lru_w_x=lru_w_x,
             lru_b_x=lru_b_x, lru_lambda=lru_lambda, attn_sinks=attn_sinks, w_rnn_out=w_rnn_out, w_attn_out=w_attn_out,
             w_o=w_o, final_norm_g=final_norm_g)
    m = dict(norm_g=m_norm_g, w_in=m_w_in, conv_w=m_conv_w, conv_b=m_conv_b, lru_w_a=m_lru_w_a, lru_b_a=m_lru_b_a,
             lru_w_x=m_lru_w_x, lru_b_x=m_lru_b_x, lru_lambda=m_lru_lambda, attn_sinks=m_attn_sinks,
             w_rnn_out=m_w_rnn_out, w_attn_out=m_w_attn_out, w_o=m_w_o, final_norm_g=m_final_norm_g)
    v = dict(norm_g=v_norm_g, w_in=v_w_in, conv_w=v_conv_w, conv_b=v_conv_b, lru_w_a=v_lru_w_a, lru_b_a=v_lru_b_a,
             lru_w_x=v_lru_w_x, lru_b_x=v_lru_b_x, lru_lambda=v_lru_lambda, attn_sinks=v_attn_sinks,
             w_rnn_out=v_w_rnn_out, w_attn_out=v_w_attn_out, w_o=v_w_o, final_norm_g=v_final_norm_g)
    mx, my, mc = _mesh_pos()
    me = 2 * mx + my
    d = D_MODEL

    my_chip = jnp.reshape(me, (1,)).astype(jnp.int32)
    (buf_in,) = _put_slots([w["w_in"][0]], my_chip, BF16, "cast_w_in")
    (buf_cw,) = _put_slots([w["conv_w"][0]], my_chip, F32, "slot_conv_w")
    row_sharded = _put_slots([w[k][0] for k in _ROW_SHARDED], my_chip, BF16, "cast_row_sharded")
    h, proj, (g_in, g_cw) = _gather_in_proj(x.reshape(-1, d), w["norm_g"], [buf_in, buf_cw], [True, False], my_chip)
    conv_full = g_cw.transpose(1, 0, 2).reshape(CONV_WIDTH, D_RNN)

    loss_local, grad_x, h, dproj, squares, gsmall = _local_grads(
        x, loss_target, h, proj, w["norm_g"], g_in, conv_full, w["conv_b"], w["lru_w_a"][0], w["lru_b_a"], w["lru_w_x"][0],
        w["lru_b_x"], w["lru_lambda"], w["attn_sinks"][0], row_sharded, w["final_norm_g"].reshape(1, d))
    gpack = _pack_small(gsmall, gsmall["conv_w"], loss_local).reshape(2, PK_HALF, LANES)
    f_in, f_r, f_a, f_o, spack = _grads_reduce_scatter(h, dproj, squares, gpack, jnp.stack([me, mc]).astype(jnp.int32))
    spack = spack.reshape(PK_ROWS, LANES)
    loss = spack[PK_SCALAR, 0]

    grads = _unpack_small(spack, w)
    conv_all = spack[PK_CONV:PK_CONV + 32].reshape(CONV_WIDTH, D_RNN)
    grads["conv_w"] = lax.dynamic_slice_in_dim(conv_all, me * (D_RNN // N_CHIPS), D_RNN // N_CHIPS, axis=1)[None]
    grads["w_in"] = f_in[None]
    grads["w_rnn_out"], grads["w_attn_out"], grads["w_o"] = f_r[None], f_a[None], f_o[None]

    delta, new_m, new_v = {}, {}, {}
    def group(k):
        return w[k][0], grads[k][0], m[k][0], v[k][0]

    for names, call in ((("w_in",), "adamw_w_in"), (_ROW_SHARDED, "adamw_row_sharded")):
        for k, (dk, mk, vk) in zip(names, _adamw([group(k) for k in names], call)):
            delta[k], new_m[k], new_v[k] = dk[None], mk[None], vk[None]
    shp = (2 * CONV_WIDTH, LANES)
    ((dk, mk, vk),) = _adamw([tuple(a.reshape(shp) for a in (w["conv_w"], grads["conv_w"], m["conv_w"], v["conv_w"]))],
                             "adamw_conv_w")
    delta["conv_w"], new_m["conv_w"], new_v["conv_w"] = (a.reshape(w["conv_w"].shape) for a in (dk, mk, vk))
    ((dk, mk, vk),) = _adamw([(_pack_small(w), spack, _pack_small(m), _pack_small(v))], "adamw_small")
    for src, dst in ((dk, delta), (mk, new_m), (vk, new_v)):
        dst.update(_unpack_small(src, w))

    return (loss, grad_x, *[grads[k] for k in _WEIGHTS], *[delta[k] for k in _WEIGHTS],
            *[new_m[k] for k in _WEIGHTS], *[new_v[k] for k in _WEIGHTS])
```

```python
import functools
import math

import jax
import jax.numpy as jnp
from jax import lax
from jax.experimental import pallas as pl
from jax.experimental.pallas import tpu as pltpu

F32 = jnp.float32
BF16 = jnp.bfloat16
MESH = pl.DeviceIdType.MESH

D_MODEL = 1024
D_RNN = 1024
N_RNN_BLOCKS = 8
RNN_BLOCK = D_RNN // N_RNN_BLOCKS
CONV_WIDTH = 4
LRU_C = 8.0
HEAD_DIM = 64
N_Q_HEADS = 16
N_KV_HEADS = 4
D_ATTN = N_Q_HEADS * HEAD_DIM
D_KV = N_KV_HEADS * HEAD_DIM
WINDOW = 128
ROPE_DIM = HEAD_DIM // 4
ROPE_THETA = 500000.0
NORM_EPS = 1e-6
OFF_RNN_X = 0
OFF_RNN_G = OFF_RNN_X + D_RNN
OFF_Q = OFF_RNN_G + D_RNN
OFF_K = OFF_Q + D_ATTN
OFF_V = OFF_K + D_KV
OFF_ATTN_G = OFF_V + D_KV
OFF_MERGE_R = OFF_ATTN_G + D_ATTN
OFF_MERGE_A = OFF_MERGE_R + D_MODEL
D_IN = OFF_MERGE_A + D_MODEL

ADAM_LR = 0.001
ADAM_B1 = 0.9
ADAM_B2 = 0.999
ADAM_EPS = 1e-08
ADAM_WD = 0.01
ADAM_STEP = 10

N_CHIPS = 4
W_BLK = D_IN // N_CHIPS
ROW_BLK = D_MODEL // N_CHIPS
LANES = 128
ATT_BLK = 128
VMEM_LIMIT = 56 * 1024 * 1024
NEG_BIG = -1e30
ATTN_SCALE = 1.0 / math.sqrt(HEAD_DIM)

PK_WA = 0
PK_WX = PK_WA + N_RNN_BLOCKS * RNN_BLOCK
PK_VEC = PK_WX + N_RNN_BLOCKS * RNN_BLOCK
PK_SINK = PK_VEC + 6 * 8
PK_CONV = PK_SINK + 8
PK_SCALAR = PK_CONV + 32
PK_ROWS = PK_SCALAR + 8
PK_HALF = PK_ROWS // 2


def _params(**kw):
    return pltpu.CompilerParams(vmem_limit_bytes=VMEM_LIMIT, **kw)


def _sigmoid(z):
    return 1.0 / (1.0 + jnp.exp(-z))


def _dot(a, b):
    return jnp.dot(a, b, preferred_element_type=F32)


def _dot_nt(a, b):
    return lax.dot_general(a, b, (((1,), (1,)), ((), ())), preferred_element_type=F32)


def _dot_tn(a, b):
    return lax.dot_general(a, b, (((0,), (0,)), ((), ())), preferred_element_type=F32)


def _put_slots(srcs, slot, dtype, name):
    rows, c = srcs[0].shape
    n = len(srcs)
    tr = _row_tile(rows, c * 4)

    def body(idx_ref, *refs):
        for s_ref, o_ref in zip(refs[:n], refs[n:]):
            o_ref[...] = s_ref[...].astype(dtype)

    grid_spec = pltpu.PrefetchScalarGridSpec(
        num_scalar_prefetch=1, grid=(rows // tr,),
        in_specs=[pl.BlockSpec((tr, c), lambda i, idx_ref: (i, 0))] * n,
        out_specs=[pl.BlockSpec((None, tr, c), lambda i, idx_ref: (idx_ref[0], i, 0))] * n)
    return pl.pallas_call(body, name=name, grid_spec=grid_spec,
                          out_shape=[jax.ShapeDtypeStruct((N_CHIPS, rows, c), dtype)] * n,
                          compiler_params=_params())(slot, *srcs)


def _adamw(groups, name):
    r, c = groups[0][0].shape
    n = len(groups)
    tr = _row_tile(r, c * 4, 1024 * 1024 // n)
    c1 = 1.0 - ADAM_B1 ** ADAM_STEP
    c2 = 1.0 - ADAM_B2 ** ADAM_STEP

    def body(*refs):
        for q in range(n):
            w_ref, g_ref, m_ref, v_ref = refs[4 * q:4 * q + 4]
            d_ref, nm_ref, nv_ref = refs[4 * n + 3 * q:4 * n + 3 * q + 3]
            gv = g_ref[...]
            nm = ADAM_B1 * m_ref[...] + (1.0 - ADAM_B1) * gv
            nv = ADAM_B2 * v_ref[...] + (1.0 - ADAM_B2) * (gv * gv)
            m_hat = nm / c1
            v_hat = nv / c2
            d_ref[...] = -ADAM_LR * (m_hat / (jnp.sqrt(v_hat) + ADAM_EPS) + ADAM_WD * w_ref[...])
            nm_ref[...] = nm
            nv_ref[...] = nv

    spec = pl.BlockSpec((tr, c), lambda i: (i, 0))
    sds = jax.ShapeDtypeStruct((r, c), F32)
    res = pl.pallas_call(
        body, name=name, grid=(r // tr,), out_shape=[sds] * (3 * n), in_specs=[spec] * (4 * n),
        out_specs=[spec] * (3 * n), compiler_params=_params())(*[a for grp in groups for a in grp])
    return [tuple(res[3 * q:3 * q + 3]) for q in range(n)]


def _grad_x(dproj, w_bm, x, dx2, g):
    t = dproj.shape[0]
    nb, d, wb = w_bm.shape
    tm = min(t, 512)
    chunk = min(tm, 256)

    def body(dp_ref, w_ref, x_ref, dx2_ref, g_ref, gx_ref, dg_ref, acc_ref):
        i, k = pl.program_id(0), pl.program_id(1)

        @pl.when(k == 0)
        def _():
            acc_ref[...] = jnp.zeros_like(acc_ref)

        @pl.when(k < nb - 1)
        def _():
            acc_ref[...] += _dot_nt(dp_ref[...], w_ref[...])

        @pl.when((i == 0) & (k == 0))
        def _():
            dg_ref[...] = jnp.zeros_like(dg_ref)

        @pl.when(k == nb - 1)
        def _():
            gv = g_ref[...]
            wv = w_ref[...]
            dg = jnp.zeros((1, d), F32)
            for r0 in range(0, tm, chunk):
                rows = slice(r0, r0 + chunk)
                dhv = acc_ref[rows, :] + _dot_nt(dp_ref[rows, :], wv)
                xv = x_ref[rows, :]
                r = lax.rsqrt(jnp.mean(xv * xv, axis=-1, keepdims=True) + NORM_EPS)
                nrm = xv * r
                dn = dhv * gv
                gx_ref[rows, :] = dx2_ref[rows, :] + r * (dn - nrm * jnp.mean(dn * nrm, axis=-1, keepdims=True))
                dg = dg + jnp.sum(dhv * nrm, axis=0, keepdims=True)
            dg_ref[...] += dg

    tile = pl.BlockSpec((tm, d), lambda i, k: (i, 0))
    vec = pl.BlockSpec((1, d), lambda i, k: (0, 0))
    return pl.pallas_call(
        body, name="grad_x", grid=(t // tm, nb),
        out_shape=(jax.ShapeDtypeStruct((t, d), F32), jax.ShapeDtypeStruct((1, d), F32)),
        in_specs=[pl.BlockSpec((tm, wb), lambda i, k: (i, k)), pl.BlockSpec((None, d, wb), lambda i, k: (k, 0, 0)),
                  tile, tile, vec],
        out_specs=(tile, vec), scratch_shapes=[pltpu.VMEM((tm, d), F32)], compiler_params=_params())(dproj, w_bm, x, dx2, g)


def _shift_down(v, d, fill):
    n = v.shape[0]
    if d % 8 == 0:
        return jnp.concatenate([jnp.full((d,) + v.shape[1:], fill, v.dtype), v[: n - d]], axis=0)
    row = lax.broadcasted_iota(jnp.int32, v.shape, 0)
    return jnp.where(row >= d, pltpu.roll(v, d, axis=0), fill)


def _shift_up(v, d, fill):
    n = v.shape[0]
    if d % 8 == 0:
        return jnp.concatenate([v[d:], jnp.full((d,) + v.shape[1:], fill, v.dtype)], axis=0)
    row = lax.broadcasted_iota(jnp.int32, v.shape, 0)
    return jnp.where(row < n - d, pltpu.roll(v, n - d, axis=0), fill)


def _scan_log(a, b, shift):
    n = a.shape[0]
    d = 1
    while d < n:
        b = a * shift(b, d, 0.0) + b
        if 2 * d < n:
            a = a * shift(a, d, 1.0)
        d *= 2
    return b


SUBLANES = 8


def _scan(a, b, sa_ref, sb_ref, reverse):
    n, c = a.shape
    g = n // SUBLANES
    a3, b3 = a.reshape(g, SUBLANES, c), b.reshape(g, SUBLANES, c)
    sub = lax.broadcasted_iota(jnp.int32, a3.shape, 1)
    d = 1
    while d < SUBLANES:
        keep = (sub < SUBLANES - d) if reverse else (sub >= d)
        amount = SUBLANES - d if reverse else d
        b3 = a3 * jnp.where(keep, pltpu.roll(b3, amount, axis=1), 0.0) + b3
        a3 = a3 * jnp.where(keep, pltpu.roll(a3, amount, axis=1), 1.0)
        d *= 2
    sa_ref[...] = a3.reshape(n, c)
    sb_ref[...] = b3.reshape(n, c)
    edge = 0 if reverse else SUBLANES - 1
    shift = _shift_up if reverse else _shift_down
    totals = _scan_log(sa_ref[pl.ds(edge, g, stride=SUBLANES), :], sb_ref[pl.ds(edge, g, stride=SUBLANES), :], shift)
    carry = shift(totals, 1, 0.0)
    return (a3 * carry[:, None, :] + b3).reshape(n, c)


def _neg_expm1_twice(log_a, a):
    return -jnp.tanh(log_a) * (a * a + 1.0)


def _softplus(z):
    e = jnp.exp(-jnp.abs(z))
    w = 1.0 + e
    log1p = jnp.where(w == 1.0, e, jnp.log(w) * (e / jnp.where(w == 1.0, 1.0, w - 1.0)))
    return jnp.maximum(z, 0.0) + log1p


def _conv(up, cw, cb):
    out = cb + cw[CONV_WIDTH - 1:CONV_WIDTH, :] * up
    for j in range(CONV_WIDTH - 1):
        out = out + cw[j:j + 1, :] * _shift_down(up, CONV_WIDTH - 1 - j, 0.0)
    return out


def _lru_gates(u, wa_ref, ba_ref, wx_ref, bx_ref, lam_ref):
    ub = u.astype(BF16)
    r = _sigmoid(_dot(ub, wa_ref[...].astype(BF16)) + ba_ref[...])
    i = _sigmoid(_dot(ub, wx_ref[...].astype(BF16)) + bx_ref[...])
    sp = _softplus(-lam_ref[...])
    log_a = (-LRU_C) * r * sp
    a = jnp.exp(log_a)
    mult = jnp.sqrt(_neg_expm1_twice(log_a, a))
    return r, i, sp, a, mult


def _lru_specs(s):
    cb = RNN_BLOCK
    vec = pl.BlockSpec((1, cb), lambda n, b: (0, n))
    return dict(
        up=pl.BlockSpec((None, s, cb), lambda n, b: (b, 0, OFF_RNN_X // cb + n)),
        gr=pl.BlockSpec((None, s, cb), lambda n, b: (b, 0, OFF_RNN_G // cb + n)),
        act=pl.BlockSpec((None, s, cb), lambda n, b: (b, 0, n)),
        cw=pl.BlockSpec((CONV_WIDTH, cb), lambda n, b: (0, n)),
        vec=vec,
        wblk=pl.BlockSpec((None, cb, cb), lambda n, b: (n, 0, 0)),
    )


def _lru_fwd(proj3, cw, cb, wa, ba, wx, bx, lam, riders):
    bsz, s, _ = proj3.shape
    sp = _lru_specs(s)
    nr = len(riders)

    def body(up_ref, gr_ref, cw_ref, cb_ref, wa_ref, ba_ref, wx_ref, bx_ref, lam_ref, *refs):
        rider_in, (h_ref, y_ref), rider_out = refs[:nr], refs[nr:nr + 2], refs[nr + 2:2 * nr + 2]
        sa_ref, sb_ref = refs[2 * nr + 2:2 * nr + 4]
        start, pass_on, finish = _row_gather(rider_in, rider_out, *refs[2 * nr + 4:])
        step = pl.program_id(0) * bsz + pl.program_id(1)
        first, last = step == 0, step == N_RNN_BLOCKS * bsz - 1
        pl.when(first)(start)
        pl.when(step == (3 * N_RNN_BLOCKS * bsz) // 4)(pass_on)
        u = _conv(up_ref[...], cw_ref[...], cb_ref[...])
        _, i, _, a, mult = _lru_gates(u, wa_ref, ba_ref, wx_ref, bx_ref, lam_ref)
        h = _scan(a, mult * (i * u), sa_ref, sb_ref, reverse=False)
        h_ref[...] = h
        g = gr_ref[...]
        y_ref[...] = (h * (g * _sigmoid(g))).astype(BF16)
        pl.when(last)(finish)

    res = pl.pallas_call(
        body, name="lru_fwd", grid=(N_RNN_BLOCKS, bsz),
        out_shape=[jax.ShapeDtypeStruct((bsz, s, D_RNN), F32), jax.ShapeDtypeStruct((bsz, s, D_RNN), BF16)] + [
            jax.ShapeDtypeStruct(r.shape, r.dtype) for r in riders],
        in_specs=[sp["up"], sp["gr"], sp["cw"], sp["vec"], sp["wblk"], sp["vec"], sp["wblk"], sp["vec"], sp["vec"]] + [
            ANY] * nr,
        out_specs=[sp["act"], sp["act"]] + [ANY] * nr, input_output_aliases={9 + t: 2 + t for t in range(nr)},
        scratch_shapes=[pltpu.VMEM((s, RNN_BLOCK), F32)] * 2 + [pltpu.SemaphoreType.DMA((3 * nr,))] * 4,
        compiler_params=_params())(proj3, proj3, cw, cb, wa, ba, wx, bx, lam, *riders)
    return res[0], res[1], res[2:]


def _lru_bwd(proj3, h3, dy3, dproj3, cw, cb, wa, ba, wx, bx, lam):
    bsz, s, _ = proj3.shape
    sp = _lru_specs(s)
    n_steps = N_RNN_BLOCKS * bsz

    def body(up_ref, gr_ref, h_ref, dy_ref, cw_ref, cb_ref, wa_ref, ba_ref, wx_ref, bx_ref, lam_ref, dp_in,
             dp_ref, dcw_ref, dcb_ref, dwa_ref, dba_ref, dwx_ref, dbx_ref, dlam_ref, sa_ref, sb_ref,
             dup_st, dgr_st, o_sems):
        del dp_in
        blk = pl.program_id(0)
        b = pl.program_id(1)
        step = blk * bsz + b
        slot = step % 2
        stages = [dup_st, dgr_st]
        dsts = [dp_ref.at[b, :, pl.ds(pl.multiple_of(OFF_RNN_X + blk * RNN_BLOCK, LANES), RNN_BLOCK)],
                dp_ref.at[b, :, pl.ds(pl.multiple_of(OFF_RNN_G + blk * RNN_BLOCK, LANES), RNN_BLOCK)]]
        _staged_reuse(step, stages, dsts, o_sems, slot)
        up = up_ref[...]
        cwv = cw_ref[...]
        u = _conv(up, cwv, cb_ref[...])
        r, i, spv, a, mult = _lru_gates(u, wa_ref, ba_ref, wx_ref, bx_ref, lam_ref)
        h = h_ref[...]
        g = gr_ref[...]
        dy = dy_ref[...]
        sg = _sigmoid(g)
        dgr_st[slot] = (dy * h * (sg * (1.0 + g * (1.0 - sg)))).astype(BF16)
        dh = dy * (g * sg)
        adj = _scan(_shift_up(a, 1, 0.0), dh, sa_ref, sb_ref, reverse=True)
        da = adj * _shift_down(h, 1, 0.0)
        dmult = adj * (i * u)
        di = adj * mult * u
        du = adj * mult * i
        dla = da * a - dmult * (a * a) / mult
        dr = dla * ((-LRU_C) * spv)
        dsp = jnp.sum(dla * ((-LRU_C) * r), axis=0, keepdims=True)
        dza = dr * r * (1.0 - r)
        dzx = di * i * (1.0 - i)
        ub = u.astype(BF16)
        dzab = dza.astype(BF16)
        dzxb = dzx.astype(BF16)
        du = du + _dot_nt(dzab, wa_ref[...].astype(BF16)) + _dot_nt(dzxb, wx_ref[...].astype(BF16))
        dup = cwv[CONV_WIDTH - 1:CONV_WIDTH, :] * du
        for j in range(CONV_WIDTH - 1):
            dup = dup + cwv[j:j + 1, :] * _shift_up(du, CONV_WIDTH - 1 - j, 0.0)
        dup_st[slot] = dup.astype(BF16)
        _staged_flush(step, n_steps, stages, dsts, o_sems, slot)

        @pl.when(b == 0)
        def _():
            for ref in (dcw_ref, dcb_ref, dwa_ref, dba_ref, dwx_ref, dbx_ref, dlam_ref):
                ref[...] = jnp.zeros_like(ref)

        rows = [jnp.sum(du * _shift_down(up, CONV_WIDTH - 1 - j, 0.0), axis=0, keepdims=True)
                for j in range(CONV_WIDTH - 1)]
        rows.append(jnp.sum(du * up, axis=0, keepdims=True))
        dcw_ref[...] += jnp.concatenate(rows, axis=0)
        dcb_ref[...] += jnp.sum(du, axis=0, keepdims=True)
        dwa_ref[...] += _dot_tn(ub, dzab)
        dba_ref[...] += jnp.sum(dza, axis=0, keepdims=True)
        dwx_ref[...] += _dot_tn(ub, dzxb)
        dbx_ref[...] += jnp.sum(dzx, axis=0, keepdims=True)
        dlam_ref[...] += dsp * (-_sigmoid(-lam_ref[...]))

    vec = jax.ShapeDtypeStruct((1, D_RNN), F32)
    wsd = jax.ShapeDtypeStruct((N_RNN_BLOCKS, RNN_BLOCK, RNN_BLOCK), F32)
    return pl.pallas_call(
        body, name="lru_bwd", grid=(N_RNN_BLOCKS, bsz),
        out_shape=(jax.ShapeDtypeStruct(dproj3.shape, dproj3.dtype), jax.ShapeDtypeStruct((CONV_WIDTH, D_RNN), F32),
                   vec, wsd, vec, wsd, vec, vec),
        in_specs=[sp["up"], sp["gr"], sp["act"], sp["act"], sp["cw"], sp["vec"], sp["wblk"], sp["vec"],
                  sp["wblk"], sp["vec"], sp["vec"], ANY],
        out_specs=(ANY, sp["cw"], sp["vec"], sp["wblk"], sp["vec"], sp["wblk"], sp["vec"], sp["vec"]),
        input_output_aliases={11: 0},
        scratch_shapes=[pltpu.VMEM((s, RNN_BLOCK), F32)] * 2 + [pltpu.VMEM((2, s, RNN_BLOCK), BF16)] * 2 + [
            pltpu.SemaphoreType.DMA((4,))],
        compiler_params=_params())(proj3, proj3, h3, dy3, cw, cb, wa, ba, wx, bx, lam, dproj3)


def _rope_tables(s):
    half = ROPE_DIM // 2
    pos = jnp.arange(s, dtype=F32)
    inv_freq = ROPE_THETA ** (-jnp.arange(0, ROPE_DIM, 2, dtype=F32) / ROPE_DIM)
    ang = pos[:, None] * inv_freq[None, :]
    cos, sin = jnp.cos(ang), jnp.sin(ang)
    rest = HEAD_DIM - ROPE_DIM
    cos64 = jnp.concatenate([cos, cos, jnp.ones((s, rest), F32)], axis=1)
    sin64 = jnp.concatenate([-sin, sin, jnp.zeros((s, rest), F32)], axis=1)
    assert half * 2 == ROPE_DIM
    return jnp.tile(cos64, (1, LANES // HEAD_DIM)), jnp.tile(sin64, (1, LANES // HEAD_DIM))


def _swap_rot_halves(v):
    half = ROPE_DIM // 2
    lane = lax.broadcasted_iota(jnp.int32, v.shape, 1) % HEAD_DIM
    second = jnp.where(lane < ROPE_DIM, pltpu.roll(v, half, axis=1), 0.0)
    return jnp.where(lane < half, pltpu.roll(v, LANES - half, axis=1), second)


def _rope(v, cos, sin):
    tiles = []
    for t in range(v.shape[1] // LANES):
        vt = v[:, t * LANES:(t + 1) * LANES]
        tiles.append(vt * cos + _swap_rot_halves(vt) * sin)
    return tiles[0] if len(tiles) == 1 else jnp.concatenate(tiles, axis=1)


def _unrope(v, cos, sin):
    tiles = []
    for t in range(v.shape[1] // LANES):
        vt = v[:, t * LANES:(t + 1) * LANES]
        tiles.append(vt * cos + _swap_rot_halves(vt * sin))
    return tiles[0] if len(tiles) == 1 else jnp.concatenate(tiles, axis=1)


HEADS_PER_STEP = 8
QW = HEADS_PER_STEP * HEAD_DIM
N_PAIRS = N_Q_HEADS // HEADS_PER_STEP
Q_PER_KV = N_Q_HEADS // N_KV_HEADS
KV_PER_STEP = HEADS_PER_STEP // Q_PER_KV


QT_COLS = Q_PER_KV * ATT_BLK


def _attn_saved_shapes(bsz, s):
    nb = s // ATT_BLK
    pad = s + ATT_BLK
    return [(bsz, N_PAIRS, nb, LANES, QT_COLS), (bsz, N_PAIRS, KV_PER_STEP, pad, LANES),
            (bsz, N_PAIRS, KV_PER_STEP, pad, LANES), (bsz, N_PAIRS, LANES, pad)]


def _attn_specs(s, order):
    def mk(width, base, **kw):
        if order == "bp":
            return pl.BlockSpec((None, s, width), lambda b, p: (b, 0, base + p), **kw)
        return pl.BlockSpec((None, s, width), lambda p, b: (b, 0, base + p), **kw)

    def saved(shape, **kw):
        blk = (None, None) + shape[2:]
        zeros = (0,) * (len(shape) - 2)
        if order == "bp":
            return pl.BlockSpec(blk, lambda b, p: (b, p) + zeros, **kw)
        return pl.BlockSpec(blk, lambda p, b: (b, p) + zeros, **kw)

    one = dict(pipeline_mode=pl.Buffered(1))
    tbl = pl.BlockSpec((s, LANES), lambda *_: (0, 0))
    shapes = _attn_saved_shapes(1, s)
    return dict(q=mk(QW, OFF_Q // QW), k=mk(LANES, OFF_K // LANES), v=mk(LANES, OFF_V // LANES),
                g=mk(QW, OFF_ATTN_G // QW), act=mk(QW, 0), kv=mk(LANES, 0), tbl=tbl,
                g1=mk(QW, OFF_ATTN_G // QW, **one), act1=mk(QW, 0, **one),
                saved=[saved(sh) for sh in shapes], saved1=[saved(sh, **one) for sh in shapes],
                smem=pl.BlockSpec(memory_space=pltpu.SMEM))


def _to_qt(blk):
    rows = []
    for j in range(KV_PER_STEP):
        cols = []
        for tt in range(2):
            t = 2 * j + tt
            tr = blk[:, t * LANES:(t + 1) * LANES].T
            cols += [tr[0:HEAD_DIM, :], tr[HEAD_DIM:, :]]
        rows.append(jnp.concatenate(cols, axis=1))
    return jnp.concatenate(rows, axis=0)


def _from_qt(xt):
    tiles = []
    for j in range(KV_PER_STEP):
        for tt in range(2):
            g0 = 2 * tt
            pair = jnp.concatenate([xt[j * HEAD_DIM:(j + 1) * HEAD_DIM, (g0 + i) * ATT_BLK:(g0 + i + 1) * ATT_BLK]
                                    for i in range(2)], axis=0)
            tiles.append(pair.T)
    return jnp.concatenate(tiles, axis=1)


def _attn_prep(q_ref, k_ref, v_ref, cos_ref, sin_ref, qt_ref, km_ref, vm_ref, kt_ref, vt_ref, nb):
    zeros = jnp.zeros((ATT_BLK, LANES), BF16)
    for j in range(KV_PER_STEP):
        km_ref[j, 0:ATT_BLK, :] = zeros
        vm_ref[j, 0:ATT_BLK, :] = zeros
    kt_ref[:, 0:ATT_BLK] = zeros
    vt_ref[:, 0:ATT_BLK] = zeros
    head_of_lane = lax.broadcasted_iota(jnp.int32, (ATT_BLK, LANES), 1) // HEAD_DIM

    def prep(n, carry):
        r0 = pl.multiple_of(n * ATT_BLK, ATT_BLK)
        cs = cos_ref[pl.ds(r0, ATT_BLK), :]
        sn = sin_ref[pl.ds(r0, ATT_BLK), :]
        qt_ref[n] = _to_qt(_rope(q_ref[pl.ds(r0, ATT_BLK), :], cs, sn) * ATTN_SCALE).astype(BF16)
        k = _rope(k_ref[pl.ds(r0, ATT_BLK), :], cs, sn)
        v = v_ref[pl.ds(r0, ATT_BLK), :]
        for j in range(KV_PER_STEP):
            km_ref[j, pl.ds(r0 + ATT_BLK, ATT_BLK), :] = jnp.where(head_of_lane == j, k, 0.0).astype(BF16)
            vm_ref[j, pl.ds(r0 + ATT_BLK, ATT_BLK), :] = jnp.where(head_of_lane == j, v, 0.0).astype(BF16)
        kt_ref[:, pl.ds(r0 + ATT_BLK, ATT_BLK)] = k.T.astype(BF16)
        vt_ref[:, pl.ds(r0 + ATT_BLK, ATT_BLK)] = v.T.astype(BF16)
        return carry

    lax.fori_loop(0, nb, prep, 0)


def _from_prev_block():
    key = lax.broadcasted_iota(jnp.int32, (ATT_BLK, QT_COLS), 0)
    qry = lax.broadcasted_iota(jnp.int32, (ATT_BLK, QT_COLS), 1) % ATT_BLK
    return key > qry


def _fold(tile, prev, prev_bias=None):
    top = tile[:ATT_BLK] if prev_bias is None else tile[:ATT_BLK] + prev_bias
    return jnp.where(prev, top, tile[ATT_BLK:])


def _unfold(folded, prev):
    zero = jnp.zeros_like(folded)
    return jnp.concatenate([jnp.where(prev, folded, zero), jnp.where(prev, zero, folded)], axis=0).astype(BF16)


def _no_prev_bias(n):
    return jnp.where(n == 0, NEG_BIG, 0.0).astype(F32)


def _sink_row(sink_ref, first):
    return jnp.concatenate([jnp.full((1, ATT_BLK), sink_ref[first + g], F32) for g in range(Q_PER_KV)], axis=1)


def _softmax_cols(sc, sink):
    m = jnp.maximum(jnp.max(sc, axis=0, keepdims=True), sink)
    e = jnp.exp(sc - m)
    es = jnp.exp(sink - m)
    inv = 1.0 / (jnp.sum(e, axis=0, keepdims=True) + es)
    return e * inv, es * inv


def _attn_fwd(proj3, sinks, cosf, sinf):
    bsz, s, _ = proj3.shape
    nb = s // ATT_BLK
    sp = _attn_specs(s, "bp")

    def body(sink_ref, q_ref, k_ref, v_ref, g_ref, cos_ref, sin_ref, o_ref, y_ref, qt_sc, km_sc, vm_sc, kt_ref, vt_sc):
        p = pl.program_id(1)
        _attn_prep(q_ref, k_ref, v_ref, cos_ref, sin_ref, qt_sc, km_sc, vm_sc, kt_ref, vt_sc, nb)
        kv_row = lax.broadcasted_iota(jnp.int32, (LANES, QT_COLS), 0) // HEAD_DIM
        prev = _from_prev_block()

        def blk(n, carry):
            r0 = pl.multiple_of(n * ATT_BLK, ATT_BLK)
            bias = _no_prev_bias(n)
            rq = qt_sc[n]
            vt = vt_sc[:, pl.ds(r0, 2 * ATT_BLK)]
            ots = []
            for j in range(KV_PER_STEP):
                st = _dot(km_sc[j, pl.ds(r0, 2 * ATT_BLK), :], rq)
                pc, _ = _softmax_cols(_fold(st, prev, bias), _sink_row(sink_ref, p * HEADS_PER_STEP + j * Q_PER_KV))
                ots.append(_dot(vt, _unfold(pc, prev)))
            o = _from_qt(jnp.where(kv_row == 0, ots[0], ots[1]))
            o_ref[pl.ds(r0, ATT_BLK), :] = o
            g = g_ref[pl.ds(r0, ATT_BLK), :]
            y_ref[pl.ds(r0, ATT_BLK), :] = (o * (g * _sigmoid(g))).astype(BF16)
            return carry

        lax.fori_loop(0, nb, blk, 0, unroll=4)

    res = pl.pallas_call(
        body, name="attn_fwd", grid=(bsz, N_PAIRS),
        out_shape=[jax.ShapeDtypeStruct((bsz, s, D_ATTN), F32), jax.ShapeDtypeStruct((bsz, s, D_ATTN), BF16)] + [
            jax.ShapeDtypeStruct(sh, BF16) for sh in _attn_saved_shapes(bsz, s)],
        in_specs=[sp["smem"], sp["q"], sp["k"], sp["v"], sp["g"], sp["tbl"], sp["tbl"]],
        out_specs=[sp["act"], sp["act"]] + sp["saved"],
        scratch_shapes=[pltpu.VMEM((LANES, s + ATT_BLK), BF16)],
        compiler_params=_params())(sinks, proj3, proj3, proj3, proj3, cosf, sinf)
    return res[0], res[1], res[2:]


def _attn_bwd(proj3, saved, o3, dy3, dproj3, sinks, cosf, sinf):
    bsz, s, _ = proj3.shape
    nb = s // ATT_BLK
    assert nb % 2 == 0
    sp = _attn_specs(s, "pb")
    n_steps = N_PAIRS * bsz

    def body(sink_ref, qt_sc, km_sc, vm_sc, kt_sc, g_ref, o_ref, dy_ref, cos_ref, sin_ref, dp_in,
             dp_ref, ds_ref, dot_sc, dqt_sc, dk_sc, dv_sc, dq_st, dk_st, dv_st, dg_st, o_sems):
        del dp_in
        p = pl.program_id(0)
        b = pl.program_id(1)
        step = p * bsz + b
        slot = step % 2
        stages = [dq_st, dk_st, dv_st, dg_st]
        dsts = [dp_ref.at[b, :, pl.ds(pl.multiple_of(OFF_Q + p * QW, LANES), QW)],
                dp_ref.at[b, :, pl.ds(pl.multiple_of(OFF_K + p * LANES, LANES), LANES)],
                dp_ref.at[b, :, pl.ds(pl.multiple_of(OFF_V + p * LANES, LANES), LANES)],
                dp_ref.at[b, :, pl.ds(pl.multiple_of(OFF_ATTN_G + p * QW, LANES), QW)]]
        _staged_reuse(step, stages, dsts, o_sems, slot)
        dk_sc[...] = jnp.zeros_like(dk_sc)
        dv_sc[...] = jnp.zeros_like(dv_sc)

        def gate(n, carry):
            r0 = pl.multiple_of(n * ATT_BLK, ATT_BLK)
            g = g_ref[pl.ds(r0, ATT_BLK), :]
            dy = dy_ref[pl.ds(r0, ATT_BLK), :]
            sg = _sigmoid(g)
            dg_st[slot, pl.ds(r0, ATT_BLK), :] = (dy * o_ref[pl.ds(r0, ATT_BLK), :] * (sg * (1.0 + g * (1.0 - sg)))).astype(BF16)
            dot_sc[n] = _to_qt(dy * (g * sg)).astype(BF16)
            return carry

        lax.fori_loop(0, nb, gate, 0)
        kv_lane = lax.broadcasted_iota(jnp.int32, (2 * ATT_BLK, LANES), 1) // HEAD_DIM
        kv_row = lax.broadcasted_iota(jnp.int32, (LANES, QT_COLS), 0) // HEAD_DIM
        prev = _from_prev_block()

        def blk(n, acc):
            r0 = pl.multiple_of(n * ATT_BLK, ATT_BLK)
            bias = _no_prev_bias(n)
            rq = qt_sc[n]
            rd = dot_sc[n]
            kt = kt_sc[:, pl.ds(r0, 2 * ATT_BLK)]
            dvs, dks, dqs, new_acc = [], [], [], []
            for j in range(KV_PER_STEP):
                st = _dot(km_sc[j, pl.ds(r0, 2 * ATT_BLK), :], rq)
                pc, ps = _softmax_cols(_fold(st, prev, bias), _sink_row(sink_ref, p * HEADS_PER_STEP + j * Q_PER_KV))
                dpc = _fold(_dot(vm_sc[j, pl.ds(r0, 2 * ATT_BLK), :], rd), prev)
                delta = jnp.sum(pc * dpc, axis=0, keepdims=True)
                dst = _unfold(pc * (dpc - delta), prev)
                new_acc.append(acc[j] + ps * delta)
                dvs.append(_dot_nt(_unfold(pc, prev), rd))
                dks.append(_dot_nt(dst, rq))
                dqs.append(_dot(kt, dst))
            dv_sc[pl.ds(r0, 2 * ATT_BLK), :] += jnp.where(kv_lane == 0, dvs[0], dvs[1])
            dk_sc[pl.ds(r0, 2 * ATT_BLK), :] += jnp.where(kv_lane == 0, dks[0], dks[1])
            dqt_sc[n] = jnp.where(kv_row == 0, dqs[0], dqs[1]) * ATTN_SCALE
            return tuple(new_acc)

        per_trip = 4 if nb % 4 == 0 else 2

        def blk_group(m, acc):
            for u in range(per_trip):
                acc = blk(per_trip * m + u, acc)
            return acc

        acc = lax.fori_loop(0, nb // per_trip, blk_group, tuple(jnp.zeros((1, QT_COLS), F32) for _ in range(KV_PER_STEP)))
        lane1 = lax.broadcasted_iota(jnp.int32, (1, LANES), 1)
        dsink = jnp.zeros((1, LANES), F32)
        for j in range(KV_PER_STEP):
            for i in range(Q_PER_KV):
                part = jnp.sum(acc[j][:, i * ATT_BLK:(i + 1) * ATT_BLK], axis=1, keepdims=True)
                dsink = dsink - jnp.where(lane1 == j * Q_PER_KV + i, part, 0.0)

        @pl.when(b == 0)
        def _():
            ds_ref[...] = jnp.zeros_like(ds_ref)

        ds_ref[...] += dsink

        def post(n, carry):
            r0 = pl.multiple_of(n * ATT_BLK, ATT_BLK)
            cs = cos_ref[pl.ds(r0, ATT_BLK), :]
            sn = sin_ref[pl.ds(r0, ATT_BLK), :]
            dq_st[slot, pl.ds(r0, ATT_BLK), :] = _unrope(_from_qt(dqt_sc[n]), cs, sn).astype(BF16)
            dk_st[slot, pl.ds(r0, ATT_BLK), :] = _unrope(dk_sc[pl.ds(r0 + ATT_BLK, ATT_BLK), :], cs, sn).astype(BF16)
            dv_st[slot, pl.ds(r0, ATT_BLK), :] = dv_sc[pl.ds(r0 + ATT_BLK, ATT_BLK), :].astype(BF16)
            return carry

        lax.fori_loop(0, nb, post, 0)
        _staged_flush(step, n_steps, stages, dsts, o_sems, slot)

    n_in = 1 + len(saved) + 5
    return pl.pallas_call(
        body, name="attn_bwd", grid=(N_PAIRS, bsz),
        out_shape=(jax.ShapeDtypeStruct(dproj3.shape, dproj3.dtype), jax.ShapeDtypeStruct((N_PAIRS, 1, LANES), F32)),
        in_specs=[sp["smem"]] + sp["saved1"] + [sp["g1"], sp["act1"], sp["act1"], sp["tbl"], sp["tbl"], ANY],
        out_specs=(ANY, pl.BlockSpec((None, 1, LANES), lambda p, b: (p, 0, 0))),
        input_output_aliases={n_in: 0},
        scratch_shapes=[pltpu.VMEM((nb, LANES, QT_COLS), BF16),
                        pltpu.VMEM((nb, LANES, QT_COLS), F32),
                        pltpu.VMEM((s + ATT_BLK, LANES), F32),
                        pltpu.VMEM((s + ATT_BLK, LANES), F32),
                        pltpu.VMEM((2, s, QW), BF16), pltpu.VMEM((2, s, LANES), BF16),
                        pltpu.VMEM((2, s, LANES), BF16), pltpu.VMEM((2, s, QW), BF16),
                        pltpu.SemaphoreType.DMA((8,))],
        compiler_params=_params())(sinks, *saved, proj3, o3, dy3, cosf, sinf, dproj3)


def _staged_copies(stages, dsts, sems, slot):
    return [pltpu.make_async_copy(st.at[slot], dst, sems.at[slot * len(stages) + t])
            for t, (st, dst) in enumerate(zip(stages, dsts))]


def _staged_reuse(step, stages, dsts, sems, slot):
    @pl.when(step >= 2)
    def _():
        for cp in _staged_copies(stages, dsts, sems, slot):
            cp.wait()


def _staged_flush(step, n_steps, stages, dsts, sems, slot):
    for cp in _staged_copies(stages, dsts, sems, slot):
        cp.start()

    @pl.when(step == n_steps - 1)
    def _():
        for cp in _staged_copies(stages, dsts, sems, slot):
            cp.wait()
        if n_steps >= 2:
            for cp in _staged_copies(stages, dsts, sems, 1 - slot):
                cp.wait()


def _merge_fwd_bwd(x, tgt, y_rnn, y_attn, proj, w_r, w_a, w_o, gf):
    t, d = x.shape
    tm = min(t, 256)
    nt = t // tm

    hw = d // 2

    def body(x_ref, t_ref, yr_ref, ya_ref, mr0_ref, mr1_ref, ma0_ref, ma1_ref, wr_ref, wa_ref, wo_ref, gf_ref,
             dp_ref, dyr_ref, dya_ref, mg_ref, dx2_ref, dx2b_ref, dpr_ref, dpa_ref, loss_ref, dgf_ref, dmg_st, o_sems):
        i = pl.program_id(0)
        slot = i % 2
        dsts = [dp_ref.at[pl.ds(pl.multiple_of(i * tm, tm), tm), pl.ds(OFF_MERGE_R, 2 * d)]]
        _staged_reuse(i, [dmg_st], dsts, o_sems, slot)
        wr = wr_ref[...]
        wa = wa_ref[...]
        wo = wo_ref[...]
        gfv = gf_ref[...]
        pr = _dot(yr_ref[...], wr)
        pa = _dot(ya_ref[...], wa)
        sr = _sigmoid(jnp.concatenate([mr0_ref[...], mr1_ref[...]], axis=1))
        sa = _sigmoid(jnp.concatenate([ma0_ref[...], ma1_ref[...]], axis=1))
        mb = (sr * pr + sa * pa).astype(BF16)
        mg_ref[...] = mb
        x2 = x_ref[...] + _dot(mb, wo)
        r2 = lax.rsqrt(jnp.mean(x2 * x2, axis=-1, keepdims=True) + NORM_EPS)
        nrm = x2 * r2
        err = nrm * gfv - t_ref[...]
        dy = err * (1.0 / d)
        dn = dy * gfv
        dx2 = r2 * (dn - nrm * jnp.mean(dn * nrm, axis=-1, keepdims=True))
        dx2_ref[...] = dx2
        dx2b = dx2.astype(BF16)
        dx2b_ref[...] = dx2b
        dmerged = _dot_nt(dx2b, wo)
        dpr = (dmerged * sr).astype(BF16)
        dpa = (dmerged * sa).astype(BF16)
        dpr_ref[...] = dpr
        dpa_ref[...] = dpa
        dmg_st[slot, :, 0:d] = (dmerged * pr * (sr * (1.0 - sr))).astype(BF16)
        dmg_st[slot, :, d:2 * d] = (dmerged * pa * (sa * (1.0 - sa))).astype(BF16)
        _staged_flush(i, nt, [dmg_st], dsts, o_sems, slot)
        dyr_ref[...] = _dot_nt(dpr, wr)
        dya_ref[...] = _dot_nt(dpa, wa)

        @pl.when(i == 0)
        def _():
            loss_ref[...] = jnp.zeros_like(loss_ref)
            dgf_ref[...] = jnp.zeros_like(dgf_ref)

        loss_ref[...] += jnp.full((1, LANES), 0.5 / d, F32) * jnp.sum(err * err)
        dgf_ref[...] += jnp.sum(dy * nrm, axis=0, keepdims=True)

    tile = pl.BlockSpec((tm, d), lambda i: (i, 0))
    wsp = pl.BlockSpec((d, d), lambda i: (0, 0))

    def gate(col_blk):
        return pl.BlockSpec((tm, hw), lambda i: (i, col_blk))

    fb = jax.ShapeDtypeStruct((t, d), BF16)
    ff = jax.ShapeDtypeStruct((t, d), F32)
    return pl.pallas_call(
        body, name="merge_fwd_bwd", grid=(nt,),
        out_shape=(jax.ShapeDtypeStruct((t, D_IN), BF16), ff, ff, fb, ff, fb, fb, fb,
                   jax.ShapeDtypeStruct((1, LANES), F32), jax.ShapeDtypeStruct((1, d), F32)),
        in_specs=[tile, tile, tile, tile] + [gate(OFF_MERGE_R // hw + j) for j in range(4)] + [
            wsp, wsp, wsp, pl.BlockSpec((1, d), lambda i: (0, 0))],
        out_specs=(ANY, tile, tile, tile, tile, tile, tile, tile,
                   pl.BlockSpec((1, LANES), lambda i: (0, 0)), pl.BlockSpec((1, d), lambda i: (0, 0))),
        scratch_shapes=[pltpu.VMEM((2, tm, 2 * d), BF16), pltpu.SemaphoreType.DMA((2,))],
        compiler_params=_params())(x, tgt, y_rnn, y_attn, proj, proj, proj, proj, w_r, w_a, w_o, gf)


def _local_grads(x, tgt, h, proj, norm_g, w_in_bm, conv_w, conv_b, lru_w_a, lru_b_a, lru_w_x, lru_b_x, lam, sinks,
                 row_sharded, gf):
    bsz, s, d = x.shape
    t = bsz * s
    x2 = x.reshape(t, d)
    proj3 = proj.reshape(bsz, s, D_IN)
    h_lru, y_rnn, gathered = _lru_fwd(proj3, conv_w, conv_b, lru_w_a, lru_b_a, lru_w_x, lru_b_x, lam, row_sharded)
    w_r, w_a, w_o = (g.reshape(d, d) for g in gathered)
    cosf, sinf = _rope_tables(s)
    o_attn, y_attn, attn_saved = _attn_fwd(proj3, sinks, cosf, sinf)
    y_rnn2 = y_rnn.reshape(t, d)
    y_attn2 = y_attn.reshape(t, d)
    dproj, dyr, dya, merged, dx2, dx2b, dpr, dpa, loss, dgf = _merge_fwd_bwd(
        x2, tgt.reshape(t, d), y_rnn2, y_attn2, proj, w_r, w_a, w_o, gf)
    dproj3, dsink = _attn_bwd(proj3, attn_saved, o_attn, dya.reshape(bsz, s, d), dproj.reshape(bsz, s, D_IN),
                              sinks, cosf, sinf)
    dproj3, dcw, dcb, dwa, dba, dwx, dbx, dlam = _lru_bwd(
        proj3, h_lru, dyr.reshape(bsz, s, d), dproj3, conv_w, conv_b, lru_w_a, lru_b_a, lru_w_x, lru_b_x, lam)
    dproj = dproj3.reshape(t, D_IN)
    grad_x, dng = _grad_x(dproj, w_in_bm, x2, dx2, norm_g)
    small = dict(norm_g=dng, conv_w=dcw, conv_b=dcb, lru_w_a=dwa, lru_b_a=dba, lru_w_x=dwx, lru_b_x=dbx,
                 lru_lambda=dlam, attn_sinks=dsink[:, 0, :HEADS_PER_STEP].reshape(1, N_Q_HEADS), final_norm_g=dgf)
    squares = [(y_rnn2, dpr), (y_attn2, dpa), (merged, dx2b)]
    return loss[0, 0], grad_x.reshape(bsz, s, d), h, dproj, squares, small


ANY = pl.BlockSpec(memory_space=pl.ANY)


def _mesh_pos():
    return lax.axis_index("x"), lax.axis_index("y"), lax.axis_index("c")


def _remote(src, dst, send_sems, recv_sems, idx, peer):
    return pltpu.make_async_remote_copy(src_ref=src, dst_ref=dst, send_sem=send_sems.at[idx],
                                        recv_sem=recv_sems.at[idx], device_id=peer, device_id_type=MESH)


def _row_gather(ins, outs, send_sems, recv_sems, fsend_sems, frecv_sems):
    n = len(ins)
    x, y, c = _mesh_pos()
    me = 2 * x + y
    sib = (x, y, 1 - c)
    peers = [((x, 1 - y, c), me ^ 1), ((1 - x, y, c), me ^ 2), ((1 - x, 1 - y, c), me ^ 3)]

    def half(ref, slot, t, which):
        hr = ins[t].shape[1] // 2
        return ref.at[slot, pl.ds(pl.multiple_of(which * hr, 8), hr), :]

    def ici(t, k):
        peer, pj = peers[k]
        src = half(ins[t], me, t, c)
        return (_remote(src, half(outs[t], me, t, c), send_sems, recv_sems, 3 * t + k, peer),
                _remote(src, half(outs[t], pj, t, c), send_sems, recv_sems, 3 * t + k, peer))

    def forward(t, k):
        got = half(outs[t], peers[k][1], t, c)
        return (_remote(got, got, fsend_sems, frecv_sems, 3 * t + k, sib),
                _remote(got, half(outs[t], peers[k][1], t, 1 - c), fsend_sems, frecv_sems, 3 * t + k, sib))

    pairs = [(t, k) for t in range(n) for k in range(3)]

    def start():
        for t, k in pairs:
            ici(t, k)[0].start()

    def pass_on():
        for t, k in pairs:
            ici(t, k)[1].wait_recv()
            forward(t, k)[0].start()

    def finish():
        for t, k in pairs:
            ici(t, k)[0].wait_send()
            forward(t, k)[0].wait_send()
            forward(t, k)[1].wait_recv()

    return start, pass_on, finish


def _gather_in_proj(x, g, bufs, split, idx):
    t_tok, d = x.shape
    n = len(bufs)
    tm = min(t_tok, 1024)
    nt = t_tok // tm
    n_fwd = 3 * sum(split)
    assert split[0]

    def body(idx_ref, x_ref, g_ref, *refs):
        ins, proj_ref, h_out, outs = refs[:n], refs[n], refs[n + 1], refs[n + 2:2 * n + 2]
        wbuf, h_all, send_sems, recv_sems, fsend_sems, frecv_sems, l_sems = refs[2 * n + 2:]
        j, i = pl.program_id(0), pl.program_id(1)
        rows = pl.ds(pl.multiple_of(i * tm, tm), tm)
        x, y, c = _mesh_pos()
        me = 2 * x + y
        sib = (x, y, 1 - c)
        peers = [((x, 1 - y, c), me ^ 1), ((1 - x, y, c), me ^ 2), ((1 - x, 1 - y, c), me ^ 3)]

        def part(ref, slot, t, half):
            if not split[t]:
                return ref.at[slot]
            hr = bufs[t].shape[1] // 2
            return ref.at[slot, pl.ds(pl.multiple_of(half * hr, 8), hr), :]

        def land(t):
            return wbuf if t == 0 else outs[t]

        def ici(t, k):
            peer, pj = peers[k]
            src = part(ins[t], me, t, c)
            return (_remote(src, part(land(t), me, t, c), send_sems, recv_sems, 3 * t + k, peer),
                    _remote(src, part(land(t), pj, t, c), send_sems, recv_sems, 3 * t + k, peer))

        fwd_index = {}
        for t in range(n):
            if split[t]:
                for k in range(3):
                    fwd_index[(t, k)] = len(fwd_index)

        def forward(t, k):
            pj = peers[k][1]
            got = part(land(t), pj, t, c)
            f = fwd_index[(t, k)]
            return (_remote(got, got, fsend_sems, frecv_sems, f, sib),
                    _remote(got, part(land(t), pj, t, 1 - c), fsend_sems, frecv_sems, f, sib))

        def write_back(k):
            pj = peers[k][1]
            return pltpu.make_async_copy(wbuf.at[pj], outs[0].at[pj], l_sems.at[1 + k])

        relay_peer = ((x + c) % 2, (y + 1 - c) % 2, c)

        def relay():
            got = part(wbuf, me ^ (2 - c), 0, c)
            return (_remote(got, got, send_sems, recv_sems, 2, relay_peer),
                    _remote(got, part(wbuf, me ^ 3, 0, c), send_sems, recv_sems, 2, relay_peer))

        direct = [(t, k) for t in range(n) for k in range(3) if (t, k) != (0, 2)]

        @pl.when((j == 0) & (i == 0))
        def _():
            for t, k in direct:
                ici(t, k)[0].start()
            own = pltpu.make_async_copy(ins[0].at[me], wbuf.at[me], l_sems.at[0])
            own.start()
            own.wait()

        @pl.when((j == 1) & (i == 0))
        def _():
            pltpu.make_async_copy(h_all, h_out, l_sems.at[4]).start()
            for k in range(2):
                ici(0, k)[1].wait_recv()
            relay()[0].start()
            for k in range(2):
                forward(0, k)[0].start()
            forward(0, 0)[1].wait_recv()
            write_back(0).start()

        @pl.when((j == 2) & (i == 0))
        def _():
            forward(0, 1)[1].wait_recv()
            write_back(1).start()

        @pl.when((j == 3) & (i == 0))
        def _():
            relay()[1].wait_recv()
            forward(0, 2)[0].start()
            forward(0, 2)[1].wait_recv()
            write_back(2).start()

        @pl.when(j == 0)
        def _():
            xv = x_ref[...]
            r = lax.rsqrt(jnp.mean(xv * xv, axis=-1, keepdims=True) + NORM_EPS)
            h_all[rows, :] = (xv * r * g_ref[...]).astype(BF16)

        proj_ref[...] = _dot(h_all[rows, :], wbuf[me ^ j])

        @pl.when((j == N_CHIPS - 1) & (i == nt - 1))
        def _():
            pltpu.make_async_copy(h_all, h_out, l_sems.at[4]).wait()
            for t in range(1, n):
                for k in range(3):
                    ici(t, k)[1].wait_recv()
                    if split[t]:
                        forward(t, k)[0].start()
            relay()[0].wait_send()
            for t, k in direct:
                ici(t, k)[0].wait_send()
            for t in range(n):
                if split[t]:
                    for k in range(3):
                        forward(t, k)[0].wait_send()
                        if t > 0:
                            forward(t, k)[1].wait_recv()
            for k in range(3):
                write_back(k).wait()

    grid_spec = pltpu.PrefetchScalarGridSpec(
        num_scalar_prefetch=1, grid=(N_CHIPS, nt),
        in_specs=[pl.BlockSpec((tm, d), lambda j, i, idx_ref: (jnp.where(j == 0, i, nt - 1), 0)),
                  pl.BlockSpec((1, d), lambda j, i, idx_ref: (0, 0))] + [ANY] * n,
        out_specs=[pl.BlockSpec((tm, W_BLK), lambda j, i, idx_ref: (i, idx_ref[0] ^ j)), ANY] + [ANY] * n,
        scratch_shapes=[pltpu.VMEM(bufs[0].shape, bufs[0].dtype), pltpu.VMEM((t_tok, d), BF16),
                        pltpu.SemaphoreType.DMA((3 * n,)), pltpu.SemaphoreType.DMA((3 * n,)),
                        pltpu.SemaphoreType.DMA((n_fwd,)), pltpu.SemaphoreType.DMA((n_fwd,)),
                        pltpu.SemaphoreType.DMA((5,))])
    out_shape = [jax.ShapeDtypeStruct((t_tok, D_IN), F32), jax.ShapeDtypeStruct((t_tok, d), BF16)] + [
        jax.ShapeDtypeStruct(a.shape, a.dtype) for a in bufs]
    res = pl.pallas_call(
        body, name="gather_in_proj", grid_spec=grid_spec, out_shape=out_shape,
        input_output_aliases={3 + t: 2 + t for t in range(n)}, compiler_params=_params())(idx, x, g, *bufs)
    return res[1], res[0], res[2:]


def _row_tile(rows, row_bytes, cap_bytes=2 * 1024 * 1024):
    best = None
    for tr in range(8, rows + 1, 8):
        if rows % tr == 0 and tr * row_bytes <= cap_bytes:
            best = tr
    return best if best is not None else rows


XOR_ORDER = (3, 2, 1)


def _grads_reduce_scatter(h, dproj, squares, small, idx):
    t, d = h.shape
    nsq = len(squares)
    hr = d // 2
    qr = ROW_BLK // 2
    tk = min(t, 1024)
    nk = t // tk
    last = N_CHIPS - 1
    n_phase = 3

    def dest(s, idx_ref):
        xo = jnp.where(s == 0, XOR_ORDER[0], jnp.where(s == 1, XOR_ORDER[1], jnp.where(s == 2, XOR_ORDER[2], 0)))
        return idx_ref[0] ^ xo

    def k_sq(p, k):
        return jnp.where(p == 0, k, nk - 1)

    def k_w(p, k):
        return jnp.where(p == 0, 0, k)

    in_specs = [
        pl.BlockSpec((tk, hr), lambda s, p, k, idx_ref: (k_w(p, k), (1 - idx_ref[1] + jnp.maximum(p - 1, 0)) % 2)),
        pl.BlockSpec((tk, W_BLK), lambda s, p, k, idx_ref: (k_w(p, k), dest(s, idx_ref)))]
    for q in range(nsq):
        in_specs.append(pl.BlockSpec((tk, ROW_BLK), lambda s, p, k, idx_ref: (k_sq(p, k), dest(s, idx_ref))))
        in_specs.append(pl.BlockSpec((tk, d), lambda s, p, k, idx_ref: (k_sq(p, k), 0)))

    def body(idx_ref, *refs):
        nj = 1 + nsq
        h_ref, dp_ref = refs[0], refs[1]
        sq_in = refs[2:2 + 2 * nsq]
        small_in = refs[2 * nj]
        outs = refs[2 * nj + 1:3 * nj + 2]
        landing = refs[3 * nj + 2:4 * nj + 3]
        sc = refs[4 * nj + 3:]
        acc_w, xr_w, sb_w = sc[0:3]
        sq_sc = [sc[3 + 3 * q:6 + 3 * q] for q in range(nsq)]
        sm, smx = sc[3 * nj:3 * nj + 2]
        x_send, x_recv, i_send, i_recv, f_send, f_recv, o_sem, l_sem = sc[3 * nj + 2:]
        s, p, k = pl.program_id(0), pl.program_id(1), pl.program_id(2)
        x, y, c = _mesh_pos()
        sib = (x, y, 1 - c)
        peers = [((1 - x) if xo & 2 else x, (1 - y) if xo & 1 else y, c) for xo in XOR_ORDER]
        slot = s % 2
        mine_w = pl.ds(pl.multiple_of(c * hr, 8), hr)
        theirs_w = pl.ds(pl.multiple_of((1 - c) * hr, 8), hr)
        mine_q = pl.ds(pl.multiple_of(c * qr, 8), qr)
        theirs_q = pl.ds(pl.multiple_of((1 - c) * qr, 8), qr)

        def exch(j, src, dst):
            return _remote(src, dst, x_send, x_recv, 2 * j + slot, sib)

        sbufs = [sb_w] + [sq_sc[q][2] for q in range(nsq)]

        def ici(j, ss):
            return _remote(sbufs[j].at[ss], landing[j].at[ss], i_send, i_recv, last * j + ss, peers[ss])

        def exchanges():
            cps = [exch(0, acc_w.at[0], xr_w.at[slot])]
            cps += [exch(1 + q, sq_sc[q][0].at[theirs_q, :], sq_sc[q][1].at[slot]) for q in range(nsq)]
            return cps

        def small_send(ss):
            return _remote(sm.at[c], landing[nj].at[ss], i_send, i_recv, last * nj + ss, peers[ss])

        def small_start():
            load = pltpu.make_async_copy(small_in, sm, l_sem.at[nj + 1])
            load.start()
            load.wait()
            swap = _remote(sm, smx.at[pl.ds(0, 2)], x_send, x_recv, 2 * nj, sib)
            swap.start()
            swap.wait_recv()
            swap.wait_send()
            sm[...] = sm[...] + smx[0:2]
            for ss in range(last):
                small_send(ss).start()

        def pair_ref(j):
            return acc_w.at[1] if j == 0 else sq_sc[j - 1][0].at[mine_q, :]

        def sq_phase():
            pl.when((s == 0) & (k == 0))(small_start)
            for q in range(nsq):
                acc = sq_sc[q][0]

                @pl.when(k == 0)
                def _():
                    acc[...] = jnp.zeros((ROW_BLK, d), F32)

                acc[...] += _dot_tn(sq_in[2 * q][...], sq_in[2 * q + 1][...])

            @pl.when(k == nk - 1)
            def _():
                for cp in exchanges()[1:]:
                    cp.start()

        def w_phase(hf):
            @pl.when(k == 0)
            def _():
                acc_w[hf] = jnp.zeros((hr, W_BLK), F32)

            acc_w[hf] += _dot_tn(h_ref[...], dp_ref[...])

            @pl.when(k == nk - 1)
            def _():
                if hf == 0:
                    exchanges()[0].start()
                else:
                    finish_step()

        def finish_step():
            for cp in exchanges():
                cp.wait_recv()
                cp.wait_send()
            acc_w[1] += xr_w[slot]
            for q in range(nsq):
                sq_sc[q][0][mine_q, :] += sq_sc[q][1][slot]
            for ss in range(last):
                @pl.when(s == ss)
                def _():
                    for j in range(nj):
                        sbufs[j][ss] = pair_ref(j)[...].astype(BF16)
                        ici(j, ss).start()

            @pl.when(s == last)
            def _():
                for ss in range(last):
                    for j in range(nj):
                        ici(j, ss).wait_recv()
                        ici(j, ss).wait_send()
                    small_send(ss).wait_recv()
                    small_send(ss).wait_send()
                stage = [pltpu.make_async_copy(landing[j], sbufs[j], l_sem.at[j]) for j in range(nj)]
                stage.append(pltpu.make_async_copy(landing[nj], smx, l_sem.at[nj]))
                for cp in stage:
                    cp.start()
                for j in range(nj):
                    stage[j].wait()
                    total = pair_ref(j)[...]
                    for ss in range(last):
                        total = total + sbufs[j][ss].astype(F32)
                    pair_ref(j)[...] = total
                stage[nj].wait()
                by_xor = {xo: smx[ss] for ss, xo in enumerate(XOR_ORDER)}
                sm[c] = (sm[c] + by_xor[1]) + (by_xor[2] + by_xor[3])
                done = [(acc_w.at[1], outs[0].at[mine_w, :], outs[0].at[theirs_w, :])]
                done += [(pair_ref(1 + q), outs[1 + q].at[mine_q, :], outs[1 + q].at[theirs_q, :]) for q in range(nsq)]
                done.append((sm.at[c], outs[nj].at[c], outs[nj].at[1 - c]))
                copies = []
                for j, (src, mine, theirs) in enumerate(done):
                    keep = pltpu.make_async_copy(src, mine, o_sem.at[j])
                    give = _remote(src, mine, f_send, f_recv, j, sib)
                    take = _remote(src, theirs, f_send, f_recv, j, sib)
                    keep.start()
                    give.start()
                    copies.append((keep, give, take))
                for keep, give, take in copies:
                    keep.wait()
                    give.wait_send()
                    take.wait_recv()

        pl.when(p == 0)(sq_phase)
        for hf in range(2):
            pl.when(p == 1 + hf)(functools.partial(w_phase, hf))

    nj = 1 + nsq
    scratch = [pltpu.VMEM((2, hr, W_BLK), F32), pltpu.VMEM((2, hr, W_BLK), F32), pltpu.VMEM((last, hr, W_BLK), BF16)]
    for _ in range(nsq):
        scratch += [pltpu.VMEM((ROW_BLK, d), F32), pltpu.VMEM((2, qr, d), F32), pltpu.VMEM((last, qr, d), BF16)]
    scratch += [pltpu.VMEM((2, PK_HALF, LANES), F32), pltpu.VMEM((last, PK_HALF, LANES), F32)]
    scratch += [pltpu.SemaphoreType.DMA((2 * nj + 1,)), pltpu.SemaphoreType.DMA((2 * nj + 1,)),
                pltpu.SemaphoreType.DMA((last * (nj + 1),)), pltpu.SemaphoreType.DMA((last * (nj + 1),)),
                pltpu.SemaphoreType.DMA((nj + 1,)), pltpu.SemaphoreType.DMA((nj + 1,)),
                pltpu.SemaphoreType.DMA((nj + 1,)), pltpu.SemaphoreType.DMA((nj + 2,))]
    grid_spec = pltpu.PrefetchScalarGridSpec(
        num_scalar_prefetch=1, grid=(N_CHIPS, n_phase, nk), in_specs=in_specs + [ANY],
        out_specs=[ANY] * (2 * nj + 2), scratch_shapes=scratch)
    out_shape = [jax.ShapeDtypeStruct((d, W_BLK), F32)] + [jax.ShapeDtypeStruct((ROW_BLK, d), F32)] * nsq
    out_shape.append(jax.ShapeDtypeStruct((2, PK_HALF, LANES), F32))
    out_shape += [jax.ShapeDtypeStruct((last, hr, W_BLK), BF16)] + [jax.ShapeDtypeStruct((last, qr, d), BF16)] * nsq
    out_shape.append(jax.ShapeDtypeStruct((last, PK_HALF, LANES), F32))
    flat = [a for pair in squares for a in pair]
    res = pl.pallas_call(body, name="grads_reduce_scatter", grid_spec=grid_spec, out_shape=out_shape,
                         compiler_params=_params())(idx, h, dproj, *flat, small)
    return res[:nj + 1]


_VEC_NAMES = ("norm_g", "conv_b", "lru_b_a", "lru_b_x", "lru_lambda", "final_norm_g")


def _pack_small(p, conv_full=None, scalar=None):
    rows = [p["lru_w_a"].reshape(PK_WX - PK_WA, LANES), p["lru_w_x"].reshape(PK_VEC - PK_WX, LANES)]
    rows += [p[k].reshape(8, LANES) for k in _VEC_NAMES]
    rows.append(jnp.pad(p["attn_sinks"].reshape(1, N_Q_HEADS), ((0, 7), (0, LANES - N_Q_HEADS))))
    rows.append(jnp.zeros((32, LANES), F32) if conv_full is None else conv_full.reshape(32, LANES))
    tail = PK_ROWS - PK_SCALAR
    if scalar is None:
        rows.append(jnp.zeros((tail, LANES), F32))
    else:
        rows.append(jnp.pad(scalar.reshape(1, 1), ((0, tail - 1), (0, LANES - 1))))
    return jnp.concatenate(rows, axis=0)


def _unpack_small(pk, like):
    out = {"lru_w_a": pk[PK_WA:PK_WX].reshape(like["lru_w_a"].shape),
           "lru_w_x": pk[PK_WX:PK_VEC].reshape(like["lru_w_x"].shape)}
    for j, k in enumerate(_VEC_NAMES):
        out[k] = pk[PK_VEC + 8 * j:PK_VEC + 8 * j + 8].reshape(like[k].shape)
    out["attn_sinks"] = pk[PK_SINK:PK_SINK + 1, :N_Q_HEADS].reshape(like["attn_sinks"].shape)
    return out


_WEIGHTS = ("norm_g", "w_in", "conv_w", "conv_b", "lru_w_a", "lru_b_a", "lru_w_x", "lru_b_x", "lru_lambda",
            "attn_sinks", "w_rnn_out", "w_attn_out", "w_o", "final_norm_g")
_SMALL = ("norm_g", "conv_b", "lru_w_a", "lru_b_a", "lru_w_x", "lru_b_x", "lru_lambda", "attn_sinks", "final_norm_g")
_ROW_SHARDED = ("w_rnn_out", "w_attn_out", "w_o")


def kernel(x, norm_g, w_in, conv_w, conv_b, lru_w_a, lru_b_a, lru_w_x, lru_b_x, lru_lambda, attn_sinks, w_rnn_out, w_attn_out, w_o, final_norm_g, loss_target, m_norm_g, m_w_in, m_conv_w, m_conv_b, m_lru_w_a, m_lru_b_a, m_lru_w_x, m_lru_b_x, m_lru_lambda, m_attn_sinks, m_w_rnn_out, m_w_attn_out, m_w_o, m_final_norm_g, v_norm_g, v_w_in, v_conv_w, v_conv_b, v_lru_w_a, v_lru_b_a, v_lru_w_x, v_lru_b_x, v_lru_lambda, v_attn_sinks, v_w_rnn_out, v_w_attn_out, v_w_o, v_final_norm_g):
    w = dict(norm_g=norm_g, w_in=w_in, conv_w=conv_w, conv_b=conv_b, lru_w_a=lru_w_a, lru_b_a=lru_b_a, lru_w_x=lru_w_x,
             lru_b_x=lru_b_x, lru_lambda=lru_lambda, attn_sinks=attn_sinks, w_rnn_out=w_rnn_out, w_attn_out=w_attn_out,
             w_o=w_o, final_norm_g=final_norm_g)
    m = dict(norm_g=m_norm_g, w_in=m_w_in, conv_w=m_conv_w, conv_b=m_conv_b, lru_w_a=m_lru_w_a, lru_b_a=m_lru_b_a,
             lru_w_x=m_lru_w_x, lru_b_x=m_lru_b_x, lru_lambda=m_lru_lambda, attn_sinks=m_attn_sinks,
             w_rnn_out=m_w_rnn_out, w_attn_out=m_w_attn_out, w_o=m_w_o, final_norm_g=m_final_norm_g)
    v = dict(norm_g=v_norm_g, w_in=v_w_in, conv_w=v_conv_w, conv_b=v_conv_b, lru_w_a=v_lru_w_a, lru_b_a=v_lru_b_a,
             lru_w_x=v_lru_w_x, lru_b_x=v_lru_b_x, lru_lambda=v_lru_lambda, attn_sinks=v_attn_sinks,
             w_rnn_out=v_w_rnn_out, w_attn_out=v_w_attn_out, w_o=v_w_o, final_norm_g=v_final_norm_g)
    mx, my, mc = _mesh_pos()
    me = 2 * mx + my
    d = D_MODEL

    my_chip = jnp.reshape(me, (1,)).astype(jnp.int32)
    (buf_in,) = _put_slots([w["w_in"][0]], my_chip, BF16, "cast_w_in")
    (buf_cw,) = _put_slots([w["conv_w"][0]], my_chip, F32, "slot_conv_w")
    row_sharded = _put_slots([w[k][0] for k in _ROW_SHARDED], my_chip, BF16, "cast_row_sharded")
    h, proj, (g_in, g_cw) = _gather_in_proj(x.reshape(-1, d), w["norm_g"], [buf_in, buf_cw], [True, False], my_chip)
    conv_full = g_cw.transpose(1, 0, 2).reshape(CONV_WIDTH, D_RNN)

    loss_local, grad_x, h, dproj, squares, gsmall = _local_grads(
        x, loss_target, h, proj, w["norm_g"], g_in, conv_full, w["conv_b"], w["lru_w_a"][0], w["lru_b_a"], w["lru_w_x"][0],
        w["lru_b_x"], w["lru_lambda"], w["attn_sinks"][0], row_sharded, w["final_norm_g"].reshape(1, d))
    gpack = _pack_small(gsmall, gsmall["conv_w"], loss_local).reshape(2, PK_HALF, LANES)
    f_in, f_r, f_a, f_o, spack = _grads_reduce_scatter(h, dproj, squares, gpack, jnp.stack([me, mc]).astype(jnp.int32))
    spack = spack.reshape(PK_ROWS, LANES)
    loss = spack[PK_SCALAR, 0]

    grads = _unpack_small(spack, w)
    conv_all = spack[PK_CONV:PK_CONV + 32].reshape(CONV_WIDTH, D_RNN)
    grads["conv_w"] = lax.dynamic_slice_in_dim(conv_all, me * (D_RNN // N_CHIPS), D_RNN // N_CHIPS, axis=1)[None]
    grads["w_in"] = f_in[None]
    grads["w_rnn_out"], grads["w_attn_out"], grads["w_o"] = f_r[None], f_a[None], f_o[None]

    delta, new_m, new_v = {}, {}, {}
    def group(k):
        return w[k][0], grads[k][0], m[k][0], v[k][0]

    for names, call in ((("w_in",), "adamw_w_in"), (_ROW_SHARDED, "adamw_row_sharded")):
        for k, (dk, mk, vk) in zip(names, _adamw([group(k) for k in names], call)):
            delta[k], new_m[k], new_v[k] = dk[None], mk[None], vk[None]
    shp = (2 * CONV_WIDTH, LANES)
    ((dk, mk, vk),) = _adamw([tuple(a.reshape(shp) for a in (w["conv_w"], grads["conv_w"], m["conv_w"], v["conv_w"]))],
                             "adamw_conv_w")
    delta["conv_w"], new_m["conv_w"], new_v["conv_w"] = (a.reshape(w["conv_w"].shape) for a in (dk, mk, vk))
    ((dk, mk, vk),) = _adamw([(_pack_small(w), spack, _pack_small(m), _pack_small(v))], "adamw_small")
    for src, dst in ((dk, delta), (mk, new_m), (vk, new_v)):
        dst.update(_unpack_small(src, w))

    return (loss, grad_x, *[grads[k] for k in _WEIGHTS], *[delta[k] for k in _WEIGHTS],
            *[new_m[k] for k in _WEIGHTS], *[new_v[k] for k in _WEIGHTS])
```

```python
import functools
import math

import jax
import jax.numpy as jnp
from jax import lax
from jax.experimental import pallas as pl
from jax.experimental.pallas import tpu as pltpu

F32 = jnp.float32
BF16 = jnp.bfloat16
MESH = pl.DeviceIdType.MESH

D_MODEL = 1024
D_RNN = 1024
N_RNN_BLOCKS = 8
RNN_BLOCK = D_RNN // N_RNN_BLOCKS
CONV_WIDTH = 4
LRU_C = 8.0
HEAD_DIM = 64
N_Q_HEADS = 16
N_KV_HEADS = 4
D_ATTN = N_Q_HEADS * HEAD_DIM
D_KV = N_KV_HEADS * HEAD_DIM
WINDOW = 128
ROPE_DIM = HEAD_DIM // 4
ROPE_THETA = 500000.0
NORM_EPS = 1e-6
OFF_RNN_X = 0
OFF_RNN_G = OFF_RNN_X + D_RNN
OFF_Q = OFF_RNN_G + D_RNN
OFF_K = OFF_Q + D_ATTN
OFF_V = OFF_K + D_KV
OFF_ATTN_G = OFF_V + D_KV
OFF_MERGE_R = OFF_ATTN_G + D_ATTN
OFF_MERGE_A = OFF_MERGE_R + D_MODEL
D_IN = OFF_MERGE_A + D_MODEL

ADAM_LR = 0.001
ADAM_B1 = 0.9
ADAM_B2 = 0.999
ADAM_EPS = 1e-08
ADAM_WD = 0.01
ADAM_STEP = 10

N_CHIPS = 4
W_BLK = D_IN // N_CHIPS
ROW_BLK = D_MODEL // N_CHIPS
LANES = 128
ATT_BLK = 128
VMEM_LIMIT = 56 * 1024 * 1024
NEG_BIG = -1e30
ATTN_SCALE = 1.0 / math.sqrt(HEAD_DIM)

PK_WA = 0
PK_WX = PK_WA + N_RNN_BLOCKS * RNN_BLOCK
PK_VEC = PK_WX + N_RNN_BLOCKS * RNN_BLOCK
PK_SINK = PK_VEC + 6 * 8
PK_CONV = PK_SINK + 8
PK_SCALAR = PK_CONV + 32
PK_ROWS = PK_SCALAR + 8
PK_HALF = PK_ROWS // 2


def _params(**kw):
    return pltpu.CompilerParams(vmem_limit_bytes=VMEM_LIMIT, **kw)


def _sigmoid(z):
    return 1.0 / (1.0 + jnp.exp(-z))


def _dot(a, b):
    return jnp.dot(a, b, preferred_element_type=F32)


def _dot_nt(a, b):
    return lax.dot_general(a, b, (((1,), (1,)), ((), ())), preferred_element_type=F32)


def _dot_tn(a, b):
    return lax.dot_general(a, b, (((0,), (0,)), ((), ())), preferred_element_type=F32)


def _put_slots(srcs, slot, dtype, name):
    rows, c = srcs[0].shape
    n = len(srcs)
    tr = _row_tile(rows, c * 4)

    def body(idx_ref, *refs):
        for s_ref, o_ref in zip(refs[:n], refs[n:]):
            o_ref[...] = s_ref[...].astype(dtype)

    grid_spec = pltpu.PrefetchScalarGridSpec(
        num_scalar_prefetch=1, grid=(rows // tr,),
        in_specs=[pl.BlockSpec((tr, c), lambda i, idx_ref: (i, 0))] * n,
        out_specs=[pl.BlockSpec((None, tr, c), lambda i, idx_ref: (idx_ref[0], i, 0))] * n)
    return pl.pallas_call(body, name=name, grid_spec=grid_spec,
                          out_shape=[jax.ShapeDtypeStruct((N_CHIPS, rows, c), dtype)] * n,
                          compiler_params=_params())(slot, *srcs)


def _adamw(groups, name, echo_grad=False):
    r, c = groups[0][0].shape
    n = len(groups)
    n_out = 4 if echo_grad else 3
    tr = _row_tile(r, c * 4, 1024 * 1024 // n)
    c1 = 1.0 - ADAM_B1 ** ADAM_STEP
    c2 = 1.0 - ADAM_B2 ** ADAM_STEP

    def body(*refs):
        for q in range(n):
            w_ref, g_ref, m_ref, v_ref = refs[4 * q:4 * q + 4]
            d_ref, nm_ref, nv_ref = refs[4 * n + n_out * q:4 * n + n_out * q + 3]
            gv = g_ref[...]
            if echo_grad:
                refs[4 * n + n_out * q + 3][...] = gv
            nm = ADAM_B1 * m_ref[...] + (1.0 - ADAM_B1) * gv
            nv = ADAM_B2 * v_ref[...] + (1.0 - ADAM_B2) * (gv * gv)
            m_hat = nm / c1
            v_hat = nv / c2
            d_ref[...] = -ADAM_LR * (m_hat / (jnp.sqrt(v_hat) + ADAM_EPS) + ADAM_WD * w_ref[...])
            nm_ref[...] = nm
            nv_ref[...] = nv

    spec = pl.BlockSpec((tr, c), lambda i: (i, 0))
    sds = jax.ShapeDtypeStruct((r, c), F32)
    res = pl.pallas_call(
        body, name=name, grid=(r // tr,), out_shape=[sds] * (n_out * n), in_specs=[spec] * (4 * n),
        out_specs=[spec] * (n_out * n), compiler_params=_params())(*[a for grp in groups for a in grp])
    return [tuple(res[n_out * q:n_out * q + n_out]) for q in range(n)]


def _grad_x(dproj, w_bm, x, dx2, g):
    t = dproj.shape[0]
    nb, d, wb = w_bm.shape
    tm = min(t, 512)
    chunk = min(tm, 256)

    def body(dp_ref, w_ref, x_ref, dx2_ref, g_ref, gx_ref, dg_ref, acc_ref):
        i, k = pl.program_id(0), pl.program_id(1)

        @pl.when(k == 0)
        def _():
            acc_ref[...] = jnp.zeros_like(acc_ref)

        @pl.when(k < nb - 1)
        def _():
            acc_ref[...] += _dot_nt(dp_ref[...], w_ref[...])

        @pl.when((i == 0) & (k == 0))
        def _():
            dg_ref[...] = jnp.zeros_like(dg_ref)

        @pl.when(k == nb - 1)
        def _():
            gv = g_ref[...]
            wv = w_ref[...]
            dg = jnp.zeros((1, d), F32)
            for r0 in range(0, tm, chunk):
                rows = slice(r0, r0 + chunk)
                dhv = acc_ref[rows, :] + _dot_nt(dp_ref[rows, :], wv)
                xv = x_ref[rows, :]
                r = lax.rsqrt(jnp.mean(xv * xv, axis=-1, keepdims=True) + NORM_EPS)
                nrm = xv * r
                dn = dhv * gv
                gx_ref[rows, :] = dx2_ref[rows, :] + r * (dn - nrm * jnp.mean(dn * nrm, axis=-1, keepdims=True))
                dg = dg + jnp.sum(dhv * nrm, axis=0, keepdims=True)
            dg_ref[...] += dg

    tile = pl.BlockSpec((tm, d), lambda i, k: (i, 0))
    vec = pl.BlockSpec((1, d), lambda i, k: (0, 0))
    return pl.pallas_call(
        body, name="grad_x", grid=(t // tm, nb),
        out_shape=(jax.ShapeDtypeStruct((t, d), F32), jax.ShapeDtypeStruct((1, d), F32)),
        in_specs=[pl.BlockSpec((tm, wb), lambda i, k: (i, k)), pl.BlockSpec((None, d, wb), lambda i, k: (k, 0, 0)),
                  tile, tile, vec],
        out_specs=(tile, vec), scratch_shapes=[pltpu.VMEM((tm, d), F32)], compiler_params=_params())(dproj, w_bm, x, dx2, g)


def _shift_down(v, d, fill):
    n = v.shape[0]
    if d % 8 == 0:
        return jnp.concatenate([jnp.full((d,) + v.shape[1:], fill, v.dtype), v[: n - d]], axis=0)
    row = lax.broadcasted_iota(jnp.int32, v.shape, 0)
    return jnp.where(row >= d, pltpu.roll(v, d, axis=0), fill)


def _shift_up(v, d, fill):
    n = v.shape[0]
    if d % 8 == 0:
        return jnp.concatenate([v[d:], jnp.full((d,) + v.shape[1:], fill, v.dtype)], axis=0)
    row = lax.broadcasted_iota(jnp.int32, v.shape, 0)
    return jnp.where(row < n - d, pltpu.roll(v, n - d, axis=0), fill)


def _scan_log(a, b, shift):
    n = a.shape[0]
    d = 1
    while d < n:
        b = a * shift(b, d, 0.0) + b
        if 2 * d < n:
            a = a * shift(a, d, 1.0)
        d *= 2
    return b


SUBLANES = 8


def _scan(a, b, sa_ref, sb_ref, reverse):
    n, c = a.shape
    g = n // SUBLANES
    a3, b3 = a.reshape(g, SUBLANES, c), b.reshape(g, SUBLANES, c)
    sub = lax.broadcasted_iota(jnp.int32, a3.shape, 1)
    d = 1
    while d < SUBLANES:
        keep = (sub < SUBLANES - d) if reverse else (sub >= d)
        amount = SUBLANES - d if reverse else d
        b3 = a3 * jnp.where(keep, pltpu.roll(b3, amount, axis=1), 0.0) + b3
        a3 = a3 * jnp.where(keep, pltpu.roll(a3, amount, axis=1), 1.0)
        d *= 2
    sa_ref[...] = a3.reshape(n, c)
    sb_ref[...] = b3.reshape(n, c)
    edge = 0 if reverse else SUBLANES - 1
    shift = _shift_up if reverse else _shift_down
    totals = _scan_log(sa_ref[pl.ds(edge, g, stride=SUBLANES), :], sb_ref[pl.ds(edge, g, stride=SUBLANES), :], shift)
    carry = shift(totals, 1, 0.0)
    return (a3 * carry[:, None, :] + b3).reshape(n, c)


def _neg_expm1_twice(log_a, a):
    return -jnp.tanh(log_a) * (a * a + 1.0)


def _softplus(z):
    e = jnp.exp(-jnp.abs(z))
    w = 1.0 + e
    log1p = jnp.where(w == 1.0, e, jnp.log(w) * (e / jnp.where(w == 1.0, 1.0, w - 1.0)))
    return jnp.maximum(z, 0.0) + log1p


def _conv(up, cw, cb):
    out = cb + cw[CONV_WIDTH - 1:CONV_WIDTH, :] * up
    for j in range(CONV_WIDTH - 1):
        out = out + cw[j:j + 1, :] * _shift_down(up, CONV_WIDTH - 1 - j, 0.0)
    return out


def _lru_gates(u, wa_ref, ba_ref, wx_ref, bx_ref, lam_ref):
    ub = u.astype(BF16)
    r = _sigmoid(_dot(ub, wa_ref[...].astype(BF16)) + ba_ref[...])
    i = _sigmoid(_dot(ub, wx_ref[...].astype(BF16)) + bx_ref[...])
    sp = _softplus(-lam_ref[...])
    log_a = (-LRU_C) * r * sp
    a = jnp.exp(log_a)
    mult = jnp.sqrt(_neg_expm1_twice(log_a, a))
    return r, i, sp, a, mult


def _lru_specs(s):
    cb = RNN_BLOCK
    vec = pl.BlockSpec((1, cb), lambda n, b: (0, n))
    return dict(
        up=pl.BlockSpec((None, s, cb), lambda n, b: (b, 0, OFF_RNN_X // cb + n)),
        gr=pl.BlockSpec((None, s, cb), lambda n, b: (b, 0, OFF_RNN_G // cb + n)),
        act=pl.BlockSpec((None, s, cb), lambda n, b: (b, 0, n)),
        cw=pl.BlockSpec((CONV_WIDTH, cb), lambda n, b: (0, n)),
        vec=vec,
        wblk=pl.BlockSpec((None, cb, cb), lambda n, b: (n, 0, 0)),
    )


def _lru_fwd(proj3, cw, cb, wa, ba, wx, bx, lam, riders):
    bsz, s, _ = proj3.shape
    sp = _lru_specs(s)
    nr = len(riders)

    def body(up_ref, gr_ref, cw_ref, cb_ref, wa_ref, ba_ref, wx_ref, bx_ref, lam_ref, *refs):
        rider_in, (h_ref, y_ref), rider_out = refs[:nr], refs[nr:nr + 2], refs[nr + 2:2 * nr + 2]
        sa_ref, sb_ref = refs[2 * nr + 2:2 * nr + 4]
        start, pass_on, finish = _row_gather(rider_in, rider_out, *refs[2 * nr + 4:])
        step = pl.program_id(0) * bsz + pl.program_id(1)
        first, last = step == 0, step == N_RNN_BLOCKS * bsz - 1
        pl.when(first)(start)
        pl.when(step == (3 * N_RNN_BLOCKS * bsz) // 4)(pass_on)
        u = _conv(up_ref[...], cw_ref[...], cb_ref[...])
        _, i, _, a, mult = _lru_gates(u, wa_ref, ba_ref, wx_ref, bx_ref, lam_ref)
        h = _scan(a, mult * (i * u), sa_ref, sb_ref, reverse=False)
        h_ref[...] = h
        g = gr_ref[...]
        y_ref[...] = (h * (g * _sigmoid(g))).astype(BF16)
        pl.when(last)(finish)

    res = pl.pallas_call(
        body, name="lru_fwd", grid=(N_RNN_BLOCKS, bsz),
        out_shape=[jax.ShapeDtypeStruct((bsz, s, D_RNN), F32), jax.ShapeDtypeStruct((bsz, s, D_RNN), BF16)] + [
            jax.ShapeDtypeStruct(r.shape, r.dtype) for r in riders],
        in_specs=[sp["up"], sp["gr"], sp["cw"], sp["vec"], sp["wblk"], sp["vec"], sp["wblk"], sp["vec"], sp["vec"]] + [
            ANY] * nr,
        out_specs=[sp["act"], sp["act"]] + [ANY] * nr, input_output_aliases={9 + t: 2 + t for t in range(nr)},
        scratch_shapes=[pltpu.VMEM((s, RNN_BLOCK), F32)] * 2 + [pltpu.SemaphoreType.DMA((3 * nr,))] * 4,
        compiler_params=_params())(proj3, proj3, cw, cb, wa, ba, wx, bx, lam, *riders)
    return res[0], res[1], res[2:]


def _lru_bwd(proj3, h3, dy3, dproj3, cw, cb, wa, ba, wx, bx, lam):
    bsz, s, _ = proj3.shape
    sp = _lru_specs(s)
    n_steps = N_RNN_BLOCKS * bsz

    def body(up_ref, gr_ref, h_ref, dy_ref, cw_ref, cb_ref, wa_ref, ba_ref, wx_ref, bx_ref, lam_ref, dp_in,
             dp_ref, dcw_ref, dcb_ref, dwa_ref, dba_ref, dwx_ref, dbx_ref, dlam_ref, sa_ref, sb_ref,
             dup_st, dgr_st, o_sems):
        del dp_in
        blk = pl.program_id(0)
        b = pl.program_id(1)
        step = blk * bsz + b
        slot = step % 2
        stages = [dup_st, dgr_st]
        dsts = [dp_ref.at[b, :, pl.ds(pl.multiple_of(OFF_RNN_X + blk * RNN_BLOCK, LANES), RNN_BLOCK)],
                dp_ref.at[b, :, pl.ds(pl.multiple_of(OFF_RNN_G + blk * RNN_BLOCK, LANES), RNN_BLOCK)]]
        _staged_reuse(step, stages, dsts, o_sems, slot)
        up = up_ref[...]
        cwv = cw_ref[...]
        u = _conv(up, cwv, cb_ref[...])
        r, i, spv, a, mult = _lru_gates(u, wa_ref, ba_ref, wx_ref, bx_ref, lam_ref)
        h = h_ref[...]
        g = gr_ref[...]
        dy = dy_ref[...]
        sg = _sigmoid(g)
        dgr_st[slot] = (dy * h * (sg * (1.0 + g * (1.0 - sg)))).astype(BF16)
        dh = dy * (g * sg)
        adj = _scan(_shift_up(a, 1, 0.0), dh, sa_ref, sb_ref, reverse=True)
        da = adj * _shift_down(h, 1, 0.0)
        dmult = adj * (i * u)
        di = adj * mult * u
        du = adj * mult * i
        dla = da * a - dmult * (a * a) / mult
        dr = dla * ((-LRU_C) * spv)
        dsp = jnp.sum(dla * ((-LRU_C) * r), axis=0, keepdims=True)
        dza = dr * r * (1.0 - r)
        dzx = di * i * (1.0 - i)
        ub = u.astype(BF16)
        dzab = dza.astype(BF16)
        dzxb = dzx.astype(BF16)
        du = du + _dot_nt(dzab, wa_ref[...].astype(BF16)) + _dot_nt(dzxb, wx_ref[...].astype(BF16))
        dup = cwv[CONV_WIDTH - 1:CONV_WIDTH, :] * du
        for j in range(CONV_WIDTH - 1):
            dup = dup + cwv[j:j + 1, :] * _shift_up(du, CONV_WIDTH - 1 - j, 0.0)
        dup_st[slot] = dup.astype(BF16)
        _staged_flush(step, n_steps, stages, dsts, o_sems, slot)

        @pl.when(b == 0)
        def _():
            for ref in (dcw_ref, dcb_ref, dwa_ref, dba_ref, dwx_ref, dbx_ref, dlam_ref):
                ref[...] = jnp.zeros_like(ref)

        rows = [jnp.sum(du * _shift_down(up, CONV_WIDTH - 1 - j, 0.0), axis=0, keepdims=True)
                for j in range(CONV_WIDTH - 1)]
        rows.append(jnp.sum(du * up, axis=0, keepdims=True))
        dcw_ref[...] += jnp.concatenate(rows, axis=0)
        dcb_ref[...] += jnp.sum(du, axis=0, keepdims=True)
        dwa_ref[...] += _dot_tn(ub, dzab)
        dba_ref[...] += jnp.sum(dza, axis=0, keepdims=True)
        dwx_ref[...] += _dot_tn(ub, dzxb)
        dbx_ref[...] += jnp.sum(dzx, axis=0, keepdims=True)
        dlam_ref[...] += dsp * (-_sigmoid(-lam_ref[...]))

    vec = jax.ShapeDtypeStruct((1, D_RNN), F32)
    wsd = jax.ShapeDtypeStruct((N_RNN_BLOCKS, RNN_BLOCK, RNN_BLOCK), F32)
    return pl.pallas_call(
        body, name="lru_bwd", grid=(N_RNN_BLOCKS, bsz),
        out_shape=(jax.ShapeDtypeStruct(dproj3.shape, dproj3.dtype), jax.ShapeDtypeStruct((CONV_WIDTH, D_RNN), F32),
                   vec, wsd, vec, wsd, vec, vec),
        in_specs=[sp["up"], sp["gr"], sp["act"], sp["act"], sp["cw"], sp["vec"], sp["wblk"], sp["vec"],
                  sp["wblk"], sp["vec"], sp["vec"], ANY],
        out_specs=(ANY, sp["cw"], sp["vec"], sp["wblk"], sp["vec"], sp["wblk"], sp["vec"], sp["vec"]),
        input_output_aliases={11: 0},
        scratch_shapes=[pltpu.VMEM((s, RNN_BLOCK), F32)] * 2 + [pltpu.VMEM((2, s, RNN_BLOCK), BF16)] * 2 + [
            pltpu.SemaphoreType.DMA((4,))],
        compiler_params=_params())(proj3, proj3, h3, dy3, cw, cb, wa, ba, wx, bx, lam, dproj3)


def _rope_tables(s):
    half = ROPE_DIM // 2
    pos = jnp.arange(s, dtype=F32)
    inv_freq = ROPE_THETA ** (-jnp.arange(0, ROPE_DIM, 2, dtype=F32) / ROPE_DIM)
    ang = pos[:, None] * inv_freq[None, :]
    cos, sin = jnp.cos(ang), jnp.sin(ang)
    rest = HEAD_DIM - ROPE_DIM
    cos64 = jnp.concatenate([cos, cos, jnp.ones((s, rest), F32)], axis=1)
    sin64 = jnp.concatenate([-sin, sin, jnp.zeros((s, rest), F32)], axis=1)
    assert half * 2 == ROPE_DIM
    return jnp.tile(cos64, (1, LANES // HEAD_DIM)), jnp.tile(sin64, (1, LANES // HEAD_DIM))


def _swap_rot_halves(v):
    half = ROPE_DIM // 2
    lane = lax.broadcasted_iota(jnp.int32, v.shape, 1) % HEAD_DIM
    second = jnp.where(lane < ROPE_DIM, pltpu.roll(v, half, axis=1), 0.0)
    return jnp.where(lane < half, pltpu.roll(v, LANES - half, axis=1), second)


def _rope(v, cos, sin):
    tiles = []
    for t in range(v.shape[1] // LANES):
        vt = v[:, t * LANES:(t + 1) * LANES]
        tiles.append(vt * cos + _swap_rot_halves(vt) * sin)
    return tiles[0] if len(tiles) == 1 else jnp.concatenate(tiles, axis=1)


def _unrope(v, cos, sin):
    tiles = []
    for t in range(v.shape[1] // LANES):
        vt = v[:, t * LANES:(t + 1) * LANES]
        tiles.append(vt * cos + _swap_rot_halves(vt * sin))
    return tiles[0] if len(tiles) == 1 else jnp.concatenate(tiles, axis=1)


HEADS_PER_STEP = 8
QW = HEADS_PER_STEP * HEAD_DIM
N_PAIRS = N_Q_HEADS // HEADS_PER_STEP
Q_PER_KV = N_Q_HEADS // N_KV_HEADS
KV_PER_STEP = HEADS_PER_STEP // Q_PER_KV


QT_COLS = Q_PER_KV * ATT_BLK


def _attn_saved_shapes(bsz, s):
    nb = s // ATT_BLK
    pad = s + ATT_BLK
    return [(bsz, N_PAIRS, nb, LANES, QT_COLS), (bsz, N_PAIRS, KV_PER_STEP, pad, LANES),
            (bsz, N_PAIRS, KV_PER_STEP, pad, LANES), (bsz, N_PAIRS, LANES, pad)]


def _attn_specs(s, order):
    def mk(width, base, **kw):
        if order == "bp":
            return pl.BlockSpec((None, s, width), lambda b, p: (b, 0, base + p), **kw)
        return pl.BlockSpec((None, s, width), lambda p, b: (b, 0, base + p), **kw)

    def saved(shape, **kw):
        blk = (None, None) + shape[2:]
        zeros = (0,) * (len(shape) - 2)
        if order == "bp":
            return pl.BlockSpec(blk, lambda b, p: (b, p) + zeros, **kw)
        return pl.BlockSpec(blk, lambda p, b: (b, p) + zeros, **kw)

    one = dict(pipeline_mode=pl.Buffered(1))
    tbl = pl.BlockSpec((s, LANES), lambda *_: (0, 0))
    shapes = _attn_saved_shapes(1, s)
    return dict(q=mk(QW, OFF_Q // QW), k=mk(LANES, OFF_K // LANES), v=mk(LANES, OFF_V // LANES),
                g=mk(QW, OFF_ATTN_G // QW), act=mk(QW, 0), kv=mk(LANES, 0), tbl=tbl,
                g1=mk(QW, OFF_ATTN_G // QW, **one), act1=mk(QW, 0, **one),
                saved=[saved(sh) for sh in shapes], saved1=[saved(sh, **one) for sh in shapes],
                smem=pl.BlockSpec(memory_space=pltpu.SMEM))


def _to_qt(blk):
    rows = []
    for j in range(KV_PER_STEP):
        cols = []
        for tt in range(2):
            t = 2 * j + tt
            tr = blk[:, t * LANES:(t + 1) * LANES].T
            cols += [tr[0:HEAD_DIM, :], tr[HEAD_DIM:, :]]
        rows.append(jnp.concatenate(cols, axis=1))
    return jnp.concatenate(rows, axis=0)


def _from_qt(xt):
    tiles = []
    for j in range(KV_PER_STEP):
        for tt in range(2):
            g0 = 2 * tt
            pair = jnp.concatenate([xt[j * HEAD_DIM:(j + 1) * HEAD_DIM, (g0 + i) * ATT_BLK:(g0 + i + 1) * ATT_BLK]
                                    for i in range(2)], axis=0)
            tiles.append(pair.T)
    return jnp.concatenate(tiles, axis=1)


def _attn_prep(q_ref, k_ref, v_ref, cos_ref, sin_ref, qt_ref, km_ref, vm_ref, kt_ref, vt_ref, nb):
    zeros = jnp.zeros((ATT_BLK, LANES), BF16)
    for j in range(KV_PER_STEP):
        km_ref[j, 0:ATT_BLK, :] = zeros
        vm_ref[j, 0:ATT_BLK, :] = zeros
    kt_ref[:, 0:ATT_BLK] = zeros
    vt_ref[:, 0:ATT_BLK] = zeros
    head_of_lane = lax.broadcasted_iota(jnp.int32, (ATT_BLK, LANES), 1) // HEAD_DIM

    def prep(n, carry):
        r0 = pl.multiple_of(n * ATT_BLK, ATT_BLK)
        cs = cos_ref[pl.ds(r0, ATT_BLK), :]
        sn = sin_ref[pl.ds(r0, ATT_BLK), :]
        qt_ref[n] = _to_qt(_rope(q_ref[pl.ds(r0, ATT_BLK), :], cs, sn) * ATTN_SCALE).astype(BF16)
        k = _rope(k_ref[pl.ds(r0, ATT_BLK), :], cs, sn)
        v = v_ref[pl.ds(r0, ATT_BLK), :]
        for j in range(KV_PER_STEP):
            km_ref[j, pl.ds(r0 + ATT_BLK, ATT_BLK), :] = jnp.where(head_of_lane == j, k, 0.0).astype(BF16)
            vm_ref[j, pl.ds(r0 + ATT_BLK, ATT_BLK), :] = jnp.where(head_of_lane == j, v, 0.0).astype(BF16)
        kt_ref[:, pl.ds(r0 + ATT_BLK, ATT_BLK)] = k.T.astype(BF16)
        vt_ref[:, pl.ds(r0 + ATT_BLK, ATT_BLK)] = v.T.astype(BF16)
        return carry

    lax.fori_loop(0, nb, prep, 0)


def _from_prev_block():
    key = lax.broadcasted_iota(jnp.int32, (ATT_BLK, QT_COLS), 0)
    qry = lax.broadcasted_iota(jnp.int32, (ATT_BLK, QT_COLS), 1) % ATT_BLK
    return key > qry


def _fold(tile, prev, prev_bias=None):
    top = tile[:ATT_BLK] if prev_bias is None else tile[:ATT_BLK] + prev_bias
    return jnp.where(prev, top, tile[ATT_BLK:])


def _unfold(folded, prev):
    zero = jnp.zeros_like(folded)
    return jnp.concatenate([jnp.where(prev, folded, zero), jnp.where(prev, zero, folded)], axis=0).astype(BF16)


def _no_prev_bias(n):
    return jnp.where(n == 0, NEG_BIG, 0.0).astype(F32)


def _sink_row(sink_ref, first):
    return jnp.concatenate([jnp.full((1, ATT_BLK), sink_ref[first + g], F32) for g in range(Q_PER_KV)], axis=1)


def _softmax_cols(sc, sink):
    m = jnp.maximum(jnp.max(sc, axis=0, keepdims=True), sink)
    e = jnp.exp(sc - m)
    es = jnp.exp(sink - m)
    inv = 1.0 / (jnp.sum(e, axis=0, keepdims=True) + es)
    return e * inv, es * inv


def _attn_fwd(proj3, sinks, cosf, sinf):
    bsz, s, _ = proj3.shape
    nb = s // ATT_BLK
    sp = _attn_specs(s, "bp")

    def body(sink_ref, q_ref, k_ref, v_ref, g_ref, cos_ref, sin_ref, o_ref, y_ref, qt_sc, km_sc, vm_sc, kt_ref, vt_sc):
        p = pl.program_id(1)
        _attn_prep(q_ref, k_ref, v_ref, cos_ref, sin_ref, qt_sc, km_sc, vm_sc, kt_ref, vt_sc, nb)
        kv_row = lax.broadcasted_iota(jnp.int32, (LANES, QT_COLS), 0) // HEAD_DIM
        prev = _from_prev_block()

        def blk(n, carry):
            r0 = pl.multiple_of(n * ATT_BLK, ATT_BLK)
            bias = _no_prev_bias(n)
            rq = qt_sc[n]
            vt = vt_sc[:, pl.ds(r0, 2 * ATT_BLK)]
            ots = []
            for j in range(KV_PER_STEP):
                st = _dot(km_sc[j, pl.ds(r0, 2 * ATT_BLK), :], rq)
                pc, _ = _softmax_cols(_fold(st, prev, bias), _sink_row(sink_ref, p * HEADS_PER_STEP + j * Q_PER_KV))
                ots.append(_dot(vt, _unfold(pc, prev)))
            o = _from_qt(jnp.where(kv_row == 0, ots[0], ots[1]))
            o_ref[pl.ds(r0, ATT_BLK), :] = o
            g = g_ref[pl.ds(r0, ATT_BLK), :]
            y_ref[pl.ds(r0, ATT_BLK), :] = (o * (g * _sigmoid(g))).astype(BF16)
            return carry

        lax.fori_loop(0, nb, blk, 0, unroll=4)

    res = pl.pallas_call(
        body, name="attn_fwd", grid=(bsz, N_PAIRS),
        out_shape=[jax.ShapeDtypeStruct((bsz, s, D_ATTN), F32), jax.ShapeDtypeStruct((bsz, s, D_ATTN), BF16)] + [
            jax.ShapeDtypeStruct(sh, BF16) for sh in _attn_saved_shapes(bsz, s)],
        in_specs=[sp["smem"], sp["q"], sp["k"], sp["v"], sp["g"], sp["tbl"], sp["tbl"]],
        out_specs=[sp["act"], sp["act"]] + sp["saved"],
        scratch_shapes=[pltpu.VMEM((LANES, s + ATT_BLK), BF16)],
        compiler_params=_params())(sinks, proj3, proj3, proj3, proj3, cosf, sinf)
    return res[0], res[1], res[2:]


def _attn_bwd(proj3, saved, o3, dy3, dproj3, sinks, cosf, sinf):
    bsz, s, _ = proj3.shape
    nb = s // ATT_BLK
    assert nb % 2 == 0
    sp = _attn_specs(s, "pb")
    n_steps = N_PAIRS * bsz

    def body(sink_ref, qt_sc, km_sc, vm_sc, kt_sc, g_ref, o_ref, dy_ref, cos_ref, sin_ref, dp_in,
             dp_ref, ds_ref, dot_sc, dqt_sc, dk_sc, dv_sc, dq_st, dk_st, dv_st, dg_st, o_sems):
        del dp_in
        p = pl.program_id(0)
        b = pl.program_id(1)
        step = p * bsz + b
        slot = step % 2
        stages = [dq_st, dk_st, dv_st, dg_st]
        dsts = [dp_ref.at[b, :, pl.ds(pl.multiple_of(OFF_Q + p * QW, LANES), QW)],
                dp_ref.at[b, :, pl.ds(pl.multiple_of(OFF_K + p * LANES, LANES), LANES)],
                dp_ref.at[b, :, pl.ds(pl.multiple_of(OFF_V + p * LANES, LANES), LANES)],
                dp_ref.at[b, :, pl.ds(pl.multiple_of(OFF_ATTN_G + p * QW, LANES), QW)]]
        _staged_reuse(step, stages, dsts, o_sems, slot)
        dk_sc[...] = jnp.zeros_like(dk_sc)
        dv_sc[...] = jnp.zeros_like(dv_sc)

        def gate(n, carry):
            r0 = pl.multiple_of(n * ATT_BLK, ATT_BLK)
            g = g_ref[pl.ds(r0, ATT_BLK), :]
            dy = dy_ref[pl.ds(r0, ATT_BLK), :]
            sg = _sigmoid(g)
            dg_st[slot, pl.ds(r0, ATT_BLK), :] = (dy * o_ref[pl.ds(r0, ATT_BLK), :] * (sg * (1.0 + g * (1.0 - sg)))).astype(BF16)
            dot_sc[n] = _to_qt(dy * (g * sg)).astype(BF16)
            return carry

        lax.fori_loop(0, nb, gate, 0)
        kv_lane = lax.broadcasted_iota(jnp.int32, (2 * ATT_BLK, LANES), 1) // HEAD_DIM
        kv_row = lax.broadcasted_iota(jnp.int32, (LANES, QT_COLS), 0) // HEAD_DIM
        prev = _from_prev_block()

        def blk(n, acc):
            r0 = pl.multiple_of(n * ATT_BLK, ATT_BLK)
            bias = _no_prev_bias(n)
            rq = qt_sc[n]
            rd = dot_sc[n]
            kt = kt_sc[:, pl.ds(r0, 2 * ATT_BLK)]
            dvs, dks, dqs, new_acc = [], [], [], []
            for j in range(KV_PER_STEP):
                st = _dot(km_sc[j, pl.ds(r0, 2 * ATT_BLK), :], rq)
                pc, ps = _softmax_cols(_fold(st, prev, bias), _sink_row(sink_ref, p * HEADS_PER_STEP + j * Q_PER_KV))
                dpc = _fold(_dot(vm_sc[j, pl.ds(r0, 2 * ATT_BLK), :], rd), prev)
                delta = jnp.sum(pc * dpc, axis=0, keepdims=True)
                dst = _unfold(pc * (dpc - delta), prev)
                new_acc.append(acc[j] + ps * delta)
                dvs.append(_dot_nt(_unfold(pc, prev), rd))
                dks.append(_dot_nt(dst, rq))
                dqs.append(_dot(kt, dst))
            dv_sc[pl.ds(r0, 2 * ATT_BLK), :] += jnp.where(kv_lane == 0, dvs[0], dvs[1])
            dk_sc[pl.ds(r0, 2 * ATT_BLK), :] += jnp.where(kv_lane == 0, dks[0], dks[1])
            dqt_sc[n] = jnp.where(kv_row == 0, dqs[0], dqs[1]) * ATTN_SCALE
            return tuple(new_acc)

        per_trip = 4 if nb % 4 == 0 else 2

        def blk_group(m, acc):
            for u in range(per_trip):
                acc = blk(per_trip * m + u, acc)
            return acc

        acc = lax.fori_loop(0, nb // per_trip, blk_group, tuple(jnp.zeros((1, QT_COLS), F32) for _ in range(KV_PER_STEP)))
        lane1 = lax.broadcasted_iota(jnp.int32, (1, LANES), 1)
        dsink = jnp.zeros((1, LANES), F32)
        for j in range(KV_PER_STEP):
            for i in range(Q_PER_KV):
                part = jnp.sum(acc[j][:, i * ATT_BLK:(i + 1) * ATT_BLK], axis=1, keepdims=True)
                dsink = dsink - jnp.where(lane1 == j * Q_PER_KV + i, part, 0.0)

        @pl.when(b == 0)
        def _():
            ds_ref[...] = jnp.zeros_like(ds_ref)

        ds_ref[...] += dsink

        def post(n, carry):
            r0 = pl.multiple_of(n * ATT_BLK, ATT_BLK)
            cs = cos_ref[pl.ds(r0, ATT_BLK), :]
            sn = sin_ref[pl.ds(r0, ATT_BLK), :]
            dq_st[slot, pl.ds(r0, ATT_BLK), :] = _unrope(_from_qt(dqt_sc[n]), cs, sn).astype(BF16)
            dk_st[slot, pl.ds(r0, ATT_BLK), :] = _unrope(dk_sc[pl.ds(r0 + ATT_BLK, ATT_BLK), :], cs, sn).astype(BF16)
            dv_st[slot, pl.ds(r0, ATT_BLK), :] = dv_sc[pl.ds(r0 + ATT_BLK, ATT_BLK), :].astype(BF16)
            return carry

        lax.fori_loop(0, nb, post, 0)
        _staged_flush(step, n_steps, stages, dsts, o_sems, slot)

    n_in = 1 + len(saved) + 5
    return pl.pallas_call(
        body, name="attn_bwd", grid=(N_PAIRS, bsz),
        out_shape=(jax.ShapeDtypeStruct(dproj3.shape, dproj3.dtype), jax.ShapeDtypeStruct((N_PAIRS, 1, LANES), F32)),
        in_specs=[sp["smem"]] + sp["saved1"] + [sp["g1"], sp["act1"], sp["act1"], sp["tbl"], sp["tbl"], ANY],
        out_specs=(ANY, pl.BlockSpec((None, 1, LANES), lambda p, b: (p, 0, 0))),
        input_output_aliases={n_in: 0},
        scratch_shapes=[pltpu.VMEM((nb, LANES, QT_COLS), BF16),
                        pltpu.VMEM((nb, LANES, QT_COLS), F32),
                        pltpu.VMEM((s + ATT_BLK, LANES), F32),
                        pltpu.VMEM((s + ATT_BLK, LANES), F32),
                        pltpu.VMEM((2, s, QW), BF16), pltpu.VMEM((2, s, LANES), BF16),
                        pltpu.VMEM((2, s, LANES), BF16), pltpu.VMEM((2, s, QW), BF16),
                        pltpu.SemaphoreType.DMA((8,))],
        compiler_params=_params())(sinks, *saved, proj3, o3, dy3, cosf, sinf, dproj3)


def _staged_copies(stages, dsts, sems, slot):
    return [pltpu.make_async_copy(st.at[slot], dst, sems.at[slot * len(stages) + t])
            for t, (st, dst) in enumerate(zip(stages, dsts))]


def _staged_reuse(step, stages, dsts, sems, slot):
    @pl.when(step >= 2)
    def _():
        for cp in _staged_copies(stages, dsts, sems, slot):
            cp.wait()


def _staged_flush(step, n_steps, stages, dsts, sems, slot):
    for cp in _staged_copies(stages, dsts, sems, slot):
        cp.start()

    @pl.when(step == n_steps - 1)
    def _():
        for cp in _staged_copies(stages, dsts, sems, slot):
            cp.wait()
        if n_steps >= 2:
            for cp in _staged_copies(stages, dsts, sems, 1 - slot):
                cp.wait()


def _merge_fwd_bwd(x, tgt, y_rnn, y_attn, proj, w_r, w_a, w_o, gf):
    t, d = x.shape
    tm = min(t, 256)
    nt = t // tm

    hw = d // 2

    def body(x_ref, t_ref, yr_ref, ya_ref, mr0_ref, mr1_ref, ma0_ref, ma1_ref, wr_ref, wa_ref, wo_ref, gf_ref,
             dp_ref, dyr_ref, dya_ref, mg_ref, dx2_ref, dx2b_ref, dpr_ref, dpa_ref, loss_ref, dgf_ref, dmg_st, o_sems):
        i = pl.program_id(0)
        slot = i % 2
        dsts = [dp_ref.at[pl.ds(pl.multiple_of(i * tm, tm), tm), pl.ds(OFF_MERGE_R, 2 * d)]]
        _staged_reuse(i, [dmg_st], dsts, o_sems, slot)
        wr = wr_ref[...]
        wa = wa_ref[...]
        wo = wo_ref[...]
        gfv = gf_ref[...]
        pr = _dot(yr_ref[...], wr)
        pa = _dot(ya_ref[...], wa)
        sr = _sigmoid(jnp.concatenate([mr0_ref[...], mr1_ref[...]], axis=1))
        sa = _sigmoid(jnp.concatenate([ma0_ref[...], ma1_ref[...]], axis=1))
        mb = (sr * pr + sa * pa).astype(BF16)
        mg_ref[...] = mb
        x2 = x_ref[...] + _dot(mb, wo)
        r2 = lax.rsqrt(jnp.mean(x2 * x2, axis=-1, keepdims=True) + NORM_EPS)
        nrm = x2 * r2
        err = nrm * gfv - t_ref[...]
        dy = err * (1.0 / d)
        dn = dy * gfv
        dx2 = r2 * (dn - nrm * jnp.mean(dn * nrm, axis=-1, keepdims=True))
        dx2_ref[...] = dx2
        dx2b = dx2.astype(BF16)
        dx2b_ref[...] = dx2b
        dmerged = _dot_nt(dx2b, wo)
        dpr = (dmerged * sr).astype(BF16)
        dpa = (dmerged * sa).astype(BF16)
        dpr_ref[...] = dpr
        dpa_ref[...] = dpa
        dmg_st[slot, :, 0:d] = (dmerged * pr * (sr * (1.0 - sr))).astype(BF16)
        dmg_st[slot, :, d:2 * d] = (dmerged * pa * (sa * (1.0 - sa))).astype(BF16)
        _staged_flush(i, nt, [dmg_st], dsts, o_sems, slot)
        dyr_ref[...] = _dot_nt(dpr, wr)
        dya_ref[...] = _dot_nt(dpa, wa)

        @pl.when(i == 0)
        def _():
            loss_ref[...] = jnp.zeros_like(loss_ref)
            dgf_ref[...] = jnp.zeros_like(dgf_ref)

        loss_ref[...] += jnp.full((1, LANES), 0.5 / d, F32) * jnp.sum(err * err)
        dgf_ref[...] += jnp.sum(dy * nrm, axis=0, keepdims=True)

    tile = pl.BlockSpec((tm, d), lambda i: (i, 0))
    wsp = pl.BlockSpec((d, d), lambda i: (0, 0))

    def gate(col_blk):
        return pl.BlockSpec((tm, hw), lambda i: (i, col_blk))

    fb = jax.ShapeDtypeStruct((t, d), BF16)
    ff = jax.ShapeDtypeStruct((t, d), F32)
    return pl.pallas_call(
        body, name="merge_fwd_bwd", grid=(nt,),
        out_shape=(jax.ShapeDtypeStruct((t, D_IN), BF16), ff, ff, fb, ff, fb, fb, fb,
                   jax.ShapeDtypeStruct((1, LANES), F32), jax.ShapeDtypeStruct((1, d), F32)),
        in_specs=[tile, tile, tile, tile] + [gate(OFF_MERGE_R // hw + j) for j in range(4)] + [
            wsp, wsp, wsp, pl.BlockSpec((1, d), lambda i: (0, 0))],
        out_specs=(ANY, tile, tile, tile, tile, tile, tile, tile,
                   pl.BlockSpec((1, LANES), lambda i: (0, 0)), pl.BlockSpec((1, d), lambda i: (0, 0))),
        scratch_shapes=[pltpu.VMEM((2, tm, 2 * d), BF16), pltpu.SemaphoreType.DMA((2,))],
        compiler_params=_params())(x, tgt, y_rnn, y_attn, proj, proj, proj, proj, w_r, w_a, w_o, gf)


def _local_grads(x, tgt, h, proj, norm_g, w_in_bm, conv_w, conv_b, lru_w_a, lru_b_a, lru_w_x, lru_b_x, lam, sinks,
                 row_sharded, gf):
    bsz, s, d = x.shape
    t = bsz * s
    x2 = x.reshape(t, d)
    proj3 = proj.reshape(bsz, s, D_IN)
    h_lru, y_rnn, gathered = _lru_fwd(proj3, conv_w, conv_b, lru_w_a, lru_b_a, lru_w_x, lru_b_x, lam, row_sharded)
    w_r, w_a, w_o = (g.reshape(d, d) for g in gathered)
    cosf, sinf = _rope_tables(s)
    o_attn, y_attn, attn_saved = _attn_fwd(proj3, sinks, cosf, sinf)
    y_rnn2 = y_rnn.reshape(t, d)
    y_attn2 = y_attn.reshape(t, d)
    dproj, dyr, dya, merged, dx2, dx2b, dpr, dpa, loss, dgf = _merge_fwd_bwd(
        x2, tgt.reshape(t, d), y_rnn2, y_attn2, proj, w_r, w_a, w_o, gf)
    dproj3, dsink = _attn_bwd(proj3, attn_saved, o_attn, dya.reshape(bsz, s, d), dproj.reshape(bsz, s, D_IN),
                              sinks, cosf, sinf)
    dproj3, dcw, dcb, dwa, dba, dwx, dbx, dlam = _lru_bwd(
        proj3, h_lru, dyr.reshape(bsz, s, d), dproj3, conv_w, conv_b, lru_w_a, lru_b_a, lru_w_x, lru_b_x, lam)
    dproj = dproj3.reshape(t, D_IN)
    grad_x, dng = _grad_x(dproj, w_in_bm, x2, dx2, norm_g)
    small = dict(norm_g=dng, conv_w=dcw, conv_b=dcb, lru_w_a=dwa, lru_b_a=dba, lru_w_x=dwx, lru_b_x=dbx,
                 lru_lambda=dlam, attn_sinks=dsink[:, 0, :HEADS_PER_STEP].reshape(1, N_Q_HEADS), final_norm_g=dgf)
    squares = [(y_rnn2, dpr), (y_attn2, dpa), (merged, dx2b)]
    return loss[0, 0], grad_x.reshape(bsz, s, d), h, dproj, squares, small


ANY = pl.BlockSpec(memory_space=pl.ANY)


def _mesh_pos():
    return lax.axis_index("x"), lax.axis_index("y"), lax.axis_index("c")


def _remote(src, dst, send_sems, recv_sems, idx, peer):
    return pltpu.make_async_remote_copy(src_ref=src, dst_ref=dst, send_sem=send_sems.at[idx],
                                        recv_sem=recv_sems.at[idx], device_id=peer, device_id_type=MESH)


def _row_gather(ins, outs, send_sems, recv_sems, fsend_sems, frecv_sems):
    n = len(ins)
    x, y, c = _mesh_pos()
    me = 2 * x + y
    sib = (x, y, 1 - c)
    peers = [((x, 1 - y, c), me ^ 1), ((1 - x, y, c), me ^ 2), ((1 - x, 1 - y, c), me ^ 3)]

    def half(ref, slot, t, which):
        hr = ins[t].shape[1] // 2
        return ref.at[slot, pl.ds(pl.multiple_of(which * hr, 8), hr), :]

    def ici(t, k):
        peer, pj = peers[k]
        src = half(ins[t], me, t, c)
        return (_remote(src, half(outs[t], me, t, c), send_sems, recv_sems, 3 * t + k, peer),
                _remote(src, half(outs[t], pj, t, c), send_sems, recv_sems, 3 * t + k, peer))

    def forward(t, k):
        got = half(outs[t], peers[k][1], t, c)
        return (_remote(got, got, fsend_sems, frecv_sems, 3 * t + k, sib),
                _remote(got, half(outs[t], peers[k][1], t, 1 - c), fsend_sems, frecv_sems, 3 * t + k, sib))

    pairs = [(t, k) for t in range(n) for k in range(3)]

    def start():
        for t, k in pairs:
            ici(t, k)[0].start()

    def pass_on():
        for t, k in pairs:
            ici(t, k)[1].wait_recv()
            forward(t, k)[0].start()

    def finish():
        for t, k in pairs:
            ici(t, k)[0].wait_send()
            forward(t, k)[0].wait_send()
            forward(t, k)[1].wait_recv()

    return start, pass_on, finish


def _gather_in_proj(x, g, bufs, split, idx):
    t_tok, d = x.shape
    n = len(bufs)
    tm = min(t_tok, 1024)
    nt = t_tok // tm
    n_fwd = 3 * sum(split)
    assert split[0]

    def body(idx_ref, x_ref, g_ref, *refs):
        ins, proj_ref, h_out, outs = refs[:n], refs[n], refs[n + 1], refs[n + 2:2 * n + 2]
        wbuf, h_all, send_sems, recv_sems, fsend_sems, frecv_sems, l_sems = refs[2 * n + 2:]
        j, i = pl.program_id(0), pl.program_id(1)
        rows = pl.ds(pl.multiple_of(i * tm, tm), tm)
        x, y, c = _mesh_pos()
        me = 2 * x + y
        sib = (x, y, 1 - c)
        peers = [((x, 1 - y, c), me ^ 1), ((1 - x, y, c), me ^ 2), ((1 - x, 1 - y, c), me ^ 3)]

        def part(ref, slot, t, half):
            if not split[t]:
                return ref.at[slot]
            hr = bufs[t].shape[1] // 2
            return ref.at[slot, pl.ds(pl.multiple_of(half * hr, 8), hr), :]

        def land(t):
            return wbuf if t == 0 else outs[t]

        def ici(t, k):
            peer, pj = peers[k]
            src = part(ins[t], me, t, c)
            return (_remote(src, part(land(t), me, t, c), send_sems, recv_sems, 3 * t + k, peer),
                    _remote(src, part(land(t), pj, t, c), send_sems, recv_sems, 3 * t + k, peer))

        fwd_index = {}
        for t in range(n):
            if split[t]:
                for k in range(3):
                    fwd_index[(t, k)] = len(fwd_index)

        def forward(t, k):
            pj = peers[k][1]
            got = part(land(t), pj, t, c)
            f = fwd_index[(t, k)]
            return (_remote(got, got, fsend_sems, frecv_sems, f, sib),
                    _remote(got, part(land(t), pj, t, 1 - c), fsend_sems, frecv_sems, f, sib))

        def write_back(k):
            pj = peers[k][1]
            return pltpu.make_async_copy(wbuf.at[pj], outs[0].at[pj], l_sems.at[1 + k])

        relay_peer = ((x + c) % 2, (y + 1 - c) % 2, c)

        def relay():
            got = part(wbuf, me ^ (2 - c), 0, c)
            return (_remote(got, got, send_sems, recv_sems, 2, relay_peer),
                    _remote(got, part(wbuf, me ^ 3, 0, c), send_sems, recv_sems, 2, relay_peer))

        direct = [(t, k) for t in range(n) for k in range(3) if (t, k) != (0, 2)]

        @pl.when((j == 0) & (i == 0))
        def _():
            for t, k in direct:
                ici(t, k)[0].start()
            own = pltpu.make_async_copy(ins[0].at[me], wbuf.at[me], l_sems.at[0])
            own.start()
            own.wait()

        @pl.when((j == 1) & (i == 0))
        def _():
            pltpu.make_async_copy(h_all, h_out, l_sems.at[4]).start()
            for k in range(2):
                ici(0, k)[1].wait_recv()
            relay()[0].start()
            for k in range(2):
                forward(0, k)[0].start()
            forward(0, 0)[1].wait_recv()
            write_back(0).start()

        @pl.when((j == 2) & (i == 0))
        def _():
            forward(0, 1)[1].wait_recv()
            write_back(1).start()

        @pl.when((j == 3) & (i == 0))
        def _():
            relay()[1].wait_recv()
            forward(0, 2)[0].start()
            forward(0, 2)[1].wait_recv()
            write_back(2).start()

        @pl.when(j == 0)
        def _():
            xv = x_ref[...]
            r = lax.rsqrt(jnp.mean(xv * xv, axis=-1, keepdims=True) + NORM_EPS)
            h_all[rows, :] = (xv * r * g_ref[...]).astype(BF16)

        proj_ref[...] = _dot(h_all[rows, :], wbuf[me ^ j])

        @pl.when((j == N_CHIPS - 1) & (i == nt - 1))
        def _():
            pltpu.make_async_copy(h_all, h_out, l_sems.at[4]).wait()
            for t in range(1, n):
                for k in range(3):
                    ici(t, k)[1].wait_recv()
                    if split[t]:
                        forward(t, k)[0].start()
            relay()[0].wait_send()
            for t, k in direct:
                ici(t, k)[0].wait_send()
            for t in range(n):
                if split[t]:
                    for k in range(3):
                        forward(t, k)[0].wait_send()
                        if t > 0:
                            forward(t, k)[1].wait_recv()
            for k in range(3):
                write_back(k).wait()

    grid_spec = pltpu.PrefetchScalarGridSpec(
        num_scalar_prefetch=1, grid=(N_CHIPS, nt),
        in_specs=[pl.BlockSpec((tm, d), lambda j, i, idx_ref: (jnp.where(j == 0, i, nt - 1), 0)),
                  pl.BlockSpec((1, d), lambda j, i, idx_ref: (0, 0))] + [ANY] * n,
        out_specs=[pl.BlockSpec((tm, W_BLK), lambda j, i, idx_ref: (i, idx_ref[0] ^ j)), ANY] + [ANY] * n,
        scratch_shapes=[pltpu.VMEM(bufs[0].shape, bufs[0].dtype), pltpu.VMEM((t_tok, d), BF16),
                        pltpu.SemaphoreType.DMA((3 * n,)), pltpu.SemaphoreType.DMA((3 * n,)),
                        pltpu.SemaphoreType.DMA((n_fwd,)), pltpu.SemaphoreType.DMA((n_fwd,)),
                        pltpu.SemaphoreType.DMA((5,))])
    out_shape = [jax.ShapeDtypeStruct((t_tok, D_IN), F32), jax.ShapeDtypeStruct((t_tok, d), BF16)] + [
        jax.ShapeDtypeStruct(a.shape, a.dtype) for a in bufs]
    res = pl.pallas_call(
        body, name="gather_in_proj", grid_spec=grid_spec, out_shape=out_shape,
        input_output_aliases={3 + t: 2 + t for t in range(n)}, compiler_params=_params())(idx, x, g, *bufs)
    return res[1], res[0], res[2:]


def _row_tile(rows, row_bytes, cap_bytes=2 * 1024 * 1024):
    best = None
    for tr in range(8, rows + 1, 8):
        if rows % tr == 0 and tr * row_bytes <= cap_bytes:
            best = tr
    return best if best is not None else rows


XOR_ORDER = (3, 2, 1)


def _grads_reduce_scatter(h, dproj, squares, small, idx):
    t, d = h.shape
    nsq = len(squares)
    hr = d // 2
    qr = ROW_BLK // 2
    tk = min(t, 1024)
    nk = t // tk
    last = N_CHIPS - 1
    n_phase = 3

    def dest(s, idx_ref):
        xo = jnp.where(s == 0, XOR_ORDER[0], jnp.where(s == 1, XOR_ORDER[1], jnp.where(s == 2, XOR_ORDER[2], 0)))
        return idx_ref[0] ^ xo

    def k_sq(p, k):
        return jnp.where(p == 0, k, nk - 1)

    def k_w(p, k):
        return jnp.where(p == 0, 0, k)

    in_specs = [
        pl.BlockSpec((tk, hr), lambda s, p, k, idx_ref: (k_w(p, k), (1 - idx_ref[1] + jnp.maximum(p - 1, 0)) % 2)),
        pl.BlockSpec((tk, W_BLK), lambda s, p, k, idx_ref: (k_w(p, k), dest(s, idx_ref)))]
    for q in range(nsq):
        in_specs.append(pl.BlockSpec((tk, ROW_BLK), lambda s, p, k, idx_ref: (k_sq(p, k), dest(s, idx_ref))))
        in_specs.append(pl.BlockSpec((tk, d), lambda s, p, k, idx_ref: (k_sq(p, k), 0)))

    def body(idx_ref, *refs):
        nj = 1 + nsq
        h_ref, dp_ref = refs[0], refs[1]
        sq_in = refs[2:2 + 2 * nsq]
        small_in = refs[2 * nj]
        outs = refs[2 * nj + 1:3 * nj + 2]
        landing = refs[3 * nj + 2:4 * nj + 3]
        sc = refs[4 * nj + 3:]
        acc_w, xr_w, sb_w = sc[0:3]
        sq_sc = [sc[3 + 3 * q:6 + 3 * q] for q in range(nsq)]
        sm, smx = sc[3 * nj:3 * nj + 2]
        x_send, x_recv, i_send, i_recv, f_send, f_recv, o_sem, l_sem = sc[3 * nj + 2:]
        s, p, k = pl.program_id(0), pl.program_id(1), pl.program_id(2)
        x, y, c = _mesh_pos()
        sib = (x, y, 1 - c)
        peers = [((1 - x) if xo & 2 else x, (1 - y) if xo & 1 else y, c) for xo in XOR_ORDER]
        slot = s % 2
        mine_w = pl.ds(pl.multiple_of(c * hr, 8), hr)
        theirs_w = pl.ds(pl.multiple_of((1 - c) * hr, 8), hr)
        mine_q = pl.ds(pl.multiple_of(c * qr, 8), qr)
        theirs_q = pl.ds(pl.multiple_of((1 - c) * qr, 8), qr)

        def exch(j, src, dst):
            return _remote(src, dst, x_send, x_recv, 2 * j + slot, sib)

        sbufs = [sb_w] + [sq_sc[q][2] for q in range(nsq)]

        def ici(j, ss):
            return _remote(sbufs[j].at[ss], landing[j].at[ss], i_send, i_recv, last * j + ss, peers[ss])

        def exchanges():
            cps = [exch(0, acc_w.at[0], xr_w.at[slot])]
            cps += [exch(1 + q, sq_sc[q][0].at[theirs_q, :], sq_sc[q][1].at[slot]) for q in range(nsq)]
            return cps

        def small_send(ss):
            return _remote(sm.at[c], landing[nj].at[ss], i_send, i_recv, last * nj + ss, peers[ss])

        def small_start():
            load = pltpu.make_async_copy(small_in, sm, l_sem.at[nj + 1])
            load.start()
            load.wait()
            swap = _remote(sm, smx.at[pl.ds(0, 2)], x_send, x_recv, 2 * nj, sib)
            swap.start()
            swap.wait_recv()
            swap.wait_send()
            sm[...] = sm[...] + smx[0:2]
            for ss in range(last):
                small_send(ss).start()

        def pair_ref(j):
            return acc_w.at[1] if j == 0 else sq_sc[j - 1][0].at[mine_q, :]

        def sq_phase():
            pl.when((s == 0) & (k == 0))(small_start)
            for q in range(nsq):
                acc = sq_sc[q][0]

                @pl.when(k == 0)
                def _():
                    acc[...] = jnp.zeros((ROW_BLK, d), F32)

                acc[...] += _dot_tn(sq_in[2 * q][...], sq_in[2 * q + 1][...])

            @pl.when(k == nk - 1)
            def _():
                for cp in exchanges()[1:]:
                    cp.start()

        def w_phase(hf):
            @pl.when(k == 0)
            def _():
                acc_w[hf] = jnp.zeros((hr, W_BLK), F32)

            acc_w[hf] += _dot_tn(h_ref[...], dp_ref[...])

            @pl.when(k == nk - 1)
            def _():
                if hf == 0:
                    exchanges()[0].start()
                else:
                    finish_step()

        def finish_step():
            for cp in exchanges():
                cp.wait_recv()
                cp.wait_send()
            acc_w[1] += xr_w[slot]
            for q in range(nsq):
                sq_sc[q][0][mine_q, :] += sq_sc[q][1][slot]
            for ss in range(last):
                @pl.when(s == ss)
                def _():
                    for j in range(nj):
                        sbufs[j][ss] = pair_ref(j)[...].astype(BF16)
                        ici(j, ss).start()

            @pl.when(s == last)
            def _():
                for ss in range(last):
                    for j in range(nj):
                        ici(j, ss).wait_recv()
                        ici(j, ss).wait_send()
                    small_send(ss).wait_recv()
                    small_send(ss).wait_send()
                stage = [pltpu.make_async_copy(landing[j], sbufs[j], l_sem.at[j]) for j in range(nj)]
                stage.append(pltpu.make_async_copy(landing[nj], smx, l_sem.at[nj]))
                for cp in stage:
                    cp.start()
                for j in range(nj):
                    stage[j].wait()
                    total = pair_ref(j)[...]
                    for ss in range(last):
                        total = total + sbufs[j][ss].astype(F32)
                    pair_ref(j)[...] = total
                stage[nj].wait()
                by_xor = {xo: smx[ss] for ss, xo in enumerate(XOR_ORDER)}
                sm[c] = (sm[c] + by_xor[1]) + (by_xor[2] + by_xor[3])
                done = [(acc_w.at[1], outs[0].at[mine_w, :], outs[0].at[theirs_w, :])]
                done += [(pair_ref(1 + q), outs[1 + q].at[mine_q, :], outs[1 + q].at[theirs_q, :]) for q in range(nsq)]
                done.append((sm.at[c], outs[nj].at[c], outs[nj].at[1 - c]))
                copies = []
                for j, (src, mine, theirs) in enumerate(done):
                    keep = pltpu.make_async_copy(src, mine, o_sem.at[j])
                    give = _remote(src, mine, f_send, f_recv, j, sib)
                    take = _remote(src, theirs, f_send, f_recv, j, sib)
                    keep.start()
                    give.start()
                    copies.append((keep, give, take))
                for keep, give, take in copies:
                    keep.wait()
                    give.wait_send()
                    take.wait_recv()

        pl.when(p == 0)(sq_phase)
        for hf in range(2):
            pl.when(p == 1 + hf)(functools.partial(w_phase, hf))

    nj = 1 + nsq
    scratch = [pltpu.VMEM((2, hr, W_BLK), F32), pltpu.VMEM((2, hr, W_BLK), F32), pltpu.VMEM((last, hr, W_BLK), BF16)]
    for _ in range(nsq):
        scratch += [pltpu.VMEM((ROW_BLK, d), F32), pltpu.VMEM((2, qr, d), F32), pltpu.VMEM((last, qr, d), BF16)]
    scratch += [pltpu.VMEM((2, PK_HALF, LANES), F32), pltpu.VMEM((last, PK_HALF, LANES), F32)]
    scratch += [pltpu.SemaphoreType.DMA((2 * nj + 1,)), pltpu.SemaphoreType.DMA((2 * nj + 1,)),
                pltpu.SemaphoreType.DMA((last * (nj + 1),)), pltpu.SemaphoreType.DMA((last * (nj + 1),)),
                pltpu.SemaphoreType.DMA((nj + 1,)), pltpu.SemaphoreType.DMA((nj + 1,)),
                pltpu.SemaphoreType.DMA((nj + 1,)), pltpu.SemaphoreType.DMA((nj + 2,))]
    grid_spec = pltpu.PrefetchScalarGridSpec(
        num_scalar_prefetch=1, grid=(N_CHIPS, n_phase, nk), in_specs=in_specs + [ANY],
        out_specs=[ANY] * (2 * nj + 2), scratch_shapes=scratch)
    out_shape = [jax.ShapeDtypeStruct((d, W_BLK), F32)] + [jax.ShapeDtypeStruct((ROW_BLK, d), F32)] * nsq
    out_shape.append(jax.ShapeDtypeStruct((2, PK_HALF, LANES), F32))
    out_shape += [jax.ShapeDtypeStruct((last, hr, W_BLK), BF16)] + [jax.ShapeDtypeStruct((last, qr, d), BF16)] * nsq
    out_shape.append(jax.ShapeDtypeStruct((last, PK_HALF, LANES), F32))
    flat = [a for pair in squares for a in pair]
    res = pl.pallas_call(body, name="grads_reduce_scatter", grid_spec=grid_spec, out_shape=out_shape,
                         compiler_params=_params())(idx, h, dproj, *flat, small)
    return res[:nj + 1]


_VEC_NAMES = ("norm_g", "conv_b", "lru_b_a", "lru_b_x", "lru_lambda", "final_norm_g")


def _pack_small(p, conv_full=None, scalar=None):
    rows = [p["lru_w_a"].reshape(PK_WX - PK_WA, LANES), p["lru_w_x"].reshape(PK_VEC - PK_WX, LANES)]
    rows += [p[k].reshape(8, LANES) for k in _VEC_NAMES]
    rows.append(jnp.pad(p["attn_sinks"].reshape(1, N_Q_HEADS), ((0, 7), (0, LANES - N_Q_HEADS))))
    rows.append(jnp.zeros((32, LANES), F32) if conv_full is None else conv_full.reshape(32, LANES))
    tail = PK_ROWS - PK_SCALAR
    if scalar is None:
        rows.append(jnp.zeros((tail, LANES), F32))
    else:
        rows.append(jnp.pad(scalar.reshape(1, 1), ((0, tail - 1), (0, LANES - 1))))
    return jnp.concatenate(rows, axis=0)


def _unpack_small(pk, like):
    out = {"lru_w_a": pk[PK_WA:PK_WX].reshape(like["lru_w_a"].shape),
           "lru_w_x": pk[PK_WX:PK_VEC].reshape(like["lru_w_x"].shape)}
    for j, k in enumerate(_VEC_NAMES):
        out[k] = pk[PK_VEC + 8 * j:PK_VEC + 8 * j + 8].reshape(like[k].shape)
    out["attn_sinks"] = pk[PK_SINK:PK_SINK + 1, :N_Q_HEADS].reshape(like["attn_sinks"].shape)
    return out


_WEIGHTS = ("norm_g", "w_in", "conv_w", "conv_b", "lru_w_a", "lru_b_a", "lru_w_x", "lru_b_x", "lru_lambda",
            "attn_sinks", "w_rnn_out", "w_attn_out", "w_o", "final_norm_g")
_SMALL = ("norm_g", "conv_b", "lru_w_a", "lru_b_a", "lru_w_x", "lru_b_x", "lru_lambda", "attn_sinks", "final_norm_g")
_ROW_SHARDED = ("w_rnn_out", "w_attn_out", "w_o")


def kernel(x, norm_g, w_in, conv_w, conv_b, lru_w_a, lru_b_a, lru_w_x, lru_b_x, lru_lambda, attn_sinks, w_rnn_out, w_attn_out, w_o, final_norm_g, loss_target, m_norm_g, m_w_in, m_conv_w, m_conv_b, m_lru_w_a, m_lru_b_a, m_lru_w_x, m_lru_b_x, m_lru_lambda, m_attn_sinks, m_w_rnn_out, m_w_attn_out, m_w_o, m_final_norm_g, v_norm_g, v_w_in, v_conv_w, v_conv_b, v_lru_w_a, v_lru_b_a, v_lru_w_x, v_lru_b_x, v_lru_lambda, v_attn_sinks, v_w_rnn_out, v_w_attn_out, v_w_o, v_final_norm_g):
    w = dict(norm_g=norm_g, w_in=w_in, conv_w=conv_w, conv_b=conv_b, lru_w_a=lru_w_a, lru_b_a=lru_b_a, lru_w_x=lru_w_x,
             lru_b_x=lru_b_x, lru_lambda=lru_lambda, attn_sinks=attn_sinks, w_rnn_out=w_rnn_out, w_attn_out=w_attn_out,
             w_o=w_o, final_norm_g=final_norm_g)
    m = dict(norm_g=m_norm_g, w_in=m_w_in, conv_w=m_conv_w, conv_b=m_conv_b, lru_w_a=m_lru_w_a, lru_b_a=m_lru_b_a,
             lru_w_x=m_lru_w_x, lru_b_x=m_lru_b_x, lru_lambda=m_lru_lambda, attn_sinks=m_attn_sinks,
             w_rnn_out=m_w_rnn_out, w_attn_out=m_w_attn_out, w_o=m_w_o, final_norm_g=m_final_norm_g)
    v = dict(norm_g=v_norm_g, w_in=v_w_in, conv_w=v_conv_w, conv_b=v_conv_b, lru_w_a=v_lru_w_a, lru_b_a=v_lru_b_a,
             lru_w_x=v_lru_w_x, lru_b_x=v_lru_b_x, lru_lambda=v_lru_lambda, attn_sinks=v_attn_sinks,
             w_rnn_out=v_w_rnn_out, w_attn_out=v_w_attn_out, w_o=v_w_o, final_norm_g=v_final_norm_g)
    mx, my, mc = _mesh_pos()
    me = 2 * mx + my
    d = D_MODEL

    my_chip = jnp.reshape(me, (1,)).astype(jnp.int32)
    (buf_in,) = _put_slots([w["w_in"][0]], my_chip, BF16, "cast_w_in")
    (buf_cw,) = _put_slots([w["conv_w"][0]], my_chip, F32, "slot_conv_w")
    row_sharded = _put_slots([w[k][0] for k in _ROW_SHARDED], my_chip, BF16, "cast_row_sharded")
    h, proj, (g_in, g_cw) = _gather_in_proj(x.reshape(-1, d), w["norm_g"], [buf_in, buf_cw], [True, False], my_chip)
    conv_full = g_cw.transpose(1, 0, 2).reshape(CONV_WIDTH, D_RNN)

    loss_local, grad_x, h, dproj, squares, gsmall = _local_grads(
        x, loss_target, h, proj, w["norm_g"], g_in, conv_full, w["conv_b"], w["lru_w_a"][0], w["lru_b_a"], w["lru_w_x"][0],
        w["lru_b_x"], w["lru_lambda"], w["attn_sinks"][0], row_sharded, w["final_norm_g"].reshape(1, d))
    gpack = _pack_small(gsmall, gsmall["conv_w"], loss_local).reshape(2, PK_HALF, LANES)
    f_in, f_r, f_a, f_o, spack = _grads_reduce_scatter(h, dproj, squares, gpack, jnp.stack([me, mc]).astype(jnp.int32))
    spack = spack.reshape(PK_ROWS, LANES)
    loss = spack[PK_SCALAR, 0]

    grads = _unpack_small(spack, w)
    conv_all = spack[PK_CONV:PK_CONV + 32].reshape(CONV_WIDTH, D_RNN)
    grads["conv_w"] = lax.dynamic_slice_in_dim(conv_all, me * (D_RNN // N_CHIPS), D_RNN // N_CHIPS, axis=1)[None]
    grads["w_in"] = f_in[None]
    grads["w_rnn_out"], grads["w_attn_out"], grads["w_o"] = f_r[None], f_a[None], f_o[None]

    delta, new_m, new_v = {}, {}, {}
    def group(k):
        return w[k][0], grads[k][0], m[k][0], v[k][0]

    for names, call in ((("w_in",), "adamw_w_in"), (_ROW_SHARDED, "adamw_row_sharded")):
        for k, (dk, mk, vk, gk) in zip(names, _adamw([group(k) for k in names], call, echo_grad=True)):
            delta[k], new_m[k], new_v[k], grads[k] = dk[None], mk[None], vk[None], gk[None]
    shp = (2 * CONV_WIDTH, LANES)
    ((dk, mk, vk),) = _adamw([tuple(a.reshape(shp) for a in (w["conv_w"], grads["conv_w"], m["conv_w"], v["conv_w"]))],
                             "adamw_conv_w")
    delta["conv_w"], new_m["conv_w"], new_v["conv_w"] = (a.reshape(w["conv_w"].shape) for a in (dk, mk, vk))
    ((dk, mk, vk),) = _adamw([(_pack_small(w), spack, _pack_small(m), _pack_small(v))], "adamw_small")
    for src, dst in ((dk, delta), (mk, new_m), (vk, new_v)):
        dst.update(_unpack_small(src, w))

    return (loss, grad_x, *[grads[k] for k in _WEIGHTS], *[delta[k] for k in _WEIGHTS],
            *[new_m[k] for k in _WEIGHTS], *[new_v[k] for k in _WEIGHTS])
```

```python
import functools
import math

import jax
import jax.numpy as jnp
from jax import lax
from jax.experimental import pallas as pl
from jax.experimental.pallas import tpu as pltpu

F32 = jnp.float32
BF16 = jnp.bfloat16
MESH = pl.DeviceIdType.MESH

D_MODEL = 1024
D_RNN = 1024
N_RNN_BLOCKS = 8
RNN_BLOCK = D_RNN // N_RNN_BLOCKS
CONV_WIDTH = 4
LRU_C = 8.0
HEAD_DIM = 64
N_Q_HEADS = 16
N_KV_HEADS = 4
D_ATTN = N_Q_HEADS * HEAD_DIM
D_KV = N_KV_HEADS * HEAD_DIM
WINDOW = 128
ROPE_DIM = HEAD_DIM // 4
ROPE_THETA = 500000.0
NORM_EPS = 1e-6
OFF_RNN_X = 0
OFF_RNN_G = OFF_RNN_X + D_RNN
OFF_Q = OFF_RNN_G + D_RNN
OFF_K = OFF_Q + D_ATTN
OFF_V = OFF_K + D_KV
OFF_ATTN_G = OFF_V + D_KV
OFF_MERGE_R = OFF_ATTN_G + D_ATTN
OFF_MERGE_A = OFF_MERGE_R + D_MODEL
D_IN = OFF_MERGE_A + D_MODEL

ADAM_LR = 0.001
ADAM_B1 = 0.9
ADAM_B2 = 0.999
ADAM_EPS = 1e-08
ADAM_WD = 0.01
ADAM_STEP = 10

N_CHIPS = 4
W_BLK = D_IN // N_CHIPS
ROW_BLK = D_MODEL // N_CHIPS
LANES = 128
ATT_BLK = 128
VMEM_LIMIT = 56 * 1024 * 1024
NEG_BIG = -1e30
ATTN_SCALE = 1.0 / math.sqrt(HEAD_DIM)

PK_WA = 0
PK_WX = PK_WA + N_RNN_BLOCKS * RNN_BLOCK
PK_VEC = PK_WX + N_RNN_BLOCKS * RNN_BLOCK
PK_SINK = PK_VEC + 6 * 8
PK_CONV = PK_SINK + 8
PK_SCALAR = PK_CONV + 32
PK_ROWS = PK_SCALAR + 8
PK_HALF = PK_ROWS // 2


def _params(**kw):
    return pltpu.CompilerParams(vmem_limit_bytes=VMEM_LIMIT, **kw)


def _sigmoid(z):
    return 1.0 / (1.0 + jnp.exp(-z))


def _dot(a, b):
    return jnp.dot(a, b, preferred_element_type=F32)


def _dot_nt(a, b):
    return lax.dot_general(a, b, (((1,), (1,)), ((), ())), preferred_element_type=F32)


def _dot_tn(a, b):
    return lax.dot_general(a, b, (((0,), (0,)), ((), ())), preferred_element_type=F32)


def _put_slots(srcs, slot, dtype, name):
    rows, c = srcs[0].shape
    n = len(srcs)
    tr = _row_tile(rows, c * 4)

    def body(idx_ref, *refs):
        for s_ref, o_ref in zip(refs[:n], refs[n:]):
            o_ref[...] = s_ref[...].astype(dtype)

    grid_spec = pltpu.PrefetchScalarGridSpec(
        num_scalar_prefetch=1, grid=(rows // tr,),
        in_specs=[pl.BlockSpec((tr, c), lambda i, idx_ref: (i, 0))] * n,
        out_specs=[pl.BlockSpec((None, tr, c), lambda i, idx_ref: (idx_ref[0], i, 0))] * n)
    return pl.pallas_call(body, name=name, grid_spec=grid_spec,
                          out_shape=[jax.ShapeDtypeStruct((N_CHIPS, rows, c), dtype)] * n,
                          compiler_params=_params())(slot, *srcs)


def _adamw(groups, name, echo_grad=False):
    r, c = groups[0][0].shape
    n = len(groups)
    n_out = 4 if echo_grad else 3
    tr = _row_tile(r, c * 4, 1024 * 1024 // n)
    c1 = 1.0 - ADAM_B1 ** ADAM_STEP
    c2 = 1.0 - ADAM_B2 ** ADAM_STEP

    def body(*refs):
        for q in range(n):
            w_ref, g_ref, m_ref, v_ref = refs[4 * q:4 * q + 4]
            d_ref, nm_ref, nv_ref = refs[4 * n + n_out * q:4 * n + n_out * q + 3]
            gv = g_ref[...]
            if echo_grad:
                refs[4 * n + n_out * q + 3][...] = gv
            nm = ADAM_B1 * m_ref[...] + (1.0 - ADAM_B1) * gv
            nv = ADAM_B2 * v_ref[...] + (1.0 - ADAM_B2) * (gv * gv)
            m_hat = nm / c1
            v_hat = nv / c2
            d_ref[...] = -ADAM_LR * (m_hat / (jnp.sqrt(v_hat) + ADAM_EPS) + ADAM_WD * w_ref[...])
            nm_ref[...] = nm
            nv_ref[...] = nv

    spec = pl.BlockSpec((tr, c), lambda i: (i, 0))
    sds = jax.ShapeDtypeStruct((r, c), F32)
    res = pl.pallas_call(
        body, name=name, grid=(r // tr,), out_shape=[sds] * (n_out * n), in_specs=[spec] * (4 * n),
        out_specs=[spec] * (n_out * n), compiler_params=_params())(*[a for grp in groups for a in grp])
    return [tuple(res[n_out * q:n_out * q + n_out]) for q in range(n)]


def _grad_x(dproj, w_bm, x, dx2, g):
    t = dproj.shape[0]
    nb, d, wb = w_bm.shape
    tm = min(t, 512)
    chunk = min(tm, 256)

    def body(dp_ref, w_ref, x_ref, dx2_ref, g_ref, gx_ref, dg_ref, acc_ref):
        i, k = pl.program_id(0), pl.program_id(1)

        @pl.when(k == 0)
        def _():
            acc_ref[...] = jnp.zeros_like(acc_ref)

        @pl.when(k < nb - 1)
        def _():
            acc_ref[...] += _dot_nt(dp_ref[...], w_ref[...])

        @pl.when((i == 0) & (k == 0))
        def _():
            dg_ref[...] = jnp.zeros_like(dg_ref)

        @pl.when(k == nb - 1)
        def _():
            gv = g_ref[...]
            wv = w_ref[...]
            dg = jnp.zeros((1, d), F32)
            for r0 in range(0, tm, chunk):
                rows = slice(r0, r0 + chunk)
                dhv = acc_ref[rows, :] + _dot_nt(dp_ref[rows, :], wv)
                xv = x_ref[rows, :]
                r = lax.rsqrt(jnp.mean(xv * xv, axis=-1, keepdims=True) + NORM_EPS)
                nrm = xv * r
                dn = dhv * gv
                gx_ref[rows, :] = dx2_ref[rows, :] + r * (dn - nrm * jnp.mean(dn * nrm, axis=-1, keepdims=True))
                dg = dg + jnp.sum(dhv * nrm, axis=0, keepdims=True)
            dg_ref[...] += dg

    tile = pl.BlockSpec((tm, d), lambda i, k: (i, 0))
    vec = pl.BlockSpec((1, d), lambda i, k: (0, 0))
    return pl.pallas_call(
        body, name="grad_x", grid=(t // tm, nb),
        out_shape=(jax.ShapeDtypeStruct((t, d), F32), jax.ShapeDtypeStruct((1, d), F32)),
        in_specs=[pl.BlockSpec((tm, wb), lambda i, k: (i, k)), pl.BlockSpec((None, d, wb), lambda i, k: (k, 0, 0)),
                  tile, tile, vec],
        out_specs=(tile, vec), scratch_shapes=[pltpu.VMEM((tm, d), F32)], compiler_params=_params())(dproj, w_bm, x, dx2, g)


def _shift_down(v, d, fill):
    n = v.shape[0]
    if d % 8 == 0:
        return jnp.concatenate([jnp.full((d,) + v.shape[1:], fill, v.dtype), v[: n - d]], axis=0)
    row = lax.broadcasted_iota(jnp.int32, v.shape, 0)
    return jnp.where(row >= d, pltpu.roll(v, d, axis=0), fill)


def _shift_up(v, d, fill):
    n = v.shape[0]
    if d % 8 == 0:
        return jnp.concatenate([v[d:], jnp.full((d,) + v.shape[1:], fill, v.dtype)], axis=0)
    row = lax.broadcasted_iota(jnp.int32, v.shape, 0)
    return jnp.where(row < n - d, pltpu.roll(v, n - d, axis=0), fill)


def _scan_log(a, b, shift):
    n = a.shape[0]
    d = 1
    while d < n:
        b = a * shift(b, d, 0.0) + b
        if 2 * d < n:
            a = a * shift(a, d, 1.0)
        d *= 2
    return b


SUBLANES = 8


def _scan(a, b, sa_ref, sb_ref, reverse):
    n, c = a.shape
    g = n // SUBLANES
    a3, b3 = a.reshape(g, SUBLANES, c), b.reshape(g, SUBLANES, c)
    sub = lax.broadcasted_iota(jnp.int32, a3.shape, 1)
    d = 1
    while d < SUBLANES:
        keep = (sub < SUBLANES - d) if reverse else (sub >= d)
        amount = SUBLANES - d if reverse else d
        b3 = a3 * jnp.where(keep, pltpu.roll(b3, amount, axis=1), 0.0) + b3
        a3 = a3 * jnp.where(keep, pltpu.roll(a3, amount, axis=1), 1.0)
        d *= 2
    sa_ref[...] = a3.reshape(n, c)
    sb_ref[...] = b3.reshape(n, c)
    edge = 0 if reverse else SUBLANES - 1
    shift = _shift_up if reverse else _shift_down
    totals = _scan_log(sa_ref[pl.ds(edge, g, stride=SUBLANES), :], sb_ref[pl.ds(edge, g, stride=SUBLANES), :], shift)
    carry = shift(totals, 1, 0.0)
    return (a3 * carry[:, None, :] + b3).reshape(n, c)


def _neg_expm1_twice(log_a, a):
    return -jnp.tanh(log_a) * (a * a + 1.0)


def _softplus(z):
    e = jnp.exp(-jnp.abs(z))
    w = 1.0 + e
    log1p = jnp.where(w == 1.0, e, jnp.log(w) * (e / jnp.where(w == 1.0, 1.0, w - 1.0)))
    return jnp.maximum(z, 0.0) + log1p


def _conv(up, cw, cb):
    out = cb + cw[CONV_WIDTH - 1:CONV_WIDTH, :] * up
    for j in range(CONV_WIDTH - 1):
        out = out + cw[j:j + 1, :] * _shift_down(up, CONV_WIDTH - 1 - j, 0.0)
    return out


def _lru_gates(u, wa_ref, ba_ref, wx_ref, bx_ref, lam_ref):
    ub = u.astype(BF16)
    r = _sigmoid(_dot(ub, wa_ref[...].astype(BF16)) + ba_ref[...])
    i = _sigmoid(_dot(ub, wx_ref[...].astype(BF16)) + bx_ref[...])
    sp = _softplus(-lam_ref[...])
    log_a = (-LRU_C) * r * sp
    a = jnp.exp(log_a)
    mult = jnp.sqrt(_neg_expm1_twice(log_a, a))
    return r, i, sp, a, mult


def _lru_specs(s):
    cb = RNN_BLOCK
    vec = pl.BlockSpec((1, cb), lambda n, b: (0, n))
    return dict(
        up=pl.BlockSpec((None, s, cb), lambda n, b: (b, 0, OFF_RNN_X // cb + n)),
        gr=pl.BlockSpec((None, s, cb), lambda n, b: (b, 0, OFF_RNN_G // cb + n)),
        act=pl.BlockSpec((None, s, cb), lambda n, b: (b, 0, n)),
        cw=pl.BlockSpec((CONV_WIDTH, cb), lambda n, b: (0, n)),
        vec=vec,
        wblk=pl.BlockSpec((None, cb, cb), lambda n, b: (n, 0, 0)),
    )


def _lru_fwd(proj3, cw, cb, wa, ba, wx, bx, lam, riders):
    bsz, s, _ = proj3.shape
    sp = _lru_specs(s)
    nr = len(riders)

    def body(up_ref, gr_ref, cw_ref, cb_ref, wa_ref, ba_ref, wx_ref, bx_ref, lam_ref, *refs):
        rider_in, (h_ref, y_ref), rider_out = refs[:nr], refs[nr:nr + 2], refs[nr + 2:2 * nr + 2]
        sa_ref, sb_ref = refs[2 * nr + 2:2 * nr + 4]
        start, pass_on, finish = _row_gather(rider_in, rider_out, *refs[2 * nr + 4:])
        step = pl.program_id(0) * bsz + pl.program_id(1)
        first, last = step == 0, step == N_RNN_BLOCKS * bsz - 1
        pl.when(first)(start)
        pl.when(step == (3 * N_RNN_BLOCKS * bsz) // 4)(pass_on)
        u = _conv(up_ref[...], cw_ref[...], cb_ref[...])
        _, i, _, a, mult = _lru_gates(u, wa_ref, ba_ref, wx_ref, bx_ref, lam_ref)
        h = _scan(a, mult * (i * u), sa_ref, sb_ref, reverse=False)
        h_ref[...] = h
        g = gr_ref[...]
        y_ref[...] = (h * (g * _sigmoid(g))).astype(BF16)
        pl.when(last)(finish)

    res = pl.pallas_call(
        body, name="lru_fwd", grid=(N_RNN_BLOCKS, bsz),
        out_shape=[jax.ShapeDtypeStruct((bsz, s, D_RNN), F32), jax.ShapeDtypeStruct((bsz, s, D_RNN), BF16)] + [
            jax.ShapeDtypeStruct(r.shape, r.dtype) for r in riders],
        in_specs=[sp["up"], sp["gr"], sp["cw"], sp["vec"], sp["wblk"], sp["vec"], sp["wblk"], sp["vec"], sp["vec"]] + [
            ANY] * nr,
        out_specs=[sp["act"], sp["act"]] + [ANY] * nr, input_output_aliases={9 + t: 2 + t for t in range(nr)},
        scratch_shapes=[pltpu.VMEM((s, RNN_BLOCK), F32)] * 2 + [pltpu.SemaphoreType.DMA((3 * nr,))] * 4,
        compiler_params=_params())(proj3, proj3, cw, cb, wa, ba, wx, bx, lam, *riders)
    return res[0], res[1], res[2:]


def _lru_bwd(proj3, h3, dy3, dproj3, cw, cb, wa, ba, wx, bx, lam):
    bsz, s, _ = proj3.shape
    sp = _lru_specs(s)
    n_steps = N_RNN_BLOCKS * bsz

    def body(up_ref, gr_ref, h_ref, dy_ref, cw_ref, cb_ref, wa_ref, ba_ref, wx_ref, bx_ref, lam_ref, dp_in,
             dp_ref, dcw_ref, dcb_ref, dwa_ref, dba_ref, dwx_ref, dbx_ref, dlam_ref, sa_ref, sb_ref,
             dup_st, dgr_st, o_sems):
        del dp_in
        blk = pl.program_id(0)
        b = pl.program_id(1)
        step = blk * bsz + b
        slot = step % 2
        stages = [dup_st, dgr_st]
        dsts = [dp_ref.at[b, :, pl.ds(pl.multiple_of(OFF_RNN_X + blk * RNN_BLOCK, LANES), RNN_BLOCK)],
                dp_ref.at[b, :, pl.ds(pl.multiple_of(OFF_RNN_G + blk * RNN_BLOCK, LANES), RNN_BLOCK)]]
        _staged_reuse(step, stages, dsts, o_sems, slot)
        up = up_ref[...]
        cwv = cw_ref[...]
        u = _conv(up, cwv, cb_ref[...])
        r, i, spv, a, mult = _lru_gates(u, wa_ref, ba_ref, wx_ref, bx_ref, lam_ref)
        h = h_ref[...]
        g = gr_ref[...]
        dy = dy_ref[...]
        sg = _sigmoid(g)
        dgr_st[slot] = (dy * h * (sg * (1.0 + g * (1.0 - sg)))).astype(BF16)
        dh = dy * (g * sg)
        adj = _scan(_shift_up(a, 1, 0.0), dh, sa_ref, sb_ref, reverse=True)
        da = adj * _shift_down(h, 1, 0.0)
        dmult = adj * (i * u)
        di = adj * mult * u
        du = adj * mult * i
        dla = da * a - dmult * (a * a) / mult
        dr = dla * ((-LRU_C) * spv)
        dsp = jnp.sum(dla * ((-LRU_C) * r), axis=0, keepdims=True)
        dza = dr * r * (1.0 - r)
        dzx = di * i * (1.0 - i)
        ub = u.astype(BF16)
        dzab = dza.astype(BF16)
        dzxb = dzx.astype(BF16)
        du = du + _dot_nt(dzab, wa_ref[...].astype(BF16)) + _dot_nt(dzxb, wx_ref[...].astype(BF16))
        dup = cwv[CONV_WIDTH - 1:CONV_WIDTH, :] * du
        for j in range(CONV_WIDTH - 1):
            dup = dup + cwv[j:j + 1, :] * _shift_up(du, CONV_WIDTH - 1 - j, 0.0)
        dup_st[slot] = dup.astype(BF16)
        _staged_flush(step, n_steps, stages, dsts, o_sems, slot)

        @pl.when(b == 0)
        def _():
            for ref in (dcw_ref, dcb_ref, dwa_ref, dba_ref, dwx_ref, dbx_ref, dlam_ref):
                ref[...] = jnp.zeros_like(ref)

        rows = [jnp.sum(du * _shift_down(up, CONV_WIDTH - 1 - j, 0.0), axis=0, keepdims=True)
                for j in range(CONV_WIDTH - 1)]
        rows.append(jnp.sum(du * up, axis=0, keepdims=True))
        dcw_ref[...] += jnp.concatenate(rows, axis=0)
        dcb_ref[...] += jnp.sum(du, axis=0, keepdims=True)
        dwa_ref[...] += _dot_tn(ub, dzab)
        dba_ref[...] += jnp.sum(dza, axis=0, keepdims=True)
        dwx_ref[...] += _dot_tn(ub, dzxb)
        dbx_ref[...] += jnp.sum(dzx, axis=0, keepdims=True)
        dlam_ref[...] += dsp * (-_sigmoid(-lam_ref[...]))

    vec = jax.ShapeDtypeStruct((1, D_RNN), F32)
    wsd = jax.ShapeDtypeStruct((N_RNN_BLOCKS, RNN_BLOCK, RNN_BLOCK), F32)
    return pl.pallas_call(
        body, name="lru_bwd", grid=(N_RNN_BLOCKS, bsz),
        out_shape=(jax.ShapeDtypeStruct(dproj3.shape, dproj3.dtype), jax.ShapeDtypeStruct((CONV_WIDTH, D_RNN), F32),
                   vec, wsd, vec, wsd, vec, vec),
        in_specs=[sp["up"], sp["gr"], sp["act"], sp["act"], sp["cw"], sp["vec"], sp["wblk"], sp["vec"],
                  sp["wblk"], sp["vec"], sp["vec"], ANY],
        out_specs=(ANY, sp["cw"], sp["vec"], sp["wblk"], sp["vec"], sp["wblk"], sp["vec"], sp["vec"]),
        input_output_aliases={11: 0},
        scratch_shapes=[pltpu.VMEM((s, RNN_BLOCK), F32)] * 2 + [pltpu.VMEM((2, s, RNN_BLOCK), BF16)] * 2 + [
            pltpu.SemaphoreType.DMA((4,))],
        compiler_params=_params())(proj3, proj3, h3, dy3, cw, cb, wa, ba, wx, bx, lam, dproj3)


def _rope_tables(s):
    half = ROPE_DIM // 2
    pos = jnp.arange(s, dtype=F32)
    inv_freq = ROPE_THETA ** (-jnp.arange(0, ROPE_DIM, 2, dtype=F32) / ROPE_DIM)
    ang = pos[:, None] * inv_freq[None, :]
    cos, sin = jnp.cos(ang), jnp.sin(ang)
    rest = HEAD_DIM - ROPE_DIM
    cos64 = jnp.concatenate([cos, cos, jnp.ones((s, rest), F32)], axis=1)
    sin64 = jnp.concatenate([-sin, sin, jnp.zeros((s, rest), F32)], axis=1)
    assert half * 2 == ROPE_DIM
    return jnp.tile(cos64, (1, LANES // HEAD_DIM)), jnp.tile(sin64, (1, LANES // HEAD_DIM))


def _swap_rot_halves(v):
    half = ROPE_DIM // 2
    lane = lax.broadcasted_iota(jnp.int32, v.shape, 1) % HEAD_DIM
    second = jnp.where(lane < ROPE_DIM, pltpu.roll(v, half, axis=1), 0.0)
    return jnp.where(lane < half, pltpu.roll(v, LANES - half, axis=1), second)


def _rope(v, cos, sin):
    tiles = []
    for t in range(v.shape[1] // LANES):
        vt = v[:, t * LANES:(t + 1) * LANES]
        tiles.append(vt * cos + _swap_rot_halves(vt) * sin)
    return tiles[0] if len(tiles) == 1 else jnp.concatenate(tiles, axis=1)


def _unrope(v, cos, sin):
    tiles = []
    for t in range(v.shape[1] // LANES):
        vt = v[:, t * LANES:(t + 1) * LANES]
        tiles.append(vt * cos + _swap_rot_halves(vt * sin))
    return tiles[0] if len(tiles) == 1 else jnp.concatenate(tiles, axis=1)


HEADS_PER_STEP = 8
QW = HEADS_PER_STEP * HEAD_DIM
N_PAIRS = N_Q_HEADS // HEADS_PER_STEP
Q_PER_KV = N_Q_HEADS // N_KV_HEADS
KV_PER_STEP = HEADS_PER_STEP // Q_PER_KV


QT_COLS = Q_PER_KV * ATT_BLK


def _attn_saved_shapes(bsz, s):
    nb = s // ATT_BLK
    pad = s + ATT_BLK
    return [(bsz, N_PAIRS, nb, LANES, QT_COLS), (bsz, N_PAIRS, KV_PER_STEP, pad, LANES),
            (bsz, N_PAIRS, KV_PER_STEP, pad, LANES), (bsz, N_PAIRS, LANES, pad)]


def _attn_specs(s, order):
    def mk(width, base, **kw):
        if order == "bp":
            return pl.BlockSpec((None, s, width), lambda b, p: (b, 0, base + p), **kw)
        return pl.BlockSpec((None, s, width), lambda p, b: (b, 0, base + p), **kw)

    def saved(shape, **kw):
        blk = (None, None) + shape[2:]
        zeros = (0,) * (len(shape) - 2)
        if order == "bp":
            return pl.BlockSpec(blk, lambda b, p: (b, p) + zeros, **kw)
        return pl.BlockSpec(blk, lambda p, b: (b, p) + zeros, **kw)

    one = dict(pipeline_mode=pl.Buffered(1))
    tbl = pl.BlockSpec((s, LANES), lambda *_: (0, 0))
    shapes = _attn_saved_shapes(1, s)
    return dict(q=mk(QW, OFF_Q // QW), k=mk(LANES, OFF_K // LANES), v=mk(LANES, OFF_V // LANES),
                g=mk(QW, OFF_ATTN_G // QW), act=mk(QW, 0), kv=mk(LANES, 0), tbl=tbl,
                g1=mk(QW, OFF_ATTN_G // QW, **one), act1=mk(QW, 0, **one),
                saved=[saved(sh) for sh in shapes], saved1=[saved(sh, **one) for sh in shapes],
                smem=pl.BlockSpec(memory_space=pltpu.SMEM))


def _to_qt(blk):
    rows = []
    for j in range(KV_PER_STEP):
        cols = []
        for tt in range(2):
            t = 2 * j + tt
            tr = blk[:, t * LANES:(t + 1) * LANES].T
            cols += [tr[0:HEAD_DIM, :], tr[HEAD_DIM:, :]]
        rows.append(jnp.concatenate(cols, axis=1))
    return jnp.concatenate(rows, axis=0)


def _from_qt(xt):
    tiles = []
    for j in range(KV_PER_STEP):
        for tt in range(2):
            g0 = 2 * tt
            pair = jnp.concatenate([xt[j * HEAD_DIM:(j + 1) * HEAD_DIM, (g0 + i) * ATT_BLK:(g0 + i + 1) * ATT_BLK]
                                    for i in range(2)], axis=0)
            tiles.append(pair.T)
    return jnp.concatenate(tiles, axis=1)


def _attn_prep(q_ref, k_ref, v_ref, cos_ref, sin_ref, qt_ref, km_ref, vm_ref, kt_ref, vt_ref, nb):
    zeros = jnp.zeros((ATT_BLK, LANES), BF16)
    for j in range(KV_PER_STEP):
        km_ref[j, 0:ATT_BLK, :] = zeros
        vm_ref[j, 0:ATT_BLK, :] = zeros
    kt_ref[:, 0:ATT_BLK] = zeros
    vt_ref[:, 0:ATT_BLK] = zeros
    head_of_lane = lax.broadcasted_iota(jnp.int32, (ATT_BLK, LANES), 1) // HEAD_DIM

    def prep(n, carry):
        r0 = pl.multiple_of(n * ATT_BLK, ATT_BLK)
        cs = cos_ref[pl.ds(r0, ATT_BLK), :]
        sn = sin_ref[pl.ds(r0, ATT_BLK), :]
        qt_ref[n] = _to_qt(_rope(q_ref[pl.ds(r0, ATT_BLK), :], cs, sn) * ATTN_SCALE).astype(BF16)
        k = _rope(k_ref[pl.ds(r0, ATT_BLK), :], cs, sn)
        v = v_ref[pl.ds(r0, ATT_BLK), :]
        for j in range(KV_PER_STEP):
            km_ref[j, pl.ds(r0 + ATT_BLK, ATT_BLK), :] = jnp.where(head_of_lane == j, k, 0.0).astype(BF16)
            vm_ref[j, pl.ds(r0 + ATT_BLK, ATT_BLK), :] = jnp.where(head_of_lane == j, v, 0.0).astype(BF16)
        kt_ref[:, pl.ds(r0 + ATT_BLK, ATT_BLK)] = k.T.astype(BF16)
        vt_ref[:, pl.ds(r0 + ATT_BLK, ATT_BLK)] = v.T.astype(BF16)
        return carry

    lax.fori_loop(0, nb, prep, 0, unroll=2)


def _from_prev_block():
    key = lax.broadcasted_iota(jnp.int32, (ATT_BLK, QT_COLS), 0)
    qry = lax.broadcasted_iota(jnp.int32, (ATT_BLK, QT_COLS), 1) % ATT_BLK
    return key > qry


def _fold(tile, prev, prev_bias=None):
    top = tile[:ATT_BLK] if prev_bias is None else tile[:ATT_BLK] + prev_bias
    return jnp.where(prev, top, tile[ATT_BLK:])


def _unfold(folded, prev):
    zero = jnp.zeros_like(folded)
    return jnp.concatenate([jnp.where(prev, folded, zero), jnp.where(prev, zero, folded)], axis=0).astype(BF16)


def _no_prev_bias(n):
    return jnp.where(n == 0, NEG_BIG, 0.0).astype(F32)


def _sink_row(sink_ref, first):
    return jnp.concatenate([jnp.full((1, ATT_BLK), sink_ref[first + g], F32) for g in range(Q_PER_KV)], axis=1)


def _softmax_cols(sc, sink):
    m = jnp.maximum(jnp.max(sc, axis=0, keepdims=True), sink)
    e = jnp.exp(sc - m)
    es = jnp.exp(sink - m)
    inv = 1.0 / (jnp.sum(e, axis=0, keepdims=True) + es)
    return e * inv, es * inv


def _attn_fwd(proj3, sinks, cosf, sinf):
    bsz, s, _ = proj3.shape
    nb = s // ATT_BLK
    sp = _attn_specs(s, "bp")

    def body(sink_ref, q_ref, k_ref, v_ref, g_ref, cos_ref, sin_ref, o_ref, y_ref, qt_sc, km_sc, vm_sc, kt_ref, vt_sc):
        p = pl.program_id(1)
        _attn_prep(q_ref, k_ref, v_ref, cos_ref, sin_ref, qt_sc, km_sc, vm_sc, kt_ref, vt_sc, nb)
        kv_row = lax.broadcasted_iota(jnp.int32, (LANES, QT_COLS), 0) // HEAD_DIM
        prev = _from_prev_block()

        def blk(n, carry):
            r0 = pl.multiple_of(n * ATT_BLK, ATT_BLK)
            bias = _no_prev_bias(n)
            rq = qt_sc[n]
            vt = vt_sc[:, pl.ds(r0, 2 * ATT_BLK)]
            ots = []
            for j in range(KV_PER_STEP):
                st = _dot(km_sc[j, pl.ds(r0, 2 * ATT_BLK), :], rq)
                pc, _ = _softmax_cols(_fold(st, prev, bias), _sink_row(sink_ref, p * HEADS_PER_STEP + j * Q_PER_KV))
                ots.append(_dot(vt, _unfold(pc, prev)))
            o = _from_qt(jnp.where(kv_row == 0, ots[0], ots[1]))
            o_ref[pl.ds(r0, ATT_BLK), :] = o
            g = g_ref[pl.ds(r0, ATT_BLK), :]
            y_ref[pl.ds(r0, ATT_BLK), :] = (o * (g * _sigmoid(g))).astype(BF16)
            return carry

        lax.fori_loop(0, nb, blk, 0, unroll=4)

    res = pl.pallas_call(
        body, name="attn_fwd", grid=(bsz, N_PAIRS),
        out_shape=[jax.ShapeDtypeStruct((bsz, s, D_ATTN), F32), jax.ShapeDtypeStruct((bsz, s, D_ATTN), BF16)] + [
            jax.ShapeDtypeStruct(sh, BF16) for sh in _attn_saved_shapes(bsz, s)],
        in_specs=[sp["smem"], sp["q"], sp["k"], sp["v"], sp["g"], sp["tbl"], sp["tbl"]],
        out_specs=[sp["act"], sp["act"]] + sp["saved"],
        scratch_shapes=[pltpu.VMEM((LANES, s + ATT_BLK), BF16)],
        compiler_params=_params())(sinks, proj3, proj3, proj3, proj3, cosf, sinf)
    return res[0], res[1], res[2:]


def _attn_bwd(proj3, saved, o3, dy3, dproj3, sinks, cosf, sinf):
    bsz, s, _ = proj3.shape
    nb = s // ATT_BLK
    assert nb % 2 == 0
    sp = _attn_specs(s, "pb")
    n_steps = N_PAIRS * bsz

    def body(sink_ref, qt_sc, km_sc, vm_sc, kt_sc, g_ref, o_ref, dy_ref, cos_ref, sin_ref, dp_in,
             dp_ref, ds_ref, dot_sc, dqt_sc, dk_sc, dv_sc, dq_st, dk_st, dv_st, dg_st, o_sems):
        del dp_in
        p = pl.program_id(0)
        b = pl.program_id(1)
        step = p * bsz + b
        slot = step % 2
        stages = [dq_st, dk_st, dv_st, dg_st]
        dsts = [dp_ref.at[b, :, pl.ds(pl.multiple_of(OFF_Q + p * QW, LANES), QW)],
                dp_ref.at[b, :, pl.ds(pl.multiple_of(OFF_K + p * LANES, LANES), LANES)],
                dp_ref.at[b, :, pl.ds(pl.multiple_of(OFF_V + p * LANES, LANES), LANES)],
                dp_ref.at[b, :, pl.ds(pl.multiple_of(OFF_ATTN_G + p * QW, LANES), QW)]]
        _staged_reuse(step, stages, dsts, o_sems, slot)
        dk_sc[...] = jnp.zeros_like(dk_sc)
        dv_sc[...] = jnp.zeros_like(dv_sc)

        def gate(n, carry):
            r0 = pl.multiple_of(n * ATT_BLK, ATT_BLK)
            g = g_ref[pl.ds(r0, ATT_BLK), :]
            dy = dy_ref[pl.ds(r0, ATT_BLK), :]
            sg = _sigmoid(g)
            dg_st[slot, pl.ds(r0, ATT_BLK), :] = (dy * o_ref[pl.ds(r0, ATT_BLK), :] * (sg * (1.0 + g * (1.0 - sg)))).astype(BF16)
            dot_sc[n] = _to_qt(dy * (g * sg)).astype(BF16)
            return carry

        lax.fori_loop(0, nb, gate, 0, unroll=2)
        kv_lane = lax.broadcasted_iota(jnp.int32, (2 * ATT_BLK, LANES), 1) // HEAD_DIM
        kv_row = lax.broadcasted_iota(jnp.int32, (LANES, QT_COLS), 0) // HEAD_DIM
        prev = _from_prev_block()

        def blk(n, acc):
            r0 = pl.multiple_of(n * ATT_BLK, ATT_BLK)
            bias = _no_prev_bias(n)
            rq = qt_sc[n]
            rd = dot_sc[n]
            kt = kt_sc[:, pl.ds(r0, 2 * ATT_BLK)]
            dvs, dks, dqs, new_acc = [], [], [], []
            for j in range(KV_PER_STEP):
                st = _dot(km_sc[j, pl.ds(r0, 2 * ATT_BLK), :], rq)
                pc, ps = _softmax_cols(_fold(st, prev, bias), _sink_row(sink_ref, p * HEADS_PER_STEP + j * Q_PER_KV))
                dpc = _fold(_dot(vm_sc[j, pl.ds(r0, 2 * ATT_BLK), :], rd), prev)
                delta = jnp.sum(pc * dpc, axis=0, keepdims=True)
                dst = _unfold(pc * (dpc - delta), prev)
                new_acc.append(acc[j] + ps * delta)
                dvs.append(_dot_nt(_unfold(pc, prev), rd))
                dks.append(_dot_nt(dst, rq))
                dqs.append(_dot(kt, dst))
            dv_sc[pl.ds(r0, 2 * ATT_BLK), :] += jnp.where(kv_lane == 0, dvs[0], dvs[1])
            dk_sc[pl.ds(r0, 2 * ATT_BLK), :] += jnp.where(kv_lane == 0, dks[0], dks[1])
            dqt_sc[n] = jnp.where(kv_row == 0, dqs[0], dqs[1]) * ATTN_SCALE
            return tuple(new_acc)

        per_trip = 4 if nb % 4 == 0 else 2

        def blk_group(m, acc):
            for u in range(per_trip):
                acc = blk(per_trip * m + u, acc)
            return acc

        acc = lax.fori_loop(0, nb // per_trip, blk_group, tuple(jnp.zeros((1, QT_COLS), F32) for _ in range(KV_PER_STEP)))
        lane1 = lax.broadcasted_iota(jnp.int32, (1, LANES), 1)
        dsink = jnp.zeros((1, LANES), F32)
        for j in range(KV_PER_STEP):
            for i in range(Q_PER_KV):
                part = jnp.sum(acc[j][:, i * ATT_BLK:(i + 1) * ATT_BLK], axis=1, keepdims=True)
                dsink = dsink - jnp.where(lane1 == j * Q_PER_KV + i, part, 0.0)

        @pl.when(b == 0)
        def _():
            ds_ref[...] = jnp.zeros_like(ds_ref)

        ds_ref[...] += dsink

        def post(n, carry):
            r0 = pl.multiple_of(n * ATT_BLK, ATT_BLK)
            cs = cos_ref[pl.ds(r0, ATT_BLK), :]
            sn = sin_ref[pl.ds(r0, ATT_BLK), :]
            dq_st[slot, pl.ds(r0, ATT_BLK), :] = _unrope(_from_qt(dqt_sc[n]), cs, sn).astype(BF16)
            dk_st[slot, pl.ds(r0, ATT_BLK), :] = _unrope(dk_sc[pl.ds(r0 + ATT_BLK, ATT_BLK), :], cs, sn).astype(BF16)
            dv_st[slot, pl.ds(r0, ATT_BLK), :] = dv_sc[pl.ds(r0 + ATT_BLK, ATT_BLK), :].astype(BF16)
            return carry

        lax.fori_loop(0, nb, post, 0, unroll=2)
        _staged_flush(step, n_steps, stages, dsts, o_sems, slot)

    n_in = 1 + len(saved) + 5
    return pl.pallas_call(
        body, name="attn_bwd", grid=(N_PAIRS, bsz),
        out_shape=(jax.ShapeDtypeStruct(dproj3.shape, dproj3.dtype), jax.ShapeDtypeStruct((N_PAIRS, 1, LANES), F32)),
        in_specs=[sp["smem"]] + sp["saved1"] + [sp["g1"], sp["act1"], sp["act1"], sp["tbl"], sp["tbl"], ANY],
        out_specs=(ANY, pl.BlockSpec((None, 1, LANES), lambda p, b: (p, 0, 0))),
        input_output_aliases={n_in: 0},
        scratch_shapes=[pltpu.VMEM((nb, LANES, QT_COLS), BF16),
                        pltpu.VMEM((nb, LANES, QT_COLS), F32),
                        pltpu.VMEM((s + ATT_BLK, LANES), F32),
                        pltpu.VMEM((s + ATT_BLK, LANES), F32),
                        pltpu.VMEM((2, s, QW), BF16), pltpu.VMEM((2, s, LANES), BF16),
                        pltpu.VMEM((2, s, LANES), BF16), pltpu.VMEM((2, s, QW), BF16),
                        pltpu.SemaphoreType.DMA((8,))],
        compiler_params=_params())(sinks, *saved, proj3, o3, dy3, cosf, sinf, dproj3)


def _staged_copies(stages, dsts, sems, slot):
    return [pltpu.make_async_copy(st.at[slot], dst, sems.at[slot * len(stages) + t])
            for t, (st, dst) in enumerate(zip(stages, dsts))]


def _staged_reuse(step, stages, dsts, sems, slot):
    @pl.when(step >= 2)
    def _():
        for cp in _staged_copies(stages, dsts, sems, slot):
            cp.wait()


def _staged_flush(step, n_steps, stages, dsts, sems, slot):
    for cp in _staged_copies(stages, dsts, sems, slot):
        cp.start()

    @pl.when(step == n_steps - 1)
    def _():
        for cp in _staged_copies(stages, dsts, sems, slot):
            cp.wait()
        if n_steps >= 2:
            for cp in _staged_copies(stages, dsts, sems, 1 - slot):
                cp.wait()


def _merge_fwd_bwd(x, tgt, y_rnn, y_attn, proj, w_r, w_a, w_o, gf):
    t, d = x.shape
    tm = min(t, 256)
    nt = t // tm

    hw = d // 2

    def body(x_ref, t_ref, yr_ref, ya_ref, mr0_ref, mr1_ref, ma0_ref, ma1_ref, wr_ref, wa_ref, wo_ref, gf_ref,
             dp_ref, dyr_ref, dya_ref, mg_ref, dx2_ref, dx2b_ref, dpr_ref, dpa_ref, loss_ref, dgf_ref, dmg_st, o_sems):
        i = pl.program_id(0)
        slot = i % 2
        dsts = [dp_ref.at[pl.ds(pl.multiple_of(i * tm, tm), tm), pl.ds(OFF_MERGE_R, 2 * d)]]
        _staged_reuse(i, [dmg_st], dsts, o_sems, slot)
        wr = wr_ref[...]
        wa = wa_ref[...]
        wo = wo_ref[...]
        gfv = gf_ref[...]
        pr = _dot(yr_ref[...], wr)
        pa = _dot(ya_ref[...], wa)
        sr = _sigmoid(jnp.concatenate([mr0_ref[...], mr1_ref[...]], axis=1))
        sa = _sigmoid(jnp.concatenate([ma0_ref[...], ma1_ref[...]], axis=1))
        mb = (sr * pr + sa * pa).astype(BF16)
        mg_ref[...] = mb
        x2 = x_ref[...] + _dot(mb, wo)
        r2 = lax.rsqrt(jnp.mean(x2 * x2, axis=-1, keepdims=True) + NORM_EPS)
        nrm = x2 * r2
        err = nrm * gfv - t_ref[...]
        dy = err * (1.0 / d)
        dn = dy * gfv
        dx2 = r2 * (dn - nrm * jnp.mean(dn * nrm, axis=-1, keepdims=True))
        dx2_ref[...] = dx2
        dx2b = dx2.astype(BF16)
        dx2b_ref[...] = dx2b
        dmerged = _dot_nt(dx2b, wo)
        dpr = (dmerged * sr).astype(BF16)
        dpa = (dmerged * sa).astype(BF16)
        dpr_ref[...] = dpr
        dpa_ref[...] = dpa
        dmg_st[slot, :, 0:d] = (dmerged * pr * (sr * (1.0 - sr))).astype(BF16)
        dmg_st[slot, :, d:2 * d] = (dmerged * pa * (sa * (1.0 - sa))).astype(BF16)
        _staged_flush(i, nt, [dmg_st], dsts, o_sems, slot)
        dyr_ref[...] = _dot_nt(dpr, wr)
        dya_ref[...] = _dot_nt(dpa, wa)

        @pl.when(i == 0)
        def _():
            loss_ref[...] = jnp.zeros_like(loss_ref)
            dgf_ref[...] = jnp.zeros_like(dgf_ref)

        loss_ref[...] += jnp.full((1, LANES), 0.5 / d, F32) * jnp.sum(err * err)
        dgf_ref[...] += jnp.sum(dy * nrm, axis=0, keepdims=True)

    tile = pl.BlockSpec((tm, d), lambda i: (i, 0))
    wsp = pl.BlockSpec((d, d), lambda i: (0, 0))

    def gate(col_blk):
        return pl.BlockSpec((tm, hw), lambda i: (i, col_blk))

    fb = jax.ShapeDtypeStruct((t, d), BF16)
    ff = jax.ShapeDtypeStruct((t, d), F32)
    return pl.pallas_call(
        body, name="merge_fwd_bwd", grid=(nt,),
        out_shape=(jax.ShapeDtypeStruct((t, D_IN), BF16), ff, ff, fb, ff, fb, fb, fb,
                   jax.ShapeDtypeStruct((1, LANES), F32), jax.ShapeDtypeStruct((1, d), F32)),
        in_specs=[tile, tile, tile, tile] + [gate(OFF_MERGE_R // hw + j) for j in range(4)] + [
            wsp, wsp, wsp, pl.BlockSpec((1, d), lambda i: (0, 0))],
        out_specs=(ANY, tile, tile, tile, tile, tile, tile, tile,
                   pl.BlockSpec((1, LANES), lambda i: (0, 0)), pl.BlockSpec((1, d), lambda i: (0, 0))),
        scratch_shapes=[pltpu.VMEM((2, tm, 2 * d), BF16), pltpu.SemaphoreType.DMA((2,))],
        compiler_params=_params())(x, tgt, y_rnn, y_attn, proj, proj, proj, proj, w_r, w_a, w_o, gf)


def _local_grads(x, tgt, h, proj, norm_g, w_in_bm, conv_w, conv_b, lru_w_a, lru_b_a, lru_w_x, lru_b_x, lam, sinks,
                 row_sharded, gf):
    bsz, s, d = x.shape
    t = bsz * s
    x2 = x.reshape(t, d)
    proj3 = proj.reshape(bsz, s, D_IN)
    h_lru, y_rnn, gathered = _lru_fwd(proj3, conv_w, conv_b, lru_w_a, lru_b_a, lru_w_x, lru_b_x, lam, row_sharded)
    w_r, w_a, w_o = (g.reshape(d, d) for g in gathered)
    cosf, sinf = _rope_tables(s)
    o_attn, y_attn, attn_saved = _attn_fwd(proj3, sinks, cosf, sinf)
    y_rnn2 = y_rnn.reshape(t, d)
    y_attn2 = y_attn.reshape(t, d)
    dproj, dyr, dya, merged, dx2, dx2b, dpr, dpa, loss, dgf = _merge_fwd_bwd(
        x2, tgt.reshape(t, d), y_rnn2, y_attn2, proj, w_r, w_a, w_o, gf)
    dproj3, dsink = _attn_bwd(proj3, attn_saved, o_attn, dya.reshape(bsz, s, d), dproj.reshape(bsz, s, D_IN),
                              sinks, cosf, sinf)
    dproj3, dcw, dcb, dwa, dba, dwx, dbx, dlam = _lru_bwd(
        proj3, h_lru, dyr.reshape(bsz, s, d), dproj3, conv_w, conv_b, lru_w_a, lru_b_a, lru_w_x, lru_b_x, lam)
    dproj = dproj3.reshape(t, D_IN)
    grad_x, dng = _grad_x(dproj, w_in_bm, x2, dx2, norm_g)
    small = dict(norm_g=dng, conv_w=dcw, conv_b=dcb, lru_w_a=dwa, lru_b_a=dba, lru_w_x=dwx, lru_b_x=dbx,
                 lru_lambda=dlam, attn_sinks=dsink[:, 0, :HEADS_PER_STEP].reshape(1, N_Q_HEADS), final_norm_g=dgf)
    squares = [(y_rnn2, dpr), (y_attn2, dpa), (merged, dx2b)]
    return loss[0, 0], grad_x.reshape(bsz, s, d), h, dproj, squares, small


ANY = pl.BlockSpec(memory_space=pl.ANY)


def _mesh_pos():
    return lax.axis_index("x"), lax.axis_index("y"), lax.axis_index("c")


def _remote(src, dst, send_sems, recv_sems, idx, peer):
    return pltpu.make_async_remote_copy(src_ref=src, dst_ref=dst, send_sem=send_sems.at[idx],
                                        recv_sem=recv_sems.at[idx], device_id=peer, device_id_type=MESH)


def _row_gather(ins, outs, send_sems, recv_sems, fsend_sems, frecv_sems):
    n = len(ins)
    x, y, c = _mesh_pos()
    me = 2 * x + y
    sib = (x, y, 1 - c)
    peers = [((x, 1 - y, c), me ^ 1), ((1 - x, y, c), me ^ 2), ((1 - x, 1 - y, c), me ^ 3)]

    def half(ref, slot, t, which):
        hr = ins[t].shape[1] // 2
        return ref.at[slot, pl.ds(pl.multiple_of(which * hr, 8), hr), :]

    def ici(t, k):
        peer, pj = peers[k]
        src = half(ins[t], me, t, c)
        return (_remote(src, half(outs[t], me, t, c), send_sems, recv_sems, 3 * t + k, peer),
                _remote(src, half(outs[t], pj, t, c), send_sems, recv_sems, 3 * t + k, peer))

    def forward(t, k):
        got = half(outs[t], peers[k][1], t, c)
        return (_remote(got, got, fsend_sems, frecv_sems, 3 * t + k, sib),
                _remote(got, half(outs[t], peers[k][1], t, 1 - c), fsend_sems, frecv_sems, 3 * t + k, sib))

    pairs = [(t, k) for t in range(n) for k in range(3)]

    def start():
        for t, k in pairs:
            ici(t, k)[0].start()

    def pass_on():
        for t, k in pairs:
            ici(t, k)[1].wait_recv()
            forward(t, k)[0].start()

    def finish():
        for t, k in pairs:
            ici(t, k)[0].wait_send()
            forward(t, k)[0].wait_send()
            forward(t, k)[1].wait_recv()

    return start, pass_on, finish


def _gather_in_proj(x, g, bufs, split, idx):
    t_tok, d = x.shape
    n = len(bufs)
    tm = min(t_tok, 1024)
    nt = t_tok // tm
    n_fwd = 3 * sum(split)
    assert split[0]

    def body(idx_ref, x_ref, g_ref, *refs):
        ins, proj_ref, h_out, outs = refs[:n], refs[n], refs[n + 1], refs[n + 2:2 * n + 2]
        wbuf, h_all, send_sems, recv_sems, fsend_sems, frecv_sems, l_sems = refs[2 * n + 2:]
        j, i = pl.program_id(0), pl.program_id(1)
        rows = pl.ds(pl.multiple_of(i * tm, tm), tm)
        x, y, c = _mesh_pos()
        me = 2 * x + y
        sib = (x, y, 1 - c)
        peers = [((x, 1 - y, c), me ^ 1), ((1 - x, y, c), me ^ 2), ((1 - x, 1 - y, c), me ^ 3)]

        def part(ref, slot, t, half):
            if not split[t]:
                return ref.at[slot]
            hr = bufs[t].shape[1] // 2
            return ref.at[slot, pl.ds(pl.multiple_of(half * hr, 8), hr), :]

        def land(t):
            return wbuf if t == 0 else outs[t]

        def ici(t, k):
            peer, pj = peers[k]
            src = part(ins[t], me, t, c)
            return (_remote(src, part(land(t), me, t, c), send_sems, recv_sems, 3 * t + k, peer),
                    _remote(src, part(land(t), pj, t, c), send_sems, recv_sems, 3 * t + k, peer))

        fwd_index = {}
        for t in range(n):
            if split[t]:
                for k in range(3):
                    fwd_index[(t, k)] = len(fwd_index)

        def forward(t, k):
            pj = peers[k][1]
            got = part(land(t), pj, t, c)
            f = fwd_index[(t, k)]
            return (_remote(got, got, fsend_sems, frecv_sems, f, sib),
                    _remote(got, part(land(t), pj, t, 1 - c), fsend_sems, frecv_sems, f, sib))

        def write_back(k):
            pj = peers[k][1]
            return pltpu.make_async_copy(wbuf.at[pj], outs[0].at[pj], l_sems.at[1 + k])

        relay_peer = ((x + c) % 2, (y + 1 - c) % 2, c)

        def relay():
            got = part(wbuf, me ^ (2 - c), 0, c)
            return (_remote(got, got, send_sems, recv_sems, 2, relay_peer),
                    _remote(got, part(wbuf, me ^ 3, 0, c), send_sems, recv_sems, 2, relay_peer))

        direct = [(t, k) for t in range(n) for k in range(3) if (t, k) != (0, 2)]

        @pl.when((j == 0) & (i == 0))
        def _():
            for t, k in direct:
                ici(t, k)[0].start()
            own = pltpu.make_async_copy(ins[0].at[me], wbuf.at[me], l_sems.at[0])
            own.start()
            own.wait()

        @pl.when((j == 1) & (i == 0))
        def _():
            pltpu.make_async_copy(h_all, h_out, l_sems.at[4]).start()
            for k in range(2):
                ici(0, k)[1].wait_recv()
            relay()[0].start()
            for k in range(2):
                forward(0, k)[0].start()
            forward(0, 0)[1].wait_recv()
            write_back(0).start()

        @pl.when((j == 2) & (i == 0))
        def _():
            forward(0, 1)[1].wait_recv()
            write_back(1).start()

        @pl.when((j == 3) & (i == 0))
        def _():
            relay()[1].wait_recv()
            forward(0, 2)[0].start()
            forward(0, 2)[1].wait_recv()
            write_back(2).start()

        @pl.when(j == 0)
        def _():
            xv = x_ref[...]
            r = lax.rsqrt(jnp.mean(xv * xv, axis=-1, keepdims=True) + NORM_EPS)
            h_all[rows, :] = (xv * r * g_ref[...]).astype(BF16)

        proj_ref[...] = _dot(h_all[rows, :], wbuf[me ^ j])

        @pl.when((j == N_CHIPS - 1) & (i == nt - 1))
        def _():
            pltpu.make_async_copy(h_all, h_out, l_sems.at[4]).wait()
            for t in range(1, n):
                for k in range(3):
                    ici(t, k)[1].wait_recv()
                    if split[t]:
                        forward(t, k)[0].start()
            relay()[0].wait_send()
            for t, k in direct:
                ici(t, k)[0].wait_send()
            for t in range(n):
                if split[t]:
                    for k in range(3):
                        forward(t, k)[0].wait_send()
                        if t > 0:
                            forward(t, k)[1].wait_recv()
            for k in range(3):
                write_back(k).wait()

    grid_spec = pltpu.PrefetchScalarGridSpec(
        num_scalar_prefetch=1, grid=(N_CHIPS, nt),
        in_specs=[pl.BlockSpec((tm, d), lambda j, i, idx_ref: (jnp.where(j == 0, i, nt - 1), 0)),
                  pl.BlockSpec((1, d), lambda j, i, idx_ref: (0, 0))] + [ANY] * n,
        out_specs=[pl.BlockSpec((tm, W_BLK), lambda j, i, idx_ref: (i, idx_ref[0] ^ j)), ANY] + [ANY] * n,
        scratch_shapes=[pltpu.VMEM(bufs[0].shape, bufs[0].dtype), pltpu.VMEM((t_tok, d), BF16),
                        pltpu.SemaphoreType.DMA((3 * n,)), pltpu.SemaphoreType.DMA((3 * n,)),
                        pltpu.SemaphoreType.DMA((n_fwd,)), pltpu.SemaphoreType.DMA((n_fwd,)),
                        pltpu.SemaphoreType.DMA((5,))])
    out_shape = [jax.ShapeDtypeStruct((t_tok, D_IN), F32), jax.ShapeDtypeStruct((t_tok, d), BF16)] + [
        jax.ShapeDtypeStruct(a.shape, a.dtype) for a in bufs]
    res = pl.pallas_call(
        body, name="gather_in_proj", grid_spec=grid_spec, out_shape=out_shape,
        input_output_aliases={3 + t: 2 + t for t in range(n)}, compiler_params=_params())(idx, x, g, *bufs)
    return res[1], res[0], res[2:]


def _row_tile(rows, row_bytes, cap_bytes=2 * 1024 * 1024):
    best = None
    for tr in range(8, rows + 1, 8):
        if rows % tr == 0 and tr * row_bytes <= cap_bytes:
            best = tr
    return best if best is not None else rows


XOR_ORDER = (3, 2, 1)


def _grads_reduce_scatter(h, dproj, squares, small, idx):
    t, d = h.shape
    nsq = len(squares)
    hr = d // 2
    qr = ROW_BLK // 2
    tk = min(t, 1024)
    nk = t // tk
    last = N_CHIPS - 1
    n_phase = 3

    def dest(s, idx_ref):
        xo = jnp.where(s == 0, XOR_ORDER[0], jnp.where(s == 1, XOR_ORDER[1], jnp.where(s == 2, XOR_ORDER[2], 0)))
        return idx_ref[0] ^ xo

    def k_sq(p, k):
        return jnp.where(p == 0, k, nk - 1)

    def k_w(p, k):
        return jnp.where(p == 0, 0, k)

    in_specs = [
        pl.BlockSpec((tk, hr), lambda s, p, k, idx_ref: (k_w(p, k), (1 - idx_ref[1] + jnp.maximum(p - 1, 0)) % 2)),
        pl.BlockSpec((tk, W_BLK), lambda s, p, k, idx_ref: (k_w(p, k), dest(s, idx_ref)))]
    for q in range(nsq):
        in_specs.append(pl.BlockSpec((tk, ROW_BLK), lambda s, p, k, idx_ref: (k_sq(p, k), dest(s, idx_ref))))
        in_specs.append(pl.BlockSpec((tk, d), lambda s, p, k, idx_ref: (k_sq(p, k), 0)))

    def body(idx_ref, *refs):
        nj = 1 + nsq
        h_ref, dp_ref = refs[0], refs[1]
        sq_in = refs[2:2 + 2 * nsq]
        small_in = refs[2 * nj]
        outs = refs[2 * nj + 1:3 * nj + 2]
        landing = refs[3 * nj + 2:4 * nj + 3]
        sc = refs[4 * nj + 3:]
        acc_w, xr_w, sb_w = sc[0:3]
        sq_sc = [sc[3 + 3 * q:6 + 3 * q] for q in range(nsq)]
        sm, smx = sc[3 * nj:3 * nj + 2]
        x_send, x_recv, i_send, i_recv, f_send, f_recv, o_sem, l_sem = sc[3 * nj + 2:]
        s, p, k = pl.program_id(0), pl.program_id(1), pl.program_id(2)
        x, y, c = _mesh_pos()
        sib = (x, y, 1 - c)
        peers = [((1 - x) if xo & 2 else x, (1 - y) if xo & 1 else y, c) for xo in XOR_ORDER]
        slot = s % 2
        mine_w = pl.ds(pl.multiple_of(c * hr, 8), hr)
        theirs_w = pl.ds(pl.multiple_of((1 - c) * hr, 8), hr)
        mine_q = pl.ds(pl.multiple_of(c * qr, 8), qr)
        theirs_q = pl.ds(pl.multiple_of((1 - c) * qr, 8), qr)

        def exch(j, src, dst):
            return _remote(src, dst, x_send, x_recv, 2 * j + slot, sib)

        sbufs = [sb_w] + [sq_sc[q][2] for q in range(nsq)]

        def ici(j, ss):
            return _remote(sbufs[j].at[ss], landing[j].at[ss], i_send, i_recv, last * j + ss, peers[ss])

        def exchanges():
            cps = [exch(0, acc_w.at[0], xr_w.at[slot])]
            cps += [exch(1 + q, sq_sc[q][0].at[theirs_q, :], sq_sc[q][1].at[slot]) for q in range(nsq)]
            return cps

        def small_send(ss):
            return _remote(sm.at[c], landing[nj].at[ss], i_send, i_recv, last * nj + ss, peers[ss])

        def small_start():
            load = pltpu.make_async_copy(small_in, sm, l_sem.at[nj + 1])
            load.start()
            load.wait()
            swap = _remote(sm, smx.at[pl.ds(0, 2)], x_send, x_recv, 2 * nj, sib)
            swap.start()
            swap.wait_recv()
            swap.wait_send()
            sm[...] = sm[...] + smx[0:2]
            for ss in range(last):
                small_send(ss).start()

        def pair_ref(j):
            return acc_w.at[1] if j == 0 else sq_sc[j - 1][0].at[mine_q, :]

        def sq_phase():
            pl.when((s == 0) & (k == 0))(small_start)
            for q in range(nsq):
                acc = sq_sc[q][0]

                @pl.when(k == 0)
                def _():
                    acc[...] = jnp.zeros((ROW_BLK, d), F32)

                acc[...] += _dot_tn(sq_in[2 * q][...], sq_in[2 * q + 1][...])

            @pl.when(k == nk - 1)
            def _():
                for cp in exchanges()[1:]:
                    cp.start()

        def w_phase(hf):
            @pl.when(k == 0)
            def _():
                acc_w[hf] = jnp.zeros((hr, W_BLK), F32)

            acc_w[hf] += _dot_tn(h_ref[...], dp_ref[...])

            @pl.when(k == nk - 1)
            def _():
                if hf == 0:
                    exchanges()[0].start()
                else:
                    finish_step()

        def finish_step():
            for cp in exchanges():
                cp.wait_recv()
                cp.wait_send()
            acc_w[1] += xr_w[slot]
            for q in range(nsq):
                sq_sc[q][0][mine_q, :] += sq_sc[q][1][slot]
            for ss in range(last):
                @pl.when(s == ss)
                def _():
                    for j in range(nj):
                        sbufs[j][ss] = pair_ref(j)[...].astype(BF16)
                        ici(j, ss).start()

            @pl.when(s == last)
            def _():
                for ss in range(last):
                    for j in range(nj):
                        ici(j, ss).wait_recv()
                        ici(j, ss).wait_send()
                    small_send(ss).wait_recv()
                    small_send(ss).wait_send()
                stage = [pltpu.make_async_copy(landing[j], sbufs[j], l_sem.at[j]) for j in range(nj)]
                stage.append(pltpu.make_async_copy(landing[nj], smx, l_sem.at[nj]))
                for cp in stage:
                    cp.start()
                for j in range(nj):
                    stage[j].wait()
                    total = pair_ref(j)[...]
                    for ss in range(last):
                        total = total + sbufs[j][ss].astype(F32)
                    pair_ref(j)[...] = total
                stage[nj].wait()
                by_xor = {xo: smx[ss] for ss, xo in enumerate(XOR_ORDER)}
                sm[c] = (sm[c] + by_xor[1]) + (by_xor[2] + by_xor[3])
                done = [(acc_w.at[1], outs[0].at[mine_w, :], outs[0].at[theirs_w, :])]
                done += [(pair_ref(1 + q), outs[1 + q].at[mine_q, :], outs[1 + q].at[theirs_q, :]) for q in range(nsq)]
                done.append((sm.at[c], outs[nj].at[c], outs[nj].at[1 - c]))
                copies = []
                for j, (src, mine, theirs) in enumerate(done):
                    keep = pltpu.make_async_copy(src, mine, o_sem.at[j])
                    give = _remote(src, mine, f_send, f_recv, j, sib)
                    take = _remote(src, theirs, f_send, f_recv, j, sib)
                    keep.start()
                    give.start()
                    copies.append((keep, give, take))
                for keep, give, take in copies:
                    keep.wait()
                    give.wait_send()
                    take.wait_recv()

        pl.when(p == 0)(sq_phase)
        for hf in range(2):
            pl.when(p == 1 + hf)(functools.partial(w_phase, hf))

    nj = 1 + nsq
    scratch = [pltpu.VMEM((2, hr, W_BLK), F32), pltpu.VMEM((2, hr, W_BLK), F32), pltpu.VMEM((last, hr, W_BLK), BF16)]
    for _ in range(nsq):
        scratch += [pltpu.VMEM((ROW_BLK, d), F32), pltpu.VMEM((2, qr, d), F32), pltpu.VMEM((last, qr, d), BF16)]
    scratch += [pltpu.VMEM((2, PK_HALF, LANES), F32), pltpu.VMEM((last, PK_HALF, LANES), F32)]
    scratch += [pltpu.SemaphoreType.DMA((2 * nj + 1,)), pltpu.SemaphoreType.DMA((2 * nj + 1,)),
                pltpu.SemaphoreType.DMA((last * (nj + 1),)), pltpu.SemaphoreType.DMA((last * (nj + 1),)),
                pltpu.SemaphoreType.DMA((nj + 1,)), pltpu.SemaphoreType.DMA((nj + 1,)),
                pltpu.SemaphoreType.DMA((nj + 1,)), pltpu.SemaphoreType.DMA((nj + 2,))]
    grid_spec = pltpu.PrefetchScalarGridSpec(
        num_scalar_prefetch=1, grid=(N_CHIPS, n_phase, nk), in_specs=in_specs + [ANY],
        out_specs=[ANY] * (2 * nj + 2), scratch_shapes=scratch)
    out_shape = [jax.ShapeDtypeStruct((d, W_BLK), F32)] + [jax.ShapeDtypeStruct((ROW_BLK, d), F32)] * nsq
    out_shape.append(jax.ShapeDtypeStruct((2, PK_HALF, LANES), F32))
    out_shape += [jax.ShapeDtypeStruct((last, hr, W_BLK), BF16)] + [jax.ShapeDtypeStruct((last, qr, d), BF16)] * nsq
    out_shape.append(jax.ShapeDtypeStruct((last, PK_HALF, LANES), F32))
    flat = [a for pair in squares for a in pair]
    res = pl.pallas_call(body, name="grads_reduce_scatter", grid_spec=grid_spec, out_shape=out_shape,
                         compiler_params=_params())(idx, h, dproj, *flat, small)
    return res[:nj + 1]


_VEC_NAMES = ("norm_g", "conv_b", "lru_b_a", "lru_b_x", "lru_lambda", "final_norm_g")


def _pack_small(p, conv_full=None, scalar=None):
    rows = [p["lru_w_a"].reshape(PK_WX - PK_WA, LANES), p["lru_w_x"].reshape(PK_VEC - PK_WX, LANES)]
    rows += [p[k].reshape(8, LANES) for k in _VEC_NAMES]
    rows.append(jnp.pad(p["attn_sinks"].reshape(1, N_Q_HEADS), ((0, 7), (0, LANES - N_Q_HEADS))))
    rows.append(jnp.zeros((32, LANES), F32) if conv_full is None else conv_full.reshape(32, LANES))
    tail = PK_ROWS - PK_SCALAR
    if scalar is None:
        rows.append(jnp.zeros((tail, LANES), F32))
    else:
        rows.append(jnp.pad(scalar.reshape(1, 1), ((0, tail - 1), (0, LANES - 1))))
    return jnp.concatenate(rows, axis=0)


def _unpack_small(pk, like):
    out = {"lru_w_a": pk[PK_WA:PK_WX].reshape(like["lru_w_a"].shape),
           "lru_w_x": pk[PK_WX:PK_VEC].reshape(like["lru_w_x"].shape)}
    for j, k in enumerate(_VEC_NAMES):
        out[k] = pk[PK_VEC + 8 * j:PK_VEC + 8 * j + 8].reshape(like[k].shape)
    out["attn_sinks"] = pk[PK_SINK:PK_SINK + 1, :N_Q_HEADS].reshape(like["attn_sinks"].shape)
    return out


_WEIGHTS = ("norm_g", "w_in", "conv_w", "conv_b", "lru_w_a", "lru_b_a", "lru_w_x", "lru_b_x", "lru_lambda",
            "attn_sinks", "w_rnn_out", "w_attn_out", "w_o", "final_norm_g")
_SMALL = ("norm_g", "conv_b", "lru_w_a", "lru_b_a", "lru_w_x", "lru_b_x", "lru_lambda", "attn_sinks", "final_norm_g")
_ROW_SHARDED = ("w_rnn_out", "w_attn_out", "w_o")


def kernel(x, norm_g, w_in, conv_w, conv_b, lru_w_a, lru_b_a, lru_w_x, lru_b_x, lru_lambda, attn_sinks, w_rnn_out, w_attn_out, w_o, final_norm_g, loss_target, m_norm_g, m_w_in, m_conv_w, m_conv_b, m_lru_w_a, m_lru_b_a, m_lru_w_x, m_lru_b_x, m_lru_lambda, m_attn_sinks, m_w_rnn_out, m_w_attn_out, m_w_o, m_final_norm_g, v_norm_g, v_w_in, v_conv_w, v_conv_b, v_lru_w_a, v_lru_b_a, v_lru_w_x, v_lru_b_x, v_lru_lambda, v_attn_sinks, v_w_rnn_out, v_w_attn_out, v_w_o, v_final_norm_g):
    w = dict(norm_g=norm_g, w_in=w_in, conv_w=conv_w, conv_b=conv_b, lru_w_a=lru_w_a, lru_b_a=lru_b_a, lru_w_x=lru_w_x,
             lru_b_x=lru_b_x, lru_lambda=lru_lambda, attn_sinks=attn_sinks, w_rnn_out=w_rnn_out, w_attn_out=w_attn_out,
             w_o=w_o, final_norm_g=final_norm_g)
    m = dict(norm_g=m_norm_g, w_in=m_w_in, conv_w=m_conv_w, conv_b=m_conv_b, lru_w_a=m_lru_w_a, lru_b_a=m_lru_b_a,
             lru_w_x=m_lru_w_x, lru_b_x=m_lru_b_x, lru_lambda=m_lru_lambda, attn_sinks=m_attn_sinks,
             w_rnn_out=m_w_rnn_out, w_attn_out=m_w_attn_out, w_o=m_w_o, final_norm_g=m_final_norm_g)
    v = dict(norm_g=v_norm_g, w_in=v_w_in, conv_w=v_conv_w, conv_b=v_conv_b, lru_w_a=v_lru_w_a, lru_b_a=v_lru_b_a,
             lru_w_x=v_lru_w_x, lru_b_x=v_lru_b_x, lru_lambda=v_lru_lambda, attn_sinks=v_attn_sinks,
             w_rnn_out=v_w_rnn_out, w_attn_out=v_w_attn_out, w_o=v_w_o, final_norm_g=v_final_norm_g)
    mx, my, mc = _mesh_pos()
    me = 2 * mx + my
    d = D_MODEL

    my_chip = jnp.reshape(me, (1,)).astype(jnp.int32)
    (buf_in,) = _put_slots([w["w_in"][0]], my_chip, BF16, "cast_w_in")
    (buf_cw,) = _put_slots([w["conv_w"][0]], my_chip, F32, "slot_conv_w")
    row_sharded = _put_slots([w[k][0] for k in _ROW_SHARDED], my_chip, BF16, "cast_row_sharded")
    h, proj, (g_in, g_cw) = _gather_in_proj(x.reshape(-1, d), w["norm_g"], [buf_in, buf_cw], [True, False], my_chip)
    conv_full = g_cw.transpose(1, 0, 2).reshape(CONV_WIDTH, D_RNN)

    loss_local, grad_x, h, dproj, squares, gsmall = _local_grads(
        x, loss_target, h, proj, w["norm_g"], g_in, conv_full, w["conv_b"], w["lru_w_a"][0], w["lru_b_a"], w["lru_w_x"][0],
        w["lru_b_x"], w["lru_lambda"], w["attn_sinks"][0], row_sharded, w["final_norm_g"].reshape(1, d))
    gpack = _pack_small(gsmall, gsmall["conv_w"], loss_local).reshape(2, PK_HALF, LANES)
    f_in, f_r, f_a, f_o, spack = _grads_reduce_scatter(h, dproj, squares, gpack, jnp.stack([me, mc]).astype(jnp.int32))
    spack = spack.reshape(PK_ROWS, LANES)
    loss = spack[PK_SCALAR, 0]

    grads = _unpack_small(spack, w)
    conv_all = spack[PK_CONV:PK_CONV + 32].reshape(CONV_WIDTH, D_RNN)
    grads["conv_w"] = lax.dynamic_slice_in_dim(conv_all, me * (D_RNN // N_CHIPS), D_RNN // N_CHIPS, axis=1)[None]
    grads["w_in"] = f_in[None]
    grads["w_rnn_out"], grads["w_attn_out"], grads["w_o"] = f_r[None], f_a[None], f_o[None]

    delta, new_m, new_v = {}, {}, {}
    def group(k):
        return w[k][0], grads[k][0], m[k][0], v[k][0]

    for names, call in ((("w_in",), "adamw_w_in"), (_ROW_SHARDED, "adamw_row_sharded")):
        for k, (dk, mk, vk, gk) in zip(names, _adamw([group(k) for k in names], call, echo_grad=True)):
            delta[k], new_m[k], new_v[k], grads[k] = dk[None], mk[None], vk[None], gk[None]
    shp = (2 * CONV_WIDTH, LANES)
    ((dk, mk, vk),) = _adamw([tuple(a.reshape(shp) for a in (w["conv_w"], grads["conv_w"], m["conv_w"], v["conv_w"]))],
                             "adamw_conv_w")
    delta["conv_w"], new_m["conv_w"], new_v["conv_w"] = (a.reshape(w["conv_w"].shape) for a in (dk, mk, vk))
    ((dk, mk, vk),) = _adamw([(_pack_small(w), spack, _pack_small(m), _pack_small(v))], "adamw_small")
    for src, dst in ((dk, delta), (mk, new_m), (vk, new_v)):
        dst.update(_unpack_small(src, w))

    return (loss, grad_x, *[grads[k] for k in _WEIGHTS], *[delta[k] for k in _WEIGHTS],
            *[new_m[k] for k in _WEIGHTS], *[new_v[k] for k in _WEIGHTS])
```

```python
import functools
import math

import jax
import jax.numpy as jnp
from jax import lax
from jax.experimental import pallas as pl
from jax.experimental.pallas import tpu as pltpu

F32 = jnp.float32
BF16 = jnp.bfloat16
MESH = pl.DeviceIdType.MESH

D_MODEL = 1024
D_RNN = 1024
N_RNN_BLOCKS = 8
RNN_BLOCK = D_RNN // N_RNN_BLOCKS
CONV_WIDTH = 4
LRU_C = 8.0
HEAD_DIM = 64
N_Q_HEADS = 16
N_KV_HEADS = 4
D_ATTN = N_Q_HEADS * HEAD_DIM
D_KV = N_KV_HEADS * HEAD_DIM
WINDOW = 128
ROPE_DIM = HEAD_DIM // 4
ROPE_THETA = 500000.0
NORM_EPS = 1e-6
OFF_RNN_X = 0
OFF_RNN_G = OFF_RNN_X + D_RNN
OFF_Q = OFF_RNN_G + D_RNN
OFF_K = OFF_Q + D_ATTN
OFF_V = OFF_K + D_KV
OFF_ATTN_G = OFF_V + D_KV
OFF_MERGE_R = OFF_ATTN_G + D_ATTN
OFF_MERGE_A = OFF_MERGE_R + D_MODEL
D_IN = OFF_MERGE_A + D_MODEL

ADAM_LR = 0.001
ADAM_B1 = 0.9
ADAM_B2 = 0.999
ADAM_EPS = 1e-08
ADAM_WD = 0.01
ADAM_STEP = 10

N_CHIPS = 4
W_BLK = D_IN // N_CHIPS
ROW_BLK = D_MODEL // N_CHIPS
LANES = 128
ATT_BLK = 128
VMEM_LIMIT = 56 * 1024 * 1024
NEG_BIG = -1e30
ATTN_SCALE = 1.0 / math.sqrt(HEAD_DIM)

PK_WA = 0
PK_WX = PK_WA + N_RNN_BLOCKS * RNN_BLOCK
PK_VEC = PK_WX + N_RNN_BLOCKS * RNN_BLOCK
PK_SINK = PK_VEC + 6 * 8
PK_CONV = PK_SINK + 8
PK_SCALAR = PK_CONV + 32
PK_ROWS = PK_SCALAR + 8
PK_HALF = PK_ROWS // 2


def _params(**kw):
    return pltpu.CompilerParams(vmem_limit_bytes=VMEM_LIMIT, **kw)


def _sigmoid(z):
    return 1.0 / (1.0 + jnp.exp(-z))


def _dot(a, b):
    return jnp.dot(a, b, preferred_element_type=F32)


def _dot_nt(a, b):
    return lax.dot_general(a, b, (((1,), (1,)), ((), ())), preferred_element_type=F32)


def _dot_tn(a, b):
    return lax.dot_general(a, b, (((0,), (0,)), ((), ())), preferred_element_type=F32)


def _put_slots(srcs, slot, dtype, name):
    rows, c = srcs[0].shape
    n = len(srcs)
    tr = _row_tile(rows, c * 4)

    def body(idx_ref, *refs):
        for s_ref, o_ref in zip(refs[:n], refs[n:]):
            o_ref[...] = s_ref[...].astype(dtype)

    grid_spec = pltpu.PrefetchScalarGridSpec(
        num_scalar_prefetch=1, grid=(rows // tr,),
        in_specs=[pl.BlockSpec((tr, c), lambda i, idx_ref: (i, 0))] * n,
        out_specs=[pl.BlockSpec((None, tr, c), lambda i, idx_ref: (idx_ref[0], i, 0))] * n)
    return pl.pallas_call(body, name=name, grid_spec=grid_spec,
                          out_shape=[jax.ShapeDtypeStruct((N_CHIPS, rows, c), dtype)] * n,
                          compiler_params=_params())(slot, *srcs)


def _adamw(groups, name, echo_grad=False):
    r, c = groups[0][0].shape
    n = len(groups)
    n_out = 4 if echo_grad else 3
    tr = _row_tile(r, c * 4, 1024 * 1024 // n)
    c1 = 1.0 - ADAM_B1 ** ADAM_STEP
    c2 = 1.0 - ADAM_B2 ** ADAM_STEP

    def body(*refs):
        for q in range(n):
            w_ref, g_ref, m_ref, v_ref = refs[4 * q:4 * q + 4]
            d_ref, nm_ref, nv_ref = refs[4 * n + n_out * q:4 * n + n_out * q + 3]
            gv = g_ref[...]
            if echo_grad:
                refs[4 * n + n_out * q + 3][...] = gv
            nm = ADAM_B1 * m_ref[...] + (1.0 - ADAM_B1) * gv
            nv = ADAM_B2 * v_ref[...] + (1.0 - ADAM_B2) * (gv * gv)
            m_hat = nm / c1
            v_hat = nv / c2
            d_ref[...] = -ADAM_LR * (m_hat / (jnp.sqrt(v_hat) + ADAM_EPS) + ADAM_WD * w_ref[...])
            nm_ref[...] = nm
            nv_ref[...] = nv

    spec = pl.BlockSpec((tr, c), lambda i: (i, 0))
    sds = jax.ShapeDtypeStruct((r, c), F32)
    res = pl.pallas_call(
        body, name=name, grid=(r // tr,), out_shape=[sds] * (n_out * n), in_specs=[spec] * (4 * n),
        out_specs=[spec] * (n_out * n), compiler_params=_params())(*[a for grp in groups for a in grp])
    return [tuple(res[n_out * q:n_out * q + n_out]) for q in range(n)]


def _grad_x(dproj, w_bm, x, dx2, g):
    t = dproj.shape[0]
    nb, d, wb = w_bm.shape
    tm = min(t, 512)
    chunk = min(tm, 256)

    def body(dp_ref, w_ref, x_ref, dx2_ref, g_ref, gx_ref, dg_ref, acc_ref):
        i, k = pl.program_id(0), pl.program_id(1)

        @pl.when(k == 0)
        def _():
            acc_ref[...] = jnp.zeros_like(acc_ref)

        @pl.when(k < nb - 1)
        def _():
            acc_ref[...] += _dot_nt(dp_ref[...], w_ref[...])

        @pl.when((i == 0) & (k == 0))
        def _():
            dg_ref[...] = jnp.zeros_like(dg_ref)

        @pl.when(k == nb - 1)
        def _():
            gv = g_ref[...]
            wv = w_ref[...]
            dg = jnp.zeros((1, d), F32)
            for r0 in range(0, tm, chunk):
                rows = slice(r0, r0 + chunk)
                dhv = acc_ref[rows, :] + _dot_nt(dp_ref[rows, :], wv)
                xv = x_ref[rows, :]
                r = lax.rsqrt(jnp.mean(xv * xv, axis=-1, keepdims=True) + NORM_EPS)
                nrm = xv * r
                dn = dhv * gv
                gx_ref[rows, :] = dx2_ref[rows, :] + r * (dn - nrm * jnp.mean(dn * nrm, axis=-1, keepdims=True))
                dg = dg + jnp.sum(dhv * nrm, axis=0, keepdims=True)
            dg_ref[...] += dg

    tile = pl.BlockSpec((tm, d), lambda i, k: (i, 0))
    vec = pl.BlockSpec((1, d), lambda i, k: (0, 0))
    return pl.pallas_call(
        body, name="grad_x", grid=(t // tm, nb),
        out_shape=(jax.ShapeDtypeStruct((t, d), F32), jax.ShapeDtypeStruct((1, d), F32)),
        in_specs=[pl.BlockSpec((tm, wb), lambda i, k: (i, k)), pl.BlockSpec((None, d, wb), lambda i, k: (k, 0, 0)),
                  tile, tile, vec],
        out_specs=(tile, vec), scratch_shapes=[pltpu.VMEM((tm, d), F32)], compiler_params=_params())(dproj, w_bm, x, dx2, g)


def _shift_down(v, d, fill):
    n = v.shape[0]
    if d % 8 == 0:
        return jnp.concatenate([jnp.full((d,) + v.shape[1:], fill, v.dtype), v[: n - d]], axis=0)
    row = lax.broadcasted_iota(jnp.int32, v.shape, 0)
    return jnp.where(row >= d, pltpu.roll(v, d, axis=0), fill)


def _shift_up(v, d, fill):
    n = v.shape[0]
    if d % 8 == 0:
        return jnp.concatenate([v[d:], jnp.full((d,) + v.shape[1:], fill, v.dtype)], axis=0)
    row = lax.broadcasted_iota(jnp.int32, v.shape, 0)
    return jnp.where(row < n - d, pltpu.roll(v, n - d, axis=0), fill)


def _scan_log(a, b, shift):
    n = a.shape[0]
    d = 1
    while d < n:
        b = a * shift(b, d, 0.0) + b
        if 2 * d < n:
            a = a * shift(a, d, 1.0)
        d *= 2
    return b


SUBLANES = 8


def _scan(a, b, sa_ref, sb_ref, reverse):
    n, c = a.shape
    g = n // SUBLANES
    a3, b3 = a.reshape(g, SUBLANES, c), b.reshape(g, SUBLANES, c)
    sub = lax.broadcasted_iota(jnp.int32, a3.shape, 1)
    d = 1
    while d < SUBLANES:
        keep = (sub < SUBLANES - d) if reverse else (sub >= d)
        amount = SUBLANES - d if reverse else d
        b3 = a3 * jnp.where(keep, pltpu.roll(b3, amount, axis=1), 0.0) + b3
        a3 = a3 * jnp.where(keep, pltpu.roll(a3, amount, axis=1), 1.0)
        d *= 2
    sa_ref[...] = a3.reshape(n, c)
    sb_ref[...] = b3.reshape(n, c)
    edge = 0 if reverse else SUBLANES - 1
    shift = _shift_up if reverse else _shift_down
    totals = _scan_log(sa_ref[pl.ds(edge, g, stride=SUBLANES), :], sb_ref[pl.ds(edge, g, stride=SUBLANES), :], shift)
    carry = shift(totals, 1, 0.0)
    return (a3 * carry[:, None, :] + b3).reshape(n, c)


def _neg_expm1_twice(log_a, a):
    return -jnp.tanh(log_a) * (a * a + 1.0)


def _softplus(z):
    e = jnp.exp(-jnp.abs(z))
    w = 1.0 + e
    log1p = jnp.where(w == 1.0, e, jnp.log(w) * (e / jnp.where(w == 1.0, 1.0, w - 1.0)))
    return jnp.maximum(z, 0.0) + log1p


def _conv(up, cw, cb):
    out = cb + cw[CONV_WIDTH - 1:CONV_WIDTH, :] * up
    for j in range(CONV_WIDTH - 1):
        out = out + cw[j:j + 1, :] * _shift_down(up, CONV_WIDTH - 1 - j, 0.0)
    return out


def _lru_gates(u, wa_ref, ba_ref, wx_ref, bx_ref, lam_ref):
    ub = u.astype(BF16)
    r = _sigmoid(_dot(ub, wa_ref[...].astype(BF16)) + ba_ref[...])
    i = _sigmoid(_dot(ub, wx_ref[...].astype(BF16)) + bx_ref[...])
    sp = _softplus(-lam_ref[...])
    log_a = (-LRU_C) * r * sp
    a = jnp.exp(log_a)
    mult = jnp.sqrt(_neg_expm1_twice(log_a, a))
    return r, i, sp, a, mult


def _lru_specs(s):
    cb = RNN_BLOCK
    vec = pl.BlockSpec((1, cb), lambda n, b: (0, n))
    return dict(
        up=pl.BlockSpec((None, s, cb), lambda n, b: (b, 0, OFF_RNN_X // cb + n)),
        gr=pl.BlockSpec((None, s, cb), lambda n, b: (b, 0, OFF_RNN_G // cb + n)),
        act=pl.BlockSpec((None, s, cb), lambda n, b: (b, 0, n)),
        cw=pl.BlockSpec((CONV_WIDTH, cb), lambda n, b: (0, n)),
        vec=vec,
        wblk=pl.BlockSpec((None, cb, cb), lambda n, b: (n, 0, 0)),
    )


def _lru_fwd(proj3, cw, cb, wa, ba, wx, bx, lam, riders):
    bsz, s, _ = proj3.shape
    sp = _lru_specs(s)
    nr = len(riders)

    def body(up_ref, gr_ref, cw_ref, cb_ref, wa_ref, ba_ref, wx_ref, bx_ref, lam_ref, *refs):
        rider_in, (h_ref, y_ref), rider_out = refs[:nr], refs[nr:nr + 2], refs[nr + 2:2 * nr + 2]
        sa_ref, sb_ref = refs[2 * nr + 2:2 * nr + 4]
        start, pass_on, finish = _row_gather(rider_in, rider_out, *refs[2 * nr + 4:])
        step = pl.program_id(0) * bsz + pl.program_id(1)
        first, last = step == 0, step == N_RNN_BLOCKS * bsz - 1
        pl.when(first)(start)
        pl.when(step == (3 * N_RNN_BLOCKS * bsz) // 4)(pass_on)
        u = _conv(up_ref[...], cw_ref[...], cb_ref[...])
        _, i, _, a, mult = _lru_gates(u, wa_ref, ba_ref, wx_ref, bx_ref, lam_ref)
        h = _scan(a, mult * (i * u), sa_ref, sb_ref, reverse=False)
        h_ref[...] = h
        g = gr_ref[...]
        y_ref[...] = (h * (g * _sigmoid(g))).astype(BF16)
        pl.when(last)(finish)

    res = pl.pallas_call(
        body, name="lru_fwd", grid=(N_RNN_BLOCKS, bsz),
        out_shape=[jax.ShapeDtypeStruct((bsz, s, D_RNN), F32), jax.ShapeDtypeStruct((bsz, s, D_RNN), BF16)] + [
            jax.ShapeDtypeStruct(r.shape, r.dtype) for r in riders],
        in_specs=[sp["up"], sp["gr"], sp["cw"], sp["vec"], sp["wblk"], sp["vec"], sp["wblk"], sp["vec"], sp["vec"]] + [
            ANY] * nr,
        out_specs=[sp["act"], sp["act"]] + [ANY] * nr, input_output_aliases={9 + t: 2 + t for t in range(nr)},
        scratch_shapes=[pltpu.VMEM((s, RNN_BLOCK), F32)] * 2 + [pltpu.SemaphoreType.DMA((3 * nr,))] * 4,
        compiler_params=_params())(proj3, proj3, cw, cb, wa, ba, wx, bx, lam, *riders)
    return res[0], res[1], res[2:]


def _lru_bwd(proj3, h3, dy3, dproj3, cw, cb, wa, ba, wx, bx, lam):
    bsz, s, _ = proj3.shape
    sp = _lru_specs(s)
    n_steps = N_RNN_BLOCKS * bsz

    def body(up_ref, gr_ref, h_ref, dy_ref, cw_ref, cb_ref, wa_ref, ba_ref, wx_ref, bx_ref, lam_ref, dp_in,
             dp_ref, dcw_ref, dcb_ref, dwa_ref, dba_ref, dwx_ref, dbx_ref, dlam_ref, sa_ref, sb_ref,
             dup_st, dgr_st, o_sems):
        del dp_in
        blk = pl.program_id(0)
        b = pl.program_id(1)
        step = blk * bsz + b
        slot = step % 2
        stages = [dup_st, dgr_st]
        dsts = [dp_ref.at[b, :, pl.ds(pl.multiple_of(OFF_RNN_X + blk * RNN_BLOCK, LANES), RNN_BLOCK)],
                dp_ref.at[b, :, pl.ds(pl.multiple_of(OFF_RNN_G + blk * RNN_BLOCK, LANES), RNN_BLOCK)]]
        _staged_reuse(step, stages, dsts, o_sems, slot)
        up = up_ref[...]
        cwv = cw_ref[...]
        u = _conv(up, cwv, cb_ref[...])
        r, i, spv, a, mult = _lru_gates(u, wa_ref, ba_ref, wx_ref, bx_ref, lam_ref)
        h = h_ref[...]
        g = gr_ref[...]
        dy = dy_ref[...]
        sg = _sigmoid(g)
        dgr_st[slot] = (dy * h * (sg * (1.0 + g * (1.0 - sg)))).astype(BF16)
        dh = dy * (g * sg)
        adj = _scan(_shift_up(a, 1, 0.0), dh, sa_ref, sb_ref, reverse=True)
        da = adj * _shift_down(h, 1, 0.0)
        dmult = adj * (i * u)
        di = adj * mult * u
        du = adj * mult * i
        dla = da * a - dmult * (a * a) / mult
        dr = dla * ((-LRU_C) * spv)
        dsp = jnp.sum(dla * ((-LRU_C) * r), axis=0, keepdims=True)
        dza = dr * r * (1.0 - r)
        dzx = di * i * (1.0 - i)
        ub = u.astype(BF16)
        dzab = dza.astype(BF16)
        dzxb = dzx.astype(BF16)
        du = du + _dot_nt(dzab, wa_ref[...].astype(BF16)) + _dot_nt(dzxb, wx_ref[...].astype(BF16))
        dup = cwv[CONV_WIDTH - 1:CONV_WIDTH, :] * du
        for j in range(CONV_WIDTH - 1):
            dup = dup + cwv[j:j + 1, :] * _shift_up(du, CONV_WIDTH - 1 - j, 0.0)
        dup_st[slot] = dup.astype(BF16)
        _staged_flush(step, n_steps, stages, dsts, o_sems, slot)

        @pl.when(b == 0)
        def _():
            for ref in (dcw_ref, dcb_ref, dwa_ref, dba_ref, dwx_ref, dbx_ref, dlam_ref):
                ref[...] = jnp.zeros_like(ref)

        rows = [jnp.sum(du * _shift_down(up, CONV_WIDTH - 1 - j, 0.0), axis=0, keepdims=True)
                for j in range(CONV_WIDTH - 1)]
        rows.append(jnp.sum(du * up, axis=0, keepdims=True))
        dcw_ref[...] += jnp.concatenate(rows, axis=0)
        dcb_ref[...] += jnp.sum(du, axis=0, keepdims=True)
        dwa_ref[...] += _dot_tn(ub, dzab)
        dba_ref[...] += jnp.sum(dza, axis=0, keepdims=True)
        dwx_ref[...] += _dot_tn(ub, dzxb)
        dbx_ref[...] += jnp.sum(dzx, axis=0, keepdims=True)
        dlam_ref[...] += dsp * (-_sigmoid(-lam_ref[...]))

    vec = jax.ShapeDtypeStruct((1, D_RNN), F32)
    wsd = jax.ShapeDtypeStruct((N_RNN_BLOCKS, RNN_BLOCK, RNN_BLOCK), F32)
    return pl.pallas_call(
        body, name="lru_bwd", grid=(N_RNN_BLOCKS, bsz),
        out_shape=(jax.ShapeDtypeStruct(dproj3.shape, dproj3.dtype), jax.ShapeDtypeStruct((CONV_WIDTH, D_RNN), F32),
                   vec, wsd, vec, wsd, vec, vec),
        in_specs=[sp["up"], sp["gr"], sp["act"], sp["act"], sp["cw"], sp["vec"], sp["wblk"], sp["vec"],
                  sp["wblk"], sp["vec"], sp["vec"], ANY],
        out_specs=(ANY, sp["cw"], sp["vec"], sp["wblk"], sp["vec"], sp["wblk"], sp["vec"], sp["vec"]),
        input_output_aliases={11: 0},
        scratch_shapes=[pltpu.VMEM((s, RNN_BLOCK), F32)] * 2 + [pltpu.VMEM((2, s, RNN_BLOCK), BF16)] * 2 + [
            pltpu.SemaphoreType.DMA((4,))],
        compiler_params=_params())(proj3, proj3, h3, dy3, cw, cb, wa, ba, wx, bx, lam, dproj3)


def _rope_tables(s):
    half = ROPE_DIM // 2
    pos = jnp.arange(s, dtype=F32)
    inv_freq = ROPE_THETA ** (-jnp.arange(0, ROPE_DIM, 2, dtype=F32) / ROPE_DIM)
    ang = pos[:, None] * inv_freq[None, :]
    cos, sin = jnp.cos(ang), jnp.sin(ang)
    rest = HEAD_DIM - ROPE_DIM
    cos64 = jnp.concatenate([cos, cos, jnp.ones((s, rest), F32)], axis=1)
    sin64 = jnp.concatenate([-sin, sin, jnp.zeros((s, rest), F32)], axis=1)
    assert half * 2 == ROPE_DIM
    return jnp.tile(cos64, (1, LANES // HEAD_DIM)), jnp.tile(sin64, (1, LANES // HEAD_DIM))


def _swap_rot_halves(v):
    half = ROPE_DIM // 2
    lane = lax.broadcasted_iota(jnp.int32, v.shape, 1) % HEAD_DIM
    second = jnp.where(lane < ROPE_DIM, pltpu.roll(v, half, axis=1), 0.0)
    return jnp.where(lane < half, pltpu.roll(v, LANES - half, axis=1), second)


def _rope(v, cos, sin):
    tiles = []
    for t in range(v.shape[1] // LANES):
        vt = v[:, t * LANES:(t + 1) * LANES]
        tiles.append(vt * cos + _swap_rot_halves(vt) * sin)
    return tiles[0] if len(tiles) == 1 else jnp.concatenate(tiles, axis=1)


def _unrope(v, cos, sin):
    tiles = []
    for t in range(v.shape[1] // LANES):
        vt = v[:, t * LANES:(t + 1) * LANES]
        tiles.append(vt * cos + _swap_rot_halves(vt * sin))
    return tiles[0] if len(tiles) == 1 else jnp.concatenate(tiles, axis=1)


HEADS_PER_STEP = 8
QW = HEADS_PER_STEP * HEAD_DIM
N_PAIRS = N_Q_HEADS // HEADS_PER_STEP
Q_PER_KV = N_Q_HEADS // N_KV_HEADS
KV_PER_STEP = HEADS_PER_STEP // Q_PER_KV


QT_COLS = Q_PER_KV * ATT_BLK


def _attn_saved_shapes(bsz, s):
    nb = s // ATT_BLK
    pad = s + ATT_BLK
    return [(bsz, N_PAIRS, nb, LANES, QT_COLS), (bsz, N_PAIRS, KV_PER_STEP, pad, LANES),
            (bsz, N_PAIRS, KV_PER_STEP, pad, LANES), (bsz, N_PAIRS, LANES, pad)]


def _attn_specs(s, order):
    def mk(width, base, **kw):
        if order == "bp":
            return pl.BlockSpec((None, s, width), lambda b, p: (b, 0, base + p), **kw)
        return pl.BlockSpec((None, s, width), lambda p, b: (b, 0, base + p), **kw)

    def saved(shape, **kw):
        blk = (None, None) + shape[2:]
        zeros = (0,) * (len(shape) - 2)
        if order == "bp":
            return pl.BlockSpec(blk, lambda b, p: (b, p) + zeros, **kw)
        return pl.BlockSpec(blk, lambda p, b: (b, p) + zeros, **kw)

    one = dict(pipeline_mode=pl.Buffered(1))
    tbl = pl.BlockSpec((s, LANES), lambda *_: (0, 0))
    shapes = _attn_saved_shapes(1, s)
    return dict(q=mk(QW, OFF_Q // QW), k=mk(LANES, OFF_K // LANES), v=mk(LANES, OFF_V // LANES),
                g=mk(QW, OFF_ATTN_G // QW), act=mk(QW, 0), kv=mk(LANES, 0), tbl=tbl,
                g1=mk(QW, OFF_ATTN_G // QW, **one), act1=mk(QW, 0, **one),
                saved=[saved(sh) for sh in shapes], saved1=[saved(sh, **one) for sh in shapes],
                smem=pl.BlockSpec(memory_space=pltpu.SMEM))


def _to_qt(blk):
    rows = []
    for j in range(KV_PER_STEP):
        cols = []
        for tt in range(2):
            t = 2 * j + tt
            tr = blk[:, t * LANES:(t + 1) * LANES].T
            cols += [tr[0:HEAD_DIM, :], tr[HEAD_DIM:, :]]
        rows.append(jnp.concatenate(cols, axis=1))
    return jnp.concatenate(rows, axis=0)


def _from_qt(xt):
    tiles = []
    for j in range(KV_PER_STEP):
        for tt in range(2):
            g0 = 2 * tt
            pair = jnp.concatenate([xt[j * HEAD_DIM:(j + 1) * HEAD_DIM, (g0 + i) * ATT_BLK:(g0 + i + 1) * ATT_BLK]
                                    for i in range(2)], axis=0)
            tiles.append(pair.T)
    return jnp.concatenate(tiles, axis=1)


def _attn_prep(q_ref, k_ref, v_ref, cos_ref, sin_ref, qt_ref, km_ref, vm_ref, kt_ref, vt_ref, nb):
    zeros = jnp.zeros((ATT_BLK, LANES), BF16)
    for j in range(KV_PER_STEP):
        km_ref[j, 0:ATT_BLK, :] = zeros
        vm_ref[j, 0:ATT_BLK, :] = zeros
    kt_ref[:, 0:ATT_BLK] = zeros
    vt_ref[:, 0:ATT_BLK] = zeros
    head_of_lane = lax.broadcasted_iota(jnp.int32, (ATT_BLK, LANES), 1) // HEAD_DIM

    def prep(n, carry):
        r0 = pl.multiple_of(n * ATT_BLK, ATT_BLK)
        cs = cos_ref[pl.ds(r0, ATT_BLK), :]
        sn = sin_ref[pl.ds(r0, ATT_BLK), :]
        qt_ref[n] = _to_qt(_rope(q_ref[pl.ds(r0, ATT_BLK), :], cs, sn) * ATTN_SCALE).astype(BF16)
        k = _rope(k_ref[pl.ds(r0, ATT_BLK), :], cs, sn)
        v = v_ref[pl.ds(r0, ATT_BLK), :]
        for j in range(KV_PER_STEP):
            km_ref[j, pl.ds(r0 + ATT_BLK, ATT_BLK), :] = jnp.where(head_of_lane == j, k, 0.0).astype(BF16)
            vm_ref[j, pl.ds(r0 + ATT_BLK, ATT_BLK), :] = jnp.where(head_of_lane == j, v, 0.0).astype(BF16)
        kt_ref[:, pl.ds(r0 + ATT_BLK, ATT_BLK)] = k.T.astype(BF16)
        vt_ref[:, pl.ds(r0 + ATT_BLK, ATT_BLK)] = v.T.astype(BF16)
        return carry

    lax.fori_loop(0, nb, prep, 0, unroll=4)


assert WINDOW == ATT_BLK


def _from_prev_block():
    key = lax.broadcasted_iota(jnp.int32, (ATT_BLK, QT_COLS), 0)
    qry = lax.broadcasted_iota(jnp.int32, (ATT_BLK, QT_COLS), 1) % ATT_BLK
    return key > qry


def _fold(tile, prev, prev_bias=None):
    top = tile[:ATT_BLK] if prev_bias is None else tile[:ATT_BLK] + prev_bias
    return jnp.where(prev, top, tile[ATT_BLK:])


def _unfold(folded, prev):
    zero = jnp.zeros_like(folded)
    return jnp.concatenate([jnp.where(prev, folded, zero), jnp.where(prev, zero, folded)], axis=0).astype(BF16)


def _no_prev_bias(n):
    return jnp.where(n == 0, NEG_BIG, 0.0).astype(F32)


def _sink_row(sink_ref, first):
    return jnp.concatenate([jnp.full((1, ATT_BLK), sink_ref[first + g], F32) for g in range(Q_PER_KV)], axis=1)


def _softmax_cols(sc, sink):
    m = jnp.maximum(jnp.max(sc, axis=0, keepdims=True), sink)
    e = jnp.exp(sc - m)
    es = jnp.exp(sink - m)
    inv = 1.0 / (jnp.sum(e, axis=0, keepdims=True) + es)
    return e * inv, es * inv


def _attn_fwd(proj3, sinks, cosf, sinf):
    bsz, s, _ = proj3.shape
    nb = s // ATT_BLK
    sp = _attn_specs(s, "bp")

    def body(sink_ref, q_ref, k_ref, v_ref, g_ref, cos_ref, sin_ref, o_ref, y_ref, qt_sc, km_sc, vm_sc, kt_ref, vt_sc):
        p = pl.program_id(1)
        _attn_prep(q_ref, k_ref, v_ref, cos_ref, sin_ref, qt_sc, km_sc, vm_sc, kt_ref, vt_sc, nb)
        kv_row = lax.broadcasted_iota(jnp.int32, (LANES, QT_COLS), 0) // HEAD_DIM
        prev = _from_prev_block()

        def blk(n, carry):
            r0 = pl.multiple_of(n * ATT_BLK, ATT_BLK)
            bias = _no_prev_bias(n)
            rq = qt_sc[n]
            vt = vt_sc[:, pl.ds(r0, 2 * ATT_BLK)]
            ots = []
            for j in range(KV_PER_STEP):
                st = _dot(km_sc[j, pl.ds(r0, 2 * ATT_BLK), :], rq)
                pc, _ = _softmax_cols(_fold(st, prev, bias), _sink_row(sink_ref, p * HEADS_PER_STEP + j * Q_PER_KV))
                ots.append(_dot(vt, _unfold(pc, prev)))
            o = _from_qt(jnp.where(kv_row == 0, ots[0], ots[1]))
            o_ref[pl.ds(r0, ATT_BLK), :] = o
            g = g_ref[pl.ds(r0, ATT_BLK), :]
            y_ref[pl.ds(r0, ATT_BLK), :] = (o * (g * _sigmoid(g))).astype(BF16)
            return carry

        lax.fori_loop(0, nb, blk, 0, unroll=4)

    res = pl.pallas_call(
        body, name="attn_fwd", grid=(bsz, N_PAIRS),
        out_shape=[jax.ShapeDtypeStruct((bsz, s, D_ATTN), F32), jax.ShapeDtypeStruct((bsz, s, D_ATTN), BF16)] + [
            jax.ShapeDtypeStruct(sh, BF16) for sh in _attn_saved_shapes(bsz, s)],
        in_specs=[sp["smem"], sp["q"], sp["k"], sp["v"], sp["g"], sp["tbl"], sp["tbl"]],
        out_specs=[sp["act"], sp["act"]] + sp["saved"],
        scratch_shapes=[pltpu.VMEM((LANES, s + ATT_BLK), BF16)],
        compiler_params=_params())(sinks, proj3, proj3, proj3, proj3, cosf, sinf)
    return res[0], res[1], res[2:]


def _attn_bwd(proj3, saved, o3, dy3, dproj3, sinks, cosf, sinf):
    bsz, s, _ = proj3.shape
    nb = s // ATT_BLK
    assert nb % 2 == 0
    sp = _attn_specs(s, "pb")
    n_steps = N_PAIRS * bsz

    def body(sink_ref, qt_sc, km_sc, vm_sc, kt_sc, g_ref, o_ref, dy_ref, cos_ref, sin_ref, dp_in,
             dp_ref, ds_ref, dot_sc, dqt_sc, dk_sc, dv_sc, dq_st, dk_st, dv_st, dg_st, o_sems):
        del dp_in
        p = pl.program_id(0)
        b = pl.program_id(1)
        step = p * bsz + b
        slot = step % 2
        stages = [dq_st, dk_st, dv_st, dg_st]
        dsts = [dp_ref.at[b, :, pl.ds(pl.multiple_of(OFF_Q + p * QW, LANES), QW)],
                dp_ref.at[b, :, pl.ds(pl.multiple_of(OFF_K + p * LANES, LANES), LANES)],
                dp_ref.at[b, :, pl.ds(pl.multiple_of(OFF_V + p * LANES, LANES), LANES)],
                dp_ref.at[b, :, pl.ds(pl.multiple_of(OFF_ATTN_G + p * QW, LANES), QW)]]
        _staged_reuse(step, stages, dsts, o_sems, slot)
        dk_sc[...] = jnp.zeros_like(dk_sc)
        dv_sc[...] = jnp.zeros_like(dv_sc)

        def gate(n, carry):
            r0 = pl.multiple_of(n * ATT_BLK, ATT_BLK)
            g = g_ref[pl.ds(r0, ATT_BLK), :]
            dy = dy_ref[pl.ds(r0, ATT_BLK), :]
            sg = _sigmoid(g)
            dg_st[slot, pl.ds(r0, ATT_BLK), :] = (dy * o_ref[pl.ds(r0, ATT_BLK), :] * (sg * (1.0 + g * (1.0 - sg)))).astype(BF16)
            dot_sc[n] = _to_qt(dy * (g * sg)).astype(BF16)
            return carry

        lax.fori_loop(0, nb, gate, 0, unroll=2)
        kv_lane = lax.broadcasted_iota(jnp.int32, (2 * ATT_BLK, LANES), 1) // HEAD_DIM
        kv_row = lax.broadcasted_iota(jnp.int32, (LANES, QT_COLS), 0) // HEAD_DIM
        prev = _from_prev_block()

        def blk(n, acc):
            r0 = pl.multiple_of(n * ATT_BLK, ATT_BLK)
            bias = _no_prev_bias(n)
            rq = qt_sc[n]
            rd = dot_sc[n]
            kt = kt_sc[:, pl.ds(r0, 2 * ATT_BLK)]
            dvs, dks, dqs, new_acc = [], [], [], []
            for j in range(KV_PER_STEP):
                st = _dot(km_sc[j, pl.ds(r0, 2 * ATT_BLK), :], rq)
                pc, ps = _softmax_cols(_fold(st, prev, bias), _sink_row(sink_ref, p * HEADS_PER_STEP + j * Q_PER_KV))
                dpc = _fold(_dot(vm_sc[j, pl.ds(r0, 2 * ATT_BLK), :], rd), prev)
                delta = jnp.sum(pc * dpc, axis=0, keepdims=True)
                dst = _unfold(pc * (dpc - delta), prev)
                new_acc.append(acc[j] + ps * delta)
                dvs.append(_dot_nt(_unfold(pc, prev), rd))
                dks.append(_dot_nt(dst, rq))
                dqs.append(_dot(kt, dst))
            dv_sc[pl.ds(r0, 2 * ATT_BLK), :] += jnp.where(kv_lane == 0, dvs[0], dvs[1])
            dk_sc[pl.ds(r0, 2 * ATT_BLK), :] += jnp.where(kv_lane == 0, dks[0], dks[1])
            dqt_sc[n] = jnp.where(kv_row == 0, dqs[0], dqs[1]) * ATTN_SCALE
            return tuple(new_acc)

        per_trip = 4 if nb % 4 == 0 else 2

        def blk_group(m, acc):
            for u in range(per_trip):
                acc = blk(per_trip * m + u, acc)
            return acc

        acc = lax.fori_loop(0, nb // per_trip, blk_group, tuple(jnp.zeros((1, QT_COLS), F32) for _ in range(KV_PER_STEP)))
        lane1 = lax.broadcasted_iota(jnp.int32, (1, LANES), 1)
        dsink = jnp.zeros((1, LANES), F32)
        for j in range(KV_PER_STEP):
            for i in range(Q_PER_KV):
                part = jnp.sum(acc[j][:, i * ATT_BLK:(i + 1) * ATT_BLK], axis=1, keepdims=True)
                dsink = dsink - jnp.where(lane1 == j * Q_PER_KV + i, part, 0.0)

        @pl.when(b == 0)
        def _():
            ds_ref[...] = jnp.zeros_like(ds_ref)

        ds_ref[...] += dsink

        def post(n, carry):
            r0 = pl.multiple_of(n * ATT_BLK, ATT_BLK)
            cs = cos_ref[pl.ds(r0, ATT_BLK), :]
            sn = sin_ref[pl.ds(r0, ATT_BLK), :]
            dq_st[slot, pl.ds(r0, ATT_BLK), :] = _unrope(_from_qt(dqt_sc[n]), cs, sn).astype(BF16)
            dk_st[slot, pl.ds(r0, ATT_BLK), :] = _unrope(dk_sc[pl.ds(r0 + ATT_BLK, ATT_BLK), :], cs, sn).astype(BF16)
            dv_st[slot, pl.ds(r0, ATT_BLK), :] = dv_sc[pl.ds(r0 + ATT_BLK, ATT_BLK), :].astype(BF16)
            return carry

        lax.fori_loop(0, nb, post, 0, unroll=2)
        _staged_flush(step, n_steps, stages, dsts, o_sems, slot)

    n_in = 1 + len(saved) + 5
    return pl.pallas_call(
        body, name="attn_bwd", grid=(N_PAIRS, bsz),
        out_shape=(jax.ShapeDtypeStruct(dproj3.shape, dproj3.dtype), jax.ShapeDtypeStruct((N_PAIRS, 1, LANES), F32)),
        in_specs=[sp["smem"]] + sp["saved1"] + [sp["g1"], sp["act1"], sp["act1"], sp["tbl"], sp["tbl"], ANY],
        out_specs=(ANY, pl.BlockSpec((None, 1, LANES), lambda p, b: (p, 0, 0))),
        input_output_aliases={n_in: 0},
        scratch_shapes=[pltpu.VMEM((nb, LANES, QT_COLS), BF16),
                        pltpu.VMEM((nb, LANES, QT_COLS), F32),
                        pltpu.VMEM((s + ATT_BLK, LANES), F32),
                        pltpu.VMEM((s + ATT_BLK, LANES), F32),
                        pltpu.VMEM((2, s, QW), BF16), pltpu.VMEM((2, s, LANES), BF16),
                        pltpu.VMEM((2, s, LANES), BF16), pltpu.VMEM((2, s, QW), BF16),
                        pltpu.SemaphoreType.DMA((8,))],
        compiler_params=_params())(sinks, *saved, proj3, o3, dy3, cosf, sinf, dproj3)


def _staged_copies(stages, dsts, sems, slot):
    return [pltpu.make_async_copy(st.at[slot], dst, sems.at[slot * len(stages) + t])
            for t, (st, dst) in enumerate(zip(stages, dsts))]


def _staged_reuse(step, stages, dsts, sems, slot):
    @pl.when(step >= 2)
    def _():
        for cp in _staged_copies(stages, dsts, sems, slot):
            cp.wait()


def _staged_flush(step, n_steps, stages, dsts, sems, slot):
    for cp in _staged_copies(stages, dsts, sems, slot):
        cp.start()

    @pl.when(step == n_steps - 1)
    def _():
        for cp in _staged_copies(stages, dsts, sems, slot):
            cp.wait()
        if n_steps >= 2:
            for cp in _staged_copies(stages, dsts, sems, 1 - slot):
                cp.wait()


def _merge_fwd_bwd(x, tgt, y_rnn, y_attn, proj, w_r, w_a, w_o, gf):
    t, d = x.shape
    tm = min(t, 256)
    nt = t // tm

    hw = d // 2

    def body(x_ref, t_ref, yr_ref, ya_ref, mr0_ref, mr1_ref, ma0_ref, ma1_ref, wr_ref, wa_ref, wo_ref, gf_ref,
             dp_ref, dyr_ref, dya_ref, mg_ref, dx2_ref, dx2b_ref, dpr_ref, dpa_ref, loss_ref, dgf_ref, dmg_st, o_sems):
        i = pl.program_id(0)
        slot = i % 2
        dsts = [dp_ref.at[pl.ds(pl.multiple_of(i * tm, tm), tm), pl.ds(OFF_MERGE_R, 2 * d)]]
        _staged_reuse(i, [dmg_st], dsts, o_sems, slot)
        wr = wr_ref[...]
        wa = wa_ref[...]
        wo = wo_ref[...]
        gfv = gf_ref[...]
        pr = _dot(yr_ref[...], wr)
        pa = _dot(ya_ref[...], wa)
        sr = _sigmoid(jnp.concatenate([mr0_ref[...], mr1_ref[...]], axis=1))
        sa = _sigmoid(jnp.concatenate([ma0_ref[...], ma1_ref[...]], axis=1))
        mb = (sr * pr + sa * pa).astype(BF16)
        mg_ref[...] = mb
        x2 = x_ref[...] + _dot(mb, wo)
        r2 = lax.rsqrt(jnp.mean(x2 * x2, axis=-1, keepdims=True) + NORM_EPS)
        nrm = x2 * r2
        err = nrm * gfv - t_ref[...]
        dy = err * (1.0 / d)
        dn = dy * gfv
        dx2 = r2 * (dn - nrm * jnp.mean(dn * nrm, axis=-1, keepdims=True))
        dx2_ref[...] = dx2
        dx2b = dx2.astype(BF16)
        dx2b_ref[...] = dx2b
        dmerged = _dot_nt(dx2b, wo)
        dpr = (dmerged * sr).astype(BF16)
        dpa = (dmerged * sa).astype(BF16)
        dpr_ref[...] = dpr
        dpa_ref[...] = dpa
        dmg_st[slot, :, 0:d] = (dmerged * pr * (sr * (1.0 - sr))).astype(BF16)
        dmg_st[slot, :, d:2 * d] = (dmerged * pa * (sa * (1.0 - sa))).astype(BF16)
        _staged_flush(i, nt, [dmg_st], dsts, o_sems, slot)
        dyr_ref[...] = _dot_nt(dpr, wr)
        dya_ref[...] = _dot_nt(dpa, wa)

        @pl.when(i == 0)
        def _():
            loss_ref[...] = jnp.zeros_like(loss_ref)
            dgf_ref[...] = jnp.zeros_like(dgf_ref)

        loss_ref[...] += jnp.full((1, LANES), 0.5 / d, F32) * jnp.sum(err * err)
        dgf_ref[...] += jnp.sum(dy * nrm, axis=0, keepdims=True)

    tile = pl.BlockSpec((tm, d), lambda i: (i, 0))
    wsp = pl.BlockSpec((d, d), lambda i: (0, 0))

    def gate(col_blk):
        return pl.BlockSpec((tm, hw), lambda i: (i, col_blk))

    fb = jax.ShapeDtypeStruct((t, d), BF16)
    ff = jax.ShapeDtypeStruct((t, d), F32)
    return pl.pallas_call(
        body, name="merge_fwd_bwd", grid=(nt,),
        out_shape=(jax.ShapeDtypeStruct((t, D_IN), BF16), ff, ff, fb, ff, fb, fb, fb,
                   jax.ShapeDtypeStruct((1, LANES), F32), jax.ShapeDtypeStruct((1, d), F32)),
        in_specs=[tile, tile, tile, tile] + [gate(OFF_MERGE_R // hw + j) for j in range(4)] + [
            wsp, wsp, wsp, pl.BlockSpec((1, d), lambda i: (0, 0))],
        out_specs=(ANY, tile, tile, tile, tile, tile, tile, tile,
                   pl.BlockSpec((1, LANES), lambda i: (0, 0)), pl.BlockSpec((1, d), lambda i: (0, 0))),
        scratch_shapes=[pltpu.VMEM((2, tm, 2 * d), BF16), pltpu.SemaphoreType.DMA((2,))],
        compiler_params=_params())(x, tgt, y_rnn, y_attn, proj, proj, proj, proj, w_r, w_a, w_o, gf)


def _local_grads(x, tgt, h, proj, norm_g, w_in_bm, conv_w, conv_b, lru_w_a, lru_b_a, lru_w_x, lru_b_x, lam, sinks,
                 row_sharded, gf):
    bsz, s, d = x.shape
    t = bsz * s
    x2 = x.reshape(t, d)
    proj3 = proj.reshape(bsz, s, D_IN)
    h_lru, y_rnn, gathered = _lru_fwd(proj3, conv_w, conv_b, lru_w_a, lru_b_a, lru_w_x, lru_b_x, lam, row_sharded)
    w_r, w_a, w_o = (g.reshape(d, d) for g in gathered)
    cosf, sinf = _rope_tables(s)
    o_attn, y_attn, attn_saved = _attn_fwd(proj3, sinks, cosf, sinf)
    y_rnn2 = y_rnn.reshape(t, d)
    y_attn2 = y_attn.reshape(t, d)
    dproj, dyr, dya, merged, dx2, dx2b, dpr, dpa, loss, dgf = _merge_fwd_bwd(
        x2, tgt.reshape(t, d), y_rnn2, y_attn2, proj, w_r, w_a, w_o, gf)
    dproj3, dsink = _attn_bwd(proj3, attn_saved, o_attn, dya.reshape(bsz, s, d), dproj.reshape(bsz, s, D_IN),
                              sinks, cosf, sinf)
    dproj3, dcw, dcb, dwa, dba, dwx, dbx, dlam = _lru_bwd(
        proj3, h_lru, dyr.reshape(bsz, s, d), dproj3, conv_w, conv_b, lru_w_a, lru_b_a, lru_w_x, lru_b_x, lam)
    dproj = dproj3.reshape(t, D_IN)
    grad_x, dng = _grad_x(dproj, w_in_bm, x2, dx2, norm_g)
    small = dict(norm_g=dng, conv_w=dcw, conv_b=dcb, lru_w_a=dwa, lru_b_a=dba, lru_w_x=dwx, lru_b_x=dbx,
                 lru_lambda=dlam, attn_sinks=dsink[:, 0, :HEADS_PER_STEP].reshape(1, N_Q_HEADS), final_norm_g=dgf)
    squares = [(y_rnn2, dpr), (y_attn2, dpa), (merged, dx2b)]
    return loss[0, 0], grad_x.reshape(bsz, s, d), h, dproj, squares, small


ANY = pl.BlockSpec(memory_space=pl.ANY)


def _mesh_pos():
    return lax.axis_index("x"), lax.axis_index("y"), lax.axis_index("c")


def _remote(src, dst, send_sems, recv_sems, idx, peer):
    return pltpu.make_async_remote_copy(src_ref=src, dst_ref=dst, send_sem=send_sems.at[idx],
                                        recv_sem=recv_sems.at[idx], device_id=peer, device_id_type=MESH)


def _row_gather(ins, outs, send_sems, recv_sems, fsend_sems, frecv_sems):
    n = len(ins)
    x, y, c = _mesh_pos()
    me = 2 * x + y
    sib = (x, y, 1 - c)
    peers = [((x, 1 - y, c), me ^ 1), ((1 - x, y, c), me ^ 2), ((1 - x, 1 - y, c), me ^ 3)]

    def half(ref, slot, t, which):
        hr = ins[t].shape[1] // 2
        return ref.at[slot, pl.ds(pl.multiple_of(which * hr, 8), hr), :]

    def ici(t, k):
        peer, pj = peers[k]
        src = half(ins[t], me, t, c)
        return (_remote(src, half(outs[t], me, t, c), send_sems, recv_sems, 3 * t + k, peer),
                _remote(src, half(outs[t], pj, t, c), send_sems, recv_sems, 3 * t + k, peer))

    def forward(t, k):
        got = half(outs[t], peers[k][1], t, c)
        return (_remote(got, got, fsend_sems, frecv_sems, 3 * t + k, sib),
                _remote(got, half(outs[t], peers[k][1], t, 1 - c), fsend_sems, frecv_sems, 3 * t + k, sib))

    pairs = [(t, k) for t in range(n) for k in range(3)]

    def start():
        for t, k in pairs:
            ici(t, k)[0].start()

    def pass_on():
        for t, k in pairs:
            ici(t, k)[1].wait_recv()
            forward(t, k)[0].start()

    def finish():
        for t, k in pairs:
            ici(t, k)[0].wait_send()
            forward(t, k)[0].wait_send()
            forward(t, k)[1].wait_recv()

    return start, pass_on, finish


def _gather_in_proj(x, g, bufs, split, idx):
    t_tok, d = x.shape
    n = len(bufs)
    tm = min(t_tok, 1024)
    nt = t_tok // tm
    n_fwd = 3 * sum(split)
    assert split[0]

    def body(idx_ref, x_ref, g_ref, *refs):
        ins, proj_ref, h_out, outs = refs[:n], refs[n], refs[n + 1], refs[n + 2:2 * n + 2]
        wbuf, h_all, send_sems, recv_sems, fsend_sems, frecv_sems, l_sems = refs[2 * n + 2:]
        j, i = pl.program_id(0), pl.program_id(1)
        rows = pl.ds(pl.multiple_of(i * tm, tm), tm)
        x, y, c = _mesh_pos()
        me = 2 * x + y
        sib = (x, y, 1 - c)
        peers = [((x, 1 - y, c), me ^ 1), ((1 - x, y, c), me ^ 2), ((1 - x, 1 - y, c), me ^ 3)]

        def part(ref, slot, t, half):
            if not split[t]:
                return ref.at[slot]
            hr = bufs[t].shape[1] // 2
            return ref.at[slot, pl.ds(pl.multiple_of(half * hr, 8), hr), :]

        def land(t):
            return wbuf if t == 0 else outs[t]

        def ici(t, k):
            peer, pj = peers[k]
            src = part(ins[t], me, t, c)
            return (_remote(src, part(land(t), me, t, c), send_sems, recv_sems, 3 * t + k, peer),
                    _remote(src, part(land(t), pj, t, c), send_sems, recv_sems, 3 * t + k, peer))

        fwd_index = {}
        for t in range(n):
            if split[t]:
                for k in range(3):
                    fwd_index[(t, k)] = len(fwd_index)

        def forward(t, k):
            pj = peers[k][1]
            got = part(land(t), pj, t, c)
            f = fwd_index[(t, k)]
            return (_remote(got, got, fsend_sems, frecv_sems, f, sib),
                    _remote(got, part(land(t), pj, t, 1 - c), fsend_sems, frecv_sems, f, sib))

        def write_back(k):
            pj = peers[k][1]
            return pltpu.make_async_copy(wbuf.at[pj], outs[0].at[pj], l_sems.at[1 + k])

        relay_peer = ((x + c) % 2, (y + 1 - c) % 2, c)

        def relay():
            got = part(wbuf, me ^ (2 - c), 0, c)
            return (_remote(got, got, send_sems, recv_sems, 2, relay_peer),
                    _remote(got, part(wbuf, me ^ 3, 0, c), send_sems, recv_sems, 2, relay_peer))

        direct = [(t, k) for t in range(n) for k in range(3) if (t, k) != (0, 2)]

        @pl.when((j == 0) & (i == 0))
        def _():
            for t, k in direct:
                ici(t, k)[0].start()
            own = pltpu.make_async_copy(ins[0].at[me], wbuf.at[me], l_sems.at[0])
            own.start()
            own.wait()

        @pl.when((j == 1) & (i == 0))
        def _():
            pltpu.make_async_copy(h_all, h_out, l_sems.at[4]).start()
            for k in range(2):
                ici(0, k)[1].wait_recv()
            relay()[0].start()
            for k in range(2):
                forward(0, k)[0].start()
            forward(0, 0)[1].wait_recv()
            write_back(0).start()

        @pl.when((j == 2) & (i == 0))
        def _():
            forward(0, 1)[1].wait_recv()
            write_back(1).start()

        @pl.when((j == 3) & (i == 0))
        def _():
            relay()[1].wait_recv()
            forward(0, 2)[0].start()
            forward(0, 2)[1].wait_recv()
            write_back(2).start()

        @pl.when(j == 0)
        def _():
            xv = x_ref[...]
            r = lax.rsqrt(jnp.mean(xv * xv, axis=-1, keepdims=True) + NORM_EPS)
            h_all[rows, :] = (xv * r * g_ref[...]).astype(BF16)

        proj_ref[...] = _dot(h_all[rows, :], wbuf[me ^ j])

        @pl.when((j == N_CHIPS - 1) & (i == nt - 1))
        def _():
            pltpu.make_async_copy(h_all, h_out, l_sems.at[4]).wait()
            for t in range(1, n):
                for k in range(3):
                    ici(t, k)[1].wait_recv()
                    if split[t]:
                        forward(t, k)[0].start()
            relay()[0].wait_send()
            for t, k in direct:
                ici(t, k)[0].wait_send()
            for t in range(n):
                if split[t]:
                    for k in range(3):
                        forward(t, k)[0].wait_send()
                        if t > 0:
                            forward(t, k)[1].wait_recv()
            for k in range(3):
                write_back(k).wait()

    grid_spec = pltpu.PrefetchScalarGridSpec(
        num_scalar_prefetch=1, grid=(N_CHIPS, nt),
        in_specs=[pl.BlockSpec((tm, d), lambda j, i, idx_ref: (jnp.where(j == 0, i, nt - 1), 0)),
                  pl.BlockSpec((1, d), lambda j, i, idx_ref: (0, 0))] + [ANY] * n,
        out_specs=[pl.BlockSpec((tm, W_BLK), lambda j, i, idx_ref: (i, idx_ref[0] ^ j)), ANY] + [ANY] * n,
        scratch_shapes=[pltpu.VMEM(bufs[0].shape, bufs[0].dtype), pltpu.VMEM((t_tok, d), BF16),
                        pltpu.SemaphoreType.DMA((3 * n,)), pltpu.SemaphoreType.DMA((3 * n,)),
                        pltpu.SemaphoreType.DMA((n_fwd,)), pltpu.SemaphoreType.DMA((n_fwd,)),
                        pltpu.SemaphoreType.DMA((5,))])
    out_shape = [jax.ShapeDtypeStruct((t_tok, D_IN), F32), jax.ShapeDtypeStruct((t_tok, d), BF16)] + [
        jax.ShapeDtypeStruct(a.shape, a.dtype) for a in bufs]
    res = pl.pallas_call(
        body, name="gather_in_proj", grid_spec=grid_spec, out_shape=out_shape,
        input_output_aliases={3 + t: 2 + t for t in range(n)}, compiler_params=_params())(idx, x, g, *bufs)
    return res[1], res[0], res[2:]


def _row_tile(rows, row_bytes, cap_bytes=2 * 1024 * 1024):
    best = None
    for tr in range(8, rows + 1, 8):
        if rows % tr == 0 and tr * row_bytes <= cap_bytes:
            best = tr
    return best if best is not None else rows


XOR_ORDER = (3, 2, 1)


def _grads_reduce_scatter(h, dproj, squares, small, idx):
    t, d = h.shape
    nsq = len(squares)
    hr = d // 2
    qr = ROW_BLK // 2
    tk = min(t, 1024)
    nk = t // tk
    last = N_CHIPS - 1
    n_phase = 3

    def dest(s, idx_ref):
        xo = jnp.where(s == 0, XOR_ORDER[0], jnp.where(s == 1, XOR_ORDER[1], jnp.where(s == 2, XOR_ORDER[2], 0)))
        return idx_ref[0] ^ xo

    def k_sq(p, k):
        return jnp.where(p == 0, k, nk - 1)

    def k_w(p, k):
        return jnp.where(p == 0, 0, k)

    in_specs = [
        pl.BlockSpec((tk, hr), lambda s, p, k, idx_ref: (k_w(p, k), (1 - idx_ref[1] + jnp.maximum(p - 1, 0)) % 2)),
        pl.BlockSpec((tk, W_BLK), lambda s, p, k, idx_ref: (k_w(p, k), dest(s, idx_ref)))]
    for q in range(nsq):
        in_specs.append(pl.BlockSpec((tk, ROW_BLK), lambda s, p, k, idx_ref: (k_sq(p, k), dest(s, idx_ref))))
        in_specs.append(pl.BlockSpec((tk, d), lambda s, p, k, idx_ref: (k_sq(p, k), 0)))

    def body(idx_ref, *refs):
        nj = 1 + nsq
        h_ref, dp_ref = refs[0], refs[1]
        sq_in = refs[2:2 + 2 * nsq]
        small_in = refs[2 * nj]
        outs = refs[2 * nj + 1:3 * nj + 2]
        landing = refs[3 * nj + 2:4 * nj + 3]
        sc = refs[4 * nj + 3:]
        acc_w, xr_w, sb_w = sc[0:3]
        sq_sc = [sc[3 + 3 * q:6 + 3 * q] for q in range(nsq)]
        sm, smx = sc[3 * nj:3 * nj + 2]
        x_send, x_recv, i_send, i_recv, f_send, f_recv, o_sem, l_sem = sc[3 * nj + 2:]
        s, p, k = pl.program_id(0), pl.program_id(1), pl.program_id(2)
        x, y, c = _mesh_pos()
        sib = (x, y, 1 - c)
        peers = [((1 - x) if xo & 2 else x, (1 - y) if xo & 1 else y, c) for xo in XOR_ORDER]
        slot = s % 2
        mine_w = pl.ds(pl.multiple_of(c * hr, 8), hr)
        theirs_w = pl.ds(pl.multiple_of((1 - c) * hr, 8), hr)
        mine_q = pl.ds(pl.multiple_of(c * qr, 8), qr)
        theirs_q = pl.ds(pl.multiple_of((1 - c) * qr, 8), qr)

        def exch(j, src, dst):
            return _remote(src, dst, x_send, x_recv, 2 * j + slot, sib)

        sbufs = [sb_w] + [sq_sc[q][2] for q in range(nsq)]

        def ici(j, ss):
            return _remote(sbufs[j].at[ss], landing[j].at[ss], i_send, i_recv, last * j + ss, peers[ss])

        def exchanges():
            cps = [exch(0, acc_w.at[0], xr_w.at[slot])]
            cps += [exch(1 + q, sq_sc[q][0].at[theirs_q, :], sq_sc[q][1].at[slot]) for q in range(nsq)]
            return cps

        def small_send(ss):
            return _remote(sm.at[c], landing[nj].at[ss], i_send, i_recv, last * nj + ss, peers[ss])

        def small_start():
            load = pltpu.make_async_copy(small_in, sm, l_sem.at[nj + 1])
            load.start()
            load.wait()
            swap = _remote(sm, smx.at[pl.ds(0, 2)], x_send, x_recv, 2 * nj, sib)
            swap.start()
            swap.wait_recv()
            swap.wait_send()
            sm[...] = sm[...] + smx[0:2]
            for ss in range(last):
                small_send(ss).start()

        def pair_ref(j):
            return acc_w.at[1] if j == 0 else sq_sc[j - 1][0].at[mine_q, :]

        def sq_phase():
            pl.when((s == 0) & (k == 0))(small_start)
            for q in range(nsq):
                acc = sq_sc[q][0]

                @pl.when(k == 0)
                def _():
                    acc[...] = jnp.zeros((ROW_BLK, d), F32)

                acc[...] += _dot_tn(sq_in[2 * q][...], sq_in[2 * q + 1][...])

            @pl.when(k == nk - 1)
            def _():
                for cp in exchanges()[1:]:
                    cp.start()

        def w_phase(hf):
            @pl.when(k == 0)
            def _():
                acc_w[hf] = jnp.zeros((hr, W_BLK), F32)

            acc_w[hf] += _dot_tn(h_ref[...], dp_ref[...])

            @pl.when(k == nk - 1)
            def _():
                if hf == 0:
                    exchanges()[0].start()
                else:
                    finish_step()

        def finish_step():
            for cp in exchanges():
                cp.wait_recv()
                cp.wait_send()
            acc_w[1] += xr_w[slot]
            for q in range(nsq):
                sq_sc[q][0][mine_q, :] += sq_sc[q][1][slot]
            for ss in range(last):
                @pl.when(s == ss)
                def _():
                    for j in range(nj):
                        sbufs[j][ss] = pair_ref(j)[...].astype(BF16)
                        ici(j, ss).start()

            @pl.when(s == last)
            def _():
                for ss in range(last):
                    for j in range(nj):
                        ici(j, ss).wait_recv()
                        ici(j, ss).wait_send()
                    small_send(ss).wait_recv()
                    small_send(ss).wait_send()
                stage = [pltpu.make_async_copy(landing[j], sbufs[j], l_sem.at[j]) for j in range(nj)]
                stage.append(pltpu.make_async_copy(landing[nj], smx, l_sem.at[nj]))
                for cp in stage:
                    cp.start()
                for j in range(nj):
                    stage[j].wait()
                    total = pair_ref(j)[...]
                    for ss in range(last):
                        total = total + sbufs[j][ss].astype(F32)
                    pair_ref(j)[...] = total
                stage[nj].wait()
                by_xor = {xo: smx[ss] for ss, xo in enumerate(XOR_ORDER)}
                sm[c] = (sm[c] + by_xor[1]) + (by_xor[2] + by_xor[3])
                done = [(acc_w.at[1], outs[0].at[mine_w, :], outs[0].at[theirs_w, :])]
                done += [(pair_ref(1 + q), outs[1 + q].at[mine_q, :], outs[1 + q].at[theirs_q, :]) for q in range(nsq)]
                done.append((sm.at[c], outs[nj].at[c], outs[nj].at[1 - c]))
                copies = []
                for j, (src, mine, theirs) in enumerate(done):
                    keep = pltpu.make_async_copy(src, mine, o_sem.at[j])
                    give = _remote(src, mine, f_send, f_recv, j, sib)
                    take = _remote(src, theirs, f_send, f_recv, j, sib)
                    keep.start()
                    give.start()
                    copies.append((keep, give, take))
                for keep, give, take in copies:
                    keep.wait()
                    give.wait_send()
                    take.wait_recv()

        pl.when(p == 0)(sq_phase)
        for hf in range(2):
            pl.when(p == 1 + hf)(functools.partial(w_phase, hf))

    nj = 1 + nsq
    scratch = [pltpu.VMEM((2, hr, W_BLK), F32), pltpu.VMEM((2, hr, W_BLK), F32), pltpu.VMEM((last, hr, W_BLK), BF16)]
    for _ in range(nsq):
        scratch += [pltpu.VMEM((ROW_BLK, d), F32), pltpu.VMEM((2, qr, d), F32), pltpu.VMEM((last, qr, d), BF16)]
    scratch += [pltpu.VMEM((2, PK_HALF, LANES), F32), pltpu.VMEM((last, PK_HALF, LANES), F32)]
    scratch += [pltpu.SemaphoreType.DMA((2 * nj + 1,)), pltpu.SemaphoreType.DMA((2 * nj + 1,)),
                pltpu.SemaphoreType.DMA((last * (nj + 1),)), pltpu.SemaphoreType.DMA((last * (nj + 1),)),
                pltpu.SemaphoreType.DMA((nj + 1,)), pltpu.SemaphoreType.DMA((nj + 1,)),
                pltpu.SemaphoreType.DMA((nj + 1,)), pltpu.SemaphoreType.DMA((nj + 2,))]
    grid_spec = pltpu.PrefetchScalarGridSpec(
        num_scalar_prefetch=1, grid=(N_CHIPS, n_phase, nk), in_specs=in_specs + [ANY],
        out_specs=[ANY] * (2 * nj + 2), scratch_shapes=scratch)
    out_shape = [jax.ShapeDtypeStruct((d, W_BLK), F32)] + [jax.ShapeDtypeStruct((ROW_BLK, d), F32)] * nsq
    out_shape.append(jax.ShapeDtypeStruct((2, PK_HALF, LANES), F32))
    out_shape += [jax.ShapeDtypeStruct((last, hr, W_BLK), BF16)] + [jax.ShapeDtypeStruct((last, qr, d), BF16)] * nsq
    out_shape.append(jax.ShapeDtypeStruct((last, PK_HALF, LANES), F32))
    flat = [a for pair in squares for a in pair]
    res = pl.pallas_call(body, name="grads_reduce_scatter", grid_spec=grid_spec, out_shape=out_shape,
                         compiler_params=_params())(idx, h, dproj, *flat, small)
    return res[:nj + 1]


_VEC_NAMES = ("norm_g", "conv_b", "lru_b_a", "lru_b_x", "lru_lambda", "final_norm_g")


def _pack_small(p, conv_full=None, scalar=None):
    rows = [p["lru_w_a"].reshape(PK_WX - PK_WA, LANES), p["lru_w_x"].reshape(PK_VEC - PK_WX, LANES)]
    rows += [p[k].reshape(8, LANES) for k in _VEC_NAMES]
    rows.append(jnp.pad(p["attn_sinks"].reshape(1, N_Q_HEADS), ((0, 7), (0, LANES - N_Q_HEADS))))
    rows.append(jnp.zeros((32, LANES), F32) if conv_full is None else conv_full.reshape(32, LANES))
    tail = PK_ROWS - PK_SCALAR
    if scalar is None:
        rows.append(jnp.zeros((tail, LANES), F32))
    else:
        rows.append(jnp.pad(scalar.reshape(1, 1), ((0, tail - 1), (0, LANES - 1))))
    return jnp.concatenate(rows, axis=0)


def _unpack_small(pk, like):
    out = {"lru_w_a": pk[PK_WA:PK_WX].reshape(like["lru_w_a"].shape),
           "lru_w_x": pk[PK_WX:PK_VEC].reshape(like["lru_w_x"].shape)}
    for j, k in enumerate(_VEC_NAMES):
        out[k] = pk[PK_VEC + 8 * j:PK_VEC + 8 * j + 8].reshape(like[k].shape)
    out["attn_sinks"] = pk[PK_SINK:PK_SINK + 1, :N_Q_HEADS].reshape(like["attn_sinks"].shape)
    return out


_WEIGHTS = ("norm_g", "w_in", "conv_w", "conv_b", "lru_w_a", "lru_b_a", "lru_w_x", "lru_b_x", "lru_lambda",
            "attn_sinks", "w_rnn_out", "w_attn_out", "w_o", "final_norm_g")
_ROW_SHARDED = ("w_rnn_out", "w_attn_out", "w_o")


def kernel(x, norm_g, w_in, conv_w, conv_b, lru_w_a, lru_b_a, lru_w_x, lru_b_x, lru_lambda, attn_sinks, w_rnn_out, w_attn_out, w_o, final_norm_g, loss_target, m_norm_g, m_w_in, m_conv_w, m_conv_b, m_lru_w_a, m_lru_b_a, m_lru_w_x, m_lru_b_x, m_lru_lambda, m_attn_sinks, m_w_rnn_out, m_w_attn_out, m_w_o, m_final_norm_g, v_norm_g, v_w_in, v_conv_w, v_conv_b, v_lru_w_a, v_lru_b_a, v_lru_w_x, v_lru_b_x, v_lru_lambda, v_attn_sinks, v_w_rnn_out, v_w_attn_out, v_w_o, v_final_norm_g):
    w = dict(norm_g=norm_g, w_in=w_in, conv_w=conv_w, conv_b=conv_b, lru_w_a=lru_w_a, lru_b_a=lru_b_a, lru_w_x=lru_w_x,
             lru_b_x=lru_b_x, lru_lambda=lru_lambda, attn_sinks=attn_sinks, w_rnn_out=w_rnn_out, w_attn_out=w_attn_out,
             w_o=w_o, final_norm_g=final_norm_g)
    m = dict(norm_g=m_norm_g, w_in=m_w_in, conv_w=m_conv_w, conv_b=m_conv_b, lru_w_a=m_lru_w_a, lru_b_a=m_lru_b_a,
             lru_w_x=m_lru_w_x, lru_b_x=m_lru_b_x, lru_lambda=m_lru_lambda, attn_sinks=m_attn_sinks,
             w_rnn_out=m_w_rnn_out, w_attn_out=m_w_attn_out, w_o=m_w_o, final_norm_g=m_final_norm_g)
    v = dict(norm_g=v_norm_g, w_in=v_w_in, conv_w=v_conv_w, conv_b=v_conv_b, lru_w_a=v_lru_w_a, lru_b_a=v_lru_b_a,
             lru_w_x=v_lru_w_x, lru_b_x=v_lru_b_x, lru_lambda=v_lru_lambda, attn_sinks=v_attn_sinks,
             w_rnn_out=v_w_rnn_out, w_attn_out=v_w_attn_out, w_o=v_w_o, final_norm_g=v_final_norm_g)
    mx, my, mc = _mesh_pos()
    me = 2 * mx + my
    d = D_MODEL

    my_chip = jnp.reshape(me, (1,)).astype(jnp.int32)
    (buf_in,) = _put_slots([w["w_in"][0]], my_chip, BF16, "cast_w_in")
    (buf_cw,) = _put_slots([w["conv_w"][0]], my_chip, F32, "slot_conv_w")
    row_sharded = _put_slots([w[k][0] for k in _ROW_SHARDED], my_chip, BF16, "cast_row_sharded")
    h, proj, (g_in, g_cw) = _gather_in_proj(x.reshape(-1, d), w["norm_g"], [buf_in, buf_cw], [True, False], my_chip)
    conv_full = g_cw.transpose(1, 0, 2).reshape(CONV_WIDTH, D_RNN)

    loss_local, grad_x, h, dproj, squares, gsmall = _local_grads(
        x, loss_target, h, proj, w["norm_g"], g_in, conv_full, w["conv_b"], w["lru_w_a"][0], w["lru_b_a"], w["lru_w_x"][0],
        w["lru_b_x"], w["lru_lambda"], w["attn_sinks"][0], row_sharded, w["final_norm_g"].reshape(1, d))
    gpack = _pack_small(gsmall, gsmall["conv_w"], loss_local).reshape(2, PK_HALF, LANES)
    f_in, f_r, f_a, f_o, spack = _grads_reduce_scatter(h, dproj, squares, gpack, jnp.stack([me, mc]).astype(jnp.int32))
    spack = spack.reshape(PK_ROWS, LANES)
    loss = spack[PK_SCALAR, 0]

    grads = _unpack_small(spack, w)
    conv_all = spack[PK_CONV:PK_CONV + 32].reshape(CONV_WIDTH, D_RNN)
    grads["conv_w"] = lax.dynamic_slice_in_dim(conv_all, me * (D_RNN // N_CHIPS), D_RNN // N_CHIPS, axis=1)[None]
    grads["w_in"] = f_in[None]
    grads["w_rnn_out"], grads["w_attn_out"], grads["w_o"] = f_r[None], f_a[None], f_o[None]

    delta, new_m, new_v = {}, {}, {}
    def group(k):
        return w[k][0], grads[k][0], m[k][0], v[k][0]

    for names, call in ((("w_in",), "adamw_w_in"), (_ROW_SHARDED, "adamw_row_sharded")):
        for k, (dk, mk, vk, gk) in zip(names, _adamw([group(k) for k in names], call, echo_grad=True)):
            delta[k], new_m[k], new_v[k], grads[k] = dk[None], mk[None], vk[None], gk[None]
    shp = (2 * CONV_WIDTH, LANES)
    ((dk, mk, vk),) = _adamw([tuple(a.reshape(shp) for a in (w["conv_w"], grads["conv_w"], m["conv_w"], v["conv_w"]))],
                             "adamw_conv_w")
    delta["conv_w"], new_m["conv_w"], new_v["conv_w"] = (a.reshape(w["conv_w"].shape) for a in (dk, mk, vk))
    ((dk, mk, vk),) = _adamw([(_pack_small(w), spack, _pack_small(m), _pack_small(v))], "adamw_small")
    for src, dst in ((dk, delta), (mk, new_m), (vk, new_v)):
        dst.update(_unpack_small(src, w))

    return (loss, grad_x, *[grads[k] for k in _WEIGHTS], *[delta[k] for k in _WEIGHTS],
            *[new_m[k] for k in _WEIGHTS], *[new_v[k] for k in _WEIGHTS])
```

```python
import functools
import math

import jax
import jax.numpy as jnp
from jax import lax
from jax.experimental import pallas as pl
from jax.experimental.pallas import tpu as pltpu

F32 = jnp.float32
BF16 = jnp.bfloat16
MESH = pl.DeviceIdType.MESH

D_MODEL = 1024
D_RNN = 1024
N_RNN_BLOCKS = 8
RNN_BLOCK = D_RNN // N_RNN_BLOCKS
CONV_WIDTH = 4
LRU_C = 8.0
HEAD_DIM = 64
N_Q_HEADS = 16
N_KV_HEADS = 4
D_ATTN = N_Q_HEADS * HEAD_DIM
D_KV = N_KV_HEADS * HEAD_DIM
WINDOW = 128
ROPE_DIM = HEAD_DIM // 4
ROPE_THETA = 500000.0
NORM_EPS = 1e-6
OFF_RNN_X = 0
OFF_RNN_G = OFF_RNN_X + D_RNN
OFF_Q = OFF_RNN_G + D_RNN
OFF_K = OFF_Q + D_ATTN
OFF_V = OFF_K + D_KV
OFF_ATTN_G = OFF_V + D_KV
OFF_MERGE_R = OFF_ATTN_G + D_ATTN
OFF_MERGE_A = OFF_MERGE_R + D_MODEL
D_IN = OFF_MERGE_A + D_MODEL

ADAM_LR = 0.001
ADAM_B1 = 0.9
ADAM_B2 = 0.999
ADAM_EPS = 1e-08
ADAM_WD = 0.01
ADAM_STEP = 10

N_CHIPS = 4
W_BLK = D_IN // N_CHIPS
ROW_BLK = D_MODEL // N_CHIPS
LANES = 128
ATT_BLK = 128
VMEM_LIMIT = 56 * 1024 * 1024
NEG_BIG = -1e30
ATTN_SCALE = 1.0 / math.sqrt(HEAD_DIM)

PK_WA = 0
PK_WX = PK_WA + N_RNN_BLOCKS * RNN_BLOCK
PK_VEC = PK_WX + N_RNN_BLOCKS * RNN_BLOCK
PK_SINK = PK_VEC + 6 * 8
PK_CONV = PK_SINK + 8
PK_SCALAR = PK_CONV + 32
PK_ROWS = PK_SCALAR + 8
PK_HALF = PK_ROWS // 2


def _params(**kw):
    return pltpu.CompilerParams(vmem_limit_bytes=VMEM_LIMIT, **kw)


def _sigmoid(z):
    return 1.0 / (1.0 + jnp.exp(-z))


def _dot(a, b):
    return jnp.dot(a, b, preferred_element_type=F32)


def _dot_nt(a, b):
    return lax.dot_general(a, b, (((1,), (1,)), ((), ())), preferred_element_type=F32)


def _dot_tn(a, b):
    return lax.dot_general(a, b, (((0,), (0,)), ((), ())), preferred_element_type=F32)


def _put_slots(srcs, slot, dtype, name):
    rows, c = srcs[0].shape
    n = len(srcs)
    tr = _row_tile(rows, c * 4)

    def body(idx_ref, *refs):
        for s_ref, o_ref in zip(refs[:n], refs[n:]):
            o_ref[...] = s_ref[...].astype(dtype)

    grid_spec = pltpu.PrefetchScalarGridSpec(
        num_scalar_prefetch=1, grid=(rows // tr,),
        in_specs=[pl.BlockSpec((tr, c), lambda i, idx_ref: (i, 0))] * n,
        out_specs=[pl.BlockSpec((None, tr, c), lambda i, idx_ref: (idx_ref[0], i, 0))] * n)
    return pl.pallas_call(body, name=name, grid_spec=grid_spec,
                          out_shape=[jax.ShapeDtypeStruct((N_CHIPS, rows, c), dtype)] * n,
                          compiler_params=_params())(slot, *srcs)


def _adamw(groups, name, echo_grad=False):
    r, c = groups[0][0].shape
    n = len(groups)
    n_out = 4 if echo_grad else 3
    tr = _row_tile(r, c * 4, 1024 * 1024 // n)
    c1 = 1.0 - ADAM_B1 ** ADAM_STEP
    c2 = 1.0 - ADAM_B2 ** ADAM_STEP

    def body(*refs):
        for q in range(n):
            w_ref, g_ref, m_ref, v_ref = refs[4 * q:4 * q + 4]
            d_ref, nm_ref, nv_ref = refs[4 * n + n_out * q:4 * n + n_out * q + 3]
            gv = g_ref[...]
            if echo_grad:
                refs[4 * n + n_out * q + 3][...] = gv
            nm = ADAM_B1 * m_ref[...] + (1.0 - ADAM_B1) * gv
            nv = ADAM_B2 * v_ref[...] + (1.0 - ADAM_B2) * (gv * gv)
            m_hat = nm / c1
            v_hat = nv / c2
            d_ref[...] = -ADAM_LR * (m_hat / (jnp.sqrt(v_hat) + ADAM_EPS) + ADAM_WD * w_ref[...])
            nm_ref[...] = nm
            nv_ref[...] = nv

    spec = pl.BlockSpec((tr, c), lambda i: (i, 0))
    sds = jax.ShapeDtypeStruct((r, c), F32)
    res = pl.pallas_call(
        body, name=name, grid=(r // tr,), out_shape=[sds] * (n_out * n), in_specs=[spec] * (4 * n),
        out_specs=[spec] * (n_out * n), compiler_params=_params())(*[a for grp in groups for a in grp])
    return [tuple(res[n_out * q:n_out * q + n_out]) for q in range(n)]


def _grad_x(dproj, w_bm, x, dx2, g):
    t = dproj.shape[0]
    nb, d, wb = w_bm.shape
    tm = min(t, 512)
    chunk = min(tm, 256)

    def body(dp_ref, w_ref, x_ref, dx2_ref, g_ref, gx_ref, dg_ref, acc_ref):
        i, k = pl.program_id(0), pl.program_id(1)

        @pl.when(k == 0)
        def _():
            acc_ref[...] = jnp.zeros_like(acc_ref)

        @pl.when(k < nb - 1)
        def _():
            acc_ref[...] += _dot_nt(dp_ref[...], w_ref[...])

        @pl.when((i == 0) & (k == 0))
        def _():
            dg_ref[...] = jnp.zeros_like(dg_ref)

        @pl.when(k == nb - 1)
        def _():
            gv = g_ref[...]
            wv = w_ref[...]
            dg = jnp.zeros((1, d), F32)
            for r0 in range(0, tm, chunk):
                rows = slice(r0, r0 + chunk)
                dhv = acc_ref[rows, :] + _dot_nt(dp_ref[rows, :], wv)
                xv = x_ref[rows, :]
                r = lax.rsqrt(jnp.mean(xv * xv, axis=-1, keepdims=True) + NORM_EPS)
                nrm = xv * r
                dn = dhv * gv
                gx_ref[rows, :] = dx2_ref[rows, :] + r * (dn - nrm * jnp.mean(dn * nrm, axis=-1, keepdims=True))
                dg = dg + jnp.sum(dhv * nrm, axis=0, keepdims=True)
            dg_ref[...] += dg

    tile = pl.BlockSpec((tm, d), lambda i, k: (i, 0))
    vec = pl.BlockSpec((1, d), lambda i, k: (0, 0))
    return pl.pallas_call(
        body, name="grad_x", grid=(t // tm, nb),
        out_shape=(jax.ShapeDtypeStruct((t, d), F32), jax.ShapeDtypeStruct((1, d), F32)),
        in_specs=[pl.BlockSpec((tm, wb), lambda i, k: (i, k)), pl.BlockSpec((None, d, wb), lambda i, k: (k, 0, 0)),
                  tile, tile, vec],
        out_specs=(tile, vec), scratch_shapes=[pltpu.VMEM((tm, d), F32)], compiler_params=_params())(dproj, w_bm, x, dx2, g)


def _shift_down(v, d, fill):
    n = v.shape[0]
    if d % 8 == 0:
        return jnp.concatenate([jnp.full((d,) + v.shape[1:], fill, v.dtype), v[: n - d]], axis=0)
    row = lax.broadcasted_iota(jnp.int32, v.shape, 0)
    return jnp.where(row >= d, pltpu.roll(v, d, axis=0), fill)


def _shift_up(v, d, fill):
    n = v.shape[0]
    if d % 8 == 0:
        return jnp.concatenate([v[d:], jnp.full((d,) + v.shape[1:], fill, v.dtype)], axis=0)
    row = lax.broadcasted_iota(jnp.int32, v.shape, 0)
    return jnp.where(row < n - d, pltpu.roll(v, n - d, axis=0), fill)


def _scan_log(a, b, shift):
    n = a.shape[0]
    d = 1
    while d < n:
        b = a * shift(b, d, 0.0) + b
        if 2 * d < n:
            a = a * shift(a, d, 1.0)
        d *= 2
    return b


SUBLANES = 8


def _scan(a, b, sa_ref, sb_ref, reverse):
    n, c = a.shape
    g = n // SUBLANES
    a3, b3 = a.reshape(g, SUBLANES, c), b.reshape(g, SUBLANES, c)
    sub = lax.broadcasted_iota(jnp.int32, a3.shape, 1)
    d = 1
    while d < SUBLANES:
        keep = (sub < SUBLANES - d) if reverse else (sub >= d)
        amount = SUBLANES - d if reverse else d
        b3 = a3 * jnp.where(keep, pltpu.roll(b3, amount, axis=1), 0.0) + b3
        a3 = a3 * jnp.where(keep, pltpu.roll(a3, amount, axis=1), 1.0)
        d *= 2
    sa_ref[...] = a3.reshape(n, c)
    sb_ref[...] = b3.reshape(n, c)
    edge = 0 if reverse else SUBLANES - 1
    shift = _shift_up if reverse else _shift_down
    totals = _scan_log(sa_ref[pl.ds(edge, g, stride=SUBLANES), :], sb_ref[pl.ds(edge, g, stride=SUBLANES), :], shift)
    carry = shift(totals, 1, 0.0)
    return (a3 * carry[:, None, :] + b3).reshape(n, c)


def _neg_expm1_twice(log_a, a):
    return -jnp.tanh(log_a) * (a * a + 1.0)


def _softplus(z):
    e = jnp.exp(-jnp.abs(z))
    w = 1.0 + e
    log1p = jnp.where(w == 1.0, e, jnp.log(w) * (e / jnp.where(w == 1.0, 1.0, w - 1.0)))
    return jnp.maximum(z, 0.0) + log1p


def _conv(up, cw, cb):
    out = cb + cw[CONV_WIDTH - 1:CONV_WIDTH, :] * up
    for j in range(CONV_WIDTH - 1):
        out = out + cw[j:j + 1, :] * _shift_down(up, CONV_WIDTH - 1 - j, 0.0)
    return out


def _lru_gates(u, wa_ref, ba_ref, wx_ref, bx_ref, lam_ref):
    ub = u.astype(BF16)
    r = _sigmoid(_dot(ub, wa_ref[...].astype(BF16)) + ba_ref[...])
    i = _sigmoid(_dot(ub, wx_ref[...].astype(BF16)) + bx_ref[...])
    sp = _softplus(-lam_ref[...])
    log_a = (-LRU_C) * r * sp
    a = jnp.exp(log_a)
    mult = jnp.sqrt(_neg_expm1_twice(log_a, a))
    return r, i, sp, a, mult


def _lru_specs(s):
    cb = RNN_BLOCK
    vec = pl.BlockSpec((1, cb), lambda n, b: (0, n))
    return dict(
        up=pl.BlockSpec((None, s, cb), lambda n, b: (b, 0, OFF_RNN_X // cb + n)),
        gr=pl.BlockSpec((None, s, cb), lambda n, b: (b, 0, OFF_RNN_G // cb + n)),
        act=pl.BlockSpec((None, s, cb), lambda n, b: (b, 0, n)),
        cw=pl.BlockSpec((CONV_WIDTH, cb), lambda n, b: (0, n)),
        vec=vec,
        wblk=pl.BlockSpec((None, cb, cb), lambda n, b: (n, 0, 0)),
    )


def _lru_fwd(proj3, cw, cb, wa, ba, wx, bx, lam, riders):
    bsz, s, _ = proj3.shape
    sp = _lru_specs(s)
    nr = len(riders)

    def body(up_ref, gr_ref, cw_ref, cb_ref, wa_ref, ba_ref, wx_ref, bx_ref, lam_ref, *refs):
        rider_in, (h_ref, y_ref), rider_out = refs[:nr], refs[nr:nr + 2], refs[nr + 2:2 * nr + 2]
        sa_ref, sb_ref = refs[2 * nr + 2:2 * nr + 4]
        start, pass_on, finish = _row_gather(rider_in, rider_out, *refs[2 * nr + 4:])
        step = pl.program_id(0) * bsz + pl.program_id(1)
        first, last = step == 0, step == N_RNN_BLOCKS * bsz - 1
        pl.when(first)(start)
        pl.when(step == (3 * N_RNN_BLOCKS * bsz) // 4)(pass_on)
        u = _conv(up_ref[...], cw_ref[...], cb_ref[...])
        _, i, _, a, mult = _lru_gates(u, wa_ref, ba_ref, wx_ref, bx_ref, lam_ref)
        h = _scan(a, mult * (i * u), sa_ref, sb_ref, reverse=False)
        h_ref[...] = h
        g = gr_ref[...]
        y_ref[...] = (h * (g * _sigmoid(g))).astype(BF16)
        pl.when(last)(finish)

    res = pl.pallas_call(
        body, name="lru_fwd", grid=(N_RNN_BLOCKS, bsz),
        out_shape=[jax.ShapeDtypeStruct((bsz, s, D_RNN), F32), jax.ShapeDtypeStruct((bsz, s, D_RNN), BF16)] + [
            jax.ShapeDtypeStruct(r.shape, r.dtype) for r in riders],
        in_specs=[sp["up"], sp["gr"], sp["cw"], sp["vec"], sp["wblk"], sp["vec"], sp["wblk"], sp["vec"], sp["vec"]] + [
            ANY] * nr,
        out_specs=[sp["act"], sp["act"]] + [ANY] * nr, input_output_aliases={9 + t: 2 + t for t in range(nr)},
        scratch_shapes=[pltpu.VMEM((s, RNN_BLOCK), F32)] * 2 + [pltpu.SemaphoreType.DMA((3 * nr,))] * 4,
        compiler_params=_params())(proj3, proj3, cw, cb, wa, ba, wx, bx, lam, *riders)
    return res[0], res[1], res[2:]


def _lru_bwd(proj3, h3, dy3, dproj3, cw, cb, wa, ba, wx, bx, lam):
    bsz, s, _ = proj3.shape
    sp = _lru_specs(s)
    n_steps = N_RNN_BLOCKS * bsz

    def body(up_ref, gr_ref, h_ref, dy_ref, cw_ref, cb_ref, wa_ref, ba_ref, wx_ref, bx_ref, lam_ref, dp_in,
             dp_ref, dcw_ref, dcb_ref, dwa_ref, dba_ref, dwx_ref, dbx_ref, dlam_ref, sa_ref, sb_ref,
             dup_st, dgr_st, o_sems):
        del dp_in
        blk = pl.program_id(0)
        b = pl.program_id(1)
        step = blk * bsz + b
        slot = step % 2
        stages = [dup_st, dgr_st]
        dsts = [dp_ref.at[b, :, pl.ds(pl.multiple_of(OFF_RNN_X + blk * RNN_BLOCK, LANES), RNN_BLOCK)],
                dp_ref.at[b, :, pl.ds(pl.multiple_of(OFF_RNN_G + blk * RNN_BLOCK, LANES), RNN_BLOCK)]]
        _staged_reuse(step, stages, dsts, o_sems, slot)
        up = up_ref[...]
        cwv = cw_ref[...]
        u = _conv(up, cwv, cb_ref[...])
        r, i, spv, a, mult = _lru_gates(u, wa_ref, ba_ref, wx_ref, bx_ref, lam_ref)
        h = h_ref[...]
        g = gr_ref[...]
        dy = dy_ref[...]
        sg = _sigmoid(g)
        dgr_st[slot] = (dy * h * (sg * (1.0 + g * (1.0 - sg)))).astype(BF16)
        dh = dy * (g * sg)
        adj = _scan(_shift_up(a, 1, 0.0), dh, sa_ref, sb_ref, reverse=True)
        da = adj * _shift_down(h, 1, 0.0)
        dmult = adj * (i * u)
        di = adj * mult * u
        du = adj * mult * i
        dla = da * a - dmult * (a * a) / mult
        dr = dla * ((-LRU_C) * spv)
        dsp = jnp.sum(dla * ((-LRU_C) * r), axis=0, keepdims=True)
        dza = dr * r * (1.0 - r)
        dzx = di * i * (1.0 - i)
        ub = u.astype(BF16)
        dzab = dza.astype(BF16)
        dzxb = dzx.astype(BF16)
        du = du + _dot_nt(dzab, wa_ref[...].astype(BF16)) + _dot_nt(dzxb, wx_ref[...].astype(BF16))
        dup = cwv[CONV_WIDTH - 1:CONV_WIDTH, :] * du
        for j in range(CONV_WIDTH - 1):
            dup = dup + cwv[j:j + 1, :] * _shift_up(du, CONV_WIDTH - 1 - j, 0.0)
        dup_st[slot] = dup.astype(BF16)
        _staged_flush(step, n_steps, stages, dsts, o_sems, slot)

        @pl.when(b == 0)
        def _():
            for ref in (dcw_ref, dcb_ref, dwa_ref, dba_ref, dwx_ref, dbx_ref, dlam_ref):
                ref[...] = jnp.zeros_like(ref)

        rows = [jnp.sum(du * _shift_down(up, CONV_WIDTH - 1 - j, 0.0), axis=0, keepdims=True)
                for j in range(CONV_WIDTH - 1)]
        rows.append(jnp.sum(du * up, axis=0, keepdims=True))
        dcw_ref[...] += jnp.concatenate(rows, axis=0)
        dcb_ref[...] += jnp.sum(du, axis=0, keepdims=True)
        dwa_ref[...] += _dot_tn(ub, dzab)
        dba_ref[...] += jnp.sum(dza, axis=0, keepdims=True)
        dwx_ref[...] += _dot_tn(ub, dzxb)
        dbx_ref[...] += jnp.sum(dzx, axis=0, keepdims=True)
        dlam_ref[...] += dsp * (-_sigmoid(-lam_ref[...]))

    vec = jax.ShapeDtypeStruct((1, D_RNN), F32)
    wsd = jax.ShapeDtypeStruct((N_RNN_BLOCKS, RNN_BLOCK, RNN_BLOCK), F32)
    return pl.pallas_call(
        body, name="lru_bwd", grid=(N_RNN_BLOCKS, bsz),
        out_shape=(jax.ShapeDtypeStruct(dproj3.shape, dproj3.dtype), jax.ShapeDtypeStruct((CONV_WIDTH, D_RNN), F32),
                   vec, wsd, vec, wsd, vec, vec),
        in_specs=[sp["up"], sp["gr"], sp["act"], sp["act"], sp["cw"], sp["vec"], sp["wblk"], sp["vec"],
                  sp["wblk"], sp["vec"], sp["vec"], ANY],
        out_specs=(ANY, sp["cw"], sp["vec"], sp["wblk"], sp["vec"], sp["wblk"], sp["vec"], sp["vec"]),
        input_output_aliases={11: 0},
        scratch_shapes=[pltpu.VMEM((s, RNN_BLOCK), F32)] * 2 + [pltpu.VMEM((2, s, RNN_BLOCK), BF16)] * 2 + [
            pltpu.SemaphoreType.DMA((4,))],
        compiler_params=_params())(proj3, proj3, h3, dy3, cw, cb, wa, ba, wx, bx, lam, dproj3)


def _rope_tables(s):
    half = ROPE_DIM // 2
    pos = jnp.arange(s, dtype=F32)
    inv_freq = ROPE_THETA ** (-jnp.arange(0, ROPE_DIM, 2, dtype=F32) / ROPE_DIM)
    ang = pos[:, None] * inv_freq[None, :]
    cos, sin = jnp.cos(ang), jnp.sin(ang)
    rest = HEAD_DIM - ROPE_DIM
    cos64 = jnp.concatenate([cos, cos, jnp.ones((s, rest), F32)], axis=1)
    sin64 = jnp.concatenate([-sin, sin, jnp.zeros((s, rest), F32)], axis=1)
    assert half * 2 == ROPE_DIM
    return jnp.tile(cos64, (1, LANES // HEAD_DIM)), jnp.tile(sin64, (1, LANES // HEAD_DIM))


def _swap_rot_halves(v):
    half = ROPE_DIM // 2
    lane = lax.broadcasted_iota(jnp.int32, v.shape, 1) % HEAD_DIM
    second = jnp.where(lane < ROPE_DIM, pltpu.roll(v, half, axis=1), 0.0)
    return jnp.where(lane < half, pltpu.roll(v, LANES - half, axis=1), second)


def _rope(v, cos, sin):
    tiles = []
    for t in range(v.shape[1] // LANES):
        vt = v[:, t * LANES:(t + 1) * LANES]
        tiles.append(vt * cos + _swap_rot_halves(vt) * sin)
    return tiles[0] if len(tiles) == 1 else jnp.concatenate(tiles, axis=1)


def _unrope(v, cos, sin):
    tiles = []
    for t in range(v.shape[1] // LANES):
        vt = v[:, t * LANES:(t + 1) * LANES]
        tiles.append(vt * cos + _swap_rot_halves(vt * sin))
    return tiles[0] if len(tiles) == 1 else jnp.concatenate(tiles, axis=1)


HEADS_PER_STEP = 8
QW = HEADS_PER_STEP * HEAD_DIM
N_PAIRS = N_Q_HEADS // HEADS_PER_STEP
Q_PER_KV = N_Q_HEADS // N_KV_HEADS
KV_PER_STEP = HEADS_PER_STEP // Q_PER_KV


QT_COLS = Q_PER_KV * ATT_BLK


def _attn_saved_shapes(bsz, s):
    nb = s // ATT_BLK
    pad = s + ATT_BLK
    return [(bsz, N_PAIRS, nb, LANES, QT_COLS), (bsz, N_PAIRS, KV_PER_STEP, pad, LANES),
            (bsz, N_PAIRS, KV_PER_STEP, pad, LANES), (bsz, N_PAIRS, LANES, pad)]


def _attn_specs(s, order):
    def mk(width, base, **kw):
        if order == "bp":
            return pl.BlockSpec((None, s, width), lambda b, p: (b, 0, base + p), **kw)
        return pl.BlockSpec((None, s, width), lambda p, b: (b, 0, base + p), **kw)

    def saved(shape, **kw):
        blk = (None, None) + shape[2:]
        zeros = (0,) * (len(shape) - 2)
        if order == "bp":
            return pl.BlockSpec(blk, lambda b, p: (b, p) + zeros, **kw)
        return pl.BlockSpec(blk, lambda p, b: (b, p) + zeros, **kw)

    one = dict(pipeline_mode=pl.Buffered(1))
    tbl = pl.BlockSpec((s, LANES), lambda *_: (0, 0))
    shapes = _attn_saved_shapes(1, s)
    return dict(q=mk(QW, OFF_Q // QW), k=mk(LANES, OFF_K // LANES), v=mk(LANES, OFF_V // LANES),
                g=mk(QW, OFF_ATTN_G // QW), act=mk(QW, 0), kv=mk(LANES, 0), tbl=tbl,
                g1=mk(QW, OFF_ATTN_G // QW, **one), act1=mk(QW, 0, **one),
                saved=[saved(sh) for sh in shapes], saved1=[saved(sh, **one) for sh in shapes],
                smem=pl.BlockSpec(memory_space=pltpu.SMEM))


def _to_qt(blk):
    rows = []
    for j in range(KV_PER_STEP):
        cols = []
        for tt in range(2):
            t = 2 * j + tt
            tr = blk[:, t * LANES:(t + 1) * LANES].T
            cols += [tr[0:HEAD_DIM, :], tr[HEAD_DIM:, :]]
        rows.append(jnp.concatenate(cols, axis=1))
    return jnp.concatenate(rows, axis=0)


def _from_qt(xt):
    tiles = []
    for j in range(KV_PER_STEP):
        for tt in range(2):
            g0 = 2 * tt
            pair = jnp.concatenate([xt[j * HEAD_DIM:(j + 1) * HEAD_DIM, (g0 + i) * ATT_BLK:(g0 + i + 1) * ATT_BLK]
                                    for i in range(2)], axis=0)
            tiles.append(pair.T)
    return jnp.concatenate(tiles, axis=1)


def _attn_prep(q_ref, k_ref, v_ref, cos_ref, sin_ref, qt_ref, km_ref, vm_ref, kt_ref, vt_ref, nb):
    zeros = jnp.zeros((ATT_BLK, LANES), BF16)
    for j in range(KV_PER_STEP):
        km_ref[j, 0:ATT_BLK, :] = zeros
        vm_ref[j, 0:ATT_BLK, :] = zeros
    kt_ref[:, 0:ATT_BLK] = zeros
    vt_ref[:, 0:ATT_BLK] = zeros
    head_of_lane = lax.broadcasted_iota(jnp.int32, (ATT_BLK, LANES), 1) // HEAD_DIM

    def prep(n, carry):
        r0 = pl.multiple_of(n * ATT_BLK, ATT_BLK)
        cs = cos_ref[pl.ds(r0, ATT_BLK), :]
        sn = sin_ref[pl.ds(r0, ATT_BLK), :]
        qt_ref[n] = _to_qt(_rope(q_ref[pl.ds(r0, ATT_BLK), :], cs, sn) * ATTN_SCALE).astype(BF16)
        k = _rope(k_ref[pl.ds(r0, ATT_BLK), :], cs, sn)
        v = v_ref[pl.ds(r0, ATT_BLK), :]
        for j in range(KV_PER_STEP):
            km_ref[j, pl.ds(r0 + ATT_BLK, ATT_BLK), :] = jnp.where(head_of_lane == j, k, 0.0).astype(BF16)
            vm_ref[j, pl.ds(r0 + ATT_BLK, ATT_BLK), :] = jnp.where(head_of_lane == j, v, 0.0).astype(BF16)
        kt_ref[:, pl.ds(r0 + ATT_BLK, ATT_BLK)] = k.T.astype(BF16)
        vt_ref[:, pl.ds(r0 + ATT_BLK, ATT_BLK)] = v.T.astype(BF16)
        return carry

    lax.fori_loop(0, nb, prep, 0, unroll=4)


assert WINDOW == ATT_BLK


def _from_prev_block():
    key = lax.broadcasted_iota(jnp.int32, (ATT_BLK, QT_COLS), 0)
    qry = lax.broadcasted_iota(jnp.int32, (ATT_BLK, QT_COLS), 1) % ATT_BLK
    return key > qry


def _fold(tile, prev, prev_bias=None):
    top = tile[:ATT_BLK] if prev_bias is None else tile[:ATT_BLK] + prev_bias
    return jnp.where(prev, top, tile[ATT_BLK:])


def _unfold(folded, prev):
    zero = jnp.zeros_like(folded)
    return jnp.concatenate([jnp.where(prev, folded, zero), jnp.where(prev, zero, folded)], axis=0).astype(BF16)


def _no_prev_bias(n):
    return jnp.where(n == 0, NEG_BIG, 0.0).astype(F32)


def _sink_row(sink_ref, first):
    return jnp.concatenate([jnp.full((1, ATT_BLK), sink_ref[first + g], F32) for g in range(Q_PER_KV)], axis=1)


def _softmax_cols(sc, sink):
    m = jnp.maximum(jnp.max(sc, axis=0, keepdims=True), sink)
    e = jnp.exp(sc - m)
    es = jnp.exp(sink - m)
    inv = 1.0 / (jnp.sum(e, axis=0, keepdims=True) + es)
    return e * inv, es * inv


def _attn_fwd(proj3, sinks, cosf, sinf):
    bsz, s, _ = proj3.shape
    nb = s // ATT_BLK
    sp = _attn_specs(s, "bp")

    def body(sink_ref, q_ref, k_ref, v_ref, g_ref, cos_ref, sin_ref, o_ref, y_ref, qt_sc, km_sc, vm_sc, kt_ref, vt_sc):
        p = pl.program_id(1)
        _attn_prep(q_ref, k_ref, v_ref, cos_ref, sin_ref, qt_sc, km_sc, vm_sc, kt_ref, vt_sc, nb)
        kv_row = lax.broadcasted_iota(jnp.int32, (LANES, QT_COLS), 0) // HEAD_DIM
        prev = _from_prev_block()

        def blk(n, carry):
            r0 = pl.multiple_of(n * ATT_BLK, ATT_BLK)
            bias = _no_prev_bias(n)
            rq = qt_sc[n]
            vt = vt_sc[:, pl.ds(r0, 2 * ATT_BLK)]
            ots = []
            for j in range(KV_PER_STEP):
                st = _dot(km_sc[j, pl.ds(r0, 2 * ATT_BLK), :], rq)
                pc, _ = _softmax_cols(_fold(st, prev, bias), _sink_row(sink_ref, p * HEADS_PER_STEP + j * Q_PER_KV))
                ots.append(_dot(vt, _unfold(pc, prev)))
            o = _from_qt(jnp.where(kv_row == 0, ots[0], ots[1]))
            o_ref[pl.ds(r0, ATT_BLK), :] = o
            g = g_ref[pl.ds(r0, ATT_BLK), :]
            y_ref[pl.ds(r0, ATT_BLK), :] = (o * (g * _sigmoid(g))).astype(BF16)
            return carry

        lax.fori_loop(0, nb, blk, 0, unroll=4)

    res = pl.pallas_call(
        body, name="attn_fwd", grid=(bsz, N_PAIRS),
        out_shape=[jax.ShapeDtypeStruct((bsz, s, D_ATTN), F32), jax.ShapeDtypeStruct((bsz, s, D_ATTN), BF16)] + [
            jax.ShapeDtypeStruct(sh, BF16) for sh in _attn_saved_shapes(bsz, s)],
        in_specs=[sp["smem"], sp["q"], sp["k"], sp["v"], sp["g"], sp["tbl"], sp["tbl"]],
        out_specs=[sp["act"], sp["act"]] + sp["saved"],
        scratch_shapes=[pltpu.VMEM((LANES, s + ATT_BLK), BF16)],
        compiler_params=_params())(sinks, proj3, proj3, proj3, proj3, cosf, sinf)
    return res[0], res[1], res[2:]


def _attn_bwd(proj3, saved, o3, dy3, dproj3, sinks, cosf, sinf):
    bsz, s, _ = proj3.shape
    nb = s // ATT_BLK
    assert nb % 2 == 0
    sp = _attn_specs(s, "pb")
    n_steps = N_PAIRS * bsz

    def body(sink_ref, qt_sc, km_sc, vm_sc, kt_sc, g_ref, o_ref, dy_ref, cos_ref, sin_ref, dp_in,
             dp_ref, ds_ref, dot_sc, dqt_sc, dk_sc, dv_sc, dq_st, dk_st, dv_st, dg_st, o_sems):
        del dp_in
        p = pl.program_id(0)
        b = pl.program_id(1)
        step = p * bsz + b
        slot = step % 2
        stages = [dq_st, dk_st, dv_st, dg_st]
        dsts = [dp_ref.at[b, :, pl.ds(pl.multiple_of(OFF_Q + p * QW, LANES), QW)],
                dp_ref.at[b, :, pl.ds(pl.multiple_of(OFF_K + p * LANES, LANES), LANES)],
                dp_ref.at[b, :, pl.ds(pl.multiple_of(OFF_V + p * LANES, LANES), LANES)],
                dp_ref.at[b, :, pl.ds(pl.multiple_of(OFF_ATTN_G + p * QW, LANES), QW)]]
        _staged_reuse(step, stages, dsts, o_sems, slot)
        dk_sc[...] = jnp.zeros_like(dk_sc)
        dv_sc[...] = jnp.zeros_like(dv_sc)

        def gate(n, carry):
            r0 = pl.multiple_of(n * ATT_BLK, ATT_BLK)
            g = g_ref[pl.ds(r0, ATT_BLK), :]
            dy = dy_ref[pl.ds(r0, ATT_BLK), :]
            sg = _sigmoid(g)
            dg_st[slot, pl.ds(r0, ATT_BLK), :] = (dy * o_ref[pl.ds(r0, ATT_BLK), :] * (sg * (1.0 + g * (1.0 - sg)))).astype(BF16)
            dot_sc[n] = _to_qt(dy * (g * sg)).astype(BF16)
            return carry

        lax.fori_loop(0, nb, gate, 0, unroll=4)
        kv_lane = lax.broadcasted_iota(jnp.int32, (2 * ATT_BLK, LANES), 1) // HEAD_DIM
        kv_row = lax.broadcasted_iota(jnp.int32, (LANES, QT_COLS), 0) // HEAD_DIM
        prev = _from_prev_block()

        def blk(n, acc):
            r0 = pl.multiple_of(n * ATT_BLK, ATT_BLK)
            bias = _no_prev_bias(n)
            rq = qt_sc[n]
            rd = dot_sc[n]
            kt = kt_sc[:, pl.ds(r0, 2 * ATT_BLK)]
            dvs, dks, dqs, new_acc = [], [], [], []
            for j in range(KV_PER_STEP):
                st = _dot(km_sc[j, pl.ds(r0, 2 * ATT_BLK), :], rq)
                pc, ps = _softmax_cols(_fold(st, prev, bias), _sink_row(sink_ref, p * HEADS_PER_STEP + j * Q_PER_KV))
                dpc = _fold(_dot(vm_sc[j, pl.ds(r0, 2 * ATT_BLK), :], rd), prev)
                delta = jnp.sum(pc * dpc, axis=0, keepdims=True)
                dst = _unfold(pc * (dpc - delta), prev)
                new_acc.append(acc[j] + ps * delta)
                dvs.append(_dot_nt(_unfold(pc, prev), rd))
                dks.append(_dot_nt(dst, rq))
                dqs.append(_dot(kt, dst))
            dv_sc[pl.ds(r0, 2 * ATT_BLK), :] += jnp.where(kv_lane == 0, dvs[0], dvs[1])
            dk_sc[pl.ds(r0, 2 * ATT_BLK), :] += jnp.where(kv_lane == 0, dks[0], dks[1])
            dqt_sc[n] = jnp.where(kv_row == 0, dqs[0], dqs[1]) * ATTN_SCALE
            return tuple(new_acc)

        per_trip = 4 if nb % 4 == 0 else 2

        def blk_group(m, acc):
            for u in range(per_trip):
                acc = blk(per_trip * m + u, acc)
            return acc

        acc = lax.fori_loop(0, nb // per_trip, blk_group, tuple(jnp.zeros((1, QT_COLS), F32) for _ in range(KV_PER_STEP)))
        lane1 = lax.broadcasted_iota(jnp.int32, (1, LANES), 1)
        dsink = jnp.zeros((1, LANES), F32)
        for j in range(KV_PER_STEP):
            for i in range(Q_PER_KV):
                part = jnp.sum(acc[j][:, i * ATT_BLK:(i + 1) * ATT_BLK], axis=1, keepdims=True)
                dsink = dsink - jnp.where(lane1 == j * Q_PER_KV + i, part, 0.0)

        @pl.when(b == 0)
        def _():
            ds_ref[...] = jnp.zeros_like(ds_ref)

        ds_ref[...] += dsink

        def post(n, carry):
            r0 = pl.multiple_of(n * ATT_BLK, ATT_BLK)
            cs = cos_ref[pl.ds(r0, ATT_BLK), :]
            sn = sin_ref[pl.ds(r0, ATT_BLK), :]
            dq_st[slot, pl.ds(r0, ATT_BLK), :] = _unrope(_from_qt(dqt_sc[n]), cs, sn).astype(BF16)
            dk_st[slot, pl.ds(r0, ATT_BLK), :] = _unrope(dk_sc[pl.ds(r0 + ATT_BLK, ATT_BLK), :], cs, sn).astype(BF16)
            dv_st[slot, pl.ds(r0, ATT_BLK), :] = dv_sc[pl.ds(r0 + ATT_BLK, ATT_BLK), :].astype(BF16)
            return carry

        lax.fori_loop(0, nb, post, 0, unroll=2)
        _staged_flush(step, n_steps, stages, dsts, o_sems, slot)

    n_in = 1 + len(saved) + 5
    return pl.pallas_call(
        body, name="attn_bwd", grid=(N_PAIRS, bsz),
        out_shape=(jax.ShapeDtypeStruct(dproj3.shape, dproj3.dtype), jax.ShapeDtypeStruct((N_PAIRS, 1, LANES), F32)),
        in_specs=[sp["smem"]] + sp["saved1"] + [sp["g1"], sp["act1"], sp["act1"], sp["tbl"], sp["tbl"], ANY],
        out_specs=(ANY, pl.BlockSpec((None, 1, LANES), lambda p, b: (p, 0, 0))),
        input_output_aliases={n_in: 0},
        scratch_shapes=[pltpu.VMEM((nb, LANES, QT_COLS), BF16),
                        pltpu.VMEM((nb, LANES, QT_COLS), F32),
                        pltpu.VMEM((s + ATT_BLK, LANES), F32),
                        pltpu.VMEM((s + ATT_BLK, LANES), F32),
                        pltpu.VMEM((2, s, QW), BF16), pltpu.VMEM((2, s, LANES), BF16),
                        pltpu.VMEM((2, s, LANES), BF16), pltpu.VMEM((2, s, QW), BF16),
                        pltpu.SemaphoreType.DMA((8,))],
        compiler_params=_params())(sinks, *saved, proj3, o3, dy3, cosf, sinf, dproj3)


def _staged_copies(stages, dsts, sems, slot):
    return [pltpu.make_async_copy(st.at[slot], dst, sems.at[slot * len(stages) + t])
            for t, (st, dst) in enumerate(zip(stages, dsts))]


def _staged_reuse(step, stages, dsts, sems, slot):
    @pl.when(step >= 2)
    def _():
        for cp in _staged_copies(stages, dsts, sems, slot):
            cp.wait()


def _staged_flush(step, n_steps, stages, dsts, sems, slot):
    for cp in _staged_copies(stages, dsts, sems, slot):
        cp.start()

    @pl.when(step == n_steps - 1)
    def _():
        for cp in _staged_copies(stages, dsts, sems, slot):
            cp.wait()
        if n_steps >= 2:
            for cp in _staged_copies(stages, dsts, sems, 1 - slot):
                cp.wait()


def _merge_fwd_bwd(x, tgt, y_rnn, y_attn, proj, w_r, w_a, w_o, gf):
    t, d = x.shape
    tm = min(t, 256)
    nt = t // tm

    hw = d // 2

    def body(x_ref, t_ref, yr_ref, ya_ref, mr0_ref, mr1_ref, ma0_ref, ma1_ref, wr_ref, wa_ref, wo_ref, gf_ref,
             dp_ref, dyr_ref, dya_ref, mg_ref, dx2_ref, dx2b_ref, dpr_ref, dpa_ref, loss_ref, dgf_ref, dmg_st, o_sems):
        i = pl.program_id(0)
        slot = i % 2
        dsts = [dp_ref.at[pl.ds(pl.multiple_of(i * tm, tm), tm), pl.ds(OFF_MERGE_R, 2 * d)]]
        _staged_reuse(i, [dmg_st], dsts, o_sems, slot)
        wr = wr_ref[...]
        wa = wa_ref[...]
        wo = wo_ref[...]
        gfv = gf_ref[...]
        pr = _dot(yr_ref[...], wr)
        pa = _dot(ya_ref[...], wa)
        sr = _sigmoid(jnp.concatenate([mr0_ref[...], mr1_ref[...]], axis=1))
        sa = _sigmoid(jnp.concatenate([ma0_ref[...], ma1_ref[...]], axis=1))
        mb = (sr * pr + sa * pa).astype(BF16)
        mg_ref[...] = mb
        x2 = x_ref[...] + _dot(mb, wo)
        r2 = lax.rsqrt(jnp.mean(x2 * x2, axis=-1, keepdims=True) + NORM_EPS)
        nrm = x2 * r2
        err = nrm * gfv - t_ref[...]
        dy = err * (1.0 / d)
        dn = dy * gfv
        dx2 = r2 * (dn - nrm * jnp.mean(dn * nrm, axis=-1, keepdims=True))
        dx2_ref[...] = dx2
        dx2b = dx2.astype(BF16)
        dx2b_ref[...] = dx2b
        dmerged = _dot_nt(dx2b, wo)
        dpr = (dmerged * sr).astype(BF16)
        dpa = (dmerged * sa).astype(BF16)
        dpr_ref[...] = dpr
        dpa_ref[...] = dpa
        dmg_st[slot, :, 0:d] = (dmerged * pr * (sr * (1.0 - sr))).astype(BF16)
        dmg_st[slot, :, d:2 * d] = (dmerged * pa * (sa * (1.0 - sa))).astype(BF16)
        _staged_flush(i, nt, [dmg_st], dsts, o_sems, slot)
        dyr_ref[...] = _dot_nt(dpr, wr)
        dya_ref[...] = _dot_nt(dpa, wa)

        @pl.when(i == 0)
        def _():
            loss_ref[...] = jnp.zeros_like(loss_ref)
            dgf_ref[...] = jnp.zeros_like(dgf_ref)

        loss_ref[...] += jnp.full((1, LANES), 0.5 / d, F32) * jnp.sum(err * err)
        dgf_ref[...] += jnp.sum(dy * nrm, axis=0, keepdims=True)

    tile = pl.BlockSpec((tm, d), lambda i: (i, 0))
    wsp = pl.BlockSpec((d, d), lambda i: (0, 0))

    def gate(col_blk):
        return pl.BlockSpec((tm, hw), lambda i: (i, col_blk))

    fb = jax.ShapeDtypeStruct((t, d), BF16)
    ff = jax.ShapeDtypeStruct((t, d), F32)
    return pl.pallas_call(
        body, name="merge_fwd_bwd", grid=(nt,),
        out_shape=(jax.ShapeDtypeStruct((t, D_IN), BF16), ff, ff, fb, ff, fb, fb, fb,
                   jax.ShapeDtypeStruct((1, LANES), F32), jax.ShapeDtypeStruct((1, d), F32)),
        in_specs=[tile, tile, tile, tile] + [gate(OFF_MERGE_R // hw + j) for j in range(4)] + [
            wsp, wsp, wsp, pl.BlockSpec((1, d), lambda i: (0, 0))],
        out_specs=(ANY, tile, tile, tile, tile, tile, tile, tile,
                   pl.BlockSpec((1, LANES), lambda i: (0, 0)), pl.BlockSpec((1, d), lambda i: (0, 0))),
        scratch_shapes=[pltpu.VMEM((2, tm, 2 * d), BF16), pltpu.SemaphoreType.DMA((2,))],
        compiler_params=_params())(x, tgt, y_rnn, y_attn, proj, proj, proj, proj, w_r, w_a, w_o, gf)


def _local_grads(x, tgt, h, proj, norm_g, w_in_bm, conv_w, conv_b, lru_w_a, lru_b_a, lru_w_x, lru_b_x, lam, sinks,
                 row_sharded, gf):
    bsz, s, d = x.shape
    t = bsz * s
    x2 = x.reshape(t, d)
    proj3 = proj.reshape(bsz, s, D_IN)
    h_lru, y_rnn, gathered = _lru_fwd(proj3, conv_w, conv_b, lru_w_a, lru_b_a, lru_w_x, lru_b_x, lam, row_sharded)
    w_r, w_a, w_o = (g.reshape(d, d) for g in gathered)
    cosf, sinf = _rope_tables(s)
    o_attn, y_attn, attn_saved = _attn_fwd(proj3, sinks, cosf, sinf)
    y_rnn2 = y_rnn.reshape(t, d)
    y_attn2 = y_attn.reshape(t, d)
    dproj, dyr, dya, merged, dx2, dx2b, dpr, dpa, loss, dgf = _merge_fwd_bwd(
        x2, tgt.reshape(t, d), y_rnn2, y_attn2, proj, w_r, w_a, w_o, gf)
    dproj3, dsink = _attn_bwd(proj3, attn_saved, o_attn, dya.reshape(bsz, s, d), dproj.reshape(bsz, s, D_IN),
                              sinks, cosf, sinf)
    dproj3, dcw, dcb, dwa, dba, dwx, dbx, dlam = _lru_bwd(
        proj3, h_lru, dyr.reshape(bsz, s, d), dproj3, conv_w, conv_b, lru_w_a, lru_b_a, lru_w_x, lru_b_x, lam)
    dproj = dproj3.reshape(t, D_IN)
    grad_x, dng = _grad_x(dproj, w_in_bm, x2, dx2, norm_g)
    small = dict(norm_g=dng, conv_w=dcw, conv_b=dcb, lru_w_a=dwa, lru_b_a=dba, lru_w_x=dwx, lru_b_x=dbx,
                 lru_lambda=dlam, attn_sinks=dsink[:, 0, :HEADS_PER_STEP].reshape(1, N_Q_HEADS), final_norm_g=dgf)
    squares = [(y_rnn2, dpr), (y_attn2, dpa), (merged, dx2b)]
    return loss[0, 0], grad_x.reshape(bsz, s, d), h, dproj, squares, small


ANY = pl.BlockSpec(memory_space=pl.ANY)


def _mesh_pos():
    return lax.axis_index("x"), lax.axis_index("y"), lax.axis_index("c")


def _remote(src, dst, send_sems, recv_sems, idx, peer):
    return pltpu.make_async_remote_copy(src_ref=src, dst_ref=dst, send_sem=send_sems.at[idx],
                                        recv_sem=recv_sems.at[idx], device_id=peer, device_id_type=MESH)


def _row_gather(ins, outs, send_sems, recv_sems, fsend_sems, frecv_sems):
    n = len(ins)
    x, y, c = _mesh_pos()
    me = 2 * x + y
    sib = (x, y, 1 - c)
    peers = [((x, 1 - y, c), me ^ 1), ((1 - x, y, c), me ^ 2), ((1 - x, 1 - y, c), me ^ 3)]

    def half(ref, slot, t, which):
        hr = ins[t].shape[1] // 2
        return ref.at[slot, pl.ds(pl.multiple_of(which * hr, 8), hr), :]

    def ici(t, k):
        peer, pj = peers[k]
        src = half(ins[t], me, t, c)
        return (_remote(src, half(outs[t], me, t, c), send_sems, recv_sems, 3 * t + k, peer),
                _remote(src, half(outs[t], pj, t, c), send_sems, recv_sems, 3 * t + k, peer))

    def forward(t, k):
        got = half(outs[t], peers[k][1], t, c)
        return (_remote(got, got, fsend_sems, frecv_sems, 3 * t + k, sib),
                _remote(got, half(outs[t], peers[k][1], t, 1 - c), fsend_sems, frecv_sems, 3 * t + k, sib))

    pairs = [(t, k) for t in range(n) for k in range(3)]

    def start():
        for t, k in pairs:
            ici(t, k)[0].start()

    def pass_on():
        for t, k in pairs:
            ici(t, k)[1].wait_recv()
            forward(t, k)[0].start()

    def finish():
        for t, k in pairs:
            ici(t, k)[0].wait_send()
            forward(t, k)[0].wait_send()
            forward(t, k)[1].wait_recv()

    return start, pass_on, finish


def _gather_in_proj(x, g, bufs, split, idx):
    t_tok, d = x.shape
    n = len(bufs)
    tm = min(t_tok, 1024)
    nt = t_tok // tm
    n_fwd = 3 * sum(split)
    assert split[0]

    def body(idx_ref, x_ref, g_ref, *refs):
        ins, proj_ref, h_out, outs = refs[:n], refs[n], refs[n + 1], refs[n + 2:2 * n + 2]
        wbuf, h_all, send_sems, recv_sems, fsend_sems, frecv_sems, l_sems = refs[2 * n + 2:]
        j, i = pl.program_id(0), pl.program_id(1)
        rows = pl.ds(pl.multiple_of(i * tm, tm), tm)
        x, y, c = _mesh_pos()
        me = 2 * x + y
        sib = (x, y, 1 - c)
        peers = [((x, 1 - y, c), me ^ 1), ((1 - x, y, c), me ^ 2), ((1 - x, 1 - y, c), me ^ 3)]

        def part(ref, slot, t, half):
            if not split[t]:
                return ref.at[slot]
            hr = bufs[t].shape[1] // 2
            return ref.at[slot, pl.ds(pl.multiple_of(half * hr, 8), hr), :]

        def land(t):
            return wbuf if t == 0 else outs[t]

        def ici(t, k):
            peer, pj = peers[k]
            src = part(ins[t], me, t, c)
            return (_remote(src, part(land(t), me, t, c), send_sems, recv_sems, 3 * t + k, peer),
                    _remote(src, part(land(t), pj, t, c), send_sems, recv_sems, 3 * t + k, peer))

        fwd_index = {}
        for t in range(n):
            if split[t]:
                for k in range(3):
                    fwd_index[(t, k)] = len(fwd_index)

        def forward(t, k):
            pj = peers[k][1]
            got = part(land(t), pj, t, c)
            f = fwd_index[(t, k)]
            return (_remote(got, got, fsend_sems, frecv_sems, f, sib),
                    _remote(got, part(land(t), pj, t, 1 - c), fsend_sems, frecv_sems, f, sib))

        def write_back(k):
            pj = peers[k][1]
            return pltpu.make_async_copy(wbuf.at[pj], outs[0].at[pj], l_sems.at[1 + k])

        relay_peer = ((x + c) % 2, (y + 1 - c) % 2, c)

        def relay():
            got = part(wbuf, me ^ (2 - c), 0, c)
            return (_remote(got, got, send_sems, recv_sems, 2, relay_peer),
                    _remote(got, part(wbuf, me ^ 3, 0, c), send_sems, recv_sems, 2, relay_peer))

        direct = [(t, k) for t in range(n) for k in range(3) if (t, k) != (0, 2)]

        @pl.when((j == 0) & (i == 0))
        def _():
            for t, k in direct:
                ici(t, k)[0].start()
            own = pltpu.make_async_copy(ins[0].at[me], wbuf.at[me], l_sems.at[0])
            own.start()
            own.wait()

        @pl.when((j == 1) & (i == 0))
        def _():
            pltpu.make_async_copy(h_all, h_out, l_sems.at[4]).start()
            for k in range(2):
                ici(0, k)[1].wait_recv()
            relay()[0].start()
            for k in range(2):
                forward(0, k)[0].start()
            forward(0, 0)[1].wait_recv()
            write_back(0).start()

        @pl.when((j == 2) & (i == 0))
        def _():
            forward(0, 1)[1].wait_recv()
            write_back(1).start()

        @pl.when((j == 3) & (i == 0))
        def _():
            relay()[1].wait_recv()
            forward(0, 2)[0].start()
            forward(0, 2)[1].wait_recv()
            write_back(2).start()

        @pl.when(j == 0)
        def _():
            xv = x_ref[...]
            r = lax.rsqrt(jnp.mean(xv * xv, axis=-1, keepdims=True) + NORM_EPS)
            h_all[rows, :] = (xv * r * g_ref[...]).astype(BF16)

        proj_ref[...] = _dot(h_all[rows, :], wbuf[me ^ j])

        @pl.when((j == N_CHIPS - 1) & (i == nt - 1))
        def _():
            pltpu.make_async_copy(h_all, h_out, l_sems.at[4]).wait()
            for t in range(1, n):
                for k in range(3):
                    ici(t, k)[1].wait_recv()
                    if split[t]:
                        forward(t, k)[0].start()
            relay()[0].wait_send()
            for t, k in direct:
                ici(t, k)[0].wait_send()
            for t in range(n):
                if split[t]:
                    for k in range(3):
                        forward(t, k)[0].wait_send()
                        if t > 0:
                            forward(t, k)[1].wait_recv()
            for k in range(3):
                write_back(k).wait()

    grid_spec = pltpu.PrefetchScalarGridSpec(
        num_scalar_prefetch=1, grid=(N_CHIPS, nt),
        in_specs=[pl.BlockSpec((tm, d), lambda j, i, idx_ref: (jnp.where(j == 0, i, nt - 1), 0)),
                  pl.BlockSpec((1, d), lambda j, i, idx_ref: (0, 0))] + [ANY] * n,
        out_specs=[pl.BlockSpec((tm, W_BLK), lambda j, i, idx_ref: (i, idx_ref[0] ^ j)), ANY] + [ANY] * n,
        scratch_shapes=[pltpu.VMEM(bufs[0].shape, bufs[0].dtype), pltpu.VMEM((t_tok, d), BF16),
                        pltpu.SemaphoreType.DMA((3 * n,)), pltpu.SemaphoreType.DMA((3 * n,)),
                        pltpu.SemaphoreType.DMA((n_fwd,)), pltpu.SemaphoreType.DMA((n_fwd,)),
                        pltpu.SemaphoreType.DMA((5,))])
    out_shape = [jax.ShapeDtypeStruct((t_tok, D_IN), F32), jax.ShapeDtypeStruct((t_tok, d), BF16)] + [
        jax.ShapeDtypeStruct(a.shape, a.dtype) for a in bufs]
    res = pl.pallas_call(
        body, name="gather_in_proj", grid_spec=grid_spec, out_shape=out_shape,
        input_output_aliases={3 + t: 2 + t for t in range(n)}, compiler_params=_params())(idx, x, g, *bufs)
    return res[1], res[0], res[2:]


def _row_tile(rows, row_bytes, cap_bytes=2 * 1024 * 1024):
    best = None
    for tr in range(8, rows + 1, 8):
        if rows % tr == 0 and tr * row_bytes <= cap_bytes:
            best = tr
    return best if best is not None else rows


XOR_ORDER = (3, 2, 1)


def _grads_reduce_scatter(h, dproj, squares, small, idx):
    t, d = h.shape
    nsq = len(squares)
    hr = d // 2
    qr = ROW_BLK // 2
    tk = min(t, 1024)
    nk = t // tk
    last = N_CHIPS - 1
    n_phase = 3

    def dest(s, idx_ref):
        xo = jnp.where(s == 0, XOR_ORDER[0], jnp.where(s == 1, XOR_ORDER[1], jnp.where(s == 2, XOR_ORDER[2], 0)))
        return idx_ref[0] ^ xo

    def k_sq(p, k):
        return jnp.where(p == 0, k, nk - 1)

    def k_w(p, k):
        return jnp.where(p == 0, 0, k)

    in_specs = [
        pl.BlockSpec((tk, hr), lambda s, p, k, idx_ref: (k_w(p, k), (1 - idx_ref[1] + jnp.maximum(p - 1, 0)) % 2)),
        pl.BlockSpec((tk, W_BLK), lambda s, p, k, idx_ref: (k_w(p, k), dest(s, idx_ref)))]
    for q in range(nsq):
        in_specs.append(pl.BlockSpec((tk, ROW_BLK), lambda s, p, k, idx_ref: (k_sq(p, k), dest(s, idx_ref))))
        in_specs.append(pl.BlockSpec((tk, d), lambda s, p, k, idx_ref: (k_sq(p, k), 0)))

    def body(idx_ref, *refs):
        nj = 1 + nsq
        h_ref, dp_ref = refs[0], refs[1]
        sq_in = refs[2:2 + 2 * nsq]
        small_in = refs[2 * nj]
        outs = refs[2 * nj + 1:3 * nj + 2]
        landing = refs[3 * nj + 2:4 * nj + 3]
        sc = refs[4 * nj + 3:]
        acc_w, xr_w, sb_w = sc[0:3]
        sq_sc = [sc[3 + 3 * q:6 + 3 * q] for q in range(nsq)]
        sm, smx = sc[3 * nj:3 * nj + 2]
        x_send, x_recv, i_send, i_recv, f_send, f_recv, o_sem, l_sem = sc[3 * nj + 2:]
        s, p, k = pl.program_id(0), pl.program_id(1), pl.program_id(2)
        x, y, c = _mesh_pos()
        sib = (x, y, 1 - c)
        peers = [((1 - x) if xo & 2 else x, (1 - y) if xo & 1 else y, c) for xo in XOR_ORDER]
        slot = s % 2
        mine_w = pl.ds(pl.multiple_of(c * hr, 8), hr)
        theirs_w = pl.ds(pl.multiple_of((1 - c) * hr, 8), hr)
        mine_q = pl.ds(pl.multiple_of(c * qr, 8), qr)
        theirs_q = pl.ds(pl.multiple_of((1 - c) * qr, 8), qr)

        def exch(j, src, dst):
            return _remote(src, dst, x_send, x_recv, 2 * j + slot, sib)

        sbufs = [sb_w] + [sq_sc[q][2] for q in range(nsq)]

        def ici(j, ss):
            return _remote(sbufs[j].at[ss], landing[j].at[ss], i_send, i_recv, last * j + ss, peers[ss])

        def exchanges():
            cps = [exch(0, acc_w.at[0], xr_w.at[slot])]
            cps += [exch(1 + q, sq_sc[q][0].at[theirs_q, :], sq_sc[q][1].at[slot]) for q in range(nsq)]
            return cps

        def small_send(ss):
            return _remote(sm.at[c], landing[nj].at[ss], i_send, i_recv, last * nj + ss, peers[ss])

        def small_start():
            load = pltpu.make_async_copy(small_in, sm, l_sem.at[nj + 1])
            load.start()
            load.wait()
            swap = _remote(sm, smx.at[pl.ds(0, 2)], x_send, x_recv, 2 * nj, sib)
            swap.start()
            swap.wait_recv()
            swap.wait_send()
            sm[...] = sm[...] + smx[0:2]
            for ss in range(last):
                small_send(ss).start()

        def pair_ref(j):
            return acc_w.at[1] if j == 0 else sq_sc[j - 1][0].at[mine_q, :]

        def sq_phase():
            pl.when((s == 0) & (k == 0))(small_start)
            for q in range(nsq):
                acc = sq_sc[q][0]

                @pl.when(k == 0)
                def _():
                    acc[...] = jnp.zeros((ROW_BLK, d), F32)

                acc[...] += _dot_tn(sq_in[2 * q][...], sq_in[2 * q + 1][...])

            @pl.when(k == nk - 1)
            def _():
                for cp in exchanges()[1:]:
                    cp.start()

        def w_phase(hf):
            prod = _dot_tn(h_ref[...], dp_ref[...])

            @pl.when(k == 0)
            def _():
                acc_w[hf] = prod

            @pl.when(k > 0)
            def _():
                acc_w[hf] += prod

            @pl.when(k == nk - 1)
            def _():
                if hf == 0:
                    exchanges()[0].start()
                else:
                    finish_step()

        def finish_step():
            for cp in exchanges():
                cp.wait_recv()
                cp.wait_send()
            acc_w[1] += xr_w[slot]
            for q in range(nsq):
                sq_sc[q][0][mine_q, :] += sq_sc[q][1][slot]
            for ss in range(last):
                @pl.when(s == ss)
                def _():
                    for j in range(nj):
                        sbufs[j][ss] = pair_ref(j)[...].astype(BF16)
                        ici(j, ss).start()

            @pl.when(s == last)
            def _():
                for ss in range(last):
                    for j in range(nj):
                        ici(j, ss).wait_recv()
                        ici(j, ss).wait_send()
                    small_send(ss).wait_recv()
                    small_send(ss).wait_send()
                stage = [pltpu.make_async_copy(landing[j], sbufs[j], l_sem.at[j]) for j in range(nj)]
                stage.append(pltpu.make_async_copy(landing[nj], smx, l_sem.at[nj]))
                for cp in stage:
                    cp.start()
                for j in range(nj):
                    stage[j].wait()
                    total = pair_ref(j)[...]
                    for ss in range(last):
                        total = total + sbufs[j][ss].astype(F32)
                    pair_ref(j)[...] = total
                stage[nj].wait()
                by_xor = {xo: smx[ss] for ss, xo in enumerate(XOR_ORDER)}
                sm[c] = (sm[c] + by_xor[1]) + (by_xor[2] + by_xor[3])
                done = [(acc_w.at[1], outs[0].at[mine_w, :], outs[0].at[theirs_w, :])]
                done += [(pair_ref(1 + q), outs[1 + q].at[mine_q, :], outs[1 + q].at[theirs_q, :]) for q in range(nsq)]
                done.append((sm.at[c], outs[nj].at[c], outs[nj].at[1 - c]))
                copies = []
                for j, (src, mine, theirs) in enumerate(done):
                    keep = pltpu.make_async_copy(src, mine, o_sem.at[j])
                    give = _remote(src, mine, f_send, f_recv, j, sib)
                    take = _remote(src, theirs, f_send, f_recv, j, sib)
                    keep.start()
                    give.start()
                    copies.append((keep, give, take))
                for keep, give, take in copies:
                    keep.wait()
                    give.wait_send()
                    take.wait_recv()

        pl.when(p == 0)(sq_phase)
        for hf in range(2):
            pl.when(p == 1 + hf)(functools.partial(w_phase, hf))

    nj = 1 + nsq
    scratch = [pltpu.VMEM((2, hr, W_BLK), F32), pltpu.VMEM((2, hr, W_BLK), F32), pltpu.VMEM((last, hr, W_BLK), BF16)]
    for _ in range(nsq):
        scratch += [pltpu.VMEM((ROW_BLK, d), F32), pltpu.VMEM((2, qr, d), F32), pltpu.VMEM((last, qr, d), BF16)]
    scratch += [pltpu.VMEM((2, PK_HALF, LANES), F32), pltpu.VMEM((last, PK_HALF, LANES), F32)]
    scratch += [pltpu.SemaphoreType.DMA((2 * nj + 1,)), pltpu.SemaphoreType.DMA((2 * nj + 1,)),
                pltpu.SemaphoreType.DMA((last * (nj + 1),)), pltpu.SemaphoreType.DMA((last * (nj + 1),)),
                pltpu.SemaphoreType.DMA((nj + 1,)), pltpu.SemaphoreType.DMA((nj + 1,)),
                pltpu.SemaphoreType.DMA((nj + 1,)), pltpu.SemaphoreType.DMA((nj + 2,))]
    grid_spec = pltpu.PrefetchScalarGridSpec(
        num_scalar_prefetch=1, grid=(N_CHIPS, n_phase, nk), in_specs=in_specs + [ANY],
        out_specs=[ANY] * (2 * nj + 2), scratch_shapes=scratch)
    out_shape = [jax.ShapeDtypeStruct((d, W_BLK), F32)] + [jax.ShapeDtypeStruct((ROW_BLK, d), F32)] * nsq
    out_shape.append(jax.ShapeDtypeStruct((2, PK_HALF, LANES), F32))
    out_shape += [jax.ShapeDtypeStruct((last, hr, W_BLK), BF16)] + [jax.ShapeDtypeStruct((last, qr, d), BF16)] * nsq
    out_shape.append(jax.ShapeDtypeStruct((last, PK_HALF, LANES), F32))
    flat = [a for pair in squares for a in pair]
    res = pl.pallas_call(body, name="grads_reduce_scatter", grid_spec=grid_spec, out_shape=out_shape,
                         compiler_params=_params())(idx, h, dproj, *flat, small)
    return res[:nj + 1]


_VEC_NAMES = ("norm_g", "conv_b", "lru_b_a", "lru_b_x", "lru_lambda", "final_norm_g")


def _pack_small(p, conv_full=None, scalar=None):
    rows = [p["lru_w_a"].reshape(PK_WX - PK_WA, LANES), p["lru_w_x"].reshape(PK_VEC - PK_WX, LANES)]
    rows += [p[k].reshape(8, LANES) for k in _VEC_NAMES]
    rows.append(jnp.pad(p["attn_sinks"].reshape(1, N_Q_HEADS), ((0, 7), (0, LANES - N_Q_HEADS))))
    rows.append(jnp.zeros((32, LANES), F32) if conv_full is None else conv_full.reshape(32, LANES))
    tail = PK_ROWS - PK_SCALAR
    if scalar is None:
        rows.append(jnp.zeros((tail, LANES), F32))
    else:
        rows.append(jnp.pad(scalar.reshape(1, 1), ((0, tail - 1), (0, LANES - 1))))
    return jnp.concatenate(rows, axis=0)


def _unpack_small(pk, like):
    out = {"lru_w_a": pk[PK_WA:PK_WX].reshape(like["lru_w_a"].shape),
           "lru_w_x": pk[PK_WX:PK_VEC].reshape(like["lru_w_x"].shape)}
    for j, k in enumerate(_VEC_NAMES):
        out[k] = pk[PK_VEC + 8 * j:PK_VEC + 8 * j + 8].reshape(like[k].shape)
    out["attn_sinks"] = pk[PK_SINK:PK_SINK + 1, :N_Q_HEADS].reshape(like["attn_sinks"].shape)
    return out


_WEIGHTS = ("norm_g", "w_in", "conv_w", "conv_b", "lru_w_a", "lru_b_a", "lru_w_x", "lru_b_x", "lru_lambda",
            "attn_sinks", "w_rnn_out", "w_attn_out", "w_o", "final_norm_g")
_ROW_SHARDED = ("w_rnn_out", "w_attn_out", "w_o")


def kernel(x, norm_g, w_in, conv_w, conv_b, lru_w_a, lru_b_a, lru_w_x, lru_b_x, lru_lambda, attn_sinks, w_rnn_out, w_attn_out, w_o, final_norm_g, loss_target, m_norm_g, m_w_in, m_conv_w, m_conv_b, m_lru_w_a, m_lru_b_a, m_lru_w_x, m_lru_b_x, m_lru_lambda, m_attn_sinks, m_w_rnn_out, m_w_attn_out, m_w_o, m_final_norm_g, v_norm_g, v_w_in, v_conv_w, v_conv_b, v_lru_w_a, v_lru_b_a, v_lru_w_x, v_lru_b_x, v_lru_lambda, v_attn_sinks, v_w_rnn_out, v_w_attn_out, v_w_o, v_final_norm_g):
    w = dict(norm_g=norm_g, w_in=w_in, conv_w=conv_w, conv_b=conv_b, lru_w_a=lru_w_a, lru_b_a=lru_b_a, lru_w_x=lru_w_x,
             lru_b_x=lru_b_x, lru_lambda=lru_lambda, attn_sinks=attn_sinks, w_rnn_out=w_rnn_out, w_attn_out=w_attn_out,
             w_o=w_o, final_norm_g=final_norm_g)
    m = dict(norm_g=m_norm_g, w_in=m_w_in, conv_w=m_conv_w, conv_b=m_conv_b, lru_w_a=m_lru_w_a, lru_b_a=m_lru_b_a,
             lru_w_x=m_lru_w_x, lru_b_x=m_lru_b_x, lru_lambda=m_lru_lambda, attn_sinks=m_attn_sinks,
             w_rnn_out=m_w_rnn_out, w_attn_out=m_w_attn_out, w_o=m_w_o, final_norm_g=m_final_norm_g)
    v = dict(norm_g=v_norm_g, w_in=v_w_in, conv_w=v_conv_w, conv_b=v_conv_b, lru_w_a=v_lru_w_a, lru_b_a=v_lru_b_a,
             lru_w_x=v_lru_w_x, lru_b_x=v_lru_b_x, lru_lambda=v_lru_lambda, attn_sinks=v_attn_sinks,
             w_rnn_out=v_w_rnn_out, w_attn_out=v_w_attn_out, w_o=v_w_o, final_norm_g=v_final_norm_g)
    mx, my, mc = _mesh_pos()
    me = 2 * mx + my
    d = D_MODEL

    my_chip = jnp.reshape(me, (1,)).astype(jnp.int32)
    (buf_in,) = _put_slots([w["w_in"][0]], my_chip, BF16, "cast_w_in")
    (buf_cw,) = _put_slots([w["conv_w"][0]], my_chip, F32, "slot_conv_w")
    row_sharded = _put_slots([w[k][0] for k in _ROW_SHARDED], my_chip, BF16, "cast_row_sharded")
    h, proj, (g_in, g_cw) = _gather_in_proj(x.reshape(-1, d), w["norm_g"], [buf_in, buf_cw], [True, False], my_chip)
    conv_full = g_cw.transpose(1, 0, 2).reshape(CONV_WIDTH, D_RNN)

    loss_local, grad_x, h, dproj, squares, gsmall = _local_grads(
        x, loss_target, h, proj, w["norm_g"], g_in, conv_full, w["conv_b"], w["lru_w_a"][0], w["lru_b_a"], w["lru_w_x"][0],
        w["lru_b_x"], w["lru_lambda"], w["attn_sinks"][0], row_sharded, w["final_norm_g"].reshape(1, d))
    gpack = _pack_small(gsmall, gsmall["conv_w"], loss_local).reshape(2, PK_HALF, LANES)
    f_in, f_r, f_a, f_o, spack = _grads_reduce_scatter(h, dproj, squares, gpack, jnp.stack([me, mc]).astype(jnp.int32))
    spack = spack.reshape(PK_ROWS, LANES)
    loss = spack[PK_SCALAR, 0]

    grads = _unpack_small(spack, w)
    conv_all = spack[PK_CONV:PK_CONV + 32].reshape(CONV_WIDTH, D_RNN)
    grads["conv_w"] = lax.dynamic_slice_in_dim(conv_all, me * (D_RNN // N_CHIPS), D_RNN // N_CHIPS, axis=1)[None]
    grads["w_in"] = f_in[None]
    grads["w_rnn_out"], grads["w_attn_out"], grads["w_o"] = f_r[None], f_a[None], f_o[None]

    delta, new_m, new_v = {}, {}, {}
    def group(k):
        return w[k][0], grads[k][0], m[k][0], v[k][0]

    for names, call in ((("w_in",), "adamw_w_in"), (_ROW_SHARDED, "adamw_row_sharded")):
        for k, (dk, mk, vk, gk) in zip(names, _adamw([group(k) for k in names], call, echo_grad=True)):
            delta[k], new_m[k], new_v[k], grads[k] = dk[None], mk[None], vk[None], gk[None]
    shp = (2 * CONV_WIDTH, LANES)
    ((dk, mk, vk),) = _adamw([tuple(a.reshape(shp) for a in (w["conv_w"], grads["conv_w"], m["conv_w"], v["conv_w"]))],
                             "adamw_conv_w")
    delta["conv_w"], new_m["conv_w"], new_v["conv_w"] = (a.reshape(w["conv_w"].shape) for a in (dk, mk, vk))
    ((dk, mk, vk),) = _adamw([(_pack_small(w), spack, _pack_small(m), _pack_small(v))], "adamw_small")
    for src, dst in ((dk, delta), (mk, new_m), (vk, new_v)):
        dst.update(_unpack_small(src, w))

    return (loss, grad_x, *[grads[k] for k in _WEIGHTS], *[delta[k] for k in _WEIGHTS],
            *[new_m[k] for k in _WEIGHTS], *[new_v[k] for k in _WEIGHTS])
```

```python
import functools
import math

import jax
import jax.numpy as jnp
from jax import lax
from jax.experimental import pallas as pl
from jax.experimental.pallas import tpu as pltpu

F32 = jnp.float32
BF16 = jnp.bfloat16
MESH = pl.DeviceIdType.MESH

D_MODEL = 1024
D_RNN = 1024
N_RNN_BLOCKS = 8
RNN_BLOCK = D_RNN // N_RNN_BLOCKS
CONV_WIDTH = 4
LRU_C = 8.0
HEAD_DIM = 64
N_Q_HEADS = 16
N_KV_HEADS = 4
D_ATTN = N_Q_HEADS * HEAD_DIM
D_KV = N_KV_HEADS * HEAD_DIM
WINDOW = 128
ROPE_DIM = HEAD_DIM // 4
ROPE_THETA = 500000.0
NORM_EPS = 1e-6
OFF_RNN_X = 0
OFF_RNN_G = OFF_RNN_X + D_RNN
OFF_Q = OFF_RNN_G + D_RNN
OFF_K = OFF_Q + D_ATTN
OFF_V = OFF_K + D_KV
OFF_ATTN_G = OFF_V + D_KV
OFF_MERGE_R = OFF_ATTN_G + D_ATTN
OFF_MERGE_A = OFF_MERGE_R + D_MODEL
D_IN = OFF_MERGE_A + D_MODEL

ADAM_LR = 0.001
ADAM_B1 = 0.9
ADAM_B2 = 0.999
ADAM_EPS = 1e-08
ADAM_WD = 0.01
ADAM_STEP = 10

N_CHIPS = 4
W_BLK = D_IN // N_CHIPS
ROW_BLK = D_MODEL // N_CHIPS
LANES = 128
ATT_BLK = 128
VMEM_LIMIT = 56 * 1024 * 1024
NEG_BIG = -1e30
ATTN_SCALE = 1.0 / math.sqrt(HEAD_DIM)

PK_WA = 0
PK_WX = PK_WA + N_RNN_BLOCKS * RNN_BLOCK
PK_VEC = PK_WX + N_RNN_BLOCKS * RNN_BLOCK
PK_SINK = PK_VEC + 6 * 8
PK_CONV = PK_SINK + 8
PK_SCALAR = PK_CONV + 32
PK_ROWS = PK_SCALAR + 8
PK_HALF = PK_ROWS // 2


def _params(**kw):
    return pltpu.CompilerParams(vmem_limit_bytes=VMEM_LIMIT, **kw)


def _sigmoid(z):
    return 1.0 / (1.0 + jnp.exp(-z))


def _dot(a, b):
    return jnp.dot(a, b, preferred_element_type=F32)


def _dot_nt(a, b):
    return lax.dot_general(a, b, (((1,), (1,)), ((), ())), preferred_element_type=F32)


def _dot_tn(a, b):
    return lax.dot_general(a, b, (((0,), (0,)), ((), ())), preferred_element_type=F32)


def _put_slots(srcs, slot, dtype, name):
    rows, c = srcs[0].shape
    n = len(srcs)
    tr = _row_tile(rows, c * 4)

    def body(idx_ref, *refs):
        for s_ref, o_ref in zip(refs[:n], refs[n:]):
            o_ref[...] = s_ref[...].astype(dtype)

    grid_spec = pltpu.PrefetchScalarGridSpec(
        num_scalar_prefetch=1, grid=(rows // tr,),
        in_specs=[pl.BlockSpec((tr, c), lambda i, idx_ref: (i, 0))] * n,
        out_specs=[pl.BlockSpec((None, tr, c), lambda i, idx_ref: (idx_ref[0], i, 0))] * n)
    return pl.pallas_call(body, name=name, grid_spec=grid_spec,
                          out_shape=[jax.ShapeDtypeStruct((N_CHIPS, rows, c), dtype)] * n,
                          compiler_params=_params())(slot, *srcs)


def _adamw(groups, name, echo_grad=False):
    r, c = groups[0][0].shape
    n = len(groups)
    n_out = 4 if echo_grad else 3
    tr = _row_tile(r, c * 4, 1024 * 1024 // n)
    c1 = 1.0 - ADAM_B1 ** ADAM_STEP
    c2 = 1.0 - ADAM_B2 ** ADAM_STEP

    def body(*refs):
        for q in range(n):
            w_ref, g_ref, m_ref, v_ref = refs[4 * q:4 * q + 4]
            d_ref, nm_ref, nv_ref = refs[4 * n + n_out * q:4 * n + n_out * q + 3]
            gv = g_ref[...]
            if echo_grad:
                refs[4 * n + n_out * q + 3][...] = gv
            nm = ADAM_B1 * m_ref[...] + (1.0 - ADAM_B1) * gv
            nv = ADAM_B2 * v_ref[...] + (1.0 - ADAM_B2) * (gv * gv)
            m_hat = nm / c1
            v_hat = nv / c2
            d_ref[...] = -ADAM_LR * (m_hat / (jnp.sqrt(v_hat) + ADAM_EPS) + ADAM_WD * w_ref[...])
            nm_ref[...] = nm
            nv_ref[...] = nv

    spec = pl.BlockSpec((tr, c), lambda i: (i, 0))
    sds = jax.ShapeDtypeStruct((r, c), F32)
    res = pl.pallas_call(
        body, name=name, grid=(r // tr,), out_shape=[sds] * (n_out * n), in_specs=[spec] * (4 * n),
        out_specs=[spec] * (n_out * n), compiler_params=_params())(*[a for grp in groups for a in grp])
    return [tuple(res[n_out * q:n_out * q + n_out]) for q in range(n)]


def _grad_x(dproj, w_bm, x, dx2, g):
    t = dproj.shape[0]
    nb, d, wb = w_bm.shape
    tm = min(t, 512)
    chunk = min(tm, 256)

    def body(dp_ref, w_ref, x_ref, dx2_ref, g_ref, gx_ref, dg_ref, acc_ref):
        i, k = pl.program_id(0), pl.program_id(1)

        @pl.when(k == 0)
        def _():
            acc_ref[...] = jnp.zeros_like(acc_ref)

        @pl.when(k < nb - 1)
        def _():
            acc_ref[...] += _dot_nt(dp_ref[...], w_ref[...])

        @pl.when((i == 0) & (k == 0))
        def _():
            dg_ref[...] = jnp.zeros_like(dg_ref)

        @pl.when(k == nb - 1)
        def _():
            gv = g_ref[...]
            wv = w_ref[...]
            dg = jnp.zeros((1, d), F32)
            for r0 in range(0, tm, chunk):
                rows = slice(r0, r0 + chunk)
                dhv = acc_ref[rows, :] + _dot_nt(dp_ref[rows, :], wv)
                xv = x_ref[rows, :]
                r = lax.rsqrt(jnp.mean(xv * xv, axis=-1, keepdims=True) + NORM_EPS)
                nrm = xv * r
                dn = dhv * gv
                gx_ref[rows, :] = dx2_ref[rows, :] + r * (dn - nrm * jnp.mean(dn * nrm, axis=-1, keepdims=True))
                dg = dg + jnp.sum(dhv * nrm, axis=0, keepdims=True)
            dg_ref[...] += dg

    tile = pl.BlockSpec((tm, d), lambda i, k: (i, 0))
    vec = pl.BlockSpec((1, d), lambda i, k: (0, 0))
    return pl.pallas_call(
        body, name="grad_x", grid=(t // tm, nb),
        out_shape=(jax.ShapeDtypeStruct((t, d), F32), jax.ShapeDtypeStruct((1, d), F32)),
        in_specs=[pl.BlockSpec((tm, wb), lambda i, k: (i, k)), pl.BlockSpec((None, d, wb), lambda i, k: (k, 0, 0)),
                  tile, tile, vec],
        out_specs=(tile, vec), scratch_shapes=[pltpu.VMEM((tm, d), F32)], compiler_params=_params())(dproj, w_bm, x, dx2, g)


def _shift_down(v, d, fill):
    n = v.shape[0]
    if d % 8 == 0:
        return jnp.concatenate([jnp.full((d,) + v.shape[1:], fill, v.dtype), v[: n - d]], axis=0)
    row = lax.broadcasted_iota(jnp.int32, v.shape, 0)
    return jnp.where(row >= d, pltpu.roll(v, d, axis=0), fill)


def _shift_up(v, d, fill):
    n = v.shape[0]
    if d % 8 == 0:
        return jnp.concatenate([v[d:], jnp.full((d,) + v.shape[1:], fill, v.dtype)], axis=0)
    row = lax.broadcasted_iota(jnp.int32, v.shape, 0)
    return jnp.where(row < n - d, pltpu.roll(v, n - d, axis=0), fill)


def _scan_log(a, b, shift):
    n = a.shape[0]
    d = 1
    while d < n:
        b = a * shift(b, d, 0.0) + b
        if 2 * d < n:
            a = a * shift(a, d, 1.0)
        d *= 2
    return b


SUBLANES = 8


def _scan(a, b, sa_ref, sb_ref, reverse):
    n, c = a.shape
    g = n // SUBLANES
    a3, b3 = a.reshape(g, SUBLANES, c), b.reshape(g, SUBLANES, c)
    sub = lax.broadcasted_iota(jnp.int32, a3.shape, 1)
    d = 1
    while d < SUBLANES:
        keep = (sub < SUBLANES - d) if reverse else (sub >= d)
        amount = SUBLANES - d if reverse else d
        b3 = a3 * jnp.where(keep, pltpu.roll(b3, amount, axis=1), 0.0) + b3
        a3 = a3 * jnp.where(keep, pltpu.roll(a3, amount, axis=1), 1.0)
        d *= 2
    sa_ref[...] = a3.reshape(n, c)
    sb_ref[...] = b3.reshape(n, c)
    edge = 0 if reverse else SUBLANES - 1
    shift = _shift_up if reverse else _shift_down
    totals = _scan_log(sa_ref[pl.ds(edge, g, stride=SUBLANES), :], sb_ref[pl.ds(edge, g, stride=SUBLANES), :], shift)
    carry = shift(totals, 1, 0.0)
    return (a3 * carry[:, None, :] + b3).reshape(n, c)


def _neg_expm1_twice(log_a, a):
    return -jnp.tanh(log_a) * (a * a + 1.0)


def _softplus(z):
    e = jnp.exp(-jnp.abs(z))
    w = 1.0 + e
    log1p = jnp.where(w == 1.0, e, jnp.log(w) * (e / jnp.where(w == 1.0, 1.0, w - 1.0)))
    return jnp.maximum(z, 0.0) + log1p


def _conv(up, cw, cb):
    out = cb + cw[CONV_WIDTH - 1:CONV_WIDTH, :] * up
    for j in range(CONV_WIDTH - 1):
        out = out + cw[j:j + 1, :] * _shift_down(up, CONV_WIDTH - 1 - j, 0.0)
    return out


def _lru_gates(u, wa_ref, ba_ref, wx_ref, bx_ref, lam_ref):
    ub = u.astype(BF16)
    r = _sigmoid(_dot(ub, wa_ref[...].astype(BF16)) + ba_ref[...])
    i = _sigmoid(_dot(ub, wx_ref[...].astype(BF16)) + bx_ref[...])
    sp = _softplus(-lam_ref[...])
    log_a = (-LRU_C) * r * sp
    a = jnp.exp(log_a)
    mult = jnp.sqrt(_neg_expm1_twice(log_a, a))
    return r, i, sp, a, mult


def _lru_specs(s):
    cb = RNN_BLOCK
    vec = pl.BlockSpec((1, cb), lambda n, b: (0, n))
    return dict(
        up=pl.BlockSpec((None, s, cb), lambda n, b: (b, 0, OFF_RNN_X // cb + n)),
        gr=pl.BlockSpec((None, s, cb), lambda n, b: (b, 0, OFF_RNN_G // cb + n)),
        act=pl.BlockSpec((None, s, cb), lambda n, b: (b, 0, n)),
        cw=pl.BlockSpec((CONV_WIDTH, cb), lambda n, b: (0, n)),
        vec=vec,
        wblk=pl.BlockSpec((None, cb, cb), lambda n, b: (n, 0, 0)),
    )


def _lru_fwd(proj3, cw, cb, wa, ba, wx, bx, lam, riders):
    bsz, s, _ = proj3.shape
    sp = _lru_specs(s)
    nr = len(riders)

    def body(up_ref, gr_ref, cw_ref, cb_ref, wa_ref, ba_ref, wx_ref, bx_ref, lam_ref, *refs):
        rider_in, (h_ref, y_ref), rider_out = refs[:nr], refs[nr:nr + 2], refs[nr + 2:2 * nr + 2]
        sa_ref, sb_ref = refs[2 * nr + 2:2 * nr + 4]
        start, pass_on, finish = _row_gather(rider_in, rider_out, *refs[2 * nr + 4:])
        step = pl.program_id(0) * bsz + pl.program_id(1)
        first, last = step == 0, step == N_RNN_BLOCKS * bsz - 1
        pl.when(first)(start)
        pl.when(step == (3 * N_RNN_BLOCKS * bsz) // 4)(pass_on)
        u = _conv(up_ref[...], cw_ref[...], cb_ref[...])
        _, i, _, a, mult = _lru_gates(u, wa_ref, ba_ref, wx_ref, bx_ref, lam_ref)
        h = _scan(a, mult * (i * u), sa_ref, sb_ref, reverse=False)
        h_ref[...] = h
        g = gr_ref[...]
        y_ref[...] = (h * (g * _sigmoid(g))).astype(BF16)
        pl.when(last)(finish)

    res = pl.pallas_call(
        body, name="lru_fwd", grid=(N_RNN_BLOCKS, bsz),
        out_shape=[jax.ShapeDtypeStruct((bsz, s, D_RNN), F32), jax.ShapeDtypeStruct((bsz, s, D_RNN), BF16)] + [
            jax.ShapeDtypeStruct(r.shape, r.dtype) for r in riders],
        in_specs=[sp["up"], sp["gr"], sp["cw"], sp["vec"], sp["wblk"], sp["vec"], sp["wblk"], sp["vec"], sp["vec"]] + [
            ANY] * nr,
        out_specs=[sp["act"], sp["act"]] + [ANY] * nr, input_output_aliases={9 + t: 2 + t for t in range(nr)},
        scratch_shapes=[pltpu.VMEM((s, RNN_BLOCK), F32)] * 2 + [pltpu.SemaphoreType.DMA((3 * nr,))] * 4,
        compiler_params=_params())(proj3, proj3, cw, cb, wa, ba, wx, bx, lam, *riders)
    return res[0], res[1], res[2:]


def _lru_bwd(proj3, h3, dy3, dproj3, cw, cb, wa, ba, wx, bx, lam):
    bsz, s, _ = proj3.shape
    sp = _lru_specs(s)
    n_steps = N_RNN_BLOCKS * bsz

    def body(up_ref, gr_ref, h_ref, dy_ref, cw_ref, cb_ref, wa_ref, ba_ref, wx_ref, bx_ref, lam_ref, dp_in,
             dp_ref, dcw_ref, dcb_ref, dwa_ref, dba_ref, dwx_ref, dbx_ref, dlam_ref, sa_ref, sb_ref,
             dup_st, dgr_st, o_sems):
        del dp_in
        blk = pl.program_id(0)
        b = pl.program_id(1)
        step = blk * bsz + b
        slot = step % 2
        stages = [dup_st, dgr_st]
        dsts = [dp_ref.at[b, :, pl.ds(pl.multiple_of(OFF_RNN_X + blk * RNN_BLOCK, LANES), RNN_BLOCK)],
                dp_ref.at[b, :, pl.ds(pl.multiple_of(OFF_RNN_G + blk * RNN_BLOCK, LANES), RNN_BLOCK)]]
        _staged_reuse(step, stages, dsts, o_sems, slot)
        up = up_ref[...]
        cwv = cw_ref[...]
        u = _conv(up, cwv, cb_ref[...])
        r, i, spv, a, mult = _lru_gates(u, wa_ref, ba_ref, wx_ref, bx_ref, lam_ref)
        h = h_ref[...]
        g = gr_ref[...]
        dy = dy_ref[...]
        sg = _sigmoid(g)
        dgr_st[slot] = (dy * h * (sg * (1.0 + g * (1.0 - sg)))).astype(BF16)
        dh = dy * (g * sg)
        adj = _scan(_shift_up(a, 1, 0.0), dh, sa_ref, sb_ref, reverse=True)
        da = adj * _shift_down(h, 1, 0.0)
        dmult = adj * (i * u)
        di = adj * mult * u
        du = adj * mult * i
        dla = da * a - dmult * (a * a) / mult
        dr = dla * ((-LRU_C) * spv)
        dsp = jnp.sum(dla * ((-LRU_C) * r), axis=0, keepdims=True)
        dza = dr * r * (1.0 - r)
        dzx = di * i * (1.0 - i)
        ub = u.astype(BF16)
        dzab = dza.astype(BF16)
        dzxb = dzx.astype(BF16)
        du = du + _dot_nt(dzab, wa_ref[...].astype(BF16)) + _dot_nt(dzxb, wx_ref[...].astype(BF16))
        dup = cwv[CONV_WIDTH - 1:CONV_WIDTH, :] * du
        for j in range(CONV_WIDTH - 1):
            dup = dup + cwv[j:j + 1, :] * _shift_up(du, CONV_WIDTH - 1 - j, 0.0)
        dup_st[slot] = dup.astype(BF16)
        _staged_flush(step, n_steps, stages, dsts, o_sems, slot)

        @pl.when(b == 0)
        def _():
            for ref in (dcw_ref, dcb_ref, dwa_ref, dba_ref, dwx_ref, dbx_ref, dlam_ref):
                ref[...] = jnp.zeros_like(ref)

        rows = [jnp.sum(du * _shift_down(up, CONV_WIDTH - 1 - j, 0.0), axis=0, keepdims=True)
                for j in range(CONV_WIDTH - 1)]
        rows.append(jnp.sum(du * up, axis=0, keepdims=True))
        dcw_ref[...] += jnp.concatenate(rows, axis=0)
        dcb_ref[...] += jnp.sum(du, axis=0, keepdims=True)
        dwa_ref[...] += _dot_tn(ub, dzab)
        dba_ref[...] += jnp.sum(dza, axis=0, keepdims=True)
        dwx_ref[...] += _dot_tn(ub, dzxb)
        dbx_ref[...] += jnp.sum(dzx, axis=0, keepdims=True)
        dlam_ref[...] += dsp * (-_sigmoid(-lam_ref[...]))

    vec = jax.ShapeDtypeStruct((1, D_RNN), F32)
    wsd = jax.ShapeDtypeStruct((N_RNN_BLOCKS, RNN_BLOCK, RNN_BLOCK), F32)
    return pl.pallas_call(
        body, name="lru_bwd", grid=(N_RNN_BLOCKS, bsz),
        out_shape=(jax.ShapeDtypeStruct(dproj3.shape, dproj3.dtype), jax.ShapeDtypeStruct((CONV_WIDTH, D_RNN), F32),
                   vec, wsd, vec, wsd, vec, vec),
        in_specs=[sp["up"], sp["gr"], sp["act"], sp["act"], sp["cw"], sp["vec"], sp["wblk"], sp["vec"],
                  sp["wblk"], sp["vec"], sp["vec"], ANY],
        out_specs=(ANY, sp["cw"], sp["vec"], sp["wblk"], sp["vec"], sp["wblk"], sp["vec"], sp["vec"]),
        input_output_aliases={11: 0},
        scratch_shapes=[pltpu.VMEM((s, RNN_BLOCK), F32)] * 2 + [pltpu.VMEM((2, s, RNN_BLOCK), BF16)] * 2 + [
            pltpu.SemaphoreType.DMA((4,))],
        compiler_params=_params())(proj3, proj3, h3, dy3, cw, cb, wa, ba, wx, bx, lam, dproj3)


def _rope_tables(s):
    half = ROPE_DIM // 2
    pos = jnp.arange(s, dtype=F32)
    inv_freq = ROPE_THETA ** (-jnp.arange(0, ROPE_DIM, 2, dtype=F32) / ROPE_DIM)
    ang = pos[:, None] * inv_freq[None, :]
    cos, sin = jnp.cos(ang), jnp.sin(ang)
    rest = HEAD_DIM - ROPE_DIM
    cos64 = jnp.concatenate([cos, cos, jnp.ones((s, rest), F32)], axis=1)
    sin64 = jnp.concatenate([-sin, sin, jnp.zeros((s, rest), F32)], axis=1)
    assert half * 2 == ROPE_DIM
    return jnp.tile(cos64, (1, LANES // HEAD_DIM)), jnp.tile(sin64, (1, LANES // HEAD_DIM))


def _swap_rot_halves(v):
    half = ROPE_DIM // 2
    lane = lax.broadcasted_iota(jnp.int32, v.shape, 1) % HEAD_DIM
    second = jnp.where(lane < ROPE_DIM, pltpu.roll(v, half, axis=1), 0.0)
    return jnp.where(lane < half, pltpu.roll(v, LANES - half, axis=1), second)


def _rope(v, cos, sin):
    tiles = []
    for t in range(v.shape[1] // LANES):
        vt = v[:, t * LANES:(t + 1) * LANES]
        tiles.append(vt * cos + _swap_rot_halves(vt) * sin)
    return tiles[0] if len(tiles) == 1 else jnp.concatenate(tiles, axis=1)


def _unrope(v, cos, sin):
    tiles = []
    for t in range(v.shape[1] // LANES):
        vt = v[:, t * LANES:(t + 1) * LANES]
        tiles.append(vt * cos + _swap_rot_halves(vt * sin))
    return tiles[0] if len(tiles) == 1 else jnp.concatenate(tiles, axis=1)


HEADS_PER_STEP = 8
QW = HEADS_PER_STEP * HEAD_DIM
N_PAIRS = N_Q_HEADS // HEADS_PER_STEP
Q_PER_KV = N_Q_HEADS // N_KV_HEADS
KV_PER_STEP = HEADS_PER_STEP // Q_PER_KV


QT_COLS = Q_PER_KV * ATT_BLK


def _attn_saved_shapes(bsz, s):
    nb = s // ATT_BLK
    pad = s + ATT_BLK
    return [(bsz, N_PAIRS, nb, LANES, QT_COLS), (bsz, N_PAIRS, KV_PER_STEP, pad, LANES),
            (bsz, N_PAIRS, KV_PER_STEP, pad, LANES), (bsz, N_PAIRS, LANES, pad)]


def _attn_specs(s, order):
    def mk(width, base, **kw):
        if order == "bp":
            return pl.BlockSpec((None, s, width), lambda b, p: (b, 0, base + p), **kw)
        return pl.BlockSpec((None, s, width), lambda p, b: (b, 0, base + p), **kw)

    def saved(shape, **kw):
        blk = (None, None) + shape[2:]
        zeros = (0,) * (len(shape) - 2)
        if order == "bp":
            return pl.BlockSpec(blk, lambda b, p: (b, p) + zeros, **kw)
        return pl.BlockSpec(blk, lambda p, b: (b, p) + zeros, **kw)

    one = dict(pipeline_mode=pl.Buffered(1))
    tbl = pl.BlockSpec((s, LANES), lambda *_: (0, 0))
    shapes = _attn_saved_shapes(1, s)
    return dict(q=mk(QW, OFF_Q // QW), k=mk(LANES, OFF_K // LANES), v=mk(LANES, OFF_V // LANES),
                g=mk(QW, OFF_ATTN_G // QW), act=mk(QW, 0), kv=mk(LANES, 0), tbl=tbl,
                g1=mk(QW, OFF_ATTN_G // QW, **one), act1=mk(QW, 0, **one),
                saved=[saved(sh) for sh in shapes], saved1=[saved(sh, **one) for sh in shapes],
                smem=pl.BlockSpec(memory_space=pltpu.SMEM))


def _to_qt(blk):
    rows = []
    for j in range(KV_PER_STEP):
        cols = []
        for tt in range(2):
            t = 2 * j + tt
            tr = blk[:, t * LANES:(t + 1) * LANES].T
            cols += [tr[0:HEAD_DIM, :], tr[HEAD_DIM:, :]]
        rows.append(jnp.concatenate(cols, axis=1))
    return jnp.concatenate(rows, axis=0)


def _from_qt(xt):
    tiles = []
    for j in range(KV_PER_STEP):
        for tt in range(2):
            g0 = 2 * tt
            pair = jnp.concatenate([xt[j * HEAD_DIM:(j + 1) * HEAD_DIM, (g0 + i) * ATT_BLK:(g0 + i + 1) * ATT_BLK]
                                    for i in range(2)], axis=0)
            tiles.append(pair.T)
    return jnp.concatenate(tiles, axis=1)


def _attn_prep(q_ref, k_ref, v_ref, cos_ref, sin_ref, qt_ref, km_ref, vm_ref, kt_ref, vt_ref, nb):
    zeros = jnp.zeros((ATT_BLK, LANES), BF16)
    for j in range(KV_PER_STEP):
        km_ref[j, 0:ATT_BLK, :] = zeros
        vm_ref[j, 0:ATT_BLK, :] = zeros
    kt_ref[:, 0:ATT_BLK] = zeros
    vt_ref[:, 0:ATT_BLK] = zeros
    head_of_lane = lax.broadcasted_iota(jnp.int32, (ATT_BLK, LANES), 1) // HEAD_DIM

    def prep(n, carry):
        r0 = pl.multiple_of(n * ATT_BLK, ATT_BLK)
        cs = cos_ref[pl.ds(r0, ATT_BLK), :]
        sn = sin_ref[pl.ds(r0, ATT_BLK), :]
        qt_ref[n] = _to_qt(_rope(q_ref[pl.ds(r0, ATT_BLK), :], cs, sn) * ATTN_SCALE).astype(BF16)
        k = _rope(k_ref[pl.ds(r0, ATT_BLK), :], cs, sn)
        v = v_ref[pl.ds(r0, ATT_BLK), :]
        for j in range(KV_PER_STEP):
            km_ref[j, pl.ds(r0 + ATT_BLK, ATT_BLK), :] = jnp.where(head_of_lane == j, k, 0.0).astype(BF16)
            vm_ref[j, pl.ds(r0 + ATT_BLK, ATT_BLK), :] = jnp.where(head_of_lane == j, v, 0.0).astype(BF16)
        kt_ref[:, pl.ds(r0 + ATT_BLK, ATT_BLK)] = k.T.astype(BF16)
        vt_ref[:, pl.ds(r0 + ATT_BLK, ATT_BLK)] = v.T.astype(BF16)
        return carry

    lax.fori_loop(0, nb, prep, 0, unroll=4)


assert WINDOW == ATT_BLK


def _from_prev_block():
    key = lax.broadcasted_iota(jnp.int32, (ATT_BLK, QT_COLS), 0)
    qry = lax.broadcasted_iota(jnp.int32, (ATT_BLK, QT_COLS), 1) % ATT_BLK
    return key > qry


def _fold(tile, prev, prev_bias=None):
    top = tile[:ATT_BLK] if prev_bias is None else tile[:ATT_BLK] + prev_bias
    return jnp.where(prev, top, tile[ATT_BLK:])


def _unfold(folded, prev):
    zero = jnp.zeros_like(folded)
    return jnp.concatenate([jnp.where(prev, folded, zero), jnp.where(prev, zero, folded)], axis=0).astype(BF16)


def _no_prev_bias(n):
    return jnp.where(n == 0, NEG_BIG, 0.0).astype(F32)


def _sink_row(sink_ref, first):
    return jnp.concatenate([jnp.full((1, ATT_BLK), sink_ref[first + g], F32) for g in range(Q_PER_KV)], axis=1)


def _softmax_cols(sc, sink):
    m = jnp.maximum(jnp.max(sc, axis=0, keepdims=True), sink)
    e = jnp.exp(sc - m)
    es = jnp.exp(sink - m)
    inv = 1.0 / (jnp.sum(e, axis=0, keepdims=True) + es)
    return e * inv, es * inv


def _attn_fwd(proj3, sinks, cosf, sinf):
    bsz, s, _ = proj3.shape
    nb = s // ATT_BLK
    sp = _attn_specs(s, "bp")

    def body(sink_ref, q_ref, k_ref, v_ref, g_ref, cos_ref, sin_ref, o_ref, y_ref, qt_sc, km_sc, vm_sc, kt_ref, vt_sc):
        p = pl.program_id(1)
        _attn_prep(q_ref, k_ref, v_ref, cos_ref, sin_ref, qt_sc, km_sc, vm_sc, kt_ref, vt_sc, nb)
        kv_row = lax.broadcasted_iota(jnp.int32, (LANES, QT_COLS), 0) // HEAD_DIM
        prev = _from_prev_block()

        def blk(n, carry):
            r0 = pl.multiple_of(n * ATT_BLK, ATT_BLK)
            bias = _no_prev_bias(n)
            rq = qt_sc[n]
            vt = vt_sc[:, pl.ds(r0, 2 * ATT_BLK)]
            ots = []
            for j in range(KV_PER_STEP):
                st = _dot(km_sc[j, pl.ds(r0, 2 * ATT_BLK), :], rq)
                pc, _ = _softmax_cols(_fold(st, prev, bias), _sink_row(sink_ref, p * HEADS_PER_STEP + j * Q_PER_KV))
                ots.append(_dot(vt, _unfold(pc, prev)))
            o = _from_qt(jnp.where(kv_row == 0, ots[0], ots[1]))
            o_ref[pl.ds(r0, ATT_BLK), :] = o
            g = g_ref[pl.ds(r0, ATT_BLK), :]
            y_ref[pl.ds(r0, ATT_BLK), :] = (o * (g * _sigmoid(g))).astype(BF16)
            return carry

        lax.fori_loop(0, nb, blk, 0, unroll=4)

    res = pl.pallas_call(
        body, name="attn_fwd", grid=(bsz, N_PAIRS),
        out_shape=[jax.ShapeDtypeStruct((bsz, s, D_ATTN), F32), jax.ShapeDtypeStruct((bsz, s, D_ATTN), BF16)] + [
            jax.ShapeDtypeStruct(sh, BF16) for sh in _attn_saved_shapes(bsz, s)],
        in_specs=[sp["smem"], sp["q"], sp["k"], sp["v"], sp["g"], sp["tbl"], sp["tbl"]],
        out_specs=[sp["act"], sp["act"]] + sp["saved"],
        scratch_shapes=[pltpu.VMEM((LANES, s + ATT_BLK), BF16)],
        compiler_params=_params())(sinks, proj3, proj3, proj3, proj3, cosf, sinf)
    return res[0], res[1], res[2:]


def _attn_bwd(proj3, saved, o3, dy3, dproj3, sinks, cosf, sinf):
    bsz, s, _ = proj3.shape
    nb = s // ATT_BLK
    assert nb % 2 == 0
    sp = _attn_specs(s, "pb")
    n_steps = N_PAIRS * bsz

    def body(sink_ref, qt_sc, km_sc, vm_sc, kt_sc, g_ref, o_ref, dy_ref, cos_ref, sin_ref, dp_in,
             dp_ref, ds_ref, dot_sc, dqt_sc, dk_sc, dv_sc, dq_st, dk_st, dv_st, dg_st, o_sems):
        del dp_in
        p = pl.program_id(0)
        b = pl.program_id(1)
        step = p * bsz + b
        slot = step % 2
        stages = [dq_st, dk_st, dv_st, dg_st]
        dsts = [dp_ref.at[b, :, pl.ds(pl.multiple_of(OFF_Q + p * QW, LANES), QW)],
                dp_ref.at[b, :, pl.ds(pl.multiple_of(OFF_K + p * LANES, LANES), LANES)],
                dp_ref.at[b, :, pl.ds(pl.multiple_of(OFF_V + p * LANES, LANES), LANES)],
                dp_ref.at[b, :, pl.ds(pl.multiple_of(OFF_ATTN_G + p * QW, LANES), QW)]]
        _staged_reuse(step, stages, dsts, o_sems, slot)
        dk_sc[...] = jnp.zeros_like(dk_sc)
        dv_sc[...] = jnp.zeros_like(dv_sc)

        def gate(n, carry):
            r0 = pl.multiple_of(n * ATT_BLK, ATT_BLK)
            g = g_ref[pl.ds(r0, ATT_BLK), :]
            dy = dy_ref[pl.ds(r0, ATT_BLK), :]
            sg = _sigmoid(g)
            dg_st[slot, pl.ds(r0, ATT_BLK), :] = (dy * o_ref[pl.ds(r0, ATT_BLK), :] * (sg * (1.0 + g * (1.0 - sg)))).astype(BF16)
            dot_sc[n] = _to_qt(dy * (g * sg)).astype(BF16)
            return carry

        lax.fori_loop(0, nb, gate, 0, unroll=2)
        kv_lane = lax.broadcasted_iota(jnp.int32, (2 * ATT_BLK, LANES), 1) // HEAD_DIM
        kv_row = lax.broadcasted_iota(jnp.int32, (LANES, QT_COLS), 0) // HEAD_DIM
        prev = _from_prev_block()

        def blk(n, acc):
            r0 = pl.multiple_of(n * ATT_BLK, ATT_BLK)
            bias = _no_prev_bias(n)
            rq = qt_sc[n]
            rd = dot_sc[n]
            kt = kt_sc[:, pl.ds(r0, 2 * ATT_BLK)]
            dvs, dks, dqs, new_acc = [], [], [], []
            for j in range(KV_PER_STEP):
                st = _dot(km_sc[j, pl.ds(r0, 2 * ATT_BLK), :], rq)
                pc, ps = _softmax_cols(_fold(st, prev, bias), _sink_row(sink_ref, p * HEADS_PER_STEP + j * Q_PER_KV))
                dpc = _fold(_dot(vm_sc[j, pl.ds(r0, 2 * ATT_BLK), :], rd), prev)
                delta = jnp.sum(pc * dpc, axis=0, keepdims=True)
                dst = _unfold(pc * (dpc - delta), prev)
                new_acc.append(acc[j] + ps * delta)
                dvs.append(_dot_nt(_unfold(pc, prev), rd))
                dks.append(_dot_nt(dst, rq))
                dqs.append(_dot(kt, dst))
            dv_sc[pl.ds(r0, 2 * ATT_BLK), :] += jnp.where(kv_lane == 0, dvs[0], dvs[1])
            dk_sc[pl.ds(r0, 2 * ATT_BLK), :] += jnp.where(kv_lane == 0, dks[0], dks[1])
            dqt_sc[n] = jnp.where(kv_row == 0, dqs[0], dqs[1]) * ATTN_SCALE
            return tuple(new_acc)

        per_trip = 4 if nb % 4 == 0 else 2

        def blk_group(m, acc):
            for u in range(per_trip):
                acc = blk(per_trip * m + u, acc)
            return acc

        acc = lax.fori_loop(0, nb // per_trip, blk_group, tuple(jnp.zeros((1, QT_COLS), F32) for _ in range(KV_PER_STEP)))
        lane1 = lax.broadcasted_iota(jnp.int32, (1, LANES), 1)
        dsink = jnp.zeros((1, LANES), F32)
        for j in range(KV_PER_STEP):
            for i in range(Q_PER_KV):
                part = jnp.sum(acc[j][:, i * ATT_BLK:(i + 1) * ATT_BLK], axis=1, keepdims=True)
                dsink = dsink - jnp.where(lane1 == j * Q_PER_KV + i, part, 0.0)

        @pl.when(b == 0)
        def _():
            ds_ref[...] = jnp.zeros_like(ds_ref)

        ds_ref[...] += dsink

        def post(n, carry):
            r0 = pl.multiple_of(n * ATT_BLK, ATT_BLK)
            cs = cos_ref[pl.ds(r0, ATT_BLK), :]
            sn = sin_ref[pl.ds(r0, ATT_BLK), :]
            dq_st[slot, pl.ds(r0, ATT_BLK), :] = _unrope(_from_qt(dqt_sc[n]), cs, sn).astype(BF16)
            dk_st[slot, pl.ds(r0, ATT_BLK), :] = _unrope(dk_sc[pl.ds(r0 + ATT_BLK, ATT_BLK), :], cs, sn).astype(BF16)
            dv_st[slot, pl.ds(r0, ATT_BLK), :] = dv_sc[pl.ds(r0 + ATT_BLK, ATT_BLK), :].astype(BF16)
            return carry

        lax.fori_loop(0, nb, post, 0, unroll=2)
        _staged_flush(step, n_steps, stages, dsts, o_sems, slot)

    n_in = 1 + len(saved) + 5
    return pl.pallas_call(
        body, name="attn_bwd", grid=(N_PAIRS, bsz),
        out_shape=(jax.ShapeDtypeStruct(dproj3.shape, dproj3.dtype), jax.ShapeDtypeStruct((N_PAIRS, 1, LANES), F32)),
        in_specs=[sp["smem"]] + sp["saved1"] + [sp["g1"], sp["act1"], sp["act1"], sp["tbl"], sp["tbl"], ANY],
        out_specs=(ANY, pl.BlockSpec((None, 1, LANES), lambda p, b: (p, 0, 0))),
        input_output_aliases={n_in: 0},
        scratch_shapes=[pltpu.VMEM((nb, LANES, QT_COLS), BF16),
                        pltpu.VMEM((nb, LANES, QT_COLS), F32),
                        pltpu.VMEM((s + ATT_BLK, LANES), F32),
                        pltpu.VMEM((s + ATT_BLK, LANES), F32),
                        pltpu.VMEM((2, s, QW), BF16), pltpu.VMEM((2, s, LANES), BF16),
                        pltpu.VMEM((2, s, LANES), BF16), pltpu.VMEM((2, s, QW), BF16),
                        pltpu.SemaphoreType.DMA((8,))],
        compiler_params=_params())(sinks, *saved, proj3, o3, dy3, cosf, sinf, dproj3)


def _staged_copies(stages, dsts, sems, slot):
    return [pltpu.make_async_copy(st.at[slot], dst, sems.at[slot * len(stages) + t])
            for t, (st, dst) in enumerate(zip(stages, dsts))]


def _staged_reuse(step, stages, dsts, sems, slot):
    @pl.when(step >= 2)
    def _():
        for cp in _staged_copies(stages, dsts, sems, slot):
            cp.wait()


def _staged_flush(step, n_steps, stages, dsts, sems, slot):
    for cp in _staged_copies(stages, dsts, sems, slot):
        cp.start()

    @pl.when(step == n_steps - 1)
    def _():
        for cp in _staged_copies(stages, dsts, sems, slot):
            cp.wait()
        if n_steps >= 2:
            for cp in _staged_copies(stages, dsts, sems, 1 - slot):
                cp.wait()


def _merge_fwd_bwd(x, tgt, y_rnn, y_attn, proj, w_r, w_a, w_o, gf):
    t, d = x.shape
    tm = min(t, 256)
    nt = t // tm

    hw = d // 2

    def body(x_ref, t_ref, yr_ref, ya_ref, mr0_ref, mr1_ref, ma0_ref, ma1_ref, wr_ref, wa_ref, wo_ref, gf_ref,
             dp_ref, dyr_ref, dya_ref, mg_ref, dx2_ref, dx2b_ref, dpr_ref, dpa_ref, loss_ref, dgf_ref, dmg_st, o_sems):
        i = pl.program_id(0)
        slot = i % 2
        dsts = [dp_ref.at[pl.ds(pl.multiple_of(i * tm, tm), tm), pl.ds(OFF_MERGE_R, 2 * d)]]
        _staged_reuse(i, [dmg_st], dsts, o_sems, slot)
        wr = wr_ref[...]
        wa = wa_ref[...]
        wo = wo_ref[...]
        gfv = gf_ref[...]
        pr = _dot(yr_ref[...], wr)
        pa = _dot(ya_ref[...], wa)
        sr = _sigmoid(jnp.concatenate([mr0_ref[...], mr1_ref[...]], axis=1))
        sa = _sigmoid(jnp.concatenate([ma0_ref[...], ma1_ref[...]], axis=1))
        mb = (sr * pr + sa * pa).astype(BF16)
        mg_ref[...] = mb
        x2 = x_ref[...] + _dot(mb, wo)
        r2 = lax.rsqrt(jnp.mean(x2 * x2, axis=-1, keepdims=True) + NORM_EPS)
        nrm = x2 * r2
        err = nrm * gfv - t_ref[...]
        dy = err * (1.0 / d)
        dn = dy * gfv
        dx2 = r2 * (dn - nrm * jnp.mean(dn * nrm, axis=-1, keepdims=True))
        dx2_ref[...] = dx2
        dx2b = dx2.astype(BF16)
        dx2b_ref[...] = dx2b
        dmerged = _dot_nt(dx2b, wo)
        dpr = (dmerged * sr).astype(BF16)
        dpa = (dmerged * sa).astype(BF16)
        dpr_ref[...] = dpr
        dpa_ref[...] = dpa
        dmg_st[slot, :, 0:d] = (dmerged * pr * (sr * (1.0 - sr))).astype(BF16)
        dmg_st[slot, :, d:2 * d] = (dmerged * pa * (sa * (1.0 - sa))).astype(BF16)
        _staged_flush(i, nt, [dmg_st], dsts, o_sems, slot)
        dyr_ref[...] = _dot_nt(dpr, wr)
        dya_ref[...] = _dot_nt(dpa, wa)

        @pl.when(i == 0)
        def _():
            loss_ref[...] = jnp.zeros_like(loss_ref)
            dgf_ref[...] = jnp.zeros_like(dgf_ref)

        loss_ref[...] += jnp.full((1, LANES), 0.5 / d, F32) * jnp.sum(err * err)
        dgf_ref[...] += jnp.sum(dy * nrm, axis=0, keepdims=True)

    tile = pl.BlockSpec((tm, d), lambda i: (i, 0))
    wsp = pl.BlockSpec((d, d), lambda i: (0, 0))

    def gate(col_blk):
        return pl.BlockSpec((tm, hw), lambda i: (i, col_blk))

    fb = jax.ShapeDtypeStruct((t, d), BF16)
    ff = jax.ShapeDtypeStruct((t, d), F32)
    return pl.pallas_call(
        body, name="merge_fwd_bwd", grid=(nt,),
        out_shape=(jax.ShapeDtypeStruct((t, D_IN), BF16), ff, ff, fb, ff, fb, fb, fb,
                   jax.ShapeDtypeStruct((1, LANES), F32), jax.ShapeDtypeStruct((1, d), F32)),
        in_specs=[tile, tile, tile, tile] + [gate(OFF_MERGE_R // hw + j) for j in range(4)] + [
            wsp, wsp, wsp, pl.BlockSpec((1, d), lambda i: (0, 0))],
        out_specs=(ANY, tile, tile, tile, tile, tile, tile, tile,
                   pl.BlockSpec((1, LANES), lambda i: (0, 0)), pl.BlockSpec((1, d), lambda i: (0, 0))),
        scratch_shapes=[pltpu.VMEM((2, tm, 2 * d), BF16), pltpu.SemaphoreType.DMA((2,))],
        compiler_params=_params())(x, tgt, y_rnn, y_attn, proj, proj, proj, proj, w_r, w_a, w_o, gf)


def _local_grads(x, tgt, h, proj, norm_g, w_in_bm, conv_w, conv_b, lru_w_a, lru_b_a, lru_w_x, lru_b_x, lam, sinks,
                 row_sharded, gf):
    bsz, s, d = x.shape
    t = bsz * s
    x2 = x.reshape(t, d)
    proj3 = proj.reshape(bsz, s, D_IN)
    h_lru, y_rnn, gathered = _lru_fwd(proj3, conv_w, conv_b, lru_w_a, lru_b_a, lru_w_x, lru_b_x, lam, row_sharded)
    w_r, w_a, w_o = (g.reshape(d, d) for g in gathered)
    cosf, sinf = _rope_tables(s)
    o_attn, y_attn, attn_saved = _attn_fwd(proj3, sinks, cosf, sinf)
    y_rnn2 = y_rnn.reshape(t, d)
    y_attn2 = y_attn.reshape(t, d)
    dproj, dyr, dya, merged, dx2, dx2b, dpr, dpa, loss, dgf = _merge_fwd_bwd(
        x2, tgt.reshape(t, d), y_rnn2, y_attn2, proj, w_r, w_a, w_o, gf)
    dproj3, dsink = _attn_bwd(proj3, attn_saved, o_attn, dya.reshape(bsz, s, d), dproj.reshape(bsz, s, D_IN),
                              sinks, cosf, sinf)
    dproj3, dcw, dcb, dwa, dba, dwx, dbx, dlam = _lru_bwd(
        proj3, h_lru, dyr.reshape(bsz, s, d), dproj3, conv_w, conv_b, lru_w_a, lru_b_a, lru_w_x, lru_b_x, lam)
    dproj = dproj3.reshape(t, D_IN)
    grad_x, dng = _grad_x(dproj, w_in_bm, x2, dx2, norm_g)
    small = dict(norm_g=dng, conv_w=dcw, conv_b=dcb, lru_w_a=dwa, lru_b_a=dba, lru_w_x=dwx, lru_b_x=dbx,
                 lru_lambda=dlam, attn_sinks=dsink[:, 0, :HEADS_PER_STEP].reshape(1, N_Q_HEADS), final_norm_g=dgf)
    squares = [(y_rnn2, dpr), (y_attn2, dpa), (merged, dx2b)]
    return loss[0, 0], grad_x.reshape(bsz, s, d), h, dproj, squares, small


ANY = pl.BlockSpec(memory_space=pl.ANY)


def _mesh_pos():
    return lax.axis_index("x"), lax.axis_index("y"), lax.axis_index("c")


def _remote(src, dst, send_sems, recv_sems, idx, peer):
    return pltpu.make_async_remote_copy(src_ref=src, dst_ref=dst, send_sem=send_sems.at[idx],
                                        recv_sem=recv_sems.at[idx], device_id=peer, device_id_type=MESH)


def _row_gather(ins, outs, send_sems, recv_sems, fsend_sems, frecv_sems):
    n = len(ins)
    x, y, c = _mesh_pos()
    me = 2 * x + y
    sib = (x, y, 1 - c)
    peers = [((x, 1 - y, c), me ^ 1), ((1 - x, y, c), me ^ 2), ((1 - x, 1 - y, c), me ^ 3)]

    def half(ref, slot, t, which):
        hr = ins[t].shape[1] // 2
        return ref.at[slot, pl.ds(pl.multiple_of(which * hr, 8), hr), :]

    def ici(t, k):
        peer, pj = peers[k]
        src = half(ins[t], me, t, c)
        return (_remote(src, half(outs[t], me, t, c), send_sems, recv_sems, 3 * t + k, peer),
                _remote(src, half(outs[t], pj, t, c), send_sems, recv_sems, 3 * t + k, peer))

    def forward(t, k):
        got = half(outs[t], peers[k][1], t, c)
        return (_remote(got, got, fsend_sems, frecv_sems, 3 * t + k, sib),
                _remote(got, half(outs[t], peers[k][1], t, 1 - c), fsend_sems, frecv_sems, 3 * t + k, sib))

    pairs = [(t, k) for t in range(n) for k in range(3)]

    def start():
        for t, k in pairs:
            ici(t, k)[0].start()

    def pass_on():
        for t, k in pairs:
            ici(t, k)[1].wait_recv()
            forward(t, k)[0].start()

    def finish():
        for t, k in pairs:
            ici(t, k)[0].wait_send()
            forward(t, k)[0].wait_send()
            forward(t, k)[1].wait_recv()

    return start, pass_on, finish


def _gather_in_proj(x, g, bufs, split, idx):
    t_tok, d = x.shape
    n = len(bufs)
    tm = min(t_tok, 1024)
    nt = t_tok // tm
    n_fwd = 3 * sum(split)
    assert split[0]

    def body(idx_ref, x_ref, g_ref, *refs):
        ins, proj_ref, h_out, outs = refs[:n], refs[n], refs[n + 1], refs[n + 2:2 * n + 2]
        wbuf, h_all, send_sems, recv_sems, fsend_sems, frecv_sems, l_sems = refs[2 * n + 2:]
        j, i = pl.program_id(0), pl.program_id(1)
        rows = pl.ds(pl.multiple_of(i * tm, tm), tm)
        x, y, c = _mesh_pos()
        me = 2 * x + y
        sib = (x, y, 1 - c)
        peers = [((x, 1 - y, c), me ^ 1), ((1 - x, y, c), me ^ 2), ((1 - x, 1 - y, c), me ^ 3)]

        def part(ref, slot, t, half):
            if not split[t]:
                return ref.at[slot]
            hr = bufs[t].shape[1] // 2
            return ref.at[slot, pl.ds(pl.multiple_of(half * hr, 8), hr), :]

        def land(t):
            return wbuf if t == 0 else outs[t]

        def ici(t, k):
            peer, pj = peers[k]
            src = part(ins[t], me, t, c)
            return (_remote(src, part(land(t), me, t, c), send_sems, recv_sems, 3 * t + k, peer),
                    _remote(src, part(land(t), pj, t, c), send_sems, recv_sems, 3 * t + k, peer))

        fwd_index = {}
        for t in range(n):
            if split[t]:
                for k in range(3):
                    fwd_index[(t, k)] = len(fwd_index)

        def forward(t, k):
            pj = peers[k][1]
            got = part(land(t), pj, t, c)
            f = fwd_index[(t, k)]
            return (_remote(got, got, fsend_sems, frecv_sems, f, sib),
                    _remote(got, part(land(t), pj, t, 1 - c), fsend_sems, frecv_sems, f, sib))

        def write_back(k):
            pj = peers[k][1]
            return pltpu.make_async_copy(wbuf.at[pj], outs[0].at[pj], l_sems.at[1 + k])

        relay_peer = ((x + c) % 2, (y + 1 - c) % 2, c)

        def relay():
            got = part(wbuf, me ^ (2 - c), 0, c)
            return (_remote(got, got, send_sems, recv_sems, 2, relay_peer),
                    _remote(got, part(wbuf, me ^ 3, 0, c), send_sems, recv_sems, 2, relay_peer))

        direct = [(t, k) for t in range(n) for k in range(3) if (t, k) != (0, 2)]

        @pl.when((j == 0) & (i == 0))
        def _():
            for t, k in direct:
                ici(t, k)[0].start()
            own = pltpu.make_async_copy(ins[0].at[me], wbuf.at[me], l_sems.at[0])
            own.start()
            own.wait()

        @pl.when((j == 1) & (i == 0))
        def _():
            pltpu.make_async_copy(h_all, h_out, l_sems.at[4]).start()
            for k in range(2):
                ici(0, k)[1].wait_recv()
            relay()[0].start()
            for k in range(2):
                forward(0, k)[0].start()
            forward(0, 0)[1].wait_recv()
            write_back(0).start()

        @pl.when((j == 2) & (i == 0))
        def _():
            forward(0, 1)[1].wait_recv()
            write_back(1).start()

        @pl.when((j == 3) & (i == 0))
        def _():
            relay()[1].wait_recv()
            forward(0, 2)[0].start()
            forward(0, 2)[1].wait_recv()
            write_back(2).start()

        @pl.when(j == 0)
        def _():
            xv = x_ref[...]
            r = lax.rsqrt(jnp.mean(xv * xv, axis=-1, keepdims=True) + NORM_EPS)
            h_all[rows, :] = (xv * r * g_ref[...]).astype(BF16)

        proj_ref[...] = _dot(h_all[rows, :], wbuf[me ^ j])

        @pl.when((j == N_CHIPS - 1) & (i == nt - 1))
        def _():
            pltpu.make_async_copy(h_all, h_out, l_sems.at[4]).wait()
            for t in range(1, n):
                for k in range(3):
                    ici(t, k)[1].wait_recv()
                    if split[t]:
                        forward(t, k)[0].start()
            relay()[0].wait_send()
            for t, k in direct:
                ici(t, k)[0].wait_send()
            for t in range(n):
                if split[t]:
                    for k in range(3):
                        forward(t, k)[0].wait_send()
                        if t > 0:
                            forward(t, k)[1].wait_recv()
            for k in range(3):
                write_back(k).wait()

    grid_spec = pltpu.PrefetchScalarGridSpec(
        num_scalar_prefetch=1, grid=(N_CHIPS, nt),
        in_specs=[pl.BlockSpec((tm, d), lambda j, i, idx_ref: (jnp.where(j == 0, i, nt - 1), 0)),
                  pl.BlockSpec((1, d), lambda j, i, idx_ref: (0, 0))] + [ANY] * n,
        out_specs=[pl.BlockSpec((tm, W_BLK), lambda j, i, idx_ref: (i, idx_ref[0] ^ j)), ANY] + [ANY] * n,
        scratch_shapes=[pltpu.VMEM(bufs[0].shape, bufs[0].dtype), pltpu.VMEM((t_tok, d), BF16),
                        pltpu.SemaphoreType.DMA((3 * n,)), pltpu.SemaphoreType.DMA((3 * n,)),
                        pltpu.SemaphoreType.DMA((n_fwd,)), pltpu.SemaphoreType.DMA((n_fwd,)),
                        pltpu.SemaphoreType.DMA((5,))])
    out_shape = [jax.ShapeDtypeStruct((t_tok, D_IN), F32), jax.ShapeDtypeStruct((t_tok, d), BF16)] + [
        jax.ShapeDtypeStruct(a.shape, a.dtype) for a in bufs]
    res = pl.pallas_call(
        body, name="gather_in_proj", grid_spec=grid_spec, out_shape=out_shape,
        input_output_aliases={3 + t: 2 + t for t in range(n)}, compiler_params=_params())(idx, x, g, *bufs)
    return res[1], res[0], res[2:]


def _row_tile(rows, row_bytes, cap_bytes=2 * 1024 * 1024):
    best = None
    for tr in range(8, rows + 1, 8):
        if rows % tr == 0 and tr * row_bytes <= cap_bytes:
            best = tr
    return best if best is not None else rows


XOR_ORDER = (2, 3, 1)


def _grads_reduce_scatter(h, dproj, squares, small, idx):
    t, d = h.shape
    nsq = len(squares)
    hr = d // 2
    qr = ROW_BLK // 2
    tk = min(t, 1024)
    nk = t // tk
    last = N_CHIPS - 1
    n_phase = 3

    def dest(s, idx_ref):
        xo = jnp.where(s == 0, XOR_ORDER[0], jnp.where(s == 1, XOR_ORDER[1], jnp.where(s == 2, XOR_ORDER[2], 0)))
        return idx_ref[0] ^ xo

    def k_sq(p, k):
        return jnp.where(p == 0, k, nk - 1)

    def k_w(p, k):
        return jnp.where(p == 0, 0, k)

    in_specs = [
        pl.BlockSpec((tk, hr), lambda s, p, k, idx_ref: (k_w(p, k), (1 - idx_ref[1] + jnp.maximum(p - 1, 0)) % 2)),
        pl.BlockSpec((tk, W_BLK), lambda s, p, k, idx_ref: (k_w(p, k), dest(s, idx_ref)))]
    for q in range(nsq):
        in_specs.append(pl.BlockSpec((tk, ROW_BLK), lambda s, p, k, idx_ref: (k_sq(p, k), dest(s, idx_ref))))
        in_specs.append(pl.BlockSpec((tk, d), lambda s, p, k, idx_ref: (k_sq(p, k), 0)))

    def body(idx_ref, *refs):
        nj = 1 + nsq
        h_ref, dp_ref = refs[0], refs[1]
        sq_in = refs[2:2 + 2 * nsq]
        small_in = refs[2 * nj]
        outs = refs[2 * nj + 1:3 * nj + 2]
        landing = refs[3 * nj + 2:4 * nj + 3]
        sc = refs[4 * nj + 3:]
        acc_w, xr_w, sb_w = sc[0:3]
        sq_sc = [sc[3 + 3 * q:6 + 3 * q] for q in range(nsq)]
        sm, smx = sc[3 * nj:3 * nj + 2]
        x_send, x_recv, i_send, i_recv, f_send, f_recv, o_sem, l_sem = sc[3 * nj + 2:]
        s, p, k = pl.program_id(0), pl.program_id(1), pl.program_id(2)
        x, y, c = _mesh_pos()
        sib = (x, y, 1 - c)
        peers = [((1 - x) if xo & 2 else x, (1 - y) if xo & 1 else y, c) for xo in XOR_ORDER]
        slot = s % 2
        mine_w = pl.ds(pl.multiple_of(c * hr, 8), hr)
        theirs_w = pl.ds(pl.multiple_of((1 - c) * hr, 8), hr)
        mine_q = pl.ds(pl.multiple_of(c * qr, 8), qr)
        theirs_q = pl.ds(pl.multiple_of((1 - c) * qr, 8), qr)

        def exch(j, src, dst):
            return _remote(src, dst, x_send, x_recv, 2 * j + slot, sib)

        sbufs = [sb_w] + [sq_sc[q][2] for q in range(nsq)]

        def ici(j, ss):
            return _remote(sbufs[j].at[ss], landing[j].at[ss], i_send, i_recv, last * j + ss, peers[ss])

        def exchanges():
            cps = [exch(0, acc_w.at[0], xr_w.at[slot])]
            cps += [exch(1 + q, sq_sc[q][0].at[theirs_q, :], sq_sc[q][1].at[slot]) for q in range(nsq)]
            return cps

        def small_send(ss):
            return _remote(sm.at[c], landing[nj].at[ss], i_send, i_recv, last * nj + ss, peers[ss])

        def small_start():
            load = pltpu.make_async_copy(small_in, sm, l_sem.at[nj + 1])
            load.start()
            load.wait()
            swap = _remote(sm, smx.at[pl.ds(0, 2)], x_send, x_recv, 2 * nj, sib)
            swap.start()
            swap.wait_recv()
            swap.wait_send()
            sm[...] = sm[...] + smx[0:2]
            for ss in range(last):
                small_send(ss).start()

        def pair_ref(j):
            return acc_w.at[1] if j == 0 else sq_sc[j - 1][0].at[mine_q, :]

        def sq_phase():
            pl.when((s == 0) & (k == 0))(small_start)
            for q in range(nsq):
                acc = sq_sc[q][0]

                @pl.when(k == 0)
                def _():
                    acc[...] = jnp.zeros((ROW_BLK, d), F32)

                acc[...] += _dot_tn(sq_in[2 * q][...], sq_in[2 * q + 1][...])

            @pl.when(k == nk - 1)
            def _():
                for cp in exchanges()[1:]:
                    cp.start()

        def w_phase(hf):
            @pl.when(k == 0)
            def _():
                acc_w[hf] = jnp.zeros((hr, W_BLK), F32)

            acc_w[hf] += _dot_tn(h_ref[...], dp_ref[...])

            @pl.when(k == nk - 1)
            def _():
                if hf == 0:
                    exchanges()[0].start()
                else:
                    finish_step()

        def finish_step():
            for cp in exchanges():
                cp.wait_recv()
                cp.wait_send()
            acc_w[1] += xr_w[slot]
            for q in range(nsq):
                sq_sc[q][0][mine_q, :] += sq_sc[q][1][slot]
            for ss in range(last):
                @pl.when(s == ss)
                def _():
                    for j in range(nj):
                        sbufs[j][ss] = pair_ref(j)[...].astype(BF16)
                        ici(j, ss).start()

            @pl.when(s == last)
            def _():
                for ss in range(last):
                    for j in range(nj):
                        ici(j, ss).wait_recv()
                        ici(j, ss).wait_send()
                    small_send(ss).wait_recv()
                    small_send(ss).wait_send()
                stage = [pltpu.make_async_copy(landing[j], sbufs[j], l_sem.at[j]) for j in range(nj)]
                stage.append(pltpu.make_async_copy(landing[nj], smx, l_sem.at[nj]))
                for cp in stage:
                    cp.start()
                for j in range(nj):
                    stage[j].wait()
                    total = pair_ref(j)[...]
                    for ss in range(last):
                        total = total + sbufs[j][ss].astype(F32)
                    pair_ref(j)[...] = total
                stage[nj].wait()
                by_xor = {xo: smx[ss] for ss, xo in enumerate(XOR_ORDER)}
                sm[c] = (sm[c] + by_xor[1]) + (by_xor[2] + by_xor[3])
                done = [(acc_w.at[1], outs[0].at[mine_w, :], outs[0].at[theirs_w, :])]
                done += [(pair_ref(1 + q), outs[1 + q].at[mine_q, :], outs[1 + q].at[theirs_q, :]) for q in range(nsq)]
                done.append((sm.at[c], outs[nj].at[c], outs[nj].at[1 - c]))
                copies = []
                for j, (src, mine, theirs) in enumerate(done):
                    keep = pltpu.make_async_copy(src, mine, o_sem.at[j])
                    give = _remote(src, mine, f_send, f_recv, j, sib)
                    take = _remote(src, theirs, f_send, f_recv, j, sib)
                    keep.start()
                    give.start()
                    copies.append((keep, give, take))
                for keep, give, take in copies:
                    keep.wait()
                    give.wait_send()
                    take.wait_recv()

        pl.when(p == 0)(sq_phase)
        for hf in range(2):
            pl.when(p == 1 + hf)(functools.partial(w_phase, hf))

    nj = 1 + nsq
    scratch = [pltpu.VMEM((2, hr, W_BLK), F32), pltpu.VMEM((2, hr, W_BLK), F32), pltpu.VMEM((last, hr, W_BLK), BF16)]
    for _ in range(nsq):
        scratch += [pltpu.VMEM((ROW_BLK, d), F32), pltpu.VMEM((2, qr, d), F32), pltpu.VMEM((last, qr, d), BF16)]
    scratch += [pltpu.VMEM((2, PK_HALF, LANES), F32), pltpu.VMEM((last, PK_HALF, LANES), F32)]
    scratch += [pltpu.SemaphoreType.DMA((2 * nj + 1,)), pltpu.SemaphoreType.DMA((2 * nj + 1,)),
                pltpu.SemaphoreType.DMA((last * (nj + 1),)), pltpu.SemaphoreType.DMA((last * (nj + 1),)),
                pltpu.SemaphoreType.DMA((nj + 1,)), pltpu.SemaphoreType.DMA((nj + 1,)),
                pltpu.SemaphoreType.DMA((nj + 1,)), pltpu.SemaphoreType.DMA((nj + 2,))]
    grid_spec = pltpu.PrefetchScalarGridSpec(
        num_scalar_prefetch=1, grid=(N_CHIPS, n_phase, nk), in_specs=in_specs + [ANY],
        out_specs=[ANY] * (2 * nj + 2), scratch_shapes=scratch)
    out_shape = [jax.ShapeDtypeStruct((d, W_BLK), F32)] + [jax.ShapeDtypeStruct((ROW_BLK, d), F32)] * nsq
    out_shape.append(jax.ShapeDtypeStruct((2, PK_HALF, LANES), F32))
    out_shape += [jax.ShapeDtypeStruct((last, hr, W_BLK), BF16)] + [jax.ShapeDtypeStruct((last, qr, d), BF16)] * nsq
    out_shape.append(jax.ShapeDtypeStruct((last, PK_HALF, LANES), F32))
    flat = [a for pair in squares for a in pair]
    res = pl.pallas_call(body, name="grads_reduce_scatter", grid_spec=grid_spec, out_shape=out_shape,
                         compiler_params=_params())(idx, h, dproj, *flat, small)
    return res[:nj + 1]


_VEC_NAMES = ("norm_g", "conv_b", "lru_b_a", "lru_b_x", "lru_lambda", "final_norm_g")


def _pack_small(p, conv_full=None, scalar=None):
    rows = [p["lru_w_a"].reshape(PK_WX - PK_WA, LANES), p["lru_w_x"].reshape(PK_VEC - PK_WX, LANES)]
    rows += [p[k].reshape(8, LANES) for k in _VEC_NAMES]
    rows.append(jnp.pad(p["attn_sinks"].reshape(1, N_Q_HEADS), ((0, 7), (0, LANES - N_Q_HEADS))))
    rows.append(jnp.zeros((32, LANES), F32) if conv_full is None else conv_full.reshape(32, LANES))
    tail = PK_ROWS - PK_SCALAR
    if scalar is None:
        rows.append(jnp.zeros((tail, LANES), F32))
    else:
        rows.append(jnp.pad(scalar.reshape(1, 1), ((0, tail - 1), (0, LANES - 1))))
    return jnp.concatenate(rows, axis=0)


def _unpack_small(pk, like):
    out = {"lru_w_a": pk[PK_WA:PK_WX].reshape(like["lru_w_a"].shape),
           "lru_w_x": pk[PK_WX:PK_VEC].reshape(like["lru_w_x"].shape)}
    for j, k in enumerate(_VEC_NAMES):
        out[k] = pk[PK_VEC + 8 * j:PK_VEC + 8 * j + 8].reshape(like[k].shape)
    out["attn_sinks"] = pk[PK_SINK:PK_SINK + 1, :N_Q_HEADS].reshape(like["attn_sinks"].shape)
    return out


_WEIGHTS = ("norm_g", "w_in", "conv_w", "conv_b", "lru_w_a", "lru_b_a", "lru_w_x", "lru_b_x", "lru_lambda",
            "attn_sinks", "w_rnn_out", "w_attn_out", "w_o", "final_norm_g")
_ROW_SHARDED = ("w_rnn_out", "w_attn_out", "w_o")


def kernel(x, norm_g, w_in, conv_w, conv_b, lru_w_a, lru_b_a, lru_w_x, lru_b_x, lru_lambda, attn_sinks, w_rnn_out, w_attn_out, w_o, final_norm_g, loss_target, m_norm_g, m_w_in, m_conv_w, m_conv_b, m_lru_w_a, m_lru_b_a, m_lru_w_x, m_lru_b_x, m_lru_lambda, m_attn_sinks, m_w_rnn_out, m_w_attn_out, m_w_o, m_final_norm_g, v_norm_g, v_w_in, v_conv_w, v_conv_b, v_lru_w_a, v_lru_b_a, v_lru_w_x, v_lru_b_x, v_lru_lambda, v_attn_sinks, v_w_rnn_out, v_w_attn_out, v_w_o, v_final_norm_g):
    w = dict(norm_g=norm_g, w_in=w_in, conv_w=conv_w, conv_b=conv_b, lru_w_a=lru_w_a, lru_b_a=lru_b_a, lru_w_x=lru_w_x,
             lru_b_x=lru_b_x, lru_lambda=lru_lambda, attn_sinks=attn_sinks, w_rnn_out=w_rnn_out, w_attn_out=w_attn_out,
             w_o=w_o, final_norm_g=final_norm_g)
    m = dict(norm_g=m_norm_g, w_in=m_w_in, conv_w=m_conv_w, conv_b=m_conv_b, lru_w_a=m_lru_w_a, lru_b_a=m_lru_b_a,
             lru_w_x=m_lru_w_x, lru_b_x=m_lru_b_x, lru_lambda=m_lru_lambda, attn_sinks=m_attn_sinks,
             w_rnn_out=m_w_rnn_out, w_attn_out=m_w_attn_out, w_o=m_w_o, final_norm_g=m_final_norm_g)
    v = dict(norm_g=v_norm_g, w_in=v_w_in, conv_w=v_conv_w, conv_b=v_conv_b, lru_w_a=v_lru_w_a, lru_b_a=v_lru_b_a,
             lru_w_x=v_lru_w_x, lru_b_x=v_lru_b_x, lru_lambda=v_lru_lambda, attn_sinks=v_attn_sinks,
             w_rnn_out=v_w_rnn_out, w_attn_out=v_w_attn_out, w_o=v_w_o, final_norm_g=v_final_norm_g)
    mx, my, mc = _mesh_pos()
    me = 2 * mx + my
    d = D_MODEL

    my_chip = jnp.reshape(me, (1,)).astype(jnp.int32)
    (buf_in,) = _put_slots([w["w_in"][0]], my_chip, BF16, "cast_w_in")
    (buf_cw,) = _put_slots([w["conv_w"][0]], my_chip, F32, "slot_conv_w")
    row_sharded = _put_slots([w[k][0] for k in _ROW_SHARDED], my_chip, BF16, "cast_row_sharded")
    h, proj, (g_in, g_cw) = _gather_in_proj(x.reshape(-1, d), w["norm_g"], [buf_in, buf_cw], [True, False], my_chip)
    conv_full = g_cw.transpose(1, 0, 2).reshape(CONV_WIDTH, D_RNN)

    loss_local, grad_x, h, dproj, squares, gsmall = _local_grads(
        x, loss_target, h, proj, w["norm_g"], g_in, conv_full, w["conv_b"], w["lru_w_a"][0], w["lru_b_a"], w["lru_w_x"][0],
        w["lru_b_x"], w["lru_lambda"], w["attn_sinks"][0], row_sharded, w["final_norm_g"].reshape(1, d))
    gpack = _pack_small(gsmall, gsmall["conv_w"], loss_local).reshape(2, PK_HALF, LANES)
    f_in, f_r, f_a, f_o, spack = _grads_reduce_scatter(h, dproj, squares, gpack, jnp.stack([me, mc]).astype(jnp.int32))
    spack = spack.reshape(PK_ROWS, LANES)
    loss = spack[PK_SCALAR, 0]

    grads = _unpack_small(spack, w)
    conv_all = spack[PK_CONV:PK_CONV + 32].reshape(CONV_WIDTH, D_RNN)
    grads["conv_w"] = lax.dynamic_slice_in_dim(conv_all, me * (D_RNN // N_CHIPS), D_RNN // N_CHIPS, axis=1)[None]
    grads["w_in"] = f_in[None]
    grads["w_rnn_out"], grads["w_attn_out"], grads["w_o"] = f_r[None], f_a[None], f_o[None]

    delta, new_m, new_v = {}, {}, {}
    def group(k):
        return w[k][0], grads[k][0], m[k][0], v[k][0]

    for names, call in ((("w_in",), "adamw_w_in"), (_ROW_SHARDED, "adamw_row_sharded")):
        for k, (dk, mk, vk, gk) in zip(names, _adamw([group(k) for k in names], call, echo_grad=True)):
            delta[k], new_m[k], new_v[k], grads[k] = dk[None], mk[None], vk[None], gk[None]
    shp = (2 * CONV_WIDTH, LANES)
    ((dk, mk, vk),) = _adamw([tuple(a.reshape(shp) for a in (w["conv_w"], grads["conv_w"], m["conv_w"], v["conv_w"]))],
                             "adamw_conv_w")
    delta["conv_w"], new_m["conv_w"], new_v["conv_w"] = (a.reshape(w["conv_w"].shape) for a in (dk, mk, vk))
    ((dk, mk, vk),) = _adamw([(_pack_small(w), spack, _pack_small(m), _pack_small(v))], "adamw_small")
    for src, dst in ((dk, delta), (mk, new_m), (vk, new_v)):
        dst.update(_unpack_small(src, w))

    return (loss, grad_x, *[grads[k] for k in _WEIGHTS], *[delta[k] for k in _WEIGHTS],
            *[new_m[k] for k in _WEIGHTS], *[new_v[k] for k in _WEIGHTS])
```
